```python
import jax, jax.numpy as jnp
from jax import lax
import numpy as np

D_MODEL = 1024
BATCH = 4
SEQ = 8192
DEPTH = 4

PLE_DIM = 256
N_A_LAYERS = DEPTH // 2
N_B_LAYERS = DEPTH - N_A_LAYERS
POOL_WINDOWS = (2, 4, 8, 16)
POOL_GROUP_DIM = D_MODEL // len(POOL_WINDOWS)
N_HEADS = 16
HEAD_DIM = 64
N_KV_GROUPS = 4
HEADS_PER_GROUP = N_HEADS // N_KV_GROUPS
KV_WIDTH = N_KV_GROUPS * HEAD_DIM
N_BRANCH = 3
ROPE_DIMS = HEAD_DIM // 4
ROPE_THETA = 500000.0
CMP_BLOCK = 32
CMP_STRIDE = 16
CMP_HIDDEN = 256
SEL_BLOCK = 64
N_SELECT = 16
WINDOW = 512
Q_BLOCK = 64
FORCE_BONUS = 1e4
N_EXPERT_GROUPS = 4
EXPERTS_PER_GROUP = 8
N_EXPERTS = N_EXPERT_GROUPS * EXPERTS_PER_GROUP
TOP_K_IN_GROUP = 2
D_FF_EXPERT = 512
MOE_CHUNK = 512
RMS_EPS = 1e-6

kernel_name = 'yoco_pool_nsa_hmoe_trunk'


def _rmsnorm(x, g):
    xf = x.astype(jnp.float32)
    y = xf * lax.rsqrt(jnp.mean(xf * xf, axis=-1, keepdims=True) + RMS_EPS)
    return (y * g.astype(jnp.float32)).astype(x.dtype)


def _masked_softmax(s, mask):
    s = jnp.where(mask, s.astype(jnp.float32), -jnp.inf)
    m = jnp.max(s, axis=-1, keepdims=True)
    m = jnp.where(jnp.isfinite(m), m, 0.0)
    e = jnp.exp(s - m)
    return e / jnp.maximum(jnp.sum(e, axis=-1, keepdims=True), 1e-30)


def _rope(x, pos):
    half = ROPE_DIMS // 2
    inv = jnp.float32(ROPE_THETA) ** (-jnp.arange(half, dtype=jnp.float32) * 2.0 / ROPE_DIMS)
    ang = pos.astype(jnp.float32)[:, None] * inv[None, :]
    cos, sin = jnp.cos(ang), jnp.sin(ang)
    xf = x.astype(jnp.float32)
    x1 = xf[..., :half]
    x2 = xf[..., half:ROPE_DIMS]
    out = jnp.concatenate([x1 * cos - x2 * sin, x2 * cos + x1 * sin, xf[..., ROPE_DIMS:]], axis=-1)
    return out.astype(x.dtype)


def _pool_mixer(xn, w_grp, b, scale):
    S = xn.shape[1]
    t = jnp.arange(S)
    outs = []
    for gi, w in enumerate(POOL_WINDOWS):
        xg = xn[..., gi * POOL_GROUP_DIM:(gi + 1) * POOL_GROUP_DIM].astype(jnp.float32)
        cs = jnp.cumsum(xg, axis=1)
        cs_prev = jnp.pad(cs, ((0, 0), (w, 0), (0, 0)))[:, :S]
        cnt = jnp.minimum(t + 1, w).astype(jnp.float32)[None, :, None]
        pooled = (cs - cs_prev) / cnt - xg
        outs.append(pooled.astype(xn.dtype) @ w_grp[gi])
    y = jnp.concatenate(outs, axis=-1)
    return (y + b) * scale


def _compress(k, pos_emb, w1, b1, w2, b2):
    S = k.shape[2]
    n_cmp = (S - CMP_BLOCK) // CMP_STRIDE + 1
    idx = jnp.arange(n_cmp)[:, None] * CMP_STRIDE + jnp.arange(CMP_BLOCK)[None, :]
    blk = k[:, :, idx] + pos_emb
    hid = jax.nn.gelu(jnp.einsum('bgnld,ldh->bgnh', blk, w1) + b1)
    return hid @ w2 + b2


def _shared_kv(h, kv_norm, w_kv, cmp_k_pos, cmp_k_w1, cmp_k_b1, cmp_k_w2, cmp_k_b2,
               cmp_v_pos, cmp_v_w1, cmp_v_b1, cmp_v_w2, cmp_v_b2):
    B, S, _ = h.shape
    hn = _rmsnorm(h, kv_norm)
    kv = (hn @ w_kv).reshape(B, S, 2 * N_BRANCH, N_KV_GROUPS, HEAD_DIM).transpose(2, 0, 3, 1, 4)
    k_c, v_c, k_s, v_s, k_w, v_w = kv[0], kv[1], kv[2], kv[3], kv[4], kv[5]
    pos = jnp.arange(S)
    kc = _compress(k_c, cmp_k_pos, cmp_k_w1, cmp_k_b1, cmp_k_w2, cmp_k_b2)
    n_cmp = kc.shape[2]
    kc = _rope(kc, jnp.arange(n_cmp) * CMP_STRIDE + CMP_BLOCK - 1)
    vc = _compress(v_c, cmp_v_pos, cmp_v_w1, cmp_v_b1, cmp_v_w2, cmp_v_b2)
    n_blk = S // SEL_BLOCK
    ks_blk = _rope(k_s, pos).reshape(B, N_KV_GROUPS, n_blk, SEL_BLOCK, HEAD_DIM)
    vs_blk = v_s.reshape(B, N_KV_GROUPS, n_blk, SEL_BLOCK, HEAD_DIM)
    pad = ((0, 0), (0, 0), (WINDOW, 0), (0, 0))
    kw_pad = jnp.pad(_rope(k_w, pos), pad)
    vw_pad = jnp.pad(v_w, pad)
    return kc, vc, ks_blk, vs_blk, kw_pad, vw_pad


def _gather_blocks(blocks, idx):
    return jax.vmap(jax.vmap(lambda kb, ix: kb[ix]))(blocks, idx)


def _nsa_mixer(xn, w_qg, b_gate, w_o, kc, vc, ks_blk, vs_blk, kw_pad, vw_pad):
    B, S, _ = xn.shape
    G, R, dh = N_KV_GROUPS, HEADS_PER_GROUP, HEAD_DIM
    HD = N_HEADS * HEAD_DIM
    qg = xn @ w_qg
    q = qg[..., :HD].reshape(B, S, N_HEADS, dh).transpose(0, 2, 1, 3)
    q = (_rope(q, jnp.arange(S)) * (dh ** -0.5)).reshape(B, G, R, S, dh)
    gates = jax.nn.sigmoid((qg[..., HD:] + b_gate).astype(jnp.float32))
    gates = gates.reshape(B, S, G, R, N_BRANCH).transpose(0, 2, 3, 1, 4)
    n_cmp = kc.shape[2]
    n_blk = ks_blk.shape[2]
    n_sel = min(N_SELECT, n_blk)
    cmp_start = jnp.arange(n_cmp) * CMP_STRIDE
    cmp_end = cmp_start + CMP_BLOCK - 1
    blk_id = jnp.arange(n_blk)
    sel_start = blk_id * SEL_BLOCK
    cmp_to_sel = ((cmp_start[:, None] < sel_start[None, :] + SEL_BLOCK)
                  & (cmp_start[:, None] + CMP_BLOCK > sel_start[None, :])).astype(jnp.float32)

    def block(qb):
        start = qb * Q_BLOCK
        t = start + jnp.arange(Q_BLOCK)
        q_b = lax.dynamic_slice_in_dim(q, start, Q_BLOCK, axis=3)
        g_b = lax.dynamic_slice_in_dim(gates, start, Q_BLOCK, axis=3)
        s_c = jnp.einsum('bgrqd,bgnd->bgrqn', q_b, kc)
        p_c = _masked_softmax(s_c, cmp_end[None, :] <= t[:, None])
        o_c = jnp.einsum('bgrqn,bgnd->bgrqd', p_c.astype(vc.dtype), vc)
        imp = jnp.einsum('bgrqn,nj->bgqj', p_c, cmp_to_sel)
        cur = t // SEL_BLOCK
        valid = sel_start[None, :] <= t[:, None]
        forced = (blk_id[None, :] == 0) | (blk_id[None, :] == cur[:, None]) | (blk_id[None, :] == cur[:, None] - 1)
        score = jnp.where(valid, imp + jnp.where(forced, FORCE_BONUS, 0.0), -jnp.inf)
        _, sel = lax.top_k(score, n_sel)
        k_sel = _gather_blocks(ks_blk, sel)
        v_sel = _gather_blocks(vs_blk, sel)
        kpos = sel[..., None] * SEL_BLOCK + jnp.arange(SEL_BLOCK)
        m_s = (kpos <= t[None, None, :, None, None]).reshape(B, G, 1, Q_BLOCK, n_sel * SEL_BLOCK)
        s_s = jnp.einsum('bgrqd,bgqkld->bgrqkl', q_b, k_sel).reshape(B, G, R, Q_BLOCK, n_sel * SEL_BLOCK)
        p_s = _masked_softmax(s_s, m_s).reshape(B, G, R, Q_BLOCK, n_sel, SEL_BLOCK)
        o_s = jnp.einsum('bgrqkl,bgqkld->bgrqd', p_s.astype(v_sel.dtype), v_sel)
        k_w = lax.dynamic_slice_in_dim(kw_pad, start, Q_BLOCK + WINDOW, axis=2)
        v_w = lax.dynamic_slice_in_dim(vw_pad, start, Q_BLOCK + WINDOW, axis=2)
        wpos = start - WINDOW + jnp.arange(Q_BLOCK + WINDOW)
        m_w = (wpos[None, :] <= t[:, None]) & (wpos[None, :] > t[:, None] - WINDOW) & (wpos[None, :] >= 0)
        s_w = jnp.einsum('bgrqd,bgkd->bgrqk', q_b, k_w)
        p_w = _masked_softmax(s_w, m_w)
        o_w = jnp.einsum('bgrqk,bgkd->bgrqd', p_w.astype(v_w.dtype), v_w)
        o = g_b[..., 0:1] * o_c + g_b[..., 1:2] * o_s + g_b[..., 2:3] * o_w
        return o.astype(xn.dtype)

    out = lax.map(block, jnp.arange(S // Q_BLOCK))
    out = out.transpose(1, 0, 4, 2, 3, 5).reshape(B, S, HD)
    return out @ w_o


def _hier_moe(xn, rg_w, rg_b, re_w, re_b, w1, w3, w2):
    B, S, D = xn.shape
    T = B * S
    xt = xn.reshape(T, D)
    g_logits = (xt @ rg_w + rg_b).astype(jnp.float32)
    grp = jnp.argmax(g_logits, axis=-1).astype(jnp.int32)
    g_prob = jnp.take_along_axis(jax.nn.softmax(g_logits, axis=-1), grp[:, None], axis=1)[:, 0]
    e_logits = (xt @ re_w + re_b).astype(jnp.float32).reshape(T, N_EXPERT_GROUPS, EXPERTS_PER_GROUP)
    e_logits = jnp.take_along_axis(e_logits, grp[:, None, None], axis=1)[:, 0]
    top_val, top_idx = lax.top_k(e_logits, TOP_K_IN_GROUP)
    weights = g_prob[:, None] * jax.nn.softmax(top_val, axis=-1)
    experts = grp[:, None] * EXPERTS_PER_GROUP + top_idx.astype(jnp.int32)
    A = T * TOP_K_IN_GROUP
    e_flat = experts.reshape(A)
    w_flat = weights.reshape(A)
    tok_flat = jnp.repeat(jnp.arange(T, dtype=jnp.int32), TOP_K_IN_GROUP)
    order = jnp.argsort(e_flat)
    e_s, w_s, tok_s = e_flat[order], w_flat[order], tok_flat[order]
    counts = jnp.bincount(e_flat, length=N_EXPERTS)
    starts = jnp.cumsum(counts) - counts
    pcounts = (counts + MOE_CHUNK - 1) // MOE_CHUNK * MOE_CHUNK
    pends = jnp.cumsum(pcounts)
    pstarts = pends - pcounts
    dest = pstarts[e_s] + jnp.arange(A, dtype=jnp.int32) - starts[e_s]
    n_chunks = -(-A // MOE_CHUNK) + N_EXPERTS
    P = n_chunks * MOE_CHUNK
    buf_tok = jnp.zeros((P,), jnp.int32).at[dest].set(tok_s)
    buf_w = jnp.zeros((P,), jnp.float32).at[dest].set(w_s)
    chunk_e = jnp.clip(jnp.searchsorted(pends, jnp.arange(n_chunks, dtype=jnp.int32) * MOE_CHUNK, side='right'),
                       0, N_EXPERTS - 1)

    def run(args):
        tok_c, w_c, e = args
        xc = xt[tok_c]
        hc = jax.nn.silu(xc @ w1[e]) * (xc @ w3[e])
        return ((hc @ w2[e]) * w_c[:, None]).astype(xt.dtype)

    out = lax.map(run, (buf_tok.reshape(n_chunks, MOE_CHUNK), buf_w.reshape(n_chunks, MOE_CHUNK), chunk_e))
    y = jnp.zeros((T, D), xt.dtype).at[buf_tok].add(out.reshape(P, D))
    return y.reshape(B, S, D)


def setup_inputs(seed: int = 0) -> dict:
    key = jax.random.key(seed)
    ks = iter(jax.random.split(key, 40))
    f32 = jnp.float32

    def nrm(shape, scale):
        return scale * jax.random.normal(next(ks), shape, f32)

    D, HD = D_MODEL, N_HEADS * HEAD_DIM
    return {
        'x': nrm((BATCH, SEQ, D), 1.0),
        'p': nrm((DEPTH, BATCH, SEQ, PLE_DIM), 1.0),
        'norm_mix': 1.0 + nrm((DEPTH, D), 0.05),
        'norm_ffn': 1.0 + nrm((DEPTH, D), 0.05),
        'norm_ple': 1.0 + nrm((DEPTH, D), 0.05),
        'pool_w': nrm((N_A_LAYERS, len(POOL_WINDOWS), POOL_GROUP_DIM, POOL_GROUP_DIM), POOL_GROUP_DIM ** -0.5),
        'pool_b': nrm((N_A_LAYERS, D), 0.02),
        'pool_scale': 1.0 + nrm((N_A_LAYERS, D), 0.1),
        'kv_norm': 1.0 + nrm((D,), 0.05),
        'w_kv': nrm((D, 2 * N_BRANCH * KV_WIDTH), D ** -0.5),
        'cmp_k_pos': nrm((CMP_BLOCK, HEAD_DIM), 0.1),
        'cmp_k_w1': nrm((CMP_BLOCK, HEAD_DIM, CMP_HIDDEN), (CMP_BLOCK * HEAD_DIM) ** -0.5),
        'cmp_k_b1': nrm((CMP_HIDDEN,), 0.02),
        'cmp_k_w2': nrm((CMP_HIDDEN, HEAD_DIM), CMP_HIDDEN ** -0.5),
        'cmp_k_b2': nrm((HEAD_DIM,), 0.02),
        'cmp_v_pos': nrm((CMP_BLOCK, HEAD_DIM), 0.1),
        'cmp_v_w1': nrm((CMP_BLOCK, HEAD_DIM, CMP_HIDDEN), (CMP_BLOCK * HEAD_DIM) ** -0.5),
        'cmp_v_b1': nrm((CMP_HIDDEN,), 0.02),
        'cmp_v_w2': nrm((CMP_HIDDEN, HEAD_DIM), CMP_HIDDEN ** -0.5),
        'cmp_v_b2': nrm((HEAD_DIM,), 0.02),
        'w_qg': nrm((N_B_LAYERS, D, HD + N_HEADS * N_BRANCH), D ** -0.5),
        'b_gate': nrm((N_B_LAYERS, N_HEADS * N_BRANCH), 0.1),
        'w_o': nrm((N_B_LAYERS, HD, D), HD ** -0.5),
        'router_g_w': nrm((DEPTH, D, N_EXPERT_GROUPS), D ** -0.5),
        'router_g_b': nrm((DEPTH, N_EXPERT_GROUPS), 0.01),
        'router_e_w': nrm((DEPTH, D, N_EXPERTS), D ** -0.5),
        'router_e_b': nrm((DEPTH, N_EXPERTS), 0.01),
        'moe_w1': nrm((DEPTH, N_EXPERTS, D, D_FF_EXPERT), D ** -0.5),
        'moe_w3': nrm((DEPTH, N_EXPERTS, D, D_FF_EXPERT), D ** -0.5),
        'moe_w2': nrm((DEPTH, N_EXPERTS, D_FF_EXPERT, D), D_FF_EXPERT ** -0.5),
        'ple_proj': nrm((DEPTH, PLE_DIM, D), PLE_DIM ** -0.5),
        'ple_gate_w': nrm((DEPTH, D, D), D ** -0.5),
        'ple_gate_b': nrm((DEPTH, D), 0.02),
        'final_norm': 1.0 + nrm((D,), 0.05),
    }


def reference(x, p, norm_mix, norm_ffn, norm_ple, pool_w, pool_b, pool_scale, kv_norm, w_kv,
              cmp_k_pos, cmp_k_w1, cmp_k_b1, cmp_k_w2, cmp_k_b2,
              cmp_v_pos, cmp_v_w1, cmp_v_b1, cmp_v_w2, cmp_v_b2,
              w_qg, b_gate, w_o, router_g_w, router_g_b, router_e_w, router_e_b,
              moe_w1, moe_w3, moe_w2, ple_proj, ple_gate_w, ple_gate_b, final_norm):
    h = x
    shared = None
    for i in range(DEPTH):
        if i == N_A_LAYERS:
            shared = _shared_kv(h, kv_norm, w_kv, cmp_k_pos, cmp_k_w1, cmp_k_b1, cmp_k_w2, cmp_k_b2,
                                cmp_v_pos, cmp_v_w1, cmp_v_b1, cmp_v_w2, cmp_v_b2)
        xn = _rmsnorm(h, norm_mix[i])
        if i < N_A_LAYERS:
            h = h + _pool_mixer(xn, pool_w[i], pool_b[i], pool_scale[i])
        else:
            j = i - N_A_LAYERS
            h = h + _nsa_mixer(xn, w_qg[j], b_gate[j], w_o[j], *shared)
        h = h + _hier_moe(_rmsnorm(h, norm_ffn[i]), router_g_w[i], router_g_b[i], router_e_w[i],
                          router_e_b[i], moe_w1[i], moe_w3[i], moe_w2[i])
        gate = jax.nn.sigmoid(_rmsnorm(h, norm_ple[i]) @ ple_gate_w[i] + ple_gate_b[i])
        h = h + (p[i] @ ple_proj[i]) * gate
    return _rmsnorm(h, final_norm)
```

```python
import functools

import jax
import jax.numpy as jnp
from jax import lax
from jax.experimental import pallas as pl
from jax.experimental.pallas import tpu as pltpu

F32 = jnp.float32
BF16 = jnp.bfloat16
I32 = jnp.int32

POOL_WINDOWS = (2, 4, 8, 16)
N_HEADS = 16
HEAD_DIM = 64
N_KV_GROUPS = 4
HEADS_PER_GROUP = N_HEADS // N_KV_GROUPS
N_BRANCH = 3
ROPE_DIMS = HEAD_DIM // 4
ROPE_HALF = ROPE_DIMS // 2
ROPE_THETA = 500000.0
CMP_BLOCK = 32
CMP_STRIDE = 16
SEL_BLOCK = 64
N_SELECT = 16
WINDOW = 512
FORCE_BONUS = 1e4
N_EXPERT_GROUPS = 4
EXPERTS_PER_GROUP = 8
N_EXPERTS = N_EXPERT_GROUPS * EXPERTS_PER_GROUP
TOP_K_IN_GROUP = 2
RMS_EPS = 1e-6

LANES = 128
MAX_SEL_BLOCKS = LANES
MASK_BIAS = -30000.0

SEQ_TILE = 512
TOK_TILE = 512
CMB_TILE = 256
MOE_CHUNK = 512
Q_TILE = 128
KEY_TILE = 512
HALO = 16

HIGHEST = lax.Precision.HIGHEST


def _cparams(sem, vmem_mb):
    return pltpu.CompilerParams(dimension_semantics=sem, vmem_limit_bytes=vmem_mb * 1024 * 1024)


def _rms(x, g):
    return x * lax.rsqrt(jnp.mean(x * x, axis=-1, keepdims=True) + RMS_EPS) * g


def _dot(a, b):
    return jnp.dot(a, b, preferred_element_type=F32)


def _dot_nt(a, b):
    return lax.dot_general(a, b, (((1,), (1,)), ((), ())), preferred_element_type=F32)


def _pool_kernel(h_ref, halo_ref, g_ref, w_ref, b_ref, sc_ref, o_ref, *, ts, cg):
    i = pl.program_id(1)
    x = h_ref[0]
    g = g_ref[...]
    xn = _rms(x, g)
    hn = _rms(halo_ref[0], g)
    hn = jnp.where(i > 0, hn, 0.0)
    ext = jnp.concatenate([hn, xn], axis=0)
    t = i * ts + lax.broadcasted_iota(I32, (ts, 1), 0)
    outs = []
    for gi, w in enumerate(POOL_WINDOWS):
        s = ext[:, gi * cg:(gi + 1) * cg]
        k = 1
        while k < w:
            s = s + pltpu.roll(s, k, axis=0)
            k *= 2
        cnt = jnp.minimum(t + 1, w).astype(F32)
        pooled = s[HALO:] / cnt - xn[:, gi * cg:(gi + 1) * cg]
        outs.append(_dot(pooled.astype(BF16), w_ref[gi]))
    y = jnp.concatenate(outs, axis=-1)
    o_ref[0] = x + (y + b_ref[...]) * sc_ref[...]


def _pool_layer(h, g, w, b, sc):
    B, S, D = h.shape
    ts = min(SEQ_TILE, S)
    cg = D // len(POOL_WINDOWS)
    row = lambda v: v.reshape(1, D)
    return pl.pallas_call(
        functools.partial(_pool_kernel, ts=ts, cg=cg),
        out_shape=jax.ShapeDtypeStruct((B, S, D), F32),
        grid=(B, S // ts),
        in_specs=[
            pl.BlockSpec((1, ts, D), lambda b_, i: (b_, i, 0)),
            pl.BlockSpec((1, HALO, D), lambda b_, i: (b_, jnp.maximum(i * (ts // HALO) - 1, 0), 0)),
            pl.BlockSpec((1, D), lambda b_, i: (0, 0)),
            pl.BlockSpec((len(POOL_WINDOWS), cg, cg), lambda b_, i: (0, 0, 0)),
            pl.BlockSpec((1, D), lambda b_, i: (0, 0)),
            pl.BlockSpec((1, D), lambda b_, i: (0, 0)),
        ],
        out_specs=pl.BlockSpec((1, ts, D), lambda b_, i: (b_, i, 0)),
        compiler_params=_cparams(("parallel", "parallel"), 40),
        name="pool_mixer",
    )(h, h, row(g), w.astype(BF16), row(b), row(sc))


def _router_kernel(h_ref, g_ref, w_ref, b_ref, info_ref, cnt_ref, *, tt):
    i = pl.program_id(0)

    @pl.when(i == 0)
    def _():
        cnt_ref[...] = jnp.zeros_like(cnt_ref)

    xn = _rms(h_ref[...], g_ref[...])
    logits = jnp.dot(xn, w_ref[...], precision=HIGHEST, preferred_element_type=F32) + b_ref[...]
    lane = lax.broadcasted_iota(I32, (tt, LANES), 1)
    neg = -jnp.inf
    gl = jnp.where(lane < N_EXPERT_GROUPS, logits, neg)
    gmax = jnp.max(gl, axis=-1, keepdims=True)
    grp = jnp.min(jnp.where(gl == gmax, lane, LANES), axis=-1, keepdims=True)
    gprob = 1.0 / jnp.sum(jnp.exp(gl - gmax), axis=-1, keepdims=True)
    lo = N_EXPERT_GROUPS + grp * EXPERTS_PER_GROUP
    el = jnp.where(lane >= lo, jnp.where(lane < lo + EXPERTS_PER_GROUP, logits, neg), neg)
    v1 = jnp.max(el, axis=-1, keepdims=True)
    i1 = jnp.min(jnp.where(el == v1, lane, LANES), axis=-1, keepdims=True)
    el2 = jnp.where(lane == i1, neg, el)
    v2 = jnp.max(el2, axis=-1, keepdims=True)
    i2 = jnp.min(jnp.where(el2 == v2, lane, LANES), axis=-1, keepdims=True)
    e2 = jnp.exp(v2 - v1)
    w1 = gprob / (1.0 + e2)
    w2 = gprob * e2 / (1.0 + e2)
    oh1 = lane == i1
    oh2 = lane == i2
    oh = jnp.where(oh1, 1.0, jnp.where(oh2, 1.0, 0.0))
    r_ = lax.broadcasted_iota(I32, (tt, tt), 0)
    c_ = lax.broadcasted_iota(I32, (tt, tt), 1)
    tri = jnp.where(r_ > c_, 1.0, 0.0).astype(BF16)
    tot = _dot(tri, oh.astype(BF16)) + cnt_ref[...]
    r1 = jnp.sum(jnp.where(oh1, tot, 0.0), axis=-1, keepdims=True)
    r2 = jnp.sum(jnp.where(oh2, tot, 0.0), axis=-1, keepdims=True)
    cnt_ref[...] = cnt_ref[...] + jnp.sum(oh, axis=0, keepdims=True)
    vals = (i1.astype(F32) - N_EXPERT_GROUPS, i2.astype(F32) - N_EXPERT_GROUPS, w1, w2, r1, r2)
    info = jnp.zeros((tt, LANES), F32)
    for k, v in enumerate(vals):
        info = jnp.where(lane == k, v, info)
    info_ref[...] = info


def _router(h2, g, wr, br):
    T, D = h2.shape
    tt = min(TOK_TILE, T)
    return pl.pallas_call(
        functools.partial(_router_kernel, tt=tt),
        out_shape=(jax.ShapeDtypeStruct((T, LANES), F32), jax.ShapeDtypeStruct((1, LANES), F32)),
        grid=(T // tt,),
        in_specs=[
            pl.BlockSpec((tt, D), lambda i: (i, 0)),
            pl.BlockSpec((1, D), lambda i: (0, 0)),
            pl.BlockSpec((D, LANES), lambda i: (0, 0)),
            pl.BlockSpec((1, LANES), lambda i: (0, 0)),
        ],
        out_specs=(pl.BlockSpec((tt, LANES), lambda i: (i, 0)), pl.BlockSpec((1, LANES), lambda i: (0, 0))),
        compiler_params=_cparams(("arbitrary",), 40),
        name="moe_router",
    )(h2, g.reshape(1, D), wr, br)


def _row_copy(src, s, dst, d, sem):
    return pltpu.make_async_copy(src.at[pl.ds(s, 1)], dst.at[pl.ds(d, 1)], sem)


def _dispatch_kernel(dest_ref, h_ref, g_ref, xs_in_ref, xs_ref, xn_s, sem, *, tt, n_tok):
    del xs_in_ref
    base = pl.program_id(0) * tt
    xn_s[...] = _rms(h_ref[...], g_ref[...])

    def issue(r, c):
        for k in range(TOP_K_IN_GROUP):
            _row_copy(xn_s, r, xs_ref, dest_ref[k * n_tok + base + r], sem).start()
        return c

    lax.fori_loop(0, tt, issue, 0)

    def drain(r, c):
        for k in range(TOP_K_IN_GROUP):
            _row_copy(xn_s, r, xs_ref, dest_ref[k * n_tok + base + r], sem).wait()
        return c

    lax.fori_loop(0, tt, drain, 0)


def _dispatch(dest, h2, g, n_rows):
    T, D = h2.shape
    tt = min(TOK_TILE, T)
    xs0 = jnp.zeros((n_rows, D), F32)
    return pl.pallas_call(
        functools.partial(_dispatch_kernel, tt=tt, n_tok=T),
        out_shape=jax.ShapeDtypeStruct((n_rows, D), F32),
        grid_spec=pltpu.PrefetchScalarGridSpec(
            num_scalar_prefetch=1,
            grid=(T // tt,),
            in_specs=[
                pl.BlockSpec((tt, D), lambda i, d: (i, 0)),
                pl.BlockSpec((1, D), lambda i, d: (0, 0)),
                pl.BlockSpec(memory_space=pl.ANY),
            ],
            out_specs=pl.BlockSpec(memory_space=pl.ANY),
            scratch_shapes=[pltpu.VMEM((tt, D), F32), pltpu.SemaphoreType.DMA],
        ),
        input_output_aliases={3: 0},
        compiler_params=_cparams(("arbitrary",), 40),
        name="moe_dispatch",
    )(dest, h2, g.reshape(1, D), xs0)


def _expert_kernel(ce_ref, nv_ref, xs_ref, w1_ref, w3_ref, w2_ref, o_ref, w1b, w3b, w2b):
    c = pl.program_id(0)
    e = ce_ref[c]
    prev = ce_ref[jnp.maximum(c - 1, 0)]

    @pl.when(jnp.logical_or(c == 0, e != prev))
    def _():
        w1b[...] = w1_ref[0].astype(BF16)
        w3b[...] = w3_ref[0].astype(BF16)
        w2b[...] = w2_ref[0].astype(BF16)

    @pl.when(c < nv_ref[0])
    def _():
        x = xs_ref[...].astype(BF16)
        a = _dot(x, w1b[...])
        b = _dot(x, w3b[...])
        hc = a * jax.nn.sigmoid(a) * b
        o_ref[...] = _dot(hc.astype(BF16), w2b[...])

    @pl.when(c >= nv_ref[0])
    def _():
        o_ref[...] = jnp.zeros_like(o_ref)


def _experts(chunk_e, n_valid, xs, w1, w3, w2):
    P, D = xs.shape
    F = w1.shape[-1]
    ch = MOE_CHUNK
    return pl.pallas_call(
        _expert_kernel,
        out_shape=jax.ShapeDtypeStruct((P, D), F32),
        grid_spec=pltpu.PrefetchScalarGridSpec(
            num_scalar_prefetch=2,
            grid=(P // ch,),
            in_specs=[
                pl.BlockSpec((ch, D), lambda c, ce, nv: (c, 0)),
                pl.BlockSpec((1, D, F), lambda c, ce, nv: (ce[c], 0, 0)),
                pl.BlockSpec((1, D, F), lambda c, ce, nv: (ce[c], 0, 0)),
                pl.BlockSpec((1, F, D), lambda c, ce, nv: (ce[c], 0, 0)),
            ],
            out_specs=pl.BlockSpec((ch, D), lambda c, ce, nv: (c, 0)),
            scratch_shapes=[pltpu.VMEM((D, F), BF16), pltpu.VMEM((D, F), BF16), pltpu.VMEM((F, D), BF16)],
        ),
        compiler_params=_cparams(("arbitrary",), 56),
        name="moe_experts",
    )(chunk_e, n_valid, xs, w1, w3, w2)


def _combine_kernel(dest_ref, h_ref, info_ref, rows_ref, p_ref, g_ref, gw_ref, gb_ref, pw_ref, fn_ref,
                    o_ref, buf, sem, *, tt, n_tok, final):
    base = pl.program_id(0) * tt

    def issue(r, c):
        for k in range(TOP_K_IN_GROUP):
            _row_copy(rows_ref, dest_ref[k * n_tok + base + r], buf.at[k], r, sem).start()
        return c

    lax.fori_loop(0, tt, issue, 0)

    def drain(r, c):
        for k in range(TOP_K_IN_GROUP):
            _row_copy(rows_ref, dest_ref[k * n_tok + base + r], buf.at[k], r, sem).wait()
        return c

    lax.fori_loop(0, tt, drain, 0)
    info = info_ref[...]
    y = h_ref[...] + info[:, 2:3] * buf[0] + info[:, 3:4] * buf[1]
    hn = _rms(y, g_ref[...])
    gate = jax.nn.sigmoid(_dot(hn.astype(BF16), gw_ref[...]) + gb_ref[...])
    out = y + _dot(p_ref[...].astype(BF16), pw_ref[...]) * gate
    if final:
        out = _rms(out, fn_ref[...])
    o_ref[...] = out


def _combine(dest, h2, info, rows, p2, g, gw, gb, pw, fn, final):
    T, D = h2.shape
    PD = p2.shape[-1]
    tt = min(CMB_TILE, T)
    full = lambda i, d: (0, 0)
    return pl.pallas_call(
        functools.partial(_combine_kernel, tt=tt, n_tok=T, final=final),
        out_shape=jax.ShapeDtypeStruct((T, D), F32),
        grid_spec=pltpu.PrefetchScalarGridSpec(
            num_scalar_prefetch=1,
            grid=(T // tt,),
            in_specs=[
                pl.BlockSpec((tt, D), lambda i, d: (i, 0)),
                pl.BlockSpec((tt, LANES), lambda i, d: (i, 0)),
                pl.BlockSpec(memory_space=pl.ANY),
                pl.BlockSpec((tt, PD), lambda i, d: (i, 0)),
                pl.BlockSpec((1, D), full),
                pl.BlockSpec((D, D), full),
                pl.BlockSpec((1, D), full),
                pl.BlockSpec((PD, D), full),
                pl.BlockSpec((1, D), full),
            ],
            out_specs=pl.BlockSpec((tt, D), lambda i, d: (i, 0)),
            scratch_shapes=[pltpu.VMEM((TOP_K_IN_GROUP, tt, D), F32), pltpu.SemaphoreType.DMA],
        ),
        compiler_params=_cparams(("arbitrary",), 40),
        name="moe_combine_ple",
    )(dest, h2, info, rows, p2, g.reshape(1, D), gw.astype(BF16), gb.reshape(1, D), pw.astype(BF16),
      fn.reshape(1, D))


def _moe_ple_layer(h2, p2, norm_ffn, rg_w, rg_b, re_w, re_b, w1, w3, w2, norm_ple, gate_w, gate_b, ple_proj,
                   final_norm, final):
    T, D = h2.shape
    A = T * TOP_K_IN_GROUP
    pad = LANES - N_EXPERT_GROUPS - N_EXPERTS
    wr = jnp.concatenate([rg_w, re_w, jnp.zeros((D, pad), F32)], axis=1)
    br = jnp.concatenate([rg_b, re_b, jnp.zeros((pad,), F32)]).reshape(1, LANES)
    info, cnt = _router(h2, norm_ffn, wr, br)
    counts = cnt[0, N_EXPERT_GROUPS:N_EXPERT_GROUPS + N_EXPERTS].astype(I32)
    n_chunks_e = (counts + MOE_CHUNK - 1) // MOE_CHUNK
    chunk_end = jnp.cumsum(n_chunks_e)
    pstarts = (chunk_end - n_chunks_e) * MOE_CHUNK
    n_chunks = -(-A // MOE_CHUNK) + N_EXPERTS
    chunk_e = jnp.searchsorted(chunk_end, jnp.arange(n_chunks, dtype=I32), side="right").astype(I32)
    n_valid = chunk_end[-1:].astype(I32)
    chunk_e = jnp.minimum(chunk_e, chunk_e[jnp.maximum(n_valid[0] - 1, 0)])
    e_idx = info[:, 0:TOP_K_IN_GROUP].astype(I32)
    rank = info[:, 4:4 + TOP_K_IN_GROUP].astype(I32)
    dest = (pstarts[e_idx] + rank).T.reshape(A)
    xs = _dispatch(dest, h2, norm_ffn, n_chunks * MOE_CHUNK)
    rows = _experts(chunk_e, n_valid, xs, w1, w3, w2)
    return _combine(dest, h2, info, rows, p2, norm_ple, gate_w, gate_b, ple_proj, final_norm, final)


def _rope_rows(xt, cos, sin):
    x1 = xt[0:ROPE_HALF]
    x2 = xt[ROPE_HALF:ROPE_DIMS]
    return jnp.concatenate([x1 * cos - x2 * sin, x2 * cos + x1 * sin, xt[ROPE_DIMS:]], axis=0)


def _kv_kernel(h_ref, g_ref, wn_ref, wt_ref, cos_ref, sin_ref, cv_ref, kst_ref, vs_ref, kwt_ref, vw_ref, *, ts):
    i = pl.program_id(1)
    kvw = N_KV_GROUPS * HEAD_DIM
    hn = _rms(h_ref[0], g_ref[...]).astype(BF16)
    nat = _dot(hn, wn_ref[...])
    cv_ref[0] = nat[:, 0:2 * kvw]
    tr = _dot_nt(wt_ref[...], hn)
    cos = cos_ref[...]
    sin = sin_ref[...]
    blk = lax.broadcasted_iota(I32, (MAX_SEL_BLOCKS, ts), 0)
    pos = i * ts + lax.broadcasted_iota(I32, (MAX_SEL_BLOCKS, ts), 1)
    onehot = jnp.where(pos // SEL_BLOCK == blk, 1.0, 0.0).astype(BF16)
    for g in range(N_KV_GROUPS):
        c0 = g * HEAD_DIM
        kst_ref[0, g, 0:MAX_SEL_BLOCKS, :] = onehot
        kst_ref[0, g, MAX_SEL_BLOCKS:, :] = _rope_rows(tr[c0:c0 + HEAD_DIM], cos, sin).astype(BF16)
        kwt_ref[0, g] = _rope_rows(tr[kvw + c0:kvw + c0 + HEAD_DIM], cos, sin).astype(BF16)
        vs_ref[0, g] = nat[:, 2 * kvw + c0:2 * kvw + c0 + HEAD_DIM].astype(BF16)
        vw_ref[0, g] = nat[:, 3 * kvw + c0:3 * kvw + c0 + HEAD_DIM].astype(BF16)


def _rope_tables(pos):
    inv = jnp.float32(ROPE_THETA) ** (-jnp.arange(ROPE_HALF, dtype=F32) * 2.0 / ROPE_DIMS)
    ang = pos.astype(F32)[:, None] * inv[None, :]
    return jnp.cos(ang), jnp.sin(ang)


def _shared_kv_proj(h, kv_norm, w_kv):
    B, S, D = h.shape
    G, dh = N_KV_GROUPS, HEAD_DIM
    kvw = G * dh
    ts = min(SEQ_TILE, S)
    br = lambda k: w_kv[:, k * kvw:(k + 1) * kvw]
    w_nat = jnp.concatenate([br(0), br(1), br(3), br(5)], axis=1).astype(BF16)
    w_tr = jnp.concatenate([br(2), br(4)], axis=1).T.astype(BF16)
    cos, sin = _rope_tables(jnp.arange(S))
    kd = MAX_SEL_BLOCKS + dh
    return pl.pallas_call(
        functools.partial(_kv_kernel, ts=ts),
        out_shape=(
            jax.ShapeDtypeStruct((B, S, 2 * kvw), F32),
            jax.ShapeDtypeStruct((B, G, kd, S), BF16),
            jax.ShapeDtypeStruct((B, G, S, dh), BF16),
            jax.ShapeDtypeStruct((B, G, dh, S), BF16),
            jax.ShapeDtypeStruct((B, G, S, dh), BF16),
        ),
        grid=(B, S // ts),
        in_specs=[
            pl.BlockSpec((1, ts, D), lambda b, i: (b, i, 0)),
            pl.BlockSpec((1, D), lambda b, i: (0, 0)),
            pl.BlockSpec((D, 4 * kvw), lambda b, i: (0, 0)),
            pl.BlockSpec((2 * kvw, D), lambda b, i: (0, 0)),
            pl.BlockSpec((ROPE_HALF, ts), lambda b, i: (0, i)),
            pl.BlockSpec((ROPE_HALF, ts), lambda b, i: (0, i)),
        ],
        out_specs=(
            pl.BlockSpec((1, ts, 2 * kvw), lambda b, i: (b, i, 0)),
            pl.BlockSpec((1, G, kd, ts), lambda b, i: (b, 0, 0, i)),
            pl.BlockSpec((1, G, ts, dh), lambda b, i: (b, 0, i, 0)),
            pl.BlockSpec((1, G, dh, ts), lambda b, i: (b, 0, 0, i)),
            pl.BlockSpec((1, G, ts, dh), lambda b, i: (b, 0, i, 0)),
        ),
        compiler_params=_cparams(("parallel", "parallel"), 48),
        name="shared_kv_proj",
    )(h, kv_norm.reshape(1, D), w_nat, w_tr, cos.T, sin.T)


def _compress_kernel(x_ref, pos_ref, w1_ref, b1_ref, w2_ref, b2_ref, cos_ref, sin_ref, o_ref, *, nh, keys):
    x = x_ref[0, 0, 0]
    a = _dot((x + pos_ref[0:1]).astype(BF16), w1_ref[0])
    b = _dot((x + pos_ref[1:2]).astype(BF16), w1_ref[1])
    hid = jax.nn.gelu(a + pltpu.roll(b, nh - 1, axis=0) + b1_ref[...]).astype(BF16)
    if keys:
        out = _dot_nt(w2_ref[...], hid) + b2_ref[...]
        o_ref[0, 0] = _rope_rows(out, cos_ref[...], sin_ref[...]).astype(BF16)
    else:
        o_ref[0, 0] = (_dot(hid, w2_ref[...]) + b2_ref[...]).astype(BF16)


def _compress(halves, which, pos_emb, w1, b1, w2, b2, keys):
    _, B, G, nh, hw = halves.shape
    dh = HEAD_DIM
    hidden = w1.shape[-1]
    pos2 = pos_emb.reshape(2, hw)
    w1s = w1.reshape(2, hw, hidden).astype(BF16)
    cos, sin = _rope_tables(jnp.arange(nh) * CMP_STRIDE + CMP_BLOCK - 1)
    if keys:
        w2a, b2a = w2.T.astype(BF16), b2.reshape(dh, 1)
        out_shape, out_block, out_idx = (B, G, dh, nh), (1, 1, dh, nh), lambda b, g: (b, g, 0, 0)
    else:
        w2a, b2a = w2.astype(BF16), b2.reshape(1, dh)
        out_shape, out_block, out_idx = (B, G, nh, dh), (1, 1, nh, dh), lambda b, g: (b, g, 0, 0)
    c2 = lambda b, g: (0, 0)
    return pl.pallas_call(
        functools.partial(_compress_kernel, nh=nh, keys=keys),
        out_shape=jax.ShapeDtypeStruct(out_shape, BF16),
        grid=(B, G),
        in_specs=[
            pl.BlockSpec((1, 1, 1, nh, hw), lambda b, g: (which, b, g, 0, 0)),
            pl.BlockSpec((2, hw), c2),
            pl.BlockSpec((2, hw, hidden), lambda b, g: (0, 0, 0)),
            pl.BlockSpec((1, hidden), c2),
            pl.BlockSpec(w2a.shape, c2),
            pl.BlockSpec(b2a.shape, c2),
            pl.BlockSpec((ROPE_HALF, nh), c2),
            pl.BlockSpec((ROPE_HALF, nh), c2),
        ],
        out_specs=pl.BlockSpec(out_block, out_idx),
        compiler_params=_cparams(("parallel", "parallel"), 40),
        name="compress_k" if keys else "compress_v",
    )(halves, pos2, w1s, b1.reshape(1, hidden), w2a, b2a, cos.T, sin.T)


def _qproj_kernel(h_ref, g_ref, wq_ref, wg_ref, bg_ref, c_ref, s1_ref, s2_ref, q_ref, gt_ref):
    xn = _rms(h_ref[0], g_ref[...]).astype(BF16)
    q = _dot(xn, wq_ref[...])
    c, s1, s2 = c_ref[...], s1_ref[...], s2_ref[...]
    for k in range(q.shape[-1] // LANES):
        x = q[:, k * LANES:(k + 1) * LANES]
        r = x * c + pltpu.roll(x, ROPE_HALF, axis=1) * s1 + pltpu.roll(x, LANES - ROPE_HALF, axis=1) * s2
        q_ref[0, :, k * LANES:(k + 1) * LANES] = r.astype(BF16)
    gt_ref[0] = jax.nn.sigmoid(_dot(xn, wg_ref[...]) + bg_ref[...])


def _q_proj(h, g, w_qg, b_gate):
    B, S, D = h.shape
    HD = N_HEADS * HEAD_DIM
    ng = N_HEADS * N_BRANCH
    ts = min(SEQ_TILE, S)
    wq = w_qg[:, :HD].astype(BF16)
    wg = jnp.pad(w_qg[:, HD:], ((0, 0), (0, LANES - ng))).astype(BF16)
    bg = jnp.pad(b_gate, (0, LANES - ng)).reshape(1, LANES)
    cos, sin = _rope_tables(jnp.arange(S))
    scale = HEAD_DIM ** -0.5
    ones = jnp.ones((S, HEAD_DIM - ROPE_DIMS), F32)
    zeros = jnp.zeros((S, HEAD_DIM - ROPE_DIMS), F32)
    zh = jnp.zeros((S, ROPE_HALF), F32)
    two = lambda a: jnp.concatenate([a, a], axis=1) * scale
    c = two(jnp.concatenate([cos, cos, ones], axis=1))
    s1 = two(jnp.concatenate([zh, sin, zeros], axis=1))
    s2 = two(jnp.concatenate([-sin, zh, zeros], axis=1))
    tab = pl.BlockSpec((ts, LANES), lambda b, i: (i, 0))
    return pl.pallas_call(
        _qproj_kernel,
        out_shape=(jax.ShapeDtypeStruct((B, S, HD), BF16), jax.ShapeDtypeStruct((B, S, LANES), F32)),
        grid=(B, S // ts),
        in_specs=[
            pl.BlockSpec((1, ts, D), lambda b, i: (b, i, 0)),
            pl.BlockSpec((1, D), lambda b, i: (0, 0)),
            pl.BlockSpec((D, HD), lambda b, i: (0, 0)),
            pl.BlockSpec((D, LANES), lambda b, i: (0, 0)),
            pl.BlockSpec((1, LANES), lambda b, i: (0, 0)),
            tab, tab, tab,
        ],
        out_specs=(pl.BlockSpec((1, ts, HD), lambda b, i: (b, i, 0)),
                   pl.BlockSpec((1, ts, LANES), lambda b, i: (b, i, 0))),
        compiler_params=_cparams(("parallel", "parallel"), 40),
        name="nsa_q_proj",
    )(h, g.reshape(1, D), wq, wg, bg, c, s1, s2)


def _stack_heads(qb):
    return jnp.concatenate([qb[:, r * HEAD_DIM:(r + 1) * HEAD_DIM] for r in range(HEADS_PER_GROUP)], axis=0)


def _gated_unstack(o, gates, g, branch, tq):
    lane = lax.broadcasted_iota(I32, gates.shape, 1)
    cols = []
    for r in range(HEADS_PER_GROUP):
        col = (g * HEADS_PER_GROUP + r) * N_BRANCH + branch
        gate = jnp.sum(jnp.where(lane == col, gates, 0.0), axis=-1, keepdims=True)
        cols.append(o[r * tq:(r + 1) * tq] * gate)
    return jnp.concatenate(cols, axis=-1)


def _query_pos(i, tq):
    rows = lax.broadcasted_iota(I32, (HEADS_PER_GROUP * tq, 1), 0)
    return i * tq + (rows & (tq - 1))


def _cmp_kernel(q_ref, gt_ref, k_ref, v_ref, o_ref, bias_ref, *, tq, nc):
    g = pl.program_id(1)
    i = pl.program_id(2)
    qs = _stack_heads(q_ref[0])
    s = _dot(qs, k_ref[0, 0])
    t = _query_pos(i, tq)
    n = lax.broadcasted_iota(I32, (1, nc), 1)
    s = jnp.where(n * CMP_STRIDE + (CMP_BLOCK - 1) <= t, s, -jnp.inf)
    m = jnp.max(s, axis=-1, keepdims=True)
    m = jnp.where(m == -jnp.inf, 0.0, m)
    e = jnp.exp(s - m)
    p = e / jnp.maximum(jnp.sum(e, axis=-1, keepdims=True), 1e-30)
    o = _dot(p.astype(BF16), v_ref[0, 0])
    o_ref[0] = _gated_unstack(o, gt_ref[0], g, 0, tq).astype(BF16)
    ps = p[0:tq]
    for r in range(1, HEADS_PER_GROUP):
        ps = ps + p[r * tq:(r + 1) * tq]
    nn = lax.broadcasted_iota(I32, (nc, MAX_SEL_BLOCKS), 0) * CMP_STRIDE
    jj = lax.broadcasted_iota(I32, (nc, MAX_SEL_BLOCKS), 1) * SEL_BLOCK
    c2s = jnp.where(nn < jj + SEL_BLOCK, jnp.where(nn + CMP_BLOCK > jj, 1.0, 0.0), 0.0)
    imp = jnp.dot(ps, c2s, precision=HIGHEST, preferred_element_type=F32)
    tq_pos = i * tq + lax.broadcasted_iota(I32, (tq, 1), 0)
    cur = tq_pos // SEL_BLOCK
    j = lax.broadcasted_iota(I32, (tq, MAX_SEL_BLOCKS), 1)
    valid = j <= cur
    bonus = jnp.where(j == 0, FORCE_BONUS, jnp.where(j == cur, FORCE_BONUS, jnp.where(j == cur - 1, FORCE_BONUS, 0.0)))
    work = jnp.where(valid, imp + bonus, -jnp.inf)
    sel = jnp.zeros((tq, MAX_SEL_BLOCKS), F32)
    for _ in range(N_SELECT):
        mx = jnp.max(work, axis=-1, keepdims=True)
        idx = jnp.min(jnp.where(work == mx, j, MAX_SEL_BLOCKS), axis=-1, keepdims=True)
        pick = j == idx
        sel = jnp.where(pick, 1.0, sel)
        work = jnp.where(pick, -jnp.inf, work)
    bias_ref[0, 0] = jnp.where(valid, jnp.where(sel > 0.0, 0.0, MASK_BIAS), MASK_BIAS).astype(BF16)


def _cmp_attn(q, gates, kct, vc):
    B, S, HD = q.shape
    G, dh = N_KV_GROUPS, HEAD_DIM
    gw = HD // G
    nc = kct.shape[-1]
    tq = min(Q_TILE, S)
    return pl.pallas_call(
        functools.partial(_cmp_kernel, tq=tq, nc=nc),
        out_shape=(jax.ShapeDtypeStruct((B, S, HD), BF16), jax.ShapeDtypeStruct((B, G, S, MAX_SEL_BLOCKS), BF16)),
        grid=(B, G, S // tq),
        in_specs=[
            pl.BlockSpec((1, tq, gw), lambda b, g, i: (b, i, g)),
            pl.BlockSpec((1, tq, LANES), lambda b, g, i: (b, i, 0)),
            pl.BlockSpec((1, 1, dh, nc), lambda b, g, i: (b, g, 0, 0)),
            pl.BlockSpec((1, 1, nc, dh), lambda b, g, i: (b, g, 0, 0)),
        ],
        out_specs=(pl.BlockSpec((1, tq, gw), lambda b, g, i: (b, i, g)),
                   pl.BlockSpec((1, 1, tq, MAX_SEL_BLOCKS), lambda b, g, i: (b, g, i, 0))),
        compiler_params=_cparams(("parallel", "parallel", "parallel"), 40),
        name="nsa_compressed",
    )(q, gates, kct, vc)


def _sel_kernel(q_ref, bias_ref, gt_ref, k_ref, v_ref, o_ref, qa, m_s, l_s, acc, *, tq, tk):
    g = pl.program_id(1)
    i = pl.program_id(2)
    qb = q_ref[0]
    bias = bias_ref[0, 0]
    for r in range(HEADS_PER_GROUP):
        qa[r * tq:(r + 1) * tq, 0:MAX_SEL_BLOCKS] = bias
        qa[r * tq:(r + 1) * tq, MAX_SEL_BLOCKS:] = qb[:, r * HEAD_DIM:(r + 1) * HEAD_DIM]
    m_s[...] = jnp.full_like(m_s, -jnp.inf)
    l_s[...] = jnp.zeros_like(l_s)
    acc[...] = jnp.zeros_like(acc)
    t = _query_pos(i, tq)
    lane = lax.broadcasted_iota(I32, (1, tk), 1)

    def body(jt, c):
        k0 = pl.multiple_of(jt * tk, tk)
        s = _dot(qa[...], k_ref[0, 0, :, pl.ds(k0, tk)])
        s = jnp.where(k0 + lane <= t, s, -jnp.inf)
        m_old = m_s[...]
        m_new = jnp.maximum(m_old, jnp.max(s, axis=-1, keepdims=True))
        alpha = jnp.exp(m_old - m_new)
        p = jnp.exp(s - m_new)
        l_s[...] = alpha * l_s[...] + jnp.sum(p, axis=-1, keepdims=True)
        acc[...] = alpha * acc[...] + _dot(p.astype(BF16), v_ref[0, 0, pl.ds(k0, tk), :])
        m_s[...] = m_new
        return c

    lax.fori_loop(0, (i * tq + tq + tk - 1) // tk, body, 0)
    o_ref[0] = _gated_unstack(acc[...] / l_s[...], gt_ref[0], g, 1, tq).astype(BF16)


def _sel_attn(q, bias, gates, kst, vs):
    B, S, HD = q.shape
    G, dh = N_KV_GROUPS, HEAD_DIM
    gw = HD // G
    kd = kst.shape[2]
    tq = min(Q_TILE, S)
    tk = min(KEY_TILE, S)
    rows = HEADS_PER_GROUP * tq
    return pl.pallas_call(
        functools.partial(_sel_kernel, tq=tq, tk=tk),
        out_shape=jax.ShapeDtypeStruct((B, S, HD), BF16),
        grid=(B, G, S // tq),
        in_specs=[
            pl.BlockSpec((1, tq, gw), lambda b, g, i: (b, i, g)),
            pl.BlockSpec((1, 1, tq, MAX_SEL_BLOCKS), lambda b, g, i: (b, g, i, 0)),
            pl.BlockSpec((1, tq, LANES), lambda b, g, i: (b, i, 0)),
            pl.BlockSpec((1, 1, kd, S), lambda b, g, i: (b, g, 0, 0)),
            pl.BlockSpec((1, 1, S, dh), lambda b, g, i: (b, g, 0, 0)),
        ],
        out_specs=pl.BlockSpec((1, tq, gw), lambda b, g, i: (b, i, g)),
        scratch_shapes=[pltpu.VMEM((rows, kd), BF16), pltpu.VMEM((rows, 1), F32), pltpu.VMEM((rows, 1), F32),
                        pltpu.VMEM((rows, dh), F32)],
        compiler_params=_cparams(("parallel", "parallel", "arbitrary"), 48),
        name="nsa_selected",
    )(q, bias, gates, kst, vs)


def _win_kernel(q_ref, gt_ref, k_ref, v_ref, o_ref, *, tq, wb):
    g = pl.program_id(1)
    i = pl.program_id(2)
    k0 = pl.multiple_of(jnp.maximum(i * tq - WINDOW, 0), tq)
    qs = _stack_heads(q_ref[0])
    s = _dot(qs, k_ref[0, 0, :, pl.ds(k0, wb)])
    t = _query_pos(i, tq)
    kpos = k0 + lax.broadcasted_iota(I32, (1, wb), 1)
    s = jnp.where(kpos <= t, jnp.where(kpos > t - WINDOW, s, -jnp.inf), -jnp.inf)
    m = jnp.max(s, axis=-1, keepdims=True)
    e = jnp.exp(s - m)
    p = e / jnp.sum(e, axis=-1, keepdims=True)
    o = _dot(p.astype(BF16), v_ref[0, 0, pl.ds(k0, wb), :])
    o_ref[0] = _gated_unstack(o, gt_ref[0], g, 2, tq).astype(BF16)


def _win_attn(q, gates, kwt, vw):
    B, S, HD = q.shape
    G, dh = N_KV_GROUPS, HEAD_DIM
    gw = HD // G
    tq = min(Q_TILE, S)
    wb = WINDOW + tq
    assert S >= wb and tq % LANES == 0
    return pl.pallas_call(
        functools.partial(_win_kernel, tq=tq, wb=wb),
        out_shape=jax.ShapeDtypeStruct((B, S, HD), BF16),
        grid=(B, G, S // tq),
        in_specs=[
            pl.BlockSpec((1, tq, gw), lambda b, g, i: (b, i, g)),
            pl.BlockSpec((1, tq, LANES), lambda b, g, i: (b, i, 0)),
            pl.BlockSpec((1, 1, dh, S), lambda b, g, i: (b, g, 0, 0)),
            pl.BlockSpec((1, 1, S, dh), lambda b, g, i: (b, g, 0, 0)),
        ],
        out_specs=pl.BlockSpec((1, tq, gw), lambda b, g, i: (b, i, g)),
        compiler_params=_cparams(("parallel", "parallel", "parallel"), 40),
        name="nsa_window",
    )(q, gates, kwt, vw)


def _oproj_kernel(h_ref, a_ref, b_ref, c_ref, w_ref, o_ref):
    o = a_ref[...].astype(F32) + b_ref[...].astype(F32) + c_ref[...].astype(F32)
    o_ref[...] = h_ref[...] + _dot(o.astype(BF16), w_ref[...])


def _out_proj(h2, oc, os_, ow, w_o):
    T, D = h2.shape
    HD = oc.shape[-1]
    tt = min(SEQ_TILE, T)
    blk = lambda w: pl.BlockSpec((tt, w), lambda i: (i, 0))
    return pl.pallas_call(
        _oproj_kernel,
        out_shape=jax.ShapeDtypeStruct((T, D), F32),
        grid=(T // tt,),
        in_specs=[blk(D), blk(HD), blk(HD), blk(HD), pl.BlockSpec((HD, D), lambda i: (0, 0))],
        out_specs=blk(D),
        compiler_params=_cparams(("parallel",), 40),
        name="nsa_out_proj",
    )(h2, oc, os_, ow, w_o.astype(BF16))


def _nsa_layer(h, g, w_qg, b_gate, w_o, shared):
    B, S, D = h.shape
    kct, vc, kst, vs, kwt, vw = shared
    q, gates = _q_proj(h, g, w_qg, b_gate)
    oc, bias = _cmp_attn(q, gates, kct, vc)
    os_ = _sel_attn(q, bias, gates, kst, vs)
    ow = _win_attn(q, gates, kwt, vw)
    flat = lambda a: a.reshape(B * S, a.shape[-1])
    return _out_proj(flat(h), flat(oc), flat(os_), flat(ow), w_o).reshape(B, S, D)


def _shared_kv(h, kv_norm, w_kv, ck, cv):
    B, S, _ = h.shape
    G, dh = N_KV_GROUPS, HEAD_DIM
    assert S % SEL_BLOCK == 0 and S // SEL_BLOCK <= MAX_SEL_BLOCKS
    cvals, kst, vs, kwt, vw = _shared_kv_proj(h, kv_norm, w_kv)
    halves = cvals.reshape(B, S, 2, G, dh).transpose(2, 0, 3, 1, 4).reshape(2, B, G, S // CMP_STRIDE, CMP_STRIDE * dh)
    kct = _compress(halves, 0, *ck, keys=True)
    vc = _compress(halves, 1, *cv, keys=False)
    return kct, vc, kst, vs, kwt, vw


def kernel(x, p, norm_mix, norm_ffn, norm_ple, pool_w, pool_b, pool_scale, kv_norm, w_kv, cmp_k_pos, cmp_k_w1, cmp_k_b1, cmp_k_w2, cmp_k_b2, cmp_v_pos, cmp_v_w1, cmp_v_b1, cmp_v_w2, cmp_v_b2, w_qg, b_gate, w_o, router_g_w, router_g_b, router_e_w, router_e_b, moe_w1, moe_w3, moe_w2, ple_proj, ple_gate_w, ple_gate_b, final_norm):
    B, S, D = x.shape
    depth = p.shape[0]
    n_a = pool_w.shape[0]
    T = B * S
    h = x
    shared = None
    for i in range(depth):
        if i == n_a:
            shared = _shared_kv(h, kv_norm, w_kv,
                                (cmp_k_pos, cmp_k_w1, cmp_k_b1, cmp_k_w2, cmp_k_b2),
                                (cmp_v_pos, cmp_v_w1, cmp_v_b1, cmp_v_w2, cmp_v_b2))
        if i < n_a:
            h = _pool_layer(h, norm_mix[i], pool_w[i], pool_b[i], pool_scale[i])
        else:
            j = i - n_a
            h = _nsa_layer(h, norm_mix[i], w_qg[j], b_gate[j], w_o[j], shared)
        h = _moe_ple_layer(h.reshape(T, D), p[i].reshape(T, p.shape[-1]), norm_ffn[i], router_g_w[i], router_g_b[i],
                           router_e_w[i], router_e_b[i], moe_w1[i], moe_w3[i], moe_w2[i], norm_ple[i],
                           ple_gate_w[i], ple_gate_b[i], ple_proj[i], final_norm, i == depth - 1).reshape(B, S, D)
    return h
```

```python
import functools

import jax
import jax.numpy as jnp
from jax import lax
from jax.experimental import pallas as pl
from jax.experimental.pallas import tpu as pltpu

F32 = jnp.float32
BF16 = jnp.bfloat16
I32 = jnp.int32

POOL_WINDOWS = (2, 4, 8, 16)
N_HEADS = 16
HEAD_DIM = 64
N_KV_GROUPS = 4
HEADS_PER_GROUP = N_HEADS // N_KV_GROUPS
N_BRANCH = 3
ROPE_DIMS = HEAD_DIM // 4
ROPE_HALF = ROPE_DIMS // 2
ROPE_THETA = 500000.0
CMP_BLOCK = 32
CMP_STRIDE = 16
SEL_BLOCK = 64
N_SELECT = 16
WINDOW = 512
FORCE_BONUS = 1e4
N_EXPERT_GROUPS = 4
EXPERTS_PER_GROUP = 8
N_EXPERTS = N_EXPERT_GROUPS * EXPERTS_PER_GROUP
TOP_K_IN_GROUP = 2
RMS_EPS = 1e-6

LANES = 128
MAX_SEL_BLOCKS = LANES
MASK_BIAS = -30000.0

SEQ_TILE = 512
TOK_TILE = 512
CMB_TILE = 256
MOE_CHUNK = 512
Q_TILE = 128
KEY_TILE = 512
HALO = 16

HIGHEST = lax.Precision.HIGHEST


def _cparams(sem, vmem_mb):
    return pltpu.CompilerParams(dimension_semantics=sem, vmem_limit_bytes=vmem_mb * 1024 * 1024)


def _rms(x, g):
    return x * lax.rsqrt(jnp.mean(x * x, axis=-1, keepdims=True) + RMS_EPS) * g


def _dot(a, b):
    return jnp.dot(a, b, preferred_element_type=F32)


def _dot_nt(a, b):
    return lax.dot_general(a, b, (((1,), (1,)), ((), ())), preferred_element_type=F32)


def _pool_kernel(h_ref, halo_ref, g_ref, w_ref, b_ref, sc_ref, o_ref, *, ts, cg):
    i = pl.program_id(1)
    x = h_ref[0]
    g = g_ref[...]
    xn = _rms(x, g)
    hn = _rms(halo_ref[0], g)
    hn = jnp.where(i > 0, hn, 0.0)
    ext = jnp.concatenate([hn, xn], axis=0)
    t = i * ts + lax.broadcasted_iota(I32, (ts, 1), 0)
    outs = []
    for gi, w in enumerate(POOL_WINDOWS):
        s = ext[:, gi * cg:(gi + 1) * cg]
        k = 1
        while k < w:
            s = s + pltpu.roll(s, k, axis=0)
            k *= 2
        cnt = jnp.minimum(t + 1, w).astype(F32)
        pooled = s[HALO:] / cnt - xn[:, gi * cg:(gi + 1) * cg]
        outs.append(_dot(pooled.astype(BF16), w_ref[gi]))
    y = jnp.concatenate(outs, axis=-1)
    o_ref[0] = x + (y + b_ref[...]) * sc_ref[...]


def _pool_layer(h, g, w, b, sc):
    B, S, D = h.shape
    ts = min(SEQ_TILE, S)
    cg = D // len(POOL_WINDOWS)
    row = lambda v: v.reshape(1, D)
    return pl.pallas_call(
        functools.partial(_pool_kernel, ts=ts, cg=cg),
        out_shape=jax.ShapeDtypeStruct((B, S, D), F32),
        grid=(B, S // ts),
        in_specs=[
            pl.BlockSpec((1, ts, D), lambda b_, i: (b_, i, 0)),
            pl.BlockSpec((1, HALO, D), lambda b_, i: (b_, jnp.maximum(i * (ts // HALO) - 1, 0), 0)),
            pl.BlockSpec((1, D), lambda b_, i: (0, 0)),
            pl.BlockSpec((len(POOL_WINDOWS), cg, cg), lambda b_, i: (0, 0, 0)),
            pl.BlockSpec((1, D), lambda b_, i: (0, 0)),
            pl.BlockSpec((1, D), lambda b_, i: (0, 0)),
        ],
        out_specs=pl.BlockSpec((1, ts, D), lambda b_, i: (b_, i, 0)),
        compiler_params=_cparams(("parallel", "parallel"), 40),
        name="pool_mixer",
    )(h, h, row(g), w.astype(BF16), row(b), row(sc))


def _router_kernel(h_ref, g_ref, w_ref, b_ref, info_ref, cnt_ref, *, tt):
    i = pl.program_id(0)

    @pl.when(i == 0)
    def _():
        cnt_ref[...] = jnp.zeros_like(cnt_ref)

    xn = _rms(h_ref[...], g_ref[...])
    logits = jnp.dot(xn, w_ref[...], precision=HIGHEST, preferred_element_type=F32) + b_ref[...]
    lane = lax.broadcasted_iota(I32, (tt, LANES), 1)
    neg = -jnp.inf
    gl = jnp.where(lane < N_EXPERT_GROUPS, logits, neg)
    gmax = jnp.max(gl, axis=-1, keepdims=True)
    grp = jnp.min(jnp.where(gl == gmax, lane, LANES), axis=-1, keepdims=True)
    gprob = 1.0 / jnp.sum(jnp.exp(gl - gmax), axis=-1, keepdims=True)
    lo = N_EXPERT_GROUPS + grp * EXPERTS_PER_GROUP
    el = jnp.where(lane >= lo, jnp.where(lane < lo + EXPERTS_PER_GROUP, logits, neg), neg)
    v1 = jnp.max(el, axis=-1, keepdims=True)
    i1 = jnp.min(jnp.where(el == v1, lane, LANES), axis=-1, keepdims=True)
    el2 = jnp.where(lane == i1, neg, el)
    v2 = jnp.max(el2, axis=-1, keepdims=True)
    i2 = jnp.min(jnp.where(el2 == v2, lane, LANES), axis=-1, keepdims=True)
    e2 = jnp.exp(v2 - v1)
    w1 = gprob / (1.0 + e2)
    w2 = gprob * e2 / (1.0 + e2)
    oh1 = lane == i1
    oh2 = lane == i2
    oh = jnp.where(oh1, 1.0, jnp.where(oh2, 1.0, 0.0))
    r_ = lax.broadcasted_iota(I32, (tt, tt), 0)
    c_ = lax.broadcasted_iota(I32, (tt, tt), 1)
    tri = jnp.where(r_ > c_, 1.0, 0.0).astype(BF16)
    tot = _dot(tri, oh.astype(BF16)) + cnt_ref[...]
    r1 = jnp.sum(jnp.where(oh1, tot, 0.0), axis=-1, keepdims=True)
    r2 = jnp.sum(jnp.where(oh2, tot, 0.0), axis=-1, keepdims=True)
    cnt_ref[...] = cnt_ref[...] + jnp.sum(oh, axis=0, keepdims=True)
    vals = (i1.astype(F32) - N_EXPERT_GROUPS, i2.astype(F32) - N_EXPERT_GROUPS, w1, w2, r1, r2)
    info = jnp.zeros((tt, LANES), F32)
    for k, v in enumerate(vals):
        info = jnp.where(lane == k, v, info)
    info_ref[...] = info


def _router(h2, g, wr, br):
    T, D = h2.shape
    tt = min(TOK_TILE, T)
    return pl.pallas_call(
        functools.partial(_router_kernel, tt=tt),
        out_shape=(jax.ShapeDtypeStruct((T, LANES), F32), jax.ShapeDtypeStruct((1, LANES), F32)),
        grid=(T // tt,),
        in_specs=[
            pl.BlockSpec((tt, D), lambda i: (i, 0)),
            pl.BlockSpec((1, D), lambda i: (0, 0)),
            pl.BlockSpec((D, LANES), lambda i: (0, 0)),
            pl.BlockSpec((1, LANES), lambda i: (0, 0)),
        ],
        out_specs=(pl.BlockSpec((tt, LANES), lambda i: (i, 0)), pl.BlockSpec((1, LANES), lambda i: (0, 0))),
        compiler_params=_cparams(("arbitrary",), 40),
        name="moe_router",
    )(h2, g.reshape(1, D), wr, br)


def _row_copy(src, s, dst, d, sem):
    return pltpu.make_async_copy(src.at[pl.ds(s, 1)], dst.at[pl.ds(d, 1)], sem)


def _dispatch_kernel(dest_ref, h_ref, g_ref, xs_in_ref, xs_ref, xn_s, sem, *, tt, n_tok):
    del xs_in_ref
    base = pl.program_id(0) * tt
    xn_s[...] = _rms(h_ref[...], g_ref[...])

    def issue(r, c):
        for k in range(TOP_K_IN_GROUP):
            _row_copy(xn_s, r, xs_ref, dest_ref[k * n_tok + base + r], sem).start()
        return c

    lax.fori_loop(0, tt, issue, 0)

    def drain(r, c):
        for k in range(TOP_K_IN_GROUP):
            _row_copy(xn_s, r, xs_ref, dest_ref[k * n_tok + base + r], sem).wait()
        return c

    lax.fori_loop(0, tt, drain, 0)


def _dispatch(dest, h2, g, n_rows):
    T, D = h2.shape
    tt = min(TOK_TILE, T)
    xs0 = jnp.zeros((n_rows, D), F32)
    return pl.pallas_call(
        functools.partial(_dispatch_kernel, tt=tt, n_tok=T),
        out_shape=jax.ShapeDtypeStruct((n_rows, D), F32),
        grid_spec=pltpu.PrefetchScalarGridSpec(
            num_scalar_prefetch=1,
            grid=(T // tt,),
            in_specs=[
                pl.BlockSpec((tt, D), lambda i, d: (i, 0)),
                pl.BlockSpec((1, D), lambda i, d: (0, 0)),
                pl.BlockSpec(memory_space=pl.ANY),
            ],
            out_specs=pl.BlockSpec(memory_space=pl.ANY),
            scratch_shapes=[pltpu.VMEM((tt, D), F32), pltpu.SemaphoreType.DMA],
        ),
        input_output_aliases={3: 0},
        compiler_params=_cparams(("arbitrary",), 40),
        name="moe_dispatch",
    )(dest, h2, g.reshape(1, D), xs0)


def _expert_kernel(ce_ref, nv_ref, xs_ref, w1_ref, w3_ref, w2_ref, o_ref, w1b, w3b, w2b):
    c = pl.program_id(0)
    e = ce_ref[c]
    prev = ce_ref[jnp.maximum(c - 1, 0)]

    @pl.when(jnp.logical_or(c == 0, e != prev))
    def _():
        w1b[...] = w1_ref[0].astype(BF16)
        w3b[...] = w3_ref[0].astype(BF16)
        w2b[...] = w2_ref[0].astype(BF16)

    @pl.when(c < nv_ref[0])
    def _():
        x = xs_ref[...].astype(BF16)
        a = _dot(x, w1b[...])
        b = _dot(x, w3b[...])
        hc = a * jax.nn.sigmoid(a) * b
        o_ref[...] = _dot(hc.astype(BF16), w2b[...])

    @pl.when(c >= nv_ref[0])
    def _():
        o_ref[...] = jnp.zeros_like(o_ref)


def _experts(chunk_e, n_valid, xs, w1, w3, w2):
    P, D = xs.shape
    F = w1.shape[-1]
    ch = MOE_CHUNK
    return pl.pallas_call(
        _expert_kernel,
        out_shape=jax.ShapeDtypeStruct((P, D), F32),
        grid_spec=pltpu.PrefetchScalarGridSpec(
            num_scalar_prefetch=2,
            grid=(P // ch,),
            in_specs=[
                pl.BlockSpec((ch, D), lambda c, ce, nv: (c, 0)),
                pl.BlockSpec((1, D, F), lambda c, ce, nv: (ce[c], 0, 0)),
                pl.BlockSpec((1, D, F), lambda c, ce, nv: (ce[c], 0, 0)),
                pl.BlockSpec((1, F, D), lambda c, ce, nv: (ce[c], 0, 0)),
            ],
            out_specs=pl.BlockSpec((ch, D), lambda c, ce, nv: (c, 0)),
            scratch_shapes=[pltpu.VMEM((D, F), BF16), pltpu.VMEM((D, F), BF16), pltpu.VMEM((F, D), BF16)],
        ),
        compiler_params=_cparams(("arbitrary",), 56),
        name="moe_experts",
    )(chunk_e, n_valid, xs, w1, w3, w2)


def _combine_kernel(dest_ref, h_ref, info_ref, rows_ref, p_ref, g_ref, gw_ref, gb_ref, pw_ref, fn_ref,
                    o_ref, buf, sem, *, tt, n_tok, final):
    base = pl.program_id(0) * tt

    def issue(r, c):
        for k in range(TOP_K_IN_GROUP):
            _row_copy(rows_ref, dest_ref[k * n_tok + base + r], buf.at[k], r, sem).start()
        return c

    lax.fori_loop(0, tt, issue, 0)

    def drain(r, c):
        for k in range(TOP_K_IN_GROUP):
            _row_copy(rows_ref, dest_ref[k * n_tok + base + r], buf.at[k], r, sem).wait()
        return c

    lax.fori_loop(0, tt, drain, 0)
    info = info_ref[...]
    y = h_ref[...] + info[:, 2:3] * buf[0] + info[:, 3:4] * buf[1]
    hn = _rms(y, g_ref[...])
    gate = jax.nn.sigmoid(_dot(hn.astype(BF16), gw_ref[...]) + gb_ref[...])
    out = y + _dot(p_ref[...].astype(BF16), pw_ref[...]) * gate
    if final:
        out = _rms(out, fn_ref[...])
    o_ref[...] = out


def _combine(dest, h2, info, rows, p2, g, gw, gb, pw, fn, final):
    T, D = h2.shape
    PD = p2.shape[-1]
    tt = min(CMB_TILE, T)
    full = lambda i, d: (0, 0)
    return pl.pallas_call(
        functools.partial(_combine_kernel, tt=tt, n_tok=T, final=final),
        out_shape=jax.ShapeDtypeStruct((T, D), F32),
        grid_spec=pltpu.PrefetchScalarGridSpec(
            num_scalar_prefetch=1,
            grid=(T // tt,),
            in_specs=[
                pl.BlockSpec((tt, D), lambda i, d: (i, 0)),
                pl.BlockSpec((tt, LANES), lambda i, d: (i, 0)),
                pl.BlockSpec(memory_space=pl.ANY),
                pl.BlockSpec((tt, PD), lambda i, d: (i, 0)),
                pl.BlockSpec((1, D), full),
                pl.BlockSpec((D, D), full),
                pl.BlockSpec((1, D), full),
                pl.BlockSpec((PD, D), full),
                pl.BlockSpec((1, D), full),
            ],
            out_specs=pl.BlockSpec((tt, D), lambda i, d: (i, 0)),
            scratch_shapes=[pltpu.VMEM((TOP_K_IN_GROUP, tt, D), F32), pltpu.SemaphoreType.DMA],
        ),
        compiler_params=_cparams(("arbitrary",), 40),
        name="moe_combine_ple",
    )(dest, h2, info, rows, p2, g.reshape(1, D), gw.astype(BF16), gb.reshape(1, D), pw.astype(BF16),
      fn.reshape(1, D))


def _moe_ple_layer(h2, p2, norm_ffn, rg_w, rg_b, re_w, re_b, w1, w3, w2, norm_ple, gate_w, gate_b, ple_proj,
                   final_norm, final):
    T, D = h2.shape
    A = T * TOP_K_IN_GROUP
    pad = LANES - N_EXPERT_GROUPS - N_EXPERTS
    wr = jnp.concatenate([rg_w, re_w, jnp.zeros((D, pad), F32)], axis=1)
    br = jnp.concatenate([rg_b, re_b, jnp.zeros((pad,), F32)]).reshape(1, LANES)
    info, cnt = _router(h2, norm_ffn, wr, br)
    counts = cnt[0, N_EXPERT_GROUPS:N_EXPERT_GROUPS + N_EXPERTS].astype(I32)
    n_chunks_e = (counts + MOE_CHUNK - 1) // MOE_CHUNK
    chunk_end = jnp.cumsum(n_chunks_e)
    pstarts = (chunk_end - n_chunks_e) * MOE_CHUNK
    n_chunks = -(-A // MOE_CHUNK) + N_EXPERTS
    chunk_e = jnp.searchsorted(chunk_end, jnp.arange(n_chunks, dtype=I32), side="right").astype(I32)
    n_valid = chunk_end[-1:].astype(I32)
    chunk_e = jnp.minimum(chunk_e, chunk_e[jnp.maximum(n_valid[0] - 1, 0)])
    e_idx = info[:, 0:TOP_K_IN_GROUP].astype(I32)
    rank = info[:, 4:4 + TOP_K_IN_GROUP].astype(I32)
    dest = (pstarts[e_idx] + rank).T.reshape(A)
    xs = _dispatch(dest, h2, norm_ffn, n_chunks * MOE_CHUNK)
    rows = _experts(chunk_e, n_valid, xs, w1, w3, w2)
    return _combine(dest, h2, info, rows, p2, norm_ple, gate_w, gate_b, ple_proj, final_norm, final)


def _rope_rows(xt, cos, sin):
    x1 = xt[0:ROPE_HALF]
    x2 = xt[ROPE_HALF:ROPE_DIMS]
    return jnp.concatenate([x1 * cos - x2 * sin, x2 * cos + x1 * sin, xt[ROPE_DIMS:]], axis=0)


def _kv_kernel(h_ref, g_ref, wn_ref, wt_ref, cos_ref, sin_ref, cv_ref, kst_ref, vs_ref, kwt_ref, vw_ref, *, ts):
    i = pl.program_id(1)
    kvw = N_KV_GROUPS * HEAD_DIM
    hn = _rms(h_ref[0], g_ref[...]).astype(BF16)
    nat = _dot(hn, wn_ref[...])
    cv_ref[0] = nat[:, 0:2 * kvw]
    tr = _dot_nt(wt_ref[...], hn)
    cos = cos_ref[...]
    sin = sin_ref[...]
    blk = lax.broadcasted_iota(I32, (MAX_SEL_BLOCKS, ts), 0)
    pos = i * ts + lax.broadcasted_iota(I32, (MAX_SEL_BLOCKS, ts), 1)
    onehot = jnp.where(pos // SEL_BLOCK == blk, 1.0, 0.0).astype(BF16)
    for g in range(N_KV_GROUPS):
        c0 = g * HEAD_DIM
        kst_ref[0, g, 0:MAX_SEL_BLOCKS, :] = onehot
        kst_ref[0, g, MAX_SEL_BLOCKS:, :] = _rope_rows(tr[c0:c0 + HEAD_DIM], cos, sin).astype(BF16)
        kwt_ref[0, g] = _rope_rows(tr[kvw + c0:kvw + c0 + HEAD_DIM], cos, sin).astype(BF16)
        vs_ref[0, g] = nat[:, 2 * kvw + c0:2 * kvw + c0 + HEAD_DIM].astype(BF16)
        vw_ref[0, g] = nat[:, 3 * kvw + c0:3 * kvw + c0 + HEAD_DIM].astype(BF16)


def _rope_tables(pos):
    inv = jnp.float32(ROPE_THETA) ** (-jnp.arange(ROPE_HALF, dtype=F32) * 2.0 / ROPE_DIMS)
    ang = pos.astype(F32)[:, None] * inv[None, :]
    return jnp.cos(ang), jnp.sin(ang)


def _shared_kv_proj(h, kv_norm, w_kv):
    B, S, D = h.shape
    G, dh = N_KV_GROUPS, HEAD_DIM
    kvw = G * dh
    ts = min(SEQ_TILE, S)
    br = lambda k: w_kv[:, k * kvw:(k + 1) * kvw]
    w_nat = jnp.concatenate([br(0), br(1), br(3), br(5)], axis=1).astype(BF16)
    w_tr = jnp.concatenate([br(2), br(4)], axis=1).T.astype(BF16)
    cos, sin = _rope_tables(jnp.arange(S))
    kd = MAX_SEL_BLOCKS + dh
    return pl.pallas_call(
        functools.partial(_kv_kernel, ts=ts),
        out_shape=(
            jax.ShapeDtypeStruct((B, S, 2 * kvw), F32),
            jax.ShapeDtypeStruct((B, G, kd, S), BF16),
            jax.ShapeDtypeStruct((B, G, S, dh), BF16),
            jax.ShapeDtypeStruct((B, G, dh, S), BF16),
            jax.ShapeDtypeStruct((B, G, S, dh), BF16),
        ),
        grid=(B, S // ts),
        in_specs=[
            pl.BlockSpec((1, ts, D), lambda b, i: (b, i, 0)),
            pl.BlockSpec((1, D), lambda b, i: (0, 0)),
            pl.BlockSpec((D, 4 * kvw), lambda b, i: (0, 0)),
            pl.BlockSpec((2 * kvw, D), lambda b, i: (0, 0)),
            pl.BlockSpec((ROPE_HALF, ts), lambda b, i: (0, i)),
            pl.BlockSpec((ROPE_HALF, ts), lambda b, i: (0, i)),
        ],
        out_specs=(
            pl.BlockSpec((1, ts, 2 * kvw), lambda b, i: (b, i, 0)),
            pl.BlockSpec((1, G, kd, ts), lambda b, i: (b, 0, 0, i)),
            pl.BlockSpec((1, G, ts, dh), lambda b, i: (b, 0, i, 0)),
            pl.BlockSpec((1, G, dh, ts), lambda b, i: (b, 0, 0, i)),
            pl.BlockSpec((1, G, ts, dh), lambda b, i: (b, 0, i, 0)),
        ),
        compiler_params=_cparams(("parallel", "parallel"), 48),
        name="shared_kv_proj",
    )(h, kv_norm.reshape(1, D), w_nat, w_tr, cos.T, sin.T)


def _compress_kernel(x_ref, pos_ref, w1_ref, b1_ref, w2_ref, b2_ref, cos_ref, sin_ref, o_ref, *, nh, keys):
    x = x_ref[0, 0, 0]
    a = _dot((x + pos_ref[0:1]).astype(BF16), w1_ref[0])
    b = _dot((x + pos_ref[1:2]).astype(BF16), w1_ref[1])
    hid = jax.nn.gelu(a + pltpu.roll(b, nh - 1, axis=0) + b1_ref[...]).astype(BF16)
    if keys:
        out = _dot_nt(w2_ref[...], hid) + b2_ref[...]
        o_ref[0, 0] = _rope_rows(out, cos_ref[...], sin_ref[...]).astype(BF16)
    else:
        o_ref[0, 0] = (_dot(hid, w2_ref[...]) + b2_ref[...]).astype(BF16)


def _compress(halves, which, pos_emb, w1, b1, w2, b2, keys):
    _, B, G, nh, hw = halves.shape
    dh = HEAD_DIM
    hidden = w1.shape[-1]
    pos2 = pos_emb.reshape(2, hw)
    w1s = w1.reshape(2, hw, hidden).astype(BF16)
    cos, sin = _rope_tables(jnp.arange(nh) * CMP_STRIDE + CMP_BLOCK - 1)
    if keys:
        w2a, b2a = w2.T.astype(BF16), b2.reshape(dh, 1)
        out_shape, out_block, out_idx = (B, G, dh, nh), (1, 1, dh, nh), lambda b, g: (b, g, 0, 0)
    else:
        w2a, b2a = w2.astype(BF16), b2.reshape(1, dh)
        out_shape, out_block, out_idx = (B, G, nh, dh), (1, 1, nh, dh), lambda b, g: (b, g, 0, 0)
    c2 = lambda b, g: (0, 0)
    return pl.pallas_call(
        functools.partial(_compress_kernel, nh=nh, keys=keys),
        out_shape=jax.ShapeDtypeStruct(out_shape, BF16),
        grid=(B, G),
        in_specs=[
            pl.BlockSpec((1, 1, 1, nh, hw), lambda b, g: (which, b, g, 0, 0)),
            pl.BlockSpec((2, hw), c2),
            pl.BlockSpec((2, hw, hidden), lambda b, g: (0, 0, 0)),
            pl.BlockSpec((1, hidden), c2),
            pl.BlockSpec(w2a.shape, c2),
            pl.BlockSpec(b2a.shape, c2),
            pl.BlockSpec((ROPE_HALF, nh), c2),
            pl.BlockSpec((ROPE_HALF, nh), c2),
        ],
        out_specs=pl.BlockSpec(out_block, out_idx),
        compiler_params=_cparams(("parallel", "parallel"), 40),
        name="compress_k" if keys else "compress_v",
    )(halves, pos2, w1s, b1.reshape(1, hidden), w2a, b2a, cos.T, sin.T)


def _qproj_kernel(h_ref, g_ref, wq_ref, wg_ref, bg_ref, c_ref, s1_ref, s2_ref, q_ref, gt_ref):
    xn = _rms(h_ref[0], g_ref[...]).astype(BF16)
    q = _dot(xn, wq_ref[...])
    c, s1, s2 = c_ref[...], s1_ref[...], s2_ref[...]
    for k in range(q.shape[-1] // LANES):
        x = q[:, k * LANES:(k + 1) * LANES]
        r = x * c + pltpu.roll(x, ROPE_HALF, axis=1) * s1 + pltpu.roll(x, LANES - ROPE_HALF, axis=1) * s2
        q_ref[0, :, k * LANES:(k + 1) * LANES] = r.astype(BF16)
    gt_ref[0] = jax.nn.sigmoid(_dot(xn, wg_ref[...]) + bg_ref[...])


def _q_proj(h, g, w_qg, b_gate):
    B, S, D = h.shape
    HD = N_HEADS * HEAD_DIM
    ng = N_HEADS * N_BRANCH
    ts = min(SEQ_TILE, S)
    wq = w_qg[:, :HD].astype(BF16)
    wg = jnp.pad(w_qg[:, HD:], ((0, 0), (0, LANES - ng))).astype(BF16)
    bg = jnp.pad(b_gate, (0, LANES - ng)).reshape(1, LANES)
    cos, sin = _rope_tables(jnp.arange(S))
    scale = HEAD_DIM ** -0.5
    ones = jnp.ones((S, HEAD_DIM - ROPE_DIMS), F32)
    zeros = jnp.zeros((S, HEAD_DIM - ROPE_DIMS), F32)
    zh = jnp.zeros((S, ROPE_HALF), F32)
    two = lambda a: jnp.concatenate([a, a], axis=1) * scale
    c = two(jnp.concatenate([cos, cos, ones], axis=1))
    s1 = two(jnp.concatenate([zh, sin, zeros], axis=1))
    s2 = two(jnp.concatenate([-sin, zh, zeros], axis=1))
    tab = pl.BlockSpec((ts, LANES), lambda b, i: (i, 0))
    return pl.pallas_call(
        _qproj_kernel,
        out_shape=(jax.ShapeDtypeStruct((B, S, HD), BF16), jax.ShapeDtypeStruct((B, S, LANES), F32)),
        grid=(B, S // ts),
        in_specs=[
            pl.BlockSpec((1, ts, D), lambda b, i: (b, i, 0)),
            pl.BlockSpec((1, D), lambda b, i: (0, 0)),
            pl.BlockSpec((D, HD), lambda b, i: (0, 0)),
            pl.BlockSpec((D, LANES), lambda b, i: (0, 0)),
            pl.BlockSpec((1, LANES), lambda b, i: (0, 0)),
            tab, tab, tab,
        ],
        out_specs=(pl.BlockSpec((1, ts, HD), lambda b, i: (b, i, 0)),
                   pl.BlockSpec((1, ts, LANES), lambda b, i: (b, i, 0))),
        compiler_params=_cparams(("parallel", "parallel"), 40),
        name="nsa_q_proj",
    )(h, g.reshape(1, D), wq, wg, bg, c, s1, s2)


def _stack_heads(qb):
    return jnp.concatenate([qb[:, r * HEAD_DIM:(r + 1) * HEAD_DIM] for r in range(HEADS_PER_GROUP)], axis=0)


def _gated_unstack(o, gates, g, branch, tq):
    lane = lax.broadcasted_iota(I32, gates.shape, 1)
    cols = []
    for r in range(HEADS_PER_GROUP):
        col = (g * HEADS_PER_GROUP + r) * N_BRANCH + branch
        gate = jnp.sum(jnp.where(lane == col, gates, 0.0), axis=-1, keepdims=True)
        cols.append(o[r * tq:(r + 1) * tq] * gate)
    return jnp.concatenate(cols, axis=-1)


def _query_pos(i, tq):
    rows = lax.broadcasted_iota(I32, (HEADS_PER_GROUP * tq, 1), 0)
    return i * tq + (rows & (tq - 1))


def _cmp_kernel(q_ref, gt_ref, k_ref, v_ref, o_ref, bias_ref, *, tq, nc):
    g = pl.program_id(1)
    i = pl.program_id(2)
    qs = _stack_heads(q_ref[0])
    s = _dot(qs, k_ref[0, 0])
    t = _query_pos(i, tq)
    n = lax.broadcasted_iota(I32, (1, nc), 1)
    s = jnp.where(n * CMP_STRIDE + (CMP_BLOCK - 1) <= t, s, -jnp.inf)
    m = jnp.max(s, axis=-1, keepdims=True)
    m = jnp.where(m == -jnp.inf, 0.0, m)
    e = jnp.exp(s - m)
    p = e / jnp.maximum(jnp.sum(e, axis=-1, keepdims=True), 1e-30)
    o = _dot(p.astype(BF16), v_ref[0, 0])
    o_ref[0] = _gated_unstack(o, gt_ref[0], g, 0, tq).astype(BF16)
    ps = p[0:tq]
    for r in range(1, HEADS_PER_GROUP):
        ps = ps + p[r * tq:(r + 1) * tq]
    nn = lax.broadcasted_iota(I32, (nc, MAX_SEL_BLOCKS), 0) * CMP_STRIDE
    jj = lax.broadcasted_iota(I32, (nc, MAX_SEL_BLOCKS), 1) * SEL_BLOCK
    c2s = jnp.where(nn < jj + SEL_BLOCK, jnp.where(nn + CMP_BLOCK > jj, 1.0, 0.0), 0.0)
    imp = jnp.dot(ps, c2s, precision=HIGHEST, preferred_element_type=F32)
    tq_pos = i * tq + lax.broadcasted_iota(I32, (tq, 1), 0)
    cur = tq_pos // SEL_BLOCK
    j = lax.broadcasted_iota(I32, (tq, MAX_SEL_BLOCKS), 1)
    valid = j <= cur
    bonus = jnp.where(j == 0, FORCE_BONUS, jnp.where(j == cur, FORCE_BONUS, jnp.where(j == cur - 1, FORCE_BONUS, 0.0)))
    work = jnp.where(valid, imp + bonus, -jnp.inf)
    sel = jnp.zeros((tq, MAX_SEL_BLOCKS), F32)
    for _ in range(N_SELECT):
        mx = jnp.max(work, axis=-1, keepdims=True)
        idx = jnp.min(jnp.where(work == mx, j, MAX_SEL_BLOCKS), axis=-1, keepdims=True)
        pick = j == idx
        sel = jnp.where(pick, 1.0, sel)
        work = jnp.where(pick, -jnp.inf, work)
    bias_ref[0, 0] = jnp.where(valid, jnp.where(sel > 0.0, 0.0, MASK_BIAS), MASK_BIAS).astype(BF16)


def _cmp_attn(q, gates, kct, vc):
    B, S, HD = q.shape
    G, dh = N_KV_GROUPS, HEAD_DIM
    gw = HD // G
    nc = kct.shape[-1]
    tq = min(Q_TILE, S)
    return pl.pallas_call(
        functools.partial(_cmp_kernel, tq=tq, nc=nc),
        out_shape=(jax.ShapeDtypeStruct((B, S, HD), BF16), jax.ShapeDtypeStruct((B, G, S, MAX_SEL_BLOCKS), BF16)),
        grid=(B, G, S // tq),
        in_specs=[
            pl.BlockSpec((1, tq, gw), lambda b, g, i: (b, i, g)),
            pl.BlockSpec((1, tq, LANES), lambda b, g, i: (b, i, 0)),
            pl.BlockSpec((1, 1, dh, nc), lambda b, g, i: (b, g, 0, 0)),
            pl.BlockSpec((1, 1, nc, dh), lambda b, g, i: (b, g, 0, 0)),
        ],
        out_specs=(pl.BlockSpec((1, tq, gw), lambda b, g, i: (b, i, g)),
                   pl.BlockSpec((1, 1, tq, MAX_SEL_BLOCKS), lambda b, g, i: (b, g, i, 0))),
        compiler_params=_cparams(("parallel", "parallel", "parallel"), 40),
        name="nsa_compressed",
    )(q, gates, kct, vc)


def _sel_kernel(q_ref, bias_ref, gt_ref, k_ref, v_ref, o_ref, qa, m_s, l_s, acc, *, tq, tk):
    g = pl.program_id(1)
    i = pl.program_id(2)
    qb = q_ref[0]
    bias = bias_ref[0, 0]
    for r in range(HEADS_PER_GROUP):
        qa[r * tq:(r + 1) * tq, 0:MAX_SEL_BLOCKS] = bias
        qa[r * tq:(r + 1) * tq, MAX_SEL_BLOCKS:] = qb[:, r * HEAD_DIM:(r + 1) * HEAD_DIM]
    m_s[...] = jnp.full_like(m_s, -jnp.inf)
    l_s[...] = jnp.zeros_like(l_s)
    acc[...] = jnp.zeros_like(acc)
    t = _query_pos(i, tq)
    lane = lax.broadcasted_iota(I32, (1, tk), 1)

    def body(jt, c):
        k0 = pl.multiple_of(jt * tk, tk)
        s = _dot(qa[...], k_ref[0, 0, :, pl.ds(k0, tk)])
        s = jnp.where(k0 + lane <= t, s, -jnp.inf)
        m_old = m_s[...]
        m_new = jnp.maximum(m_old, jnp.max(s, axis=-1, keepdims=True))
        alpha = jnp.exp(m_old - m_new)
        p = jnp.exp(s - m_new)
        l_s[...] = alpha * l_s[...] + jnp.sum(p, axis=-1, keepdims=True)
        acc[...] = alpha * acc[...] + _dot(p.astype(BF16), v_ref[0, 0, pl.ds(k0, tk), :])
        m_s[...] = m_new
        return c

    lax.fori_loop(0, (i * tq + tq + tk - 1) // tk, body, 0)
    o_ref[0] = _gated_unstack(acc[...] / l_s[...], gt_ref[0], g, 1, tq).astype(BF16)


def _sel_attn(q, bias, gates, kst, vs):
    B, S, HD = q.shape
    G, dh = N_KV_GROUPS, HEAD_DIM
    gw = HD // G
    kd = kst.shape[2]
    tq = min(Q_TILE, S)
    tk = min(KEY_TILE, S)
    rows = HEADS_PER_GROUP * tq
    return pl.pallas_call(
        functools.partial(_sel_kernel, tq=tq, tk=tk),
        out_shape=jax.ShapeDtypeStruct((B, S, HD), BF16),
        grid=(B, G, S // tq),
        in_specs=[
            pl.BlockSpec((1, tq, gw), lambda b, g, i: (b, i, g)),
            pl.BlockSpec((1, 1, tq, MAX_SEL_BLOCKS), lambda b, g, i: (b, g, i, 0)),
            pl.BlockSpec((1, tq, LANES), lambda b, g, i: (b, i, 0)),
            pl.BlockSpec((1, 1, kd, S), lambda b, g, i: (b, g, 0, 0)),
            pl.BlockSpec((1, 1, S, dh), lambda b, g, i: (b, g, 0, 0)),
        ],
        out_specs=pl.BlockSpec((1, tq, gw), lambda b, g, i: (b, i, g)),
        scratch_shapes=[pltpu.VMEM((rows, kd), BF16), pltpu.VMEM((rows, 1), F32), pltpu.VMEM((rows, 1), F32),
                        pltpu.VMEM((rows, dh), F32)],
        compiler_params=_cparams(("parallel", "parallel", "arbitrary"), 48),
        name="nsa_selected",
    )(q, bias, gates, kst, vs)


def _win_kernel(q_ref, gt_ref, k_ref, v_ref, o_ref, *, tq, wb):
    g = pl.program_id(1)
    i = pl.program_id(2)
    k0 = pl.multiple_of(jnp.maximum(i * tq - WINDOW, 0), tq)
    qs = _stack_heads(q_ref[0])
    s = _dot(qs, k_ref[0, 0, :, pl.ds(k0, wb)])
    t = _query_pos(i, tq)
    kpos = k0 + lax.broadcasted_iota(I32, (1, wb), 1)
    s = jnp.where(kpos <= t, jnp.where(kpos > t - WINDOW, s, -jnp.inf), -jnp.inf)
    m = jnp.max(s, axis=-1, keepdims=True)
    e = jnp.exp(s - m)
    p = e / jnp.sum(e, axis=-1, keepdims=True)
    o = _dot(p.astype(BF16), v_ref[0, 0, pl.ds(k0, wb), :])
    o_ref[0] = _gated_unstack(o, gt_ref[0], g, 2, tq).astype(BF16)


def _win_attn(q, gates, kwt, vw):
    B, S, HD = q.shape
    G, dh = N_KV_GROUPS, HEAD_DIM
    gw = HD // G
    tq = min(Q_TILE, S)
    wb = WINDOW + tq
    assert S >= wb and tq % LANES == 0
    return pl.pallas_call(
        functools.partial(_win_kernel, tq=tq, wb=wb),
        out_shape=jax.ShapeDtypeStruct((B, S, HD), BF16),
        grid=(B, G, S // tq),
        in_specs=[
            pl.BlockSpec((1, tq, gw), lambda b, g, i: (b, i, g)),
            pl.BlockSpec((1, tq, LANES), lambda b, g, i: (b, i, 0)),
            pl.BlockSpec((1, 1, dh, S), lambda b, g, i: (b, g, 0, 0)),
            pl.BlockSpec((1, 1, S, dh), lambda b, g, i: (b, g, 0, 0)),
        ],
        out_specs=pl.BlockSpec((1, tq, gw), lambda b, g, i: (b, i, g)),
        compiler_params=_cparams(("parallel", "parallel", "parallel"), 40),
        name="nsa_window",
    )(q, gates, kwt, vw)


LOG2E = 1.4426950408889634
Q_SCALE = HEAD_DIM ** -0.5 * LOG2E
V_ROWS = HEAD_DIM + 16
KEY_AUG = 2 * LANES


def _lane_rope_tables(pos):
    cos, sin = _rope_tables(pos)
    n = pos.shape[0]
    ones = jnp.ones((n, HEAD_DIM - ROPE_DIMS), F32)
    zeros = jnp.zeros((n, HEAD_DIM - ROPE_DIMS), F32)
    zh = jnp.zeros((n, ROPE_HALF), F32)
    two = lambda a: jnp.concatenate([a, a], axis=1)
    return (two(jnp.concatenate([cos, cos, ones], axis=1)), two(jnp.concatenate([zh, sin, zeros], axis=1)),
            two(jnp.concatenate([-sin, zh, zeros], axis=1)))


def _rope_lanes(x, c, s1, s2):
    return x * c + pltpu.roll(x, ROPE_HALF, axis=1) * s1 + pltpu.roll(x, LANES - ROPE_HALF, axis=1) * s2


def _kv2_kernel(h_ref, g_ref, wn_ref, wt_ref, c_ref, s1_ref, s2_ref, cv_ref, ksa_ref, vst_ref, kw_ref, vwt_ref, *, ts):
    i = pl.program_id(1)
    kvw = N_KV_GROUPS * HEAD_DIM
    hn = _rms(h_ref[0], g_ref[...]).astype(BF16)
    nat = _dot(hn, wn_ref[...])
    cv_ref[0] = nat[:, 0:2 * kvw]
    tr = _dot_nt(wt_ref[...], hn)
    c, s1, s2 = c_ref[...], s1_ref[...], s2_ref[...]
    roped = [_rope_lanes(nat[:, 2 * kvw + k * LANES:2 * kvw + (k + 1) * LANES], c, s1, s2)
             for k in range(2 * kvw // LANES)]
    lane = lax.broadcasted_iota(I32, (ts, KEY_AUG), 1)
    pos = i * ts + lax.broadcasted_iota(I32, (ts, KEY_AUG), 0)
    onehot = jnp.where(lane - HEAD_DIM == pos // SEL_BLOCK, 1.0, 0.0).astype(BF16)
    ones_row = jnp.where(lax.broadcasted_iota(I32, (V_ROWS - HEAD_DIM, ts), 0) == 0, 1.0, 0.0).astype(BF16)
    per_tile = LANES // HEAD_DIM
    for g in range(N_KV_GROUPS):
        lo = (g % per_tile) * HEAD_DIM
        ksa_ref[0, g] = onehot
        ksa_ref[0, g, :, 0:HEAD_DIM] = roped[g // per_tile][:, lo:lo + HEAD_DIM].astype(BF16)
        kw_ref[0, g] = roped[N_KV_GROUPS // per_tile + g // per_tile][:, lo:lo + HEAD_DIM].astype(BF16)
        for ref, base in ((vst_ref, 0), (vwt_ref, kvw)):
            ref[0, g, 0:HEAD_DIM, :] = tr[base + g * HEAD_DIM:base + (g + 1) * HEAD_DIM].astype(BF16)
            ref[0, g, HEAD_DIM:, :] = ones_row


def _shared_kv_proj2(h, kv_norm, w_kv):
    B, S, D = h.shape
    G, dh = N_KV_GROUPS, HEAD_DIM
    kvw = G * dh
    ts = min(SEQ_TILE, S)
    br = lambda k: w_kv[:, k * kvw:(k + 1) * kvw]
    w_nat = jnp.concatenate([br(0), br(1), br(2), br(4)], axis=1).astype(BF16)
    w_tr = jnp.concatenate([br(3), br(5)], axis=1).T.astype(BF16)
    tabs = _lane_rope_tables(jnp.arange(S))
    tab = pl.BlockSpec((ts, LANES), lambda b, i: (i, 0))
    return pl.pallas_call(
        functools.partial(_kv2_kernel, ts=ts),
        out_shape=(
            jax.ShapeDtypeStruct((B, S, 2 * kvw), F32),
            jax.ShapeDtypeStruct((B, G, S, KEY_AUG), BF16),
            jax.ShapeDtypeStruct((B, G, V_ROWS, S), BF16),
            jax.ShapeDtypeStruct((B, G, S, dh), BF16),
            jax.ShapeDtypeStruct((B, G, V_ROWS, S), BF16),
        ),
        grid=(B, S // ts),
        in_specs=[
            pl.BlockSpec((1, ts, D), lambda b, i: (b, i, 0)),
            pl.BlockSpec((1, D), lambda b, i: (0, 0)),
            pl.BlockSpec((D, 4 * kvw), lambda b, i: (0, 0)),
            pl.BlockSpec((2 * kvw, D), lambda b, i: (0, 0)),
            tab, tab, tab,
        ],
        out_specs=(
            pl.BlockSpec((1, ts, 2 * kvw), lambda b, i: (b, i, 0)),
            pl.BlockSpec((1, G, ts, KEY_AUG), lambda b, i: (b, 0, i, 0)),
            pl.BlockSpec((1, G, V_ROWS, ts), lambda b, i: (b, 0, 0, i)),
            pl.BlockSpec((1, G, ts, dh), lambda b, i: (b, 0, i, 0)),
            pl.BlockSpec((1, G, V_ROWS, ts), lambda b, i: (b, 0, 0, i)),
        ),
        compiler_params=_cparams(("parallel", "parallel"), 48),
        name="shared_kv_proj",
    )(h, kv_norm.reshape(1, D), w_nat, w_tr, *tabs)


def _compress2_kernel(x_ref, pos_ref, w1_ref, b1_ref, w2_ref, b2_ref, c_ref, s1_ref, s2_ref, o_ref, *, nh, keys):
    x = x_ref[0, 0, 0]
    a = _dot((x + pos_ref[0:1]).astype(BF16), w1_ref[0])
    b = _dot((x + pos_ref[1:2]).astype(BF16), w1_ref[1])
    hid = jax.nn.gelu(a + pltpu.roll(b, nh - 1, axis=0) + b1_ref[...]).astype(BF16)
    if keys:
        out = _rope_lanes(_dot(hid, w2_ref[...]) + b2_ref[...], c_ref[...], s1_ref[...], s2_ref[...])
        o_ref[0, 0] = out[:, 0:HEAD_DIM].astype(BF16)
    else:
        o_ref[0, 0] = (_dot_nt(w2_ref[...], hid) + b2_ref[...]).astype(BF16)


def _compress2(halves, which, pos_emb, w1, b1, w2, b2, keys):
    _, B, G, nh, hw = halves.shape
    dh = HEAD_DIM
    hidden = w1.shape[-1]
    pos2 = pos_emb.reshape(2, hw)
    w1s = w1.reshape(2, hw, hidden).astype(BF16)
    tabs = _lane_rope_tables(jnp.arange(nh) * CMP_STRIDE + CMP_BLOCK - 1)
    if keys:
        w2a = jnp.pad(w2, ((0, 0), (0, LANES - dh))).astype(BF16)
        b2a = jnp.pad(b2, (0, LANES - dh)).reshape(1, LANES)
        out_shape, out_block = (B, G, nh, dh), (1, 1, nh, dh)
    else:
        w2a, b2a = w2.T.astype(BF16), b2.reshape(dh, 1)
        out_shape, out_block = (B, G, dh, nh), (1, 1, dh, nh)
    c2 = lambda b, g: (0, 0)
    tab = pl.BlockSpec((nh, LANES), c2)
    return pl.pallas_call(
        functools.partial(_compress2_kernel, nh=nh, keys=keys),
        out_shape=jax.ShapeDtypeStruct(out_shape, BF16),
        grid=(B, G),
        in_specs=[
            pl.BlockSpec((1, 1, 1, nh, hw), lambda b, g: (which, b, g, 0, 0)),
            pl.BlockSpec((2, hw), c2),
            pl.BlockSpec((2, hw, hidden), lambda b, g: (0, 0, 0)),
            pl.BlockSpec((1, hidden), c2),
            pl.BlockSpec(w2a.shape, c2),
            pl.BlockSpec(b2a.shape, c2),
            tab, tab, tab,
        ],
        out_specs=pl.BlockSpec(out_block, lambda b, g: (b, g, 0, 0)),
        compiler_params=_cparams(("parallel", "parallel"), 40),
        name="compress_k" if keys else "compress_v",
    )(halves, pos2, w1s, b1.reshape(1, hidden), w2a, b2a, *tabs)


def _qproj2_kernel(h_ref, g_ref, wqt_ref, wg_ref, bg_ref, cos_ref, sin_ref, qt_ref, gt_ref):
    xn = _rms(h_ref[0], g_ref[...]).astype(BF16)
    tr = _dot_nt(wqt_ref[...], xn)
    cos, sin = cos_ref[...], sin_ref[...]
    for hd in range(N_HEADS):
        rows = slice(hd * HEAD_DIM, (hd + 1) * HEAD_DIM)
        qt_ref[0, rows, :] = (_rope_rows(tr[rows], cos, sin) * Q_SCALE).astype(BF16)
    gt_ref[0] = jax.nn.sigmoid(_dot(xn, wg_ref[...]) + bg_ref[...])


def _q_proj2(h, g, w_qg, b_gate):
    B, S, D = h.shape
    HD = N_HEADS * HEAD_DIM
    ng = N_HEADS * N_BRANCH
    ts = min(SEQ_TILE, S)
    wqt = w_qg[:, :HD].T.astype(BF16)
    wg = jnp.pad(w_qg[:, HD:], ((0, 0), (0, LANES - ng))).astype(BF16)
    bg = jnp.pad(b_gate, (0, LANES - ng)).reshape(1, LANES)
    cos, sin = _rope_tables(jnp.arange(S))
    tab = pl.BlockSpec((ROPE_HALF, ts), lambda b, i: (0, i))
    return pl.pallas_call(
        _qproj2_kernel,
        out_shape=(jax.ShapeDtypeStruct((B, HD, S), BF16), jax.ShapeDtypeStruct((B, S, LANES), F32)),
        grid=(B, S // ts),
        in_specs=[
            pl.BlockSpec((1, ts, D), lambda b, i: (b, i, 0)),
            pl.BlockSpec((1, D), lambda b, i: (0, 0)),
            pl.BlockSpec((HD, D), lambda b, i: (0, 0)),
            pl.BlockSpec((D, LANES), lambda b, i: (0, 0)),
            pl.BlockSpec((1, LANES), lambda b, i: (0, 0)),
            tab, tab,
        ],
        out_specs=(pl.BlockSpec((1, HD, ts), lambda b, i: (b, 0, i)),
                   pl.BlockSpec((1, ts, LANES), lambda b, i: (b, i, 0))),
        compiler_params=_cparams(("parallel", "parallel"), 40),
        name="nsa_q_proj",
    )(h, g.reshape(1, D), wqt, wg, bg, cos.T, sin.T)


def _heads_on_lanes(qt):
    return jnp.concatenate([qt[r * HEAD_DIM:(r + 1) * HEAD_DIM] for r in range(HEADS_PER_GROUP)], axis=1)


def _lane_query_pos(i, tq):
    lanes = lax.broadcasted_iota(I32, (1, HEADS_PER_GROUP * tq), 1)
    return i * tq + (lanes & (tq - 1))


def _finish_heads(acc_t, gates, g, branch, tq, denom_row):
    lane = lax.broadcasted_iota(I32, gates.shape, 1)
    rows = acc_t.shape[0]
    cols = []
    for r in range(HEADS_PER_GROUP):
        blk = acc_t[:, r * tq:(r + 1) * tq]
        nat = jnp.concatenate([blk, jnp.zeros((tq - rows, tq), F32)], axis=0).T
        col = (g * HEADS_PER_GROUP + r) * N_BRANCH + branch
        scale = jnp.sum(jnp.where(lane == col, gates, 0.0), axis=-1, keepdims=True)
        if denom_row is not None:
            scale = scale / nat[:, denom_row:denom_row + 1]
        cols.append(nat[:, 0:HEAD_DIM] * scale)
    return jnp.concatenate(cols, axis=-1)


def _cmp2_kernel(qt_ref, gt_ref, k_ref, vt_ref, c2s_ref, o_ref, bias_ref, *, tq, nc):
    g = pl.program_id(1)
    i = pl.program_id(2)
    s = _dot(k_ref[0, 0], _heads_on_lanes(qt_ref[0]))
    t = _lane_query_pos(i, tq)
    n = lax.broadcasted_iota(I32, (nc, 1), 0)
    s = jnp.where(n * CMP_STRIDE + (CMP_BLOCK - 1) <= t, s, -jnp.inf)
    m = jnp.max(s, axis=0, keepdims=True)
    m = jnp.where(m == -jnp.inf, 0.0, m)
    e = jnp.exp2(s - m)
    p = e * (1.0 / jnp.maximum(jnp.sum(e, axis=0, keepdims=True), 1e-30))
    ot = _dot(vt_ref[0, 0], p.astype(BF16))
    o_ref[0] = _finish_heads(ot, gt_ref[0], g, 0, tq, None).astype(BF16)
    ps = p[:, 0:tq]
    for r in range(1, HEADS_PER_GROUP):
        ps = ps + p[:, r * tq:(r + 1) * tq]
    hi = ps.astype(BF16)
    rem = ps - hi.astype(F32)
    mid = rem.astype(BF16)
    lo = (rem - mid.astype(F32)).astype(BF16)
    c2s = c2s_ref[...]
    imp = _dot(c2s, hi) + _dot(c2s, mid) + _dot(c2s, lo)
    cur = (i * tq + lax.broadcasted_iota(I32, (1, tq), 1)) // SEL_BLOCK
    j = lax.broadcasted_iota(I32, (MAX_SEL_BLOCKS, tq), 0)
    valid = j <= cur
    bonus = jnp.where(j == 0, FORCE_BONUS, jnp.where(j == cur, FORCE_BONUS, jnp.where(j == cur - 1, FORCE_BONUS, 0.0)))
    work = jnp.where(valid, imp + bonus, -jnp.inf)
    sel = jnp.zeros((MAX_SEL_BLOCKS, tq), F32)
    for _ in range(N_SELECT):
        mx = jnp.max(work, axis=0, keepdims=True)
        idx = jnp.min(jnp.where(work == mx, j, MAX_SEL_BLOCKS), axis=0, keepdims=True)
        pick = j == idx
        sel = jnp.where(pick, 1.0, sel)
        work = jnp.where(pick, -jnp.inf, work)
    bias_ref[0, 0] = jnp.where(valid, jnp.where(sel > 0.0, 0.0, MASK_BIAS), MASK_BIAS).astype(BF16)


def _cmp_attn2(qt, gates, kc, vct):
    B, HD, S = qt.shape
    G, dh = N_KV_GROUPS, HEAD_DIM
    gw = HD // G
    nc = kc.shape[2]
    tq = min(Q_TILE, S)
    n0 = jnp.arange(nc)[None, :] * CMP_STRIDE
    j0 = jnp.arange(MAX_SEL_BLOCKS)[:, None] * SEL_BLOCK
    c2s = ((n0 < j0 + SEL_BLOCK) & (n0 + CMP_BLOCK > j0)).astype(BF16)
    return pl.pallas_call(
        functools.partial(_cmp2_kernel, tq=tq, nc=nc),
        out_shape=(jax.ShapeDtypeStruct((B, S, HD), BF16), jax.ShapeDtypeStruct((B, G, MAX_SEL_BLOCKS, S), BF16)),
        grid=(B, G, S // tq),
        in_specs=[
            pl.BlockSpec((1, gw, tq), lambda b, g, i: (b, g, i)),
            pl.BlockSpec((1, tq, LANES), lambda b, g, i: (b, i, 0)),
            pl.BlockSpec((1, 1, nc, dh), lambda b, g, i: (b, g, 0, 0)),
            pl.BlockSpec((1, 1, dh, nc), lambda b, g, i: (b, g, 0, 0)),
            pl.BlockSpec((MAX_SEL_BLOCKS, nc), lambda b, g, i: (0, 0)),
        ],
        out_specs=(pl.BlockSpec((1, tq, gw), lambda b, g, i: (b, i, g)),
                   pl.BlockSpec((1, 1, MAX_SEL_BLOCKS, tq), lambda b, g, i: (b, g, 0, i))),
        compiler_params=_cparams(("parallel", "parallel", "parallel"), 40),
        name="nsa_compressed",
    )(qt, gates, kc, vct, c2s)


def _sel2_kernel(qt_ref, bias_ref, gt_ref, k_ref, vt_ref, o_ref, qa, m_s, acc, *, tq, tk):
    g = pl.program_id(1)
    i = pl.program_id(2)
    qt = qt_ref[0]
    bias = bias_ref[0, 0]
    for r in range(HEADS_PER_GROUP):
        cols = slice(r * tq, (r + 1) * tq)
        qa[0:HEAD_DIM, cols] = qt[r * HEAD_DIM:(r + 1) * HEAD_DIM]
        qa[HEAD_DIM:HEAD_DIM + MAX_SEL_BLOCKS, cols] = bias
        qa[HEAD_DIM + MAX_SEL_BLOCKS:, cols] = jnp.zeros((KEY_AUG - HEAD_DIM - MAX_SEL_BLOCKS, tq), BF16)
    m_s[...] = jnp.full_like(m_s, -jnp.inf)
    acc[...] = jnp.zeros_like(acc)
    t = _lane_query_pos(i, tq)

    def step(k0, diagonal):
        s = _dot(k_ref[0, 0, pl.ds(k0, tk), :], qa[...])
        if diagonal:
            s = jnp.where(k0 + lax.broadcasted_iota(I32, (tk, 1), 0) <= t, s, -jnp.inf)
        m_old = m_s[...]
        m_new = jnp.maximum(m_old, jnp.max(s, axis=0, keepdims=True))
        p = jnp.exp2(s - m_new).astype(BF16)
        acc[...] = jnp.exp2(m_old - m_new) * acc[...] + _dot(vt_ref[0, 0, :, pl.ds(k0, tk)], p)
        m_s[...] = m_new

    def full_tile(jt, c):
        step(pl.multiple_of(jt * tk, tk), False)
        return c

    last = (i * tq + tq + tk - 1) // tk - 1
    lax.fori_loop(0, last, full_tile, 0)
    step(pl.multiple_of(last * tk, tk), True)
    o_ref[0] = _finish_heads(acc[...], gt_ref[0], g, 1, tq, HEAD_DIM).astype(BF16)


def _sel_attn2(qt, bias, gates, ksa, vst):
    B, HD, S = qt.shape
    G = N_KV_GROUPS
    gw = HD // G
    tq = min(Q_TILE, S)
    tk = min(KEY_TILE, S)
    width = HEADS_PER_GROUP * tq
    return pl.pallas_call(
        functools.partial(_sel2_kernel, tq=tq, tk=tk),
        out_shape=jax.ShapeDtypeStruct((B, S, HD), BF16),
        grid=(B, G, S // tq),
        in_specs=[
            pl.BlockSpec((1, gw, tq), lambda b, g, i: (b, g, i)),
            pl.BlockSpec((1, 1, MAX_SEL_BLOCKS, tq), lambda b, g, i: (b, g, 0, i)),
            pl.BlockSpec((1, tq, LANES), lambda b, g, i: (b, i, 0)),
            pl.BlockSpec((1, 1, S, KEY_AUG), lambda b, g, i: (b, g, 0, 0)),
            pl.BlockSpec((1, 1, V_ROWS, S), lambda b, g, i: (b, g, 0, 0)),
        ],
        out_specs=pl.BlockSpec((1, tq, gw), lambda b, g, i: (b, i, g)),
        scratch_shapes=[pltpu.VMEM((KEY_AUG, width), BF16), pltpu.VMEM((1, width), F32),
                        pltpu.VMEM((V_ROWS, width), F32)],
        compiler_params=_cparams(("parallel", "parallel", "arbitrary"), 48),
        name="nsa_selected",
    )(qt, bias, gates, ksa, vst)


def _win2_kernel(qt_ref, gt_ref, k_ref, vt_ref, o_ref, *, tq, wb):
    g = pl.program_id(1)
    i = pl.program_id(2)
    k0 = pl.multiple_of(jnp.maximum(i * tq - WINDOW, 0), tq)
    s = _dot(k_ref[0, 0, pl.ds(k0, wb), :], _heads_on_lanes(qt_ref[0]))
    t = _lane_query_pos(i, tq)
    kpos = k0 + lax.broadcasted_iota(I32, (wb, 1), 0)
    s = jnp.where(kpos <= t, jnp.where(kpos > t - WINDOW, s, -jnp.inf), -jnp.inf)
    m = jnp.max(s, axis=0, keepdims=True)
    p = jnp.exp2(s - m).astype(BF16)
    acc = _dot(vt_ref[0, 0, :, pl.ds(k0, wb)], p)
    o_ref[0] = _finish_heads(acc, gt_ref[0], g, 2, tq, HEAD_DIM).astype(BF16)


def _win_attn2(qt, gates, kw, vwt):
    B, HD, S = qt.shape
    G, dh = N_KV_GROUPS, HEAD_DIM
    gw = HD // G
    tq = min(Q_TILE, S)
    wb = WINDOW + tq
    assert S >= wb and tq % LANES == 0
    return pl.pallas_call(
        functools.partial(_win2_kernel, tq=tq, wb=wb),
        out_shape=jax.ShapeDtypeStruct((B, S, HD), BF16),
        grid=(B, G, S // tq),
        in_specs=[
            pl.BlockSpec((1, gw, tq), lambda b, g, i: (b, g, i)),
            pl.BlockSpec((1, tq, LANES), lambda b, g, i: (b, i, 0)),
            pl.BlockSpec((1, 1, S, dh), lambda b, g, i: (b, g, 0, 0)),
            pl.BlockSpec((1, 1, V_ROWS, S), lambda b, g, i: (b, g, 0, 0)),
        ],
        out_specs=pl.BlockSpec((1, tq, gw), lambda b, g, i: (b, i, g)),
        compiler_params=_cparams(("parallel", "parallel", "parallel"), 40),
        name="nsa_window",
    )(qt, gates, kw, vwt)


def _oproj_kernel(h_ref, a_ref, b_ref, c_ref, w_ref, o_ref):
    o = a_ref[...].astype(F32) + b_ref[...].astype(F32) + c_ref[...].astype(F32)
    o_ref[...] = h_ref[...] + _dot(o.astype(BF16), w_ref[...])


def _out_proj(h2, oc, os_, ow, w_o):
    T, D = h2.shape
    HD = oc.shape[-1]
    tt = min(SEQ_TILE, T)
    blk = lambda w: pl.BlockSpec((tt, w), lambda i: (i, 0))
    return pl.pallas_call(
        _oproj_kernel,
        out_shape=jax.ShapeDtypeStruct((T, D), F32),
        grid=(T // tt,),
        in_specs=[blk(D), blk(HD), blk(HD), blk(HD), pl.BlockSpec((HD, D), lambda i: (0, 0))],
        out_specs=blk(D),
        compiler_params=_cparams(("parallel",), 40),
        name="nsa_out_proj",
    )(h2, oc, os_, ow, w_o.astype(BF16))


def _nsa_layer(h, g, w_qg, b_gate, w_o, shared):
    B, S, D = h.shape
    kc, vct, ksa, vst, kw, vwt = shared
    qt, gates = _q_proj2(h, g, w_qg, b_gate)
    oc, bias = _cmp_attn2(qt, gates, kc, vct)
    os_ = _sel_attn2(qt, bias, gates, ksa, vst)
    ow = _win_attn2(qt, gates, kw, vwt)
    flat = lambda a: a.reshape(B * S, a.shape[-1])
    return _out_proj(flat(h), flat(oc), flat(os_), flat(ow), w_o).reshape(B, S, D)


def _shared_kv(h, kv_norm, w_kv, ck, cv):
    B, S, _ = h.shape
    G, dh = N_KV_GROUPS, HEAD_DIM
    assert S % SEL_BLOCK == 0 and S // SEL_BLOCK <= MAX_SEL_BLOCKS
    cvals, ksa, vst, kw, vwt = _shared_kv_proj2(h, kv_norm, w_kv)
    halves = cvals.reshape(B, S, 2, G, dh).transpose(2, 0, 3, 1, 4).reshape(2, B, G, S // CMP_STRIDE, CMP_STRIDE * dh)
    kc = _compress2(halves, 0, *ck, keys=True)
    vct = _compress2(halves, 1, *cv, keys=False)
    return kc, vct, ksa, vst, kw, vwt


def kernel(x, p, norm_mix, norm_ffn, norm_ple, pool_w, pool_b, pool_scale, kv_norm, w_kv, cmp_k_pos, cmp_k_w1, cmp_k_b1, cmp_k_w2, cmp_k_b2, cmp_v_pos, cmp_v_w1, cmp_v_b1, cmp_v_w2, cmp_v_b2, w_qg, b_gate, w_o, router_g_w, router_g_b, router_e_w, router_e_b, moe_w1, moe_w3, moe_w2, ple_proj, ple_gate_w, ple_gate_b, final_norm):
    B, S, D = x.shape
    depth = p.shape[0]
    n_a = pool_w.shape[0]
    T = B * S
    h = x
    shared = None
    for i in range(depth):
        if i == n_a:
            shared = _shared_kv(h, kv_norm, w_kv,
                                (cmp_k_pos, cmp_k_w1, cmp_k_b1, cmp_k_w2, cmp_k_b2),
                                (cmp_v_pos, cmp_v_w1, cmp_v_b1, cmp_v_w2, cmp_v_b2))
        if i < n_a:
            h = _pool_layer(h, norm_mix[i], pool_w[i], pool_b[i], pool_scale[i])
        else:
            j = i - n_a
            h = _nsa_layer(h, norm_mix[i], w_qg[j], b_gate[j], w_o[j], shared)
        h = _moe_ple_layer(h.reshape(T, D), p[i].reshape(T, p.shape[-1]), norm_ffn[i], router_g_w[i], router_g_b[i],
                           router_e_w[i], router_e_b[i], moe_w1[i], moe_w3[i], moe_w2[i], norm_ple[i],
                           ple_gate_w[i], ple_gate_b[i], ple_proj[i], final_norm, i == depth - 1).reshape(B, S, D)
    return h
```

```python
import functools

import jax
import jax.numpy as jnp
from jax import lax
from jax.experimental import pallas as pl
from jax.experimental.pallas import tpu as pltpu

F32 = jnp.float32
BF16 = jnp.bfloat16
I32 = jnp.int32

POOL_WINDOWS = (2, 4, 8, 16)
N_HEADS = 16
HEAD_DIM = 64
N_KV_GROUPS = 4
HEADS_PER_GROUP = N_HEADS // N_KV_GROUPS
N_BRANCH = 3
ROPE_DIMS = HEAD_DIM // 4
ROPE_HALF = ROPE_DIMS // 2
ROPE_THETA = 500000.0
CMP_BLOCK = 32
CMP_STRIDE = 16
SEL_BLOCK = 64
N_SELECT = 16
WINDOW = 512
FORCE_BONUS = 1e4
N_EXPERT_GROUPS = 4
EXPERTS_PER_GROUP = 8
N_EXPERTS = N_EXPERT_GROUPS * EXPERTS_PER_GROUP
TOP_K_IN_GROUP = 2
RMS_EPS = 1e-6

LANES = 128
MAX_SEL_BLOCKS = LANES
MASK_BIAS = -30000.0

SEQ_TILE = 512
TOK_TILE = 512
CMB_TILE = 256
MOE_CHUNK = 512
Q_TILE = 128
KEY_TILE = 512
HALO = 16

HIGHEST = lax.Precision.HIGHEST


def _cparams(sem, vmem_mb):
    return pltpu.CompilerParams(dimension_semantics=sem, vmem_limit_bytes=vmem_mb * 1024 * 1024)


def _rms(x, g):
    return x * lax.rsqrt(jnp.mean(x * x, axis=-1, keepdims=True) + RMS_EPS) * g


def _dot(a, b):
    return jnp.dot(a, b, preferred_element_type=F32)


def _dot_nt(a, b):
    return lax.dot_general(a, b, (((1,), (1,)), ((), ())), preferred_element_type=F32)


def _pool_kernel(h_ref, halo_ref, g_ref, w_ref, b_ref, sc_ref, o_ref, *, ts, cg):
    i = pl.program_id(1)
    x = h_ref[0]
    g = g_ref[...]
    xn = _rms(x, g)
    hn = _rms(halo_ref[0], g)
    hn = jnp.where(i > 0, hn, 0.0)
    ext = jnp.concatenate([hn, xn], axis=0)
    t = i * ts + lax.broadcasted_iota(I32, (ts, 1), 0)
    outs = []
    for gi, w in enumerate(POOL_WINDOWS):
        s = ext[:, gi * cg:(gi + 1) * cg]
        k = 1
        while k < w:
            s = s + pltpu.roll(s, k, axis=0)
            k *= 2
        cnt = jnp.minimum(t + 1, w).astype(F32)
        pooled = s[HALO:] / cnt - xn[:, gi * cg:(gi + 1) * cg]
        outs.append(_dot(pooled.astype(BF16), w_ref[gi]))
    y = jnp.concatenate(outs, axis=-1)
    o_ref[0] = x + (y + b_ref[...]) * sc_ref[...]


def _pool_layer(h, g, w, b, sc):
    B, S, D = h.shape
    ts = min(SEQ_TILE, S)
    cg = D // len(POOL_WINDOWS)
    row = lambda v: v.reshape(1, D)
    return pl.pallas_call(
        functools.partial(_pool_kernel, ts=ts, cg=cg),
        out_shape=jax.ShapeDtypeStruct((B, S, D), F32),
        grid=(B, S // ts),
        in_specs=[
            pl.BlockSpec((1, ts, D), lambda b_, i: (b_, i, 0)),
            pl.BlockSpec((1, HALO, D), lambda b_, i: (b_, jnp.maximum(i * (ts // HALO) - 1, 0), 0)),
            pl.BlockSpec((1, D), lambda b_, i: (0, 0)),
            pl.BlockSpec((len(POOL_WINDOWS), cg, cg), lambda b_, i: (0, 0, 0)),
            pl.BlockSpec((1, D), lambda b_, i: (0, 0)),
            pl.BlockSpec((1, D), lambda b_, i: (0, 0)),
        ],
        out_specs=pl.BlockSpec((1, ts, D), lambda b_, i: (b_, i, 0)),
        compiler_params=_cparams(("parallel", "parallel"), 40),
        name="pool_mixer",
    )(h, h, row(g), w.astype(BF16), row(b), row(sc))


def _router_kernel(h_ref, g_ref, w_ref, b_ref, info_ref, cnt_ref, *, tt):
    i = pl.program_id(0)

    @pl.when(i == 0)
    def _():
        cnt_ref[...] = jnp.zeros_like(cnt_ref)

    xn = _rms(h_ref[...], g_ref[...])
    logits = jnp.dot(xn, w_ref[...], precision=HIGHEST, preferred_element_type=F32) + b_ref[...]
    lane = lax.broadcasted_iota(I32, (tt, LANES), 1)
    neg = -jnp.inf
    gl = jnp.where(lane < N_EXPERT_GROUPS, logits, neg)
    gmax = jnp.max(gl, axis=-1, keepdims=True)
    grp = jnp.min(jnp.where(gl == gmax, lane, LANES), axis=-1, keepdims=True)
    gprob = 1.0 / jnp.sum(jnp.exp(gl - gmax), axis=-1, keepdims=True)
    lo = N_EXPERT_GROUPS + grp * EXPERTS_PER_GROUP
    el = jnp.where(lane >= lo, jnp.where(lane < lo + EXPERTS_PER_GROUP, logits, neg), neg)
    v1 = jnp.max(el, axis=-1, keepdims=True)
    i1 = jnp.min(jnp.where(el == v1, lane, LANES), axis=-1, keepdims=True)
    el2 = jnp.where(lane == i1, neg, el)
    v2 = jnp.max(el2, axis=-1, keepdims=True)
    i2 = jnp.min(jnp.where(el2 == v2, lane, LANES), axis=-1, keepdims=True)
    e2 = jnp.exp(v2 - v1)
    w1 = gprob / (1.0 + e2)
    w2 = gprob * e2 / (1.0 + e2)
    oh1 = lane == i1
    oh2 = lane == i2
    oh = jnp.where(oh1, 1.0, jnp.where(oh2, 1.0, 0.0))
    r_ = lax.broadcasted_iota(I32, (tt, tt), 0)
    c_ = lax.broadcasted_iota(I32, (tt, tt), 1)
    tri = jnp.where(r_ > c_, 1.0, 0.0).astype(BF16)
    tot = _dot(tri, oh.astype(BF16)) + cnt_ref[...]
    r1 = jnp.sum(jnp.where(oh1, tot, 0.0), axis=-1, keepdims=True)
    r2 = jnp.sum(jnp.where(oh2, tot, 0.0), axis=-1, keepdims=True)
    cnt_ref[...] = cnt_ref[...] + jnp.sum(oh, axis=0, keepdims=True)
    vals = (i1.astype(F32) - N_EXPERT_GROUPS, i2.astype(F32) - N_EXPERT_GROUPS, w1, w2, r1, r2)
    info = jnp.zeros((tt, LANES), F32)
    for k, v in enumerate(vals):
        info = jnp.where(lane == k, v, info)
    info_ref[...] = info


def _router(h2, g, wr, br):
    T, D = h2.shape
    tt = min(TOK_TILE, T)
    return pl.pallas_call(
        functools.partial(_router_kernel, tt=tt),
        out_shape=(jax.ShapeDtypeStruct((T, LANES), F32), jax.ShapeDtypeStruct((1, LANES), F32)),
        grid=(T // tt,),
        in_specs=[
            pl.BlockSpec((tt, D), lambda i: (i, 0)),
            pl.BlockSpec((1, D), lambda i: (0, 0)),
            pl.BlockSpec((D, LANES), lambda i: (0, 0)),
            pl.BlockSpec((1, LANES), lambda i: (0, 0)),
        ],
        out_specs=(pl.BlockSpec((tt, LANES), lambda i: (i, 0)), pl.BlockSpec((1, LANES), lambda i: (0, 0))),
        compiler_params=_cparams(("arbitrary",), 40),
        name="moe_router",
    )(h2, g.reshape(1, D), wr, br)


def _row_copy(src, s, dst, d, sem):
    return pltpu.make_async_copy(src.at[pl.ds(s, 1)], dst.at[pl.ds(d, 1)], sem)


def _dispatch_kernel(dest_ref, cend_ref, h_ref, g_ref, xs_ref, xn_s, sem, *, tt, n_tok, n_chunks):
    base = pl.program_id(0) * tt

    @pl.when(pl.program_id(0) == 0)
    def _():
        xn_s[...] = jnp.zeros_like(xn_s)

        def tail(e):
            nonempty = cend_ref[e] > (cend_ref[e - 1] if e > 0 else 0)
            row = pl.multiple_of((cend_ref[e] - 1) * tt, tt)
            return nonempty, pltpu.make_async_copy(xn_s, xs_ref.at[pl.ds(row, tt)], sem)

        def unused(c):
            return pltpu.make_async_copy(xn_s, xs_ref.at[pl.ds(pl.multiple_of(c * tt, tt), tt)], sem)

        n_used = cend_ref[N_EXPERTS - 1]
        for e in range(N_EXPERTS):
            nonempty, cp = tail(e)
            pl.when(nonempty)(cp.start)
        lax.fori_loop(n_used, n_chunks, lambda c, z: (unused(c).start(), z)[1], 0)
        for e in range(N_EXPERTS):
            nonempty, cp = tail(e)
            pl.when(nonempty)(cp.wait)
        lax.fori_loop(n_used, n_chunks, lambda c, z: (unused(c).wait(), z)[1], 0)

    xn_s[...] = _rms(h_ref[...], g_ref[...])

    def issue(r, c):
        for k in range(TOP_K_IN_GROUP):
            _row_copy(xn_s, r, xs_ref, dest_ref[k * n_tok + base + r], sem).start()
        return c

    lax.fori_loop(0, tt, issue, 0)

    def drain(r, c):
        for k in range(TOP_K_IN_GROUP):
            _row_copy(xn_s, r, xs_ref, dest_ref[k * n_tok + base + r], sem).wait()
        return c

    lax.fori_loop(0, tt, drain, 0)


def _dispatch(dest, chunk_end, h2, g, n_rows):
    T, D = h2.shape
    tt = MOE_CHUNK
    assert T % tt == 0
    return pl.pallas_call(
        functools.partial(_dispatch_kernel, tt=tt, n_tok=T, n_chunks=n_rows // tt),
        out_shape=jax.ShapeDtypeStruct((n_rows, D), F32),
        grid_spec=pltpu.PrefetchScalarGridSpec(
            num_scalar_prefetch=2,
            grid=(T // tt,),
            in_specs=[
                pl.BlockSpec((tt, D), lambda i, d, ce: (i, 0)),
                pl.BlockSpec((1, D), lambda i, d, ce: (0, 0)),
            ],
            out_specs=pl.BlockSpec(memory_space=pl.ANY),
            scratch_shapes=[pltpu.VMEM((tt, D), F32), pltpu.SemaphoreType.DMA],
        ),
        compiler_params=_cparams(("arbitrary",), 40),
        name="moe_dispatch",
    )(dest, chunk_end, h2, g.reshape(1, D))


def _expert_kernel(ce_ref, nv_ref, xs_ref, w1_ref, w3_ref, w2_ref, o_ref, w1b, w3b, w2b):
    c = pl.program_id(0)
    e = ce_ref[c]
    prev = ce_ref[jnp.maximum(c - 1, 0)]

    @pl.when(jnp.logical_or(c == 0, e != prev))
    def _():
        w1b[...] = w1_ref[0, 0].astype(BF16)
        w3b[...] = w3_ref[0, 0].astype(BF16)
        w2b[...] = w2_ref[0, 0].astype(BF16)

    @pl.when(c < nv_ref[0])
    def _():
        x = xs_ref[...].astype(BF16)
        a = _dot(x, w1b[...])
        b = _dot(x, w3b[...])
        hc = a * jax.nn.sigmoid(a) * b
        o_ref[...] = _dot(hc.astype(BF16), w2b[...])

    @pl.when(c >= nv_ref[0])
    def _():
        o_ref[...] = jnp.zeros_like(o_ref)


def _experts(chunk_e, n_valid, xs, w1, w3, w2, layer):
    P, D = xs.shape
    F = w1.shape[-1]
    ch = MOE_CHUNK
    rows = lambda c, ce, nv: (jnp.minimum(c, nv[0] - 1), 0)
    wsel = lambda c, ce, nv: (layer, ce[c], 0, 0)
    return pl.pallas_call(
        _expert_kernel,
        out_shape=jax.ShapeDtypeStruct((P, D), F32),
        grid_spec=pltpu.PrefetchScalarGridSpec(
            num_scalar_prefetch=2,
            grid=(P // ch,),
            in_specs=[
                pl.BlockSpec((ch, D), rows),
                pl.BlockSpec((1, 1, D, F), wsel),
                pl.BlockSpec((1, 1, D, F), wsel),
                pl.BlockSpec((1, 1, F, D), wsel),
            ],
            out_specs=pl.BlockSpec((ch, D), lambda c, ce, nv: (c, 0)),
            scratch_shapes=[pltpu.VMEM((D, F), BF16), pltpu.VMEM((D, F), BF16), pltpu.VMEM((F, D), BF16)],
        ),
        compiler_params=_cparams(("arbitrary",), 56),
        name="moe_experts",
    )(chunk_e, n_valid, xs, w1, w3, w2)


def _combine_kernel(dest_ref, h_ref, info_ref, rows_ref, p_ref, g_ref, gw_ref, gb_ref, pw_ref, fn_ref,
                    o_ref, buf, sem, *, tt, n_tok, final):
    base = pl.program_id(0) * tt

    def issue(r, c):
        for k in range(TOP_K_IN_GROUP):
            _row_copy(rows_ref, dest_ref[k * n_tok + base + r], buf.at[k], r, sem).start()
        return c

    lax.fori_loop(0, tt, issue, 0)

    def drain(r, c):
        for k in range(TOP_K_IN_GROUP):
            _row_copy(rows_ref, dest_ref[k * n_tok + base + r], buf.at[k], r, sem).wait()
        return c

    lax.fori_loop(0, tt, drain, 0)
    info = info_ref[...]
    y = h_ref[...] + info[:, 2:3] * buf[0] + info[:, 3:4] * buf[1]
    hn = _rms(y, g_ref[...])
    gate = jax.nn.sigmoid(_dot(hn.astype(BF16), gw_ref[...]) + gb_ref[...])
    out = y + _dot(p_ref[0].astype(BF16), pw_ref[...]) * gate
    if final:
        out = _rms(out, fn_ref[...])
    o_ref[...] = out


def _combine(dest, h2, info, rows, p3, layer, g, gw, gb, pw, fn, final):
    T, D = h2.shape
    PD = p3.shape[-1]
    tt = min(CMB_TILE, T)
    full = lambda i, d: (0, 0)
    return pl.pallas_call(
        functools.partial(_combine_kernel, tt=tt, n_tok=T, final=final),
        out_shape=jax.ShapeDtypeStruct((T, D), F32),
        grid_spec=pltpu.PrefetchScalarGridSpec(
            num_scalar_prefetch=1,
            grid=(T // tt,),
            in_specs=[
                pl.BlockSpec((tt, D), lambda i, d: (i, 0)),
                pl.BlockSpec((tt, LANES), lambda i, d: (i, 0)),
                pl.BlockSpec(memory_space=pl.ANY),
                pl.BlockSpec((1, tt, PD), lambda i, d: (layer, i, 0)),
                pl.BlockSpec((1, D), full),
                pl.BlockSpec((D, D), full),
                pl.BlockSpec((1, D), full),
                pl.BlockSpec((PD, D), full),
                pl.BlockSpec((1, D), full),
            ],
            out_specs=pl.BlockSpec((tt, D), lambda i, d: (i, 0)),
            scratch_shapes=[pltpu.VMEM((TOP_K_IN_GROUP, tt, D), F32), pltpu.SemaphoreType.DMA],
        ),
        compiler_params=_cparams(("arbitrary",), 40),
        name="moe_combine_ple",
    )(dest, h2, info, rows, p3, g.reshape(1, D), gw.astype(BF16), gb.reshape(1, D), pw.astype(BF16),
      fn.reshape(1, D))


def _moe_ple_layer(h2, p3, layer, norm_ffn, rg_w, rg_b, re_w, re_b, w1, w3, w2, norm_ple, gate_w, gate_b, ple_proj,
                   final_norm, final):
    T, D = h2.shape
    A = T * TOP_K_IN_GROUP
    pad = LANES - N_EXPERT_GROUPS - N_EXPERTS
    wr = jnp.concatenate([rg_w, re_w, jnp.zeros((D, pad), F32)], axis=1)
    br = jnp.concatenate([rg_b, re_b, jnp.zeros((pad,), F32)]).reshape(1, LANES)
    info, cnt = _router(h2, norm_ffn, wr, br)
    counts = cnt[0, N_EXPERT_GROUPS:N_EXPERT_GROUPS + N_EXPERTS].astype(I32)
    n_chunks_e = (counts + MOE_CHUNK - 1) // MOE_CHUNK
    chunk_end = jnp.cumsum(n_chunks_e)
    pstarts = (chunk_end - n_chunks_e) * MOE_CHUNK
    n_chunks = -(-A // MOE_CHUNK) + N_EXPERTS
    n_valid = chunk_end[-1:].astype(I32)
    cidx = jnp.minimum(jnp.arange(n_chunks, dtype=I32), n_valid[0] - 1)
    chunk_e = jnp.sum((chunk_end[None, :] <= cidx[:, None]).astype(I32), axis=1)
    chunk_e = jnp.minimum(chunk_e, N_EXPERTS - 1)
    e_idx = info[:, 0:TOP_K_IN_GROUP].astype(I32)
    rank = info[:, 4:4 + TOP_K_IN_GROUP].astype(I32)
    dest = (pstarts[e_idx] + rank).T.reshape(A)
    xs = _dispatch(dest, chunk_end.astype(I32), h2, norm_ffn, n_chunks * MOE_CHUNK)
    rows = _experts(chunk_e, n_valid, xs, w1, w3, w2, layer)
    return _combine(dest, h2, info, rows, p3, layer, norm_ple, gate_w, gate_b, ple_proj, final_norm, final)


def _rope_rows(xt, cos, sin):
    x1 = xt[0:ROPE_HALF]
    x2 = xt[ROPE_HALF:ROPE_DIMS]
    return jnp.concatenate([x1 * cos - x2 * sin, x2 * cos + x1 * sin, xt[ROPE_DIMS:]], axis=0)


def _kv_kernel(h_ref, g_ref, wn_ref, wt_ref, cos_ref, sin_ref, cv_ref, kst_ref, vs_ref, kwt_ref, vw_ref, *, ts):
    i = pl.program_id(1)
    kvw = N_KV_GROUPS * HEAD_DIM
    hn = _rms(h_ref[0], g_ref[...]).astype(BF16)
    nat = _dot(hn, wn_ref[...])
    cv_ref[0] = nat[:, 0:2 * kvw]
    tr = _dot_nt(wt_ref[...], hn)
    cos = cos_ref[...]
    sin = sin_ref[...]
    blk = lax.broadcasted_iota(I32, (MAX_SEL_BLOCKS, ts), 0)
    pos = i * ts + lax.broadcasted_iota(I32, (MAX_SEL_BLOCKS, ts), 1)
    onehot = jnp.where(pos // SEL_BLOCK == blk, 1.0, 0.0).astype(BF16)
    for g in range(N_KV_GROUPS):
        c0 = g * HEAD_DIM
        kst_ref[0, g, 0:MAX_SEL_BLOCKS, :] = onehot
        kst_ref[0, g, MAX_SEL_BLOCKS:, :] = _rope_rows(tr[c0:c0 + HEAD_DIM], cos, sin).astype(BF16)
        kwt_ref[0, g] = _rope_rows(tr[kvw + c0:kvw + c0 + HEAD_DIM], cos, sin).astype(BF16)
        vs_ref[0, g] = nat[:, 2 * kvw + c0:2 * kvw + c0 + HEAD_DIM].astype(BF16)
        vw_ref[0, g] = nat[:, 3 * kvw + c0:3 * kvw + c0 + HEAD_DIM].astype(BF16)


def _rope_tables(pos):
    inv = jnp.float32(ROPE_THETA) ** (-jnp.arange(ROPE_HALF, dtype=F32) * 2.0 / ROPE_DIMS)
    ang = pos.astype(F32)[:, None] * inv[None, :]
    return jnp.cos(ang), jnp.sin(ang)


def _shared_kv_proj(h, kv_norm, w_kv):
    B, S, D = h.shape
    G, dh = N_KV_GROUPS, HEAD_DIM
    kvw = G * dh
    ts = min(SEQ_TILE, S)
    br = lambda k: w_kv[:, k * kvw:(k + 1) * kvw]
    w_nat = jnp.concatenate([br(0), br(1), br(3), br(5)], axis=1).astype(BF16)
    w_tr = jnp.concatenate([br(2), br(4)], axis=1).T.astype(BF16)
    cos, sin = _rope_tables(jnp.arange(S))
    kd = MAX_SEL_BLOCKS + dh
    return pl.pallas_call(
        functools.partial(_kv_kernel, ts=ts),
        out_shape=(
            jax.ShapeDtypeStruct((B, S, 2 * kvw), F32),
            jax.ShapeDtypeStruct((B, G, kd, S), BF16),
            jax.ShapeDtypeStruct((B, G, S, dh), BF16),
            jax.ShapeDtypeStruct((B, G, dh, S), BF16),
            jax.ShapeDtypeStruct((B, G, S, dh), BF16),
        ),
        grid=(B, S // ts),
        in_specs=[
            pl.BlockSpec((1, ts, D), lambda b, i: (b, i, 0)),
            pl.BlockSpec((1, D), lambda b, i: (0, 0)),
            pl.BlockSpec((D, 4 * kvw), lambda b, i: (0, 0)),
            pl.BlockSpec((2 * kvw, D), lambda b, i: (0, 0)),
            pl.BlockSpec((ROPE_HALF, ts), lambda b, i: (0, i)),
            pl.BlockSpec((ROPE_HALF, ts), lambda b, i: (0, i)),
        ],
        out_specs=(
            pl.BlockSpec((1, ts, 2 * kvw), lambda b, i: (b, i, 0)),
            pl.BlockSpec((1, G, kd, ts), lambda b, i: (b, 0, 0, i)),
            pl.BlockSpec((1, G, ts, dh), lambda b, i: (b, 0, i, 0)),
            pl.BlockSpec((1, G, dh, ts), lambda b, i: (b, 0, 0, i)),
            pl.BlockSpec((1, G, ts, dh), lambda b, i: (b, 0, i, 0)),
        ),
        compiler_params=_cparams(("parallel", "parallel"), 48),
        name="shared_kv_proj",
    )(h, kv_norm.reshape(1, D), w_nat, w_tr, cos.T, sin.T)


def _compress_kernel(x_ref, pos_ref, w1_ref, b1_ref, w2_ref, b2_ref, cos_ref, sin_ref, o_ref, *, nh, keys):
    x = x_ref[0, 0, 0]
    a = _dot((x + pos_ref[0:1]).astype(BF16), w1_ref[0])
    b = _dot((x + pos_ref[1:2]).astype(BF16), w1_ref[1])
    hid = jax.nn.gelu(a + pltpu.roll(b, nh - 1, axis=0) + b1_ref[...]).astype(BF16)
    if keys:
        out = _dot_nt(w2_ref[...], hid) + b2_ref[...]
        o_ref[0, 0] = _rope_rows(out, cos_ref[...], sin_ref[...]).astype(BF16)
    else:
        o_ref[0, 0] = (_dot(hid, w2_ref[...]) + b2_ref[...]).astype(BF16)


def _compress(halves, which, pos_emb, w1, b1, w2, b2, keys):
    _, B, G, nh, hw = halves.shape
    dh = HEAD_DIM
    hidden = w1.shape[-1]
    pos2 = pos_emb.reshape(2, hw)
    w1s = w1.reshape(2, hw, hidden).astype(BF16)
    cos, sin = _rope_tables(jnp.arange(nh) * CMP_STRIDE + CMP_BLOCK - 1)
    if keys:
        w2a, b2a = w2.T.astype(BF16), b2.reshape(dh, 1)
        out_shape, out_block, out_idx = (B, G, dh, nh), (1, 1, dh, nh), lambda b, g: (b, g, 0, 0)
    else:
        w2a, b2a = w2.astype(BF16), b2.reshape(1, dh)
        out_shape, out_block, out_idx = (B, G, nh, dh), (1, 1, nh, dh), lambda b, g: (b, g, 0, 0)
    c2 = lambda b, g: (0, 0)
    return pl.pallas_call(
        functools.partial(_compress_kernel, nh=nh, keys=keys),
        out_shape=jax.ShapeDtypeStruct(out_shape, BF16),
        grid=(B, G),
        in_specs=[
            pl.BlockSpec((1, 1, 1, nh, hw), lambda b, g: (which, b, g, 0, 0)),
            pl.BlockSpec((2, hw), c2),
            pl.BlockSpec((2, hw, hidden), lambda b, g: (0, 0, 0)),
            pl.BlockSpec((1, hidden), c2),
            pl.BlockSpec(w2a.shape, c2),
            pl.BlockSpec(b2a.shape, c2),
            pl.BlockSpec((ROPE_HALF, nh), c2),
            pl.BlockSpec((ROPE_HALF, nh), c2),
        ],
        out_specs=pl.BlockSpec(out_block, out_idx),
        compiler_params=_cparams(("parallel", "parallel"), 40),
        name="compress_k" if keys else "compress_v",
    )(halves, pos2, w1s, b1.reshape(1, hidden), w2a, b2a, cos.T, sin.T)


def _qproj_kernel(h_ref, g_ref, wq_ref, wg_ref, bg_ref, c_ref, s1_ref, s2_ref, q_ref, gt_ref):
    xn = _rms(h_ref[0], g_ref[...]).astype(BF16)
    q = _dot(xn, wq_ref[...])
    c, s1, s2 = c_ref[...], s1_ref[...], s2_ref[...]
    for k in range(q.shape[-1] // LANES):
        x = q[:, k * LANES:(k + 1) * LANES]
        r = x * c + pltpu.roll(x, ROPE_HALF, axis=1) * s1 + pltpu.roll(x, LANES - ROPE_HALF, axis=1) * s2
        q_ref[0, :, k * LANES:(k + 1) * LANES] = r.astype(BF16)
    gt_ref[0] = jax.nn.sigmoid(_dot(xn, wg_ref[...]) + bg_ref[...])


def _q_proj(h, g, w_qg, b_gate):
    B, S, D = h.shape
    HD = N_HEADS * HEAD_DIM
    ng = N_HEADS * N_BRANCH
    ts = min(SEQ_TILE, S)
    wq = w_qg[:, :HD].astype(BF16)
    wg = jnp.pad(w_qg[:, HD:], ((0, 0), (0, LANES - ng))).astype(BF16)
    bg = jnp.pad(b_gate, (0, LANES - ng)).reshape(1, LANES)
    cos, sin = _rope_tables(jnp.arange(S))
    scale = HEAD_DIM ** -0.5
    ones = jnp.ones((S, HEAD_DIM - ROPE_DIMS), F32)
    zeros = jnp.zeros((S, HEAD_DIM - ROPE_DIMS), F32)
    zh = jnp.zeros((S, ROPE_HALF), F32)
    two = lambda a: jnp.concatenate([a, a], axis=1) * scale
    c = two(jnp.concatenate([cos, cos, ones], axis=1))
    s1 = two(jnp.concatenate([zh, sin, zeros], axis=1))
    s2 = two(jnp.concatenate([-sin, zh, zeros], axis=1))
    tab = pl.BlockSpec((ts, LANES), lambda b, i: (i, 0))
    return pl.pallas_call(
        _qproj_kernel,
        out_shape=(jax.ShapeDtypeStruct((B, S, HD), BF16), jax.ShapeDtypeStruct((B, S, LANES), F32)),
        grid=(B, S // ts),
        in_specs=[
            pl.BlockSpec((1, ts, D), lambda b, i: (b, i, 0)),
            pl.BlockSpec((1, D), lambda b, i: (0, 0)),
            pl.BlockSpec((D, HD), lambda b, i: (0, 0)),
            pl.BlockSpec((D, LANES), lambda b, i: (0, 0)),
            pl.BlockSpec((1, LANES), lambda b, i: (0, 0)),
            tab, tab, tab,
        ],
        out_specs=(pl.BlockSpec((1, ts, HD), lambda b, i: (b, i, 0)),
                   pl.BlockSpec((1, ts, LANES), lambda b, i: (b, i, 0))),
        compiler_params=_cparams(("parallel", "parallel"), 40),
        name="nsa_q_proj",
    )(h, g.reshape(1, D), wq, wg, bg, c, s1, s2)


def _stack_heads(qb):
    return jnp.concatenate([qb[:, r * HEAD_DIM:(r + 1) * HEAD_DIM] for r in range(HEADS_PER_GROUP)], axis=0)


def _gated_unstack(o, gates, g, branch, tq):
    lane = lax.broadcasted_iota(I32, gates.shape, 1)
    cols = []
    for r in range(HEADS_PER_GROUP):
        col = (g * HEADS_PER_GROUP + r) * N_BRANCH + branch
        gate = jnp.sum(jnp.where(lane == col, gates, 0.0), axis=-1, keepdims=True)
        cols.append(o[r * tq:(r + 1) * tq] * gate)
    return jnp.concatenate(cols, axis=-1)


def _query_pos(i, tq):
    rows = lax.broadcasted_iota(I32, (HEADS_PER_GROUP * tq, 1), 0)
    return i * tq + (rows & (tq - 1))


def _cmp_kernel(q_ref, gt_ref, k_ref, v_ref, o_ref, bias_ref, *, tq, nc):
    g = pl.program_id(1)
    i = pl.program_id(2)
    qs = _stack_heads(q_ref[0])
    s = _dot(qs, k_ref[0, 0])
    t = _query_pos(i, tq)
    n = lax.broadcasted_iota(I32, (1, nc), 1)
    s = jnp.where(n * CMP_STRIDE + (CMP_BLOCK - 1) <= t, s, -jnp.inf)
    m = jnp.max(s, axis=-1, keepdims=True)
    m = jnp.where(m == -jnp.inf, 0.0, m)
    e = jnp.exp(s - m)
    p = e / jnp.maximum(jnp.sum(e, axis=-1, keepdims=True), 1e-30)
    o = _dot(p.astype(BF16), v_ref[0, 0])
    o_ref[0] = _gated_unstack(o, gt_ref[0], g, 0, tq).astype(BF16)
    ps = p[0:tq]
    for r in range(1, HEADS_PER_GROUP):
        ps = ps + p[r * tq:(r + 1) * tq]
    nn = lax.broadcasted_iota(I32, (nc, MAX_SEL_BLOCKS), 0) * CMP_STRIDE
    jj = lax.broadcasted_iota(I32, (nc, MAX_SEL_BLOCKS), 1) * SEL_BLOCK
    c2s = jnp.where(nn < jj + SEL_BLOCK, jnp.where(nn + CMP_BLOCK > jj, 1.0, 0.0), 0.0)
    imp = jnp.dot(ps, c2s, precision=HIGHEST, preferred_element_type=F32)
    tq_pos = i * tq + lax.broadcasted_iota(I32, (tq, 1), 0)
    cur = tq_pos // SEL_BLOCK
    j = lax.broadcasted_iota(I32, (tq, MAX_SEL_BLOCKS), 1)
    valid = j <= cur
    bonus = jnp.where(j == 0, FORCE_BONUS, jnp.where(j == cur, FORCE_BONUS, jnp.where(j == cur - 1, FORCE_BONUS, 0.0)))
    work = jnp.where(valid, imp + bonus, -jnp.inf)
    sel = jnp.zeros((tq, MAX_SEL_BLOCKS), F32)
    for _ in range(N_SELECT):
        mx = jnp.max(work, axis=-1, keepdims=True)
        idx = jnp.min(jnp.where(work == mx, j, MAX_SEL_BLOCKS), axis=-1, keepdims=True)
        pick = j == idx
        sel = jnp.where(pick, 1.0, sel)
        work = jnp.where(pick, -jnp.inf, work)
    bias_ref[0, 0] = jnp.where(valid, jnp.where(sel > 0.0, 0.0, MASK_BIAS), MASK_BIAS).astype(BF16)


def _cmp_attn(q, gates, kct, vc):
    B, S, HD = q.shape
    G, dh = N_KV_GROUPS, HEAD_DIM
    gw = HD // G
    nc = kct.shape[-1]
    tq = min(Q_TILE, S)
    return pl.pallas_call(
        functools.partial(_cmp_kernel, tq=tq, nc=nc),
        out_shape=(jax.ShapeDtypeStruct((B, S, HD), BF16), jax.ShapeDtypeStruct((B, G, S, MAX_SEL_BLOCKS), BF16)),
        grid=(B, G, S // tq),
        in_specs=[
            pl.BlockSpec((1, tq, gw), lambda b, g, i: (b, i, g)),
            pl.BlockSpec((1, tq, LANES), lambda b, g, i: (b, i, 0)),
            pl.BlockSpec((1, 1, dh, nc), lambda b, g, i: (b, g, 0, 0)),
            pl.BlockSpec((1, 1, nc, dh), lambda b, g, i: (b, g, 0, 0)),
        ],
        out_specs=(pl.BlockSpec((1, tq, gw), lambda b, g, i: (b, i, g)),
                   pl.BlockSpec((1, 1, tq, MAX_SEL_BLOCKS), lambda b, g, i: (b, g, i, 0))),
        compiler_params=_cparams(("parallel", "parallel", "parallel"), 40),
        name="nsa_compressed",
    )(q, gates, kct, vc)


def _sel_kernel(q_ref, bias_ref, gt_ref, k_ref, v_ref, o_ref, qa, m_s, l_s, acc, *, tq, tk):
    g = pl.program_id(1)
    i = pl.program_id(2)
    qb = q_ref[0]
    bias = bias_ref[0, 0]
    for r in range(HEADS_PER_GROUP):
        qa[r * tq:(r + 1) * tq, 0:MAX_SEL_BLOCKS] = bias
        qa[r * tq:(r + 1) * tq, MAX_SEL_BLOCKS:] = qb[:, r * HEAD_DIM:(r + 1) * HEAD_DIM]
    m_s[...] = jnp.full_like(m_s, -jnp.inf)
    l_s[...] = jnp.zeros_like(l_s)
    acc[...] = jnp.zeros_like(acc)
    t = _query_pos(i, tq)
    lane = lax.broadcasted_iota(I32, (1, tk), 1)

    def body(jt, c):
        k0 = pl.multiple_of(jt * tk, tk)
        s = _dot(qa[...], k_ref[0, 0, :, pl.ds(k0, tk)])
        s = jnp.where(k0 + lane <= t, s, -jnp.inf)
        m_old = m_s[...]
        m_new = jnp.maximum(m_old, jnp.max(s, axis=-1, keepdims=True))
        alpha = jnp.exp(m_old - m_new)
        p = jnp.exp(s - m_new)
        l_s[...] = alpha * l_s[...] + jnp.sum(p, axis=-1, keepdims=True)
        acc[...] = alpha * acc[...] + _dot(p.astype(BF16), v_ref[0, 0, pl.ds(k0, tk), :])
        m_s[...] = m_new
        return c

    lax.fori_loop(0, (i * tq + tq + tk - 1) // tk, body, 0)
    o_ref[0] = _gated_unstack(acc[...] / l_s[...], gt_ref[0], g, 1, tq).astype(BF16)


def _sel_attn(q, bias, gates, kst, vs):
    B, S, HD = q.shape
    G, dh = N_KV_GROUPS, HEAD_DIM
    gw = HD // G
    kd = kst.shape[2]
    tq = min(Q_TILE, S)
    tk = min(KEY_TILE, S)
    rows = HEADS_PER_GROUP * tq
    return pl.pallas_call(
        functools.partial(_sel_kernel, tq=tq, tk=tk),
        out_shape=jax.ShapeDtypeStruct((B, S, HD), BF16),
        grid=(B, G, S // tq),
        in_specs=[
            pl.BlockSpec((1, tq, gw), lambda b, g, i: (b, i, g)),
            pl.BlockSpec((1, 1, tq, MAX_SEL_BLOCKS), lambda b, g, i: (b, g, i, 0)),
            pl.BlockSpec((1, tq, LANES), lambda b, g, i: (b, i, 0)),
            pl.BlockSpec((1, 1, kd, S), lambda b, g, i: (b, g, 0, 0)),
            pl.BlockSpec((1, 1, S, dh), lambda b, g, i: (b, g, 0, 0)),
        ],
        out_specs=pl.BlockSpec((1, tq, gw), lambda b, g, i: (b, i, g)),
        scratch_shapes=[pltpu.VMEM((rows, kd), BF16), pltpu.VMEM((rows, 1), F32), pltpu.VMEM((rows, 1), F32),
                        pltpu.VMEM((rows, dh), F32)],
        compiler_params=_cparams(("parallel", "parallel", "arbitrary"), 48),
        name="nsa_selected",
    )(q, bias, gates, kst, vs)


def _win_kernel(q_ref, gt_ref, k_ref, v_ref, o_ref, *, tq, wb):
    g = pl.program_id(1)
    i = pl.program_id(2)
    k0 = pl.multiple_of(jnp.maximum(i * tq - WINDOW, 0), tq)
    qs = _stack_heads(q_ref[0])
    s = _dot(qs, k_ref[0, 0, :, pl.ds(k0, wb)])
    t = _query_pos(i, tq)
    kpos = k0 + lax.broadcasted_iota(I32, (1, wb), 1)
    s = jnp.where(kpos <= t, jnp.where(kpos > t - WINDOW, s, -jnp.inf), -jnp.inf)
    m = jnp.max(s, axis=-1, keepdims=True)
    e = jnp.exp(s - m)
    p = e / jnp.sum(e, axis=-1, keepdims=True)
    o = _dot(p.astype(BF16), v_ref[0, 0, pl.ds(k0, wb), :])
    o_ref[0] = _gated_unstack(o, gt_ref[0], g, 2, tq).astype(BF16)


def _win_attn(q, gates, kwt, vw):
    B, S, HD = q.shape
    G, dh = N_KV_GROUPS, HEAD_DIM
    gw = HD // G
    tq = min(Q_TILE, S)
    wb = WINDOW + tq
    assert S >= wb and tq % LANES == 0
    return pl.pallas_call(
        functools.partial(_win_kernel, tq=tq, wb=wb),
        out_shape=jax.ShapeDtypeStruct((B, S, HD), BF16),
        grid=(B, G, S // tq),
        in_specs=[
            pl.BlockSpec((1, tq, gw), lambda b, g, i: (b, i, g)),
            pl.BlockSpec((1, tq, LANES), lambda b, g, i: (b, i, 0)),
            pl.BlockSpec((1, 1, dh, S), lambda b, g, i: (b, g, 0, 0)),
            pl.BlockSpec((1, 1, S, dh), lambda b, g, i: (b, g, 0, 0)),
        ],
        out_specs=pl.BlockSpec((1, tq, gw), lambda b, g, i: (b, i, g)),
        compiler_params=_cparams(("parallel", "parallel", "parallel"), 40),
        name="nsa_window",
    )(q, gates, kwt, vw)


LOG2E = 1.4426950408889634
Q_SCALE = HEAD_DIM ** -0.5 * LOG2E
V_ROWS = HEAD_DIM + 16
KEY_AUG = 2 * LANES
SEL_CHAINS = 1
NEG_BIG = -1e30


def _lane_rope_tables(pos):
    cos, sin = _rope_tables(pos)
    n = pos.shape[0]
    ones = jnp.ones((n, HEAD_DIM - ROPE_DIMS), F32)
    zeros = jnp.zeros((n, HEAD_DIM - ROPE_DIMS), F32)
    zh = jnp.zeros((n, ROPE_HALF), F32)
    two = lambda a: jnp.concatenate([a, a], axis=1)
    return (two(jnp.concatenate([cos, cos, ones], axis=1)), two(jnp.concatenate([zh, sin, zeros], axis=1)),
            two(jnp.concatenate([-sin, zh, zeros], axis=1)))


def _rope_lanes(x, c, s1, s2):
    return x * c + pltpu.roll(x, ROPE_HALF, axis=1) * s1 + pltpu.roll(x, LANES - ROPE_HALF, axis=1) * s2


def _kv2_kernel(h_ref, g_ref, wn_ref, wt_ref, c_ref, s1_ref, s2_ref, cv_ref, ksa_ref, vst_ref, kw_ref, vwt_ref, *, ts):
    i = pl.program_id(1)
    kvw = N_KV_GROUPS * HEAD_DIM
    hn = _rms(h_ref[0], g_ref[...]).astype(BF16)
    nat = _dot(hn, wn_ref[...])
    cv_ref[0] = nat[:, 0:2 * kvw]
    tr = _dot_nt(wt_ref[...], hn)
    c, s1, s2 = c_ref[...], s1_ref[...], s2_ref[...]
    roped = [_rope_lanes(nat[:, 2 * kvw + k * LANES:2 * kvw + (k + 1) * LANES], c, s1, s2)
             for k in range(2 * kvw // LANES)]
    lane = lax.broadcasted_iota(I32, (ts, KEY_AUG), 1)
    pos = i * ts + lax.broadcasted_iota(I32, (ts, KEY_AUG), 0)
    onehot = jnp.where(lane - HEAD_DIM == pos // SEL_BLOCK, 1.0, 0.0).astype(BF16)
    ones_row = jnp.where(lax.broadcasted_iota(I32, (V_ROWS - HEAD_DIM, ts), 0) == 0, 1.0, 0.0).astype(BF16)
    per_tile = LANES // HEAD_DIM
    for g in range(N_KV_GROUPS):
        lo = (g % per_tile) * HEAD_DIM
        ksa_ref[0, g] = onehot
        ksa_ref[0, g, :, 0:HEAD_DIM] = roped[g // per_tile][:, lo:lo + HEAD_DIM].astype(BF16)
        kw_ref[0, g] = roped[N_KV_GROUPS // per_tile + g // per_tile][:, lo:lo + HEAD_DIM].astype(BF16)
        for ref, base in ((vst_ref, 0), (vwt_ref, kvw)):
            ref[0, g, 0:HEAD_DIM, :] = tr[base + g * HEAD_DIM:base + (g + 1) * HEAD_DIM].astype(BF16)
            ref[0, g, HEAD_DIM:, :] = ones_row


def _shared_kv_proj2(h, kv_norm, w_kv):
    B, S, D = h.shape
    G, dh = N_KV_GROUPS, HEAD_DIM
    kvw = G * dh
    ts = min(SEQ_TILE, S)
    br = lambda k: w_kv[:, k * kvw:(k + 1) * kvw]
    w_nat = jnp.concatenate([br(0), br(1), br(2), br(4)], axis=1).astype(BF16)
    w_tr = jnp.concatenate([br(3), br(5)], axis=1).T.astype(BF16)
    tabs = _lane_rope_tables(jnp.arange(S))
    tab = pl.BlockSpec((ts, LANES), lambda b, i: (i, 0))
    return pl.pallas_call(
        functools.partial(_kv2_kernel, ts=ts),
        out_shape=(
            jax.ShapeDtypeStruct((B, S, 2 * kvw), F32),
            jax.ShapeDtypeStruct((B, G, S, KEY_AUG), BF16),
            jax.ShapeDtypeStruct((B, G, V_ROWS, S), BF16),
            jax.ShapeDtypeStruct((B, G, S, dh), BF16),
            jax.ShapeDtypeStruct((B, G, V_ROWS, S), BF16),
        ),
        grid=(B, S // ts),
        in_specs=[
            pl.BlockSpec((1, ts, D), lambda b, i: (b, i, 0)),
            pl.BlockSpec((1, D), lambda b, i: (0, 0)),
            pl.BlockSpec((D, 4 * kvw), lambda b, i: (0, 0)),
            pl.BlockSpec((2 * kvw, D), lambda b, i: (0, 0)),
            tab, tab, tab,
        ],
        out_specs=(
            pl.BlockSpec((1, ts, 2 * kvw), lambda b, i: (b, i, 0)),
            pl.BlockSpec((1, G, ts, KEY_AUG), lambda b, i: (b, 0, i, 0)),
            pl.BlockSpec((1, G, V_ROWS, ts), lambda b, i: (b, 0, 0, i)),
            pl.BlockSpec((1, G, ts, dh), lambda b, i: (b, 0, i, 0)),
            pl.BlockSpec((1, G, V_ROWS, ts), lambda b, i: (b, 0, 0, i)),
        ),
        compiler_params=_cparams(("parallel", "parallel"), 48),
        name="shared_kv_proj",
    )(h, kv_norm.reshape(1, D), w_nat, w_tr, *tabs)


def _compress2_kernel(x_ref, pos_ref, w1_ref, b1_ref, w2_ref, b2_ref, c_ref, s1_ref, s2_ref, o_ref, *, nh, keys):
    x = x_ref[0, 0, 0]
    a = _dot((x + pos_ref[0:1]).astype(BF16), w1_ref[0])
    b = _dot((x + pos_ref[1:2]).astype(BF16), w1_ref[1])
    hid = jax.nn.gelu(a + pltpu.roll(b, nh - 1, axis=0) + b1_ref[...]).astype(BF16)
    if keys:
        out = _rope_lanes(_dot(hid, w2_ref[...]) + b2_ref[...], c_ref[...], s1_ref[...], s2_ref[...])
        o_ref[0, 0] = out[:, 0:HEAD_DIM].astype(BF16)
    else:
        o_ref[0, 0] = (_dot_nt(w2_ref[...], hid) + b2_ref[...]).astype(BF16)


def _compress2(halves, which, pos_emb, w1, b1, w2, b2, keys):
    _, B, G, nh, hw = halves.shape
    dh = HEAD_DIM
    hidden = w1.shape[-1]
    pos2 = pos_emb.reshape(2, hw)
    w1s = w1.reshape(2, hw, hidden).astype(BF16)
    tabs = _lane_rope_tables(jnp.arange(nh) * CMP_STRIDE + CMP_BLOCK - 1)
    if keys:
        w2a = jnp.pad(w2, ((0, 0), (0, LANES - dh))).astype(BF16)
        b2a = jnp.pad(b2, (0, LANES - dh)).reshape(1, LANES)
        out_shape, out_block = (B, G, nh, dh), (1, 1, nh, dh)
    else:
        w2a, b2a = w2.T.astype(BF16), b2.reshape(dh, 1)
        out_shape, out_block = (B, G, dh, nh), (1, 1, dh, nh)
    c2 = lambda b, g: (0, 0)
    tab = pl.BlockSpec((nh, LANES), c2)
    return pl.pallas_call(
        functools.partial(_compress2_kernel, nh=nh, keys=keys),
        out_shape=jax.ShapeDtypeStruct(out_shape, BF16),
        grid=(B, G),
        in_specs=[
            pl.BlockSpec((1, 1, 1, nh, hw), lambda b, g: (which, b, g, 0, 0)),
            pl.BlockSpec((2, hw), c2),
            pl.BlockSpec((2, hw, hidden), lambda b, g: (0, 0, 0)),
            pl.BlockSpec((1, hidden), c2),
            pl.BlockSpec(w2a.shape, c2),
            pl.BlockSpec(b2a.shape, c2),
            tab, tab, tab,
        ],
        out_specs=pl.BlockSpec(out_block, lambda b, g: (b, g, 0, 0)),
        compiler_params=_cparams(("parallel", "parallel"), 40),
        name="compress_k" if keys else "compress_v",
    )(halves, pos2, w1s, b1.reshape(1, hidden), w2a, b2a, *tabs)


def _qproj2_kernel(h_ref, g_ref, wqt_ref, wg_ref, bg_ref, cos_ref, sin_ref, qt_ref, gt_ref):
    xn = _rms(h_ref[0], g_ref[...]).astype(BF16)
    tr = _dot_nt(wqt_ref[...], xn)
    cos, sin = cos_ref[...], sin_ref[...]
    for hd in range(N_HEADS):
        rows = slice(hd * HEAD_DIM, (hd + 1) * HEAD_DIM)
        qt_ref[0, rows, :] = (_rope_rows(tr[rows], cos, sin) * Q_SCALE).astype(BF16)
    gt_ref[0] = jax.nn.sigmoid(_dot(xn, wg_ref[...]) + bg_ref[...])


def _q_proj2(h, g, w_qg, b_gate):
    B, S, D = h.shape
    HD = N_HEADS * HEAD_DIM
    ng = N_HEADS * N_BRANCH
    ts = min(SEQ_TILE, S)
    wqt = w_qg[:, :HD].T.astype(BF16)
    wg = jnp.pad(w_qg[:, HD:], ((0, 0), (0, LANES - ng))).astype(BF16)
    bg = jnp.pad(b_gate, (0, LANES - ng)).reshape(1, LANES)
    cos, sin = _rope_tables(jnp.arange(S))
    tab = pl.BlockSpec((ROPE_HALF, ts), lambda b, i: (0, i))
    return pl.pallas_call(
        _qproj2_kernel,
        out_shape=(jax.ShapeDtypeStruct((B, HD, S), BF16), jax.ShapeDtypeStruct((B, S, LANES), F32)),
        grid=(B, S // ts),
        in_specs=[
            pl.BlockSpec((1, ts, D), lambda b, i: (b, i, 0)),
            pl.BlockSpec((1, D), lambda b, i: (0, 0)),
            pl.BlockSpec((HD, D), lambda b, i: (0, 0)),
            pl.BlockSpec((D, LANES), lambda b, i: (0, 0)),
            pl.BlockSpec((1, LANES), lambda b, i: (0, 0)),
            tab, tab,
        ],
        out_specs=(pl.BlockSpec((1, HD, ts), lambda b, i: (b, 0, i)),
                   pl.BlockSpec((1, ts, LANES), lambda b, i: (b, i, 0))),
        compiler_params=_cparams(("parallel", "parallel"), 40),
        name="nsa_q_proj",
    )(h, g.reshape(1, D), wqt, wg, bg, cos.T, sin.T)


def _heads_on_lanes(qt):
    return jnp.concatenate([qt[r * HEAD_DIM:(r + 1) * HEAD_DIM] for r in range(HEADS_PER_GROUP)], axis=1)


def _lane_query_pos(i, tq):
    lanes = lax.broadcasted_iota(I32, (1, HEADS_PER_GROUP * tq), 1)
    return i * tq + (lanes & (tq - 1))


def _finish_heads(acc_t, gates, g, branch, tq, denom_row):
    lane = lax.broadcasted_iota(I32, gates.shape, 1)
    rows = acc_t.shape[0]
    cols = []
    for r in range(HEADS_PER_GROUP):
        blk = acc_t[:, r * tq:(r + 1) * tq]
        nat = jnp.concatenate([blk, jnp.zeros((tq - rows, tq), F32)], axis=0).T
        col = (g * HEADS_PER_GROUP + r) * N_BRANCH + branch
        scale = jnp.sum(jnp.where(lane == col, gates, 0.0), axis=-1, keepdims=True)
        if denom_row is not None:
            scale = scale / nat[:, denom_row:denom_row + 1]
        cols.append(nat[:, 0:HEAD_DIM] * scale)
    return jnp.concatenate(cols, axis=-1)


def _cmp2_kernel(qt_ref, gt_ref, k_ref, vt_ref, c2s_ref, o_ref, bias_ref, *, tq, nc):
    g = pl.program_id(1)
    i = pl.program_id(2)
    s = _dot(k_ref[0, 0], _heads_on_lanes(qt_ref[0]))
    t = _lane_query_pos(i, tq)
    n = lax.broadcasted_iota(I32, (nc, 1), 0)
    s = jnp.where(n * CMP_STRIDE + (CMP_BLOCK - 1) <= t, s, -jnp.inf)
    m = jnp.max(s, axis=0, keepdims=True)
    m = jnp.where(m == -jnp.inf, 0.0, m)
    e = jnp.exp2(s - m)
    p = e * (1.0 / jnp.maximum(jnp.sum(e, axis=0, keepdims=True), 1e-30))
    ot = _dot(vt_ref[0, 0], p.astype(BF16))
    o_ref[0] = _finish_heads(ot, gt_ref[0], g, 0, tq, None).astype(BF16)
    ps = p[:, 0:tq]
    for r in range(1, HEADS_PER_GROUP):
        ps = ps + p[:, r * tq:(r + 1) * tq]
    hi = ps.astype(BF16)
    rem = ps - hi.astype(F32)
    mid = rem.astype(BF16)
    lo = (rem - mid.astype(F32)).astype(BF16)
    c2s = c2s_ref[...]
    imp = _dot(c2s, hi) + _dot(c2s, mid) + _dot(c2s, lo)
    cur = (i * tq + lax.broadcasted_iota(I32, (1, tq), 1)) // SEL_BLOCK
    j = lax.broadcasted_iota(I32, (MAX_SEL_BLOCKS, tq), 0)
    valid = j <= cur
    bonus = jnp.where(j == 0, FORCE_BONUS, jnp.where(j == cur, FORCE_BONUS, jnp.where(j == cur - 1, FORCE_BONUS, 0.0)))
    work = jnp.where(valid, imp + bonus, -jnp.inf)
    sel = jnp.zeros((MAX_SEL_BLOCKS, tq), F32)
    for _ in range(N_SELECT):
        mx = jnp.max(work, axis=0, keepdims=True)
        idx = jnp.min(jnp.where(work == mx, j, MAX_SEL_BLOCKS), axis=0, keepdims=True)
        pick = j == idx
        sel = jnp.where(pick, 1.0, sel)
        work = jnp.where(pick, -jnp.inf, work)
    bias_ref[0, 0] = jnp.where(valid, jnp.where(sel > 0.0, 0.0, MASK_BIAS), MASK_BIAS).astype(BF16)


def _cmp_attn2(qt, gates, kc, vct):
    B, HD, S = qt.shape
    G, dh = N_KV_GROUPS, HEAD_DIM
    gw = HD // G
    nc = kc.shape[2]
    tq = min(Q_TILE, S)
    n0 = jnp.arange(nc)[None, :] * CMP_STRIDE
    j0 = jnp.arange(MAX_SEL_BLOCKS)[:, None] * SEL_BLOCK
    c2s = ((n0 < j0 + SEL_BLOCK) & (n0 + CMP_BLOCK > j0)).astype(BF16)
    return pl.pallas_call(
        functools.partial(_cmp2_kernel, tq=tq, nc=nc),
        out_shape=(jax.ShapeDtypeStruct((B, S, HD), BF16), jax.ShapeDtypeStruct((B, G, MAX_SEL_BLOCKS, S), BF16)),
        grid=(B, G, S // tq),
        in_specs=[
            pl.BlockSpec((1, gw, tq), lambda b, g, i: (b, g, i)),
            pl.BlockSpec((1, tq, LANES), lambda b, g, i: (b, i, 0)),
            pl.BlockSpec((1, 1, nc, dh), lambda b, g, i: (b, g, 0, 0)),
            pl.BlockSpec((1, 1, dh, nc), lambda b, g, i: (b, g, 0, 0)),
            pl.BlockSpec((MAX_SEL_BLOCKS, nc), lambda b, g, i: (0, 0)),
        ],
        out_specs=(pl.BlockSpec((1, tq, gw), lambda b, g, i: (b, i, g)),
                   pl.BlockSpec((1, 1, MAX_SEL_BLOCKS, tq), lambda b, g, i: (b, g, 0, i))),
        compiler_params=_cparams(("parallel", "parallel", "parallel"), 40),
        name="nsa_compressed",
    )(qt, gates, kc, vct, c2s)


def _sel2_kernel(qt_ref, bias_ref, gt_ref, k_ref, vt_ref, o_ref, qa, m_s, acc, s_a, s_b, *, tq, tk):
    g = pl.program_id(1)
    i = pl.program_id(2)
    qt = qt_ref[0]
    bias = bias_ref[0, 0]
    for r in range(HEADS_PER_GROUP):
        cols = slice(r * tq, (r + 1) * tq)
        qa[0:HEAD_DIM, cols] = qt[r * HEAD_DIM:(r + 1) * HEAD_DIM]
        qa[HEAD_DIM:HEAD_DIM + MAX_SEL_BLOCKS, cols] = bias
        qa[HEAD_DIM + MAX_SEL_BLOCKS:, cols] = jnp.zeros((KEY_AUG - HEAD_DIM - MAX_SEL_BLOCKS, tq), BF16)
    for c in range(SEL_CHAINS):
        m_s[c] = jnp.full(m_s.shape[1:], NEG_BIG, F32)
        acc[c] = jnp.zeros(acc.shape[1:], F32)
    t = _lane_query_pos(i, tq)
    ck = tk // SEL_CHAINS

    def qk(jt, buf):
        buf[...] = _dot(k_ref[0, 0, pl.ds(pl.multiple_of(jt * tk, tk), tk), :], qa[...])

    def absorb(jt, buf, diagonal):
        for c in range(SEL_CHAINS):
            kc = pl.multiple_of(jt * tk + c * ck, ck)
            s = buf[c * ck:(c + 1) * ck, :]
            if diagonal:
                s = jnp.where(kc + lax.broadcasted_iota(I32, (ck, 1), 0) <= t, s, NEG_BIG)
            m_old = m_s[c]
            m_new = jnp.maximum(m_old, jnp.max(s, axis=0, keepdims=True))
            p = jnp.exp2(s - m_new).astype(BF16)
            acc[c] = jnp.exp2(m_old - m_new) * acc[c] + _dot(vt_ref[0, 0, :, pl.ds(kc, ck)], p)
            m_s[c] = m_new

    def pair(u, c):
        qk(2 * u + 1, s_b)
        absorb(2 * u, s_a, False)
        qk(2 * u + 2, s_a)
        absorb(2 * u + 1, s_b, False)
        return c

    last = (i * tq + tq + tk - 1) // tk - 1
    qk(0, s_a)
    lax.fori_loop(0, last // 2, pair, 0)

    @pl.when(last % 2 == 1)
    def _():
        qk(last, s_b)
        absorb(last - 1, s_a, False)
        absorb(last, s_b, True)

    @pl.when(last % 2 == 0)
    def _():
        absorb(last, s_a, True)

    m_all = m_s[0]
    for c in range(1, SEL_CHAINS):
        m_all = jnp.maximum(m_all, m_s[c])
    acc_t = jnp.exp2(m_s[0] - m_all) * acc[0]
    for c in range(1, SEL_CHAINS):
        acc_t = acc_t + jnp.exp2(m_s[c] - m_all) * acc[c]
    o_ref[0] = _finish_heads(acc_t, gt_ref[0], g, 1, tq, HEAD_DIM).astype(BF16)


def _sel_attn2(qt, bias, gates, ksa, vst):
    B, HD, S = qt.shape
    G = N_KV_GROUPS
    gw = HD // G
    tq = min(Q_TILE, S)
    tk = min(KEY_TILE, S)
    width = HEADS_PER_GROUP * tq
    return pl.pallas_call(
        functools.partial(_sel2_kernel, tq=tq, tk=tk),
        out_shape=jax.ShapeDtypeStruct((B, S, HD), BF16),
        grid=(B, G, S // tq),
        in_specs=[
            pl.BlockSpec((1, gw, tq), lambda b, g, i: (b, g, i)),
            pl.BlockSpec((1, 1, MAX_SEL_BLOCKS, tq), lambda b, g, i: (b, g, 0, i)),
            pl.BlockSpec((1, tq, LANES), lambda b, g, i: (b, i, 0)),
            pl.BlockSpec((1, 1, S, KEY_AUG), lambda b, g, i: (b, g, 0, 0)),
            pl.BlockSpec((1, 1, V_ROWS, S), lambda b, g, i: (b, g, 0, 0)),
        ],
        out_specs=pl.BlockSpec((1, tq, gw), lambda b, g, i: (b, i, g)),
        scratch_shapes=[pltpu.VMEM((KEY_AUG, width), BF16), pltpu.VMEM((SEL_CHAINS, 1, width), F32),
                        pltpu.VMEM((SEL_CHAINS, V_ROWS, width), F32),
                        pltpu.VMEM((tk, width), F32), pltpu.VMEM((tk, width), F32)],
        compiler_params=_cparams(("parallel", "parallel", "arbitrary"), 48),
        name="nsa_selected",
    )(qt, bias, gates, ksa, vst)


def _win2_kernel(qt_ref, gt_ref, k_ref, vt_ref, o_ref, *, tq, wb):
    g = pl.program_id(1)
    i = pl.program_id(2)
    k0 = pl.multiple_of(jnp.maximum(i * tq - WINDOW, 0), tq)
    s = _dot(k_ref[0, 0, pl.ds(k0, wb), :], _heads_on_lanes(qt_ref[0]))
    t = _lane_query_pos(i, tq)
    kpos = k0 + lax.broadcasted_iota(I32, (wb, 1), 0)
    s = jnp.where(kpos <= t, jnp.where(kpos > t - WINDOW, s, -jnp.inf), -jnp.inf)
    m = jnp.max(s, axis=0, keepdims=True)
    p = jnp.exp2(s - m).astype(BF16)
    acc = _dot(vt_ref[0, 0, :, pl.ds(k0, wb)], p)
    o_ref[0] = _finish_heads(acc, gt_ref[0], g, 2, tq, HEAD_DIM).astype(BF16)


def _win_attn2(qt, gates, kw, vwt):
    B, HD, S = qt.shape
    G, dh = N_KV_GROUPS, HEAD_DIM
    gw = HD // G
    tq = min(Q_TILE, S)
    wb = WINDOW + tq
    assert S >= wb and tq % LANES == 0
    return pl.pallas_call(
        functools.partial(_win2_kernel, tq=tq, wb=wb),
        out_shape=jax.ShapeDtypeStruct((B, S, HD), BF16),
        grid=(B, G, S // tq),
        in_specs=[
            pl.BlockSpec((1, gw, tq), lambda b, g, i: (b, g, i)),
            pl.BlockSpec((1, tq, LANES), lambda b, g, i: (b, i, 0)),
            pl.BlockSpec((1, 1, S, dh), lambda b, g, i: (b, g, 0, 0)),
            pl.BlockSpec((1, 1, V_ROWS, S), lambda b, g, i: (b, g, 0, 0)),
        ],
        out_specs=pl.BlockSpec((1, tq, gw), lambda b, g, i: (b, i, g)),
        compiler_params=_cparams(("parallel", "parallel", "parallel"), 40),
        name="nsa_window",
    )(qt, gates, kw, vwt)


def _oproj_kernel(h_ref, a_ref, b_ref, c_ref, w_ref, o_ref):
    o = a_ref[...].astype(F32) + b_ref[...].astype(F32) + c_ref[...].astype(F32)
    o_ref[...] = h_ref[...] + _dot(o.astype(BF16), w_ref[...])


def _out_proj(h2, oc, os_, ow, w_o):
    T, D = h2.shape
    HD = oc.shape[-1]
    tt = min(SEQ_TILE, T)
    blk = lambda w: pl.BlockSpec((tt, w), lambda i: (i, 0))
    return pl.pallas_call(
        _oproj_kernel,
        out_shape=jax.ShapeDtypeStruct((T, D), F32),
        grid=(T // tt,),
        in_specs=[blk(D), blk(HD), blk(HD), blk(HD), pl.BlockSpec((HD, D), lambda i: (0, 0))],
        out_specs=blk(D),
        compiler_params=_cparams(("parallel",), 40),
        name="nsa_out_proj",
    )(h2, oc, os_, ow, w_o.astype(BF16))


def _nsa_layer(h, g, w_qg, b_gate, w_o, shared):
    B, S, D = h.shape
    kc, vct, ksa, vst, kw, vwt = shared
    qt, gates = _q_proj2(h, g, w_qg, b_gate)
    oc, bias = _cmp_attn2(qt, gates, kc, vct)
    os_ = _sel_attn2(qt, bias, gates, ksa, vst)
    ow = _win_attn2(qt, gates, kw, vwt)
    flat = lambda a: a.reshape(B * S, a.shape[-1])
    return _out_proj(flat(h), flat(oc), flat(os_), flat(ow), w_o).reshape(B, S, D)


def _shared_kv(h, kv_norm, w_kv, ck, cv):
    B, S, _ = h.shape
    G, dh = N_KV_GROUPS, HEAD_DIM
    assert S % SEL_BLOCK == 0 and S // SEL_BLOCK <= MAX_SEL_BLOCKS
    cvals, ksa, vst, kw, vwt = _shared_kv_proj2(h, kv_norm, w_kv)
    halves = cvals.reshape(B, S, 2, G, dh).transpose(2, 0, 3, 1, 4).reshape(2, B, G, S // CMP_STRIDE, CMP_STRIDE * dh)
    kc = _compress2(halves, 0, *ck, keys=True)
    vct = _compress2(halves, 1, *cv, keys=False)
    return kc, vct, ksa, vst, kw, vwt


def kernel(x, p, norm_mix, norm_ffn, norm_ple, pool_w, pool_b, pool_scale, kv_norm, w_kv, cmp_k_pos, cmp_k_w1, cmp_k_b1, cmp_k_w2, cmp_k_b2, cmp_v_pos, cmp_v_w1, cmp_v_b1, cmp_v_w2, cmp_v_b2, w_qg, b_gate, w_o, router_g_w, router_g_b, router_e_w, router_e_b, moe_w1, moe_w3, moe_w2, ple_proj, ple_gate_w, ple_gate_b, final_norm):
    B, S, D = x.shape
    depth = p.shape[0]
    n_a = pool_w.shape[0]
    T = B * S
    h = x
    shared = None
    for i in range(depth):
        if i == n_a:
            shared = _shared_kv(h, kv_norm, w_kv,
                                (cmp_k_pos, cmp_k_w1, cmp_k_b1, cmp_k_w2, cmp_k_b2),
                                (cmp_v_pos, cmp_v_w1, cmp_v_b1, cmp_v_w2, cmp_v_b2))
        if i < n_a:
            h = _pool_layer(h, norm_mix[i], pool_w[i], pool_b[i], pool_scale[i])
        else:
            j = i - n_a
            h = _nsa_layer(h, norm_mix[i], w_qg[j], b_gate[j], w_o[j], shared)
        h = _moe_ple_layer(h.reshape(T, D), p.reshape(depth, T, p.shape[-1]), i, norm_ffn[i], router_g_w[i],
                           router_g_b[i], router_e_w[i], router_e_b[i], moe_w1, moe_w3, moe_w2, norm_ple[i],
                           ple_gate_w[i], ple_gate_b[i], ple_proj[i], final_norm, i == depth - 1).reshape(B, S, D)
    return h
```

```python
import functools

import jax
import jax.numpy as jnp
from jax import lax
from jax.experimental import pallas as pl
from jax.experimental.pallas import tpu as pltpu

F32 = jnp.float32
BF16 = jnp.bfloat16
I32 = jnp.int32

POOL_WINDOWS = (2, 4, 8, 16)
N_HEADS = 16
HEAD_DIM = 64
N_KV_GROUPS = 4
HEADS_PER_GROUP = N_HEADS // N_KV_GROUPS
N_BRANCH = 3
ROPE_DIMS = HEAD_DIM // 4
ROPE_HALF = ROPE_DIMS // 2
ROPE_THETA = 500000.0
CMP_BLOCK = 32
CMP_STRIDE = 16
SEL_BLOCK = 64
N_SELECT = 16
WINDOW = 512
FORCE_BONUS = 1e4
N_EXPERT_GROUPS = 4
EXPERTS_PER_GROUP = 8
N_EXPERTS = N_EXPERT_GROUPS * EXPERTS_PER_GROUP
TOP_K_IN_GROUP = 2
RMS_EPS = 1e-6

LANES = 128
MAX_SEL_BLOCKS = LANES
MASK_BIAS = -30000.0

SEQ_TILE = 512
TOK_TILE = 512
CMB_TILE = 256
MOE_CHUNK = 512
Q_TILE = 256
KEY_TILE = 512
HALO = 16

HIGHEST = lax.Precision.HIGHEST


def _cparams(sem, vmem_mb):
    return pltpu.CompilerParams(dimension_semantics=sem, vmem_limit_bytes=vmem_mb * 1024 * 1024)


def _rms(x, g):
    return x * lax.rsqrt(jnp.mean(x * x, axis=-1, keepdims=True) + RMS_EPS) * g


def _dot(a, b):
    return jnp.dot(a, b, preferred_element_type=F32)


def _dot_nt(a, b):
    return lax.dot_general(a, b, (((1,), (1,)), ((), ())), preferred_element_type=F32)


def _pool_kernel(h_ref, halo_ref, g_ref, w_ref, b_ref, sc_ref, o_ref, *, ts, cg):
    i = pl.program_id(1)
    x = h_ref[0]
    g = g_ref[...]
    xn = _rms(x, g)
    hn = _rms(halo_ref[0], g)
    hn = jnp.where(i > 0, hn, 0.0)
    ext = jnp.concatenate([hn, xn], axis=0)
    t = i * ts + lax.broadcasted_iota(I32, (ts, 1), 0)
    outs = []
    for gi, w in enumerate(POOL_WINDOWS):
        s = ext[:, gi * cg:(gi + 1) * cg]
        k = 1
        while k < w:
            s = s + pltpu.roll(s, k, axis=0)
            k *= 2
        cnt = jnp.minimum(t + 1, w).astype(F32)
        pooled = s[HALO:] / cnt - xn[:, gi * cg:(gi + 1) * cg]
        outs.append(_dot(pooled.astype(BF16), w_ref[gi]))
    y = jnp.concatenate(outs, axis=-1)
    o_ref[0] = x + (y + b_ref[...]) * sc_ref[...]


def _pool_layer(h, g, w, b, sc):
    B, S, D = h.shape
    ts = min(SEQ_TILE, S)
    cg = D // len(POOL_WINDOWS)
    row = lambda v: v.reshape(1, D)
    return pl.pallas_call(
        functools.partial(_pool_kernel, ts=ts, cg=cg),
        out_shape=jax.ShapeDtypeStruct((B, S, D), F32),
        grid=(B, S // ts),
        in_specs=[
            pl.BlockSpec((1, ts, D), lambda b_, i: (b_, i, 0)),
            pl.BlockSpec((1, HALO, D), lambda b_, i: (b_, jnp.maximum(i * (ts // HALO) - 1, 0), 0)),
            pl.BlockSpec((1, D), lambda b_, i: (0, 0)),
            pl.BlockSpec((len(POOL_WINDOWS), cg, cg), lambda b_, i: (0, 0, 0)),
            pl.BlockSpec((1, D), lambda b_, i: (0, 0)),
            pl.BlockSpec((1, D), lambda b_, i: (0, 0)),
        ],
        out_specs=pl.BlockSpec((1, ts, D), lambda b_, i: (b_, i, 0)),
        compiler_params=_cparams(("parallel", "parallel"), 40),
        name="pool_mixer",
    )(h, h, row(g), w.astype(BF16), row(b), row(sc))


def _router_kernel(h_ref, g_ref, w_ref, b_ref, info_ref, cnt_ref, *, tt):
    i = pl.program_id(0)

    @pl.when(i == 0)
    def _():
        cnt_ref[...] = jnp.zeros_like(cnt_ref)

    xn = _rms(h_ref[...], g_ref[...])
    logits = jnp.dot(xn, w_ref[...], precision=HIGHEST, preferred_element_type=F32) + b_ref[...]
    lane = lax.broadcasted_iota(I32, (tt, LANES), 1)
    neg = -jnp.inf
    gl = jnp.where(lane < N_EXPERT_GROUPS, logits, neg)
    gmax = jnp.max(gl, axis=-1, keepdims=True)
    grp = jnp.min(jnp.where(gl == gmax, lane, LANES), axis=-1, keepdims=True)
    gprob = 1.0 / jnp.sum(jnp.exp(gl - gmax), axis=-1, keepdims=True)
    lo = N_EXPERT_GROUPS + grp * EXPERTS_PER_GROUP
    el = jnp.where(lane >= lo, jnp.where(lane < lo + EXPERTS_PER_GROUP, logits, neg), neg)
    v1 = jnp.max(el, axis=-1, keepdims=True)
    i1 = jnp.min(jnp.where(el == v1, lane, LANES), axis=-1, keepdims=True)
    el2 = jnp.where(lane == i1, neg, el)
    v2 = jnp.max(el2, axis=-1, keepdims=True)
    i2 = jnp.min(jnp.where(el2 == v2, lane, LANES), axis=-1, keepdims=True)
    e2 = jnp.exp(v2 - v1)
    w1 = gprob / (1.0 + e2)
    w2 = gprob * e2 / (1.0 + e2)
    oh1 = lane == i1
    oh2 = lane == i2
    oh = jnp.where(oh1, 1.0, jnp.where(oh2, 1.0, 0.0))
    r_ = lax.broadcasted_iota(I32, (tt, tt), 0)
    c_ = lax.broadcasted_iota(I32, (tt, tt), 1)
    tri = jnp.where(r_ > c_, 1.0, 0.0).astype(BF16)
    tot = _dot(tri, oh.astype(BF16)) + cnt_ref[...]
    r1 = jnp.sum(jnp.where(oh1, tot, 0.0), axis=-1, keepdims=True)
    r2 = jnp.sum(jnp.where(oh2, tot, 0.0), axis=-1, keepdims=True)
    cnt_ref[...] = cnt_ref[...] + jnp.sum(oh, axis=0, keepdims=True)
    vals = (i1.astype(F32) - N_EXPERT_GROUPS, i2.astype(F32) - N_EXPERT_GROUPS, w1, w2, r1, r2)
    info = jnp.zeros((tt, LANES), F32)
    for k, v in enumerate(vals):
        info = jnp.where(lane == k, v, info)
    info_ref[...] = info


def _router(h2, g, wr, br):
    T, D = h2.shape
    tt = min(TOK_TILE, T)
    return pl.pallas_call(
        functools.partial(_router_kernel, tt=tt),
        out_shape=(jax.ShapeDtypeStruct((T, LANES), F32), jax.ShapeDtypeStruct((1, LANES), F32)),
        grid=(T // tt,),
        in_specs=[
            pl.BlockSpec((tt, D), lambda i: (i, 0)),
            pl.BlockSpec((1, D), lambda i: (0, 0)),
            pl.BlockSpec((D, LANES), lambda i: (0, 0)),
            pl.BlockSpec((1, LANES), lambda i: (0, 0)),
        ],
        out_specs=(pl.BlockSpec((tt, LANES), lambda i: (i, 0)), pl.BlockSpec((1, LANES), lambda i: (0, 0))),
        compiler_params=_cparams(("arbitrary",), 40),
        name="moe_router",
    )(h2, g.reshape(1, D), wr, br)


def _row_copy(src, s, dst, d, sem):
    return pltpu.make_async_copy(src.at[pl.ds(s, 1)], dst.at[pl.ds(d, 1)], sem)


def _dispatch_kernel(dest_ref, cend_ref, h_ref, g_ref, xs_ref, xn_s, sem, *, tt, n_tok, n_chunks):
    base = pl.program_id(0) * tt

    @pl.when(pl.program_id(0) == 0)
    def _():
        xn_s[...] = jnp.zeros_like(xn_s)

        def tail(e):
            nonempty = cend_ref[e] > (cend_ref[e - 1] if e > 0 else 0)
            row = pl.multiple_of((cend_ref[e] - 1) * tt, tt)
            return nonempty, pltpu.make_async_copy(xn_s, xs_ref.at[pl.ds(row, tt)], sem)

        def unused(c):
            return pltpu.make_async_copy(xn_s, xs_ref.at[pl.ds(pl.multiple_of(c * tt, tt), tt)], sem)

        n_used = cend_ref[N_EXPERTS - 1]
        for e in range(N_EXPERTS):
            nonempty, cp = tail(e)
            pl.when(nonempty)(cp.start)
        lax.fori_loop(n_used, n_chunks, lambda c, z: (unused(c).start(), z)[1], 0)
        for e in range(N_EXPERTS):
            nonempty, cp = tail(e)
            pl.when(nonempty)(cp.wait)
        lax.fori_loop(n_used, n_chunks, lambda c, z: (unused(c).wait(), z)[1], 0)

    xn_s[...] = _rms(h_ref[...], g_ref[...])

    def issue(r, c):
        for k in range(TOP_K_IN_GROUP):
            _row_copy(xn_s, r, xs_ref, dest_ref[k * n_tok + base + r], sem).start()
        return c

    lax.fori_loop(0, tt, issue, 0, unroll=8)
    for k in range(TOP_K_IN_GROUP):
        pltpu.make_async_copy(xn_s, xs_ref.at[pl.ds(0, tt)], sem).wait()


def _dispatch(dest, chunk_end, h2, g, n_rows):
    T, D = h2.shape
    tt = MOE_CHUNK
    assert T % tt == 0
    return pl.pallas_call(
        functools.partial(_dispatch_kernel, tt=tt, n_tok=T, n_chunks=n_rows // tt),
        out_shape=jax.ShapeDtypeStruct((n_rows, D), F32),
        grid_spec=pltpu.PrefetchScalarGridSpec(
            num_scalar_prefetch=2,
            grid=(T // tt,),
            in_specs=[
                pl.BlockSpec((tt, D), lambda i, d, ce: (i, 0)),
                pl.BlockSpec((1, D), lambda i, d, ce: (0, 0)),
            ],
            out_specs=pl.BlockSpec(memory_space=pl.ANY),
            scratch_shapes=[pltpu.VMEM((tt, D), F32), pltpu.SemaphoreType.DMA],
        ),
        compiler_params=_cparams(("arbitrary",), 40),
        name="moe_dispatch",
    )(dest, chunk_end, h2, g.reshape(1, D))


def _expert_kernel(ce_ref, nv_ref, xs_ref, w1_ref, w3_ref, w2_ref, o_ref, w1b, w3b, w2b):
    c = pl.program_id(0)
    e = ce_ref[c]
    prev = ce_ref[jnp.maximum(c - 1, 0)]

    @pl.when(jnp.logical_or(c == 0, e != prev))
    def _():
        w1b[...] = w1_ref[0, 0].astype(BF16)
        w3b[...] = w3_ref[0, 0].astype(BF16)
        w2b[...] = w2_ref[0, 0].astype(BF16)

    @pl.when(c < nv_ref[0])
    def _():
        x = xs_ref[...].astype(BF16)
        a = _dot(x, w1b[...])
        b = _dot(x, w3b[...])
        hc = a * jax.nn.sigmoid(a) * b
        o_ref[...] = _dot(hc.astype(BF16), w2b[...])

    @pl.when(c >= nv_ref[0])
    def _():
        o_ref[...] = jnp.zeros_like(o_ref)


def _experts(chunk_e, n_valid, xs, w1, w3, w2, layer):
    P, D = xs.shape
    F = w1.shape[-1]
    ch = MOE_CHUNK
    rows = lambda c, ce, nv: (jnp.minimum(c, nv[0] - 1), 0)
    wsel = lambda c, ce, nv: (layer, ce[c], 0, 0)
    return pl.pallas_call(
        _expert_kernel,
        out_shape=jax.ShapeDtypeStruct((P, D), F32),
        grid_spec=pltpu.PrefetchScalarGridSpec(
            num_scalar_prefetch=2,
            grid=(P // ch,),
            in_specs=[
                pl.BlockSpec((ch, D), rows),
                pl.BlockSpec((1, 1, D, F), wsel),
                pl.BlockSpec((1, 1, D, F), wsel),
                pl.BlockSpec((1, 1, F, D), wsel),
            ],
            out_specs=pl.BlockSpec((ch, D), lambda c, ce, nv: (c, 0)),
            scratch_shapes=[pltpu.VMEM((D, F), BF16), pltpu.VMEM((D, F), BF16), pltpu.VMEM((F, D), BF16)],
        ),
        compiler_params=_cparams(("arbitrary",), 56),
        name="moe_experts",
    )(chunk_e, n_valid, xs, w1, w3, w2)


def _combine_kernel(dest_ref, h_ref, info_ref, rows_ref, p_ref, g_ref, gw_ref, gb_ref, pw_ref, fn_ref,
                    o_ref, buf, sem, *, tt, n_tok, final):
    base = pl.program_id(0) * tt

    def issue(r, c):
        for k in range(TOP_K_IN_GROUP):
            _row_copy(rows_ref, dest_ref[k * n_tok + base + r], buf.at[k], r, sem).start()
        return c

    lax.fori_loop(0, tt, issue, 0, unroll=8)
    for k in range(TOP_K_IN_GROUP):
        pltpu.make_async_copy(rows_ref.at[pl.ds(0, tt)], buf.at[k], sem).wait()
    info = info_ref[...]
    y = h_ref[...] + info[:, 2:3] * buf[0] + info[:, 3:4] * buf[1]
    hn = _rms(y, g_ref[...])
    gate = jax.nn.sigmoid(_dot(hn.astype(BF16), gw_ref[...]) + gb_ref[...])
    out = y + _dot(p_ref[0].astype(BF16), pw_ref[...]) * gate
    if final:
        out = _rms(out, fn_ref[...])
    o_ref[...] = out


def _combine(dest, h2, info, rows, p3, layer, g, gw, gb, pw, fn, final):
    T, D = h2.shape
    PD = p3.shape[-1]
    tt = min(CMB_TILE, T)
    full = lambda i, d: (0, 0)
    return pl.pallas_call(
        functools.partial(_combine_kernel, tt=tt, n_tok=T, final=final),
        out_shape=jax.ShapeDtypeStruct((T, D), F32),
        grid_spec=pltpu.PrefetchScalarGridSpec(
            num_scalar_prefetch=1,
            grid=(T // tt,),
            in_specs=[
                pl.BlockSpec((tt, D), lambda i, d: (i, 0)),
                pl.BlockSpec((tt, LANES), lambda i, d: (i, 0)),
                pl.BlockSpec(memory_space=pl.ANY),
                pl.BlockSpec((1, tt, PD), lambda i, d: (layer, i, 0)),
                pl.BlockSpec((1, D), full),
                pl.BlockSpec((D, D), full),
                pl.BlockSpec((1, D), full),
                pl.BlockSpec((PD, D), full),
                pl.BlockSpec((1, D), full),
            ],
            out_specs=pl.BlockSpec((tt, D), lambda i, d: (i, 0)),
            scratch_shapes=[pltpu.VMEM((TOP_K_IN_GROUP, tt, D), F32), pltpu.SemaphoreType.DMA],
        ),
        compiler_params=_cparams(("arbitrary",), 40),
        name="moe_combine_ple",
    )(dest, h2, info, rows, p3, g.reshape(1, D), gw.astype(BF16), gb.reshape(1, D), pw.astype(BF16),
      fn.reshape(1, D))


def _moe_ple_layer(h2, p3, layer, norm_ffn, rg_w, rg_b, re_w, re_b, w1, w3, w2, norm_ple, gate_w, gate_b, ple_proj,
                   final_norm, final):
    T, D = h2.shape
    A = T * TOP_K_IN_GROUP
    pad = LANES - N_EXPERT_GROUPS - N_EXPERTS
    wr = jnp.concatenate([rg_w, re_w, jnp.zeros((D, pad), F32)], axis=1)
    br = jnp.concatenate([rg_b, re_b, jnp.zeros((pad,), F32)]).reshape(1, LANES)
    info, cnt = _router(h2, norm_ffn, wr, br)
    counts = cnt[0, N_EXPERT_GROUPS:N_EXPERT_GROUPS + N_EXPERTS].astype(I32)
    n_chunks_e = (counts + MOE_CHUNK - 1) // MOE_CHUNK
    chunk_end = jnp.cumsum(n_chunks_e)
    pstarts = (chunk_end - n_chunks_e) * MOE_CHUNK
    n_chunks = -(-A // MOE_CHUNK) + N_EXPERTS
    n_valid = chunk_end[-1:].astype(I32)
    cidx = jnp.minimum(jnp.arange(n_chunks, dtype=I32), n_valid[0] - 1)
    chunk_e = jnp.sum((chunk_end[None, :] <= cidx[:, None]).astype(I32), axis=1)
    chunk_e = jnp.minimum(chunk_e, N_EXPERTS - 1)
    e_idx = info[:, 0:TOP_K_IN_GROUP].astype(I32)
    rank = info[:, 4:4 + TOP_K_IN_GROUP].astype(I32)
    dest = (pstarts[e_idx] + rank).T.reshape(A)
    xs = _dispatch(dest, chunk_end.astype(I32), h2, norm_ffn, n_chunks * MOE_CHUNK)
    rows = _experts(chunk_e, n_valid, xs, w1, w3, w2, layer)
    return _combine(dest, h2, info, rows, p3, layer, norm_ple, gate_w, gate_b, ple_proj, final_norm, final)


def _rope_rows(xt, cos, sin):
    x1 = xt[0:ROPE_HALF]
    x2 = xt[ROPE_HALF:ROPE_DIMS]
    return jnp.concatenate([x1 * cos - x2 * sin, x2 * cos + x1 * sin, xt[ROPE_DIMS:]], axis=0)


def _kv_kernel(h_ref, g_ref, wn_ref, wt_ref, cos_ref, sin_ref, cv_ref, kst_ref, vs_ref, kwt_ref, vw_ref, *, ts):
    i = pl.program_id(1)
    kvw = N_KV_GROUPS * HEAD_DIM
    hn = _rms(h_ref[0], g_ref[...]).astype(BF16)
    nat = _dot(hn, wn_ref[...])
    cv_ref[0] = nat[:, 0:2 * kvw]
    tr = _dot_nt(wt_ref[...], hn)
    cos = cos_ref[...]
    sin = sin_ref[...]
    blk = lax.broadcasted_iota(I32, (MAX_SEL_BLOCKS, ts), 0)
    pos = i * ts + lax.broadcasted_iota(I32, (MAX_SEL_BLOCKS, ts), 1)
    onehot = jnp.where(pos // SEL_BLOCK == blk, 1.0, 0.0).astype(BF16)
    for g in range(N_KV_GROUPS):
        c0 = g * HEAD_DIM
        kst_ref[0, g, 0:MAX_SEL_BLOCKS, :] = onehot
        kst_ref[0, g, MAX_SEL_BLOCKS:, :] = _rope_rows(tr[c0:c0 + HEAD_DIM], cos, sin).astype(BF16)
        kwt_ref[0, g] = _rope_rows(tr[kvw + c0:kvw + c0 + HEAD_DIM], cos, sin).astype(BF16)
        vs_ref[0, g] = nat[:, 2 * kvw + c0:2 * kvw + c0 + HEAD_DIM].astype(BF16)
        vw_ref[0, g] = nat[:, 3 * kvw + c0:3 * kvw + c0 + HEAD_DIM].astype(BF16)


def _rope_tables(pos):
    inv = jnp.float32(ROPE_THETA) ** (-jnp.arange(ROPE_HALF, dtype=F32) * 2.0 / ROPE_DIMS)
    ang = pos.astype(F32)[:, None] * inv[None, :]
    return jnp.cos(ang), jnp.sin(ang)


def _shared_kv_proj(h, kv_norm, w_kv):
    B, S, D = h.shape
    G, dh = N_KV_GROUPS, HEAD_DIM
    kvw = G * dh
    ts = min(SEQ_TILE, S)
    br = lambda k: w_kv[:, k * kvw:(k + 1) * kvw]
    w_nat = jnp.concatenate([br(0), br(1), br(3), br(5)], axis=1).astype(BF16)
    w_tr = jnp.concatenate([br(2), br(4)], axis=1).T.astype(BF16)
    cos, sin = _rope_tables(jnp.arange(S))
    kd = MAX_SEL_BLOCKS + dh
    return pl.pallas_call(
        functools.partial(_kv_kernel, ts=ts),
        out_shape=(
            jax.ShapeDtypeStruct((B, S, 2 * kvw), F32),
            jax.ShapeDtypeStruct((B, G, kd, S), BF16),
            jax.ShapeDtypeStruct((B, G, S, dh), BF16),
            jax.ShapeDtypeStruct((B, G, dh, S), BF16),
            jax.ShapeDtypeStruct((B, G, S, dh), BF16),
        ),
        grid=(B, S // ts),
        in_specs=[
            pl.BlockSpec((1, ts, D), lambda b, i: (b, i, 0)),
            pl.BlockSpec((1, D), lambda b, i: (0, 0)),
            pl.BlockSpec((D, 4 * kvw), lambda b, i: (0, 0)),
            pl.BlockSpec((2 * kvw, D), lambda b, i: (0, 0)),
            pl.BlockSpec((ROPE_HALF, ts), lambda b, i: (0, i)),
            pl.BlockSpec((ROPE_HALF, ts), lambda b, i: (0, i)),
        ],
        out_specs=(
            pl.BlockSpec((1, ts, 2 * kvw), lambda b, i: (b, i, 0)),
            pl.BlockSpec((1, G, kd, ts), lambda b, i: (b, 0, 0, i)),
            pl.BlockSpec((1, G, ts, dh), lambda b, i: (b, 0, i, 0)),
            pl.BlockSpec((1, G, dh, ts), lambda b, i: (b, 0, 0, i)),
            pl.BlockSpec((1, G, ts, dh), lambda b, i: (b, 0, i, 0)),
        ),
        compiler_params=_cparams(("parallel", "parallel"), 48),
        name="shared_kv_proj",
    )(h, kv_norm.reshape(1, D), w_nat, w_tr, cos.T, sin.T)


def _compress_kernel(x_ref, pos_ref, w1_ref, b1_ref, w2_ref, b2_ref, cos_ref, sin_ref, o_ref, *, nh, keys):
    x = x_ref[0, 0, 0]
    a = _dot((x + pos_ref[0:1]).astype(BF16), w1_ref[0])
    b = _dot((x + pos_ref[1:2]).astype(BF16), w1_ref[1])
    hid = jax.nn.gelu(a + pltpu.roll(b, nh - 1, axis=0) + b1_ref[...]).astype(BF16)
    if keys:
        out = _dot_nt(w2_ref[...], hid) + b2_ref[...]
        o_ref[0, 0] = _rope_rows(out, cos_ref[...], sin_ref[...]).astype(BF16)
    else:
        o_ref[0, 0] = (_dot(hid, w2_ref[...]) + b2_ref[...]).astype(BF16)


def _compress(halves, which, pos_emb, w1, b1, w2, b2, keys):
    _, B, G, nh, hw = halves.shape
    dh = HEAD_DIM
    hidden = w1.shape[-1]
    pos2 = pos_emb.reshape(2, hw)
    w1s = w1.reshape(2, hw, hidden).astype(BF16)
    cos, sin = _rope_tables(jnp.arange(nh) * CMP_STRIDE + CMP_BLOCK - 1)
    if keys:
        w2a, b2a = w2.T.astype(BF16), b2.reshape(dh, 1)
        out_shape, out_block, out_idx = (B, G, dh, nh), (1, 1, dh, nh), lambda b, g: (b, g, 0, 0)
    else:
        w2a, b2a = w2.astype(BF16), b2.reshape(1, dh)
        out_shape, out_block, out_idx = (B, G, nh, dh), (1, 1, nh, dh), lambda b, g: (b, g, 0, 0)
    c2 = lambda b, g: (0, 0)
    return pl.pallas_call(
        functools.partial(_compress_kernel, nh=nh, keys=keys),
        out_shape=jax.ShapeDtypeStruct(out_shape, BF16),
        grid=(B, G),
        in_specs=[
            pl.BlockSpec((1, 1, 1, nh, hw), lambda b, g: (which, b, g, 0, 0)),
            pl.BlockSpec((2, hw), c2),
            pl.BlockSpec((2, hw, hidden), lambda b, g: (0, 0, 0)),
            pl.BlockSpec((1, hidden), c2),
            pl.BlockSpec(w2a.shape, c2),
            pl.BlockSpec(b2a.shape, c2),
            pl.BlockSpec((ROPE_HALF, nh), c2),
            pl.BlockSpec((ROPE_HALF, nh), c2),
        ],
        out_specs=pl.BlockSpec(out_block, out_idx),
        compiler_params=_cparams(("parallel", "parallel"), 40),
        name="compress_k" if keys else "compress_v",
    )(halves, pos2, w1s, b1.reshape(1, hidden), w2a, b2a, cos.T, sin.T)


def _qproj_kernel(h_ref, g_ref, wq_ref, wg_ref, bg_ref, c_ref, s1_ref, s2_ref, q_ref, gt_ref):
    xn = _rms(h_ref[0], g_ref[...]).astype(BF16)
    q = _dot(xn, wq_ref[...])
    c, s1, s2 = c_ref[...], s1_ref[...], s2_ref[...]
    for k in range(q.shape[-1] // LANES):
        x = q[:, k * LANES:(k + 1) * LANES]
        r = x * c + pltpu.roll(x, ROPE_HALF, axis=1) * s1 + pltpu.roll(x, LANES - ROPE_HALF, axis=1) * s2
        q_ref[0, :, k * LANES:(k + 1) * LANES] = r.astype(BF16)
    gt_ref[0] = jax.nn.sigmoid(_dot(xn, wg_ref[...]) + bg_ref[...])


def _q_proj(h, g, w_qg, b_gate):
    B, S, D = h.shape
    HD = N_HEADS * HEAD_DIM
    ng = N_HEADS * N_BRANCH
    ts = min(SEQ_TILE, S)
    wq = w_qg[:, :HD].astype(BF16)
    wg = jnp.pad(w_qg[:, HD:], ((0, 0), (0, LANES - ng))).astype(BF16)
    bg = jnp.pad(b_gate, (0, LANES - ng)).reshape(1, LANES)
    cos, sin = _rope_tables(jnp.arange(S))
    scale = HEAD_DIM ** -0.5
    ones = jnp.ones((S, HEAD_DIM - ROPE_DIMS), F32)
    zeros = jnp.zeros((S, HEAD_DIM - ROPE_DIMS), F32)
    zh = jnp.zeros((S, ROPE_HALF), F32)
    two = lambda a: jnp.concatenate([a, a], axis=1) * scale
    c = two(jnp.concatenate([cos, cos, ones], axis=1))
    s1 = two(jnp.concatenate([zh, sin, zeros], axis=1))
    s2 = two(jnp.concatenate([-sin, zh, zeros], axis=1))
    tab = pl.BlockSpec((ts, LANES), lambda b, i: (i, 0))
    return pl.pallas_call(
        _qproj_kernel,
        out_shape=(jax.ShapeDtypeStruct((B, S, HD), BF16), jax.ShapeDtypeStruct((B, S, LANES), F32)),
        grid=(B, S // ts),
        in_specs=[
            pl.BlockSpec((1, ts, D), lambda b, i: (b, i, 0)),
            pl.BlockSpec((1, D), lambda b, i: (0, 0)),
            pl.BlockSpec((D, HD), lambda b, i: (0, 0)),
            pl.BlockSpec((D, LANES), lambda b, i: (0, 0)),
            pl.BlockSpec((1, LANES), lambda b, i: (0, 0)),
            tab, tab, tab,
        ],
        out_specs=(pl.BlockSpec((1, ts, HD), lambda b, i: (b, i, 0)),
                   pl.BlockSpec((1, ts, LANES), lambda b, i: (b, i, 0))),
        compiler_params=_cparams(("parallel", "parallel"), 40),
        name="nsa_q_proj",
    )(h, g.reshape(1, D), wq, wg, bg, c, s1, s2)


def _stack_heads(qb):
    return jnp.concatenate([qb[:, r * HEAD_DIM:(r + 1) * HEAD_DIM] for r in range(HEADS_PER_GROUP)], axis=0)


def _gated_unstack(o, gates, g, branch, tq):
    lane = lax.broadcasted_iota(I32, gates.shape, 1)
    cols = []
    for r in range(HEADS_PER_GROUP):
        col = (g * HEADS_PER_GROUP + r) * N_BRANCH + branch
        gate = jnp.sum(jnp.where(lane == col, gates, 0.0), axis=-1, keepdims=True)
        cols.append(o[r * tq:(r + 1) * tq] * gate)
    return jnp.concatenate(cols, axis=-1)


def _query_pos(i, tq):
    rows = lax.broadcasted_iota(I32, (HEADS_PER_GROUP * tq, 1), 0)
    return i * tq + (rows & (tq - 1))


def _cmp_kernel(q_ref, gt_ref, k_ref, v_ref, o_ref, bias_ref, *, tq, nc):
    g = pl.program_id(1)
    i = pl.program_id(2)
    qs = _stack_heads(q_ref[0])
    s = _dot(qs, k_ref[0, 0])
    t = _query_pos(i, tq)
    n = lax.broadcasted_iota(I32, (1, nc), 1)
    s = jnp.where(n * CMP_STRIDE + (CMP_BLOCK - 1) <= t, s, -jnp.inf)
    m = jnp.max(s, axis=-1, keepdims=True)
    m = jnp.where(m == -jnp.inf, 0.0, m)
    e = jnp.exp(s - m)
    p = e / jnp.maximum(jnp.sum(e, axis=-1, keepdims=True), 1e-30)
    o = _dot(p.astype(BF16), v_ref[0, 0])
    o_ref[0] = _gated_unstack(o, gt_ref[0], g, 0, tq).astype(BF16)
    ps = p[0:tq]
    for r in range(1, HEADS_PER_GROUP):
        ps = ps + p[r * tq:(r + 1) * tq]
    nn = lax.broadcasted_iota(I32, (nc, MAX_SEL_BLOCKS), 0) * CMP_STRIDE
    jj = lax.broadcasted_iota(I32, (nc, MAX_SEL_BLOCKS), 1) * SEL_BLOCK
    c2s = jnp.where(nn < jj + SEL_BLOCK, jnp.where(nn + CMP_BLOCK > jj, 1.0, 0.0), 0.0)
    imp = jnp.dot(ps, c2s, precision=HIGHEST, preferred_element_type=F32)
    tq_pos = i * tq + lax.broadcasted_iota(I32, (tq, 1), 0)
    cur = tq_pos // SEL_BLOCK
    j = lax.broadcasted_iota(I32, (tq, MAX_SEL_BLOCKS), 1)
    valid = j <= cur
    bonus = jnp.where(j == 0, FORCE_BONUS, jnp.where(j == cur, FORCE_BONUS, jnp.where(j == cur - 1, FORCE_BONUS, 0.0)))
    work = jnp.where(valid, imp + bonus, -jnp.inf)
    sel = jnp.zeros((tq, MAX_SEL_BLOCKS), F32)
    for _ in range(N_SELECT):
        mx = jnp.max(work, axis=-1, keepdims=True)
        idx = jnp.min(jnp.where(work == mx, j, MAX_SEL_BLOCKS), axis=-1, keepdims=True)
        pick = j == idx
        sel = jnp.where(pick, 1.0, sel)
        work = jnp.where(pick, -jnp.inf, work)
    bias_ref[0, 0] = jnp.where(valid, jnp.where(sel > 0.0, 0.0, MASK_BIAS), MASK_BIAS).astype(BF16)


def _cmp_attn(q, gates, kct, vc):
    B, S, HD = q.shape
    G, dh = N_KV_GROUPS, HEAD_DIM
    gw = HD // G
    nc = kct.shape[-1]
    tq = min(Q_TILE, S)
    return pl.pallas_call(
        functools.partial(_cmp_kernel, tq=tq, nc=nc),
        out_shape=(jax.ShapeDtypeStruct((B, S, HD), BF16), jax.ShapeDtypeStruct((B, G, S, MAX_SEL_BLOCKS), BF16)),
        grid=(B, G, S // tq),
        in_specs=[
            pl.BlockSpec((1, tq, gw), lambda b, g, i: (b, i, g)),
            pl.BlockSpec((1, tq, LANES), lambda b, g, i: (b, i, 0)),
            pl.BlockSpec((1, 1, dh, nc), lambda b, g, i: (b, g, 0, 0)),
            pl.BlockSpec((1, 1, nc, dh), lambda b, g, i: (b, g, 0, 0)),
        ],
        out_specs=(pl.BlockSpec((1, tq, gw), lambda b, g, i: (b, i, g)),
                   pl.BlockSpec((1, 1, tq, MAX_SEL_BLOCKS), lambda b, g, i: (b, g, i, 0))),
        compiler_params=_cparams(("parallel", "parallel", "parallel"), 40),
        name="nsa_compressed",
    )(q, gates, kct, vc)


def _sel_kernel(q_ref, bias_ref, gt_ref, k_ref, v_ref, o_ref, qa, m_s, l_s, acc, *, tq, tk):
    g = pl.program_id(1)
    i = pl.program_id(2)
    qb = q_ref[0]
    bias = bias_ref[0, 0]
    for r in range(HEADS_PER_GROUP):
        qa[r * tq:(r + 1) * tq, 0:MAX_SEL_BLOCKS] = bias
        qa[r * tq:(r + 1) * tq, MAX_SEL_BLOCKS:] = qb[:, r * HEAD_DIM:(r + 1) * HEAD_DIM]
    m_s[...] = jnp.full_like(m_s, -jnp.inf)
    l_s[...] = jnp.zeros_like(l_s)
    acc[...] = jnp.zeros_like(acc)
    t = _query_pos(i, tq)
    lane = lax.broadcasted_iota(I32, (1, tk), 1)

    def body(jt, c):
        k0 = pl.multiple_of(jt * tk, tk)
        s = _dot(qa[...], k_ref[0, 0, :, pl.ds(k0, tk)])
        s = jnp.where(k0 + lane <= t, s, -jnp.inf)
        m_old = m_s[...]
        m_new = jnp.maximum(m_old, jnp.max(s, axis=-1, keepdims=True))
        alpha = jnp.exp(m_old - m_new)
        p = jnp.exp(s - m_new)
        l_s[...] = alpha * l_s[...] + jnp.sum(p, axis=-1, keepdims=True)
        acc[...] = alpha * acc[...] + _dot(p.astype(BF16), v_ref[0, 0, pl.ds(k0, tk), :])
        m_s[...] = m_new
        return c

    lax.fori_loop(0, (i * tq + tq + tk - 1) // tk, body, 0)
    o_ref[0] = _gated_unstack(acc[...] / l_s[...], gt_ref[0], g, 1, tq).astype(BF16)


def _sel_attn(q, bias, gates, kst, vs):
    B, S, HD = q.shape
    G, dh = N_KV_GROUPS, HEAD_DIM
    gw = HD // G
    kd = kst.shape[2]
    tq = min(Q_TILE, S)
    tk = min(KEY_TILE, S)
    rows = HEADS_PER_GROUP * tq
    return pl.pallas_call(
        functools.partial(_sel_kernel, tq=tq, tk=tk),
        out_shape=jax.ShapeDtypeStruct((B, S, HD), BF16),
        grid=(B, G, S // tq),
        in_specs=[
            pl.BlockSpec((1, tq, gw), lambda b, g, i: (b, i, g)),
            pl.BlockSpec((1, 1, tq, MAX_SEL_BLOCKS), lambda b, g, i: (b, g, i, 0)),
            pl.BlockSpec((1, tq, LANES), lambda b, g, i: (b, i, 0)),
            pl.BlockSpec((1, 1, kd, S), lambda b, g, i: (b, g, 0, 0)),
            pl.BlockSpec((1, 1, S, dh), lambda b, g, i: (b, g, 0, 0)),
        ],
        out_specs=pl.BlockSpec((1, tq, gw), lambda b, g, i: (b, i, g)),
        scratch_shapes=[pltpu.VMEM((rows, kd), BF16), pltpu.VMEM((rows, 1), F32), pltpu.VMEM((rows, 1), F32),
                        pltpu.VMEM((rows, dh), F32)],
        compiler_params=_cparams(("parallel", "parallel", "arbitrary"), 48),
        name="nsa_selected",
    )(q, bias, gates, kst, vs)


def _win_kernel(q_ref, gt_ref, k_ref, v_ref, o_ref, *, tq, wb):
    g = pl.program_id(1)
    i = pl.program_id(2)
    k0 = pl.multiple_of(jnp.maximum(i * tq - WINDOW, 0), tq)
    qs = _stack_heads(q_ref[0])
    s = _dot(qs, k_ref[0, 0, :, pl.ds(k0, wb)])
    t = _query_pos(i, tq)
    kpos = k0 + lax.broadcasted_iota(I32, (1, wb), 1)
    s = jnp.where(kpos <= t, jnp.where(kpos > t - WINDOW, s, -jnp.inf), -jnp.inf)
    m = jnp.max(s, axis=-1, keepdims=True)
    e = jnp.exp(s - m)
    p = e / jnp.sum(e, axis=-1, keepdims=True)
    o = _dot(p.astype(BF16), v_ref[0, 0, pl.ds(k0, wb), :])
    o_ref[0] = _gated_unstack(o, gt_ref[0], g, 2, tq).astype(BF16)


def _win_attn(q, gates, kwt, vw):
    B, S, HD = q.shape
    G, dh = N_KV_GROUPS, HEAD_DIM
    gw = HD // G
    tq = min(Q_TILE, S)
    wb = WINDOW + tq
    assert S >= wb and tq % LANES == 0
    return pl.pallas_call(
        functools.partial(_win_kernel, tq=tq, wb=wb),
        out_shape=jax.ShapeDtypeStruct((B, S, HD), BF16),
        grid=(B, G, S // tq),
        in_specs=[
            pl.BlockSpec((1, tq, gw), lambda b, g, i: (b, i, g)),
            pl.BlockSpec((1, tq, LANES), lambda b, g, i: (b, i, 0)),
            pl.BlockSpec((1, 1, dh, S), lambda b, g, i: (b, g, 0, 0)),
            pl.BlockSpec((1, 1, S, dh), lambda b, g, i: (b, g, 0, 0)),
        ],
        out_specs=pl.BlockSpec((1, tq, gw), lambda b, g, i: (b, i, g)),
        compiler_params=_cparams(("parallel", "parallel", "parallel"), 40),
        name="nsa_window",
    )(q, gates, kwt, vw)


LOG2E = 1.4426950408889634
Q_SCALE = HEAD_DIM ** -0.5 * LOG2E
V_ROWS = HEAD_DIM + 16
KEY_AUG = 2 * LANES
SEL_CHAINS = 1
NEG_BIG = -1e30


def _lane_rope_tables(pos):
    cos, sin = _rope_tables(pos)
    n = pos.shape[0]
    ones = jnp.ones((n, HEAD_DIM - ROPE_DIMS), F32)
    zeros = jnp.zeros((n, HEAD_DIM - ROPE_DIMS), F32)
    zh = jnp.zeros((n, ROPE_HALF), F32)
    two = lambda a: jnp.concatenate([a, a], axis=1)
    return (two(jnp.concatenate([cos, cos, ones], axis=1)), two(jnp.concatenate([zh, sin, zeros], axis=1)),
            two(jnp.concatenate([-sin, zh, zeros], axis=1)))


def _rope_lanes(x, c, s1, s2):
    return x * c + pltpu.roll(x, ROPE_HALF, axis=1) * s1 + pltpu.roll(x, LANES - ROPE_HALF, axis=1) * s2


def _kv2_kernel(h_ref, g_ref, wn_ref, wt_ref, c_ref, s1_ref, s2_ref, cv_ref, ksa_ref, vst_ref, kw_ref, vwt_ref, *, ts):
    i = pl.program_id(1)
    kvw = N_KV_GROUPS * HEAD_DIM
    hn = _rms(h_ref[0], g_ref[...]).astype(BF16)
    nat = _dot(hn, wn_ref[...])
    cv_ref[0] = nat[:, 0:2 * kvw]
    tr = _dot_nt(wt_ref[...], hn)
    c, s1, s2 = c_ref[...], s1_ref[...], s2_ref[...]
    roped = [_rope_lanes(nat[:, 2 * kvw + k * LANES:2 * kvw + (k + 1) * LANES], c, s1, s2)
             for k in range(2 * kvw // LANES)]
    lane = lax.broadcasted_iota(I32, (ts, KEY_AUG), 1)
    pos = i * ts + lax.broadcasted_iota(I32, (ts, KEY_AUG), 0)
    onehot = jnp.where(lane - HEAD_DIM == pos // SEL_BLOCK, 1.0, 0.0).astype(BF16)
    ones_row = jnp.where(lax.broadcasted_iota(I32, (V_ROWS - HEAD_DIM, ts), 0) == 0, 1.0, 0.0).astype(BF16)
    per_tile = LANES // HEAD_DIM
    for g in range(N_KV_GROUPS):
        lo = (g % per_tile) * HEAD_DIM
        ksa_ref[0, g] = onehot
        ksa_ref[0, g, :, 0:HEAD_DIM] = roped[g // per_tile][:, lo:lo + HEAD_DIM].astype(BF16)
        kw_ref[0, g] = roped[N_KV_GROUPS // per_tile + g // per_tile][:, lo:lo + HEAD_DIM].astype(BF16)
        for ref, base in ((vst_ref, 0), (vwt_ref, kvw)):
            ref[0, g, 0:HEAD_DIM, :] = tr[base + g * HEAD_DIM:base + (g + 1) * HEAD_DIM].astype(BF16)
            ref[0, g, HEAD_DIM:, :] = ones_row


def _shared_kv_proj2(h, kv_norm, w_kv):
    B, S, D = h.shape
    G, dh = N_KV_GROUPS, HEAD_DIM
    kvw = G * dh
    ts = min(SEQ_TILE, S)
    br = lambda k: w_kv[:, k * kvw:(k + 1) * kvw]
    w_nat = jnp.concatenate([br(0), br(1), br(2), br(4)], axis=1).astype(BF16)
    w_tr = jnp.concatenate([br(3), br(5)], axis=1).T.astype(BF16)
    tabs = _lane_rope_tables(jnp.arange(S))
    tab = pl.BlockSpec((ts, LANES), lambda b, i: (i, 0))
    return pl.pallas_call(
        functools.partial(_kv2_kernel, ts=ts),
        out_shape=(
            jax.ShapeDtypeStruct((B, S, 2 * kvw), F32),
            jax.ShapeDtypeStruct((B, G, S, KEY_AUG), BF16),
            jax.ShapeDtypeStruct((B, G, V_ROWS, S), BF16),
            jax.ShapeDtypeStruct((B, G, S, dh), BF16),
            jax.ShapeDtypeStruct((B, G, V_ROWS, S), BF16),
        ),
        grid=(B, S // ts),
        in_specs=[
            pl.BlockSpec((1, ts, D), lambda b, i: (b, i, 0)),
            pl.BlockSpec((1, D), lambda b, i: (0, 0)),
            pl.BlockSpec((D, 4 * kvw), lambda b, i: (0, 0)),
            pl.BlockSpec((2 * kvw, D), lambda b, i: (0, 0)),
            tab, tab, tab,
        ],
        out_specs=(
            pl.BlockSpec((1, ts, 2 * kvw), lambda b, i: (b, i, 0)),
            pl.BlockSpec((1, G, ts, KEY_AUG), lambda b, i: (b, 0, i, 0)),
            pl.BlockSpec((1, G, V_ROWS, ts), lambda b, i: (b, 0, 0, i)),
            pl.BlockSpec((1, G, ts, dh), lambda b, i: (b, 0, i, 0)),
            pl.BlockSpec((1, G, V_ROWS, ts), lambda b, i: (b, 0, 0, i)),
        ),
        compiler_params=_cparams(("parallel", "parallel"), 48),
        name="shared_kv_proj",
    )(h, kv_norm.reshape(1, D), w_nat, w_tr, *tabs)


def _compress2_kernel(x_ref, pos_ref, w1_ref, b1_ref, w2_ref, b2_ref, c_ref, s1_ref, s2_ref, o_ref, *, nh, keys):
    x = x_ref[0, 0, 0]
    a = _dot((x + pos_ref[0:1]).astype(BF16), w1_ref[0])
    b = _dot((x + pos_ref[1:2]).astype(BF16), w1_ref[1])
    hid = jax.nn.gelu(a + pltpu.roll(b, nh - 1, axis=0) + b1_ref[...]).astype(BF16)
    if keys:
        out = _rope_lanes(_dot(hid, w2_ref[...]) + b2_ref[...], c_ref[...], s1_ref[...], s2_ref[...])
        o_ref[0, 0] = out[:, 0:HEAD_DIM].astype(BF16)
    else:
        o_ref[0, 0] = (_dot_nt(w2_ref[...], hid) + b2_ref[...]).astype(BF16)


def _compress2(halves, which, pos_emb, w1, b1, w2, b2, keys):
    _, B, G, nh, hw = halves.shape
    dh = HEAD_DIM
    hidden = w1.shape[-1]
    pos2 = pos_emb.reshape(2, hw)
    w1s = w1.reshape(2, hw, hidden).astype(BF16)
    tabs = _lane_rope_tables(jnp.arange(nh) * CMP_STRIDE + CMP_BLOCK - 1)
    if keys:
        w2a = jnp.pad(w2, ((0, 0), (0, LANES - dh))).astype(BF16)
        b2a = jnp.pad(b2, (0, LANES - dh)).reshape(1, LANES)
        out_shape, out_block = (B, G, nh, dh), (1, 1, nh, dh)
    else:
        w2a, b2a = w2.T.astype(BF16), b2.reshape(dh, 1)
        out_shape, out_block = (B, G, dh, nh), (1, 1, dh, nh)
    c2 = lambda b, g: (0, 0)
    tab = pl.BlockSpec((nh, LANES), c2)
    return pl.pallas_call(
        functools.partial(_compress2_kernel, nh=nh, keys=keys),
        out_shape=jax.ShapeDtypeStruct(out_shape, BF16),
        grid=(B, G),
        in_specs=[
            pl.BlockSpec((1, 1, 1, nh, hw), lambda b, g: (which, b, g, 0, 0)),
            pl.BlockSpec((2, hw), c2),
            pl.BlockSpec((2, hw, hidden), lambda b, g: (0, 0, 0)),
            pl.BlockSpec((1, hidden), c2),
            pl.BlockSpec(w2a.shape, c2),
            pl.BlockSpec(b2a.shape, c2),
            tab, tab, tab,
        ],
        out_specs=pl.BlockSpec(out_block, lambda b, g: (b, g, 0, 0)),
        compiler_params=_cparams(("parallel", "parallel"), 40),
        name="compress_k" if keys else "compress_v",
    )(halves, pos2, w1s, b1.reshape(1, hidden), w2a, b2a, *tabs)


def _qproj2_kernel(h_ref, g_ref, wqt_ref, wg_ref, bg_ref, cos_ref, sin_ref, qt_ref, gt_ref):
    xn = _rms(h_ref[0], g_ref[...]).astype(BF16)
    tr = _dot_nt(wqt_ref[...], xn)
    cos, sin = cos_ref[...], sin_ref[...]
    for hd in range(N_HEADS):
        rows = slice(hd * HEAD_DIM, (hd + 1) * HEAD_DIM)
        qt_ref[0, rows, :] = (_rope_rows(tr[rows], cos, sin) * Q_SCALE).astype(BF16)
    gt_ref[0] = jax.nn.sigmoid(_dot(xn, wg_ref[...]) + bg_ref[...])


def _q_proj2(h, g, w_qg, b_gate):
    B, S, D = h.shape
    HD = N_HEADS * HEAD_DIM
    ng = N_HEADS * N_BRANCH
    ts = min(SEQ_TILE, S)
    wqt = w_qg[:, :HD].T.astype(BF16)
    wg = jnp.pad(w_qg[:, HD:], ((0, 0), (0, LANES - ng))).astype(BF16)
    bg = jnp.pad(b_gate, (0, LANES - ng)).reshape(1, LANES)
    cos, sin = _rope_tables(jnp.arange(S))
    tab = pl.BlockSpec((ROPE_HALF, ts), lambda b, i: (0, i))
    return pl.pallas_call(
        _qproj2_kernel,
        out_shape=(jax.ShapeDtypeStruct((B, HD, S), BF16), jax.ShapeDtypeStruct((B, S, LANES), F32)),
        grid=(B, S // ts),
        in_specs=[
            pl.BlockSpec((1, ts, D), lambda b, i: (b, i, 0)),
            pl.BlockSpec((1, D), lambda b, i: (0, 0)),
            pl.BlockSpec((HD, D), lambda b, i: (0, 0)),
            pl.BlockSpec((D, LANES), lambda b, i: (0, 0)),
            pl.BlockSpec((1, LANES), lambda b, i: (0, 0)),
            tab, tab,
        ],
        out_specs=(pl.BlockSpec((1, HD, ts), lambda b, i: (b, 0, i)),
                   pl.BlockSpec((1, ts, LANES), lambda b, i: (b, i, 0))),
        compiler_params=_cparams(("parallel", "parallel"), 40),
        name="nsa_q_proj",
    )(h, g.reshape(1, D), wqt, wg, bg, cos.T, sin.T)


def _heads_on_lanes(qt):
    return jnp.concatenate([qt[r * HEAD_DIM:(r + 1) * HEAD_DIM] for r in range(HEADS_PER_GROUP)], axis=1)


def _lane_query_pos(i, tq):
    lanes = lax.broadcasted_iota(I32, (1, HEADS_PER_GROUP * tq), 1)
    return i * tq + (lanes & (tq - 1))


def _finish_heads(acc_t, gates, g, branch, tq, denom_row):
    lane = lax.broadcasted_iota(I32, gates.shape, 1)
    rows = acc_t.shape[0]
    cols = []
    for r in range(HEADS_PER_GROUP):
        blk = acc_t[:, r * tq:(r + 1) * tq]
        nat = jnp.concatenate([blk, jnp.zeros((tq - rows, tq), F32)], axis=0).T
        col = (g * HEADS_PER_GROUP + r) * N_BRANCH + branch
        scale = jnp.sum(jnp.where(lane == col, gates, 0.0), axis=-1, keepdims=True)
        if denom_row is not None:
            scale = scale / nat[:, denom_row:denom_row + 1]
        cols.append(nat[:, 0:HEAD_DIM] * scale)
    return jnp.concatenate(cols, axis=-1)


def _cmp2_kernel(qt_ref, gt_ref, k_ref, vt_ref, c2s_ref, o_ref, bias_ref, *, tq, nc):
    g = pl.program_id(1)
    i = pl.program_id(2)
    s = _dot(k_ref[0, 0], _heads_on_lanes(qt_ref[0]))
    t = _lane_query_pos(i, tq)
    n = lax.broadcasted_iota(I32, (nc, 1), 0)
    s = jnp.where(n * CMP_STRIDE + (CMP_BLOCK - 1) <= t, s, -jnp.inf)
    m = jnp.max(s, axis=0, keepdims=True)
    m = jnp.where(m == -jnp.inf, 0.0, m)
    e = jnp.exp2(s - m)
    p = e * (1.0 / jnp.maximum(jnp.sum(e, axis=0, keepdims=True), 1e-30))
    ot = _dot(vt_ref[0, 0], p.astype(BF16))
    o_ref[0] = _finish_heads(ot, gt_ref[0], g, 0, tq, None).astype(BF16)
    ps = p[:, 0:tq]
    for r in range(1, HEADS_PER_GROUP):
        ps = ps + p[:, r * tq:(r + 1) * tq]
    hi = ps.astype(BF16)
    rem = ps - hi.astype(F32)
    mid = rem.astype(BF16)
    lo = (rem - mid.astype(F32)).astype(BF16)
    c2s = c2s_ref[...]
    imp = _dot(c2s, hi) + _dot(c2s, mid) + _dot(c2s, lo)
    cur = (i * tq + lax.broadcasted_iota(I32, (1, tq), 1)) // SEL_BLOCK
    j = lax.broadcasted_iota(I32, (MAX_SEL_BLOCKS, tq), 0)
    valid = j <= cur
    bonus = jnp.where(j == 0, FORCE_BONUS, jnp.where(j == cur, FORCE_BONUS, jnp.where(j == cur - 1, FORCE_BONUS, 0.0)))
    work = jnp.where(valid, imp + bonus, -jnp.inf)
    sel = jnp.zeros((MAX_SEL_BLOCKS, tq), F32)
    for _ in range(N_SELECT):
        mx = jnp.max(work, axis=0, keepdims=True)
        idx = jnp.min(jnp.where(work == mx, j, MAX_SEL_BLOCKS), axis=0, keepdims=True)
        pick = j == idx
        sel = jnp.where(pick, 1.0, sel)
        work = jnp.where(pick, -jnp.inf, work)
    bias_ref[0, 0] = jnp.where(valid, jnp.where(sel > 0.0, 0.0, MASK_BIAS), MASK_BIAS).astype(BF16)


def _cmp_attn2(qt, gates, kc, vct):
    B, HD, S = qt.shape
    G, dh = N_KV_GROUPS, HEAD_DIM
    gw = HD // G
    nc = kc.shape[2]
    tq = min(Q_TILE, S)
    n0 = jnp.arange(nc)[None, :] * CMP_STRIDE
    j0 = jnp.arange(MAX_SEL_BLOCKS)[:, None] * SEL_BLOCK
    c2s = ((n0 < j0 + SEL_BLOCK) & (n0 + CMP_BLOCK > j0)).astype(BF16)
    return pl.pallas_call(
        functools.partial(_cmp2_kernel, tq=tq, nc=nc),
        out_shape=(jax.ShapeDtypeStruct((B, S, HD), BF16), jax.ShapeDtypeStruct((B, G, MAX_SEL_BLOCKS, S), BF16)),
        grid=(B, G, S // tq),
        in_specs=[
            pl.BlockSpec((1, gw, tq), lambda b, g, i: (b, g, i)),
            pl.BlockSpec((1, tq, LANES), lambda b, g, i: (b, i, 0)),
            pl.BlockSpec((1, 1, nc, dh), lambda b, g, i: (b, g, 0, 0)),
            pl.BlockSpec((1, 1, dh, nc), lambda b, g, i: (b, g, 0, 0)),
            pl.BlockSpec((MAX_SEL_BLOCKS, nc), lambda b, g, i: (0, 0)),
        ],
        out_specs=(pl.BlockSpec((1, tq, gw), lambda b, g, i: (b, i, g)),
                   pl.BlockSpec((1, 1, MAX_SEL_BLOCKS, tq), lambda b, g, i: (b, g, 0, i))),
        compiler_params=_cparams(("parallel", "parallel", "parallel"), 40),
        name="nsa_compressed",
    )(qt, gates, kc, vct, c2s)


def _sel2_kernel(qt_ref, bias_ref, gt_ref, k_ref, vt_ref, o_ref, qa, m_s, acc, s_a, s_b, *, tq, tk):
    g = pl.program_id(1)
    i = pl.program_id(2)
    qt = qt_ref[0]
    bias = bias_ref[0, 0]
    for r in range(HEADS_PER_GROUP):
        cols = slice(r * tq, (r + 1) * tq)
        qa[0:HEAD_DIM, cols] = qt[r * HEAD_DIM:(r + 1) * HEAD_DIM]
        qa[HEAD_DIM:HEAD_DIM + MAX_SEL_BLOCKS, cols] = bias
        qa[HEAD_DIM + MAX_SEL_BLOCKS:, cols] = jnp.zeros((KEY_AUG - HEAD_DIM - MAX_SEL_BLOCKS, tq), BF16)
    for c in range(SEL_CHAINS):
        m_s[c] = jnp.full(m_s.shape[1:], NEG_BIG, F32)
        acc[c] = jnp.zeros(acc.shape[1:], F32)
    t = _lane_query_pos(i, tq)
    ck = tk // SEL_CHAINS

    def qk(jt, buf):
        buf[...] = _dot(k_ref[0, 0, pl.ds(pl.multiple_of(jt * tk, tk), tk), :], qa[...])

    def absorb(jt, buf, diagonal):
        for c in range(SEL_CHAINS):
            kc = pl.multiple_of(jt * tk + c * ck, ck)
            s = buf[c * ck:(c + 1) * ck, :]
            if diagonal:
                s = jnp.where(kc + lax.broadcasted_iota(I32, (ck, 1), 0) <= t, s, NEG_BIG)
            m_old = m_s[c]
            m_new = jnp.maximum(m_old, jnp.max(s, axis=0, keepdims=True))
            p = jnp.exp2(s - m_new).astype(BF16)
            acc[c] = jnp.exp2(m_old - m_new) * acc[c] + _dot(vt_ref[0, 0, :, pl.ds(kc, ck)], p)
            m_s[c] = m_new

    def pair(u, c):
        qk(2 * u + 1, s_b)
        absorb(2 * u, s_a, False)
        qk(2 * u + 2, s_a)
        absorb(2 * u + 1, s_b, False)
        return c

    last = (i * tq + tq + tk - 1) // tk - 1
    qk(0, s_a)
    lax.fori_loop(0, last // 2, pair, 0)

    @pl.when(last % 2 == 1)
    def _():
        qk(last, s_b)
        absorb(last - 1, s_a, False)
        absorb(last, s_b, True)

    @pl.when(last % 2 == 0)
    def _():
        absorb(last, s_a, True)

    m_all = m_s[0]
    for c in range(1, SEL_CHAINS):
        m_all = jnp.maximum(m_all, m_s[c])
    acc_t = jnp.exp2(m_s[0] - m_all) * acc[0]
    for c in range(1, SEL_CHAINS):
        acc_t = acc_t + jnp.exp2(m_s[c] - m_all) * acc[c]
    o_ref[0] = _finish_heads(acc_t, gt_ref[0], g, 1, tq, HEAD_DIM).astype(BF16)


def _sel_attn2(qt, bias, gates, ksa, vst):
    B, HD, S = qt.shape
    G = N_KV_GROUPS
    gw = HD // G
    tq = min(Q_TILE, S)
    tk = min(KEY_TILE, S)
    width = HEADS_PER_GROUP * tq
    return pl.pallas_call(
        functools.partial(_sel2_kernel, tq=tq, tk=tk),
        out_shape=jax.ShapeDtypeStruct((B, S, HD), BF16),
        grid=(B, G, S // tq),
        in_specs=[
            pl.BlockSpec((1, gw, tq), lambda b, g, i: (b, g, i)),
            pl.BlockSpec((1, 1, MAX_SEL_BLOCKS, tq), lambda b, g, i: (b, g, 0, i)),
            pl.BlockSpec((1, tq, LANES), lambda b, g, i: (b, i, 0)),
            pl.BlockSpec((1, 1, S, KEY_AUG), lambda b, g, i: (b, g, 0, 0)),
            pl.BlockSpec((1, 1, V_ROWS, S), lambda b, g, i: (b, g, 0, 0)),
        ],
        out_specs=pl.BlockSpec((1, tq, gw), lambda b, g, i: (b, i, g)),
        scratch_shapes=[pltpu.VMEM((KEY_AUG, width), BF16), pltpu.VMEM((SEL_CHAINS, 1, width), F32),
                        pltpu.VMEM((SEL_CHAINS, V_ROWS, width), F32),
                        pltpu.VMEM((tk, width), F32), pltpu.VMEM((tk, width), F32)],
        compiler_params=_cparams(("parallel", "parallel", "arbitrary"), 48),
        name="nsa_selected",
    )(qt, bias, gates, ksa, vst)


def _win2_kernel(qt_ref, gt_ref, k_ref, vt_ref, o_ref, *, tq, wb):
    g = pl.program_id(1)
    i = pl.program_id(2)
    k0 = pl.multiple_of(jnp.maximum(i * tq - WINDOW, 0), tq)
    s = _dot(k_ref[0, 0, pl.ds(k0, wb), :], _heads_on_lanes(qt_ref[0]))
    t = _lane_query_pos(i, tq)
    kpos = k0 + lax.broadcasted_iota(I32, (wb, 1), 0)
    s = jnp.where(kpos <= t, jnp.where(kpos > t - WINDOW, s, -jnp.inf), -jnp.inf)
    m = jnp.max(s, axis=0, keepdims=True)
    p = jnp.exp2(s - m).astype(BF16)
    acc = _dot(vt_ref[0, 0, :, pl.ds(k0, wb)], p)
    o_ref[0] = _finish_heads(acc, gt_ref[0], g, 2, tq, HEAD_DIM).astype(BF16)


def _win_attn2(qt, gates, kw, vwt):
    B, HD, S = qt.shape
    G, dh = N_KV_GROUPS, HEAD_DIM
    gw = HD // G
    tq = min(Q_TILE, S)
    wb = WINDOW + tq
    assert S >= wb and tq % LANES == 0
    return pl.pallas_call(
        functools.partial(_win2_kernel, tq=tq, wb=wb),
        out_shape=jax.ShapeDtypeStruct((B, S, HD), BF16),
        grid=(B, G, S // tq),
        in_specs=[
            pl.BlockSpec((1, gw, tq), lambda b, g, i: (b, g, i)),
            pl.BlockSpec((1, tq, LANES), lambda b, g, i: (b, i, 0)),
            pl.BlockSpec((1, 1, S, dh), lambda b, g, i: (b, g, 0, 0)),
            pl.BlockSpec((1, 1, V_ROWS, S), lambda b, g, i: (b, g, 0, 0)),
        ],
        out_specs=pl.BlockSpec((1, tq, gw), lambda b, g, i: (b, i, g)),
        compiler_params=_cparams(("parallel", "parallel", "parallel"), 40),
        name="nsa_window",
    )(qt, gates, kw, vwt)


def _oproj_kernel(h_ref, a_ref, b_ref, c_ref, w_ref, o_ref):
    o = a_ref[...].astype(F32) + b_ref[...].astype(F32) + c_ref[...].astype(F32)
    o_ref[...] = h_ref[...] + _dot(o.astype(BF16), w_ref[...])


def _out_proj(h2, oc, os_, ow, w_o):
    T, D = h2.shape
    HD = oc.shape[-1]
    tt = min(SEQ_TILE, T)
    blk = lambda w: pl.BlockSpec((tt, w), lambda i: (i, 0))
    return pl.pallas_call(
        _oproj_kernel,
        out_shape=jax.ShapeDtypeStruct((T, D), F32),
        grid=(T // tt,),
        in_specs=[blk(D), blk(HD), blk(HD), blk(HD), pl.BlockSpec((HD, D), lambda i: (0, 0))],
        out_specs=blk(D),
        compiler_params=_cparams(("parallel",), 40),
        name="nsa_out_proj",
    )(h2, oc, os_, ow, w_o.astype(BF16))


def _nsa_layer(h, g, w_qg, b_gate, w_o, shared):
    B, S, D = h.shape
    kc, vct, ksa, vst, kw, vwt = shared
    qt, gates = _q_proj2(h, g, w_qg, b_gate)
    oc, bias = _cmp_attn2(qt, gates, kc, vct)
    os_ = _sel_attn2(qt, bias, gates, ksa, vst)
    ow = _win_attn2(qt, gates, kw, vwt)
    flat = lambda a: a.reshape(B * S, a.shape[-1])
    return _out_proj(flat(h), flat(oc), flat(os_), flat(ow), w_o).reshape(B, S, D)


def _shared_kv(h, kv_norm, w_kv, ck, cv):
    B, S, _ = h.shape
    G, dh = N_KV_GROUPS, HEAD_DIM
    assert S % SEL_BLOCK == 0 and S // SEL_BLOCK <= MAX_SEL_BLOCKS
    cvals, ksa, vst, kw, vwt = _shared_kv_proj2(h, kv_norm, w_kv)
    halves = cvals.reshape(B, S, 2, G, dh).transpose(2, 0, 3, 1, 4).reshape(2, B, G, S // CMP_STRIDE, CMP_STRIDE * dh)
    kc = _compress2(halves, 0, *ck, keys=True)
    vct = _compress2(halves, 1, *cv, keys=False)
    return kc, vct, ksa, vst, kw, vwt


def kernel(x, p, norm_mix, norm_ffn, norm_ple, pool_w, pool_b, pool_scale, kv_norm, w_kv, cmp_k_pos, cmp_k_w1, cmp_k_b1, cmp_k_w2, cmp_k_b2, cmp_v_pos, cmp_v_w1, cmp_v_b1, cmp_v_w2, cmp_v_b2, w_qg, b_gate, w_o, router_g_w, router_g_b, router_e_w, router_e_b, moe_w1, moe_w3, moe_w2, ple_proj, ple_gate_w, ple_gate_b, final_norm):
    B, S, D = x.shape
    depth = p.shape[0]
    n_a = pool_w.shape[0]
    T = B * S
    h = x
    shared = None
    for i in range(depth):
        if i == n_a:
            shared = _shared_kv(h, kv_norm, w_kv,
                                (cmp_k_pos, cmp_k_w1, cmp_k_b1, cmp_k_w2, cmp_k_b2),
                                (cmp_v_pos, cmp_v_w1, cmp_v_b1, cmp_v_w2, cmp_v_b2))
        if i < n_a:
            h = _pool_layer(h, norm_mix[i], pool_w[i], pool_b[i], pool_scale[i])
        else:
            j = i - n_a
            h = _nsa_layer(h, norm_mix[i], w_qg[j], b_gate[j], w_o[j], shared)
        h = _moe_ple_layer(h.reshape(T, D), p.reshape(depth, T, p.shape[-1]), i, norm_ffn[i], router_g_w[i],
                           router_g_b[i], router_e_w[i], router_e_b[i], moe_w1, moe_w3, moe_w2, norm_ple[i],
                           ple_gate_w[i], ple_gate_b[i], ple_proj[i], final_norm, i == depth - 1).reshape(B, S, D)
    return h
```

```python
import functools

import jax
import jax.numpy as jnp
from jax import lax
from jax.experimental import pallas as pl
from jax.experimental.pallas import tpu as pltpu

F32 = jnp.float32
BF16 = jnp.bfloat16
I32 = jnp.int32

POOL_WINDOWS = (2, 4, 8, 16)
N_HEADS = 16
HEAD_DIM = 64
N_KV_GROUPS = 4
HEADS_PER_GROUP = N_HEADS // N_KV_GROUPS
N_BRANCH = 3
ROPE_DIMS = HEAD_DIM // 4
ROPE_HALF = ROPE_DIMS // 2
ROPE_THETA = 500000.0
CMP_BLOCK = 32
CMP_STRIDE = 16
SEL_BLOCK = 64
N_SELECT = 16
WINDOW = 512
FORCE_BONUS = 1e4
N_EXPERT_GROUPS = 4
EXPERTS_PER_GROUP = 8
N_EXPERTS = N_EXPERT_GROUPS * EXPERTS_PER_GROUP
TOP_K_IN_GROUP = 2
RMS_EPS = 1e-6

LANES = 128
MAX_SEL_BLOCKS = LANES
MASK_BIAS = -30000.0

SEQ_TILE = 512
TOK_TILE = 512
CMB_TILE = 256
MOE_CHUNK = 512
Q_TILE = 256
KEY_TILE = 512
HALO = 16

HIGHEST = lax.Precision.HIGHEST


def _cparams(sem, vmem_mb):
    return pltpu.CompilerParams(dimension_semantics=sem, vmem_limit_bytes=vmem_mb * 1024 * 1024)


def _rms(x, g):
    return x * lax.rsqrt(jnp.mean(x * x, axis=-1, keepdims=True) + RMS_EPS) * g


def _dot(a, b):
    return jnp.dot(a, b, preferred_element_type=F32)


def _dot_nt(a, b):
    return lax.dot_general(a, b, (((1,), (1,)), ((), ())), preferred_element_type=F32)


def _pool_kernel(h_ref, halo_ref, g_ref, w_ref, b_ref, sc_ref, o_ref, *, ts, cg):
    i = pl.program_id(1)
    x = h_ref[0]
    g = g_ref[...]
    xn = _rms(x, g)
    hn = _rms(halo_ref[0], g)
    hn = jnp.where(i > 0, hn, 0.0)
    ext = jnp.concatenate([hn, xn], axis=0)
    t = i * ts + lax.broadcasted_iota(I32, (ts, 1), 0)
    outs = []
    for gi, w in enumerate(POOL_WINDOWS):
        s = ext[:, gi * cg:(gi + 1) * cg]
        k = 1
        while k < w:
            s = s + pltpu.roll(s, k, axis=0)
            k *= 2
        cnt = jnp.minimum(t + 1, w).astype(F32)
        pooled = s[HALO:] / cnt - xn[:, gi * cg:(gi + 1) * cg]
        outs.append(_dot(pooled.astype(BF16), w_ref[gi]))
    y = jnp.concatenate(outs, axis=-1)
    o_ref[0] = x + (y + b_ref[...]) * sc_ref[...]


def _pool_layer(h, g, w, b, sc):
    B, S, D = h.shape
    ts = min(SEQ_TILE, S)
    cg = D // len(POOL_WINDOWS)
    row = lambda v: v.reshape(1, D)
    return pl.pallas_call(
        functools.partial(_pool_kernel, ts=ts, cg=cg),
        out_shape=jax.ShapeDtypeStruct((B, S, D), F32),
        grid=(B, S // ts),
        in_specs=[
            pl.BlockSpec((1, ts, D), lambda b_, i: (b_, i, 0)),
            pl.BlockSpec((1, HALO, D), lambda b_, i: (b_, jnp.maximum(i * (ts // HALO) - 1, 0), 0)),
            pl.BlockSpec((1, D), lambda b_, i: (0, 0)),
            pl.BlockSpec((len(POOL_WINDOWS), cg, cg), lambda b_, i: (0, 0, 0)),
            pl.BlockSpec((1, D), lambda b_, i: (0, 0)),
            pl.BlockSpec((1, D), lambda b_, i: (0, 0)),
        ],
        out_specs=pl.BlockSpec((1, ts, D), lambda b_, i: (b_, i, 0)),
        compiler_params=_cparams(("parallel", "parallel"), 40),
        name="pool_mixer",
    )(h, h, row(g), w.astype(BF16), row(b), row(sc))


def _router_kernel(h_ref, g_ref, wh_ref, wl_ref, b_ref, info_ref, cnt_ref, *, tt):
    i = pl.program_id(0)

    @pl.when(i == 0)
    def _():
        cnt_ref[...] = jnp.zeros_like(cnt_ref)

    xn = _rms(h_ref[...], g_ref[...])
    xh = xn.astype(BF16)
    xl = (xn - xh.astype(F32)).astype(BF16)
    logits = _dot(xh, wh_ref[...]) + (_dot(xh, wl_ref[...]) + _dot(xl, wh_ref[...])) + b_ref[...]
    lane = lax.broadcasted_iota(I32, (tt, LANES), 1)
    neg = -jnp.inf
    gl = jnp.where(lane < N_EXPERT_GROUPS, logits, neg)
    gmax = jnp.max(gl, axis=-1, keepdims=True)
    grp = jnp.min(jnp.where(gl == gmax, lane, LANES), axis=-1, keepdims=True)
    gprob = 1.0 / jnp.sum(jnp.exp(gl - gmax), axis=-1, keepdims=True)
    lo = N_EXPERT_GROUPS + grp * EXPERTS_PER_GROUP
    el = jnp.where(lane >= lo, jnp.where(lane < lo + EXPERTS_PER_GROUP, logits, neg), neg)
    v1 = jnp.max(el, axis=-1, keepdims=True)
    i1 = jnp.min(jnp.where(el == v1, lane, LANES), axis=-1, keepdims=True)
    el2 = jnp.where(lane == i1, neg, el)
    v2 = jnp.max(el2, axis=-1, keepdims=True)
    i2 = jnp.min(jnp.where(el2 == v2, lane, LANES), axis=-1, keepdims=True)
    e2 = jnp.exp(v2 - v1)
    w1 = gprob / (1.0 + e2)
    w2 = gprob * e2 / (1.0 + e2)
    oh1 = lane == i1
    oh2 = lane == i2
    oh = jnp.where(oh1, 1.0, jnp.where(oh2, 1.0, 0.0))
    r_ = lax.broadcasted_iota(I32, (tt, tt), 0)
    c_ = lax.broadcasted_iota(I32, (tt, tt), 1)
    tri = jnp.where(r_ > c_, 1.0, 0.0).astype(BF16)
    tot = _dot(tri, oh.astype(BF16)) + cnt_ref[...]
    r1 = jnp.sum(jnp.where(oh1, tot, 0.0), axis=-1, keepdims=True)
    r2 = jnp.sum(jnp.where(oh2, tot, 0.0), axis=-1, keepdims=True)
    cnt_ref[...] = cnt_ref[...] + jnp.sum(oh, axis=0, keepdims=True)
    vals = (i1.astype(F32) - N_EXPERT_GROUPS, i2.astype(F32) - N_EXPERT_GROUPS, w1, w2, r1, r2)
    info = jnp.zeros((tt, LANES), F32)
    for k, v in enumerate(vals):
        info = jnp.where(lane == k, v, info)
    info_ref[...] = info


def _router(h2, g, wr, br):
    T, D = h2.shape
    tt = min(TOK_TILE, T)
    return pl.pallas_call(
        functools.partial(_router_kernel, tt=tt),
        out_shape=(jax.ShapeDtypeStruct((T, LANES), F32), jax.ShapeDtypeStruct((1, LANES), F32)),
        grid=(T // tt,),
        in_specs=[
            pl.BlockSpec((tt, D), lambda i: (i, 0)),
            pl.BlockSpec((1, D), lambda i: (0, 0)),
            pl.BlockSpec((D, LANES), lambda i: (0, 0)),
            pl.BlockSpec((D, LANES), lambda i: (0, 0)),
            pl.BlockSpec((1, LANES), lambda i: (0, 0)),
        ],
        out_specs=(pl.BlockSpec((tt, LANES), lambda i: (i, 0)), pl.BlockSpec((1, LANES), lambda i: (0, 0))),
        compiler_params=_cparams(("arbitrary",), 40),
        name="moe_router",
    )(h2, g.reshape(1, D), wr.astype(BF16), (wr - wr.astype(BF16).astype(F32)).astype(BF16), br)


def _row_copy(src, s, dst, d, sem):
    return pltpu.make_async_copy(src.at[pl.ds(s, 1)], dst.at[pl.ds(d, 1)], sem)


def _dispatch_kernel(dest_ref, cend_ref, h_ref, g_ref, xs_ref, xn_s, sem, *, tt, n_tok, n_chunks):
    base = pl.program_id(0) * tt

    @pl.when(pl.program_id(0) == 0)
    def _():
        xn_s[...] = jnp.zeros_like(xn_s)

        def tail(e):
            nonempty = cend_ref[e] > (cend_ref[e - 1] if e > 0 else 0)
            row = pl.multiple_of((cend_ref[e] - 1) * tt, tt)
            return nonempty, pltpu.make_async_copy(xn_s, xs_ref.at[pl.ds(row, tt)], sem)

        def unused(c):
            return pltpu.make_async_copy(xn_s, xs_ref.at[pl.ds(pl.multiple_of(c * tt, tt), tt)], sem)

        n_used = cend_ref[N_EXPERTS - 1]
        for e in range(N_EXPERTS):
            nonempty, cp = tail(e)
            pl.when(nonempty)(cp.start)
        lax.fori_loop(n_used, n_chunks, lambda c, z: (unused(c).start(), z)[1], 0)
        for e in range(N_EXPERTS):
            nonempty, cp = tail(e)
            pl.when(nonempty)(cp.wait)
        lax.fori_loop(n_used, n_chunks, lambda c, z: (unused(c).wait(), z)[1], 0)

    xn_s[...] = _rms(h_ref[...], g_ref[...])

    def issue(r, c):
        for k in range(TOP_K_IN_GROUP):
            _row_copy(xn_s, r, xs_ref, dest_ref[k * n_tok + base + r], sem).start()
        return c

    lax.fori_loop(0, tt, issue, 0, unroll=8)
    for k in range(TOP_K_IN_GROUP):
        pltpu.make_async_copy(xn_s, xs_ref.at[pl.ds(0, tt)], sem).wait()


def _dispatch(dest, chunk_end, h2, g, n_rows):
    T, D = h2.shape
    tt = MOE_CHUNK
    assert T % tt == 0
    return pl.pallas_call(
        functools.partial(_dispatch_kernel, tt=tt, n_tok=T, n_chunks=n_rows // tt),
        out_shape=jax.ShapeDtypeStruct((n_rows, D), F32),
        grid_spec=pltpu.PrefetchScalarGridSpec(
            num_scalar_prefetch=2,
            grid=(T // tt,),
            in_specs=[
                pl.BlockSpec((tt, D), lambda i, d, ce: (i, 0)),
                pl.BlockSpec((1, D), lambda i, d, ce: (0, 0)),
            ],
            out_specs=pl.BlockSpec(memory_space=pl.ANY),
            scratch_shapes=[pltpu.VMEM((tt, D), F32), pltpu.SemaphoreType.DMA],
        ),
        compiler_params=_cparams(("arbitrary",), 40),
        name="moe_dispatch",
    )(dest, chunk_end, h2, g.reshape(1, D))


def _expert_kernel(ce_ref, nv_ref, xs_ref, w1_ref, w3_ref, w2_ref, o_ref, w1b, w3b, w2b):
    c = pl.program_id(0)
    e = ce_ref[c]
    prev = ce_ref[jnp.maximum(c - 1, 0)]

    @pl.when(jnp.logical_or(c == 0, e != prev))
    def _():
        w1b[...] = w1_ref[0, 0].astype(BF16)
        w3b[...] = w3_ref[0, 0].astype(BF16)
        w2b[...] = w2_ref[0, 0].astype(BF16)

    @pl.when(c < nv_ref[0])
    def _():
        x = xs_ref[...].astype(BF16)
        a = _dot(x, w1b[...])
        b = _dot(x, w3b[...])
        hc = a * jax.nn.sigmoid(a) * b
        o_ref[...] = _dot(hc.astype(BF16), w2b[...])

    @pl.when(c >= nv_ref[0])
    def _():
        o_ref[...] = jnp.zeros_like(o_ref)


def _experts(chunk_e, n_valid, xs, w1, w3, w2, layer):
    P, D = xs.shape
    F = w1.shape[-1]
    ch = MOE_CHUNK
    rows = lambda c, ce, nv: (jnp.minimum(c, nv[0] - 1), 0)
    wsel = lambda c, ce, nv: (layer, ce[c], 0, 0)
    return pl.pallas_call(
        _expert_kernel,
        out_shape=jax.ShapeDtypeStruct((P, D), F32),
        grid_spec=pltpu.PrefetchScalarGridSpec(
            num_scalar_prefetch=2,
            grid=(P // ch,),
            in_specs=[
                pl.BlockSpec((ch, D), rows),
                pl.BlockSpec((1, 1, D, F), wsel),
                pl.BlockSpec((1, 1, D, F), wsel),
                pl.BlockSpec((1, 1, F, D), wsel),
            ],
            out_specs=pl.BlockSpec((ch, D), lambda c, ce, nv: (c, 0)),
            scratch_shapes=[pltpu.VMEM((D, F), BF16), pltpu.VMEM((D, F), BF16), pltpu.VMEM((F, D), BF16)],
        ),
        compiler_params=_cparams(("arbitrary",), 56),
        name="moe_experts",
    )(chunk_e, n_valid, xs, w1, w3, w2)


def _combine_kernel(dest_ref, h_ref, info_ref, rows_ref, p_ref, g_ref, gw_ref, gb_ref, pw_ref, fn_ref,
                    o_ref, buf, sem, *, tt, n_tok, final):
    base = pl.program_id(0) * tt

    def issue(r, c):
        for k in range(TOP_K_IN_GROUP):
            _row_copy(rows_ref, dest_ref[k * n_tok + base + r], buf.at[k], r, sem).start()
        return c

    lax.fori_loop(0, tt, issue, 0, unroll=8)
    for k in range(TOP_K_IN_GROUP):
        pltpu.make_async_copy(rows_ref.at[pl.ds(0, tt)], buf.at[k], sem).wait()
    info = info_ref[...]
    y = h_ref[...] + info[:, 2:3] * buf[0] + info[:, 3:4] * buf[1]
    hn = _rms(y, g_ref[...])
    gate = jax.nn.sigmoid(_dot(hn.astype(BF16), gw_ref[...]) + gb_ref[...])
    out = y + _dot(p_ref[0].astype(BF16), pw_ref[...]) * gate
    if final:
        out = _rms(out, fn_ref[...])
    o_ref[...] = out


def _combine(dest, h2, info, rows, p3, layer, g, gw, gb, pw, fn, final):
    T, D = h2.shape
    PD = p3.shape[-1]
    tt = min(CMB_TILE, T)
    full = lambda i, d: (0, 0)
    return pl.pallas_call(
        functools.partial(_combine_kernel, tt=tt, n_tok=T, final=final),
        out_shape=jax.ShapeDtypeStruct((T, D), F32),
        grid_spec=pltpu.PrefetchScalarGridSpec(
            num_scalar_prefetch=1,
            grid=(T // tt,),
            in_specs=[
                pl.BlockSpec((tt, D), lambda i, d: (i, 0)),
                pl.BlockSpec((tt, LANES), lambda i, d: (i, 0)),
                pl.BlockSpec(memory_space=pl.ANY),
                pl.BlockSpec((1, tt, PD), lambda i, d: (layer, i, 0)),
                pl.BlockSpec((1, D), full),
                pl.BlockSpec((D, D), full),
                pl.BlockSpec((1, D), full),
                pl.BlockSpec((PD, D), full),
                pl.BlockSpec((1, D), full),
            ],
            out_specs=pl.BlockSpec((tt, D), lambda i, d: (i, 0)),
            scratch_shapes=[pltpu.VMEM((TOP_K_IN_GROUP, tt, D), F32), pltpu.SemaphoreType.DMA],
        ),
        compiler_params=_cparams(("arbitrary",), 40),
        name="moe_combine_ple",
    )(dest, h2, info, rows, p3, g.reshape(1, D), gw.astype(BF16), gb.reshape(1, D), pw.astype(BF16),
      fn.reshape(1, D))


IDX_SLOTS = 3


def _expert2_kernel(ce_ref, nv_ref, idx_hbm, h_hbm, g_ref, w1_ref, w3_ref, w2_ref, y_hbm,
                    idx_s, x_a, x_b, o_a, o_b, w1b, w3b, w2b, idx_sem, g_sem, s_sem, *, ch, n_tok, n_chunks):
    c = pl.program_id(0)
    nv = nv_ref[0]
    x_buf = (x_a, x_b)
    o_buf = (o_a, o_b)

    def idx_copy(chunk, sl):
        return pltpu.make_async_copy(idx_hbm.at[chunk], idx_s.at[sl], idx_sem.at[sl])

    def gather(chunk_slot, sl, r):
        return pltpu.make_async_copy(h_hbm.at[pl.ds(idx_s[chunk_slot, 0, r], 1)], x_buf[sl].at[pl.ds(r, 1)],
                                     g_sem.at[sl])

    def scatter(chunk_slot, sl, r):
        return pltpu.make_async_copy(o_buf[sl].at[pl.ds(r, 1)], y_hbm.at[pl.ds(idx_s[chunk_slot, 1, r], 1)],
                                     s_sem.at[sl])

    def wait_gathers(sl):
        pltpu.make_async_copy(h_hbm.at[pl.ds(0, ch)], x_buf[sl], g_sem.at[sl]).wait()

    def wait_scatters(sl):
        pltpu.make_async_copy(o_buf[sl], y_hbm.at[pl.ds(0, ch)], s_sem.at[sl]).wait()

    @pl.when(c == 0)
    def _():
        o_a[...] = jnp.zeros_like(o_a)
        o_b[...] = jnp.zeros_like(o_b)
        pltpu.make_async_copy(o_a, y_hbm.at[pl.ds(2 * n_tok, ch)], s_sem.at[0]).start()
        first = idx_copy(0, 0)
        first.start()
        first.wait()
        lax.fori_loop(0, ch, lambda r, z: (gather(0, 0, r).start(), z)[1], 0, unroll=8)
        null = idx_copy(n_chunks, IDX_SLOTS - 1)
        null.start()
        null.wait()
        idx_copy(jnp.minimum(1, n_chunks), 1).start()

    e = ce_ref[c]
    prev = ce_ref[jnp.maximum(c - 1, 0)]

    @pl.when(jnp.logical_or(c == 0, e != prev))
    def _():
        w1b[...] = w1_ref[0, 0].astype(BF16)
        w3b[...] = w3_ref[0, 0].astype(BF16)
        w2b[...] = w2_ref[0, 0].astype(BF16)

    def chunk_step(slot):
        other = 1 - slot
        cur3 = c % IDX_SLOTS
        nxt3 = (c + 1) % IDX_SLOTS
        prv3 = (c + IDX_SLOTS - 1) % IDX_SLOTS
        wait_scatters(slot)
        wait_gathers(slot)
        idx_copy(0, nxt3).wait()
        for r in range(ch):
            gather(nxt3, other, r).start()
            scatter(prv3, other, r).start()
        x = x_buf[slot][...]
        x = (x * lax.rsqrt(jnp.mean(x * x, axis=-1, keepdims=True) + RMS_EPS) * g_ref[...]).astype(BF16)
        a = _dot(x, w1b[...])
        b = _dot(x, w3b[...])
        hc = a * jax.nn.sigmoid(a) * b
        o_buf[slot][...] = _dot(hc.astype(BF16), w2b[...])
        idx_copy(jnp.minimum(c + 2, n_chunks), prv3).start()

        @pl.when(c == nv - 1)
        def _():
            lax.fori_loop(0, ch, lambda r, z: (scatter(cur3, slot, r).start(), z)[1], 0, unroll=8)
            wait_scatters(other)
            wait_scatters(slot)
            wait_gathers(other)
            idx_copy(0, prv3).wait()

    for parity in range(2):
        pl.when(jnp.logical_and(c < nv, c % 2 == parity))(functools.partial(chunk_step, parity))


def _experts2(chunk_e, n_valid, idx, h2, g, w1, w3, w2, layer):
    T, D = h2.shape
    F = w1.shape[-1]
    ch = MOE_CHUNK
    n_chunks = idx.shape[0] - 1
    wsel = lambda c, ce, nv: (layer, ce[c], 0, 0)
    return pl.pallas_call(
        functools.partial(_expert2_kernel, ch=ch, n_tok=T, n_chunks=n_chunks),
        out_shape=jax.ShapeDtypeStruct((TOP_K_IN_GROUP * T + 2 * ch, D), F32),
        grid_spec=pltpu.PrefetchScalarGridSpec(
            num_scalar_prefetch=2,
            grid=(n_chunks,),
            in_specs=[
                pl.BlockSpec(memory_space=pl.ANY),
                pl.BlockSpec(memory_space=pl.ANY),
                pl.BlockSpec((1, D), lambda c, ce, nv: (0, 0)),
                pl.BlockSpec((1, 1, D, F), wsel),
                pl.BlockSpec((1, 1, D, F), wsel),
                pl.BlockSpec((1, 1, F, D), wsel),
            ],
            out_specs=pl.BlockSpec(memory_space=pl.ANY),
            scratch_shapes=[
                pltpu.SMEM((IDX_SLOTS, 2, ch), I32),
                pltpu.VMEM((ch, D), F32), pltpu.VMEM((ch, D), F32),
                pltpu.VMEM((ch, D), F32), pltpu.VMEM((ch, D), F32),
                pltpu.VMEM((D, F), BF16), pltpu.VMEM((D, F), BF16), pltpu.VMEM((F, D), BF16),
                pltpu.SemaphoreType.DMA((IDX_SLOTS,)),
                pltpu.SemaphoreType.DMA((2,)),
                pltpu.SemaphoreType.DMA((2,)),
            ],
        ),
        compiler_params=_cparams(("arbitrary",), 56),
        name="moe_experts",
    )(chunk_e, n_valid, idx, h2, g.reshape(1, D), w1, w3, w2)


def _combine2_kernel(h_ref, info_ref, ya_ref, yb_ref, p_ref, g_ref, gw_ref, gb_ref, pw_ref, fn_ref, o_ref, *, final):
    info = info_ref[...]
    y = h_ref[...] + info[:, 2:3] * ya_ref[...] + info[:, 3:4] * yb_ref[...]
    hn = _rms(y, g_ref[...])
    gate = jax.nn.sigmoid(_dot(hn.astype(BF16), gw_ref[...]) + gb_ref[...])
    out = y + _dot(p_ref[0].astype(BF16), pw_ref[...]) * gate
    if final:
        out = _rms(out, fn_ref[...])
    o_ref[...] = out


def _combine2(h2, info, y2, p3, layer, g, gw, gb, pw, fn, final):
    T, D = h2.shape
    PD = p3.shape[-1]
    tt = min(SEQ_TILE, T)
    full = lambda i: (0, 0)
    return pl.pallas_call(
        functools.partial(_combine2_kernel, final=final),
        out_shape=jax.ShapeDtypeStruct((T, D), F32),
        grid=(T // tt,),
        in_specs=[
            pl.BlockSpec((tt, D), lambda i: (i, 0)),
            pl.BlockSpec((tt, LANES), lambda i: (i, 0)),
            pl.BlockSpec((tt, D), lambda i: (i, 0)),
            pl.BlockSpec((tt, D), lambda i: (T // tt + i, 0)),
            pl.BlockSpec((1, tt, PD), lambda i: (layer, i, 0)),
            pl.BlockSpec((1, D), full),
            pl.BlockSpec((D, D), full),
            pl.BlockSpec((1, D), full),
            pl.BlockSpec((PD, D), full),
            pl.BlockSpec((1, D), full),
        ],
        out_specs=pl.BlockSpec((tt, D), lambda i: (i, 0)),
        compiler_params=_cparams(("parallel",), 48),
        name="moe_combine_ple",
    )(h2, info, y2, y2, p3, g.reshape(1, D), gw.astype(BF16), gb.reshape(1, D), pw.astype(BF16), fn.reshape(1, D))


def _moe_ple_layer(h2, p3, layer, norm_ffn, rg_w, rg_b, re_w, re_b, w1, w3, w2, norm_ple, gate_w, gate_b, ple_proj,
                   final_norm, final):
    T, D = h2.shape
    A = T * TOP_K_IN_GROUP
    pad = LANES - N_EXPERT_GROUPS - N_EXPERTS
    wr = jnp.concatenate([rg_w, re_w, jnp.zeros((D, pad), F32)], axis=1)
    br = jnp.concatenate([rg_b, re_b, jnp.zeros((pad,), F32)]).reshape(1, LANES)
    info, cnt = _router(h2, norm_ffn, wr, br)
    counts = cnt[0, N_EXPERT_GROUPS:N_EXPERT_GROUPS + N_EXPERTS].astype(I32)
    n_chunks_e = (counts + MOE_CHUNK - 1) // MOE_CHUNK
    chunk_end = jnp.cumsum(n_chunks_e)
    pstarts = (chunk_end - n_chunks_e) * MOE_CHUNK
    n_chunks = -(-A // MOE_CHUNK) + N_EXPERTS
    n_valid = chunk_end[-1:].astype(I32)
    cidx = jnp.minimum(jnp.arange(n_chunks, dtype=I32), n_valid[0] - 1)
    chunk_e = jnp.sum((chunk_end[None, :] <= cidx[:, None]).astype(I32), axis=1)
    chunk_e = jnp.minimum(chunk_e, N_EXPERTS - 1)
    e_idx = info[:, 0:TOP_K_IN_GROUP].astype(I32)
    rank = info[:, 4:4 + TOP_K_IN_GROUP].astype(I32)
    dest = (pstarts[e_idx] + rank).T.reshape(A)
    ch = MOE_CHUNK
    pos = jnp.arange((n_chunks + 1) * ch, dtype=I32)
    parity = jnp.where(pos // ch == n_chunks, 1, (pos // ch) % 2)
    src = jnp.zeros_like(pos).at[dest].set(jnp.tile(jnp.arange(T, dtype=I32), TOP_K_IN_GROUP))
    dst = (A + parity * ch + pos % ch).at[dest].set(jnp.arange(A, dtype=I32))
    idx = jnp.stack([src.reshape(n_chunks + 1, ch), dst.reshape(n_chunks + 1, ch)], axis=1)
    y2 = _experts2(chunk_e, n_valid, idx, h2, norm_ffn, w1, w3, w2, layer)
    return _combine2(h2, info, y2, p3, layer, norm_ple, gate_w, gate_b, ple_proj, final_norm, final)


def _rope_rows(xt, cos, sin):
    x1 = xt[0:ROPE_HALF]
    x2 = xt[ROPE_HALF:ROPE_DIMS]
    return jnp.concatenate([x1 * cos - x2 * sin, x2 * cos + x1 * sin, xt[ROPE_DIMS:]], axis=0)


def _kv_kernel(h_ref, g_ref, wn_ref, wt_ref, cos_ref, sin_ref, cv_ref, kst_ref, vs_ref, kwt_ref, vw_ref, *, ts):
    i = pl.program_id(1)
    kvw = N_KV_GROUPS * HEAD_DIM
    hn = _rms(h_ref[0], g_ref[...]).astype(BF16)
    nat = _dot(hn, wn_ref[...])
    cv_ref[0] = nat[:, 0:2 * kvw]
    tr = _dot_nt(wt_ref[...], hn)
    cos = cos_ref[...]
    sin = sin_ref[...]
    blk = lax.broadcasted_iota(I32, (MAX_SEL_BLOCKS, ts), 0)
    pos = i * ts + lax.broadcasted_iota(I32, (MAX_SEL_BLOCKS, ts), 1)
    onehot = jnp.where(pos // SEL_BLOCK == blk, 1.0, 0.0).astype(BF16)
    for g in range(N_KV_GROUPS):
        c0 = g * HEAD_DIM
        kst_ref[0, g, 0:MAX_SEL_BLOCKS, :] = onehot
        kst_ref[0, g, MAX_SEL_BLOCKS:, :] = _rope_rows(tr[c0:c0 + HEAD_DIM], cos, sin).astype(BF16)
        kwt_ref[0, g] = _rope_rows(tr[kvw + c0:kvw + c0 + HEAD_DIM], cos, sin).astype(BF16)
        vs_ref[0, g] = nat[:, 2 * kvw + c0:2 * kvw + c0 + HEAD_DIM].astype(BF16)
        vw_ref[0, g] = nat[:, 3 * kvw + c0:3 * kvw + c0 + HEAD_DIM].astype(BF16)


def _rope_tables(pos):
    inv = jnp.float32(ROPE_THETA) ** (-jnp.arange(ROPE_HALF, dtype=F32) * 2.0 / ROPE_DIMS)
    ang = pos.astype(F32)[:, None] * inv[None, :]
    return jnp.cos(ang), jnp.sin(ang)


def _shared_kv_proj(h, kv_norm, w_kv):
    B, S, D = h.shape
    G, dh = N_KV_GROUPS, HEAD_DIM
    kvw = G * dh
    ts = min(SEQ_TILE, S)
    br = lambda k: w_kv[:, k * kvw:(k + 1) * kvw]
    w_nat = jnp.concatenate([br(0), br(1), br(3), br(5)], axis=1).astype(BF16)
    w_tr = jnp.concatenate([br(2), br(4)], axis=1).T.astype(BF16)
    cos, sin = _rope_tables(jnp.arange(S))
    kd = MAX_SEL_BLOCKS + dh
    return pl.pallas_call(
        functools.partial(_kv_kernel, ts=ts),
        out_shape=(
            jax.ShapeDtypeStruct((B, S, 2 * kvw), F32),
            jax.ShapeDtypeStruct((B, G, kd, S), BF16),
            jax.ShapeDtypeStruct((B, G, S, dh), BF16),
            jax.ShapeDtypeStruct((B, G, dh, S), BF16),
            jax.ShapeDtypeStruct((B, G, S, dh), BF16),
        ),
        grid=(B, S // ts),
        in_specs=[
            pl.BlockSpec((1, ts, D), lambda b, i: (b, i, 0)),
            pl.BlockSpec((1, D), lambda b, i: (0, 0)),
            pl.BlockSpec((D, 4 * kvw), lambda b, i: (0, 0)),
            pl.BlockSpec((2 * kvw, D), lambda b, i: (0, 0)),
            pl.BlockSpec((ROPE_HALF, ts), lambda b, i: (0, i)),
            pl.BlockSpec((ROPE_HALF, ts), lambda b, i: (0, i)),
        ],
        out_specs=(
            pl.BlockSpec((1, ts, 2 * kvw), lambda b, i: (b, i, 0)),
            pl.BlockSpec((1, G, kd, ts), lambda b, i: (b, 0, 0, i)),
            pl.BlockSpec((1, G, ts, dh), lambda b, i: (b, 0, i, 0)),
            pl.BlockSpec((1, G, dh, ts), lambda b, i: (b, 0, 0, i)),
            pl.BlockSpec((1, G, ts, dh), lambda b, i: (b, 0, i, 0)),
        ),
        compiler_params=_cparams(("parallel", "parallel"), 48),
        name="shared_kv_proj",
    )(h, kv_norm.reshape(1, D), w_nat, w_tr, cos.T, sin.T)


def _compress_kernel(x_ref, pos_ref, w1_ref, b1_ref, w2_ref, b2_ref, cos_ref, sin_ref, o_ref, *, nh, keys):
    x = x_ref[0, 0, 0]
    a = _dot((x + pos_ref[0:1]).astype(BF16), w1_ref[0])
    b = _dot((x + pos_ref[1:2]).astype(BF16), w1_ref[1])
    hid = jax.nn.gelu(a + pltpu.roll(b, nh - 1, axis=0) + b1_ref[...]).astype(BF16)
    if keys:
        out = _dot_nt(w2_ref[...], hid) + b2_ref[...]
        o_ref[0, 0] = _rope_rows(out, cos_ref[...], sin_ref[...]).astype(BF16)
    else:
        o_ref[0, 0] = (_dot(hid, w2_ref[...]) + b2_ref[...]).astype(BF16)


def _compress(halves, which, pos_emb, w1, b1, w2, b2, keys):
    _, B, G, nh, hw = halves.shape
    dh = HEAD_DIM
    hidden = w1.shape[-1]
    pos2 = pos_emb.reshape(2, hw)
    w1s = w1.reshape(2, hw, hidden).astype(BF16)
    cos, sin = _rope_tables(jnp.arange(nh) * CMP_STRIDE + CMP_BLOCK - 1)
    if keys:
        w2a, b2a = w2.T.astype(BF16), b2.reshape(dh, 1)
        out_shape, out_block, out_idx = (B, G, dh, nh), (1, 1, dh, nh), lambda b, g: (b, g, 0, 0)
    else:
        w2a, b2a = w2.astype(BF16), b2.reshape(1, dh)
        out_shape, out_block, out_idx = (B, G, nh, dh), (1, 1, nh, dh), lambda b, g: (b, g, 0, 0)
    c2 = lambda b, g: (0, 0)
    return pl.pallas_call(
        functools.partial(_compress_kernel, nh=nh, keys=keys),
        out_shape=jax.ShapeDtypeStruct(out_shape, BF16),
        grid=(B, G),
        in_specs=[
            pl.BlockSpec((1, 1, 1, nh, hw), lambda b, g: (which, b, g, 0, 0)),
            pl.BlockSpec((2, hw), c2),
            pl.BlockSpec((2, hw, hidden), lambda b, g: (0, 0, 0)),
            pl.BlockSpec((1, hidden), c2),
            pl.BlockSpec(w2a.shape, c2),
            pl.BlockSpec(b2a.shape, c2),
            pl.BlockSpec((ROPE_HALF, nh), c2),
            pl.BlockSpec((ROPE_HALF, nh), c2),
        ],
        out_specs=pl.BlockSpec(out_block, out_idx),
        compiler_params=_cparams(("parallel", "parallel"), 40),
        name="compress_k" if keys else "compress_v",
    )(halves, pos2, w1s, b1.reshape(1, hidden), w2a, b2a, cos.T, sin.T)


def _qproj_kernel(h_ref, g_ref, wq_ref, wg_ref, bg_ref, c_ref, s1_ref, s2_ref, q_ref, gt_ref):
    xn = _rms(h_ref[0], g_ref[...]).astype(BF16)
    q = _dot(xn, wq_ref[...])
    c, s1, s2 = c_ref[...], s1_ref[...], s2_ref[...]
    for k in range(q.shape[-1] // LANES):
        x = q[:, k * LANES:(k + 1) * LANES]
        r = x * c + pltpu.roll(x, ROPE_HALF, axis=1) * s1 + pltpu.roll(x, LANES - ROPE_HALF, axis=1) * s2
        q_ref[0, :, k * LANES:(k + 1) * LANES] = r.astype(BF16)
    gt_ref[0] = jax.nn.sigmoid(_dot(xn, wg_ref[...]) + bg_ref[...])


def _q_proj(h, g, w_qg, b_gate):
    B, S, D = h.shape
    HD = N_HEADS * HEAD_DIM
    ng = N_HEADS * N_BRANCH
    ts = min(SEQ_TILE, S)
    wq = w_qg[:, :HD].astype(BF16)
    wg = jnp.pad(w_qg[:, HD:], ((0, 0), (0, LANES - ng))).astype(BF16)
    bg = jnp.pad(b_gate, (0, LANES - ng)).reshape(1, LANES)
    cos, sin = _rope_tables(jnp.arange(S))
    scale = HEAD_DIM ** -0.5
    ones = jnp.ones((S, HEAD_DIM - ROPE_DIMS), F32)
    zeros = jnp.zeros((S, HEAD_DIM - ROPE_DIMS), F32)
    zh = jnp.zeros((S, ROPE_HALF), F32)
    two = lambda a: jnp.concatenate([a, a], axis=1) * scale
    c = two(jnp.concatenate([cos, cos, ones], axis=1))
    s1 = two(jnp.concatenate([zh, sin, zeros], axis=1))
    s2 = two(jnp.concatenate([-sin, zh, zeros], axis=1))
    tab = pl.BlockSpec((ts, LANES), lambda b, i: (i, 0))
    return pl.pallas_call(
        _qproj_kernel,
        out_shape=(jax.ShapeDtypeStruct((B, S, HD), BF16), jax.ShapeDtypeStruct((B, S, LANES), F32)),
        grid=(B, S // ts),
        in_specs=[
            pl.BlockSpec((1, ts, D), lambda b, i: (b, i, 0)),
            pl.BlockSpec((1, D), lambda b, i: (0, 0)),
            pl.BlockSpec((D, HD), lambda b, i: (0, 0)),
            pl.BlockSpec((D, LANES), lambda b, i: (0, 0)),
            pl.BlockSpec((1, LANES), lambda b, i: (0, 0)),
            tab, tab, tab,
        ],
        out_specs=(pl.BlockSpec((1, ts, HD), lambda b, i: (b, i, 0)),
                   pl.BlockSpec((1, ts, LANES), lambda b, i: (b, i, 0))),
        compiler_params=_cparams(("parallel", "parallel"), 40),
        name="nsa_q_proj",
    )(h, g.reshape(1, D), wq, wg, bg, c, s1, s2)


def _stack_heads(qb):
    return jnp.concatenate([qb[:, r * HEAD_DIM:(r + 1) * HEAD_DIM] for r in range(HEADS_PER_GROUP)], axis=0)


def _gated_unstack(o, gates, g, branch, tq):
    lane = lax.broadcasted_iota(I32, gates.shape, 1)
    cols = []
    for r in range(HEADS_PER_GROUP):
        col = (g * HEADS_PER_GROUP + r) * N_BRANCH + branch
        gate = jnp.sum(jnp.where(lane == col, gates, 0.0), axis=-1, keepdims=True)
        cols.append(o[r * tq:(r + 1) * tq] * gate)
    return jnp.concatenate(cols, axis=-1)


def _query_pos(i, tq):
    rows = lax.broadcasted_iota(I32, (HEADS_PER_GROUP * tq, 1), 0)
    return i * tq + (rows & (tq - 1))


def _cmp_kernel(q_ref, gt_ref, k_ref, v_ref, o_ref, bias_ref, *, tq, nc):
    g = pl.program_id(1)
    i = pl.program_id(2)
    qs = _stack_heads(q_ref[0])
    s = _dot(qs, k_ref[0, 0])
    t = _query_pos(i, tq)
    n = lax.broadcasted_iota(I32, (1, nc), 1)
    s = jnp.where(n * CMP_STRIDE + (CMP_BLOCK - 1) <= t, s, -jnp.inf)
    m = jnp.max(s, axis=-1, keepdims=True)
    m = jnp.where(m == -jnp.inf, 0.0, m)
    e = jnp.exp(s - m)
    p = e / jnp.maximum(jnp.sum(e, axis=-1, keepdims=True), 1e-30)
    o = _dot(p.astype(BF16), v_ref[0, 0])
    o_ref[0] = _gated_unstack(o, gt_ref[0], g, 0, tq).astype(BF16)
    ps = p[0:tq]
    for r in range(1, HEADS_PER_GROUP):
        ps = ps + p[r * tq:(r + 1) * tq]
    nn = lax.broadcasted_iota(I32, (nc, MAX_SEL_BLOCKS), 0) * CMP_STRIDE
    jj = lax.broadcasted_iota(I32, (nc, MAX_SEL_BLOCKS), 1) * SEL_BLOCK
    c2s = jnp.where(nn < jj + SEL_BLOCK, jnp.where(nn + CMP_BLOCK > jj, 1.0, 0.0), 0.0)
    imp = jnp.dot(ps, c2s, precision=HIGHEST, preferred_element_type=F32)
    tq_pos = i * tq + lax.broadcasted_iota(I32, (tq, 1), 0)
    cur = tq_pos // SEL_BLOCK
    j = lax.broadcasted_iota(I32, (tq, MAX_SEL_BLOCKS), 1)
    valid = j <= cur
    bonus = jnp.where(j == 0, FORCE_BONUS, jnp.where(j == cur, FORCE_BONUS, jnp.where(j == cur - 1, FORCE_BONUS, 0.0)))
    work = jnp.where(valid, imp + bonus, -jnp.inf)
    sel = jnp.zeros((tq, MAX_SEL_BLOCKS), F32)
    for _ in range(N_SELECT):
        mx = jnp.max(work, axis=-1, keepdims=True)
        idx = jnp.min(jnp.where(work == mx, j, MAX_SEL_BLOCKS), axis=-1, keepdims=True)
        pick = j == idx
        sel = jnp.where(pick, 1.0, sel)
        work = jnp.where(pick, -jnp.inf, work)
    bias_ref[0, 0] = jnp.where(valid, jnp.where(sel > 0.0, 0.0, MASK_BIAS), MASK_BIAS).astype(BF16)


def _cmp_attn(q, gates, kct, vc):
    B, S, HD = q.shape
    G, dh = N_KV_GROUPS, HEAD_DIM
    gw = HD // G
    nc = kct.shape[-1]
    tq = min(Q_TILE, S)
    return pl.pallas_call(
        functools.partial(_cmp_kernel, tq=tq, nc=nc),
        out_shape=(jax.ShapeDtypeStruct((B, S, HD), BF16), jax.ShapeDtypeStruct((B, G, S, MAX_SEL_BLOCKS), BF16)),
        grid=(B, G, S // tq),
        in_specs=[
            pl.BlockSpec((1, tq, gw), lambda b, g, i: (b, i, g)),
            pl.BlockSpec((1, tq, LANES), lambda b, g, i: (b, i, 0)),
            pl.BlockSpec((1, 1, dh, nc), lambda b, g, i: (b, g, 0, 0)),
            pl.BlockSpec((1, 1, nc, dh), lambda b, g, i: (b, g, 0, 0)),
        ],
        out_specs=(pl.BlockSpec((1, tq, gw), lambda b, g, i: (b, i, g)),
                   pl.BlockSpec((1, 1, tq, MAX_SEL_BLOCKS), lambda b, g, i: (b, g, i, 0))),
        compiler_params=_cparams(("parallel", "parallel", "parallel"), 40),
        name="nsa_compressed",
    )(q, gates, kct, vc)


def _sel_kernel(q_ref, bias_ref, gt_ref, k_ref, v_ref, o_ref, qa, m_s, l_s, acc, *, tq, tk):
    g = pl.program_id(1)
    i = pl.program_id(2)
    qb = q_ref[0]
    bias = bias_ref[0, 0]
    for r in range(HEADS_PER_GROUP):
        qa[r * tq:(r + 1) * tq, 0:MAX_SEL_BLOCKS] = bias
        qa[r * tq:(r + 1) * tq, MAX_SEL_BLOCKS:] = qb[:, r * HEAD_DIM:(r + 1) * HEAD_DIM]
    m_s[...] = jnp.full_like(m_s, -jnp.inf)
    l_s[...] = jnp.zeros_like(l_s)
    acc[...] = jnp.zeros_like(acc)
    t = _query_pos(i, tq)
    lane = lax.broadcasted_iota(I32, (1, tk), 1)

    def body(jt, c):
        k0 = pl.multiple_of(jt * tk, tk)
        s = _dot(qa[...], k_ref[0, 0, :, pl.ds(k0, tk)])
        s = jnp.where(k0 + lane <= t, s, -jnp.inf)
        m_old = m_s[...]
        m_new = jnp.maximum(m_old, jnp.max(s, axis=-1, keepdims=True))
        alpha = jnp.exp(m_old - m_new)
        p = jnp.exp(s - m_new)
        l_s[...] = alpha * l_s[...] + jnp.sum(p, axis=-1, keepdims=True)
        acc[...] = alpha * acc[...] + _dot(p.astype(BF16), v_ref[0, 0, pl.ds(k0, tk), :])
        m_s[...] = m_new
        return c

    lax.fori_loop(0, (i * tq + tq + tk - 1) // tk, body, 0)
    o_ref[0] = _gated_unstack(acc[...] / l_s[...], gt_ref[0], g, 1, tq).astype(BF16)


def _sel_attn(q, bias, gates, kst, vs):
    B, S, HD = q.shape
    G, dh = N_KV_GROUPS, HEAD_DIM
    gw = HD // G
    kd = kst.shape[2]
    tq = min(Q_TILE, S)
    tk = min(KEY_TILE, S)
    rows = HEADS_PER_GROUP * tq
    return pl.pallas_call(
        functools.partial(_sel_kernel, tq=tq, tk=tk),
        out_shape=jax.ShapeDtypeStruct((B, S, HD), BF16),
        grid=(B, G, S // tq),
        in_specs=[
            pl.BlockSpec((1, tq, gw), lambda b, g, i: (b, i, g)),
            pl.BlockSpec((1, 1, tq, MAX_SEL_BLOCKS), lambda b, g, i: (b, g, i, 0)),
            pl.BlockSpec((1, tq, LANES), lambda b, g, i: (b, i, 0)),
            pl.BlockSpec((1, 1, kd, S), lambda b, g, i: (b, g, 0, 0)),
            pl.BlockSpec((1, 1, S, dh), lambda b, g, i: (b, g, 0, 0)),
        ],
        out_specs=pl.BlockSpec((1, tq, gw), lambda b, g, i: (b, i, g)),
        scratch_shapes=[pltpu.VMEM((rows, kd), BF16), pltpu.VMEM((rows, 1), F32), pltpu.VMEM((rows, 1), F32),
                        pltpu.VMEM((rows, dh), F32)],
        compiler_params=_cparams(("parallel", "parallel", "arbitrary"), 48),
        name="nsa_selected",
    )(q, bias, gates, kst, vs)


def _win_kernel(q_ref, gt_ref, k_ref, v_ref, o_ref, *, tq, wb):
    g = pl.program_id(1)
    i = pl.program_id(2)
    k0 = pl.multiple_of(jnp.maximum(i * tq - WINDOW, 0), tq)
    qs = _stack_heads(q_ref[0])
    s = _dot(qs, k_ref[0, 0, :, pl.ds(k0, wb)])
    t = _query_pos(i, tq)
    kpos = k0 + lax.broadcasted_iota(I32, (1, wb), 1)
    s = jnp.where(kpos <= t, jnp.where(kpos > t - WINDOW, s, -jnp.inf), -jnp.inf)
    m = jnp.max(s, axis=-1, keepdims=True)
    e = jnp.exp(s - m)
    p = e / jnp.sum(e, axis=-1, keepdims=True)
    o = _dot(p.astype(BF16), v_ref[0, 0, pl.ds(k0, wb), :])
    o_ref[0] = _gated_unstack(o, gt_ref[0], g, 2, tq).astype(BF16)


def _win_attn(q, gates, kwt, vw):
    B, S, HD = q.shape
    G, dh = N_KV_GROUPS, HEAD_DIM
    gw = HD // G
    tq = min(Q_TILE, S)
    wb = WINDOW + tq
    assert S >= wb and tq % LANES == 0
    return pl.pallas_call(
        functools.partial(_win_kernel, tq=tq, wb=wb),
        out_shape=jax.ShapeDtypeStruct((B, S, HD), BF16),
        grid=(B, G, S // tq),
        in_specs=[
            pl.BlockSpec((1, tq, gw), lambda b, g, i: (b, i, g)),
            pl.BlockSpec((1, tq, LANES), lambda b, g, i: (b, i, 0)),
            pl.BlockSpec((1, 1, dh, S), lambda b, g, i: (b, g, 0, 0)),
            pl.BlockSpec((1, 1, S, dh), lambda b, g, i: (b, g, 0, 0)),
        ],
        out_specs=pl.BlockSpec((1, tq, gw), lambda b, g, i: (b, i, g)),
        compiler_params=_cparams(("parallel", "parallel", "parallel"), 40),
        name="nsa_window",
    )(q, gates, kwt, vw)


LOG2E = 1.4426950408889634
Q_SCALE = HEAD_DIM ** -0.5 * LOG2E
V_ROWS = HEAD_DIM + 16
KEY_AUG = 2 * LANES
SEL_CHAINS = 1
NEG_BIG = -1e30


def _lane_rope_tables(pos):
    cos, sin = _rope_tables(pos)
    n = pos.shape[0]
    ones = jnp.ones((n, HEAD_DIM - ROPE_DIMS), F32)
    zeros = jnp.zeros((n, HEAD_DIM - ROPE_DIMS), F32)
    zh = jnp.zeros((n, ROPE_HALF), F32)
    two = lambda a: jnp.concatenate([a, a], axis=1)
    return (two(jnp.concatenate([cos, cos, ones], axis=1)), two(jnp.concatenate([zh, sin, zeros], axis=1)),
            two(jnp.concatenate([-sin, zh, zeros], axis=1)))


def _rope_lanes(x, c, s1, s2):
    return x * c + pltpu.roll(x, ROPE_HALF, axis=1) * s1 + pltpu.roll(x, LANES - ROPE_HALF, axis=1) * s2


def _kv2_kernel(h_ref, g_ref, wn_ref, wt_ref, c_ref, s1_ref, s2_ref, cv_ref, ksa_ref, vst_ref, kw_ref, vwt_ref, *, ts):
    i = pl.program_id(1)
    kvw = N_KV_GROUPS * HEAD_DIM
    hn = _rms(h_ref[0], g_ref[...]).astype(BF16)
    nat = _dot(hn, wn_ref[...])
    cv_ref[0] = nat[:, 0:2 * kvw]
    tr = _dot_nt(wt_ref[...], hn)
    c, s1, s2 = c_ref[...], s1_ref[...], s2_ref[...]
    roped = [_rope_lanes(nat[:, 2 * kvw + k * LANES:2 * kvw + (k + 1) * LANES], c, s1, s2)
             for k in range(2 * kvw // LANES)]
    lane = lax.broadcasted_iota(I32, (ts, KEY_AUG), 1)
    pos = i * ts + lax.broadcasted_iota(I32, (ts, KEY_AUG), 0)
    onehot = jnp.where(lane - HEAD_DIM == pos // SEL_BLOCK, 1.0, 0.0).astype(BF16)
    ones_row = jnp.where(lax.broadcasted_iota(I32, (V_ROWS - HEAD_DIM, ts), 0) == 0, 1.0, 0.0).astype(BF16)
    per_tile = LANES // HEAD_DIM
    for g in range(N_KV_GROUPS):
        lo = (g % per_tile) * HEAD_DIM
        ksa_ref[0, g] = onehot
        ksa_ref[0, g, :, 0:HEAD_DIM] = roped[g // per_tile][:, lo:lo + HEAD_DIM].astype(BF16)
        kw_ref[0, g] = roped[N_KV_GROUPS // per_tile + g // per_tile][:, lo:lo + HEAD_DIM].astype(BF16)
        for ref, base in ((vst_ref, 0), (vwt_ref, kvw)):
            ref[0, g, 0:HEAD_DIM, :] = tr[base + g * HEAD_DIM:base + (g + 1) * HEAD_DIM].astype(BF16)
            ref[0, g, HEAD_DIM:, :] = ones_row


def _shared_kv_proj2(h, kv_norm, w_kv):
    B, S, D = h.shape
    G, dh = N_KV_GROUPS, HEAD_DIM
    kvw = G * dh
    ts = min(SEQ_TILE, S)
    br = lambda k: w_kv[:, k * kvw:(k + 1) * kvw]
    w_nat = jnp.concatenate([br(0), br(1), br(2), br(4)], axis=1).astype(BF16)
    w_tr = jnp.concatenate([br(3), br(5)], axis=1).T.astype(BF16)
    tabs = _lane_rope_tables(jnp.arange(S))
    tab = pl.BlockSpec((ts, LANES), lambda b, i: (i, 0))
    return pl.pallas_call(
        functools.partial(_kv2_kernel, ts=ts),
        out_shape=(
            jax.ShapeDtypeStruct((B, S, 2 * kvw), F32),
            jax.ShapeDtypeStruct((B, G, S, KEY_AUG), BF16),
            jax.ShapeDtypeStruct((B, G, V_ROWS, S), BF16),
            jax.ShapeDtypeStruct((B, G, S, dh), BF16),
            jax.ShapeDtypeStruct((B, G, V_ROWS, S), BF16),
        ),
        grid=(B, S // ts),
        in_specs=[
            pl.BlockSpec((1, ts, D), lambda b, i: (b, i, 0)),
            pl.BlockSpec((1, D), lambda b, i: (0, 0)),
            pl.BlockSpec((D, 4 * kvw), lambda b, i: (0, 0)),
            pl.BlockSpec((2 * kvw, D), lambda b, i: (0, 0)),
            tab, tab, tab,
        ],
        out_specs=(
            pl.BlockSpec((1, ts, 2 * kvw), lambda b, i: (b, i, 0)),
            pl.BlockSpec((1, G, ts, KEY_AUG), lambda b, i: (b, 0, i, 0)),
            pl.BlockSpec((1, G, V_ROWS, ts), lambda b, i: (b, 0, 0, i)),
            pl.BlockSpec((1, G, ts, dh), lambda b, i: (b, 0, i, 0)),
            pl.BlockSpec((1, G, V_ROWS, ts), lambda b, i: (b, 0, 0, i)),
        ),
        compiler_params=_cparams(("parallel", "parallel"), 48),
        name="shared_kv_proj",
    )(h, kv_norm.reshape(1, D), w_nat, w_tr, *tabs)


def _compress2_kernel(x_ref, pos_ref, w1_ref, b1_ref, w2_ref, b2_ref, c_ref, s1_ref, s2_ref, o_ref, *, nh, keys):
    x = x_ref[0, 0, 0]
    a = _dot((x + pos_ref[0:1]).astype(BF16), w1_ref[0])
    b = _dot((x + pos_ref[1:2]).astype(BF16), w1_ref[1])
    hid = jax.nn.gelu(a + pltpu.roll(b, nh - 1, axis=0) + b1_ref[...]).astype(BF16)
    if keys:
        out = _rope_lanes(_dot(hid, w2_ref[...]) + b2_ref[...], c_ref[...], s1_ref[...], s2_ref[...])
        o_ref[0, 0] = out[:, 0:HEAD_DIM].astype(BF16)
    else:
        o_ref[0, 0] = (_dot_nt(w2_ref[...], hid) + b2_ref[...]).astype(BF16)


def _compress2(halves, which, pos_emb, w1, b1, w2, b2, keys):
    _, B, G, nh, hw = halves.shape
    dh = HEAD_DIM
    hidden = w1.shape[-1]
    pos2 = pos_emb.reshape(2, hw)
    w1s = w1.reshape(2, hw, hidden).astype(BF16)
    tabs = _lane_rope_tables(jnp.arange(nh) * CMP_STRIDE + CMP_BLOCK - 1)
    if keys:
        w2a = jnp.pad(w2, ((0, 0), (0, LANES - dh))).astype(BF16)
        b2a = jnp.pad(b2, (0, LANES - dh)).reshape(1, LANES)
        out_shape, out_block = (B, G, nh, dh), (1, 1, nh, dh)
    else:
        w2a, b2a = w2.T.astype(BF16), b2.reshape(dh, 1)
        out_shape, out_block = (B, G, dh, nh), (1, 1, dh, nh)
    c2 = lambda b, g: (0, 0)
    tab = pl.BlockSpec((nh, LANES), c2)
    return pl.pallas_call(
        functools.partial(_compress2_kernel, nh=nh, keys=keys),
        out_shape=jax.ShapeDtypeStruct(out_shape, BF16),
        grid=(B, G),
        in_specs=[
            pl.BlockSpec((1, 1, 1, nh, hw), lambda b, g: (which, b, g, 0, 0)),
            pl.BlockSpec((2, hw), c2),
            pl.BlockSpec((2, hw, hidden), lambda b, g: (0, 0, 0)),
            pl.BlockSpec((1, hidden), c2),
            pl.BlockSpec(w2a.shape, c2),
            pl.BlockSpec(b2a.shape, c2),
            tab, tab, tab,
        ],
        out_specs=pl.BlockSpec(out_block, lambda b, g: (b, g, 0, 0)),
        compiler_params=_cparams(("parallel", "parallel"), 40),
        name="compress_k" if keys else "compress_v",
    )(halves, pos2, w1s, b1.reshape(1, hidden), w2a, b2a, *tabs)


def _qproj2_kernel(h_ref, g_ref, wqt_ref, wg_ref, bg_ref, cos_ref, sin_ref, qt_ref, gt_ref):
    xn = _rms(h_ref[0], g_ref[...]).astype(BF16)
    tr = _dot_nt(wqt_ref[...], xn)
    cos, sin = cos_ref[...], sin_ref[...]
    for hd in range(N_HEADS):
        rows = slice(hd * HEAD_DIM, (hd + 1) * HEAD_DIM)
        qt_ref[0, rows, :] = (_rope_rows(tr[rows], cos, sin) * Q_SCALE).astype(BF16)
    gt_ref[0] = jax.nn.sigmoid(_dot(xn, wg_ref[...]) + bg_ref[...])


def _q_proj2(h, g, w_qg, b_gate):
    B, S, D = h.shape
    HD = N_HEADS * HEAD_DIM
    ng = N_HEADS * N_BRANCH
    ts = min(SEQ_TILE, S)
    wqt = w_qg[:, :HD].T.astype(BF16)
    wg = jnp.pad(w_qg[:, HD:], ((0, 0), (0, LANES - ng))).astype(BF16)
    bg = jnp.pad(b_gate, (0, LANES - ng)).reshape(1, LANES)
    cos, sin = _rope_tables(jnp.arange(S))
    tab = pl.BlockSpec((ROPE_HALF, ts), lambda b, i: (0, i))
    return pl.pallas_call(
        _qproj2_kernel,
        out_shape=(jax.ShapeDtypeStruct((B, HD, S), BF16), jax.ShapeDtypeStruct((B, S, LANES), F32)),
        grid=(B, S // ts),
        in_specs=[
            pl.BlockSpec((1, ts, D), lambda b, i: (b, i, 0)),
            pl.BlockSpec((1, D), lambda b, i: (0, 0)),
            pl.BlockSpec((HD, D), lambda b, i: (0, 0)),
            pl.BlockSpec((D, LANES), lambda b, i: (0, 0)),
            pl.BlockSpec((1, LANES), lambda b, i: (0, 0)),
            tab, tab,
        ],
        out_specs=(pl.BlockSpec((1, HD, ts), lambda b, i: (b, 0, i)),
                   pl.BlockSpec((1, ts, LANES), lambda b, i: (b, i, 0))),
        compiler_params=_cparams(("parallel", "parallel"), 40),
        name="nsa_q_proj",
    )(h, g.reshape(1, D), wqt, wg, bg, cos.T, sin.T)


def _heads_on_lanes(qt):
    return jnp.concatenate([qt[r * HEAD_DIM:(r + 1) * HEAD_DIM] for r in range(HEADS_PER_GROUP)], axis=1)


def _lane_query_pos(i, tq):
    lanes = lax.broadcasted_iota(I32, (1, HEADS_PER_GROUP * tq), 1)
    return i * tq + (lanes & (tq - 1))


def _finish_heads(acc_t, gates, g, branch, tq, denom_row):
    lane = lax.broadcasted_iota(I32, gates.shape, 1)
    rows = acc_t.shape[0]
    cols = []
    for r in range(HEADS_PER_GROUP):
        blk = acc_t[:, r * tq:(r + 1) * tq]
        nat = jnp.concatenate([blk, jnp.zeros((tq - rows, tq), F32)], axis=0).T
        col = (g * HEADS_PER_GROUP + r) * N_BRANCH + branch
        scale = jnp.sum(jnp.where(lane == col, gates, 0.0), axis=-1, keepdims=True)
        if denom_row is not None:
            scale = scale / nat[:, denom_row:denom_row + 1]
        cols.append(nat[:, 0:HEAD_DIM] * scale)
    return jnp.concatenate(cols, axis=-1)


def _cmp2_kernel(qt_ref, gt_ref, k_ref, vt_ref, c2s_ref, o_ref, bias_ref, *, tq, nc):
    g = pl.program_id(1)
    i = pl.program_id(2)
    s = _dot(k_ref[0, 0], _heads_on_lanes(qt_ref[0]))
    t = _lane_query_pos(i, tq)
    n = lax.broadcasted_iota(I32, (nc, 1), 0)
    s = jnp.where(n * CMP_STRIDE + (CMP_BLOCK - 1) <= t, s, -jnp.inf)
    m = jnp.max(s, axis=0, keepdims=True)
    m = jnp.where(m == -jnp.inf, 0.0, m)
    e = jnp.exp2(s - m)
    p = e * (1.0 / jnp.maximum(jnp.sum(e, axis=0, keepdims=True), 1e-30))
    ot = _dot(vt_ref[0, 0], p.astype(BF16))
    o_ref[0] = _finish_heads(ot, gt_ref[0], g, 0, tq, None).astype(BF16)
    ps = p[:, 0:tq]
    for r in range(1, HEADS_PER_GROUP):
        ps = ps + p[:, r * tq:(r + 1) * tq]
    hi = ps.astype(BF16)
    rem = ps - hi.astype(F32)
    mid = rem.astype(BF16)
    lo = (rem - mid.astype(F32)).astype(BF16)
    c2s = c2s_ref[...]
    imp = _dot(c2s, hi) + _dot(c2s, mid) + _dot(c2s, lo)
    cur = (i * tq + lax.broadcasted_iota(I32, (1, tq), 1)) // SEL_BLOCK
    j = lax.broadcasted_iota(I32, (MAX_SEL_BLOCKS, tq), 0)
    valid = j <= cur
    bonus = jnp.where(j == 0, FORCE_BONUS, jnp.where(j == cur, FORCE_BONUS, jnp.where(j == cur - 1, FORCE_BONUS, 0.0)))
    work = jnp.where(valid, imp + bonus, -jnp.inf)
    sel = jnp.zeros((MAX_SEL_BLOCKS, tq), F32)
    for _ in range(N_SELECT):
        mx = jnp.max(work, axis=0, keepdims=True)
        idx = jnp.min(jnp.where(work == mx, j, MAX_SEL_BLOCKS), axis=0, keepdims=True)
        pick = j == idx
        sel = jnp.where(pick, 1.0, sel)
        work = jnp.where(pick, -jnp.inf, work)
    bias_ref[0, 0] = jnp.where(valid, jnp.where(sel > 0.0, 0.0, MASK_BIAS), MASK_BIAS).astype(BF16)


def _cmp_attn2(qt, gates, kc, vct):
    B, HD, S = qt.shape
    G, dh = N_KV_GROUPS, HEAD_DIM
    gw = HD // G
    nc = kc.shape[2]
    tq = min(Q_TILE, S)
    n0 = jnp.arange(nc)[None, :] * CMP_STRIDE
    j0 = jnp.arange(MAX_SEL_BLOCKS)[:, None] * SEL_BLOCK
    c2s = ((n0 < j0 + SEL_BLOCK) & (n0 + CMP_BLOCK > j0)).astype(BF16)
    return pl.pallas_call(
        functools.partial(_cmp2_kernel, tq=tq, nc=nc),
        out_shape=(jax.ShapeDtypeStruct((B, S, HD), BF16), jax.ShapeDtypeStruct((B, G, MAX_SEL_BLOCKS, S), BF16)),
        grid=(B, G, S // tq),
        in_specs=[
            pl.BlockSpec((1, gw, tq), lambda b, g, i: (b, g, i)),
            pl.BlockSpec((1, tq, LANES), lambda b, g, i: (b, i, 0)),
            pl.BlockSpec((1, 1, nc, dh), lambda b, g, i: (b, g, 0, 0)),
            pl.BlockSpec((1, 1, dh, nc), lambda b, g, i: (b, g, 0, 0)),
            pl.BlockSpec((MAX_SEL_BLOCKS, nc), lambda b, g, i: (0, 0)),
        ],
        out_specs=(pl.BlockSpec((1, tq, gw), lambda b, g, i: (b, i, g)),
                   pl.BlockSpec((1, 1, MAX_SEL_BLOCKS, tq), lambda b, g, i: (b, g, 0, i))),
        compiler_params=_cparams(("parallel", "parallel", "parallel"), 40),
        name="nsa_compressed",
    )(qt, gates, kc, vct, c2s)


def _sel2_kernel(qt_ref, bias_ref, gt_ref, k_ref, vt_ref, o_ref, qa, m_s, acc, s_a, s_b, *, tq, tk):
    g = pl.program_id(1)
    i = pl.program_id(2)
    qt = qt_ref[0]
    bias = bias_ref[0, 0]
    for r in range(HEADS_PER_GROUP):
        cols = slice(r * tq, (r + 1) * tq)
        qa[0:HEAD_DIM, cols] = qt[r * HEAD_DIM:(r + 1) * HEAD_DIM]
        qa[HEAD_DIM:HEAD_DIM + MAX_SEL_BLOCKS, cols] = bias
        qa[HEAD_DIM + MAX_SEL_BLOCKS:, cols] = jnp.zeros((KEY_AUG - HEAD_DIM - MAX_SEL_BLOCKS, tq), BF16)
    for c in range(SEL_CHAINS):
        m_s[c] = jnp.full(m_s.shape[1:], NEG_BIG, F32)
        acc[c] = jnp.zeros(acc.shape[1:], F32)
    t = _lane_query_pos(i, tq)
    ck = tk // SEL_CHAINS

    def qk(jt, buf):
        buf[...] = _dot(k_ref[0, 0, pl.ds(pl.multiple_of(jt * tk, tk), tk), :], qa[...])

    def absorb(jt, buf, diagonal):
        for c in range(SEL_CHAINS):
            kc = pl.multiple_of(jt * tk + c * ck, ck)
            s = buf[c * ck:(c + 1) * ck, :]
            if diagonal:
                s = jnp.where(kc + lax.broadcasted_iota(I32, (ck, 1), 0) <= t, s, NEG_BIG)
            m_old = m_s[c]
            m_new = jnp.maximum(m_old, jnp.max(s, axis=0, keepdims=True))
            p = jnp.exp2(s - m_new).astype(BF16)
            acc[c] = jnp.exp2(m_old - m_new) * acc[c] + _dot(vt_ref[0, 0, :, pl.ds(kc, ck)], p)
            m_s[c] = m_new

    def pair(u, c):
        qk(2 * u + 1, s_b)
        absorb(2 * u, s_a, False)
        qk(2 * u + 2, s_a)
        absorb(2 * u + 1, s_b, False)
        return c

    last = (i * tq + tq + tk - 1) // tk - 1
    qk(0, s_a)
    lax.fori_loop(0, last // 2, pair, 0)

    @pl.when(last % 2 == 1)
    def _():
        qk(last, s_b)
        absorb(last - 1, s_a, False)
        absorb(last, s_b, True)

    @pl.when(last % 2 == 0)
    def _():
        absorb(last, s_a, True)

    m_all = m_s[0]
    for c in range(1, SEL_CHAINS):
        m_all = jnp.maximum(m_all, m_s[c])
    acc_t = jnp.exp2(m_s[0] - m_all) * acc[0]
    for c in range(1, SEL_CHAINS):
        acc_t = acc_t + jnp.exp2(m_s[c] - m_all) * acc[c]
    o_ref[0] = _finish_heads(acc_t, gt_ref[0], g, 1, tq, HEAD_DIM).astype(BF16)


def _sel_attn2(qt, bias, gates, ksa, vst):
    B, HD, S = qt.shape
    G = N_KV_GROUPS
    gw = HD // G
    tq = min(Q_TILE, S)
    tk = min(KEY_TILE, S)
    width = HEADS_PER_GROUP * tq
    return pl.pallas_call(
        functools.partial(_sel2_kernel, tq=tq, tk=tk),
        out_shape=jax.ShapeDtypeStruct((B, S, HD), BF16),
        grid=(B, G, S // tq),
        in_specs=[
            pl.BlockSpec((1, gw, tq), lambda b, g, i: (b, g, i)),
            pl.BlockSpec((1, 1, MAX_SEL_BLOCKS, tq), lambda b, g, i: (b, g, 0, i)),
            pl.BlockSpec((1, tq, LANES), lambda b, g, i: (b, i, 0)),
            pl.BlockSpec((1, 1, S, KEY_AUG), lambda b, g, i: (b, g, 0, 0)),
            pl.BlockSpec((1, 1, V_ROWS, S), lambda b, g, i: (b, g, 0, 0)),
        ],
        out_specs=pl.BlockSpec((1, tq, gw), lambda b, g, i: (b, i, g)),
        scratch_shapes=[pltpu.VMEM((KEY_AUG, width), BF16), pltpu.VMEM((SEL_CHAINS, 1, width), F32),
                        pltpu.VMEM((SEL_CHAINS, V_ROWS, width), F32),
                        pltpu.VMEM((tk, width), F32), pltpu.VMEM((tk, width), F32)],
        compiler_params=_cparams(("parallel", "parallel", "arbitrary"), 48),
        name="nsa_selected",
    )(qt, bias, gates, ksa, vst)


def _win2_kernel(qt_ref, gt_ref, k_ref, vt_ref, o_ref, *, tq, wb):
    g = pl.program_id(1)
    i = pl.program_id(2)
    k0 = pl.multiple_of(jnp.maximum(i * tq - WINDOW, 0), tq)
    s = _dot(k_ref[0, 0, pl.ds(k0, wb), :], _heads_on_lanes(qt_ref[0]))
    t = _lane_query_pos(i, tq)
    kpos = k0 + lax.broadcasted_iota(I32, (wb, 1), 0)
    s = jnp.where(kpos <= t, jnp.where(kpos > t - WINDOW, s, -jnp.inf), -jnp.inf)
    m = jnp.max(s, axis=0, keepdims=True)
    p = jnp.exp2(s - m).astype(BF16)
    acc = _dot(vt_ref[0, 0, :, pl.ds(k0, wb)], p)
    o_ref[0] = _finish_heads(acc, gt_ref[0], g, 2, tq, HEAD_DIM).astype(BF16)


def _win_attn2(qt, gates, kw, vwt):
    B, HD, S = qt.shape
    G, dh = N_KV_GROUPS, HEAD_DIM
    gw = HD // G
    tq = min(Q_TILE, S)
    wb = WINDOW + tq
    assert S >= wb and tq % LANES == 0
    return pl.pallas_call(
        functools.partial(_win2_kernel, tq=tq, wb=wb),
        out_shape=jax.ShapeDtypeStruct((B, S, HD), BF16),
        grid=(B, G, S // tq),
        in_specs=[
            pl.BlockSpec((1, gw, tq), lambda b, g, i: (b, g, i)),
            pl.BlockSpec((1, tq, LANES), lambda b, g, i: (b, i, 0)),
            pl.BlockSpec((1, 1, S, dh), lambda b, g, i: (b, g, 0, 0)),
            pl.BlockSpec((1, 1, V_ROWS, S), lambda b, g, i: (b, g, 0, 0)),
        ],
        out_specs=pl.BlockSpec((1, tq, gw), lambda b, g, i: (b, i, g)),
        compiler_params=_cparams(("parallel", "parallel", "parallel"), 40),
        name="nsa_window",
    )(qt, gates, kw, vwt)


def _oproj_kernel(h_ref, a_ref, b_ref, c_ref, w_ref, o_ref):
    o = a_ref[...].astype(F32) + b_ref[...].astype(F32) + c_ref[...].astype(F32)
    o_ref[...] = h_ref[...] + _dot(o.astype(BF16), w_ref[...])


def _out_proj(h2, oc, os_, ow, w_o):
    T, D = h2.shape
    HD = oc.shape[-1]
    tt = min(SEQ_TILE, T)
    blk = lambda w: pl.BlockSpec((tt, w), lambda i: (i, 0))
    return pl.pallas_call(
        _oproj_kernel,
        out_shape=jax.ShapeDtypeStruct((T, D), F32),
        grid=(T // tt,),
        in_specs=[blk(D), blk(HD), blk(HD), blk(HD), pl.BlockSpec((HD, D), lambda i: (0, 0))],
        out_specs=blk(D),
        compiler_params=_cparams(("parallel",), 40),
        name="nsa_out_proj",
    )(h2, oc, os_, ow, w_o.astype(BF16))


def _nsa_layer(h, g, w_qg, b_gate, w_o, shared):
    B, S, D = h.shape
    kc, vct, ksa, vst, kw, vwt = shared
    qt, gates = _q_proj2(h, g, w_qg, b_gate)
    oc, bias = _cmp_attn2(qt, gates, kc, vct)
    os_ = _sel_attn2(qt, bias, gates, ksa, vst)
    ow = _win_attn2(qt, gates, kw, vwt)
    flat = lambda a: a.reshape(B * S, a.shape[-1])
    return _out_proj(flat(h), flat(oc), flat(os_), flat(ow), w_o).reshape(B, S, D)


def _shared_kv(h, kv_norm, w_kv, ck, cv):
    B, S, _ = h.shape
    G, dh = N_KV_GROUPS, HEAD_DIM
    assert S % SEL_BLOCK == 0 and S // SEL_BLOCK <= MAX_SEL_BLOCKS
    cvals, ksa, vst, kw, vwt = _shared_kv_proj2(h, kv_norm, w_kv)
    halves = cvals.reshape(B, S, 2, G, dh).transpose(2, 0, 3, 1, 4).reshape(2, B, G, S // CMP_STRIDE, CMP_STRIDE * dh)
    kc = _compress2(halves, 0, *ck, keys=True)
    vct = _compress2(halves, 1, *cv, keys=False)
    return kc, vct, ksa, vst, kw, vwt


def kernel(x, p, norm_mix, norm_ffn, norm_ple, pool_w, pool_b, pool_scale, kv_norm, w_kv, cmp_k_pos, cmp_k_w1, cmp_k_b1, cmp_k_w2, cmp_k_b2, cmp_v_pos, cmp_v_w1, cmp_v_b1, cmp_v_w2, cmp_v_b2, w_qg, b_gate, w_o, router_g_w, router_g_b, router_e_w, router_e_b, moe_w1, moe_w3, moe_w2, ple_proj, ple_gate_w, ple_gate_b, final_norm):
    B, S, D = x.shape
    depth = p.shape[0]
    n_a = pool_w.shape[0]
    T = B * S
    h = x
    shared = None
    for i in range(depth):
        if i == n_a:
            shared = _shared_kv(h, kv_norm, w_kv,
                                (cmp_k_pos, cmp_k_w1, cmp_k_b1, cmp_k_w2, cmp_k_b2),
                                (cmp_v_pos, cmp_v_w1, cmp_v_b1, cmp_v_w2, cmp_v_b2))
        if i < n_a:
            h = _pool_layer(h, norm_mix[i], pool_w[i], pool_b[i], pool_scale[i])
        else:
            j = i - n_a
            h = _nsa_layer(h, norm_mix[i], w_qg[j], b_gate[j], w_o[j], shared)
        h = _moe_ple_layer(h.reshape(T, D), p.reshape(depth, T, p.shape[-1]), i, norm_ffn[i], router_g_w[i],
                           router_g_b[i], router_e_w[i], router_e_b[i], moe_w1, moe_w3, moe_w2, norm_ple[i],
                           ple_gate_w[i], ple_gate_b[i], ple_proj[i], final_norm, i == depth - 1).reshape(B, S, D)
    return h
```

```python
import functools

import jax
import jax.numpy as jnp
from jax import lax
from jax.experimental import pallas as pl
from jax.experimental.pallas import tpu as pltpu

F32 = jnp.float32
BF16 = jnp.bfloat16
I32 = jnp.int32

POOL_WINDOWS = (2, 4, 8, 16)
N_HEADS = 16
HEAD_DIM = 64
N_KV_GROUPS = 4
HEADS_PER_GROUP = N_HEADS // N_KV_GROUPS
N_BRANCH = 3
ROPE_DIMS = HEAD_DIM // 4
ROPE_HALF = ROPE_DIMS // 2
ROPE_THETA = 500000.0
CMP_BLOCK = 32
CMP_STRIDE = 16
SEL_BLOCK = 64
N_SELECT = 16
WINDOW = 512
FORCE_BONUS = 1e4
N_EXPERT_GROUPS = 4
EXPERTS_PER_GROUP = 8
N_EXPERTS = N_EXPERT_GROUPS * EXPERTS_PER_GROUP
TOP_K_IN_GROUP = 2
RMS_EPS = 1e-6

LANES = 128
MAX_SEL_BLOCKS = LANES
MASK_BIAS = -30000.0

SEQ_TILE = 512
TOK_TILE = 512
CMB_TILE = 256
MOE_CHUNK = 512
Q_TILE = 256
KEY_TILE = 512
HALO = 16

HIGHEST = lax.Precision.HIGHEST


def _cparams(sem, vmem_mb):
    return pltpu.CompilerParams(dimension_semantics=sem, vmem_limit_bytes=vmem_mb * 1024 * 1024)


def _rms(x, g):
    return x * lax.rsqrt(jnp.mean(x * x, axis=-1, keepdims=True) + RMS_EPS) * g


def _dot(a, b):
    return jnp.dot(a, b, preferred_element_type=F32)


def _dot_nt(a, b):
    return lax.dot_general(a, b, (((1,), (1,)), ((), ())), preferred_element_type=F32)


def _pool_kernel(h_ref, halo_ref, g_ref, w_ref, b_ref, sc_ref, o_ref, *, ts, cg):
    i = pl.program_id(1)
    x = h_ref[0]
    g = g_ref[...]
    xn = _rms(x, g)
    hn = _rms(halo_ref[0], g)
    hn = jnp.where(i > 0, hn, 0.0)
    ext = jnp.concatenate([hn, xn], axis=0)
    t = i * ts + lax.broadcasted_iota(I32, (ts, 1), 0)
    outs = []
    for gi, w in enumerate(POOL_WINDOWS):
        s = ext[:, gi * cg:(gi + 1) * cg]
        k = 1
        while k < w:
            s = s + pltpu.roll(s, k, axis=0)
            k *= 2
        cnt = jnp.minimum(t + 1, w).astype(F32)
        pooled = s[HALO:] / cnt - xn[:, gi * cg:(gi + 1) * cg]
        outs.append(_dot(pooled.astype(BF16), w_ref[gi]))
    y = jnp.concatenate(outs, axis=-1)
    o_ref[0] = x + (y + b_ref[...]) * sc_ref[...]


def _pool_layer(h, g, w, b, sc):
    B, S, D = h.shape
    ts = min(SEQ_TILE, S)
    cg = D // len(POOL_WINDOWS)
    row = lambda v: v.reshape(1, D)
    return pl.pallas_call(
        functools.partial(_pool_kernel, ts=ts, cg=cg),
        out_shape=jax.ShapeDtypeStruct((B, S, D), F32),
        grid=(B, S // ts),
        in_specs=[
            pl.BlockSpec((1, ts, D), lambda b_, i: (b_, i, 0)),
            pl.BlockSpec((1, HALO, D), lambda b_, i: (b_, jnp.maximum(i * (ts // HALO) - 1, 0), 0)),
            pl.BlockSpec((1, D), lambda b_, i: (0, 0)),
            pl.BlockSpec((len(POOL_WINDOWS), cg, cg), lambda b_, i: (0, 0, 0)),
            pl.BlockSpec((1, D), lambda b_, i: (0, 0)),
            pl.BlockSpec((1, D), lambda b_, i: (0, 0)),
        ],
        out_specs=pl.BlockSpec((1, ts, D), lambda b_, i: (b_, i, 0)),
        compiler_params=_cparams(("parallel", "parallel"), 40),
        name="pool_mixer",
    )(h, h, row(g), w.astype(BF16), row(b), row(sc))


def _router_kernel(h_ref, g_ref, wh_ref, wl_ref, b_ref, info_ref, cnt_ref, *, tt):
    i = pl.program_id(0)

    @pl.when(i == 0)
    def _():
        cnt_ref[...] = jnp.zeros_like(cnt_ref)

    xn = _rms(h_ref[...], g_ref[...])
    xh = xn.astype(BF16)
    xl = (xn - xh.astype(F32)).astype(BF16)
    logits = _dot(xh, wh_ref[...]) + (_dot(xh, wl_ref[...]) + _dot(xl, wh_ref[...])) + b_ref[...]
    lane = lax.broadcasted_iota(I32, (tt, LANES), 1)
    neg = -jnp.inf
    gl = jnp.where(lane < N_EXPERT_GROUPS, logits, neg)
    gmax = jnp.max(gl, axis=-1, keepdims=True)
    grp = jnp.min(jnp.where(gl == gmax, lane, LANES), axis=-1, keepdims=True)
    gprob = 1.0 / jnp.sum(jnp.exp(gl - gmax), axis=-1, keepdims=True)
    lo = N_EXPERT_GROUPS + grp * EXPERTS_PER_GROUP
    el = jnp.where(lane >= lo, jnp.where(lane < lo + EXPERTS_PER_GROUP, logits, neg), neg)
    v1 = jnp.max(el, axis=-1, keepdims=True)
    i1 = jnp.min(jnp.where(el == v1, lane, LANES), axis=-1, keepdims=True)
    el2 = jnp.where(lane == i1, neg, el)
    v2 = jnp.max(el2, axis=-1, keepdims=True)
    i2 = jnp.min(jnp.where(el2 == v2, lane, LANES), axis=-1, keepdims=True)
    e2 = jnp.exp(v2 - v1)
    w1 = gprob / (1.0 + e2)
    w2 = gprob * e2 / (1.0 + e2)
    oh1 = lane == i1
    oh2 = lane == i2
    oh = jnp.where(oh1, 1.0, jnp.where(oh2, 1.0, 0.0))
    r_ = lax.broadcasted_iota(I32, (tt, tt), 0)
    c_ = lax.broadcasted_iota(I32, (tt, tt), 1)
    tri = jnp.where(r_ > c_, 1.0, 0.0).astype(BF16)
    tot = _dot(tri, oh.astype(BF16)) + cnt_ref[...]
    r1 = jnp.sum(jnp.where(oh1, tot, 0.0), axis=-1, keepdims=True)
    r2 = jnp.sum(jnp.where(oh2, tot, 0.0), axis=-1, keepdims=True)
    cnt_ref[...] = cnt_ref[...] + jnp.sum(oh, axis=0, keepdims=True)
    vals = (i1.astype(F32) - N_EXPERT_GROUPS, i2.astype(F32) - N_EXPERT_GROUPS, w1, w2, r1, r2)
    info = jnp.zeros((tt, LANES), F32)
    for k, v in enumerate(vals):
        info = jnp.where(lane == k, v, info)
    info_ref[...] = info


def _router(h2, g, wr, br):
    T, D = h2.shape
    tt = min(TOK_TILE, T)
    return pl.pallas_call(
        functools.partial(_router_kernel, tt=tt),
        out_shape=(jax.ShapeDtypeStruct((T, LANES), F32), jax.ShapeDtypeStruct((1, LANES), F32)),
        grid=(T // tt,),
        in_specs=[
            pl.BlockSpec((tt, D), lambda i: (i, 0)),
            pl.BlockSpec((1, D), lambda i: (0, 0)),
            pl.BlockSpec((D, LANES), lambda i: (0, 0)),
            pl.BlockSpec((D, LANES), lambda i: (0, 0)),
            pl.BlockSpec((1, LANES), lambda i: (0, 0)),
        ],
        out_specs=(pl.BlockSpec((tt, LANES), lambda i: (i, 0)), pl.BlockSpec((1, LANES), lambda i: (0, 0))),
        compiler_params=_cparams(("arbitrary",), 40),
        name="moe_router",
    )(h2, g.reshape(1, D), wr.astype(BF16), (wr - wr.astype(BF16).astype(F32)).astype(BF16), br)


def _row_copy(src, s, dst, d, sem):
    return pltpu.make_async_copy(src.at[pl.ds(s, 1)], dst.at[pl.ds(d, 1)], sem)


def _dispatch_kernel(dest_ref, cend_ref, h_ref, g_ref, xs_ref, xn_s, sem, *, tt, n_tok, n_chunks):
    base = pl.program_id(0) * tt

    @pl.when(pl.program_id(0) == 0)
    def _():
        xn_s[...] = jnp.zeros_like(xn_s)

        def tail(e):
            nonempty = cend_ref[e] > (cend_ref[e - 1] if e > 0 else 0)
            row = pl.multiple_of((cend_ref[e] - 1) * tt, tt)
            return nonempty, pltpu.make_async_copy(xn_s, xs_ref.at[pl.ds(row, tt)], sem)

        def unused(c):
            return pltpu.make_async_copy(xn_s, xs_ref.at[pl.ds(pl.multiple_of(c * tt, tt), tt)], sem)

        n_used = cend_ref[N_EXPERTS - 1]
        for e in range(N_EXPERTS):
            nonempty, cp = tail(e)
            pl.when(nonempty)(cp.start)
        lax.fori_loop(n_used, n_chunks, lambda c, z: (unused(c).start(), z)[1], 0)
        for e in range(N_EXPERTS):
            nonempty, cp = tail(e)
            pl.when(nonempty)(cp.wait)
        lax.fori_loop(n_used, n_chunks, lambda c, z: (unused(c).wait(), z)[1], 0)

    xn_s[...] = _rms(h_ref[...], g_ref[...])

    def issue(r, c):
        for k in range(TOP_K_IN_GROUP):
            _row_copy(xn_s, r, xs_ref, dest_ref[k * n_tok + base + r], sem).start(priority=k % 2)
        return c

    lax.fori_loop(0, tt, issue, 0, unroll=8)
    for k in range(TOP_K_IN_GROUP):
        pltpu.make_async_copy(xn_s, xs_ref.at[pl.ds(0, tt)], sem).wait()


def _dispatch(dest, chunk_end, h2, g, n_rows):
    T, D = h2.shape
    tt = MOE_CHUNK
    assert T % tt == 0
    return pl.pallas_call(
        functools.partial(_dispatch_kernel, tt=tt, n_tok=T, n_chunks=n_rows // tt),
        out_shape=jax.ShapeDtypeStruct((n_rows, D), F32),
        grid_spec=pltpu.PrefetchScalarGridSpec(
            num_scalar_prefetch=2,
            grid=(T // tt,),
            in_specs=[
                pl.BlockSpec((tt, D), lambda i, d, ce: (i, 0)),
                pl.BlockSpec((1, D), lambda i, d, ce: (0, 0)),
            ],
            out_specs=pl.BlockSpec(memory_space=pl.ANY),
            scratch_shapes=[pltpu.VMEM((tt, D), F32), pltpu.SemaphoreType.DMA],
        ),
        compiler_params=_cparams(("arbitrary",), 40),
        name="moe_dispatch",
    )(dest, chunk_end, h2, g.reshape(1, D))


def _expert_kernel(ce_ref, nv_ref, xs_ref, w1_ref, w3_ref, w2_ref, o_ref, w1b, w3b, w2b):
    c = pl.program_id(0)
    e = ce_ref[c]
    prev = ce_ref[jnp.maximum(c - 1, 0)]

    @pl.when(jnp.logical_or(c == 0, e != prev))
    def _():
        w1b[...] = w1_ref[0, 0].astype(BF16)
        w3b[...] = w3_ref[0, 0].astype(BF16)
        w2b[...] = w2_ref[0, 0].astype(BF16)

    @pl.when(c < nv_ref[0])
    def _():
        x = xs_ref[...].astype(BF16)
        a = _dot(x, w1b[...])
        b = _dot(x, w3b[...])
        hc = a * jax.nn.sigmoid(a) * b
        o_ref[...] = _dot(hc.astype(BF16), w2b[...])

    @pl.when(c >= nv_ref[0])
    def _():
        o_ref[...] = jnp.zeros_like(o_ref)


def _experts(chunk_e, n_valid, xs, w1, w3, w2, layer):
    P, D = xs.shape
    F = w1.shape[-1]
    ch = MOE_CHUNK
    rows = lambda c, ce, nv: (jnp.minimum(c, nv[0] - 1), 0)
    wsel = lambda c, ce, nv: (layer, ce[c], 0, 0)
    return pl.pallas_call(
        _expert_kernel,
        out_shape=jax.ShapeDtypeStruct((P, D), F32),
        grid_spec=pltpu.PrefetchScalarGridSpec(
            num_scalar_prefetch=2,
            grid=(P // ch,),
            in_specs=[
                pl.BlockSpec((ch, D), rows),
                pl.BlockSpec((1, 1, D, F), wsel),
                pl.BlockSpec((1, 1, D, F), wsel),
                pl.BlockSpec((1, 1, F, D), wsel),
            ],
            out_specs=pl.BlockSpec((ch, D), lambda c, ce, nv: (c, 0)),
            scratch_shapes=[pltpu.VMEM((D, F), BF16), pltpu.VMEM((D, F), BF16), pltpu.VMEM((F, D), BF16)],
        ),
        compiler_params=_cparams(("arbitrary",), 56),
        name="moe_experts",
    )(chunk_e, n_valid, xs, w1, w3, w2)


def _combine_kernel(dest_ref, h_ref, info_ref, rows_ref, p_ref, g_ref, gw_ref, gb_ref, pw_ref, fn_ref,
                    o_ref, buf_a, buf_b, sem, *, tt, n_tok, n_steps, final):
    i = pl.program_id(0)
    bufs = (buf_a, buf_b)

    def row(tile, sl, k, r):
        return _row_copy(rows_ref, dest_ref[k * n_tok + tile * tt + r], bufs[sl].at[k], r, sem.at[sl])

    def wait_rows(sl):
        for k in range(TOP_K_IN_GROUP):
            pltpu.make_async_copy(rows_ref.at[pl.ds(0, tt)], bufs[sl].at[k], sem.at[sl]).wait()

    @pl.when(i == 0)
    def _():
        def issue(r, c):
            for k in range(TOP_K_IN_GROUP):
                row(0, 0, k, r).start(priority=k % 2)
            return c

        lax.fori_loop(0, tt, issue, 0, unroll=8)

    def step(sl):
        wait_rows(sl)
        nxt = jnp.minimum(i + 1, n_steps - 1)
        for r in range(tt):
            for k in range(TOP_K_IN_GROUP):
                row(nxt, 1 - sl, k, r).start(priority=k % 2)
        info = info_ref[...]
        y = h_ref[...] + info[:, 2:3] * bufs[sl][0] + info[:, 3:4] * bufs[sl][1]
        hn = _rms(y, g_ref[...])
        gate = jax.nn.sigmoid(_dot(hn.astype(BF16), gw_ref[...]) + gb_ref[...])
        out = y + _dot(p_ref[0].astype(BF16), pw_ref[...]) * gate
        if final:
            out = _rms(out, fn_ref[...])
        o_ref[...] = out
        pl.when(i == n_steps - 1)(functools.partial(wait_rows, 1 - sl))

    for parity in range(2):
        pl.when(i % 2 == parity)(functools.partial(step, parity))


def _combine(dest, h2, info, rows, p3, layer, g, gw, gb, pw, fn, final):
    T, D = h2.shape
    PD = p3.shape[-1]
    tt = min(CMB_TILE, T)
    full = lambda i, d: (0, 0)
    return pl.pallas_call(
        functools.partial(_combine_kernel, tt=tt, n_tok=T, n_steps=T // tt, final=final),
        out_shape=jax.ShapeDtypeStruct((T, D), F32),
        grid_spec=pltpu.PrefetchScalarGridSpec(
            num_scalar_prefetch=1,
            grid=(T // tt,),
            in_specs=[
                pl.BlockSpec((tt, D), lambda i, d: (i, 0)),
                pl.BlockSpec((tt, LANES), lambda i, d: (i, 0)),
                pl.BlockSpec(memory_space=pl.ANY),
                pl.BlockSpec((1, tt, PD), lambda i, d: (layer, i, 0)),
                pl.BlockSpec((1, D), full),
                pl.BlockSpec((D, D), full),
                pl.BlockSpec((1, D), full),
                pl.BlockSpec((PD, D), full),
                pl.BlockSpec((1, D), full),
            ],
            out_specs=pl.BlockSpec((tt, D), lambda i, d: (i, 0)),
            scratch_shapes=[pltpu.VMEM((TOP_K_IN_GROUP, tt, D), F32), pltpu.VMEM((TOP_K_IN_GROUP, tt, D), F32),
                            pltpu.SemaphoreType.DMA((2,))],
        ),
        compiler_params=_cparams(("arbitrary",), 40),
        name="moe_combine_ple",
    )(dest, h2, info, rows, p3, g.reshape(1, D), gw.astype(BF16), gb.reshape(1, D), pw.astype(BF16),
      fn.reshape(1, D))


IDX_SLOTS = 3


def _expert2_kernel(ce_ref, nv_ref, idx_hbm, h_hbm, g_ref, w1_ref, w3_ref, w2_ref, y_hbm,
                    idx_s, x_a, x_b, o_a, o_b, w1b, w3b, w2b, idx_sem, g_sem, s_sem, *, ch, n_tok, n_chunks):
    c = pl.program_id(0)
    nv = nv_ref[0]
    x_buf = (x_a, x_b)
    o_buf = (o_a, o_b)

    def idx_copy(chunk, sl):
        return pltpu.make_async_copy(idx_hbm.at[chunk], idx_s.at[sl], idx_sem.at[sl])

    def gather(chunk_slot, sl, r):
        return pltpu.make_async_copy(h_hbm.at[pl.ds(idx_s[chunk_slot, 0, r], 1)], x_buf[sl].at[pl.ds(r, 1)],
                                     g_sem.at[sl])

    def scatter(chunk_slot, sl, r):
        return pltpu.make_async_copy(o_buf[sl].at[pl.ds(r, 1)], y_hbm.at[pl.ds(idx_s[chunk_slot, 1, r], 1)],
                                     s_sem.at[sl])

    def wait_gathers(sl):
        pltpu.make_async_copy(h_hbm.at[pl.ds(0, ch)], x_buf[sl], g_sem.at[sl]).wait()

    def wait_scatters(sl):
        pltpu.make_async_copy(o_buf[sl], y_hbm.at[pl.ds(0, ch)], s_sem.at[sl]).wait()

    @pl.when(c == 0)
    def _():
        o_a[...] = jnp.zeros_like(o_a)
        o_b[...] = jnp.zeros_like(o_b)
        pltpu.make_async_copy(o_a, y_hbm.at[pl.ds(2 * n_tok, ch)], s_sem.at[0]).start()
        first = idx_copy(0, 0)
        first.start()
        first.wait()
        lax.fori_loop(0, ch, lambda r, z: (gather(0, 0, r).start(), z)[1], 0, unroll=8)
        null = idx_copy(n_chunks, IDX_SLOTS - 1)
        null.start()
        null.wait()
        idx_copy(jnp.minimum(1, n_chunks), 1).start()

    e = ce_ref[c]
    prev = ce_ref[jnp.maximum(c - 1, 0)]

    @pl.when(jnp.logical_or(c == 0, e != prev))
    def _():
        w1b[...] = w1_ref[0, 0].astype(BF16)
        w3b[...] = w3_ref[0, 0].astype(BF16)
        w2b[...] = w2_ref[0, 0].astype(BF16)

    def chunk_step(slot):
        other = 1 - slot
        cur3 = c % IDX_SLOTS
        nxt3 = (c + 1) % IDX_SLOTS
        prv3 = (c + IDX_SLOTS - 1) % IDX_SLOTS
        wait_scatters(slot)
        wait_gathers(slot)
        idx_copy(0, nxt3).wait()
        for r in range(ch):
            gather(nxt3, other, r).start()
            scatter(prv3, other, r).start()
        x = x_buf[slot][...]
        x = (x * lax.rsqrt(jnp.mean(x * x, axis=-1, keepdims=True) + RMS_EPS) * g_ref[...]).astype(BF16)
        a = _dot(x, w1b[...])
        b = _dot(x, w3b[...])
        hc = a * jax.nn.sigmoid(a) * b
        o_buf[slot][...] = _dot(hc.astype(BF16), w2b[...])
        idx_copy(jnp.minimum(c + 2, n_chunks), prv3).start()

        @pl.when(c == nv - 1)
        def _():
            lax.fori_loop(0, ch, lambda r, z: (scatter(cur3, slot, r).start(), z)[1], 0, unroll=8)
            wait_scatters(other)
            wait_scatters(slot)
            wait_gathers(other)
            idx_copy(0, prv3).wait()

    for parity in range(2):
        pl.when(jnp.logical_and(c < nv, c % 2 == parity))(functools.partial(chunk_step, parity))


def _experts2(chunk_e, n_valid, idx, h2, g, w1, w3, w2, layer):
    T, D = h2.shape
    F = w1.shape[-1]
    ch = MOE_CHUNK
    n_chunks = idx.shape[0] - 1
    wsel = lambda c, ce, nv: (layer, ce[c], 0, 0)
    return pl.pallas_call(
        functools.partial(_expert2_kernel, ch=ch, n_tok=T, n_chunks=n_chunks),
        out_shape=jax.ShapeDtypeStruct((TOP_K_IN_GROUP * T + 2 * ch, D), F32),
        grid_spec=pltpu.PrefetchScalarGridSpec(
            num_scalar_prefetch=2,
            grid=(n_chunks,),
            in_specs=[
                pl.BlockSpec(memory_space=pl.ANY),
                pl.BlockSpec(memory_space=pl.ANY),
                pl.BlockSpec((1, D), lambda c, ce, nv: (0, 0)),
                pl.BlockSpec((1, 1, D, F), wsel),
                pl.BlockSpec((1, 1, D, F), wsel),
                pl.BlockSpec((1, 1, F, D), wsel),
            ],
            out_specs=pl.BlockSpec(memory_space=pl.ANY),
            scratch_shapes=[
                pltpu.SMEM((IDX_SLOTS, 2, ch), I32),
                pltpu.VMEM((ch, D), F32), pltpu.VMEM((ch, D), F32),
                pltpu.VMEM((ch, D), F32), pltpu.VMEM((ch, D), F32),
                pltpu.VMEM((D, F), BF16), pltpu.VMEM((D, F), BF16), pltpu.VMEM((F, D), BF16),
                pltpu.SemaphoreType.DMA((IDX_SLOTS,)),
                pltpu.SemaphoreType.DMA((2,)),
                pltpu.SemaphoreType.DMA((2,)),
            ],
        ),
        compiler_params=_cparams(("arbitrary",), 56),
        name="moe_experts",
    )(chunk_e, n_valid, idx, h2, g.reshape(1, D), w1, w3, w2)


def _combine2_kernel(h_ref, info_ref, ya_ref, yb_ref, p_ref, g_ref, gw_ref, gb_ref, pw_ref, fn_ref, o_ref, *, final):
    info = info_ref[...]
    y = h_ref[...] + info[:, 2:3] * ya_ref[...] + info[:, 3:4] * yb_ref[...]
    hn = _rms(y, g_ref[...])
    gate = jax.nn.sigmoid(_dot(hn.astype(BF16), gw_ref[...]) + gb_ref[...])
    out = y + _dot(p_ref[0].astype(BF16), pw_ref[...]) * gate
    if final:
        out = _rms(out, fn_ref[...])
    o_ref[...] = out


def _combine2(h2, info, y2, p3, layer, g, gw, gb, pw, fn, final):
    T, D = h2.shape
    PD = p3.shape[-1]
    tt = min(SEQ_TILE, T)
    full = lambda i: (0, 0)
    return pl.pallas_call(
        functools.partial(_combine2_kernel, final=final),
        out_shape=jax.ShapeDtypeStruct((T, D), F32),
        grid=(T // tt,),
        in_specs=[
            pl.BlockSpec((tt, D), lambda i: (i, 0)),
            pl.BlockSpec((tt, LANES), lambda i: (i, 0)),
            pl.BlockSpec((tt, D), lambda i: (i, 0)),
            pl.BlockSpec((tt, D), lambda i: (T // tt + i, 0)),
            pl.BlockSpec((1, tt, PD), lambda i: (layer, i, 0)),
            pl.BlockSpec((1, D), full),
            pl.BlockSpec((D, D), full),
            pl.BlockSpec((1, D), full),
            pl.BlockSpec((PD, D), full),
            pl.BlockSpec((1, D), full),
        ],
        out_specs=pl.BlockSpec((tt, D), lambda i: (i, 0)),
        compiler_params=_cparams(("parallel",), 48),
        name="moe_combine_ple",
    )(h2, info, y2, y2, p3, g.reshape(1, D), gw.astype(BF16), gb.reshape(1, D), pw.astype(BF16), fn.reshape(1, D))


def _moe_ple_layer(h2, p3, layer, norm_ffn, rg_w, rg_b, re_w, re_b, w1, w3, w2, norm_ple, gate_w, gate_b, ple_proj,
                   final_norm, final):
    T, D = h2.shape
    A = T * TOP_K_IN_GROUP
    pad = LANES - N_EXPERT_GROUPS - N_EXPERTS
    wr = jnp.concatenate([rg_w, re_w, jnp.zeros((D, pad), F32)], axis=1)
    br = jnp.concatenate([rg_b, re_b, jnp.zeros((pad,), F32)]).reshape(1, LANES)
    info, cnt = _router(h2, norm_ffn, wr, br)
    counts = cnt[0, N_EXPERT_GROUPS:N_EXPERT_GROUPS + N_EXPERTS].astype(I32)
    n_chunks_e = (counts + MOE_CHUNK - 1) // MOE_CHUNK
    chunk_end = jnp.cumsum(n_chunks_e)
    pstarts = (chunk_end - n_chunks_e) * MOE_CHUNK
    n_chunks = -(-A // MOE_CHUNK) + N_EXPERTS
    n_valid = chunk_end[-1:].astype(I32)
    cidx = jnp.minimum(jnp.arange(n_chunks, dtype=I32), n_valid[0] - 1)
    chunk_e = jnp.sum((chunk_end[None, :] <= cidx[:, None]).astype(I32), axis=1)
    chunk_e = jnp.minimum(chunk_e, N_EXPERTS - 1)
    e_idx = info[:, 0:TOP_K_IN_GROUP].astype(I32)
    rank = info[:, 4:4 + TOP_K_IN_GROUP].astype(I32)
    dest = (pstarts[e_idx] + rank).T.reshape(A)
    xs = _dispatch(dest, chunk_end.astype(I32), h2, norm_ffn, n_chunks * MOE_CHUNK)
    rows = _experts(chunk_e, n_valid, xs, w1, w3, w2, layer)
    return _combine(dest, h2, info, rows, p3, layer, norm_ple, gate_w, gate_b, ple_proj, final_norm, final)


def _rope_rows(xt, cos, sin):
    x1 = xt[0:ROPE_HALF]
    x2 = xt[ROPE_HALF:ROPE_DIMS]
    return jnp.concatenate([x1 * cos - x2 * sin, x2 * cos + x1 * sin, xt[ROPE_DIMS:]], axis=0)


def _kv_kernel(h_ref, g_ref, wn_ref, wt_ref, cos_ref, sin_ref, cv_ref, kst_ref, vs_ref, kwt_ref, vw_ref, *, ts):
    i = pl.program_id(1)
    kvw = N_KV_GROUPS * HEAD_DIM
    hn = _rms(h_ref[0], g_ref[...]).astype(BF16)
    nat = _dot(hn, wn_ref[...])
    cv_ref[0] = nat[:, 0:2 * kvw]
    tr = _dot_nt(wt_ref[...], hn)
    cos = cos_ref[...]
    sin = sin_ref[...]
    blk = lax.broadcasted_iota(I32, (MAX_SEL_BLOCKS, ts), 0)
    pos = i * ts + lax.broadcasted_iota(I32, (MAX_SEL_BLOCKS, ts), 1)
    onehot = jnp.where(pos // SEL_BLOCK == blk, 1.0, 0.0).astype(BF16)
    for g in range(N_KV_GROUPS):
        c0 = g * HEAD_DIM
        kst_ref[0, g, 0:MAX_SEL_BLOCKS, :] = onehot
        kst_ref[0, g, MAX_SEL_BLOCKS:, :] = _rope_rows(tr[c0:c0 + HEAD_DIM], cos, sin).astype(BF16)
        kwt_ref[0, g] = _rope_rows(tr[kvw + c0:kvw + c0 + HEAD_DIM], cos, sin).astype(BF16)
        vs_ref[0, g] = nat[:, 2 * kvw + c0:2 * kvw + c0 + HEAD_DIM].astype(BF16)
        vw_ref[0, g] = nat[:, 3 * kvw + c0:3 * kvw + c0 + HEAD_DIM].astype(BF16)


def _rope_tables(pos):
    inv = jnp.float32(ROPE_THETA) ** (-jnp.arange(ROPE_HALF, dtype=F32) * 2.0 / ROPE_DIMS)
    ang = pos.astype(F32)[:, None] * inv[None, :]
    return jnp.cos(ang), jnp.sin(ang)


def _shared_kv_proj(h, kv_norm, w_kv):
    B, S, D = h.shape
    G, dh = N_KV_GROUPS, HEAD_DIM
    kvw = G * dh
    ts = min(SEQ_TILE, S)
    br = lambda k: w_kv[:, k * kvw:(k + 1) * kvw]
    w_nat = jnp.concatenate([br(0), br(1), br(3), br(5)], axis=1).astype(BF16)
    w_tr = jnp.concatenate([br(2), br(4)], axis=1).T.astype(BF16)
    cos, sin = _rope_tables(jnp.arange(S))
    kd = MAX_SEL_BLOCKS + dh
    return pl.pallas_call(
        functools.partial(_kv_kernel, ts=ts),
        out_shape=(
            jax.ShapeDtypeStruct((B, S, 2 * kvw), F32),
            jax.ShapeDtypeStruct((B, G, kd, S), BF16),
            jax.ShapeDtypeStruct((B, G, S, dh), BF16),
            jax.ShapeDtypeStruct((B, G, dh, S), BF16),
            jax.ShapeDtypeStruct((B, G, S, dh), BF16),
        ),
        grid=(B, S // ts),
        in_specs=[
            pl.BlockSpec((1, ts, D), lambda b, i: (b, i, 0)),
            pl.BlockSpec((1, D), lambda b, i: (0, 0)),
            pl.BlockSpec((D, 4 * kvw), lambda b, i: (0, 0)),
            pl.BlockSpec((2 * kvw, D), lambda b, i: (0, 0)),
            pl.BlockSpec((ROPE_HALF, ts), lambda b, i: (0, i)),
            pl.BlockSpec((ROPE_HALF, ts), lambda b, i: (0, i)),
        ],
        out_specs=(
            pl.BlockSpec((1, ts, 2 * kvw), lambda b, i: (b, i, 0)),
            pl.BlockSpec((1, G, kd, ts), lambda b, i: (b, 0, 0, i)),
            pl.BlockSpec((1, G, ts, dh), lambda b, i: (b, 0, i, 0)),
            pl.BlockSpec((1, G, dh, ts), lambda b, i: (b, 0, 0, i)),
            pl.BlockSpec((1, G, ts, dh), lambda b, i: (b, 0, i, 0)),
        ),
        compiler_params=_cparams(("parallel", "parallel"), 48),
        name="shared_kv_proj",
    )(h, kv_norm.reshape(1, D), w_nat, w_tr, cos.T, sin.T)


def _compress_kernel(x_ref, pos_ref, w1_ref, b1_ref, w2_ref, b2_ref, cos_ref, sin_ref, o_ref, *, nh, keys):
    x = x_ref[0, 0, 0]
    a = _dot((x + pos_ref[0:1]).astype(BF16), w1_ref[0])
    b = _dot((x + pos_ref[1:2]).astype(BF16), w1_ref[1])
    hid = jax.nn.gelu(a + pltpu.roll(b, nh - 1, axis=0) + b1_ref[...]).astype(BF16)
    if keys:
        out = _dot_nt(w2_ref[...], hid) + b2_ref[...]
        o_ref[0, 0] = _rope_rows(out, cos_ref[...], sin_ref[...]).astype(BF16)
    else:
        o_ref[0, 0] = (_dot(hid, w2_ref[...]) + b2_ref[...]).astype(BF16)


def _compress(halves, which, pos_emb, w1, b1, w2, b2, keys):
    _, B, G, nh, hw = halves.shape
    dh = HEAD_DIM
    hidden = w1.shape[-1]
    pos2 = pos_emb.reshape(2, hw)
    w1s = w1.reshape(2, hw, hidden).astype(BF16)
    cos, sin = _rope_tables(jnp.arange(nh) * CMP_STRIDE + CMP_BLOCK - 1)
    if keys:
        w2a, b2a = w2.T.astype(BF16), b2.reshape(dh, 1)
        out_shape, out_block, out_idx = (B, G, dh, nh), (1, 1, dh, nh), lambda b, g: (b, g, 0, 0)
    else:
        w2a, b2a = w2.astype(BF16), b2.reshape(1, dh)
        out_shape, out_block, out_idx = (B, G, nh, dh), (1, 1, nh, dh), lambda b, g: (b, g, 0, 0)
    c2 = lambda b, g: (0, 0)
    return pl.pallas_call(
        functools.partial(_compress_kernel, nh=nh, keys=keys),
        out_shape=jax.ShapeDtypeStruct(out_shape, BF16),
        grid=(B, G),
        in_specs=[
            pl.BlockSpec((1, 1, 1, nh, hw), lambda b, g: (which, b, g, 0, 0)),
            pl.BlockSpec((2, hw), c2),
            pl.BlockSpec((2, hw, hidden), lambda b, g: (0, 0, 0)),
            pl.BlockSpec((1, hidden), c2),
            pl.BlockSpec(w2a.shape, c2),
            pl.BlockSpec(b2a.shape, c2),
            pl.BlockSpec((ROPE_HALF, nh), c2),
            pl.BlockSpec((ROPE_HALF, nh), c2),
        ],
        out_specs=pl.BlockSpec(out_block, out_idx),
        compiler_params=_cparams(("parallel", "parallel"), 40),
        name="compress_k" if keys else "compress_v",
    )(halves, pos2, w1s, b1.reshape(1, hidden), w2a, b2a, cos.T, sin.T)


def _qproj_kernel(h_ref, g_ref, wq_ref, wg_ref, bg_ref, c_ref, s1_ref, s2_ref, q_ref, gt_ref):
    xn = _rms(h_ref[0], g_ref[...]).astype(BF16)
    q = _dot(xn, wq_ref[...])
    c, s1, s2 = c_ref[...], s1_ref[...], s2_ref[...]
    for k in range(q.shape[-1] // LANES):
        x = q[:, k * LANES:(k + 1) * LANES]
        r = x * c + pltpu.roll(x, ROPE_HALF, axis=1) * s1 + pltpu.roll(x, LANES - ROPE_HALF, axis=1) * s2
        q_ref[0, :, k * LANES:(k + 1) * LANES] = r.astype(BF16)
    gt_ref[0] = jax.nn.sigmoid(_dot(xn, wg_ref[...]) + bg_ref[...])


def _q_proj(h, g, w_qg, b_gate):
    B, S, D = h.shape
    HD = N_HEADS * HEAD_DIM
    ng = N_HEADS * N_BRANCH
    ts = min(SEQ_TILE, S)
    wq = w_qg[:, :HD].astype(BF16)
    wg = jnp.pad(w_qg[:, HD:], ((0, 0), (0, LANES - ng))).astype(BF16)
    bg = jnp.pad(b_gate, (0, LANES - ng)).reshape(1, LANES)
    cos, sin = _rope_tables(jnp.arange(S))
    scale = HEAD_DIM ** -0.5
    ones = jnp.ones((S, HEAD_DIM - ROPE_DIMS), F32)
    zeros = jnp.zeros((S, HEAD_DIM - ROPE_DIMS), F32)
    zh = jnp.zeros((S, ROPE_HALF), F32)
    two = lambda a: jnp.concatenate([a, a], axis=1) * scale
    c = two(jnp.concatenate([cos, cos, ones], axis=1))
    s1 = two(jnp.concatenate([zh, sin, zeros], axis=1))
    s2 = two(jnp.concatenate([-sin, zh, zeros], axis=1))
    tab = pl.BlockSpec((ts, LANES), lambda b, i: (i, 0))
    return pl.pallas_call(
        _qproj_kernel,
        out_shape=(jax.ShapeDtypeStruct((B, S, HD), BF16), jax.ShapeDtypeStruct((B, S, LANES), F32)),
        grid=(B, S // ts),
        in_specs=[
            pl.BlockSpec((1, ts, D), lambda b, i: (b, i, 0)),
            pl.BlockSpec((1, D), lambda b, i: (0, 0)),
            pl.BlockSpec((D, HD), lambda b, i: (0, 0)),
            pl.BlockSpec((D, LANES), lambda b, i: (0, 0)),
            pl.BlockSpec((1, LANES), lambda b, i: (0, 0)),
            tab, tab, tab,
        ],
        out_specs=(pl.BlockSpec((1, ts, HD), lambda b, i: (b, i, 0)),
                   pl.BlockSpec((1, ts, LANES), lambda b, i: (b, i, 0))),
        compiler_params=_cparams(("parallel", "parallel"), 40),
        name="nsa_q_proj",
    )(h, g.reshape(1, D), wq, wg, bg, c, s1, s2)


def _stack_heads(qb):
    return jnp.concatenate([qb[:, r * HEAD_DIM:(r + 1) * HEAD_DIM] for r in range(HEADS_PER_GROUP)], axis=0)


def _gated_unstack(o, gates, g, branch, tq):
    lane = lax.broadcasted_iota(I32, gates.shape, 1)
    cols = []
    for r in range(HEADS_PER_GROUP):
        col = (g * HEADS_PER_GROUP + r) * N_BRANCH + branch
        gate = jnp.sum(jnp.where(lane == col, gates, 0.0), axis=-1, keepdims=True)
        cols.append(o[r * tq:(r + 1) * tq] * gate)
    return jnp.concatenate(cols, axis=-1)


def _query_pos(i, tq):
    rows = lax.broadcasted_iota(I32, (HEADS_PER_GROUP * tq, 1), 0)
    return i * tq + (rows & (tq - 1))


def _cmp_kernel(q_ref, gt_ref, k_ref, v_ref, o_ref, bias_ref, *, tq, nc):
    g = pl.program_id(1)
    i = pl.program_id(2)
    qs = _stack_heads(q_ref[0])
    s = _dot(qs, k_ref[0, 0])
    t = _query_pos(i, tq)
    n = lax.broadcasted_iota(I32, (1, nc), 1)
    s = jnp.where(n * CMP_STRIDE + (CMP_BLOCK - 1) <= t, s, -jnp.inf)
    m = jnp.max(s, axis=-1, keepdims=True)
    m = jnp.where(m == -jnp.inf, 0.0, m)
    e = jnp.exp(s - m)
    p = e / jnp.maximum(jnp.sum(e, axis=-1, keepdims=True), 1e-30)
    o = _dot(p.astype(BF16), v_ref[0, 0])
    o_ref[0] = _gated_unstack(o, gt_ref[0], g, 0, tq).astype(BF16)
    ps = p[0:tq]
    for r in range(1, HEADS_PER_GROUP):
        ps = ps + p[r * tq:(r + 1) * tq]
    nn = lax.broadcasted_iota(I32, (nc, MAX_SEL_BLOCKS), 0) * CMP_STRIDE
    jj = lax.broadcasted_iota(I32, (nc, MAX_SEL_BLOCKS), 1) * SEL_BLOCK
    c2s = jnp.where(nn < jj + SEL_BLOCK, jnp.where(nn + CMP_BLOCK > jj, 1.0, 0.0), 0.0)
    imp = jnp.dot(ps, c2s, precision=HIGHEST, preferred_element_type=F32)
    tq_pos = i * tq + lax.broadcasted_iota(I32, (tq, 1), 0)
    cur = tq_pos // SEL_BLOCK
    j = lax.broadcasted_iota(I32, (tq, MAX_SEL_BLOCKS), 1)
    valid = j <= cur
    bonus = jnp.where(j == 0, FORCE_BONUS, jnp.where(j == cur, FORCE_BONUS, jnp.where(j == cur - 1, FORCE_BONUS, 0.0)))
    work = jnp.where(valid, imp + bonus, -jnp.inf)
    sel = jnp.zeros((tq, MAX_SEL_BLOCKS), F32)
    for _ in range(N_SELECT):
        mx = jnp.max(work, axis=-1, keepdims=True)
        idx = jnp.min(jnp.where(work == mx, j, MAX_SEL_BLOCKS), axis=-1, keepdims=True)
        pick = j == idx
        sel = jnp.where(pick, 1.0, sel)
        work = jnp.where(pick, -jnp.inf, work)
    bias_ref[0, 0] = jnp.where(valid, jnp.where(sel > 0.0, 0.0, MASK_BIAS), MASK_BIAS).astype(BF16)


def _cmp_attn(q, gates, kct, vc):
    B, S, HD = q.shape
    G, dh = N_KV_GROUPS, HEAD_DIM
    gw = HD // G
    nc = kct.shape[-1]
    tq = min(Q_TILE, S)
    return pl.pallas_call(
        functools.partial(_cmp_kernel, tq=tq, nc=nc),
        out_shape=(jax.ShapeDtypeStruct((B, S, HD), BF16), jax.ShapeDtypeStruct((B, G, S, MAX_SEL_BLOCKS), BF16)),
        grid=(B, G, S // tq),
        in_specs=[
            pl.BlockSpec((1, tq, gw), lambda b, g, i: (b, i, g)),
            pl.BlockSpec((1, tq, LANES), lambda b, g, i: (b, i, 0)),
            pl.BlockSpec((1, 1, dh, nc), lambda b, g, i: (b, g, 0, 0)),
            pl.BlockSpec((1, 1, nc, dh), lambda b, g, i: (b, g, 0, 0)),
        ],
        out_specs=(pl.BlockSpec((1, tq, gw), lambda b, g, i: (b, i, g)),
                   pl.BlockSpec((1, 1, tq, MAX_SEL_BLOCKS), lambda b, g, i: (b, g, i, 0))),
        compiler_params=_cparams(("parallel", "parallel", "parallel"), 40),
        name="nsa_compressed",
    )(q, gates, kct, vc)


def _sel_kernel(q_ref, bias_ref, gt_ref, k_ref, v_ref, o_ref, qa, m_s, l_s, acc, *, tq, tk):
    g = pl.program_id(1)
    i = pl.program_id(2)
    qb = q_ref[0]
    bias = bias_ref[0, 0]
    for r in range(HEADS_PER_GROUP):
        qa[r * tq:(r + 1) * tq, 0:MAX_SEL_BLOCKS] = bias
        qa[r * tq:(r + 1) * tq, MAX_SEL_BLOCKS:] = qb[:, r * HEAD_DIM:(r + 1) * HEAD_DIM]
    m_s[...] = jnp.full_like(m_s, -jnp.inf)
    l_s[...] = jnp.zeros_like(l_s)
    acc[...] = jnp.zeros_like(acc)
    t = _query_pos(i, tq)
    lane = lax.broadcasted_iota(I32, (1, tk), 1)

    def body(jt, c):
        k0 = pl.multiple_of(jt * tk, tk)
        s = _dot(qa[...], k_ref[0, 0, :, pl.ds(k0, tk)])
        s = jnp.where(k0 + lane <= t, s, -jnp.inf)
        m_old = m_s[...]
        m_new = jnp.maximum(m_old, jnp.max(s, axis=-1, keepdims=True))
        alpha = jnp.exp(m_old - m_new)
        p = jnp.exp(s - m_new)
        l_s[...] = alpha * l_s[...] + jnp.sum(p, axis=-1, keepdims=True)
        acc[...] = alpha * acc[...] + _dot(p.astype(BF16), v_ref[0, 0, pl.ds(k0, tk), :])
        m_s[...] = m_new
        return c

    lax.fori_loop(0, (i * tq + tq + tk - 1) // tk, body, 0)
    o_ref[0] = _gated_unstack(acc[...] / l_s[...], gt_ref[0], g, 1, tq).astype(BF16)


def _sel_attn(q, bias, gates, kst, vs):
    B, S, HD = q.shape
    G, dh = N_KV_GROUPS, HEAD_DIM
    gw = HD // G
    kd = kst.shape[2]
    tq = min(Q_TILE, S)
    tk = min(KEY_TILE, S)
    rows = HEADS_PER_GROUP * tq
    return pl.pallas_call(
        functools.partial(_sel_kernel, tq=tq, tk=tk),
        out_shape=jax.ShapeDtypeStruct((B, S, HD), BF16),
        grid=(B, G, S // tq),
        in_specs=[
            pl.BlockSpec((1, tq, gw), lambda b, g, i: (b, i, g)),
            pl.BlockSpec((1, 1, tq, MAX_SEL_BLOCKS), lambda b, g, i: (b, g, i, 0)),
            pl.BlockSpec((1, tq, LANES), lambda b, g, i: (b, i, 0)),
            pl.BlockSpec((1, 1, kd, S), lambda b, g, i: (b, g, 0, 0)),
            pl.BlockSpec((1, 1, S, dh), lambda b, g, i: (b, g, 0, 0)),
        ],
        out_specs=pl.BlockSpec((1, tq, gw), lambda b, g, i: (b, i, g)),
        scratch_shapes=[pltpu.VMEM((rows, kd), BF16), pltpu.VMEM((rows, 1), F32), pltpu.VMEM((rows, 1), F32),
                        pltpu.VMEM((rows, dh), F32)],
        compiler_params=_cparams(("parallel", "parallel", "arbitrary"), 48),
        name="nsa_selected",
    )(q, bias, gates, kst, vs)


def _win_kernel(q_ref, gt_ref, k_ref, v_ref, o_ref, *, tq, wb):
    g = pl.program_id(1)
    i = pl.program_id(2)
    k0 = pl.multiple_of(jnp.maximum(i * tq - WINDOW, 0), tq)
    qs = _stack_heads(q_ref[0])
    s = _dot(qs, k_ref[0, 0, :, pl.ds(k0, wb)])
    t = _query_pos(i, tq)
    kpos = k0 + lax.broadcasted_iota(I32, (1, wb), 1)
    s = jnp.where(kpos <= t, jnp.where(kpos > t - WINDOW, s, -jnp.inf), -jnp.inf)
    m = jnp.max(s, axis=-1, keepdims=True)
    e = jnp.exp(s - m)
    p = e / jnp.sum(e, axis=-1, keepdims=True)
    o = _dot(p.astype(BF16), v_ref[0, 0, pl.ds(k0, wb), :])
    o_ref[0] = _gated_unstack(o, gt_ref[0], g, 2, tq).astype(BF16)


def _win_attn(q, gates, kwt, vw):
    B, S, HD = q.shape
    G, dh = N_KV_GROUPS, HEAD_DIM
    gw = HD // G
    tq = min(Q_TILE, S)
    wb = WINDOW + tq
    assert S >= wb and tq % LANES == 0
    return pl.pallas_call(
        functools.partial(_win_kernel, tq=tq, wb=wb),
        out_shape=jax.ShapeDtypeStruct((B, S, HD), BF16),
        grid=(B, G, S // tq),
        in_specs=[
            pl.BlockSpec((1, tq, gw), lambda b, g, i: (b, i, g)),
            pl.BlockSpec((1, tq, LANES), lambda b, g, i: (b, i, 0)),
            pl.BlockSpec((1, 1, dh, S), lambda b, g, i: (b, g, 0, 0)),
            pl.BlockSpec((1, 1, S, dh), lambda b, g, i: (b, g, 0, 0)),
        ],
        out_specs=pl.BlockSpec((1, tq, gw), lambda b, g, i: (b, i, g)),
        compiler_params=_cparams(("parallel", "parallel", "parallel"), 40),
        name="nsa_window",
    )(q, gates, kwt, vw)


LOG2E = 1.4426950408889634
Q_SCALE = HEAD_DIM ** -0.5 * LOG2E
V_ROWS = HEAD_DIM + 16
KEY_AUG = 2 * LANES
SEL_CHAINS = 1
NEG_BIG = -1e30


def _lane_rope_tables(pos):
    cos, sin = _rope_tables(pos)
    n = pos.shape[0]
    ones = jnp.ones((n, HEAD_DIM - ROPE_DIMS), F32)
    zeros = jnp.zeros((n, HEAD_DIM - ROPE_DIMS), F32)
    zh = jnp.zeros((n, ROPE_HALF), F32)
    two = lambda a: jnp.concatenate([a, a], axis=1)
    return (two(jnp.concatenate([cos, cos, ones], axis=1)), two(jnp.concatenate([zh, sin, zeros], axis=1)),
            two(jnp.concatenate([-sin, zh, zeros], axis=1)))


def _rope_lanes(x, c, s1, s2):
    return x * c + pltpu.roll(x, ROPE_HALF, axis=1) * s1 + pltpu.roll(x, LANES - ROPE_HALF, axis=1) * s2


def _kv2_kernel(h_ref, g_ref, wn_ref, wt_ref, c_ref, s1_ref, s2_ref, cv_ref, ksa_ref, vst_ref, kw_ref, vwt_ref, *, ts):
    i = pl.program_id(1)
    kvw = N_KV_GROUPS * HEAD_DIM
    hn = _rms(h_ref[0], g_ref[...]).astype(BF16)
    nat = _dot(hn, wn_ref[...])
    cv_ref[0] = nat[:, 0:2 * kvw]
    tr = _dot_nt(wt_ref[...], hn)
    c, s1, s2 = c_ref[...], s1_ref[...], s2_ref[...]
    roped = [_rope_lanes(nat[:, 2 * kvw + k * LANES:2 * kvw + (k + 1) * LANES], c, s1, s2)
             for k in range(2 * kvw // LANES)]
    lane = lax.broadcasted_iota(I32, (ts, KEY_AUG), 1)
    pos = i * ts + lax.broadcasted_iota(I32, (ts, KEY_AUG), 0)
    onehot = jnp.where(lane - HEAD_DIM == pos // SEL_BLOCK, 1.0, 0.0).astype(BF16)
    ones_row = jnp.where(lax.broadcasted_iota(I32, (V_ROWS - HEAD_DIM, ts), 0) == 0, 1.0, 0.0).astype(BF16)
    per_tile = LANES // HEAD_DIM
    for g in range(N_KV_GROUPS):
        lo = (g % per_tile) * HEAD_DIM
        ksa_ref[0, g] = onehot
        ksa_ref[0, g, :, 0:HEAD_DIM] = roped[g // per_tile][:, lo:lo + HEAD_DIM].astype(BF16)
        kw_ref[0, g] = roped[N_KV_GROUPS // per_tile + g // per_tile][:, lo:lo + HEAD_DIM].astype(BF16)
        for ref, base in ((vst_ref, 0), (vwt_ref, kvw)):
            ref[0, g, 0:HEAD_DIM, :] = tr[base + g * HEAD_DIM:base + (g + 1) * HEAD_DIM].astype(BF16)
            ref[0, g, HEAD_DIM:, :] = ones_row


def _shared_kv_proj2(h, kv_norm, w_kv):
    B, S, D = h.shape
    G, dh = N_KV_GROUPS, HEAD_DIM
    kvw = G * dh
    ts = min(SEQ_TILE, S)
    br = lambda k: w_kv[:, k * kvw:(k + 1) * kvw]
    w_nat = jnp.concatenate([br(0), br(1), br(2), br(4)], axis=1).astype(BF16)
    w_tr = jnp.concatenate([br(3), br(5)], axis=1).T.astype(BF16)
    tabs = _lane_rope_tables(jnp.arange(S))
    tab = pl.BlockSpec((ts, LANES), lambda b, i: (i, 0))
    return pl.pallas_call(
        functools.partial(_kv2_kernel, ts=ts),
        out_shape=(
            jax.ShapeDtypeStruct((B, S, 2 * kvw), F32),
            jax.ShapeDtypeStruct((B, G, S, KEY_AUG), BF16),
            jax.ShapeDtypeStruct((B, G, V_ROWS, S), BF16),
            jax.ShapeDtypeStruct((B, G, S, dh), BF16),
            jax.ShapeDtypeStruct((B, G, V_ROWS, S), BF16),
        ),
        grid=(B, S // ts),
        in_specs=[
            pl.BlockSpec((1, ts, D), lambda b, i: (b, i, 0)),
            pl.BlockSpec((1, D), lambda b, i: (0, 0)),
            pl.BlockSpec((D, 4 * kvw), lambda b, i: (0, 0)),
            pl.BlockSpec((2 * kvw, D), lambda b, i: (0, 0)),
            tab, tab, tab,
        ],
        out_specs=(
            pl.BlockSpec((1, ts, 2 * kvw), lambda b, i: (b, i, 0)),
            pl.BlockSpec((1, G, ts, KEY_AUG), lambda b, i: (b, 0, i, 0)),
            pl.BlockSpec((1, G, V_ROWS, ts), lambda b, i: (b, 0, 0, i)),
            pl.BlockSpec((1, G, ts, dh), lambda b, i: (b, 0, i, 0)),
            pl.BlockSpec((1, G, V_ROWS, ts), lambda b, i: (b, 0, 0, i)),
        ),
        compiler_params=_cparams(("parallel", "parallel"), 48),
        name="shared_kv_proj",
    )(h, kv_norm.reshape(1, D), w_nat, w_tr, *tabs)


def _compress2_kernel(x_ref, pos_ref, w1_ref, b1_ref, w2_ref, b2_ref, c_ref, s1_ref, s2_ref, o_ref, *, nh, keys):
    x = x_ref[0, 0, 0]
    a = _dot((x + pos_ref[0:1]).astype(BF16), w1_ref[0])
    b = _dot((x + pos_ref[1:2]).astype(BF16), w1_ref[1])
    hid = jax.nn.gelu(a + pltpu.roll(b, nh - 1, axis=0) + b1_ref[...]).astype(BF16)
    if keys:
        out = _rope_lanes(_dot(hid, w2_ref[...]) + b2_ref[...], c_ref[...], s1_ref[...], s2_ref[...])
        o_ref[0, 0] = out[:, 0:HEAD_DIM].astype(BF16)
    else:
        o_ref[0, 0] = (_dot_nt(w2_ref[...], hid) + b2_ref[...]).astype(BF16)


def _compress2(halves, which, pos_emb, w1, b1, w2, b2, keys):
    _, B, G, nh, hw = halves.shape
    dh = HEAD_DIM
    hidden = w1.shape[-1]
    pos2 = pos_emb.reshape(2, hw)
    w1s = w1.reshape(2, hw, hidden).astype(BF16)
    tabs = _lane_rope_tables(jnp.arange(nh) * CMP_STRIDE + CMP_BLOCK - 1)
    if keys:
        w2a = jnp.pad(w2, ((0, 0), (0, LANES - dh))).astype(BF16)
        b2a = jnp.pad(b2, (0, LANES - dh)).reshape(1, LANES)
        out_shape, out_block = (B, G, nh, dh), (1, 1, nh, dh)
    else:
        w2a, b2a = w2.T.astype(BF16), b2.reshape(dh, 1)
        out_shape, out_block = (B, G, dh, nh), (1, 1, dh, nh)
    c2 = lambda b, g: (0, 0)
    tab = pl.BlockSpec((nh, LANES), c2)
    return pl.pallas_call(
        functools.partial(_compress2_kernel, nh=nh, keys=keys),
        out_shape=jax.ShapeDtypeStruct(out_shape, BF16),
        grid=(B, G),
        in_specs=[
            pl.BlockSpec((1, 1, 1, nh, hw), lambda b, g: (which, b, g, 0, 0)),
            pl.BlockSpec((2, hw), c2),
            pl.BlockSpec((2, hw, hidden), lambda b, g: (0, 0, 0)),
            pl.BlockSpec((1, hidden), c2),
            pl.BlockSpec(w2a.shape, c2),
            pl.BlockSpec(b2a.shape, c2),
            tab, tab, tab,
        ],
        out_specs=pl.BlockSpec(out_block, lambda b, g: (b, g, 0, 0)),
        compiler_params=_cparams(("parallel", "parallel"), 40),
        name="compress_k" if keys else "compress_v",
    )(halves, pos2, w1s, b1.reshape(1, hidden), w2a, b2a, *tabs)


def _qproj2_kernel(h_ref, g_ref, wqt_ref, wg_ref, bg_ref, cos_ref, sin_ref, qt_ref, gt_ref):
    xn = _rms(h_ref[0], g_ref[...]).astype(BF16)
    tr = _dot_nt(wqt_ref[...], xn)
    cos, sin = cos_ref[...], sin_ref[...]
    for hd in range(N_HEADS):
        rows = slice(hd * HEAD_DIM, (hd + 1) * HEAD_DIM)
        qt_ref[0, rows, :] = (_rope_rows(tr[rows], cos, sin) * Q_SCALE).astype(BF16)
    gt_ref[0] = jax.nn.sigmoid(_dot(xn, wg_ref[...]) + bg_ref[...])


def _q_proj2(h, g, w_qg, b_gate):
    B, S, D = h.shape
    HD = N_HEADS * HEAD_DIM
    ng = N_HEADS * N_BRANCH
    ts = min(SEQ_TILE, S)
    wqt = w_qg[:, :HD].T.astype(BF16)
    wg = jnp.pad(w_qg[:, HD:], ((0, 0), (0, LANES - ng))).astype(BF16)
    bg = jnp.pad(b_gate, (0, LANES - ng)).reshape(1, LANES)
    cos, sin = _rope_tables(jnp.arange(S))
    tab = pl.BlockSpec((ROPE_HALF, ts), lambda b, i: (0, i))
    return pl.pallas_call(
        _qproj2_kernel,
        out_shape=(jax.ShapeDtypeStruct((B, HD, S), BF16), jax.ShapeDtypeStruct((B, S, LANES), F32)),
        grid=(B, S // ts),
        in_specs=[
            pl.BlockSpec((1, ts, D), lambda b, i: (b, i, 0)),
            pl.BlockSpec((1, D), lambda b, i: (0, 0)),
            pl.BlockSpec((HD, D), lambda b, i: (0, 0)),
            pl.BlockSpec((D, LANES), lambda b, i: (0, 0)),
            pl.BlockSpec((1, LANES), lambda b, i: (0, 0)),
            tab, tab,
        ],
        out_specs=(pl.BlockSpec((1, HD, ts), lambda b, i: (b, 0, i)),
                   pl.BlockSpec((1, ts, LANES), lambda b, i: (b, i, 0))),
        compiler_params=_cparams(("parallel", "parallel"), 40),
        name="nsa_q_proj",
    )(h, g.reshape(1, D), wqt, wg, bg, cos.T, sin.T)


def _heads_on_lanes(qt):
    return jnp.concatenate([qt[r * HEAD_DIM:(r + 1) * HEAD_DIM] for r in range(HEADS_PER_GROUP)], axis=1)


def _lane_query_pos(i, tq):
    lanes = lax.broadcasted_iota(I32, (1, HEADS_PER_GROUP * tq), 1)
    return i * tq + (lanes & (tq - 1))


def _finish_heads(acc_t, gates, g, branch, tq, denom_row):
    lane = lax.broadcasted_iota(I32, gates.shape, 1)
    rows = acc_t.shape[0]
    cols = []
    for r in range(HEADS_PER_GROUP):
        blk = acc_t[:, r * tq:(r + 1) * tq]
        nat = jnp.concatenate([blk, jnp.zeros((tq - rows, tq), F32)], axis=0).T
        col = (g * HEADS_PER_GROUP + r) * N_BRANCH + branch
        scale = jnp.sum(jnp.where(lane == col, gates, 0.0), axis=-1, keepdims=True)
        if denom_row is not None:
            scale = scale / nat[:, denom_row:denom_row + 1]
        cols.append(nat[:, 0:HEAD_DIM] * scale)
    return jnp.concatenate(cols, axis=-1)


def _cmp2_kernel(qt_ref, gt_ref, k_ref, vt_ref, c2s_ref, o_ref, bias_ref, *, tq, nc):
    g = pl.program_id(1)
    i = pl.program_id(2)
    s = _dot(k_ref[0, 0], _heads_on_lanes(qt_ref[0]))
    t = _lane_query_pos(i, tq)
    n = lax.broadcasted_iota(I32, (nc, 1), 0)
    s = jnp.where(n * CMP_STRIDE + (CMP_BLOCK - 1) <= t, s, -jnp.inf)
    m = jnp.max(s, axis=0, keepdims=True)
    m = jnp.where(m == -jnp.inf, 0.0, m)
    e = jnp.exp2(s - m)
    p = e * (1.0 / jnp.maximum(jnp.sum(e, axis=0, keepdims=True), 1e-30))
    ot = _dot(vt_ref[0, 0], p.astype(BF16))
    o_ref[0] = _finish_heads(ot, gt_ref[0], g, 0, tq, None).astype(BF16)
    ps = p[:, 0:tq]
    for r in range(1, HEADS_PER_GROUP):
        ps = ps + p[:, r * tq:(r + 1) * tq]
    hi = ps.astype(BF16)
    rem = ps - hi.astype(F32)
    mid = rem.astype(BF16)
    lo = (rem - mid.astype(F32)).astype(BF16)
    c2s = c2s_ref[...]
    imp = _dot(c2s, hi) + _dot(c2s, mid) + _dot(c2s, lo)
    cur = (i * tq + lax.broadcasted_iota(I32, (1, tq), 1)) // SEL_BLOCK
    j = lax.broadcasted_iota(I32, (MAX_SEL_BLOCKS, tq), 0)
    valid = j <= cur
    bonus = jnp.where(j == 0, FORCE_BONUS, jnp.where(j == cur, FORCE_BONUS, jnp.where(j == cur - 1, FORCE_BONUS, 0.0)))
    work = jnp.where(valid, imp + bonus, -jnp.inf)
    sel = jnp.zeros((MAX_SEL_BLOCKS, tq), F32)
    for _ in range(N_SELECT):
        mx = jnp.max(work, axis=0, keepdims=True)
        idx = jnp.min(jnp.where(work == mx, j, MAX_SEL_BLOCKS), axis=0, keepdims=True)
        pick = j == idx
        sel = jnp.where(pick, 1.0, sel)
        work = jnp.where(pick, -jnp.inf, work)
    bias_ref[0, 0] = jnp.where(valid, jnp.where(sel > 0.0, 0.0, MASK_BIAS), MASK_BIAS).astype(BF16)


def _cmp_attn2(qt, gates, kc, vct):
    B, HD, S = qt.shape
    G, dh = N_KV_GROUPS, HEAD_DIM
    gw = HD // G
    nc = kc.shape[2]
    tq = min(Q_TILE, S)
    n0 = jnp.arange(nc)[None, :] * CMP_STRIDE
    j0 = jnp.arange(MAX_SEL_BLOCKS)[:, None] * SEL_BLOCK
    c2s = ((n0 < j0 + SEL_BLOCK) & (n0 + CMP_BLOCK > j0)).astype(BF16)
    return pl.pallas_call(
        functools.partial(_cmp2_kernel, tq=tq, nc=nc),
        out_shape=(jax.ShapeDtypeStruct((B, S, HD), BF16), jax.ShapeDtypeStruct((B, G, MAX_SEL_BLOCKS, S), BF16)),
        grid=(B, G, S // tq),
        in_specs=[
            pl.BlockSpec((1, gw, tq), lambda b, g, i: (b, g, i)),
            pl.BlockSpec((1, tq, LANES), lambda b, g, i: (b, i, 0)),
            pl.BlockSpec((1, 1, nc, dh), lambda b, g, i: (b, g, 0, 0)),
            pl.BlockSpec((1, 1, dh, nc), lambda b, g, i: (b, g, 0, 0)),
            pl.BlockSpec((MAX_SEL_BLOCKS, nc), lambda b, g, i: (0, 0)),
        ],
        out_specs=(pl.BlockSpec((1, tq, gw), lambda b, g, i: (b, i, g)),
                   pl.BlockSpec((1, 1, MAX_SEL_BLOCKS, tq), lambda b, g, i: (b, g, 0, i))),
        compiler_params=_cparams(("parallel", "parallel", "parallel"), 40),
        name="nsa_compressed",
    )(qt, gates, kc, vct, c2s)


def _sel2_kernel(qt_ref, bias_ref, gt_ref, k_ref, vt_ref, o_ref, qa, m_s, acc, s_a, s_b, *, tq, tk):
    g = pl.program_id(1)
    i = pl.program_id(2)
    qt = qt_ref[0]
    bias = bias_ref[0, 0]
    for r in range(HEADS_PER_GROUP):
        cols = slice(r * tq, (r + 1) * tq)
        qa[0:HEAD_DIM, cols] = qt[r * HEAD_DIM:(r + 1) * HEAD_DIM]
        qa[HEAD_DIM:HEAD_DIM + MAX_SEL_BLOCKS, cols] = bias
        qa[HEAD_DIM + MAX_SEL_BLOCKS:, cols] = jnp.zeros((KEY_AUG - HEAD_DIM - MAX_SEL_BLOCKS, tq), BF16)
    for c in range(SEL_CHAINS):
        m_s[c] = jnp.full(m_s.shape[1:], NEG_BIG, F32)
        acc[c] = jnp.zeros(acc.shape[1:], F32)
    t = _lane_query_pos(i, tq)
    ck = tk // SEL_CHAINS

    def qk(jt, buf):
        buf[...] = _dot(k_ref[0, 0, pl.ds(pl.multiple_of(jt * tk, tk), tk), :], qa[...])

    def absorb(jt, buf, diagonal):
        for c in range(SEL_CHAINS):
            kc = pl.multiple_of(jt * tk + c * ck, ck)
            s = buf[c * ck:(c + 1) * ck, :]
            if diagonal:
                s = jnp.where(kc + lax.broadcasted_iota(I32, (ck, 1), 0) <= t, s, NEG_BIG)
            m_old = m_s[c]
            m_new = jnp.maximum(m_old, jnp.max(s, axis=0, keepdims=True))
            p = jnp.exp2(s - m_new).astype(BF16)
            acc[c] = jnp.exp2(m_old - m_new) * acc[c] + _dot(vt_ref[0, 0, :, pl.ds(kc, ck)], p)
            m_s[c] = m_new

    def pair(u, c):
        qk(2 * u + 1, s_b)
        absorb(2 * u, s_a, False)
        qk(2 * u + 2, s_a)
        absorb(2 * u + 1, s_b, False)
        return c

    last = (i * tq + tq + tk - 1) // tk - 1
    qk(0, s_a)
    lax.fori_loop(0, last // 2, pair, 0)

    @pl.when(last % 2 == 1)
    def _():
        qk(last, s_b)
        absorb(last - 1, s_a, False)
        absorb(last, s_b, True)

    @pl.when(last % 2 == 0)
    def _():
        absorb(last, s_a, True)

    m_all = m_s[0]
    for c in range(1, SEL_CHAINS):
        m_all = jnp.maximum(m_all, m_s[c])
    acc_t = jnp.exp2(m_s[0] - m_all) * acc[0]
    for c in range(1, SEL_CHAINS):
        acc_t = acc_t + jnp.exp2(m_s[c] - m_all) * acc[c]
    o_ref[0] = _finish_heads(acc_t, gt_ref[0], g, 1, tq, HEAD_DIM).astype(BF16)


def _sel_attn2(qt, bias, gates, ksa, vst):
    B, HD, S = qt.shape
    G = N_KV_GROUPS
    gw = HD // G
    tq = min(Q_TILE, S)
    tk = min(KEY_TILE, S)
    width = HEADS_PER_GROUP * tq
    return pl.pallas_call(
        functools.partial(_sel2_kernel, tq=tq, tk=tk),
        out_shape=jax.ShapeDtypeStruct((B, S, HD), BF16),
        grid=(B, G, S // tq),
        in_specs=[
            pl.BlockSpec((1, gw, tq), lambda b, g, i: (b, g, i)),
            pl.BlockSpec((1, 1, MAX_SEL_BLOCKS, tq), lambda b, g, i: (b, g, 0, i)),
            pl.BlockSpec((1, tq, LANES), lambda b, g, i: (b, i, 0)),
            pl.BlockSpec((1, 1, S, KEY_AUG), lambda b, g, i: (b, g, 0, 0)),
            pl.BlockSpec((1, 1, V_ROWS, S), lambda b, g, i: (b, g, 0, 0)),
        ],
        out_specs=pl.BlockSpec((1, tq, gw), lambda b, g, i: (b, i, g)),
        scratch_shapes=[pltpu.VMEM((KEY_AUG, width), BF16), pltpu.VMEM((SEL_CHAINS, 1, width), F32),
                        pltpu.VMEM((SEL_CHAINS, V_ROWS, width), F32),
                        pltpu.VMEM((tk, width), F32), pltpu.VMEM((tk, width), F32)],
        compiler_params=_cparams(("parallel", "parallel", "arbitrary"), 48),
        name="nsa_selected",
    )(qt, bias, gates, ksa, vst)


def _win2_kernel(qt_ref, gt_ref, k_ref, vt_ref, o_ref, *, tq, wb):
    g = pl.program_id(1)
    i = pl.program_id(2)
    k0 = pl.multiple_of(jnp.maximum(i * tq - WINDOW, 0), tq)
    s = _dot(k_ref[0, 0, pl.ds(k0, wb), :], _heads_on_lanes(qt_ref[0]))
    t = _lane_query_pos(i, tq)
    kpos = k0 + lax.broadcasted_iota(I32, (wb, 1), 0)
    s = jnp.where(kpos <= t, jnp.where(kpos > t - WINDOW, s, -jnp.inf), -jnp.inf)
    m = jnp.max(s, axis=0, keepdims=True)
    p = jnp.exp2(s - m).astype(BF16)
    acc = _dot(vt_ref[0, 0, :, pl.ds(k0, wb)], p)
    o_ref[0] = _finish_heads(acc, gt_ref[0], g, 2, tq, HEAD_DIM).astype(BF16)


def _win_attn2(qt, gates, kw, vwt):
    B, HD, S = qt.shape
    G, dh = N_KV_GROUPS, HEAD_DIM
    gw = HD // G
    tq = min(Q_TILE, S)
    wb = WINDOW + tq
    assert S >= wb and tq % LANES == 0
    return pl.pallas_call(
        functools.partial(_win2_kernel, tq=tq, wb=wb),
        out_shape=jax.ShapeDtypeStruct((B, S, HD), BF16),
        grid=(B, G, S // tq),
        in_specs=[
            pl.BlockSpec((1, gw, tq), lambda b, g, i: (b, g, i)),
            pl.BlockSpec((1, tq, LANES), lambda b, g, i: (b, i, 0)),
            pl.BlockSpec((1, 1, S, dh), lambda b, g, i: (b, g, 0, 0)),
            pl.BlockSpec((1, 1, V_ROWS, S), lambda b, g, i: (b, g, 0, 0)),
        ],
        out_specs=pl.BlockSpec((1, tq, gw), lambda b, g, i: (b, i, g)),
        compiler_params=_cparams(("parallel", "parallel", "parallel"), 40),
        name="nsa_window",
    )(qt, gates, kw, vwt)


def _oproj_kernel(h_ref, a_ref, b_ref, c_ref, w_ref, o_ref):
    o = a_ref[...].astype(F32) + b_ref[...].astype(F32) + c_ref[...].astype(F32)
    o_ref[...] = h_ref[...] + _dot(o.astype(BF16), w_ref[...])


def _out_proj(h2, oc, os_, ow, w_o):
    T, D = h2.shape
    HD = oc.shape[-1]
    tt = min(SEQ_TILE, T)
    blk = lambda w: pl.BlockSpec((tt, w), lambda i: (i, 0))
    return pl.pallas_call(
        _oproj_kernel,
        out_shape=jax.ShapeDtypeStruct((T, D), F32),
        grid=(T // tt,),
        in_specs=[blk(D), blk(HD), blk(HD), blk(HD), pl.BlockSpec((HD, D), lambda i: (0, 0))],
        out_specs=blk(D),
        compiler_params=_cparams(("parallel",), 40),
        name="nsa_out_proj",
    )(h2, oc, os_, ow, w_o.astype(BF16))


def _nsa_layer(h, g, w_qg, b_gate, w_o, shared):
    B, S, D = h.shape
    kc, vct, ksa, vst, kw, vwt = shared
    qt, gates = _q_proj2(h, g, w_qg, b_gate)
    oc, bias = _cmp_attn2(qt, gates, kc, vct)
    os_ = _sel_attn2(qt, bias, gates, ksa, vst)
    ow = _win_attn2(qt, gates, kw, vwt)
    flat = lambda a: a.reshape(B * S, a.shape[-1])
    return _out_proj(flat(h), flat(oc), flat(os_), flat(ow), w_o).reshape(B, S, D)


def _shared_kv(h, kv_norm, w_kv, ck, cv):
    B, S, _ = h.shape
    G, dh = N_KV_GROUPS, HEAD_DIM
    assert S % SEL_BLOCK == 0 and S // SEL_BLOCK <= MAX_SEL_BLOCKS
    cvals, ksa, vst, kw, vwt = _shared_kv_proj2(h, kv_norm, w_kv)
    halves = cvals.reshape(B, S, 2, G, dh).transpose(2, 0, 3, 1, 4).reshape(2, B, G, S // CMP_STRIDE, CMP_STRIDE * dh)
    kc = _compress2(halves, 0, *ck, keys=True)
    vct = _compress2(halves, 1, *cv, keys=False)
    return kc, vct, ksa, vst, kw, vwt


def kernel(x, p, norm_mix, norm_ffn, norm_ple, pool_w, pool_b, pool_scale, kv_norm, w_kv, cmp_k_pos, cmp_k_w1, cmp_k_b1, cmp_k_w2, cmp_k_b2, cmp_v_pos, cmp_v_w1, cmp_v_b1, cmp_v_w2, cmp_v_b2, w_qg, b_gate, w_o, router_g_w, router_g_b, router_e_w, router_e_b, moe_w1, moe_w3, moe_w2, ple_proj, ple_gate_w, ple_gate_b, final_norm):
    B, S, D = x.shape
    depth = p.shape[0]
    n_a = pool_w.shape[0]
    T = B * S
    h = x
    shared = None
    for i in range(depth):
        if i == n_a:
            shared = _shared_kv(h, kv_norm, w_kv,
                                (cmp_k_pos, cmp_k_w1, cmp_k_b1, cmp_k_w2, cmp_k_b2),
                                (cmp_v_pos, cmp_v_w1, cmp_v_b1, cmp_v_w2, cmp_v_b2))
        if i < n_a:
            h = _pool_layer(h, norm_mix[i], pool_w[i], pool_b[i], pool_scale[i])
        else:
            j = i - n_a
            h = _nsa_layer(h, norm_mix[i], w_qg[j], b_gate[j], w_o[j], shared)
        h = _moe_ple_layer(h.reshape(T, D), p.reshape(depth, T, p.shape[-1]), i, norm_ffn[i], router_g_w[i],
                           router_g_b[i], router_e_w[i], router_e_b[i], moe_w1, moe_w3, moe_w2, norm_ple[i],
                           ple_gate_w[i], ple_gate_b[i], ple_proj[i], final_norm, i == depth - 1).reshape(B, S, D)
    return h
```

```python
import functools

import jax
import jax.numpy as jnp
from jax import lax
from jax.experimental import pallas as pl
from jax.experimental.pallas import tpu as pltpu

F32 = jnp.float32
BF16 = jnp.bfloat16
I32 = jnp.int32

POOL_WINDOWS = (2, 4, 8, 16)
N_HEADS = 16
HEAD_DIM = 64
N_KV_GROUPS = 4
HEADS_PER_GROUP = N_HEADS // N_KV_GROUPS
N_BRANCH = 3
ROPE_DIMS = HEAD_DIM // 4
ROPE_HALF = ROPE_DIMS // 2
ROPE_THETA = 500000.0
CMP_BLOCK = 32
CMP_STRIDE = 16
SEL_BLOCK = 64
N_SELECT = 16
WINDOW = 512
FORCE_BONUS = 1e4
N_EXPERT_GROUPS = 4
EXPERTS_PER_GROUP = 8
N_EXPERTS = N_EXPERT_GROUPS * EXPERTS_PER_GROUP
TOP_K_IN_GROUP = 2
RMS_EPS = 1e-6

LANES = 128
MAX_SEL_BLOCKS = LANES
MASK_BIAS = -30000.0

SEQ_TILE = 512
TOK_TILE = 512
CMB_TILE = 256
MOE_CHUNK = 512
Q_TILE = 256
KEY_TILE = 512
HALO = 16

HIGHEST = lax.Precision.HIGHEST


def _cparams(sem, vmem_mb):
    return pltpu.CompilerParams(dimension_semantics=sem, vmem_limit_bytes=vmem_mb * 1024 * 1024)


def _rms(x, g):
    return x * lax.rsqrt(jnp.mean(x * x, axis=-1, keepdims=True) + RMS_EPS) * g


def _dot(a, b):
    return jnp.dot(a, b, preferred_element_type=F32)


def _dot_nt(a, b):
    return lax.dot_general(a, b, (((1,), (1,)), ((), ())), preferred_element_type=F32)


def _pool_kernel(h_ref, halo_ref, g_ref, w_ref, b_ref, sc_ref, o_ref, *, ts, cg):
    i = pl.program_id(1)
    x = h_ref[0]
    g = g_ref[...]
    xn = _rms(x, g)
    hn = _rms(halo_ref[0], g)
    hn = jnp.where(i > 0, hn, 0.0)
    ext = jnp.concatenate([hn, xn], axis=0)
    t = i * ts + lax.broadcasted_iota(I32, (ts, 1), 0)
    outs = []
    for gi, w in enumerate(POOL_WINDOWS):
        s = ext[:, gi * cg:(gi + 1) * cg]
        k = 1
        while k < w:
            s = s + pltpu.roll(s, k, axis=0)
            k *= 2
        cnt = jnp.minimum(t + 1, w).astype(F32)
        pooled = s[HALO:] / cnt - xn[:, gi * cg:(gi + 1) * cg]
        outs.append(_dot(pooled.astype(BF16), w_ref[gi]))
    y = jnp.concatenate(outs, axis=-1)
    o_ref[0] = x + (y + b_ref[...]) * sc_ref[...]


def _pool_layer(h, g, w, b, sc):
    B, S, D = h.shape
    ts = min(SEQ_TILE, S)
    cg = D // len(POOL_WINDOWS)
    row = lambda v: v.reshape(1, D)
    return pl.pallas_call(
        functools.partial(_pool_kernel, ts=ts, cg=cg),
        out_shape=jax.ShapeDtypeStruct((B, S, D), F32),
        grid=(B, S // ts),
        in_specs=[
            pl.BlockSpec((1, ts, D), lambda b_, i: (b_, i, 0)),
            pl.BlockSpec((1, HALO, D), lambda b_, i: (b_, jnp.maximum(i * (ts // HALO) - 1, 0), 0)),
            pl.BlockSpec((1, D), lambda b_, i: (0, 0)),
            pl.BlockSpec((len(POOL_WINDOWS), cg, cg), lambda b_, i: (0, 0, 0)),
            pl.BlockSpec((1, D), lambda b_, i: (0, 0)),
            pl.BlockSpec((1, D), lambda b_, i: (0, 0)),
        ],
        out_specs=pl.BlockSpec((1, ts, D), lambda b_, i: (b_, i, 0)),
        compiler_params=_cparams(("parallel", "parallel"), 40),
        name="pool_mixer",
    )(h, h, row(g), w.astype(BF16), row(b), row(sc))


def _router_kernel(h_ref, g_ref, wh_ref, wl_ref, b_ref, info_ref, cnt_ref, *, tt):
    i = pl.program_id(0)

    @pl.when(i == 0)
    def _():
        cnt_ref[...] = jnp.zeros_like(cnt_ref)

    xn = _rms(h_ref[...], g_ref[...])
    xh = xn.astype(BF16)
    xl = (xn - xh.astype(F32)).astype(BF16)
    logits = _dot(xh, wh_ref[...]) + (_dot(xh, wl_ref[...]) + _dot(xl, wh_ref[...])) + b_ref[...]
    lane = lax.broadcasted_iota(I32, (tt, LANES), 1)
    neg = -jnp.inf
    gl = jnp.where(lane < N_EXPERT_GROUPS, logits, neg)
    gmax = jnp.max(gl, axis=-1, keepdims=True)
    grp = jnp.min(jnp.where(gl == gmax, lane, LANES), axis=-1, keepdims=True)
    gprob = 1.0 / jnp.sum(jnp.exp(gl - gmax), axis=-1, keepdims=True)
    lo = N_EXPERT_GROUPS + grp * EXPERTS_PER_GROUP
    el = jnp.where(lane >= lo, jnp.where(lane < lo + EXPERTS_PER_GROUP, logits, neg), neg)
    v1 = jnp.max(el, axis=-1, keepdims=True)
    i1 = jnp.min(jnp.where(el == v1, lane, LANES), axis=-1, keepdims=True)
    el2 = jnp.where(lane == i1, neg, el)
    v2 = jnp.max(el2, axis=-1, keepdims=True)
    i2 = jnp.min(jnp.where(el2 == v2, lane, LANES), axis=-1, keepdims=True)
    e2 = jnp.exp(v2 - v1)
    w1 = gprob / (1.0 + e2)
    w2 = gprob * e2 / (1.0 + e2)
    oh1 = lane == i1
    oh2 = lane == i2
    oh = jnp.where(oh1, 1.0, jnp.where(oh2, 1.0, 0.0))
    r_ = lax.broadcasted_iota(I32, (tt, tt), 0)
    c_ = lax.broadcasted_iota(I32, (tt, tt), 1)
    tri = jnp.where(r_ > c_, 1.0, 0.0).astype(BF16)
    tot = _dot(tri, oh.astype(BF16)) + cnt_ref[...]
    r1 = jnp.sum(jnp.where(oh1, tot, 0.0), axis=-1, keepdims=True)
    r2 = jnp.sum(jnp.where(oh2, tot, 0.0), axis=-1, keepdims=True)
    cnt_ref[...] = cnt_ref[...] + jnp.sum(oh, axis=0, keepdims=True)
    vals = (i1.astype(F32) - N_EXPERT_GROUPS, i2.astype(F32) - N_EXPERT_GROUPS, w1, w2, r1, r2)
    info = jnp.zeros((tt, LANES), F32)
    for k, v in enumerate(vals):
        info = jnp.where(lane == k, v, info)
    info_ref[...] = info


def _router(h2, g, wr, br):
    T, D = h2.shape
    tt = min(TOK_TILE, T)
    return pl.pallas_call(
        functools.partial(_router_kernel, tt=tt),
        out_shape=(jax.ShapeDtypeStruct((T, LANES), F32), jax.ShapeDtypeStruct((1, LANES), F32)),
        grid=(T // tt,),
        in_specs=[
            pl.BlockSpec((tt, D), lambda i: (i, 0)),
            pl.BlockSpec((1, D), lambda i: (0, 0)),
            pl.BlockSpec((D, LANES), lambda i: (0, 0)),
            pl.BlockSpec((D, LANES), lambda i: (0, 0)),
            pl.BlockSpec((1, LANES), lambda i: (0, 0)),
        ],
        out_specs=(pl.BlockSpec((tt, LANES), lambda i: (i, 0)), pl.BlockSpec((1, LANES), lambda i: (0, 0))),
        compiler_params=_cparams(("arbitrary",), 40),
        name="moe_router",
    )(h2, g.reshape(1, D), wr.astype(BF16), (wr - wr.astype(BF16).astype(F32)).astype(BF16), br)


def _row_copy(src, s, dst, d, sem):
    return pltpu.make_async_copy(src.at[pl.ds(s, 1)], dst.at[pl.ds(d, 1)], sem)


def _dispatch_kernel(dest_ref, cend_ref, h_ref, g_ref, xs_ref, xn_s, sem, *, tt, n_tok, n_chunks):
    base = pl.program_id(0) * tt

    @pl.when(pl.program_id(0) == 0)
    def _():
        xn_s[...] = jnp.zeros_like(xn_s)

        def tail(e):
            nonempty = cend_ref[e] > (cend_ref[e - 1] if e > 0 else 0)
            row = pl.multiple_of((cend_ref[e] - 1) * tt, tt)
            return nonempty, pltpu.make_async_copy(xn_s, xs_ref.at[pl.ds(row, tt)], sem)

        def unused(c):
            return pltpu.make_async_copy(xn_s, xs_ref.at[pl.ds(pl.multiple_of(c * tt, tt), tt)], sem)

        n_used = cend_ref[N_EXPERTS - 1]
        for e in range(N_EXPERTS):
            nonempty, cp = tail(e)
            pl.when(nonempty)(cp.start)
        lax.fori_loop(n_used, n_chunks, lambda c, z: (unused(c).start(), z)[1], 0)
        for e in range(N_EXPERTS):
            nonempty, cp = tail(e)
            pl.when(nonempty)(cp.wait)
        lax.fori_loop(n_used, n_chunks, lambda c, z: (unused(c).wait(), z)[1], 0)

    xn_s[...] = _rms(h_ref[...], g_ref[...])

    def issue(r, c):
        for k in range(TOP_K_IN_GROUP):
            _row_copy(xn_s, r, xs_ref, dest_ref[k * n_tok + base + r], sem).start(priority=k % 2)
        return c

    lax.fori_loop(0, tt, issue, 0, unroll=8)
    for k in range(TOP_K_IN_GROUP):
        pltpu.make_async_copy(xn_s, xs_ref.at[pl.ds(0, tt)], sem).wait()


def _dispatch(dest, chunk_end, h2, g, n_rows):
    T, D = h2.shape
    tt = MOE_CHUNK
    assert T % tt == 0
    return pl.pallas_call(
        functools.partial(_dispatch_kernel, tt=tt, n_tok=T, n_chunks=n_rows // tt),
        out_shape=jax.ShapeDtypeStruct((n_rows, D), F32),
        grid_spec=pltpu.PrefetchScalarGridSpec(
            num_scalar_prefetch=2,
            grid=(T // tt,),
            in_specs=[
                pl.BlockSpec((tt, D), lambda i, d, ce: (i, 0)),
                pl.BlockSpec((1, D), lambda i, d, ce: (0, 0)),
            ],
            out_specs=pl.BlockSpec(memory_space=pl.ANY),
            scratch_shapes=[pltpu.VMEM((tt, D), F32), pltpu.SemaphoreType.DMA],
        ),
        compiler_params=_cparams(("arbitrary",), 40),
        name="moe_dispatch",
    )(dest, chunk_end, h2, g.reshape(1, D))


def _expert_kernel(ce_ref, nv_ref, xs_ref, w1_ref, w3_ref, w2_ref, o_ref, w1b, w3b, w2b):
    c = pl.program_id(0)
    e = ce_ref[c]
    prev = ce_ref[jnp.maximum(c - 1, 0)]

    @pl.when(jnp.logical_or(c == 0, e != prev))
    def _():
        w1b[...] = w1_ref[0, 0].astype(BF16)
        w3b[...] = w3_ref[0, 0].astype(BF16)
        w2b[...] = w2_ref[0, 0].astype(BF16)

    @pl.when(c < nv_ref[0])
    def _():
        x = xs_ref[...].astype(BF16)
        a = _dot(x, w1b[...])
        b = _dot(x, w3b[...])
        hc = a * jax.nn.sigmoid(a) * b
        o_ref[...] = _dot(hc.astype(BF16), w2b[...])

    @pl.when(c >= nv_ref[0])
    def _():
        o_ref[...] = jnp.zeros_like(o_ref)


def _experts(chunk_e, n_valid, xs, w1, w3, w2, layer):
    P, D = xs.shape
    F = w1.shape[-1]
    ch = MOE_CHUNK
    rows = lambda c, ce, nv: (jnp.minimum(c, nv[0] - 1), 0)
    wsel = lambda c, ce, nv: (layer, ce[c], 0, 0)
    return pl.pallas_call(
        _expert_kernel,
        out_shape=jax.ShapeDtypeStruct((P, D), F32),
        grid_spec=pltpu.PrefetchScalarGridSpec(
            num_scalar_prefetch=2,
            grid=(P // ch,),
            in_specs=[
                pl.BlockSpec((ch, D), rows),
                pl.BlockSpec((1, 1, D, F), wsel),
                pl.BlockSpec((1, 1, D, F), wsel),
                pl.BlockSpec((1, 1, F, D), wsel),
            ],
            out_specs=pl.BlockSpec((ch, D), lambda c, ce, nv: (c, 0)),
            scratch_shapes=[pltpu.VMEM((D, F), BF16), pltpu.VMEM((D, F), BF16), pltpu.VMEM((F, D), BF16)],
        ),
        compiler_params=_cparams(("arbitrary",), 56),
        name="moe_experts",
    )(chunk_e, n_valid, xs, w1, w3, w2)


def _combine_kernel(dest_ref, h_ref, info_ref, rows_ref, p_ref, g_ref, gw_ref, gb_ref, pw_ref, fn_ref,
                    o_ref, buf_a, buf_b, sem, *, tt, n_tok, n_steps, final):
    i = pl.program_id(0)
    bufs = (buf_a, buf_b)

    def row(tile, sl, k, r):
        return _row_copy(rows_ref, dest_ref[k * n_tok + tile * tt + r], bufs[sl].at[k], r, sem.at[sl])

    def wait_rows(sl):
        for k in range(TOP_K_IN_GROUP):
            pltpu.make_async_copy(rows_ref.at[pl.ds(0, tt)], bufs[sl].at[k], sem.at[sl]).wait()

    @pl.when(i == 0)
    def _():
        def issue(r, c):
            for k in range(TOP_K_IN_GROUP):
                row(0, 0, k, r).start(priority=k % 2)
            return c

        lax.fori_loop(0, tt, issue, 0, unroll=8)

    def step(sl):
        wait_rows(sl)
        nxt = jnp.minimum(i + 1, n_steps - 1)
        for r in range(tt):
            for k in range(TOP_K_IN_GROUP):
                row(nxt, 1 - sl, k, r).start(priority=k % 2)
        info = info_ref[...]
        y = h_ref[...] + info[:, 2:3] * bufs[sl][0] + info[:, 3:4] * bufs[sl][1]
        hn = _rms(y, g_ref[...])
        gate = jax.nn.sigmoid(_dot(hn.astype(BF16), gw_ref[...]) + gb_ref[...])
        out = y + _dot(p_ref[0].astype(BF16), pw_ref[...]) * gate
        if final:
            out = _rms(out, fn_ref[...])
        o_ref[...] = out
        pl.when(i == n_steps - 1)(functools.partial(wait_rows, 1 - sl))

    for parity in range(2):
        pl.when(i % 2 == parity)(functools.partial(step, parity))


def _combine(dest, h2, info, rows, p3, layer, g, gw, gb, pw, fn, final):
    T, D = h2.shape
    PD = p3.shape[-1]
    tt = min(CMB_TILE, T)
    full = lambda i, d: (0, 0)
    return pl.pallas_call(
        functools.partial(_combine_kernel, tt=tt, n_tok=T, n_steps=T // tt, final=final),
        out_shape=jax.ShapeDtypeStruct((T, D), F32),
        grid_spec=pltpu.PrefetchScalarGridSpec(
            num_scalar_prefetch=1,
            grid=(T // tt,),
            in_specs=[
                pl.BlockSpec((tt, D), lambda i, d: (i, 0)),
                pl.BlockSpec((tt, LANES), lambda i, d: (i, 0)),
                pl.BlockSpec(memory_space=pl.ANY),
                pl.BlockSpec((1, tt, PD), lambda i, d: (layer, i, 0)),
                pl.BlockSpec((1, D), full),
                pl.BlockSpec((D, D), full),
                pl.BlockSpec((1, D), full),
                pl.BlockSpec((PD, D), full),
                pl.BlockSpec((1, D), full),
            ],
            out_specs=pl.BlockSpec((tt, D), lambda i, d: (i, 0)),
            scratch_shapes=[pltpu.VMEM((TOP_K_IN_GROUP, tt, D), F32), pltpu.VMEM((TOP_K_IN_GROUP, tt, D), F32),
                            pltpu.SemaphoreType.DMA((2,))],
        ),
        compiler_params=_cparams(("arbitrary",), 40),
        name="moe_combine_ple",
    )(dest, h2, info, rows, p3, g.reshape(1, D), gw.astype(BF16), gb.reshape(1, D), pw.astype(BF16),
      fn.reshape(1, D))


IDX_SLOTS = 3


def _expert2_kernel(ce_ref, nv_ref, idx_hbm, h_hbm, g_ref, w1_ref, w3_ref, w2_ref, y_hbm,
                    idx_s, x_a, x_b, o_a, o_b, w1b, w3b, w2b, idx_sem, g_sem, s_sem, *, ch, n_tok, n_chunks):
    c = pl.program_id(0)
    nv = nv_ref[0]
    x_buf = (x_a, x_b)
    o_buf = (o_a, o_b)

    def idx_copy(chunk, sl):
        return pltpu.make_async_copy(idx_hbm.at[chunk], idx_s.at[sl], idx_sem.at[sl])

    def gather(chunk_slot, sl, r):
        return pltpu.make_async_copy(h_hbm.at[pl.ds(idx_s[chunk_slot, 0, r], 1)], x_buf[sl].at[pl.ds(r, 1)],
                                     g_sem.at[sl])

    def scatter(chunk_slot, sl, r):
        return pltpu.make_async_copy(o_buf[sl].at[pl.ds(r, 1)], y_hbm.at[pl.ds(idx_s[chunk_slot, 1, r], 1)],
                                     s_sem.at[sl])

    def wait_gathers(sl):
        pltpu.make_async_copy(h_hbm.at[pl.ds(0, ch)], x_buf[sl], g_sem.at[sl]).wait()

    def wait_scatters(sl):
        pltpu.make_async_copy(o_buf[sl], y_hbm.at[pl.ds(0, ch)], s_sem.at[sl]).wait()

    @pl.when(c == 0)
    def _():
        o_a[...] = jnp.zeros_like(o_a)
        o_b[...] = jnp.zeros_like(o_b)
        pltpu.make_async_copy(o_a, y_hbm.at[pl.ds(2 * n_tok, ch)], s_sem.at[0]).start()
        first = idx_copy(0, 0)
        first.start()
        first.wait()
        lax.fori_loop(0, ch, lambda r, z: (gather(0, 0, r).start(), z)[1], 0, unroll=8)
        null = idx_copy(n_chunks, IDX_SLOTS - 1)
        null.start()
        null.wait()
        idx_copy(jnp.minimum(1, n_chunks), 1).start()

    e = ce_ref[c]
    prev = ce_ref[jnp.maximum(c - 1, 0)]

    @pl.when(jnp.logical_or(c == 0, e != prev))
    def _():
        w1b[...] = w1_ref[0, 0].astype(BF16)
        w3b[...] = w3_ref[0, 0].astype(BF16)
        w2b[...] = w2_ref[0, 0].astype(BF16)

    def chunk_step(slot):
        other = 1 - slot
        cur3 = c % IDX_SLOTS
        nxt3 = (c + 1) % IDX_SLOTS
        prv3 = (c + IDX_SLOTS - 1) % IDX_SLOTS
        wait_scatters(slot)
        wait_gathers(slot)
        idx_copy(0, nxt3).wait()
        for r in range(ch):
            gather(nxt3, other, r).start()
            scatter(prv3, other, r).start()
        x = x_buf[slot][...]
        x = (x * lax.rsqrt(jnp.mean(x * x, axis=-1, keepdims=True) + RMS_EPS) * g_ref[...]).astype(BF16)
        a = _dot(x, w1b[...])
        b = _dot(x, w3b[...])
        hc = a * jax.nn.sigmoid(a) * b
        o_buf[slot][...] = _dot(hc.astype(BF16), w2b[...])
        idx_copy(jnp.minimum(c + 2, n_chunks), prv3).start()

        @pl.when(c == nv - 1)
        def _():
            lax.fori_loop(0, ch, lambda r, z: (scatter(cur3, slot, r).start(), z)[1], 0, unroll=8)
            wait_scatters(other)
            wait_scatters(slot)
            wait_gathers(other)
            idx_copy(0, prv3).wait()

    for parity in range(2):
        pl.when(jnp.logical_and(c < nv, c % 2 == parity))(functools.partial(chunk_step, parity))


def _experts2(chunk_e, n_valid, idx, h2, g, w1, w3, w2, layer):
    T, D = h2.shape
    F = w1.shape[-1]
    ch = MOE_CHUNK
    n_chunks = idx.shape[0] - 1
    wsel = lambda c, ce, nv: (layer, ce[c], 0, 0)
    return pl.pallas_call(
        functools.partial(_expert2_kernel, ch=ch, n_tok=T, n_chunks=n_chunks),
        out_shape=jax.ShapeDtypeStruct((TOP_K_IN_GROUP * T + 2 * ch, D), F32),
        grid_spec=pltpu.PrefetchScalarGridSpec(
            num_scalar_prefetch=2,
            grid=(n_chunks,),
            in_specs=[
                pl.BlockSpec(memory_space=pl.ANY),
                pl.BlockSpec(memory_space=pl.ANY),
                pl.BlockSpec((1, D), lambda c, ce, nv: (0, 0)),
                pl.BlockSpec((1, 1, D, F), wsel),
                pl.BlockSpec((1, 1, D, F), wsel),
                pl.BlockSpec((1, 1, F, D), wsel),
            ],
            out_specs=pl.BlockSpec(memory_space=pl.ANY),
            scratch_shapes=[
                pltpu.SMEM((IDX_SLOTS, 2, ch), I32),
                pltpu.VMEM((ch, D), F32), pltpu.VMEM((ch, D), F32),
                pltpu.VMEM((ch, D), F32), pltpu.VMEM((ch, D), F32),
                pltpu.VMEM((D, F), BF16), pltpu.VMEM((D, F), BF16), pltpu.VMEM((F, D), BF16),
                pltpu.SemaphoreType.DMA((IDX_SLOTS,)),
                pltpu.SemaphoreType.DMA((2,)),
                pltpu.SemaphoreType.DMA((2,)),
            ],
        ),
        compiler_params=_cparams(("arbitrary",), 56),
        name="moe_experts",
    )(chunk_e, n_valid, idx, h2, g.reshape(1, D), w1, w3, w2)


def _combine2_kernel(h_ref, info_ref, ya_ref, yb_ref, p_ref, g_ref, gw_ref, gb_ref, pw_ref, fn_ref, o_ref, *, final):
    info = info_ref[...]
    y = h_ref[...] + info[:, 2:3] * ya_ref[...] + info[:, 3:4] * yb_ref[...]
    hn = _rms(y, g_ref[...])
    gate = jax.nn.sigmoid(_dot(hn.astype(BF16), gw_ref[...]) + gb_ref[...])
    out = y + _dot(p_ref[0].astype(BF16), pw_ref[...]) * gate
    if final:
        out = _rms(out, fn_ref[...])
    o_ref[...] = out


def _combine2(h2, info, y2, p3, layer, g, gw, gb, pw, fn, final):
    T, D = h2.shape
    PD = p3.shape[-1]
    tt = min(SEQ_TILE, T)
    full = lambda i: (0, 0)
    return pl.pallas_call(
        functools.partial(_combine2_kernel, final=final),
        out_shape=jax.ShapeDtypeStruct((T, D), F32),
        grid=(T // tt,),
        in_specs=[
            pl.BlockSpec((tt, D), lambda i: (i, 0)),
            pl.BlockSpec((tt, LANES), lambda i: (i, 0)),
            pl.BlockSpec((tt, D), lambda i: (i, 0)),
            pl.BlockSpec((tt, D), lambda i: (T // tt + i, 0)),
            pl.BlockSpec((1, tt, PD), lambda i: (layer, i, 0)),
            pl.BlockSpec((1, D), full),
            pl.BlockSpec((D, D), full),
            pl.BlockSpec((1, D), full),
            pl.BlockSpec((PD, D), full),
            pl.BlockSpec((1, D), full),
        ],
        out_specs=pl.BlockSpec((tt, D), lambda i: (i, 0)),
        compiler_params=_cparams(("parallel",), 48),
        name="moe_combine_ple",
    )(h2, info, y2, y2, p3, g.reshape(1, D), gw.astype(BF16), gb.reshape(1, D), pw.astype(BF16), fn.reshape(1, D))


def _moe_ple_layer(h2, p3, layer, norm_ffn, rg_w, rg_b, re_w, re_b, w1, w3, w2, norm_ple, gate_w, gate_b, ple_proj,
                   final_norm, final):
    T, D = h2.shape
    A = T * TOP_K_IN_GROUP
    pad = LANES - N_EXPERT_GROUPS - N_EXPERTS
    wr = jnp.concatenate([rg_w, re_w, jnp.zeros((D, pad), F32)], axis=1)
    br = jnp.concatenate([rg_b, re_b, jnp.zeros((pad,), F32)]).reshape(1, LANES)
    info, cnt = _router(h2, norm_ffn, wr, br)
    counts = cnt[0, N_EXPERT_GROUPS:N_EXPERT_GROUPS + N_EXPERTS].astype(I32)
    n_chunks_e = (counts + MOE_CHUNK - 1) // MOE_CHUNK
    chunk_end = jnp.cumsum(n_chunks_e)
    pstarts = (chunk_end - n_chunks_e) * MOE_CHUNK
    n_chunks = -(-A // MOE_CHUNK) + N_EXPERTS
    n_valid = chunk_end[-1:].astype(I32)
    cidx = jnp.minimum(jnp.arange(n_chunks, dtype=I32), n_valid[0] - 1)
    chunk_e = jnp.sum((chunk_end[None, :] <= cidx[:, None]).astype(I32), axis=1)
    chunk_e = jnp.minimum(chunk_e, N_EXPERTS - 1)
    e_idx = info[:, 0:TOP_K_IN_GROUP].astype(I32)
    rank = info[:, 4:4 + TOP_K_IN_GROUP].astype(I32)
    dest = (pstarts[e_idx] + rank).T.reshape(A)
    xs = _dispatch(dest, chunk_end.astype(I32), h2, norm_ffn, n_chunks * MOE_CHUNK)
    rows = _experts(chunk_e, n_valid, xs, w1, w3, w2, layer)
    return _combine(dest, h2, info, rows, p3, layer, norm_ple, gate_w, gate_b, ple_proj, final_norm, final)


def _rope_rows(xt, cos, sin):
    x1 = xt[0:ROPE_HALF]
    x2 = xt[ROPE_HALF:ROPE_DIMS]
    return jnp.concatenate([x1 * cos - x2 * sin, x2 * cos + x1 * sin, xt[ROPE_DIMS:]], axis=0)


def _kv_kernel(h_ref, g_ref, wn_ref, wt_ref, cos_ref, sin_ref, cv_ref, kst_ref, vs_ref, kwt_ref, vw_ref, *, ts):
    i = pl.program_id(1)
    kvw = N_KV_GROUPS * HEAD_DIM
    hn = _rms(h_ref[0], g_ref[...]).astype(BF16)
    nat = _dot(hn, wn_ref[...])
    cv_ref[0] = nat[:, 0:2 * kvw]
    tr = _dot_nt(wt_ref[...], hn)
    cos = cos_ref[...]
    sin = sin_ref[...]
    blk = lax.broadcasted_iota(I32, (MAX_SEL_BLOCKS, ts), 0)
    pos = i * ts + lax.broadcasted_iota(I32, (MAX_SEL_BLOCKS, ts), 1)
    onehot = jnp.where(pos // SEL_BLOCK == blk, 1.0, 0.0).astype(BF16)
    for g in range(N_KV_GROUPS):
        c0 = g * HEAD_DIM
        kst_ref[0, g, 0:MAX_SEL_BLOCKS, :] = onehot
        kst_ref[0, g, MAX_SEL_BLOCKS:, :] = _rope_rows(tr[c0:c0 + HEAD_DIM], cos, sin).astype(BF16)
        kwt_ref[0, g] = _rope_rows(tr[kvw + c0:kvw + c0 + HEAD_DIM], cos, sin).astype(BF16)
        vs_ref[0, g] = nat[:, 2 * kvw + c0:2 * kvw + c0 + HEAD_DIM].astype(BF16)
        vw_ref[0, g] = nat[:, 3 * kvw + c0:3 * kvw + c0 + HEAD_DIM].astype(BF16)


def _rope_tables(pos):
    inv = jnp.float32(ROPE_THETA) ** (-jnp.arange(ROPE_HALF, dtype=F32) * 2.0 / ROPE_DIMS)
    ang = pos.astype(F32)[:, None] * inv[None, :]
    return jnp.cos(ang), jnp.sin(ang)


def _shared_kv_proj(h, kv_norm, w_kv):
    B, S, D = h.shape
    G, dh = N_KV_GROUPS, HEAD_DIM
    kvw = G * dh
    ts = min(SEQ_TILE, S)
    br = lambda k: w_kv[:, k * kvw:(k + 1) * kvw]
    w_nat = jnp.concatenate([br(0), br(1), br(3), br(5)], axis=1).astype(BF16)
    w_tr = jnp.concatenate([br(2), br(4)], axis=1).T.astype(BF16)
    cos, sin = _rope_tables(jnp.arange(S))
    kd = MAX_SEL_BLOCKS + dh
    return pl.pallas_call(
        functools.partial(_kv_kernel, ts=ts),
        out_shape=(
            jax.ShapeDtypeStruct((B, S, 2 * kvw), F32),
            jax.ShapeDtypeStruct((B, G, kd, S), BF16),
            jax.ShapeDtypeStruct((B, G, S, dh), BF16),
            jax.ShapeDtypeStruct((B, G, dh, S), BF16),
            jax.ShapeDtypeStruct((B, G, S, dh), BF16),
        ),
        grid=(B, S // ts),
        in_specs=[
            pl.BlockSpec((1, ts, D), lambda b, i: (b, i, 0)),
            pl.BlockSpec((1, D), lambda b, i: (0, 0)),
            pl.BlockSpec((D, 4 * kvw), lambda b, i: (0, 0)),
            pl.BlockSpec((2 * kvw, D), lambda b, i: (0, 0)),
            pl.BlockSpec((ROPE_HALF, ts), lambda b, i: (0, i)),
            pl.BlockSpec((ROPE_HALF, ts), lambda b, i: (0, i)),
        ],
        out_specs=(
            pl.BlockSpec((1, ts, 2 * kvw), lambda b, i: (b, i, 0)),
            pl.BlockSpec((1, G, kd, ts), lambda b, i: (b, 0, 0, i)),
            pl.BlockSpec((1, G, ts, dh), lambda b, i: (b, 0, i, 0)),
            pl.BlockSpec((1, G, dh, ts), lambda b, i: (b, 0, 0, i)),
            pl.BlockSpec((1, G, ts, dh), lambda b, i: (b, 0, i, 0)),
        ),
        compiler_params=_cparams(("parallel", "parallel"), 48),
        name="shared_kv_proj",
    )(h, kv_norm.reshape(1, D), w_nat, w_tr, cos.T, sin.T)


def _compress_kernel(x_ref, pos_ref, w1_ref, b1_ref, w2_ref, b2_ref, cos_ref, sin_ref, o_ref, *, nh, keys):
    x = x_ref[0, 0, 0]
    a = _dot((x + pos_ref[0:1]).astype(BF16), w1_ref[0])
    b = _dot((x + pos_ref[1:2]).astype(BF16), w1_ref[1])
    hid = jax.nn.gelu(a + pltpu.roll(b, nh - 1, axis=0) + b1_ref[...]).astype(BF16)
    if keys:
        out = _dot_nt(w2_ref[...], hid) + b2_ref[...]
        o_ref[0, 0] = _rope_rows(out, cos_ref[...], sin_ref[...]).astype(BF16)
    else:
        o_ref[0, 0] = (_dot(hid, w2_ref[...]) + b2_ref[...]).astype(BF16)


def _compress(halves, which, pos_emb, w1, b1, w2, b2, keys):
    _, B, G, nh, hw = halves.shape
    dh = HEAD_DIM
    hidden = w1.shape[-1]
    pos2 = pos_emb.reshape(2, hw)
    w1s = w1.reshape(2, hw, hidden).astype(BF16)
    cos, sin = _rope_tables(jnp.arange(nh) * CMP_STRIDE + CMP_BLOCK - 1)
    if keys:
        w2a, b2a = w2.T.astype(BF16), b2.reshape(dh, 1)
        out_shape, out_block, out_idx = (B, G, dh, nh), (1, 1, dh, nh), lambda b, g: (b, g, 0, 0)
    else:
        w2a, b2a = w2.astype(BF16), b2.reshape(1, dh)
        out_shape, out_block, out_idx = (B, G, nh, dh), (1, 1, nh, dh), lambda b, g: (b, g, 0, 0)
    c2 = lambda b, g: (0, 0)
    return pl.pallas_call(
        functools.partial(_compress_kernel, nh=nh, keys=keys),
        out_shape=jax.ShapeDtypeStruct(out_shape, BF16),
        grid=(B, G),
        in_specs=[
            pl.BlockSpec((1, 1, 1, nh, hw), lambda b, g: (which, b, g, 0, 0)),
            pl.BlockSpec((2, hw), c2),
            pl.BlockSpec((2, hw, hidden), lambda b, g: (0, 0, 0)),
            pl.BlockSpec((1, hidden), c2),
            pl.BlockSpec(w2a.shape, c2),
            pl.BlockSpec(b2a.shape, c2),
            pl.BlockSpec((ROPE_HALF, nh), c2),
            pl.BlockSpec((ROPE_HALF, nh), c2),
        ],
        out_specs=pl.BlockSpec(out_block, out_idx),
        compiler_params=_cparams(("parallel", "parallel"), 40),
        name="compress_k" if keys else "compress_v",
    )(halves, pos2, w1s, b1.reshape(1, hidden), w2a, b2a, cos.T, sin.T)


def _qproj_kernel(h_ref, g_ref, wq_ref, wg_ref, bg_ref, c_ref, s1_ref, s2_ref, q_ref, gt_ref):
    xn = _rms(h_ref[0], g_ref[...]).astype(BF16)
    q = _dot(xn, wq_ref[...])
    c, s1, s2 = c_ref[...], s1_ref[...], s2_ref[...]
    for k in range(q.shape[-1] // LANES):
        x = q[:, k * LANES:(k + 1) * LANES]
        r = x * c + pltpu.roll(x, ROPE_HALF, axis=1) * s1 + pltpu.roll(x, LANES - ROPE_HALF, axis=1) * s2
        q_ref[0, :, k * LANES:(k + 1) * LANES] = r.astype(BF16)
    gt_ref[0] = jax.nn.sigmoid(_dot(xn, wg_ref[...]) + bg_ref[...])


def _q_proj(h, g, w_qg, b_gate):
    B, S, D = h.shape
    HD = N_HEADS * HEAD_DIM
    ng = N_HEADS * N_BRANCH
    ts = min(SEQ_TILE, S)
    wq = w_qg[:, :HD].astype(BF16)
    wg = jnp.pad(w_qg[:, HD:], ((0, 0), (0, LANES - ng))).astype(BF16)
    bg = jnp.pad(b_gate, (0, LANES - ng)).reshape(1, LANES)
    cos, sin = _rope_tables(jnp.arange(S))
    scale = HEAD_DIM ** -0.5
    ones = jnp.ones((S, HEAD_DIM - ROPE_DIMS), F32)
    zeros = jnp.zeros((S, HEAD_DIM - ROPE_DIMS), F32)
    zh = jnp.zeros((S, ROPE_HALF), F32)
    two = lambda a: jnp.concatenate([a, a], axis=1) * scale
    c = two(jnp.concatenate([cos, cos, ones], axis=1))
    s1 = two(jnp.concatenate([zh, sin, zeros], axis=1))
    s2 = two(jnp.concatenate([-sin, zh, zeros], axis=1))
    tab = pl.BlockSpec((ts, LANES), lambda b, i: (i, 0))
    return pl.pallas_call(
        _qproj_kernel,
        out_shape=(jax.ShapeDtypeStruct((B, S, HD), BF16), jax.ShapeDtypeStruct((B, S, LANES), F32)),
        grid=(B, S // ts),
        in_specs=[
            pl.BlockSpec((1, ts, D), lambda b, i: (b, i, 0)),
            pl.BlockSpec((1, D), lambda b, i: (0, 0)),
            pl.BlockSpec((D, HD), lambda b, i: (0, 0)),
            pl.BlockSpec((D, LANES), lambda b, i: (0, 0)),
            pl.BlockSpec((1, LANES), lambda b, i: (0, 0)),
            tab, tab, tab,
        ],
        out_specs=(pl.BlockSpec((1, ts, HD), lambda b, i: (b, i, 0)),
                   pl.BlockSpec((1, ts, LANES), lambda b, i: (b, i, 0))),
        compiler_params=_cparams(("parallel", "parallel"), 40),
        name="nsa_q_proj",
    )(h, g.reshape(1, D), wq, wg, bg, c, s1, s2)


def _stack_heads(qb):
    return jnp.concatenate([qb[:, r * HEAD_DIM:(r + 1) * HEAD_DIM] for r in range(HEADS_PER_GROUP)], axis=0)


def _gated_unstack(o, gates, g, branch, tq):
    lane = lax.broadcasted_iota(I32, gates.shape, 1)
    cols = []
    for r in range(HEADS_PER_GROUP):
        col = (g * HEADS_PER_GROUP + r) * N_BRANCH + branch
        gate = jnp.sum(jnp.where(lane == col, gates, 0.0), axis=-1, keepdims=True)
        cols.append(o[r * tq:(r + 1) * tq] * gate)
    return jnp.concatenate(cols, axis=-1)


def _query_pos(i, tq):
    rows = lax.broadcasted_iota(I32, (HEADS_PER_GROUP * tq, 1), 0)
    return i * tq + (rows & (tq - 1))


def _cmp_kernel(q_ref, gt_ref, k_ref, v_ref, o_ref, bias_ref, *, tq, nc):
    g = pl.program_id(1)
    i = pl.program_id(2)
    qs = _stack_heads(q_ref[0])
    s = _dot(qs, k_ref[0, 0])
    t = _query_pos(i, tq)
    n = lax.broadcasted_iota(I32, (1, nc), 1)
    s = jnp.where(n * CMP_STRIDE + (CMP_BLOCK - 1) <= t, s, -jnp.inf)
    m = jnp.max(s, axis=-1, keepdims=True)
    m = jnp.where(m == -jnp.inf, 0.0, m)
    e = jnp.exp(s - m)
    p = e / jnp.maximum(jnp.sum(e, axis=-1, keepdims=True), 1e-30)
    o = _dot(p.astype(BF16), v_ref[0, 0])
    o_ref[0] = _gated_unstack(o, gt_ref[0], g, 0, tq).astype(BF16)
    ps = p[0:tq]
    for r in range(1, HEADS_PER_GROUP):
        ps = ps + p[r * tq:(r + 1) * tq]
    nn = lax.broadcasted_iota(I32, (nc, MAX_SEL_BLOCKS), 0) * CMP_STRIDE
    jj = lax.broadcasted_iota(I32, (nc, MAX_SEL_BLOCKS), 1) * SEL_BLOCK
    c2s = jnp.where(nn < jj + SEL_BLOCK, jnp.where(nn + CMP_BLOCK > jj, 1.0, 0.0), 0.0)
    imp = jnp.dot(ps, c2s, precision=HIGHEST, preferred_element_type=F32)
    tq_pos = i * tq + lax.broadcasted_iota(I32, (tq, 1), 0)
    cur = tq_pos // SEL_BLOCK
    j = lax.broadcasted_iota(I32, (tq, MAX_SEL_BLOCKS), 1)
    valid = j <= cur
    bonus = jnp.where(j == 0, FORCE_BONUS, jnp.where(j == cur, FORCE_BONUS, jnp.where(j == cur - 1, FORCE_BONUS, 0.0)))
    work = jnp.where(valid, imp + bonus, -jnp.inf)
    sel = jnp.zeros((tq, MAX_SEL_BLOCKS), F32)
    for _ in range(N_SELECT):
        mx = jnp.max(work, axis=-1, keepdims=True)
        idx = jnp.min(jnp.where(work == mx, j, MAX_SEL_BLOCKS), axis=-1, keepdims=True)
        pick = j == idx
        sel = jnp.where(pick, 1.0, sel)
        work = jnp.where(pick, -jnp.inf, work)
    bias_ref[0, 0] = jnp.where(valid, jnp.where(sel > 0.0, 0.0, MASK_BIAS), MASK_BIAS).astype(BF16)


def _cmp_attn(q, gates, kct, vc):
    B, S, HD = q.shape
    G, dh = N_KV_GROUPS, HEAD_DIM
    gw = HD // G
    nc = kct.shape[-1]
    tq = min(Q_TILE, S)
    return pl.pallas_call(
        functools.partial(_cmp_kernel, tq=tq, nc=nc),
        out_shape=(jax.ShapeDtypeStruct((B, S, HD), BF16), jax.ShapeDtypeStruct((B, G, S, MAX_SEL_BLOCKS), BF16)),
        grid=(B, G, S // tq),
        in_specs=[
            pl.BlockSpec((1, tq, gw), lambda b, g, i: (b, i, g)),
            pl.BlockSpec((1, tq, LANES), lambda b, g, i: (b, i, 0)),
            pl.BlockSpec((1, 1, dh, nc), lambda b, g, i: (b, g, 0, 0)),
            pl.BlockSpec((1, 1, nc, dh), lambda b, g, i: (b, g, 0, 0)),
        ],
        out_specs=(pl.BlockSpec((1, tq, gw), lambda b, g, i: (b, i, g)),
                   pl.BlockSpec((1, 1, tq, MAX_SEL_BLOCKS), lambda b, g, i: (b, g, i, 0))),
        compiler_params=_cparams(("parallel", "parallel", "parallel"), 40),
        name="nsa_compressed",
    )(q, gates, kct, vc)


def _sel_kernel(q_ref, bias_ref, gt_ref, k_ref, v_ref, o_ref, qa, m_s, l_s, acc, *, tq, tk):
    g = pl.program_id(1)
    i = pl.program_id(2)
    qb = q_ref[0]
    bias = bias_ref[0, 0]
    for r in range(HEADS_PER_GROUP):
        qa[r * tq:(r + 1) * tq, 0:MAX_SEL_BLOCKS] = bias
        qa[r * tq:(r + 1) * tq, MAX_SEL_BLOCKS:] = qb[:, r * HEAD_DIM:(r + 1) * HEAD_DIM]
    m_s[...] = jnp.full_like(m_s, -jnp.inf)
    l_s[...] = jnp.zeros_like(l_s)
    acc[...] = jnp.zeros_like(acc)
    t = _query_pos(i, tq)
    lane = lax.broadcasted_iota(I32, (1, tk), 1)

    def body(jt, c):
        k0 = pl.multiple_of(jt * tk, tk)
        s = _dot(qa[...], k_ref[0, 0, :, pl.ds(k0, tk)])
        s = jnp.where(k0 + lane <= t, s, -jnp.inf)
        m_old = m_s[...]
        m_new = jnp.maximum(m_old, jnp.max(s, axis=-1, keepdims=True))
        alpha = jnp.exp(m_old - m_new)
        p = jnp.exp(s - m_new)
        l_s[...] = alpha * l_s[...] + jnp.sum(p, axis=-1, keepdims=True)
        acc[...] = alpha * acc[...] + _dot(p.astype(BF16), v_ref[0, 0, pl.ds(k0, tk), :])
        m_s[...] = m_new
        return c

    lax.fori_loop(0, (i * tq + tq + tk - 1) // tk, body, 0)
    o_ref[0] = _gated_unstack(acc[...] / l_s[...], gt_ref[0], g, 1, tq).astype(BF16)


def _sel_attn(q, bias, gates, kst, vs):
    B, S, HD = q.shape
    G, dh = N_KV_GROUPS, HEAD_DIM
    gw = HD // G
    kd = kst.shape[2]
    tq = min(Q_TILE, S)
    tk = min(KEY_TILE, S)
    rows = HEADS_PER_GROUP * tq
    return pl.pallas_call(
        functools.partial(_sel_kernel, tq=tq, tk=tk),
        out_shape=jax.ShapeDtypeStruct((B, S, HD), BF16),
        grid=(B, G, S // tq),
        in_specs=[
            pl.BlockSpec((1, tq, gw), lambda b, g, i: (b, i, g)),
            pl.BlockSpec((1, 1, tq, MAX_SEL_BLOCKS), lambda b, g, i: (b, g, i, 0)),
            pl.BlockSpec((1, tq, LANES), lambda b, g, i: (b, i, 0)),
            pl.BlockSpec((1, 1, kd, S), lambda b, g, i: (b, g, 0, 0)),
            pl.BlockSpec((1, 1, S, dh), lambda b, g, i: (b, g, 0, 0)),
        ],
        out_specs=pl.BlockSpec((1, tq, gw), lambda b, g, i: (b, i, g)),
        scratch_shapes=[pltpu.VMEM((rows, kd), BF16), pltpu.VMEM((rows, 1), F32), pltpu.VMEM((rows, 1), F32),
                        pltpu.VMEM((rows, dh), F32)],
        compiler_params=_cparams(("parallel", "parallel", "arbitrary"), 48),
        name="nsa_selected",
    )(q, bias, gates, kst, vs)


def _win_kernel(q_ref, gt_ref, k_ref, v_ref, o_ref, *, tq, wb):
    g = pl.program_id(1)
    i = pl.program_id(2)
    k0 = pl.multiple_of(jnp.maximum(i * tq - WINDOW, 0), tq)
    qs = _stack_heads(q_ref[0])
    s = _dot(qs, k_ref[0, 0, :, pl.ds(k0, wb)])
    t = _query_pos(i, tq)
    kpos = k0 + lax.broadcasted_iota(I32, (1, wb), 1)
    s = jnp.where(kpos <= t, jnp.where(kpos > t - WINDOW, s, -jnp.inf), -jnp.inf)
    m = jnp.max(s, axis=-1, keepdims=True)
    e = jnp.exp(s - m)
    p = e / jnp.sum(e, axis=-1, keepdims=True)
    o = _dot(p.astype(BF16), v_ref[0, 0, pl.ds(k0, wb), :])
    o_ref[0] = _gated_unstack(o, gt_ref[0], g, 2, tq).astype(BF16)


def _win_attn(q, gates, kwt, vw):
    B, S, HD = q.shape
    G, dh = N_KV_GROUPS, HEAD_DIM
    gw = HD // G
    tq = min(Q_TILE, S)
    wb = WINDOW + tq
    assert S >= wb and tq % LANES == 0
    return pl.pallas_call(
        functools.partial(_win_kernel, tq=tq, wb=wb),
        out_shape=jax.ShapeDtypeStruct((B, S, HD), BF16),
        grid=(B, G, S // tq),
        in_specs=[
            pl.BlockSpec((1, tq, gw), lambda b, g, i: (b, i, g)),
            pl.BlockSpec((1, tq, LANES), lambda b, g, i: (b, i, 0)),
            pl.BlockSpec((1, 1, dh, S), lambda b, g, i: (b, g, 0, 0)),
            pl.BlockSpec((1, 1, S, dh), lambda b, g, i: (b, g, 0, 0)),
        ],
        out_specs=pl.BlockSpec((1, tq, gw), lambda b, g, i: (b, i, g)),
        compiler_params=_cparams(("parallel", "parallel", "parallel"), 40),
        name="nsa_window",
    )(q, gates, kwt, vw)


LOG2E = 1.4426950408889634
Q_SCALE = HEAD_DIM ** -0.5 * LOG2E
V_ROWS = HEAD_DIM + 16
KEY_AUG = 2 * LANES
NEG_BIG = -1e30


def _lane_rope_tables(pos):
    cos, sin = _rope_tables(pos)
    n = pos.shape[0]
    ones = jnp.ones((n, HEAD_DIM - ROPE_DIMS), F32)
    zeros = jnp.zeros((n, HEAD_DIM - ROPE_DIMS), F32)
    zh = jnp.zeros((n, ROPE_HALF), F32)
    two = lambda a: jnp.concatenate([a, a], axis=1)
    return (two(jnp.concatenate([cos, cos, ones], axis=1)), two(jnp.concatenate([zh, sin, zeros], axis=1)),
            two(jnp.concatenate([-sin, zh, zeros], axis=1)))


def _rope_lanes(x, c, s1, s2):
    return x * c + pltpu.roll(x, ROPE_HALF, axis=1) * s1 + pltpu.roll(x, LANES - ROPE_HALF, axis=1) * s2


def _kv2_kernel(h_ref, g_ref, wn_ref, wt_ref, c_ref, s1_ref, s2_ref, cv_ref, ksa_ref, vst_ref, kw_ref, vwt_ref, *, ts):
    i = pl.program_id(1)
    kvw = N_KV_GROUPS * HEAD_DIM
    hn = _rms(h_ref[0], g_ref[...]).astype(BF16)
    nat = _dot(hn, wn_ref[...])
    cv_ref[0] = nat[:, 0:2 * kvw]
    tr = _dot_nt(wt_ref[...], hn)
    c, s1, s2 = c_ref[...], s1_ref[...], s2_ref[...]
    roped = [_rope_lanes(nat[:, 2 * kvw + k * LANES:2 * kvw + (k + 1) * LANES], c, s1, s2)
             for k in range(2 * kvw // LANES)]
    lane = lax.broadcasted_iota(I32, (ts, KEY_AUG), 1)
    pos = i * ts + lax.broadcasted_iota(I32, (ts, KEY_AUG), 0)
    onehot = jnp.where(lane - HEAD_DIM == pos // SEL_BLOCK, 1.0, 0.0).astype(BF16)
    ones_row = jnp.where(lax.broadcasted_iota(I32, (V_ROWS - HEAD_DIM, ts), 0) == 0, 1.0, 0.0).astype(BF16)
    per_tile = LANES // HEAD_DIM
    for g in range(N_KV_GROUPS):
        lo = (g % per_tile) * HEAD_DIM
        ksa_ref[0, g] = onehot
        ksa_ref[0, g, :, 0:HEAD_DIM] = roped[g // per_tile][:, lo:lo + HEAD_DIM].astype(BF16)
        kw_ref[0, g] = roped[N_KV_GROUPS // per_tile + g // per_tile][:, lo:lo + HEAD_DIM].astype(BF16)
        for ref, base in ((vst_ref, 0), (vwt_ref, kvw)):
            ref[0, g, 0:HEAD_DIM, :] = tr[base + g * HEAD_DIM:base + (g + 1) * HEAD_DIM].astype(BF16)
            ref[0, g, HEAD_DIM:, :] = ones_row


def _shared_kv_proj2(h, kv_norm, w_kv):
    B, S, D = h.shape
    G, dh = N_KV_GROUPS, HEAD_DIM
    kvw = G * dh
    ts = min(SEQ_TILE, S)
    br = lambda k: w_kv[:, k * kvw:(k + 1) * kvw]
    w_nat = jnp.concatenate([br(0), br(1), br(2), br(4)], axis=1).astype(BF16)
    w_tr = jnp.concatenate([br(3), br(5)], axis=1).T.astype(BF16)
    tabs = _lane_rope_tables(jnp.arange(S))
    tab = pl.BlockSpec((ts, LANES), lambda b, i: (i, 0))
    return pl.pallas_call(
        functools.partial(_kv2_kernel, ts=ts),
        out_shape=(
            jax.ShapeDtypeStruct((B, S, 2 * kvw), F32),
            jax.ShapeDtypeStruct((B, G, S, KEY_AUG), BF16),
            jax.ShapeDtypeStruct((B, G, V_ROWS, S), BF16),
            jax.ShapeDtypeStruct((B, G, S, dh), BF16),
            jax.ShapeDtypeStruct((B, G, V_ROWS, S), BF16),
        ),
        grid=(B, S // ts),
        in_specs=[
            pl.BlockSpec((1, ts, D), lambda b, i: (b, i, 0)),
            pl.BlockSpec((1, D), lambda b, i: (0, 0)),
            pl.BlockSpec((D, 4 * kvw), lambda b, i: (0, 0)),
            pl.BlockSpec((2 * kvw, D), lambda b, i: (0, 0)),
            tab, tab, tab,
        ],
        out_specs=(
            pl.BlockSpec((1, ts, 2 * kvw), lambda b, i: (b, i, 0)),
            pl.BlockSpec((1, G, ts, KEY_AUG), lambda b, i: (b, 0, i, 0)),
            pl.BlockSpec((1, G, V_ROWS, ts), lambda b, i: (b, 0, 0, i)),
            pl.BlockSpec((1, G, ts, dh), lambda b, i: (b, 0, i, 0)),
            pl.BlockSpec((1, G, V_ROWS, ts), lambda b, i: (b, 0, 0, i)),
        ),
        compiler_params=_cparams(("parallel", "parallel"), 48),
        name="shared_kv_proj",
    )(h, kv_norm.reshape(1, D), w_nat, w_tr, *tabs)


def _compress2_kernel(x_ref, pos_ref, w1_ref, b1_ref, w2_ref, b2_ref, c_ref, s1_ref, s2_ref, o_ref, *, nh, keys):
    x = x_ref[0, 0, 0]
    a = _dot((x + pos_ref[0:1]).astype(BF16), w1_ref[0])
    b = _dot((x + pos_ref[1:2]).astype(BF16), w1_ref[1])
    hid = jax.nn.gelu(a + pltpu.roll(b, nh - 1, axis=0) + b1_ref[...]).astype(BF16)
    if keys:
        out = _rope_lanes(_dot(hid, w2_ref[...]) + b2_ref[...], c_ref[...], s1_ref[...], s2_ref[...])
        o_ref[0, 0] = out[:, 0:HEAD_DIM].astype(BF16)
    else:
        o_ref[0, 0] = (_dot_nt(w2_ref[...], hid) + b2_ref[...]).astype(BF16)


def _compress2(halves, which, pos_emb, w1, b1, w2, b2, keys):
    _, B, G, nh, hw = halves.shape
    dh = HEAD_DIM
    hidden = w1.shape[-1]
    pos2 = pos_emb.reshape(2, hw)
    w1s = w1.reshape(2, hw, hidden).astype(BF16)
    tabs = _lane_rope_tables(jnp.arange(nh) * CMP_STRIDE + CMP_BLOCK - 1)
    if keys:
        w2a = jnp.pad(w2, ((0, 0), (0, LANES - dh))).astype(BF16)
        b2a = jnp.pad(b2, (0, LANES - dh)).reshape(1, LANES)
        out_shape, out_block = (B, G, nh, dh), (1, 1, nh, dh)
    else:
        w2a, b2a = w2.T.astype(BF16), b2.reshape(dh, 1)
        out_shape, out_block = (B, G, dh, nh), (1, 1, dh, nh)
    c2 = lambda b, g: (0, 0)
    tab = pl.BlockSpec((nh, LANES), c2)
    return pl.pallas_call(
        functools.partial(_compress2_kernel, nh=nh, keys=keys),
        out_shape=jax.ShapeDtypeStruct(out_shape, BF16),
        grid=(B, G),
        in_specs=[
            pl.BlockSpec((1, 1, 1, nh, hw), lambda b, g: (which, b, g, 0, 0)),
            pl.BlockSpec((2, hw), c2),
            pl.BlockSpec((2, hw, hidden), lambda b, g: (0, 0, 0)),
            pl.BlockSpec((1, hidden), c2),
            pl.BlockSpec(w2a.shape, c2),
            pl.BlockSpec(b2a.shape, c2),
            tab, tab, tab,
        ],
        out_specs=pl.BlockSpec(out_block, lambda b, g: (b, g, 0, 0)),
        compiler_params=_cparams(("parallel", "parallel"), 40),
        name="compress_k" if keys else "compress_v",
    )(halves, pos2, w1s, b1.reshape(1, hidden), w2a, b2a, *tabs)


def _qproj2_kernel(h_ref, g_ref, wqt_ref, wg_ref, bg_ref, cos_ref, sin_ref, qt_ref, gt_ref):
    xn = _rms(h_ref[0], g_ref[...]).astype(BF16)
    tr = _dot_nt(wqt_ref[...], xn)
    cos, sin = cos_ref[...], sin_ref[...]
    for hd in range(N_HEADS):
        rows = slice(hd * HEAD_DIM, (hd + 1) * HEAD_DIM)
        qt_ref[0, rows, :] = (_rope_rows(tr[rows], cos, sin) * Q_SCALE).astype(BF16)
    gt_ref[0] = jax.nn.sigmoid(_dot(xn, wg_ref[...]) + bg_ref[...])


def _q_proj2(h, g, w_qg, b_gate):
    B, S, D = h.shape
    HD = N_HEADS * HEAD_DIM
    ng = N_HEADS * N_BRANCH
    ts = min(SEQ_TILE, S)
    wqt = w_qg[:, :HD].T.astype(BF16)
    wg = jnp.pad(w_qg[:, HD:], ((0, 0), (0, LANES - ng))).astype(BF16)
    bg = jnp.pad(b_gate, (0, LANES - ng)).reshape(1, LANES)
    cos, sin = _rope_tables(jnp.arange(S))
    tab = pl.BlockSpec((ROPE_HALF, ts), lambda b, i: (0, i))
    return pl.pallas_call(
        _qproj2_kernel,
        out_shape=(jax.ShapeDtypeStruct((B, HD, S), BF16), jax.ShapeDtypeStruct((B, S, LANES), F32)),
        grid=(B, S // ts),
        in_specs=[
            pl.BlockSpec((1, ts, D), lambda b, i: (b, i, 0)),
            pl.BlockSpec((1, D), lambda b, i: (0, 0)),
            pl.BlockSpec((HD, D), lambda b, i: (0, 0)),
            pl.BlockSpec((D, LANES), lambda b, i: (0, 0)),
            pl.BlockSpec((1, LANES), lambda b, i: (0, 0)),
            tab, tab,
        ],
        out_specs=(pl.BlockSpec((1, HD, ts), lambda b, i: (b, 0, i)),
                   pl.BlockSpec((1, ts, LANES), lambda b, i: (b, i, 0))),
        compiler_params=_cparams(("parallel", "parallel"), 40),
        name="nsa_q_proj",
    )(h, g.reshape(1, D), wqt, wg, bg, cos.T, sin.T)


def _heads_on_lanes(qt):
    return jnp.concatenate([qt[r * HEAD_DIM:(r + 1) * HEAD_DIM] for r in range(HEADS_PER_GROUP)], axis=1)


def _lane_query_pos(i, tq):
    lanes = lax.broadcasted_iota(I32, (1, HEADS_PER_GROUP * tq), 1)
    return i * tq + (lanes & (tq - 1))


def _finish_heads(acc_t, gates, g, branch, tq, denom_row):
    lane = lax.broadcasted_iota(I32, gates.shape, 1)
    rows = acc_t.shape[0]
    cols = []
    for r in range(HEADS_PER_GROUP):
        blk = acc_t[:, r * tq:(r + 1) * tq]
        nat = jnp.concatenate([blk, jnp.zeros((-rows % LANES, tq), F32)], axis=0).T
        col = (g * HEADS_PER_GROUP + r) * N_BRANCH + branch
        scale = jnp.sum(jnp.where(lane == col, gates, 0.0), axis=-1, keepdims=True)
        if denom_row is not None:
            scale = scale / nat[:, denom_row:denom_row + 1]
        cols.append(nat[:, 0:HEAD_DIM] * scale)
    return jnp.concatenate(cols, axis=-1)


def _cmp2_kernel(qt_ref, gt_ref, k_ref, vt_ref, c2s_ref, o_ref, bias_ref, *, tq, nc):
    g = pl.program_id(1)
    i = pl.program_id(2)

    def body(nk, nb):
        s = _dot(k_ref[0, 0, 0:nk, :], _heads_on_lanes(qt_ref[0]))
        t = _lane_query_pos(i, tq)
        n = lax.broadcasted_iota(I32, (nk, 1), 0)
        s = jnp.where(n * CMP_STRIDE + (CMP_BLOCK - 1) <= t, s, -jnp.inf)
        m = jnp.max(s, axis=0, keepdims=True)
        m = jnp.where(m == -jnp.inf, 0.0, m)
        e = jnp.exp2(s - m)
        p = e * (1.0 / jnp.maximum(jnp.sum(e, axis=0, keepdims=True), 1e-30))
        ot = _dot(vt_ref[0, 0, :, 0:nk], p.astype(BF16))
        o_ref[0] = _finish_heads(ot, gt_ref[0], g, 0, tq, None).astype(BF16)
        ps = p[:, 0:tq]
        for r in range(1, HEADS_PER_GROUP):
            ps = ps + p[:, r * tq:(r + 1) * tq]
        hi = ps.astype(BF16)
        rem = ps - hi.astype(F32)
        mid = rem.astype(BF16)
        lo = (rem - mid.astype(F32)).astype(BF16)
        c2s = c2s_ref[0:nb, 0:nk]
        imp = _dot(c2s, hi) + _dot(c2s, mid) + _dot(c2s, lo)
        cur = (i * tq + lax.broadcasted_iota(I32, (1, tq), 1)) // SEL_BLOCK
        j = lax.broadcasted_iota(I32, (nb, tq), 0)
        valid = j <= cur
        bonus = jnp.where(j == 0, FORCE_BONUS,
                          jnp.where(j == cur, FORCE_BONUS, jnp.where(j == cur - 1, FORCE_BONUS, 0.0)))
        work = jnp.where(valid, imp + bonus, -jnp.inf)
        sel = jnp.zeros((nb, tq), F32)
        for _ in range(N_SELECT):
            mx = jnp.max(work, axis=0, keepdims=True)
            idx = jnp.min(jnp.where(work == mx, j, MAX_SEL_BLOCKS), axis=0, keepdims=True)
            pick = j == idx
            sel = jnp.where(pick, 1.0, sel)
            work = jnp.where(pick, -jnp.inf, work)
        bias_ref[0, 0, 0:nb, :] = jnp.where(valid, jnp.where(sel > 0.0, 0.0, MASK_BIAS), MASK_BIAS).astype(BF16)
        if nb < MAX_SEL_BLOCKS:
            bias_ref[0, 0, nb:, :] = jnp.full((MAX_SEL_BLOCKS - nb, tq), MASK_BIAS, BF16)

    per_class = LANES * CMP_STRIDE // tq
    n_class = -(-(nc * CMP_STRIDE // tq) // per_class)
    for cls in range(n_class):
        nk = min(nc, LANES * (cls + 1))
        nb = min(MAX_SEL_BLOCKS, nk * CMP_STRIDE // SEL_BLOCK)
        pl.when(i // per_class == cls)(functools.partial(body, nk, nb))


def _cmp_attn2(qt, gates, kc, vct):
    B, HD, S = qt.shape
    G, dh = N_KV_GROUPS, HEAD_DIM
    gw = HD // G
    nc = kc.shape[2]
    tq = min(Q_TILE, S)
    n0 = jnp.arange(nc)[None, :] * CMP_STRIDE
    j0 = jnp.arange(MAX_SEL_BLOCKS)[:, None] * SEL_BLOCK
    c2s = ((n0 < j0 + SEL_BLOCK) & (n0 + CMP_BLOCK > j0)).astype(BF16)
    return pl.pallas_call(
        functools.partial(_cmp2_kernel, tq=tq, nc=nc),
        out_shape=(jax.ShapeDtypeStruct((B, S, HD), BF16), jax.ShapeDtypeStruct((B, G, MAX_SEL_BLOCKS, S), BF16)),
        grid=(B, G, S // tq),
        in_specs=[
            pl.BlockSpec((1, gw, tq), lambda b, g, i: (b, g, i)),
            pl.BlockSpec((1, tq, LANES), lambda b, g, i: (b, i, 0)),
            pl.BlockSpec((1, 1, nc, dh), lambda b, g, i: (b, g, 0, 0)),
            pl.BlockSpec((1, 1, dh, nc), lambda b, g, i: (b, g, 0, 0)),
            pl.BlockSpec((MAX_SEL_BLOCKS, nc), lambda b, g, i: (0, 0)),
        ],
        out_specs=(pl.BlockSpec((1, tq, gw), lambda b, g, i: (b, i, g)),
                   pl.BlockSpec((1, 1, MAX_SEL_BLOCKS, tq), lambda b, g, i: (b, g, 0, i))),
        compiler_params=_cparams(("parallel", "parallel", "parallel"), 40),
        name="nsa_compressed",
    )(qt, gates, kc, vct, c2s)


def _sel2_kernel(qt_ref, bias_ref, gt_ref, k_ref, vt_ref, o_ref, qa, m_s, acc, s_a, s_b, *, tq, tk):
    g = pl.program_id(1)
    i = pl.program_id(2)
    qt = qt_ref[0]
    bias = bias_ref[0, 0]
    for r in range(HEADS_PER_GROUP):
        cols = slice(r * tq, (r + 1) * tq)
        qa[0:HEAD_DIM, cols] = qt[r * HEAD_DIM:(r + 1) * HEAD_DIM]
        qa[HEAD_DIM:HEAD_DIM + MAX_SEL_BLOCKS, cols] = bias
        qa[HEAD_DIM + MAX_SEL_BLOCKS:, cols] = jnp.zeros((KEY_AUG - HEAD_DIM - MAX_SEL_BLOCKS, tq), BF16)
    m_s[...] = jnp.full(m_s.shape, NEG_BIG, F32)
    acc[...] = jnp.zeros(acc.shape, F32)
    width = HEADS_PER_GROUP * tq

    def qk(jt, buf):
        buf[...] = _dot(k_ref[0, 0, pl.ds(pl.multiple_of(jt * tk, tk), tk), :], qa[...])

    def online_softmax(key0, n_keys, s):
        m_old = m_s[...]
        m_new = jnp.maximum(m_old, jnp.max(s, axis=0, keepdims=True))
        p = jnp.exp2(s - m_new).astype(BF16)
        acc[...] = jnp.exp2(m_old - m_new) * acc[...] + _dot(vt_ref[0, 0, :, pl.ds(key0, n_keys)], p)
        m_s[...] = m_new

    def absorb(jt, buf):
        online_softmax(pl.multiple_of(jt * tk, tk), tk, buf[...])

    def absorb_last(jt, buf):
        own = jnp.where(lax.broadcasted_iota(I32, (tq, 1), 0) <= (lax.broadcasted_iota(I32, (1, width), 1) & (tq - 1)),
                        0.0, NEG_BIG)
        k0 = pl.multiple_of(jt * tk, tk)

        @pl.when(i % 2 == 1)
        def _():
            online_softmax(k0, tq, buf[0:tq, :])
            online_softmax(pl.multiple_of(k0 + tq, tq), tq, buf[tq:tk, :] + own)

        @pl.when(i % 2 == 0)
        def _():
            online_softmax(k0, tq, buf[0:tq, :] + own)

    def pair(u, c):
        qk(2 * u + 1, s_b)
        absorb(2 * u, s_a)
        qk(2 * u + 2, s_a)
        absorb(2 * u + 1, s_b)
        return c

    last = (i * tq + tq + tk - 1) // tk - 1
    qk(0, s_a)
    lax.fori_loop(0, last // 2, pair, 0)

    @pl.when(last % 2 == 1)
    def _():
        qk(last, s_b)
        absorb(last - 1, s_a)
        absorb_last(last, s_b)

    @pl.when(last % 2 == 0)
    def _():
        absorb_last(last, s_a)

    o_ref[0] = _finish_heads(acc[...], gt_ref[0], g, 1, tq, HEAD_DIM).astype(BF16)


def _sel_attn2(qt, bias, gates, ksa, vst):
    B, HD, S = qt.shape
    G = N_KV_GROUPS
    gw = HD // G
    tq = min(Q_TILE, S)
    tk = 2 * tq
    assert S % tk == 0
    width = HEADS_PER_GROUP * tq
    return pl.pallas_call(
        functools.partial(_sel2_kernel, tq=tq, tk=tk),
        out_shape=jax.ShapeDtypeStruct((B, S, HD), BF16),
        grid=(B, G, S // tq),
        in_specs=[
            pl.BlockSpec((1, gw, tq), lambda b, g, i: (b, g, i)),
            pl.BlockSpec((1, 1, MAX_SEL_BLOCKS, tq), lambda b, g, i: (b, g, 0, i)),
            pl.BlockSpec((1, tq, LANES), lambda b, g, i: (b, i, 0)),
            pl.BlockSpec((1, 1, S, KEY_AUG), lambda b, g, i: (b, g, 0, 0)),
            pl.BlockSpec((1, 1, V_ROWS, S), lambda b, g, i: (b, g, 0, 0)),
        ],
        out_specs=pl.BlockSpec((1, tq, gw), lambda b, g, i: (b, i, g)),
        scratch_shapes=[pltpu.VMEM((KEY_AUG, width), BF16), pltpu.VMEM((1, width), F32),
                        pltpu.VMEM((V_ROWS, width), F32),
                        pltpu.VMEM((tk, width), F32), pltpu.VMEM((tk, width), F32)],
        compiler_params=_cparams(("parallel", "parallel", "arbitrary"), 48),
        name="nsa_selected",
    )(qt, bias, gates, ksa, vst)


def _win2_kernel(qt_ref, gt_ref, k_ref, vt_ref, o_ref, *, tq, wb):
    g = pl.program_id(1)
    i = pl.program_id(2)
    k0 = pl.multiple_of(jnp.maximum(i * tq - WINDOW, 0), tq)
    s = _dot(k_ref[0, 0, pl.ds(k0, wb), :], _heads_on_lanes(qt_ref[0]))
    t = _lane_query_pos(i, tq)
    kpos = k0 + lax.broadcasted_iota(I32, (wb, 1), 0)
    s = jnp.where(kpos <= t, jnp.where(kpos > t - WINDOW, s, -jnp.inf), -jnp.inf)
    m = jnp.max(s, axis=0, keepdims=True)
    p = jnp.exp2(s - m).astype(BF16)
    acc = _dot(vt_ref[0, 0, :, pl.ds(k0, wb)], p)
    o_ref[0] = _finish_heads(acc, gt_ref[0], g, 2, tq, HEAD_DIM).astype(BF16)


def _win_attn2(qt, gates, kw, vwt):
    B, HD, S = qt.shape
    G, dh = N_KV_GROUPS, HEAD_DIM
    gw = HD // G
    tq = min(Q_TILE, S)
    wb = WINDOW + tq
    assert S >= wb and tq % LANES == 0
    return pl.pallas_call(
        functools.partial(_win2_kernel, tq=tq, wb=wb),
        out_shape=jax.ShapeDtypeStruct((B, S, HD), BF16),
        grid=(B, G, S // tq),
        in_specs=[
            pl.BlockSpec((1, gw, tq), lambda b, g, i: (b, g, i)),
            pl.BlockSpec((1, tq, LANES), lambda b, g, i: (b, i, 0)),
            pl.BlockSpec((1, 1, S, dh), lambda b, g, i: (b, g, 0, 0)),
            pl.BlockSpec((1, 1, V_ROWS, S), lambda b, g, i: (b, g, 0, 0)),
        ],
        out_specs=pl.BlockSpec((1, tq, gw), lambda b, g, i: (b, i, g)),
        compiler_params=_cparams(("parallel", "parallel", "parallel"), 40),
        name="nsa_window",
    )(qt, gates, kw, vwt)


def _oproj_kernel(h_ref, a_ref, b_ref, c_ref, w_ref, o_ref):
    o = a_ref[...].astype(F32) + b_ref[...].astype(F32) + c_ref[...].astype(F32)
    o_ref[...] = h_ref[...] + _dot(o.astype(BF16), w_ref[...])


def _out_proj(h2, oc, os_, ow, w_o):
    T, D = h2.shape
    HD = oc.shape[-1]
    tt = min(SEQ_TILE, T)
    blk = lambda w: pl.BlockSpec((tt, w), lambda i: (i, 0))
    return pl.pallas_call(
        _oproj_kernel,
        out_shape=jax.ShapeDtypeStruct((T, D), F32),
        grid=(T // tt,),
        in_specs=[blk(D), blk(HD), blk(HD), blk(HD), pl.BlockSpec((HD, D), lambda i: (0, 0))],
        out_specs=blk(D),
        compiler_params=_cparams(("parallel",), 40),
        name="nsa_out_proj",
    )(h2, oc, os_, ow, w_o.astype(BF16))


def _nsa_layer(h, g, w_qg, b_gate, w_o, shared):
    B, S, D = h.shape
    kc, vct, ksa, vst, kw, vwt = shared
    qt, gates = _q_proj2(h, g, w_qg, b_gate)
    oc, bias = _cmp_attn2(qt, gates, kc, vct)
    os_ = _sel_attn2(qt, bias, gates, ksa, vst)
    ow = _win_attn2(qt, gates, kw, vwt)
    flat = lambda a: a.reshape(B * S, a.shape[-1])
    return _out_proj(flat(h), flat(oc), flat(os_), flat(ow), w_o).reshape(B, S, D)


def _shared_kv(h, kv_norm, w_kv, ck, cv):
    B, S, _ = h.shape
    G, dh = N_KV_GROUPS, HEAD_DIM
    assert S % SEL_BLOCK == 0 and S // SEL_BLOCK <= MAX_SEL_BLOCKS
    cvals, ksa, vst, kw, vwt = _shared_kv_proj2(h, kv_norm, w_kv)
    halves = cvals.reshape(B, S, 2, G, dh).transpose(2, 0, 3, 1, 4).reshape(2, B, G, S // CMP_STRIDE, CMP_STRIDE * dh)
    kc = _compress2(halves, 0, *ck, keys=True)
    vct = _compress2(halves, 1, *cv, keys=False)
    return kc, vct, ksa, vst, kw, vwt


def kernel(x, p, norm_mix, norm_ffn, norm_ple, pool_w, pool_b, pool_scale, kv_norm, w_kv, cmp_k_pos, cmp_k_w1, cmp_k_b1, cmp_k_w2, cmp_k_b2, cmp_v_pos, cmp_v_w1, cmp_v_b1, cmp_v_w2, cmp_v_b2, w_qg, b_gate, w_o, router_g_w, router_g_b, router_e_w, router_e_b, moe_w1, moe_w3, moe_w2, ple_proj, ple_gate_w, ple_gate_b, final_norm):
    B, S, D = x.shape
    depth = p.shape[0]
    n_a = pool_w.shape[0]
    T = B * S
    h = x
    shared = None
    for i in range(depth):
        if i == n_a:
            shared = _shared_kv(h, kv_norm, w_kv,
                                (cmp_k_pos, cmp_k_w1, cmp_k_b1, cmp_k_w2, cmp_k_b2),
                                (cmp_v_pos, cmp_v_w1, cmp_v_b1, cmp_v_w2, cmp_v_b2))
        if i < n_a:
            h = _pool_layer(h, norm_mix[i], pool_w[i], pool_b[i], pool_scale[i])
        else:
            j = i - n_a
            h = _nsa_layer(h, norm_mix[i], w_qg[j], b_gate[j], w_o[j], shared)
        h = _moe_ple_layer(h.reshape(T, D), p.reshape(depth, T, p.shape[-1]), i, norm_ffn[i], router_g_w[i],
                           router_g_b[i], router_e_w[i], router_e_b[i], moe_w1, moe_w3, moe_w2, norm_ple[i],
                           ple_gate_w[i], ple_gate_b[i], ple_proj[i], final_norm, i == depth - 1).reshape(B, S, D)
    return h
```

```python
import functools

import jax
import jax.numpy as jnp
from jax import lax
from jax.experimental import pallas as pl
from jax.experimental.pallas import tpu as pltpu

F32 = jnp.float32
BF16 = jnp.bfloat16
I32 = jnp.int32

POOL_WINDOWS = (2, 4, 8, 16)
N_HEADS = 16
HEAD_DIM = 64
N_KV_GROUPS = 4
HEADS_PER_GROUP = N_HEADS // N_KV_GROUPS
N_BRANCH = 3
ROPE_DIMS = HEAD_DIM // 4
ROPE_HALF = ROPE_DIMS // 2
ROPE_THETA = 500000.0
CMP_BLOCK = 32
CMP_STRIDE = 16
SEL_BLOCK = 64
N_SELECT = 16
WINDOW = 512
FORCE_BONUS = 1e4
N_EXPERT_GROUPS = 4
EXPERTS_PER_GROUP = 8
N_EXPERTS = N_EXPERT_GROUPS * EXPERTS_PER_GROUP
TOP_K_IN_GROUP = 2
RMS_EPS = 1e-6

LANES = 128
MAX_SEL_BLOCKS = LANES
MASK_BIAS = -30000.0

SEQ_TILE = 512
TOK_TILE = 512
CMB_TILE = 256
MOE_CHUNK = 512
Q_TILE = 256
SEL_Q_TILE = 512
KEY_TILE = 512
HALO = 16

HIGHEST = lax.Precision.HIGHEST


def _cparams(sem, vmem_mb):
    return pltpu.CompilerParams(dimension_semantics=sem, vmem_limit_bytes=vmem_mb * 1024 * 1024)


def _rms(x, g):
    return x * lax.rsqrt(jnp.mean(x * x, axis=-1, keepdims=True) + RMS_EPS) * g


def _dot(a, b):
    return jnp.dot(a, b, preferred_element_type=F32)


def _dot_nt(a, b):
    return lax.dot_general(a, b, (((1,), (1,)), ((), ())), preferred_element_type=F32)


def _pool_kernel(h_ref, halo_ref, g_ref, w_ref, b_ref, sc_ref, o_ref, *, ts, cg):
    i = pl.program_id(1)
    x = h_ref[0]
    g = g_ref[...]
    xn = _rms(x, g)
    hn = _rms(halo_ref[0], g)
    hn = jnp.where(i > 0, hn, 0.0)
    ext = jnp.concatenate([hn, xn], axis=0)
    t = i * ts + lax.broadcasted_iota(I32, (ts, 1), 0)
    outs = []
    for gi, w in enumerate(POOL_WINDOWS):
        s = ext[:, gi * cg:(gi + 1) * cg]
        k = 1
        while k < w:
            s = s + pltpu.roll(s, k, axis=0)
            k *= 2
        cnt = jnp.minimum(t + 1, w).astype(F32)
        pooled = s[HALO:] / cnt - xn[:, gi * cg:(gi + 1) * cg]
        outs.append(_dot(pooled.astype(BF16), w_ref[gi]))
    y = jnp.concatenate(outs, axis=-1)
    o_ref[0] = x + (y + b_ref[...]) * sc_ref[...]


def _pool_layer(h, g, w, b, sc):
    B, S, D = h.shape
    ts = min(SEQ_TILE, S)
    cg = D // len(POOL_WINDOWS)
    row = lambda v: v.reshape(1, D)
    return pl.pallas_call(
        functools.partial(_pool_kernel, ts=ts, cg=cg),
        out_shape=jax.ShapeDtypeStruct((B, S, D), F32),
        grid=(B, S // ts),
        in_specs=[
            pl.BlockSpec((1, ts, D), lambda b_, i: (b_, i, 0)),
            pl.BlockSpec((1, HALO, D), lambda b_, i: (b_, jnp.maximum(i * (ts // HALO) - 1, 0), 0)),
            pl.BlockSpec((1, D), lambda b_, i: (0, 0)),
            pl.BlockSpec((len(POOL_WINDOWS), cg, cg), lambda b_, i: (0, 0, 0)),
            pl.BlockSpec((1, D), lambda b_, i: (0, 0)),
            pl.BlockSpec((1, D), lambda b_, i: (0, 0)),
        ],
        out_specs=pl.BlockSpec((1, ts, D), lambda b_, i: (b_, i, 0)),
        compiler_params=_cparams(("parallel", "parallel"), 40),
        name="pool_mixer",
    )(h, h, row(g), w.astype(BF16), row(b), row(sc))


def _router_kernel(h_ref, g_ref, wh_ref, wl_ref, b_ref, info_ref, cnt_ref, *, tt):
    i = pl.program_id(0)

    @pl.when(i == 0)
    def _():
        cnt_ref[...] = jnp.zeros_like(cnt_ref)

    xn = _rms(h_ref[...], g_ref[...])
    xh = xn.astype(BF16)
    xl = (xn - xh.astype(F32)).astype(BF16)
    logits = _dot(xh, wh_ref[...]) + (_dot(xh, wl_ref[...]) + _dot(xl, wh_ref[...])) + b_ref[...]
    lane = lax.broadcasted_iota(I32, (tt, LANES), 1)
    neg = -jnp.inf
    gl = jnp.where(lane < N_EXPERT_GROUPS, logits, neg)
    gmax = jnp.max(gl, axis=-1, keepdims=True)
    grp = jnp.min(jnp.where(gl == gmax, lane, LANES), axis=-1, keepdims=True)
    gprob = 1.0 / jnp.sum(jnp.exp(gl - gmax), axis=-1, keepdims=True)
    lo = N_EXPERT_GROUPS + grp * EXPERTS_PER_GROUP
    el = jnp.where(lane >= lo, jnp.where(lane < lo + EXPERTS_PER_GROUP, logits, neg), neg)
    v1 = jnp.max(el, axis=-1, keepdims=True)
    i1 = jnp.min(jnp.where(el == v1, lane, LANES), axis=-1, keepdims=True)
    el2 = jnp.where(lane == i1, neg, el)
    v2 = jnp.max(el2, axis=-1, keepdims=True)
    i2 = jnp.min(jnp.where(el2 == v2, lane, LANES), axis=-1, keepdims=True)
    e2 = jnp.exp(v2 - v1)
    w1 = gprob / (1.0 + e2)
    w2 = gprob * e2 / (1.0 + e2)
    oh1 = lane == i1
    oh2 = lane == i2
    oh = jnp.where(oh1, 1.0, jnp.where(oh2, 1.0, 0.0))
    r_ = lax.broadcasted_iota(I32, (tt, tt), 0)
    c_ = lax.broadcasted_iota(I32, (tt, tt), 1)
    tri = jnp.where(r_ > c_, 1.0, 0.0).astype(BF16)
    tot = _dot(tri, oh.astype(BF16)) + cnt_ref[...]
    r1 = jnp.sum(jnp.where(oh1, tot, 0.0), axis=-1, keepdims=True)
    r2 = jnp.sum(jnp.where(oh2, tot, 0.0), axis=-1, keepdims=True)
    cnt_ref[...] = cnt_ref[...] + jnp.sum(oh, axis=0, keepdims=True)
    vals = (i1.astype(F32) - N_EXPERT_GROUPS, i2.astype(F32) - N_EXPERT_GROUPS, w1, w2, r1, r2)
    info = jnp.zeros((tt, LANES), F32)
    for k, v in enumerate(vals):
        info = jnp.where(lane == k, v, info)
    info_ref[...] = info


def _router(h2, g, wr, br):
    T, D = h2.shape
    tt = min(TOK_TILE, T)
    return pl.pallas_call(
        functools.partial(_router_kernel, tt=tt),
        out_shape=(jax.ShapeDtypeStruct((T, LANES), F32), jax.ShapeDtypeStruct((1, LANES), F32)),
        grid=(T // tt,),
        in_specs=[
            pl.BlockSpec((tt, D), lambda i: (i, 0)),
            pl.BlockSpec((1, D), lambda i: (0, 0)),
            pl.BlockSpec((D, LANES), lambda i: (0, 0)),
            pl.BlockSpec((D, LANES), lambda i: (0, 0)),
            pl.BlockSpec((1, LANES), lambda i: (0, 0)),
        ],
        out_specs=(pl.BlockSpec((tt, LANES), lambda i: (i, 0)), pl.BlockSpec((1, LANES), lambda i: (0, 0))),
        compiler_params=_cparams(("arbitrary",), 40),
        name="moe_router",
    )(h2, g.reshape(1, D), wr.astype(BF16), (wr - wr.astype(BF16).astype(F32)).astype(BF16), br)


def _row_copy(src, s, dst, d, sem):
    return pltpu.make_async_copy(src.at[pl.ds(s, 1)], dst.at[pl.ds(d, 1)], sem)


def _dispatch_kernel(dest_ref, cend_ref, h_ref, g_ref, xs_ref, xn_s, sem, *, tt, n_tok, n_chunks):
    base = pl.program_id(0) * tt

    @pl.when(pl.program_id(0) == 0)
    def _():
        xn_s[...] = jnp.zeros_like(xn_s)

        def tail(e):
            nonempty = cend_ref[e] > (cend_ref[e - 1] if e > 0 else 0)
            row = pl.multiple_of((cend_ref[e] - 1) * tt, tt)
            return nonempty, pltpu.make_async_copy(xn_s, xs_ref.at[pl.ds(row, tt)], sem)

        def unused(c):
            return pltpu.make_async_copy(xn_s, xs_ref.at[pl.ds(pl.multiple_of(c * tt, tt), tt)], sem)

        n_used = cend_ref[N_EXPERTS - 1]
        for e in range(N_EXPERTS):
            nonempty, cp = tail(e)
            pl.when(nonempty)(cp.start)
        lax.fori_loop(n_used, n_chunks, lambda c, z: (unused(c).start(), z)[1], 0)
        for e in range(N_EXPERTS):
            nonempty, cp = tail(e)
            pl.when(nonempty)(cp.wait)
        lax.fori_loop(n_used, n_chunks, lambda c, z: (unused(c).wait(), z)[1], 0)

    xn_s[...] = _rms(h_ref[...], g_ref[...])

    def issue(r, c):
        for k in range(TOP_K_IN_GROUP):
            _row_copy(xn_s, r, xs_ref, dest_ref[k * n_tok + base + r], sem).start(priority=k % 2)
        return c

    lax.fori_loop(0, tt, issue, 0, unroll=8)
    for k in range(TOP_K_IN_GROUP):
        pltpu.make_async_copy(xn_s, xs_ref.at[pl.ds(0, tt)], sem).wait()


def _dispatch(dest, chunk_end, h2, g, n_rows):
    T, D = h2.shape
    tt = MOE_CHUNK
    assert T % tt == 0
    return pl.pallas_call(
        functools.partial(_dispatch_kernel, tt=tt, n_tok=T, n_chunks=n_rows // tt),
        out_shape=jax.ShapeDtypeStruct((n_rows, D), F32),
        grid_spec=pltpu.PrefetchScalarGridSpec(
            num_scalar_prefetch=2,
            grid=(T // tt,),
            in_specs=[
                pl.BlockSpec((tt, D), lambda i, d, ce: (i, 0)),
                pl.BlockSpec((1, D), lambda i, d, ce: (0, 0)),
            ],
            out_specs=pl.BlockSpec(memory_space=pl.ANY),
            scratch_shapes=[pltpu.VMEM((tt, D), F32), pltpu.SemaphoreType.DMA],
        ),
        compiler_params=_cparams(("arbitrary",), 40),
        name="moe_dispatch",
    )(dest, chunk_end, h2, g.reshape(1, D))


def _expert_kernel(ce_ref, nv_ref, xs_ref, w1_ref, w3_ref, w2_ref, o_ref, w1b, w3b, w2b):
    c = pl.program_id(0)
    e = ce_ref[c]
    prev = ce_ref[jnp.maximum(c - 1, 0)]

    @pl.when(jnp.logical_or(c == 0, e != prev))
    def _():
        w1b[...] = w1_ref[0, 0].astype(BF16)
        w3b[...] = w3_ref[0, 0].astype(BF16)
        w2b[...] = w2_ref[0, 0].astype(BF16)

    @pl.when(c < nv_ref[0])
    def _():
        x = xs_ref[...].astype(BF16)
        a = _dot(x, w1b[...])
        b = _dot(x, w3b[...])
        hc = a * jax.nn.sigmoid(a) * b
        o_ref[...] = _dot(hc.astype(BF16), w2b[...])

    @pl.when(c >= nv_ref[0])
    def _():
        o_ref[...] = jnp.zeros_like(o_ref)


def _experts(chunk_e, n_valid, xs, w1, w3, w2, layer):
    P, D = xs.shape
    F = w1.shape[-1]
    ch = MOE_CHUNK
    rows = lambda c, ce, nv: (jnp.minimum(c, nv[0] - 1), 0)
    wsel = lambda c, ce, nv: (layer, ce[c], 0, 0)
    return pl.pallas_call(
        _expert_kernel,
        out_shape=jax.ShapeDtypeStruct((P, D), F32),
        grid_spec=pltpu.PrefetchScalarGridSpec(
            num_scalar_prefetch=2,
            grid=(P // ch,),
            in_specs=[
                pl.BlockSpec((ch, D), rows),
                pl.BlockSpec((1, 1, D, F), wsel),
                pl.BlockSpec((1, 1, D, F), wsel),
                pl.BlockSpec((1, 1, F, D), wsel),
            ],
            out_specs=pl.BlockSpec((ch, D), lambda c, ce, nv: (c, 0)),
            scratch_shapes=[pltpu.VMEM((D, F), BF16), pltpu.VMEM((D, F), BF16), pltpu.VMEM((F, D), BF16)],
        ),
        compiler_params=_cparams(("arbitrary",), 56),
        name="moe_experts",
    )(chunk_e, n_valid, xs, w1, w3, w2)


def _combine_kernel(dest_ref, h_ref, info_ref, rows_ref, p_ref, g_ref, gw_ref, gb_ref, pw_ref, fn_ref,
                    o_ref, buf_a, buf_b, sem, *, tt, n_tok, n_steps, final):
    i = pl.program_id(0)
    bufs = (buf_a, buf_b)

    def row(tile, sl, k, r):
        return _row_copy(rows_ref, dest_ref[k * n_tok + tile * tt + r], bufs[sl].at[k], r, sem.at[sl])

    def wait_rows(sl):
        for k in range(TOP_K_IN_GROUP):
            pltpu.make_async_copy(rows_ref.at[pl.ds(0, tt)], bufs[sl].at[k], sem.at[sl]).wait()

    @pl.when(i == 0)
    def _():
        def issue(r, c):
            for k in range(TOP_K_IN_GROUP):
                row(0, 0, k, r).start(priority=k % 2)
            return c

        lax.fori_loop(0, tt, issue, 0, unroll=8)

    def step(sl):
        wait_rows(sl)
        nxt = jnp.minimum(i + 1, n_steps - 1)
        for r in range(tt):
            for k in range(TOP_K_IN_GROUP):
                row(nxt, 1 - sl, k, r).start(priority=k % 2)
        info = info_ref[...]
        y = h_ref[...] + info[:, 2:3] * bufs[sl][0] + info[:, 3:4] * bufs[sl][1]
        hn = _rms(y, g_ref[...])
        gate = jax.nn.sigmoid(_dot(hn.astype(BF16), gw_ref[...]) + gb_ref[...])
        out = y + _dot(p_ref[0].astype(BF16), pw_ref[...]) * gate
        if final:
            out = _rms(out, fn_ref[...])
        o_ref[...] = out
        pl.when(i == n_steps - 1)(functools.partial(wait_rows, 1 - sl))

    for parity in range(2):
        pl.when(i % 2 == parity)(functools.partial(step, parity))


def _combine(dest, h2, info, rows, p3, layer, g, gw, gb, pw, fn, final):
    T, D = h2.shape
    PD = p3.shape[-1]
    tt = min(CMB_TILE, T)
    full = lambda i, d: (0, 0)
    return pl.pallas_call(
        functools.partial(_combine_kernel, tt=tt, n_tok=T, n_steps=T // tt, final=final),
        out_shape=jax.ShapeDtypeStruct((T, D), F32),
        grid_spec=pltpu.PrefetchScalarGridSpec(
            num_scalar_prefetch=1,
            grid=(T // tt,),
            in_specs=[
                pl.BlockSpec((tt, D), lambda i, d: (i, 0)),
                pl.BlockSpec((tt, LANES), lambda i, d: (i, 0)),
                pl.BlockSpec(memory_space=pl.ANY),
                pl.BlockSpec((1, tt, PD), lambda i, d: (layer, i, 0)),
                pl.BlockSpec((1, D), full),
                pl.BlockSpec((D, D), full),
                pl.BlockSpec((1, D), full),
                pl.BlockSpec((PD, D), full),
                pl.BlockSpec((1, D), full),
            ],
            out_specs=pl.BlockSpec((tt, D), lambda i, d: (i, 0)),
            scratch_shapes=[pltpu.VMEM((TOP_K_IN_GROUP, tt, D), F32), pltpu.VMEM((TOP_K_IN_GROUP, tt, D), F32),
                            pltpu.SemaphoreType.DMA((2,))],
        ),
        compiler_params=_cparams(("arbitrary",), 40),
        name="moe_combine_ple",
    )(dest, h2, info, rows, p3, g.reshape(1, D), gw.astype(BF16), gb.reshape(1, D), pw.astype(BF16),
      fn.reshape(1, D))


IDX_SLOTS = 3


def _expert2_kernel(ce_ref, nv_ref, idx_hbm, h_hbm, g_ref, w1_ref, w3_ref, w2_ref, y_hbm,
                    idx_s, x_a, x_b, o_a, o_b, w1b, w3b, w2b, idx_sem, g_sem, s_sem, *, ch, n_tok, n_chunks):
    c = pl.program_id(0)
    nv = nv_ref[0]
    x_buf = (x_a, x_b)
    o_buf = (o_a, o_b)

    def idx_copy(chunk, sl):
        return pltpu.make_async_copy(idx_hbm.at[chunk], idx_s.at[sl], idx_sem.at[sl])

    def gather(chunk_slot, sl, r):
        return pltpu.make_async_copy(h_hbm.at[pl.ds(idx_s[chunk_slot, 0, r], 1)], x_buf[sl].at[pl.ds(r, 1)],
                                     g_sem.at[sl])

    def scatter(chunk_slot, sl, r):
        return pltpu.make_async_copy(o_buf[sl].at[pl.ds(r, 1)], y_hbm.at[pl.ds(idx_s[chunk_slot, 1, r], 1)],
                                     s_sem.at[sl])

    def wait_gathers(sl):
        pltpu.make_async_copy(h_hbm.at[pl.ds(0, ch)], x_buf[sl], g_sem.at[sl]).wait()

    def wait_scatters(sl):
        pltpu.make_async_copy(o_buf[sl], y_hbm.at[pl.ds(0, ch)], s_sem.at[sl]).wait()

    @pl.when(c == 0)
    def _():
        o_a[...] = jnp.zeros_like(o_a)
        o_b[...] = jnp.zeros_like(o_b)
        pltpu.make_async_copy(o_a, y_hbm.at[pl.ds(2 * n_tok, ch)], s_sem.at[0]).start()
        first = idx_copy(0, 0)
        first.start()
        first.wait()
        lax.fori_loop(0, ch, lambda r, z: (gather(0, 0, r).start(), z)[1], 0, unroll=8)
        null = idx_copy(n_chunks, IDX_SLOTS - 1)
        null.start()
        null.wait()
        idx_copy(jnp.minimum(1, n_chunks), 1).start()

    e = ce_ref[c]
    prev = ce_ref[jnp.maximum(c - 1, 0)]

    @pl.when(jnp.logical_or(c == 0, e != prev))
    def _():
        w1b[...] = w1_ref[0, 0].astype(BF16)
        w3b[...] = w3_ref[0, 0].astype(BF16)
        w2b[...] = w2_ref[0, 0].astype(BF16)

    def chunk_step(slot):
        other = 1 - slot
        cur3 = c % IDX_SLOTS
        nxt3 = (c + 1) % IDX_SLOTS
        prv3 = (c + IDX_SLOTS - 1) % IDX_SLOTS
        wait_scatters(slot)
        wait_gathers(slot)
        idx_copy(0, nxt3).wait()
        for r in range(ch):
            gather(nxt3, other, r).start()
            scatter(prv3, other, r).start()
        x = x_buf[slot][...]
        x = (x * lax.rsqrt(jnp.mean(x * x, axis=-1, keepdims=True) + RMS_EPS) * g_ref[...]).astype(BF16)
        a = _dot(x, w1b[...])
        b = _dot(x, w3b[...])
        hc = a * jax.nn.sigmoid(a) * b
        o_buf[slot][...] = _dot(hc.astype(BF16), w2b[...])
        idx_copy(jnp.minimum(c + 2, n_chunks), prv3).start()

        @pl.when(c == nv - 1)
        def _():
            lax.fori_loop(0, ch, lambda r, z: (scatter(cur3, slot, r).start(), z)[1], 0, unroll=8)
            wait_scatters(other)
            wait_scatters(slot)
            wait_gathers(other)
            idx_copy(0, prv3).wait()

    for parity in range(2):
        pl.when(jnp.logical_and(c < nv, c % 2 == parity))(functools.partial(chunk_step, parity))


def _experts2(chunk_e, n_valid, idx, h2, g, w1, w3, w2, layer):
    T, D = h2.shape
    F = w1.shape[-1]
    ch = MOE_CHUNK
    n_chunks = idx.shape[0] - 1
    wsel = lambda c, ce, nv: (layer, ce[c], 0, 0)
    return pl.pallas_call(
        functools.partial(_expert2_kernel, ch=ch, n_tok=T, n_chunks=n_chunks),
        out_shape=jax.ShapeDtypeStruct((TOP_K_IN_GROUP * T + 2 * ch, D), F32),
        grid_spec=pltpu.PrefetchScalarGridSpec(
            num_scalar_prefetch=2,
            grid=(n_chunks,),
            in_specs=[
                pl.BlockSpec(memory_space=pl.ANY),
                pl.BlockSpec(memory_space=pl.ANY),
                pl.BlockSpec((1, D), lambda c, ce, nv: (0, 0)),
                pl.BlockSpec((1, 1, D, F), wsel),
                pl.BlockSpec((1, 1, D, F), wsel),
                pl.BlockSpec((1, 1, F, D), wsel),
            ],
            out_specs=pl.BlockSpec(memory_space=pl.ANY),
            scratch_shapes=[
                pltpu.SMEM((IDX_SLOTS, 2, ch), I32),
                pltpu.VMEM((ch, D), F32), pltpu.VMEM((ch, D), F32),
                pltpu.VMEM((ch, D), F32), pltpu.VMEM((ch, D), F32),
                pltpu.VMEM((D, F), BF16), pltpu.VMEM((D, F), BF16), pltpu.VMEM((F, D), BF16),
                pltpu.SemaphoreType.DMA((IDX_SLOTS,)),
                pltpu.SemaphoreType.DMA((2,)),
                pltpu.SemaphoreType.DMA((2,)),
            ],
        ),
        compiler_params=_cparams(("arbitrary",), 56),
        name="moe_experts",
    )(chunk_e, n_valid, idx, h2, g.reshape(1, D), w1, w3, w2)


def _combine2_kernel(h_ref, info_ref, ya_ref, yb_ref, p_ref, g_ref, gw_ref, gb_ref, pw_ref, fn_ref, o_ref, *, final):
    info = info_ref[...]
    y = h_ref[...] + info[:, 2:3] * ya_ref[...] + info[:, 3:4] * yb_ref[...]
    hn = _rms(y, g_ref[...])
    gate = jax.nn.sigmoid(_dot(hn.astype(BF16), gw_ref[...]) + gb_ref[...])
    out = y + _dot(p_ref[0].astype(BF16), pw_ref[...]) * gate
    if final:
        out = _rms(out, fn_ref[...])
    o_ref[...] = out


def _combine2(h2, info, y2, p3, layer, g, gw, gb, pw, fn, final):
    T, D = h2.shape
    PD = p3.shape[-1]
    tt = min(SEQ_TILE, T)
    full = lambda i: (0, 0)
    return pl.pallas_call(
        functools.partial(_combine2_kernel, final=final),
        out_shape=jax.ShapeDtypeStruct((T, D), F32),
        grid=(T // tt,),
        in_specs=[
            pl.BlockSpec((tt, D), lambda i: (i, 0)),
            pl.BlockSpec((tt, LANES), lambda i: (i, 0)),
            pl.BlockSpec((tt, D), lambda i: (i, 0)),
            pl.BlockSpec((tt, D), lambda i: (T // tt + i, 0)),
            pl.BlockSpec((1, tt, PD), lambda i: (layer, i, 0)),
            pl.BlockSpec((1, D), full),
            pl.BlockSpec((D, D), full),
            pl.BlockSpec((1, D), full),
            pl.BlockSpec((PD, D), full),
            pl.BlockSpec((1, D), full),
        ],
        out_specs=pl.BlockSpec((tt, D), lambda i: (i, 0)),
        compiler_params=_cparams(("parallel",), 48),
        name="moe_combine_ple",
    )(h2, info, y2, y2, p3, g.reshape(1, D), gw.astype(BF16), gb.reshape(1, D), pw.astype(BF16), fn.reshape(1, D))


def _moe_ple_layer(h2, p3, layer, norm_ffn, rg_w, rg_b, re_w, re_b, w1, w3, w2, norm_ple, gate_w, gate_b, ple_proj,
                   final_norm, final):
    T, D = h2.shape
    A = T * TOP_K_IN_GROUP
    pad = LANES - N_EXPERT_GROUPS - N_EXPERTS
    wr = jnp.concatenate([rg_w, re_w, jnp.zeros((D, pad), F32)], axis=1)
    br = jnp.concatenate([rg_b, re_b, jnp.zeros((pad,), F32)]).reshape(1, LANES)
    info, cnt = _router(h2, norm_ffn, wr, br)
    counts = cnt[0, N_EXPERT_GROUPS:N_EXPERT_GROUPS + N_EXPERTS].astype(I32)
    n_chunks_e = (counts + MOE_CHUNK - 1) // MOE_CHUNK
    chunk_end = jnp.cumsum(n_chunks_e)
    pstarts = (chunk_end - n_chunks_e) * MOE_CHUNK
    n_chunks = -(-A // MOE_CHUNK) + N_EXPERTS
    n_valid = chunk_end[-1:].astype(I32)
    cidx = jnp.minimum(jnp.arange(n_chunks, dtype=I32), n_valid[0] - 1)
    chunk_e = jnp.sum((chunk_end[None, :] <= cidx[:, None]).astype(I32), axis=1)
    chunk_e = jnp.minimum(chunk_e, N_EXPERTS - 1)
    e_idx = info[:, 0:TOP_K_IN_GROUP].astype(I32)
    rank = info[:, 4:4 + TOP_K_IN_GROUP].astype(I32)
    dest = (pstarts[e_idx] + rank).T.reshape(A)
    xs = _dispatch(dest, chunk_end.astype(I32), h2, norm_ffn, n_chunks * MOE_CHUNK)
    rows = _experts(chunk_e, n_valid, xs, w1, w3, w2, layer)
    return _combine(dest, h2, info, rows, p3, layer, norm_ple, gate_w, gate_b, ple_proj, final_norm, final)


def _rope_rows(xt, cos, sin):
    x1 = xt[0:ROPE_HALF]
    x2 = xt[ROPE_HALF:ROPE_DIMS]
    return jnp.concatenate([x1 * cos - x2 * sin, x2 * cos + x1 * sin, xt[ROPE_DIMS:]], axis=0)


def _kv_kernel(h_ref, g_ref, wn_ref, wt_ref, cos_ref, sin_ref, cv_ref, kst_ref, vs_ref, kwt_ref, vw_ref, *, ts):
    i = pl.program_id(1)
    kvw = N_KV_GROUPS * HEAD_DIM
    hn = _rms(h_ref[0], g_ref[...]).astype(BF16)
    nat = _dot(hn, wn_ref[...])
    cv_ref[0] = nat[:, 0:2 * kvw]
    tr = _dot_nt(wt_ref[...], hn)
    cos = cos_ref[...]
    sin = sin_ref[...]
    blk = lax.broadcasted_iota(I32, (MAX_SEL_BLOCKS, ts), 0)
    pos = i * ts + lax.broadcasted_iota(I32, (MAX_SEL_BLOCKS, ts), 1)
    onehot = jnp.where(pos // SEL_BLOCK == blk, 1.0, 0.0).astype(BF16)
    for g in range(N_KV_GROUPS):
        c0 = g * HEAD_DIM
        kst_ref[0, g, 0:MAX_SEL_BLOCKS, :] = onehot
        kst_ref[0, g, MAX_SEL_BLOCKS:, :] = _rope_rows(tr[c0:c0 + HEAD_DIM], cos, sin).astype(BF16)
        kwt_ref[0, g] = _rope_rows(tr[kvw + c0:kvw + c0 + HEAD_DIM], cos, sin).astype(BF16)
        vs_ref[0, g] = nat[:, 2 * kvw + c0:2 * kvw + c0 + HEAD_DIM].astype(BF16)
        vw_ref[0, g] = nat[:, 3 * kvw + c0:3 * kvw + c0 + HEAD_DIM].astype(BF16)


def _rope_tables(pos):
    inv = jnp.float32(ROPE_THETA) ** (-jnp.arange(ROPE_HALF, dtype=F32) * 2.0 / ROPE_DIMS)
    ang = pos.astype(F32)[:, None] * inv[None, :]
    return jnp.cos(ang), jnp.sin(ang)


def _shared_kv_proj(h, kv_norm, w_kv):
    B, S, D = h.shape
    G, dh = N_KV_GROUPS, HEAD_DIM
    kvw = G * dh
    ts = min(SEQ_TILE, S)
    br = lambda k: w_kv[:, k * kvw:(k + 1) * kvw]
    w_nat = jnp.concatenate([br(0), br(1), br(3), br(5)], axis=1).astype(BF16)
    w_tr = jnp.concatenate([br(2), br(4)], axis=1).T.astype(BF16)
    cos, sin = _rope_tables(jnp.arange(S))
    kd = MAX_SEL_BLOCKS + dh
    return pl.pallas_call(
        functools.partial(_kv_kernel, ts=ts),
        out_shape=(
            jax.ShapeDtypeStruct((B, S, 2 * kvw), F32),
            jax.ShapeDtypeStruct((B, G, kd, S), BF16),
            jax.ShapeDtypeStruct((B, G, S, dh), BF16),
            jax.ShapeDtypeStruct((B, G, dh, S), BF16),
            jax.ShapeDtypeStruct((B, G, S, dh), BF16),
        ),
        grid=(B, S // ts),
        in_specs=[
            pl.BlockSpec((1, ts, D), lambda b, i: (b, i, 0)),
            pl.BlockSpec((1, D), lambda b, i: (0, 0)),
            pl.BlockSpec((D, 4 * kvw), lambda b, i: (0, 0)),
            pl.BlockSpec((2 * kvw, D), lambda b, i: (0, 0)),
            pl.BlockSpec((ROPE_HALF, ts), lambda b, i: (0, i)),
            pl.BlockSpec((ROPE_HALF, ts), lambda b, i: (0, i)),
        ],
        out_specs=(
            pl.BlockSpec((1, ts, 2 * kvw), lambda b, i: (b, i, 0)),
            pl.BlockSpec((1, G, kd, ts), lambda b, i: (b, 0, 0, i)),
            pl.BlockSpec((1, G, ts, dh), lambda b, i: (b, 0, i, 0)),
            pl.BlockSpec((1, G, dh, ts), lambda b, i: (b, 0, 0, i)),
            pl.BlockSpec((1, G, ts, dh), lambda b, i: (b, 0, i, 0)),
        ),
        compiler_params=_cparams(("parallel", "parallel"), 48),
        name="shared_kv_proj",
    )(h, kv_norm.reshape(1, D), w_nat, w_tr, cos.T, sin.T)


def _compress_kernel(x_ref, pos_ref, w1_ref, b1_ref, w2_ref, b2_ref, cos_ref, sin_ref, o_ref, *, nh, keys):
    x = x_ref[0, 0, 0]
    a = _dot((x + pos_ref[0:1]).astype(BF16), w1_ref[0])
    b = _dot((x + pos_ref[1:2]).astype(BF16), w1_ref[1])
    hid = jax.nn.gelu(a + pltpu.roll(b, nh - 1, axis=0) + b1_ref[...]).astype(BF16)
    if keys:
        out = _dot_nt(w2_ref[...], hid) + b2_ref[...]
        o_ref[0, 0] = _rope_rows(out, cos_ref[...], sin_ref[...]).astype(BF16)
    else:
        o_ref[0, 0] = (_dot(hid, w2_ref[...]) + b2_ref[...]).astype(BF16)


def _compress(halves, which, pos_emb, w1, b1, w2, b2, keys):
    _, B, G, nh, hw = halves.shape
    dh = HEAD_DIM
    hidden = w1.shape[-1]
    pos2 = pos_emb.reshape(2, hw)
    w1s = w1.reshape(2, hw, hidden).astype(BF16)
    cos, sin = _rope_tables(jnp.arange(nh) * CMP_STRIDE + CMP_BLOCK - 1)
    if keys:
        w2a, b2a = w2.T.astype(BF16), b2.reshape(dh, 1)
        out_shape, out_block, out_idx = (B, G, dh, nh), (1, 1, dh, nh), lambda b, g: (b, g, 0, 0)
    else:
        w2a, b2a = w2.astype(BF16), b2.reshape(1, dh)
        out_shape, out_block, out_idx = (B, G, nh, dh), (1, 1, nh, dh), lambda b, g: (b, g, 0, 0)
    c2 = lambda b, g: (0, 0)
    return pl.pallas_call(
        functools.partial(_compress_kernel, nh=nh, keys=keys),
        out_shape=jax.ShapeDtypeStruct(out_shape, BF16),
        grid=(B, G),
        in_specs=[
            pl.BlockSpec((1, 1, 1, nh, hw), lambda b, g: (which, b, g, 0, 0)),
            pl.BlockSpec((2, hw), c2),
            pl.BlockSpec((2, hw, hidden), lambda b, g: (0, 0, 0)),
            pl.BlockSpec((1, hidden), c2),
            pl.BlockSpec(w2a.shape, c2),
            pl.BlockSpec(b2a.shape, c2),
            pl.BlockSpec((ROPE_HALF, nh), c2),
            pl.BlockSpec((ROPE_HALF, nh), c2),
        ],
        out_specs=pl.BlockSpec(out_block, out_idx),
        compiler_params=_cparams(("parallel", "parallel"), 40),
        name="compress_k" if keys else "compress_v",
    )(halves, pos2, w1s, b1.reshape(1, hidden), w2a, b2a, cos.T, sin.T)


def _qproj_kernel(h_ref, g_ref, wq_ref, wg_ref, bg_ref, c_ref, s1_ref, s2_ref, q_ref, gt_ref):
    xn = _rms(h_ref[0], g_ref[...]).astype(BF16)
    q = _dot(xn, wq_ref[...])
    c, s1, s2 = c_ref[...], s1_ref[...], s2_ref[...]
    for k in range(q.shape[-1] // LANES):
        x = q[:, k * LANES:(k + 1) * LANES]
        r = x * c + pltpu.roll(x, ROPE_HALF, axis=1) * s1 + pltpu.roll(x, LANES - ROPE_HALF, axis=1) * s2
        q_ref[0, :, k * LANES:(k + 1) * LANES] = r.astype(BF16)
    gt_ref[0] = jax.nn.sigmoid(_dot(xn, wg_ref[...]) + bg_ref[...])


def _q_proj(h, g, w_qg, b_gate):
    B, S, D = h.shape
    HD = N_HEADS * HEAD_DIM
    ng = N_HEADS * N_BRANCH
    ts = min(SEQ_TILE, S)
    wq = w_qg[:, :HD].astype(BF16)
    wg = jnp.pad(w_qg[:, HD:], ((0, 0), (0, LANES - ng))).astype(BF16)
    bg = jnp.pad(b_gate, (0, LANES - ng)).reshape(1, LANES)
    cos, sin = _rope_tables(jnp.arange(S))
    scale = HEAD_DIM ** -0.5
    ones = jnp.ones((S, HEAD_DIM - ROPE_DIMS), F32)
    zeros = jnp.zeros((S, HEAD_DIM - ROPE_DIMS), F32)
    zh = jnp.zeros((S, ROPE_HALF), F32)
    two = lambda a: jnp.concatenate([a, a], axis=1) * scale
    c = two(jnp.concatenate([cos, cos, ones], axis=1))
    s1 = two(jnp.concatenate([zh, sin, zeros], axis=1))
    s2 = two(jnp.concatenate([-sin, zh, zeros], axis=1))
    tab = pl.BlockSpec((ts, LANES), lambda b, i: (i, 0))
    return pl.pallas_call(
        _qproj_kernel,
        out_shape=(jax.ShapeDtypeStruct((B, S, HD), BF16), jax.ShapeDtypeStruct((B, S, LANES), F32)),
        grid=(B, S // ts),
        in_specs=[
            pl.BlockSpec((1, ts, D), lambda b, i: (b, i, 0)),
            pl.BlockSpec((1, D), lambda b, i: (0, 0)),
            pl.BlockSpec((D, HD), lambda b, i: (0, 0)),
            pl.BlockSpec((D, LANES), lambda b, i: (0, 0)),
            pl.BlockSpec((1, LANES), lambda b, i: (0, 0)),
            tab, tab, tab,
        ],
        out_specs=(pl.BlockSpec((1, ts, HD), lambda b, i: (b, i, 0)),
                   pl.BlockSpec((1, ts, LANES), lambda b, i: (b, i, 0))),
        compiler_params=_cparams(("parallel", "parallel"), 40),
        name="nsa_q_proj",
    )(h, g.reshape(1, D), wq, wg, bg, c, s1, s2)


def _stack_heads(qb):
    return jnp.concatenate([qb[:, r * HEAD_DIM:(r + 1) * HEAD_DIM] for r in range(HEADS_PER_GROUP)], axis=0)


def _gated_unstack(o, gates, g, branch, tq):
    lane = lax.broadcasted_iota(I32, gates.shape, 1)
    cols = []
    for r in range(HEADS_PER_GROUP):
        col = (g * HEADS_PER_GROUP + r) * N_BRANCH + branch
        gate = jnp.sum(jnp.where(lane == col, gates, 0.0), axis=-1, keepdims=True)
        cols.append(o[r * tq:(r + 1) * tq] * gate)
    return jnp.concatenate(cols, axis=-1)


def _query_pos(i, tq):
    rows = lax.broadcasted_iota(I32, (HEADS_PER_GROUP * tq, 1), 0)
    return i * tq + (rows & (tq - 1))


def _cmp_kernel(q_ref, gt_ref, k_ref, v_ref, o_ref, bias_ref, *, tq, nc):
    g = pl.program_id(1)
    i = pl.program_id(2)
    qs = _stack_heads(q_ref[0])
    s = _dot(qs, k_ref[0, 0])
    t = _query_pos(i, tq)
    n = lax.broadcasted_iota(I32, (1, nc), 1)
    s = jnp.where(n * CMP_STRIDE + (CMP_BLOCK - 1) <= t, s, -jnp.inf)
    m = jnp.max(s, axis=-1, keepdims=True)
    m = jnp.where(m == -jnp.inf, 0.0, m)
    e = jnp.exp(s - m)
    p = e / jnp.maximum(jnp.sum(e, axis=-1, keepdims=True), 1e-30)
    o = _dot(p.astype(BF16), v_ref[0, 0])
    o_ref[0] = _gated_unstack(o, gt_ref[0], g, 0, tq).astype(BF16)
    ps = p[0:tq]
    for r in range(1, HEADS_PER_GROUP):
        ps = ps + p[r * tq:(r + 1) * tq]
    nn = lax.broadcasted_iota(I32, (nc, MAX_SEL_BLOCKS), 0) * CMP_STRIDE
    jj = lax.broadcasted_iota(I32, (nc, MAX_SEL_BLOCKS), 1) * SEL_BLOCK
    c2s = jnp.where(nn < jj + SEL_BLOCK, jnp.where(nn + CMP_BLOCK > jj, 1.0, 0.0), 0.0)
    imp = jnp.dot(ps, c2s, precision=HIGHEST, preferred_element_type=F32)
    tq_pos = i * tq + lax.broadcasted_iota(I32, (tq, 1), 0)
    cur = tq_pos // SEL_BLOCK
    j = lax.broadcasted_iota(I32, (tq, MAX_SEL_BLOCKS), 1)
    valid = j <= cur
    bonus = jnp.where(j == 0, FORCE_BONUS, jnp.where(j == cur, FORCE_BONUS, jnp.where(j == cur - 1, FORCE_BONUS, 0.0)))
    work = jnp.where(valid, imp + bonus, -jnp.inf)
    sel = jnp.zeros((tq, MAX_SEL_BLOCKS), F32)
    for _ in range(N_SELECT):
        mx = jnp.max(work, axis=-1, keepdims=True)
        idx = jnp.min(jnp.where(work == mx, j, MAX_SEL_BLOCKS), axis=-1, keepdims=True)
        pick = j == idx
        sel = jnp.where(pick, 1.0, sel)
        work = jnp.where(pick, -jnp.inf, work)
    bias_ref[0, 0] = jnp.where(valid, jnp.where(sel > 0.0, 0.0, MASK_BIAS), MASK_BIAS).astype(BF16)


def _cmp_attn(q, gates, kct, vc):
    B, S, HD = q.shape
    G, dh = N_KV_GROUPS, HEAD_DIM
    gw = HD // G
    nc = kct.shape[-1]
    tq = min(Q_TILE, S)
    return pl.pallas_call(
        functools.partial(_cmp_kernel, tq=tq, nc=nc),
        out_shape=(jax.ShapeDtypeStruct((B, S, HD), BF16), jax.ShapeDtypeStruct((B, G, S, MAX_SEL_BLOCKS), BF16)),
        grid=(B, G, S // tq),
        in_specs=[
            pl.BlockSpec((1, tq, gw), lambda b, g, i: (b, i, g)),
            pl.BlockSpec((1, tq, LANES), lambda b, g, i: (b, i, 0)),
            pl.BlockSpec((1, 1, dh, nc), lambda b, g, i: (b, g, 0, 0)),
            pl.BlockSpec((1, 1, nc, dh), lambda b, g, i: (b, g, 0, 0)),
        ],
        out_specs=(pl.BlockSpec((1, tq, gw), lambda b, g, i: (b, i, g)),
                   pl.BlockSpec((1, 1, tq, MAX_SEL_BLOCKS), lambda b, g, i: (b, g, i, 0))),
        compiler_params=_cparams(("parallel", "parallel", "parallel"), 40),
        name="nsa_compressed",
    )(q, gates, kct, vc)


def _sel_kernel(q_ref, bias_ref, gt_ref, k_ref, v_ref, o_ref, qa, m_s, l_s, acc, *, tq, tk):
    g = pl.program_id(1)
    i = pl.program_id(2)
    qb = q_ref[0]
    bias = bias_ref[0, 0]
    for r in range(HEADS_PER_GROUP):
        qa[r * tq:(r + 1) * tq, 0:MAX_SEL_BLOCKS] = bias
        qa[r * tq:(r + 1) * tq, MAX_SEL_BLOCKS:] = qb[:, r * HEAD_DIM:(r + 1) * HEAD_DIM]
    m_s[...] = jnp.full_like(m_s, -jnp.inf)
    l_s[...] = jnp.zeros_like(l_s)
    acc[...] = jnp.zeros_like(acc)
    t = _query_pos(i, tq)
    lane = lax.broadcasted_iota(I32, (1, tk), 1)

    def body(jt, c):
        k0 = pl.multiple_of(jt * tk, tk)
        s = _dot(qa[...], k_ref[0, 0, :, pl.ds(k0, tk)])
        s = jnp.where(k0 + lane <= t, s, -jnp.inf)
        m_old = m_s[...]
        m_new = jnp.maximum(m_old, jnp.max(s, axis=-1, keepdims=True))
        alpha = jnp.exp(m_old - m_new)
        p = jnp.exp(s - m_new)
        l_s[...] = alpha * l_s[...] + jnp.sum(p, axis=-1, keepdims=True)
        acc[...] = alpha * acc[...] + _dot(p.astype(BF16), v_ref[0, 0, pl.ds(k0, tk), :])
        m_s[...] = m_new
        return c

    lax.fori_loop(0, (i * tq + tq + tk - 1) // tk, body, 0)
    o_ref[0] = _gated_unstack(acc[...] / l_s[...], gt_ref[0], g, 1, tq).astype(BF16)


def _sel_attn(q, bias, gates, kst, vs):
    B, S, HD = q.shape
    G, dh = N_KV_GROUPS, HEAD_DIM
    gw = HD // G
    kd = kst.shape[2]
    tq = min(Q_TILE, S)
    tk = min(KEY_TILE, S)
    rows = HEADS_PER_GROUP * tq
    return pl.pallas_call(
        functools.partial(_sel_kernel, tq=tq, tk=tk),
        out_shape=jax.ShapeDtypeStruct((B, S, HD), BF16),
        grid=(B, G, S // tq),
        in_specs=[
            pl.BlockSpec((1, tq, gw), lambda b, g, i: (b, i, g)),
            pl.BlockSpec((1, 1, tq, MAX_SEL_BLOCKS), lambda b, g, i: (b, g, i, 0)),
            pl.BlockSpec((1, tq, LANES), lambda b, g, i: (b, i, 0)),
            pl.BlockSpec((1, 1, kd, S), lambda b, g, i: (b, g, 0, 0)),
            pl.BlockSpec((1, 1, S, dh), lambda b, g, i: (b, g, 0, 0)),
        ],
        out_specs=pl.BlockSpec((1, tq, gw), lambda b, g, i: (b, i, g)),
        scratch_shapes=[pltpu.VMEM((rows, kd), BF16), pltpu.VMEM((rows, 1), F32), pltpu.VMEM((rows, 1), F32),
                        pltpu.VMEM((rows, dh), F32)],
        compiler_params=_cparams(("parallel", "parallel", "arbitrary"), 48),
        name="nsa_selected",
    )(q, bias, gates, kst, vs)


def _win_kernel(q_ref, gt_ref, k_ref, v_ref, o_ref, *, tq, wb):
    g = pl.program_id(1)
    i = pl.program_id(2)
    k0 = pl.multiple_of(jnp.maximum(i * tq - WINDOW, 0), tq)
    qs = _stack_heads(q_ref[0])
    s = _dot(qs, k_ref[0, 0, :, pl.ds(k0, wb)])
    t = _query_pos(i, tq)
    kpos = k0 + lax.broadcasted_iota(I32, (1, wb), 1)
    s = jnp.where(kpos <= t, jnp.where(kpos > t - WINDOW, s, -jnp.inf), -jnp.inf)
    m = jnp.max(s, axis=-1, keepdims=True)
    e = jnp.exp(s - m)
    p = e / jnp.sum(e, axis=-1, keepdims=True)
    o = _dot(p.astype(BF16), v_ref[0, 0, pl.ds(k0, wb), :])
    o_ref[0] = _gated_unstack(o, gt_ref[0], g, 2, tq).astype(BF16)


def _win_attn(q, gates, kwt, vw):
    B, S, HD = q.shape
    G, dh = N_KV_GROUPS, HEAD_DIM
    gw = HD // G
    tq = min(Q_TILE, S)
    wb = WINDOW + tq
    assert S >= wb and tq % LANES == 0
    return pl.pallas_call(
        functools.partial(_win_kernel, tq=tq, wb=wb),
        out_shape=jax.ShapeDtypeStruct((B, S, HD), BF16),
        grid=(B, G, S // tq),
        in_specs=[
            pl.BlockSpec((1, tq, gw), lambda b, g, i: (b, i, g)),
            pl.BlockSpec((1, tq, LANES), lambda b, g, i: (b, i, 0)),
            pl.BlockSpec((1, 1, dh, S), lambda b, g, i: (b, g, 0, 0)),
            pl.BlockSpec((1, 1, S, dh), lambda b, g, i: (b, g, 0, 0)),
        ],
        out_specs=pl.BlockSpec((1, tq, gw), lambda b, g, i: (b, i, g)),
        compiler_params=_cparams(("parallel", "parallel", "parallel"), 40),
        name="nsa_window",
    )(q, gates, kwt, vw)


LOG2E = 1.4426950408889634
Q_SCALE = HEAD_DIM ** -0.5 * LOG2E
V_ROWS = HEAD_DIM + 16
KEY_AUG = 2 * LANES
NEG_BIG = -1e30


def _lane_rope_tables(pos):
    cos, sin = _rope_tables(pos)
    n = pos.shape[0]
    ones = jnp.ones((n, HEAD_DIM - ROPE_DIMS), F32)
    zeros = jnp.zeros((n, HEAD_DIM - ROPE_DIMS), F32)
    zh = jnp.zeros((n, ROPE_HALF), F32)
    two = lambda a: jnp.concatenate([a, a], axis=1)
    return (two(jnp.concatenate([cos, cos, ones], axis=1)), two(jnp.concatenate([zh, sin, zeros], axis=1)),
            two(jnp.concatenate([-sin, zh, zeros], axis=1)))


def _rope_lanes(x, c, s1, s2):
    return x * c + pltpu.roll(x, ROPE_HALF, axis=1) * s1 + pltpu.roll(x, LANES - ROPE_HALF, axis=1) * s2


def _kv2_kernel(h_ref, g_ref, wn_ref, wt_ref, c_ref, s1_ref, s2_ref, cv_ref, ksa_ref, vst_ref, kw_ref, vwt_ref, *, ts):
    i = pl.program_id(1)
    kvw = N_KV_GROUPS * HEAD_DIM
    hn = _rms(h_ref[0], g_ref[...]).astype(BF16)
    nat = _dot(hn, wn_ref[...])
    cv_ref[0] = nat[:, 0:2 * kvw]
    tr = _dot_nt(wt_ref[...], hn)
    c, s1, s2 = c_ref[...], s1_ref[...], s2_ref[...]
    roped = [_rope_lanes(nat[:, 2 * kvw + k * LANES:2 * kvw + (k + 1) * LANES], c, s1, s2)
             for k in range(2 * kvw // LANES)]
    lane = lax.broadcasted_iota(I32, (ts, KEY_AUG), 1)
    pos = i * ts + lax.broadcasted_iota(I32, (ts, KEY_AUG), 0)
    onehot = jnp.where(lane - HEAD_DIM == pos // SEL_BLOCK, 1.0, 0.0).astype(BF16)
    ones_row = jnp.where(lax.broadcasted_iota(I32, (V_ROWS - HEAD_DIM, ts), 0) == 0, 1.0, 0.0).astype(BF16)
    per_tile = LANES // HEAD_DIM
    for g in range(N_KV_GROUPS):
        lo = (g % per_tile) * HEAD_DIM
        ksa_ref[0, g] = onehot
        ksa_ref[0, g, :, 0:HEAD_DIM] = roped[g // per_tile][:, lo:lo + HEAD_DIM].astype(BF16)
        kw_ref[0, g] = roped[N_KV_GROUPS // per_tile + g // per_tile][:, lo:lo + HEAD_DIM].astype(BF16)
        for ref, base in ((vst_ref, 0), (vwt_ref, kvw)):
            ref[0, g, 0:HEAD_DIM, :] = tr[base + g * HEAD_DIM:base + (g + 1) * HEAD_DIM].astype(BF16)
            ref[0, g, HEAD_DIM:, :] = ones_row


def _shared_kv_proj2(h, kv_norm, w_kv):
    B, S, D = h.shape
    G, dh = N_KV_GROUPS, HEAD_DIM
    kvw = G * dh
    ts = min(SEQ_TILE, S)
    br = lambda k: w_kv[:, k * kvw:(k + 1) * kvw]
    w_nat = jnp.concatenate([br(0), br(1), br(2), br(4)], axis=1).astype(BF16)
    w_tr = jnp.concatenate([br(3), br(5)], axis=1).T.astype(BF16)
    tabs = _lane_rope_tables(jnp.arange(S))
    tab = pl.BlockSpec((ts, LANES), lambda b, i: (i, 0))
    return pl.pallas_call(
        functools.partial(_kv2_kernel, ts=ts),
        out_shape=(
            jax.ShapeDtypeStruct((B, S, 2 * kvw), F32),
            jax.ShapeDtypeStruct((B, G, S, KEY_AUG), BF16),
            jax.ShapeDtypeStruct((B, G, V_ROWS, S), BF16),
            jax.ShapeDtypeStruct((B, G, S, dh), BF16),
            jax.ShapeDtypeStruct((B, G, V_ROWS, S), BF16),
        ),
        grid=(B, S // ts),
        in_specs=[
            pl.BlockSpec((1, ts, D), lambda b, i: (b, i, 0)),
            pl.BlockSpec((1, D), lambda b, i: (0, 0)),
            pl.BlockSpec((D, 4 * kvw), lambda b, i: (0, 0)),
            pl.BlockSpec((2 * kvw, D), lambda b, i: (0, 0)),
            tab, tab, tab,
        ],
        out_specs=(
            pl.BlockSpec((1, ts, 2 * kvw), lambda b, i: (b, i, 0)),
            pl.BlockSpec((1, G, ts, KEY_AUG), lambda b, i: (b, 0, i, 0)),
            pl.BlockSpec((1, G, V_ROWS, ts), lambda b, i: (b, 0, 0, i)),
            pl.BlockSpec((1, G, ts, dh), lambda b, i: (b, 0, i, 0)),
            pl.BlockSpec((1, G, V_ROWS, ts), lambda b, i: (b, 0, 0, i)),
        ),
        compiler_params=_cparams(("parallel", "parallel"), 48),
        name="shared_kv_proj",
    )(h, kv_norm.reshape(1, D), w_nat, w_tr, *tabs)


def _compress2_kernel(x_ref, pos_ref, w1_ref, b1_ref, w2_ref, b2_ref, c_ref, s1_ref, s2_ref, o_ref, *, nh, keys):
    x = x_ref[0, 0, 0]
    a = _dot((x + pos_ref[0:1]).astype(BF16), w1_ref[0])
    b = _dot((x + pos_ref[1:2]).astype(BF16), w1_ref[1])
    hid = jax.nn.gelu(a + pltpu.roll(b, nh - 1, axis=0) + b1_ref[...]).astype(BF16)
    if keys:
        out = _rope_lanes(_dot(hid, w2_ref[...]) + b2_ref[...], c_ref[...], s1_ref[...], s2_ref[...])
        o_ref[0, 0] = out[:, 0:HEAD_DIM].astype(BF16)
    else:
        o_ref[0, 0] = (_dot_nt(w2_ref[...], hid) + b2_ref[...]).astype(BF16)


def _compress2(halves, which, pos_emb, w1, b1, w2, b2, keys):
    _, B, G, nh, hw = halves.shape
    dh = HEAD_DIM
    hidden = w1.shape[-1]
    pos2 = pos_emb.reshape(2, hw)
    w1s = w1.reshape(2, hw, hidden).astype(BF16)
    tabs = _lane_rope_tables(jnp.arange(nh) * CMP_STRIDE + CMP_BLOCK - 1)
    if keys:
        w2a = jnp.pad(w2, ((0, 0), (0, LANES - dh))).astype(BF16)
        b2a = jnp.pad(b2, (0, LANES - dh)).reshape(1, LANES)
        out_shape, out_block = (B, G, nh, dh), (1, 1, nh, dh)
    else:
        w2a, b2a = w2.T.astype(BF16), b2.reshape(dh, 1)
        out_shape, out_block = (B, G, dh, nh), (1, 1, dh, nh)
    c2 = lambda b, g: (0, 0)
    tab = pl.BlockSpec((nh, LANES), c2)
    return pl.pallas_call(
        functools.partial(_compress2_kernel, nh=nh, keys=keys),
        out_shape=jax.ShapeDtypeStruct(out_shape, BF16),
        grid=(B, G),
        in_specs=[
            pl.BlockSpec((1, 1, 1, nh, hw), lambda b, g: (which, b, g, 0, 0)),
            pl.BlockSpec((2, hw), c2),
            pl.BlockSpec((2, hw, hidden), lambda b, g: (0, 0, 0)),
            pl.BlockSpec((1, hidden), c2),
            pl.BlockSpec(w2a.shape, c2),
            pl.BlockSpec(b2a.shape, c2),
            tab, tab, tab,
        ],
        out_specs=pl.BlockSpec(out_block, lambda b, g: (b, g, 0, 0)),
        compiler_params=_cparams(("parallel", "parallel"), 40),
        name="compress_k" if keys else "compress_v",
    )(halves, pos2, w1s, b1.reshape(1, hidden), w2a, b2a, *tabs)


def _qproj2_kernel(h_ref, g_ref, wqt_ref, wg_ref, bg_ref, cos_ref, sin_ref, qt_ref, gt_ref):
    xn = _rms(h_ref[0], g_ref[...]).astype(BF16)
    tr = _dot_nt(wqt_ref[...], xn)
    cos, sin = cos_ref[...], sin_ref[...]
    for hd in range(N_HEADS):
        rows = slice(hd * HEAD_DIM, (hd + 1) * HEAD_DIM)
        qt_ref[0, rows, :] = (_rope_rows(tr[rows], cos, sin) * Q_SCALE).astype(BF16)
    gt_ref[0] = jax.nn.sigmoid(_dot(xn, wg_ref[...]) + bg_ref[...])


def _q_proj2(h, g, w_qg, b_gate):
    B, S, D = h.shape
    HD = N_HEADS * HEAD_DIM
    ng = N_HEADS * N_BRANCH
    ts = min(SEQ_TILE, S)
    wqt = w_qg[:, :HD].T.astype(BF16)
    wg = jnp.pad(w_qg[:, HD:], ((0, 0), (0, LANES - ng))).astype(BF16)
    bg = jnp.pad(b_gate, (0, LANES - ng)).reshape(1, LANES)
    cos, sin = _rope_tables(jnp.arange(S))
    tab = pl.BlockSpec((ROPE_HALF, ts), lambda b, i: (0, i))
    return pl.pallas_call(
        _qproj2_kernel,
        out_shape=(jax.ShapeDtypeStruct((B, HD, S), BF16), jax.ShapeDtypeStruct((B, S, LANES), F32)),
        grid=(B, S // ts),
        in_specs=[
            pl.BlockSpec((1, ts, D), lambda b, i: (b, i, 0)),
            pl.BlockSpec((1, D), lambda b, i: (0, 0)),
            pl.BlockSpec((HD, D), lambda b, i: (0, 0)),
            pl.BlockSpec((D, LANES), lambda b, i: (0, 0)),
            pl.BlockSpec((1, LANES), lambda b, i: (0, 0)),
            tab, tab,
        ],
        out_specs=(pl.BlockSpec((1, HD, ts), lambda b, i: (b, 0, i)),
                   pl.BlockSpec((1, ts, LANES), lambda b, i: (b, i, 0))),
        compiler_params=_cparams(("parallel", "parallel"), 40),
        name="nsa_q_proj",
    )(h, g.reshape(1, D), wqt, wg, bg, cos.T, sin.T)


def _heads_on_lanes(qt):
    return jnp.concatenate([qt[r * HEAD_DIM:(r + 1) * HEAD_DIM] for r in range(HEADS_PER_GROUP)], axis=1)


def _lane_query_pos(i, tq):
    lanes = lax.broadcasted_iota(I32, (1, HEADS_PER_GROUP * tq), 1)
    return i * tq + (lanes & (tq - 1))


def _finish_heads(acc_t, gates, g, branch, tq, denom_row):
    lane = lax.broadcasted_iota(I32, gates.shape, 1)
    rows = acc_t.shape[0]
    cols = []
    for r in range(HEADS_PER_GROUP):
        blk = acc_t[:, r * tq:(r + 1) * tq]
        nat = jnp.concatenate([blk, jnp.zeros((-rows % LANES, tq), F32)], axis=0).T
        col = (g * HEADS_PER_GROUP + r) * N_BRANCH + branch
        scale = jnp.sum(jnp.where(lane == col, gates, 0.0), axis=-1, keepdims=True)
        if denom_row is not None:
            scale = scale / nat[:, denom_row:denom_row + 1]
        cols.append(nat[:, 0:HEAD_DIM] * scale)
    return jnp.concatenate(cols, axis=-1)


def _cmp2_kernel(qt_ref, gt_ref, k_ref, vt_ref, c2s_ref, o_ref, bias_ref, *, tq, nc):
    g = pl.program_id(1)
    i = pl.program_id(2)

    def body(nk, nb):
        s = _dot(k_ref[0, 0, 0:nk, :], _heads_on_lanes(qt_ref[0]))
        t = _lane_query_pos(i, tq)
        n = lax.broadcasted_iota(I32, (nk, 1), 0)
        s = jnp.where(n * CMP_STRIDE + (CMP_BLOCK - 1) <= t, s, -jnp.inf)
        m = jnp.max(s, axis=0, keepdims=True)
        m = jnp.where(m == -jnp.inf, 0.0, m)
        e = jnp.exp2(s - m)
        p = e * (1.0 / jnp.maximum(jnp.sum(e, axis=0, keepdims=True), 1e-30))
        ot = _dot(vt_ref[0, 0, :, 0:nk], p.astype(BF16))
        o_ref[0] = _finish_heads(ot, gt_ref[0], g, 0, tq, None).astype(BF16)
        ps = p[:, 0:tq]
        for r in range(1, HEADS_PER_GROUP):
            ps = ps + p[:, r * tq:(r + 1) * tq]
        hi = ps.astype(BF16)
        rem = ps - hi.astype(F32)
        mid = rem.astype(BF16)
        lo = (rem - mid.astype(F32)).astype(BF16)
        c2s = c2s_ref[0:nb, 0:nk]
        imp = _dot(c2s, hi) + _dot(c2s, mid) + _dot(c2s, lo)
        cur = (i * tq + lax.broadcasted_iota(I32, (1, tq), 1)) // SEL_BLOCK
        j = lax.broadcasted_iota(I32, (nb, tq), 0)
        valid = j <= cur
        bonus = jnp.where(j == 0, FORCE_BONUS,
                          jnp.where(j == cur, FORCE_BONUS, jnp.where(j == cur - 1, FORCE_BONUS, 0.0)))
        work = jnp.where(valid, imp + bonus, -jnp.inf)
        sel = jnp.zeros((nb, tq), F32)
        for _ in range(N_SELECT):
            mx = jnp.max(work, axis=0, keepdims=True)
            idx = jnp.min(jnp.where(work == mx, j, MAX_SEL_BLOCKS), axis=0, keepdims=True)
            pick = j == idx
            sel = jnp.where(pick, 1.0, sel)
            work = jnp.where(pick, -jnp.inf, work)
        bias_ref[0, 0, 0:nb, :] = jnp.where(valid, jnp.where(sel > 0.0, 0.0, MASK_BIAS), MASK_BIAS).astype(BF16)
        if nb < MAX_SEL_BLOCKS:
            bias_ref[0, 0, nb:, :] = jnp.full((MAX_SEL_BLOCKS - nb, tq), MASK_BIAS, BF16)

    per_class = LANES * CMP_STRIDE // tq
    n_class = -(-(nc * CMP_STRIDE // tq) // per_class)
    for cls in range(n_class):
        nk = min(nc, LANES * (cls + 1))
        nb = min(MAX_SEL_BLOCKS, nk * CMP_STRIDE // SEL_BLOCK)
        pl.when(i // per_class == cls)(functools.partial(body, nk, nb))


def _cmp_attn2(qt, gates, kc, vct):
    B, HD, S = qt.shape
    G, dh = N_KV_GROUPS, HEAD_DIM
    gw = HD // G
    nc = kc.shape[2]
    tq = min(Q_TILE, S)
    n0 = jnp.arange(nc)[None, :] * CMP_STRIDE
    j0 = jnp.arange(MAX_SEL_BLOCKS)[:, None] * SEL_BLOCK
    c2s = ((n0 < j0 + SEL_BLOCK) & (n0 + CMP_BLOCK > j0)).astype(BF16)
    return pl.pallas_call(
        functools.partial(_cmp2_kernel, tq=tq, nc=nc),
        out_shape=(jax.ShapeDtypeStruct((B, S, HD), BF16), jax.ShapeDtypeStruct((B, G, MAX_SEL_BLOCKS, S), BF16)),
        grid=(B, G, S // tq),
        in_specs=[
            pl.BlockSpec((1, gw, tq), lambda b, g, i: (b, g, i)),
            pl.BlockSpec((1, tq, LANES), lambda b, g, i: (b, i, 0)),
            pl.BlockSpec((1, 1, nc, dh), lambda b, g, i: (b, g, 0, 0)),
            pl.BlockSpec((1, 1, dh, nc), lambda b, g, i: (b, g, 0, 0)),
            pl.BlockSpec((MAX_SEL_BLOCKS, nc), lambda b, g, i: (0, 0)),
        ],
        out_specs=(pl.BlockSpec((1, tq, gw), lambda b, g, i: (b, i, g)),
                   pl.BlockSpec((1, 1, MAX_SEL_BLOCKS, tq), lambda b, g, i: (b, g, 0, i))),
        compiler_params=_cparams(("parallel", "parallel", "parallel"), 40),
        name="nsa_compressed",
    )(qt, gates, kc, vct, c2s)


def _sel2_kernel(qt_ref, bias_ref, gt_ref, k_ref, vt_ref, o_ref, qa, m_s, acc, s_a, s_b, *, tq, tk):
    g = pl.program_id(1)
    i = pl.program_id(2)
    qt = qt_ref[0]
    bias = bias_ref[0, 0]
    for r in range(HEADS_PER_GROUP):
        cols = slice(r * tq, (r + 1) * tq)
        qa[0:HEAD_DIM, cols] = qt[r * HEAD_DIM:(r + 1) * HEAD_DIM]
        qa[HEAD_DIM:HEAD_DIM + MAX_SEL_BLOCKS, cols] = bias
        qa[HEAD_DIM + MAX_SEL_BLOCKS:, cols] = jnp.zeros((KEY_AUG - HEAD_DIM - MAX_SEL_BLOCKS, tq), BF16)
    m_s[...] = jnp.full(m_s.shape, NEG_BIG, F32)
    acc[...] = jnp.zeros(acc.shape, F32)
    width = HEADS_PER_GROUP * tq

    def qk(jt, buf):
        buf[...] = _dot(k_ref[0, 0, pl.ds(pl.multiple_of(jt * tk, tk), tk), :], qa[...])

    def online_softmax(key0, n_keys, s):
        m_old = m_s[...]
        m_new = jnp.maximum(m_old, jnp.max(s, axis=0, keepdims=True))
        p = jnp.exp2(s - m_new).astype(BF16)
        acc[...] = jnp.exp2(m_old - m_new) * acc[...] + _dot(vt_ref[0, 0, :, pl.ds(key0, n_keys)], p)
        m_s[...] = m_new

    def absorb(jt, buf):
        online_softmax(pl.multiple_of(jt * tk, tk), tk, buf[...])

    def absorb_last(jt, buf):
        own = jnp.where(lax.broadcasted_iota(I32, (tq, 1), 0) <= (lax.broadcasted_iota(I32, (1, width), 1) & (tq - 1)),
                        0.0, NEG_BIG)
        k0 = pl.multiple_of(jt * tk, tk)

        @pl.when(i % 2 == 1)
        def _():
            online_softmax(k0, tq, buf[0:tq, :])
            online_softmax(pl.multiple_of(k0 + tq, tq), tq, buf[tq:tk, :] + own)

        @pl.when(i % 2 == 0)
        def _():
            online_softmax(k0, tq, buf[0:tq, :] + own)

    def pair(u, c):
        qk(2 * u + 1, s_b)
        absorb(2 * u, s_a)
        qk(2 * u + 2, s_a)
        absorb(2 * u + 1, s_b)
        return c

    last = (i * tq + tq + tk - 1) // tk - 1
    qk(0, s_a)
    lax.fori_loop(0, last // 2, pair, 0)

    @pl.when(last % 2 == 1)
    def _():
        qk(last, s_b)
        absorb(last - 1, s_a)
        absorb_last(last, s_b)

    @pl.when(last % 2 == 0)
    def _():
        absorb_last(last, s_a)

    o_ref[0] = _finish_heads(acc[...], gt_ref[0], g, 1, tq, HEAD_DIM).astype(BF16)


def _sel_attn2(qt, bias, gates, ksa, vst):
    B, HD, S = qt.shape
    G = N_KV_GROUPS
    gw = HD // G
    tq = min(SEL_Q_TILE, S // 2)
    tk = 2 * tq
    assert S % tk == 0
    width = HEADS_PER_GROUP * tq
    return pl.pallas_call(
        functools.partial(_sel2_kernel, tq=tq, tk=tk),
        out_shape=jax.ShapeDtypeStruct((B, S, HD), BF16),
        grid=(B, G, S // tq),
        in_specs=[
            pl.BlockSpec((1, gw, tq), lambda b, g, i: (b, g, i)),
            pl.BlockSpec((1, 1, MAX_SEL_BLOCKS, tq), lambda b, g, i: (b, g, 0, i)),
            pl.BlockSpec((1, tq, LANES), lambda b, g, i: (b, i, 0)),
            pl.BlockSpec((1, 1, S, KEY_AUG), lambda b, g, i: (b, g, 0, 0)),
            pl.BlockSpec((1, 1, V_ROWS, S), lambda b, g, i: (b, g, 0, 0)),
        ],
        out_specs=pl.BlockSpec((1, tq, gw), lambda b, g, i: (b, i, g)),
        scratch_shapes=[pltpu.VMEM((KEY_AUG, width), BF16), pltpu.VMEM((1, width), F32),
                        pltpu.VMEM((V_ROWS, width), F32),
                        pltpu.VMEM((tk, width), F32), pltpu.VMEM((tk, width), F32)],
        compiler_params=_cparams(("parallel", "parallel", "arbitrary"), 56),
        name="nsa_selected",
    )(qt, bias, gates, ksa, vst)


def _win2_kernel(qt_ref, gt_ref, k_ref, vt_ref, o_ref, *, tq, wb):
    g = pl.program_id(1)
    i = pl.program_id(2)
    nw = WINDOW // tq
    width = HEADS_PER_GROUP * tq
    r = lax.broadcasted_iota(I32, (tq, 1), 0)
    c = lax.broadcasted_iota(I32, (1, width), 1) & (tq - 1)
    oldest = jnp.where(r > c, 0.0, -jnp.inf)
    own = jnp.where(r <= c, 0.0, -jnp.inf)

    def body(miss):
        n_keys = (nw + 1 - miss) * tq
        k0 = pl.multiple_of((i - nw + miss) * tq, tq)
        s = _dot(k_ref[0, 0, pl.ds(k0, n_keys), :], _heads_on_lanes(qt_ref[0]))
        parts = []
        for j in range(miss, nw + 1):
            part = s[(j - miss) * tq:(j - miss + 1) * tq]
            parts.append(part + oldest if j == 0 else part + own if j == nw else part)
        s = jnp.concatenate(parts, axis=0)
        m = jnp.max(s, axis=0, keepdims=True)
        p = jnp.exp2(s - m).astype(BF16)
        acc = _dot(vt_ref[0, 0, :, pl.ds(k0, n_keys)], p)
        o_ref[0] = _finish_heads(acc, gt_ref[0], g, 2, tq, HEAD_DIM).astype(BF16)

    for miss in range(nw + 1):
        pl.when(jnp.maximum(nw - i, 0) == miss)(functools.partial(body, miss))


def _win_attn2(qt, gates, kw, vwt):
    B, HD, S = qt.shape
    G, dh = N_KV_GROUPS, HEAD_DIM
    gw = HD // G
    tq = min(Q_TILE, S)
    wb = WINDOW + tq
    assert S >= wb and tq % LANES == 0
    return pl.pallas_call(
        functools.partial(_win2_kernel, tq=tq, wb=wb),
        out_shape=jax.ShapeDtypeStruct((B, S, HD), BF16),
        grid=(B, G, S // tq),
        in_specs=[
            pl.BlockSpec((1, gw, tq), lambda b, g, i: (b, g, i)),
            pl.BlockSpec((1, tq, LANES), lambda b, g, i: (b, i, 0)),
            pl.BlockSpec((1, 1, S, dh), lambda b, g, i: (b, g, 0, 0)),
            pl.BlockSpec((1, 1, V_ROWS, S), lambda b, g, i: (b, g, 0, 0)),
        ],
        out_specs=pl.BlockSpec((1, tq, gw), lambda b, g, i: (b, i, g)),
        compiler_params=_cparams(("parallel", "parallel", "parallel"), 40),
        name="nsa_window",
    )(qt, gates, kw, vwt)


def _oproj_kernel(h_ref, a_ref, b_ref, c_ref, w_ref, o_ref):
    o = a_ref[...].astype(F32) + b_ref[...].astype(F32) + c_ref[...].astype(F32)
    o_ref[...] = h_ref[...] + _dot(o.astype(BF16), w_ref[...])


def _out_proj(h2, oc, os_, ow, w_o):
    T, D = h2.shape
    HD = oc.shape[-1]
    tt = min(SEQ_TILE, T)
    blk = lambda w: pl.BlockSpec((tt, w), lambda i: (i, 0))
    return pl.pallas_call(
        _oproj_kernel,
        out_shape=jax.ShapeDtypeStruct((T, D), F32),
        grid=(T // tt,),
        in_specs=[blk(D), blk(HD), blk(HD), blk(HD), pl.BlockSpec((HD, D), lambda i: (0, 0))],
        out_specs=blk(D),
        compiler_params=_cparams(("parallel",), 40),
        name="nsa_out_proj",
    )(h2, oc, os_, ow, w_o.astype(BF16))


def _nsa_layer(h, g, w_qg, b_gate, w_o, shared):
    B, S, D = h.shape
    kc, vct, ksa, vst, kw, vwt = shared
    qt, gates = _q_proj2(h, g, w_qg, b_gate)
    oc, bias = _cmp_attn2(qt, gates, kc, vct)
    os_ = _sel_attn2(qt, bias, gates, ksa, vst)
    ow = _win_attn2(qt, gates, kw, vwt)
    flat = lambda a: a.reshape(B * S, a.shape[-1])
    return _out_proj(flat(h), flat(oc), flat(os_), flat(ow), w_o).reshape(B, S, D)


def _shared_kv(h, kv_norm, w_kv, ck, cv):
    B, S, _ = h.shape
    G, dh = N_KV_GROUPS, HEAD_DIM
    assert S % SEL_BLOCK == 0 and S // SEL_BLOCK <= MAX_SEL_BLOCKS
    cvals, ksa, vst, kw, vwt = _shared_kv_proj2(h, kv_norm, w_kv)
    halves = cvals.reshape(B, S, 2, G, dh).transpose(2, 0, 3, 1, 4).reshape(2, B, G, S // CMP_STRIDE, CMP_STRIDE * dh)
    kc = _compress2(halves, 0, *ck, keys=True)
    vct = _compress2(halves, 1, *cv, keys=False)
    return kc, vct, ksa, vst, kw, vwt


def kernel(x, p, norm_mix, norm_ffn, norm_ple, pool_w, pool_b, pool_scale, kv_norm, w_kv, cmp_k_pos, cmp_k_w1, cmp_k_b1, cmp_k_w2, cmp_k_b2, cmp_v_pos, cmp_v_w1, cmp_v_b1, cmp_v_w2, cmp_v_b2, w_qg, b_gate, w_o, router_g_w, router_g_b, router_e_w, router_e_b, moe_w1, moe_w3, moe_w2, ple_proj, ple_gate_w, ple_gate_b, final_norm):
    B, S, D = x.shape
    depth = p.shape[0]
    n_a = pool_w.shape[0]
    T = B * S
    h = x
    shared = None
    for i in range(depth):
        if i == n_a:
            shared = _shared_kv(h, kv_norm, w_kv,
                                (cmp_k_pos, cmp_k_w1, cmp_k_b1, cmp_k_w2, cmp_k_b2),
                                (cmp_v_pos, cmp_v_w1, cmp_v_b1, cmp_v_w2, cmp_v_b2))
        if i < n_a:
            h = _pool_layer(h, norm_mix[i], pool_w[i], pool_b[i], pool_scale[i])
        else:
            j = i - n_a
            h = _nsa_layer(h, norm_mix[i], w_qg[j], b_gate[j], w_o[j], shared)
        h = _moe_ple_layer(h.reshape(T, D), p.reshape(depth, T, p.shape[-1]), i, norm_ffn[i], router_g_w[i],
                           router_g_b[i], router_e_w[i], router_e_b[i], moe_w1, moe_w3, moe_w2, norm_ple[i],
                           ple_gate_w[i], ple_gate_b[i], ple_proj[i], final_norm, i == depth - 1).reshape(B, S, D)
    return h
```

```python
import functools

import jax
import jax.numpy as jnp
from jax import lax
from jax.experimental import pallas as pl
from jax.experimental.pallas import tpu as pltpu

F32 = jnp.float32
BF16 = jnp.bfloat16
I32 = jnp.int32

POOL_WINDOWS = (2, 4, 8, 16)
N_HEADS = 16
HEAD_DIM = 64
N_KV_GROUPS = 4
HEADS_PER_GROUP = N_HEADS // N_KV_GROUPS
N_BRANCH = 3
ROPE_DIMS = HEAD_DIM // 4
ROPE_HALF = ROPE_DIMS // 2
ROPE_THETA = 500000.0
CMP_BLOCK = 32
CMP_STRIDE = 16
SEL_BLOCK = 64
N_SELECT = 16
WINDOW = 512
FORCE_BONUS = 1e4
N_EXPERT_GROUPS = 4
EXPERTS_PER_GROUP = 8
N_EXPERTS = N_EXPERT_GROUPS * EXPERTS_PER_GROUP
TOP_K_IN_GROUP = 2
RMS_EPS = 1e-6

LANES = 128
MAX_SEL_BLOCKS = LANES
MASK_BIAS = -30000.0

SEQ_TILE = 512
TOK_TILE = 512
CMB_TILE = 256
MOE_CHUNK = 512
Q_TILE = 256
SEL_Q_TILE = 512
KEY_TILE = 512
HALO = 16

HIGHEST = lax.Precision.HIGHEST


def _cparams(sem, vmem_mb):
    return pltpu.CompilerParams(dimension_semantics=sem, vmem_limit_bytes=vmem_mb * 1024 * 1024)


def _rms(x, g):
    return x * lax.rsqrt(jnp.mean(x * x, axis=-1, keepdims=True) + RMS_EPS) * g


def _dot(a, b):
    return jnp.dot(a, b, preferred_element_type=F32)


def _dot_nt(a, b):
    return lax.dot_general(a, b, (((1,), (1,)), ((), ())), preferred_element_type=F32)


def _pool_kernel(h_ref, halo_ref, g_ref, w_ref, b_ref, sc_ref, o_ref, *, ts, cg):
    i = pl.program_id(1)
    x = h_ref[0]
    g = g_ref[...]
    xn = _rms(x, g)
    hn = _rms(halo_ref[0], g)
    hn = jnp.where(i > 0, hn, 0.0)
    ext = jnp.concatenate([hn, xn], axis=0)
    t = i * ts + lax.broadcasted_iota(I32, (ts, 1), 0)
    outs = []
    for gi, w in enumerate(POOL_WINDOWS):
        s = ext[:, gi * cg:(gi + 1) * cg]
        k = 1
        while k < w:
            s = s + pltpu.roll(s, k, axis=0)
            k *= 2
        cnt = jnp.minimum(t + 1, w).astype(F32)
        pooled = s[HALO:] / cnt - xn[:, gi * cg:(gi + 1) * cg]
        outs.append(_dot(pooled.astype(BF16), w_ref[gi]))
    y = jnp.concatenate(outs, axis=-1)
    o_ref[0] = x + (y + b_ref[...]) * sc_ref[...]


def _pool_layer(h, g, w, b, sc):
    B, S, D = h.shape
    ts = min(SEQ_TILE, S)
    cg = D // len(POOL_WINDOWS)
    row = lambda v: v.reshape(1, D)
    return pl.pallas_call(
        functools.partial(_pool_kernel, ts=ts, cg=cg),
        out_shape=jax.ShapeDtypeStruct((B, S, D), F32),
        grid=(B, S // ts),
        in_specs=[
            pl.BlockSpec((1, ts, D), lambda b_, i: (b_, i, 0)),
            pl.BlockSpec((1, HALO, D), lambda b_, i: (b_, jnp.maximum(i * (ts // HALO) - 1, 0), 0)),
            pl.BlockSpec((1, D), lambda b_, i: (0, 0)),
            pl.BlockSpec((len(POOL_WINDOWS), cg, cg), lambda b_, i: (0, 0, 0)),
            pl.BlockSpec((1, D), lambda b_, i: (0, 0)),
            pl.BlockSpec((1, D), lambda b_, i: (0, 0)),
        ],
        out_specs=pl.BlockSpec((1, ts, D), lambda b_, i: (b_, i, 0)),
        compiler_params=_cparams(("parallel", "parallel"), 40),
        name="pool_mixer",
    )(h, h, row(g), w.astype(BF16), row(b), row(sc))


def _router_kernel(h_ref, g_ref, wh_ref, wl_ref, b_ref, info_ref, cnt_ref, *, tt):
    i = pl.program_id(0)

    @pl.when(i == 0)
    def _():
        cnt_ref[...] = jnp.zeros_like(cnt_ref)

    xn = _rms(h_ref[...], g_ref[...])
    xh = xn.astype(BF16)
    xl = (xn - xh.astype(F32)).astype(BF16)
    logits = _dot(xh, wh_ref[...]) + (_dot(xh, wl_ref[...]) + _dot(xl, wh_ref[...])) + b_ref[...]
    lane = lax.broadcasted_iota(I32, (tt, LANES), 1)
    neg = -jnp.inf
    gl = jnp.where(lane < N_EXPERT_GROUPS, logits, neg)
    gmax = jnp.max(gl, axis=-1, keepdims=True)
    grp = jnp.min(jnp.where(gl == gmax, lane, LANES), axis=-1, keepdims=True)
    gprob = 1.0 / jnp.sum(jnp.exp(gl - gmax), axis=-1, keepdims=True)
    lo = N_EXPERT_GROUPS + grp * EXPERTS_PER_GROUP
    el = jnp.where(lane >= lo, jnp.where(lane < lo + EXPERTS_PER_GROUP, logits, neg), neg)
    v1 = jnp.max(el, axis=-1, keepdims=True)
    i1 = jnp.min(jnp.where(el == v1, lane, LANES), axis=-1, keepdims=True)
    el2 = jnp.where(lane == i1, neg, el)
    v2 = jnp.max(el2, axis=-1, keepdims=True)
    i2 = jnp.min(jnp.where(el2 == v2, lane, LANES), axis=-1, keepdims=True)
    e2 = jnp.exp(v2 - v1)
    w1 = gprob / (1.0 + e2)
    w2 = gprob * e2 / (1.0 + e2)
    oh1 = lane == i1
    oh2 = lane == i2
    oh = jnp.where(oh1, 1.0, jnp.where(oh2, 1.0, 0.0))
    r_ = lax.broadcasted_iota(I32, (tt, tt), 0)
    c_ = lax.broadcasted_iota(I32, (tt, tt), 1)
    tri = jnp.where(r_ > c_, 1.0, 0.0).astype(BF16)
    tot = _dot(tri, oh.astype(BF16)) + cnt_ref[...]
    r1 = jnp.sum(jnp.where(oh1, tot, 0.0), axis=-1, keepdims=True)
    r2 = jnp.sum(jnp.where(oh2, tot, 0.0), axis=-1, keepdims=True)
    cnt_ref[...] = cnt_ref[...] + jnp.sum(oh, axis=0, keepdims=True)
    vals = (i1.astype(F32) - N_EXPERT_GROUPS, i2.astype(F32) - N_EXPERT_GROUPS, w1, w2, r1, r2)
    info = jnp.zeros((tt, LANES), F32)
    for k, v in enumerate(vals):
        info = jnp.where(lane == k, v, info)
    info_ref[...] = info


def _router(h2, g, wr, br):
    T, D = h2.shape
    tt = min(TOK_TILE, T)
    return pl.pallas_call(
        functools.partial(_router_kernel, tt=tt),
        out_shape=(jax.ShapeDtypeStruct((T, LANES), F32), jax.ShapeDtypeStruct((1, LANES), F32)),
        grid=(T // tt,),
        in_specs=[
            pl.BlockSpec((tt, D), lambda i: (i, 0)),
            pl.BlockSpec((1, D), lambda i: (0, 0)),
            pl.BlockSpec((D, LANES), lambda i: (0, 0)),
            pl.BlockSpec((D, LANES), lambda i: (0, 0)),
            pl.BlockSpec((1, LANES), lambda i: (0, 0)),
        ],
        out_specs=(pl.BlockSpec((tt, LANES), lambda i: (i, 0)), pl.BlockSpec((1, LANES), lambda i: (0, 0))),
        compiler_params=_cparams(("arbitrary",), 40),
        name="moe_router",
    )(h2, g.reshape(1, D), wr.astype(BF16), (wr - wr.astype(BF16).astype(F32)).astype(BF16), br)


def _row_copy(src, s, dst, d, sem):
    return pltpu.make_async_copy(src.at[pl.ds(s, 1)], dst.at[pl.ds(d, 1)], sem)


def _dispatch_kernel(dest_ref, cend_ref, h_ref, g_ref, xs_ref, xn_s, sem, *, tt, n_tok, n_chunks):
    base = pl.program_id(0) * tt

    @pl.when(pl.program_id(0) == 0)
    def _():
        xn_s[...] = jnp.zeros_like(xn_s)

        def tail(e):
            nonempty = cend_ref[e] > (cend_ref[e - 1] if e > 0 else 0)
            row = pl.multiple_of((cend_ref[e] - 1) * tt, tt)
            return nonempty, pltpu.make_async_copy(xn_s, xs_ref.at[pl.ds(row, tt)], sem)

        def unused(c):
            return pltpu.make_async_copy(xn_s, xs_ref.at[pl.ds(pl.multiple_of(c * tt, tt), tt)], sem)

        n_used = cend_ref[N_EXPERTS - 1]
        for e in range(N_EXPERTS):
            nonempty, cp = tail(e)
            pl.when(nonempty)(cp.start)
        lax.fori_loop(n_used, n_chunks, lambda c, z: (unused(c).start(), z)[1], 0)
        for e in range(N_EXPERTS):
            nonempty, cp = tail(e)
            pl.when(nonempty)(cp.wait)
        lax.fori_loop(n_used, n_chunks, lambda c, z: (unused(c).wait(), z)[1], 0)

    xn_s[...] = _rms(h_ref[...], g_ref[...])

    for r in range(tt):
        for k in range(TOP_K_IN_GROUP):
            _row_copy(xn_s, r, xs_ref, dest_ref[k * n_tok + base + r], sem).start(priority=k % 2)
    for k in range(TOP_K_IN_GROUP):
        pltpu.make_async_copy(xn_s, xs_ref.at[pl.ds(0, tt)], sem).wait()


def _dispatch(dest, chunk_end, h2, g, n_rows):
    T, D = h2.shape
    tt = MOE_CHUNK
    assert T % tt == 0
    return pl.pallas_call(
        functools.partial(_dispatch_kernel, tt=tt, n_tok=T, n_chunks=n_rows // tt),
        out_shape=jax.ShapeDtypeStruct((n_rows, D), F32),
        grid_spec=pltpu.PrefetchScalarGridSpec(
            num_scalar_prefetch=2,
            grid=(T // tt,),
            in_specs=[
                pl.BlockSpec((tt, D), lambda i, d, ce: (i, 0)),
                pl.BlockSpec((1, D), lambda i, d, ce: (0, 0)),
            ],
            out_specs=pl.BlockSpec(memory_space=pl.ANY),
            scratch_shapes=[pltpu.VMEM((tt, D), F32), pltpu.SemaphoreType.DMA],
        ),
        compiler_params=_cparams(("arbitrary",), 40),
        name="moe_dispatch",
    )(dest, chunk_end, h2, g.reshape(1, D))


def _expert_kernel(ce_ref, nv_ref, xs_ref, w1_ref, w3_ref, w2_ref, o_ref, w1b, w3b, w2b):
    c = pl.program_id(0)
    e = ce_ref[c]
    prev = ce_ref[jnp.maximum(c - 1, 0)]

    @pl.when(jnp.logical_or(c == 0, e != prev))
    def _():
        w1b[...] = w1_ref[0, 0].astype(BF16)
        w3b[...] = w3_ref[0, 0].astype(BF16)
        w2b[...] = w2_ref[0, 0].astype(BF16)

    @pl.when(c < nv_ref[0])
    def _():
        x = xs_ref[...].astype(BF16)
        a = _dot(x, w1b[...])
        b = _dot(x, w3b[...])
        hc = a * jax.nn.sigmoid(a) * b
        o_ref[...] = _dot(hc.astype(BF16), w2b[...])

    @pl.when(c >= nv_ref[0])
    def _():
        o_ref[...] = jnp.zeros_like(o_ref)


def _experts(chunk_e, n_valid, xs, w1, w3, w2, layer):
    P, D = xs.shape
    F = w1.shape[-1]
    ch = MOE_CHUNK
    rows = lambda c, ce, nv: (jnp.minimum(c, nv[0] - 1), 0)
    wsel = lambda c, ce, nv: (layer, ce[c], 0, 0)
    return pl.pallas_call(
        _expert_kernel,
        out_shape=jax.ShapeDtypeStruct((P, D), F32),
        grid_spec=pltpu.PrefetchScalarGridSpec(
            num_scalar_prefetch=2,
            grid=(P // ch,),
            in_specs=[
                pl.BlockSpec((ch, D), rows),
                pl.BlockSpec((1, 1, D, F), wsel),
                pl.BlockSpec((1, 1, D, F), wsel),
                pl.BlockSpec((1, 1, F, D), wsel),
            ],
            out_specs=pl.BlockSpec((ch, D), lambda c, ce, nv: (c, 0)),
            scratch_shapes=[pltpu.VMEM((D, F), BF16), pltpu.VMEM((D, F), BF16), pltpu.VMEM((F, D), BF16)],
        ),
        compiler_params=_cparams(("arbitrary",), 56),
        name="moe_experts",
    )(chunk_e, n_valid, xs, w1, w3, w2)


def _combine_kernel(dest_ref, h_ref, info_ref, rows_ref, p_ref, g_ref, gw_ref, gb_ref, pw_ref, fn_ref,
                    o_ref, buf_a, buf_b, sem, *, tt, n_tok, n_steps, final):
    i = pl.program_id(0)
    bufs = (buf_a, buf_b)

    def row(tile, sl, k, r):
        return _row_copy(rows_ref, dest_ref[k * n_tok + tile * tt + r], bufs[sl].at[k], r, sem.at[sl])

    def wait_rows(sl):
        for k in range(TOP_K_IN_GROUP):
            pltpu.make_async_copy(rows_ref.at[pl.ds(0, tt)], bufs[sl].at[k], sem.at[sl]).wait()

    @pl.when(i == 0)
    def _():
        def issue(r, c):
            for k in range(TOP_K_IN_GROUP):
                row(0, 0, k, r).start(priority=k % 2)
            return c

        lax.fori_loop(0, tt, issue, 0, unroll=8)

    def step(sl):
        wait_rows(sl)
        nxt = jnp.minimum(i + 1, n_steps - 1)
        for r in range(tt):
            for k in range(TOP_K_IN_GROUP):
                row(nxt, 1 - sl, k, r).start(priority=k % 2)
        info = info_ref[...]
        y = h_ref[...] + info[:, 2:3] * bufs[sl][0] + info[:, 3:4] * bufs[sl][1]
        hn = _rms(y, g_ref[...])
        gate = jax.nn.sigmoid(_dot(hn.astype(BF16), gw_ref[...]) + gb_ref[...])
        out = y + _dot(p_ref[0].astype(BF16), pw_ref[...]) * gate
        if final:
            out = _rms(out, fn_ref[...])
        o_ref[...] = out
        pl.when(i == n_steps - 1)(functools.partial(wait_rows, 1 - sl))

    for parity in range(2):
        pl.when(i % 2 == parity)(functools.partial(step, parity))


def _combine(dest, h2, info, rows, p3, layer, g, gw, gb, pw, fn, final):
    T, D = h2.shape
    PD = p3.shape[-1]
    tt = min(CMB_TILE, T)
    full = lambda i, d: (0, 0)
    return pl.pallas_call(
        functools.partial(_combine_kernel, tt=tt, n_tok=T, n_steps=T // tt, final=final),
        out_shape=jax.ShapeDtypeStruct((T, D), F32),
        grid_spec=pltpu.PrefetchScalarGridSpec(
            num_scalar_prefetch=1,
            grid=(T // tt,),
            in_specs=[
                pl.BlockSpec((tt, D), lambda i, d: (i, 0)),
                pl.BlockSpec((tt, LANES), lambda i, d: (i, 0)),
                pl.BlockSpec(memory_space=pl.ANY),
                pl.BlockSpec((1, tt, PD), lambda i, d: (layer, i, 0)),
                pl.BlockSpec((1, D), full),
                pl.BlockSpec((D, D), full),
                pl.BlockSpec((1, D), full),
                pl.BlockSpec((PD, D), full),
                pl.BlockSpec((1, D), full),
            ],
            out_specs=pl.BlockSpec((tt, D), lambda i, d: (i, 0)),
            scratch_shapes=[pltpu.VMEM((TOP_K_IN_GROUP, tt, D), F32), pltpu.VMEM((TOP_K_IN_GROUP, tt, D), F32),
                            pltpu.SemaphoreType.DMA((2,))],
        ),
        compiler_params=_cparams(("arbitrary",), 40),
        name="moe_combine_ple",
    )(dest, h2, info, rows, p3, g.reshape(1, D), gw.astype(BF16), gb.reshape(1, D), pw.astype(BF16),
      fn.reshape(1, D))


IDX_SLOTS = 3


def _expert2_kernel(ce_ref, nv_ref, idx_hbm, h_hbm, g_ref, w1_ref, w3_ref, w2_ref, y_hbm,
                    idx_s, x_a, x_b, o_a, o_b, w1b, w3b, w2b, idx_sem, g_sem, s_sem, *, ch, n_tok, n_chunks):
    c = pl.program_id(0)
    nv = nv_ref[0]
    x_buf = (x_a, x_b)
    o_buf = (o_a, o_b)

    def idx_copy(chunk, sl):
        return pltpu.make_async_copy(idx_hbm.at[chunk], idx_s.at[sl], idx_sem.at[sl])

    def gather(chunk_slot, sl, r):
        return pltpu.make_async_copy(h_hbm.at[pl.ds(idx_s[chunk_slot, 0, r], 1)], x_buf[sl].at[pl.ds(r, 1)],
                                     g_sem.at[sl])

    def scatter(chunk_slot, sl, r):
        return pltpu.make_async_copy(o_buf[sl].at[pl.ds(r, 1)], y_hbm.at[pl.ds(idx_s[chunk_slot, 1, r], 1)],
                                     s_sem.at[sl])

    def wait_gathers(sl):
        pltpu.make_async_copy(h_hbm.at[pl.ds(0, ch)], x_buf[sl], g_sem.at[sl]).wait()

    def wait_scatters(sl):
        pltpu.make_async_copy(o_buf[sl], y_hbm.at[pl.ds(0, ch)], s_sem.at[sl]).wait()

    @pl.when(c == 0)
    def _():
        o_a[...] = jnp.zeros_like(o_a)
        o_b[...] = jnp.zeros_like(o_b)
        pltpu.make_async_copy(o_a, y_hbm.at[pl.ds(2 * n_tok, ch)], s_sem.at[0]).start()
        first = idx_copy(0, 0)
        first.start()
        first.wait()
        lax.fori_loop(0, ch, lambda r, z: (gather(0, 0, r).start(), z)[1], 0, unroll=8)
        null = idx_copy(n_chunks, IDX_SLOTS - 1)
        null.start()
        null.wait()
        idx_copy(jnp.minimum(1, n_chunks), 1).start()

    e = ce_ref[c]
    prev = ce_ref[jnp.maximum(c - 1, 0)]

    @pl.when(jnp.logical_or(c == 0, e != prev))
    def _():
        w1b[...] = w1_ref[0, 0].astype(BF16)
        w3b[...] = w3_ref[0, 0].astype(BF16)
        w2b[...] = w2_ref[0, 0].astype(BF16)

    def chunk_step(slot):
        other = 1 - slot
        cur3 = c % IDX_SLOTS
        nxt3 = (c + 1) % IDX_SLOTS
        prv3 = (c + IDX_SLOTS - 1) % IDX_SLOTS
        wait_scatters(slot)
        wait_gathers(slot)
        idx_copy(0, nxt3).wait()
        for r in range(ch):
            gather(nxt3, other, r).start()
            scatter(prv3, other, r).start()
        x = x_buf[slot][...]
        x = (x * lax.rsqrt(jnp.mean(x * x, axis=-1, keepdims=True) + RMS_EPS) * g_ref[...]).astype(BF16)
        a = _dot(x, w1b[...])
        b = _dot(x, w3b[...])
        hc = a * jax.nn.sigmoid(a) * b
        o_buf[slot][...] = _dot(hc.astype(BF16), w2b[...])
        idx_copy(jnp.minimum(c + 2, n_chunks), prv3).start()

        @pl.when(c == nv - 1)
        def _():
            lax.fori_loop(0, ch, lambda r, z: (scatter(cur3, slot, r).start(), z)[1], 0, unroll=8)
            wait_scatters(other)
            wait_scatters(slot)
            wait_gathers(other)
            idx_copy(0, prv3).wait()

    for parity in range(2):
        pl.when(jnp.logical_and(c < nv, c % 2 == parity))(functools.partial(chunk_step, parity))


def _experts2(chunk_e, n_valid, idx, h2, g, w1, w3, w2, layer):
    T, D = h2.shape
    F = w1.shape[-1]
    ch = MOE_CHUNK
    n_chunks = idx.shape[0] - 1
    wsel = lambda c, ce, nv: (layer, ce[c], 0, 0)
    return pl.pallas_call(
        functools.partial(_expert2_kernel, ch=ch, n_tok=T, n_chunks=n_chunks),
        out_shape=jax.ShapeDtypeStruct((TOP_K_IN_GROUP * T + 2 * ch, D), F32),
        grid_spec=pltpu.PrefetchScalarGridSpec(
            num_scalar_prefetch=2,
            grid=(n_chunks,),
            in_specs=[
                pl.BlockSpec(memory_space=pl.ANY),
                pl.BlockSpec(memory_space=pl.ANY),
                pl.BlockSpec((1, D), lambda c, ce, nv: (0, 0)),
                pl.BlockSpec((1, 1, D, F), wsel),
                pl.BlockSpec((1, 1, D, F), wsel),
                pl.BlockSpec((1, 1, F, D), wsel),
            ],
            out_specs=pl.BlockSpec(memory_space=pl.ANY),
            scratch_shapes=[
                pltpu.SMEM((IDX_SLOTS, 2, ch), I32),
                pltpu.VMEM((ch, D), F32), pltpu.VMEM((ch, D), F32),
                pltpu.VMEM((ch, D), F32), pltpu.VMEM((ch, D), F32),
                pltpu.VMEM((D, F), BF16), pltpu.VMEM((D, F), BF16), pltpu.VMEM((F, D), BF16),
                pltpu.SemaphoreType.DMA((IDX_SLOTS,)),
                pltpu.SemaphoreType.DMA((2,)),
                pltpu.SemaphoreType.DMA((2,)),
            ],
        ),
        compiler_params=_cparams(("arbitrary",), 56),
        name="moe_experts",
    )(chunk_e, n_valid, idx, h2, g.reshape(1, D), w1, w3, w2)


def _combine2_kernel(h_ref, info_ref, ya_ref, yb_ref, p_ref, g_ref, gw_ref, gb_ref, pw_ref, fn_ref, o_ref, *, final):
    info = info_ref[...]
    y = h_ref[...] + info[:, 2:3] * ya_ref[...] + info[:, 3:4] * yb_ref[...]
    hn = _rms(y, g_ref[...])
    gate = jax.nn.sigmoid(_dot(hn.astype(BF16), gw_ref[...]) + gb_ref[...])
    out = y + _dot(p_ref[0].astype(BF16), pw_ref[...]) * gate
    if final:
        out = _rms(out, fn_ref[...])
    o_ref[...] = out


def _combine2(h2, info, y2, p3, layer, g, gw, gb, pw, fn, final):
    T, D = h2.shape
    PD = p3.shape[-1]
    tt = min(SEQ_TILE, T)
    full = lambda i: (0, 0)
    return pl.pallas_call(
        functools.partial(_combine2_kernel, final=final),
        out_shape=jax.ShapeDtypeStruct((T, D), F32),
        grid=(T // tt,),
        in_specs=[
            pl.BlockSpec((tt, D), lambda i: (i, 0)),
            pl.BlockSpec((tt, LANES), lambda i: (i, 0)),
            pl.BlockSpec((tt, D), lambda i: (i, 0)),
            pl.BlockSpec((tt, D), lambda i: (T // tt + i, 0)),
            pl.BlockSpec((1, tt, PD), lambda i: (layer, i, 0)),
            pl.BlockSpec((1, D), full),
            pl.BlockSpec((D, D), full),
            pl.BlockSpec((1, D), full),
            pl.BlockSpec((PD, D), full),
            pl.BlockSpec((1, D), full),
        ],
        out_specs=pl.BlockSpec((tt, D), lambda i: (i, 0)),
        compiler_params=_cparams(("parallel",), 48),
        name="moe_combine_ple",
    )(h2, info, y2, y2, p3, g.reshape(1, D), gw.astype(BF16), gb.reshape(1, D), pw.astype(BF16), fn.reshape(1, D))


def _moe_ple_layer(h2, p3, layer, norm_ffn, rg_w, rg_b, re_w, re_b, w1, w3, w2, norm_ple, gate_w, gate_b, ple_proj,
                   final_norm, final):
    T, D = h2.shape
    A = T * TOP_K_IN_GROUP
    pad = LANES - N_EXPERT_GROUPS - N_EXPERTS
    wr = jnp.concatenate([rg_w, re_w, jnp.zeros((D, pad), F32)], axis=1)
    br = jnp.concatenate([rg_b, re_b, jnp.zeros((pad,), F32)]).reshape(1, LANES)
    info, cnt = _router(h2, norm_ffn, wr, br)
    counts = cnt[0, N_EXPERT_GROUPS:N_EXPERT_GROUPS + N_EXPERTS].astype(I32)
    n_chunks_e = (counts + MOE_CHUNK - 1) // MOE_CHUNK
    chunk_end = jnp.cumsum(n_chunks_e)
    pstarts = (chunk_end - n_chunks_e) * MOE_CHUNK
    n_chunks = -(-A // MOE_CHUNK) + N_EXPERTS
    n_valid = chunk_end[-1:].astype(I32)
    cidx = jnp.minimum(jnp.arange(n_chunks, dtype=I32), n_valid[0] - 1)
    chunk_e = jnp.sum((chunk_end[None, :] <= cidx[:, None]).astype(I32), axis=1)
    chunk_e = jnp.minimum(chunk_e, N_EXPERTS - 1)
    e_idx = info[:, 0:TOP_K_IN_GROUP].astype(I32)
    rank = info[:, 4:4 + TOP_K_IN_GROUP].astype(I32)
    dest = (pstarts[e_idx] + rank).T.reshape(A)
    xs = _dispatch(dest, chunk_end.astype(I32), h2, norm_ffn, n_chunks * MOE_CHUNK)
    rows = _experts(chunk_e, n_valid, xs, w1, w3, w2, layer)
    return _combine(dest, h2, info, rows, p3, layer, norm_ple, gate_w, gate_b, ple_proj, final_norm, final)


def _rope_rows(xt, cos, sin):
    x1 = xt[0:ROPE_HALF]
    x2 = xt[ROPE_HALF:ROPE_DIMS]
    return jnp.concatenate([x1 * cos - x2 * sin, x2 * cos + x1 * sin, xt[ROPE_DIMS:]], axis=0)


def _kv_kernel(h_ref, g_ref, wn_ref, wt_ref, cos_ref, sin_ref, cv_ref, kst_ref, vs_ref, kwt_ref, vw_ref, *, ts):
    i = pl.program_id(1)
    kvw = N_KV_GROUPS * HEAD_DIM
    hn = _rms(h_ref[0], g_ref[...]).astype(BF16)
    nat = _dot(hn, wn_ref[...])
    cv_ref[0] = nat[:, 0:2 * kvw]
    tr = _dot_nt(wt_ref[...], hn)
    cos = cos_ref[...]
    sin = sin_ref[...]
    blk = lax.broadcasted_iota(I32, (MAX_SEL_BLOCKS, ts), 0)
    pos = i * ts + lax.broadcasted_iota(I32, (MAX_SEL_BLOCKS, ts), 1)
    onehot = jnp.where(pos // SEL_BLOCK == blk, 1.0, 0.0).astype(BF16)
    for g in range(N_KV_GROUPS):
        c0 = g * HEAD_DIM
        kst_ref[0, g, 0:MAX_SEL_BLOCKS, :] = onehot
        kst_ref[0, g, MAX_SEL_BLOCKS:, :] = _rope_rows(tr[c0:c0 + HEAD_DIM], cos, sin).astype(BF16)
        kwt_ref[0, g] = _rope_rows(tr[kvw + c0:kvw + c0 + HEAD_DIM], cos, sin).astype(BF16)
        vs_ref[0, g] = nat[:, 2 * kvw + c0:2 * kvw + c0 + HEAD_DIM].astype(BF16)
        vw_ref[0, g] = nat[:, 3 * kvw + c0:3 * kvw + c0 + HEAD_DIM].astype(BF16)


def _rope_tables(pos):
    inv = jnp.float32(ROPE_THETA) ** (-jnp.arange(ROPE_HALF, dtype=F32) * 2.0 / ROPE_DIMS)
    ang = pos.astype(F32)[:, None] * inv[None, :]
    return jnp.cos(ang), jnp.sin(ang)


def _shared_kv_proj(h, kv_norm, w_kv):
    B, S, D = h.shape
    G, dh = N_KV_GROUPS, HEAD_DIM
    kvw = G * dh
    ts = min(SEQ_TILE, S)
    br = lambda k: w_kv[:, k * kvw:(k + 1) * kvw]
    w_nat = jnp.concatenate([br(0), br(1), br(3), br(5)], axis=1).astype(BF16)
    w_tr = jnp.concatenate([br(2), br(4)], axis=1).T.astype(BF16)
    cos, sin = _rope_tables(jnp.arange(S))
    kd = MAX_SEL_BLOCKS + dh
    return pl.pallas_call(
        functools.partial(_kv_kernel, ts=ts),
        out_shape=(
            jax.ShapeDtypeStruct((B, S, 2 * kvw), F32),
            jax.ShapeDtypeStruct((B, G, kd, S), BF16),
            jax.ShapeDtypeStruct((B, G, S, dh), BF16),
            jax.ShapeDtypeStruct((B, G, dh, S), BF16),
            jax.ShapeDtypeStruct((B, G, S, dh), BF16),
        ),
        grid=(B, S // ts),
        in_specs=[
            pl.BlockSpec((1, ts, D), lambda b, i: (b, i, 0)),
            pl.BlockSpec((1, D), lambda b, i: (0, 0)),
            pl.BlockSpec((D, 4 * kvw), lambda b, i: (0, 0)),
            pl.BlockSpec((2 * kvw, D), lambda b, i: (0, 0)),
            pl.BlockSpec((ROPE_HALF, ts), lambda b, i: (0, i)),
            pl.BlockSpec((ROPE_HALF, ts), lambda b, i: (0, i)),
        ],
        out_specs=(
            pl.BlockSpec((1, ts, 2 * kvw), lambda b, i: (b, i, 0)),
            pl.BlockSpec((1, G, kd, ts), lambda b, i: (b, 0, 0, i)),
            pl.BlockSpec((1, G, ts, dh), lambda b, i: (b, 0, i, 0)),
            pl.BlockSpec((1, G, dh, ts), lambda b, i: (b, 0, 0, i)),
            pl.BlockSpec((1, G, ts, dh), lambda b, i: (b, 0, i, 0)),
        ),
        compiler_params=_cparams(("parallel", "parallel"), 48),
        name="shared_kv_proj",
    )(h, kv_norm.reshape(1, D), w_nat, w_tr, cos.T, sin.T)


def _compress_kernel(x_ref, pos_ref, w1_ref, b1_ref, w2_ref, b2_ref, cos_ref, sin_ref, o_ref, *, nh, keys):
    x = x_ref[0, 0, 0]
    a = _dot((x + pos_ref[0:1]).astype(BF16), w1_ref[0])
    b = _dot((x + pos_ref[1:2]).astype(BF16), w1_ref[1])
    hid = jax.nn.gelu(a + pltpu.roll(b, nh - 1, axis=0) + b1_ref[...]).astype(BF16)
    if keys:
        out = _dot_nt(w2_ref[...], hid) + b2_ref[...]
        o_ref[0, 0] = _rope_rows(out, cos_ref[...], sin_ref[...]).astype(BF16)
    else:
        o_ref[0, 0] = (_dot(hid, w2_ref[...]) + b2_ref[...]).astype(BF16)


def _compress(halves, which, pos_emb, w1, b1, w2, b2, keys):
    _, B, G, nh, hw = halves.shape
    dh = HEAD_DIM
    hidden = w1.shape[-1]
    pos2 = pos_emb.reshape(2, hw)
    w1s = w1.reshape(2, hw, hidden).astype(BF16)
    cos, sin = _rope_tables(jnp.arange(nh) * CMP_STRIDE + CMP_BLOCK - 1)
    if keys:
        w2a, b2a = w2.T.astype(BF16), b2.reshape(dh, 1)
        out_shape, out_block, out_idx = (B, G, dh, nh), (1, 1, dh, nh), lambda b, g: (b, g, 0, 0)
    else:
        w2a, b2a = w2.astype(BF16), b2.reshape(1, dh)
        out_shape, out_block, out_idx = (B, G, nh, dh), (1, 1, nh, dh), lambda b, g: (b, g, 0, 0)
    c2 = lambda b, g: (0, 0)
    return pl.pallas_call(
        functools.partial(_compress_kernel, nh=nh, keys=keys),
        out_shape=jax.ShapeDtypeStruct(out_shape, BF16),
        grid=(B, G),
        in_specs=[
            pl.BlockSpec((1, 1, 1, nh, hw), lambda b, g: (which, b, g, 0, 0)),
            pl.BlockSpec((2, hw), c2),
            pl.BlockSpec((2, hw, hidden), lambda b, g: (0, 0, 0)),
            pl.BlockSpec((1, hidden), c2),
            pl.BlockSpec(w2a.shape, c2),
            pl.BlockSpec(b2a.shape, c2),
            pl.BlockSpec((ROPE_HALF, nh), c2),
            pl.BlockSpec((ROPE_HALF, nh), c2),
        ],
        out_specs=pl.BlockSpec(out_block, out_idx),
        compiler_params=_cparams(("parallel", "parallel"), 40),
        name="compress_k" if keys else "compress_v",
    )(halves, pos2, w1s, b1.reshape(1, hidden), w2a, b2a, cos.T, sin.T)


def _qproj_kernel(h_ref, g_ref, wq_ref, wg_ref, bg_ref, c_ref, s1_ref, s2_ref, q_ref, gt_ref):
    xn = _rms(h_ref[0], g_ref[...]).astype(BF16)
    q = _dot(xn, wq_ref[...])
    c, s1, s2 = c_ref[...], s1_ref[...], s2_ref[...]
    for k in range(q.shape[-1] // LANES):
        x = q[:, k * LANES:(k + 1) * LANES]
        r = x * c + pltpu.roll(x, ROPE_HALF, axis=1) * s1 + pltpu.roll(x, LANES - ROPE_HALF, axis=1) * s2
        q_ref[0, :, k * LANES:(k + 1) * LANES] = r.astype(BF16)
    gt_ref[0] = jax.nn.sigmoid(_dot(xn, wg_ref[...]) + bg_ref[...])


def _q_proj(h, g, w_qg, b_gate):
    B, S, D = h.shape
    HD = N_HEADS * HEAD_DIM
    ng = N_HEADS * N_BRANCH
    ts = min(SEQ_TILE, S)
    wq = w_qg[:, :HD].astype(BF16)
    wg = jnp.pad(w_qg[:, HD:], ((0, 0), (0, LANES - ng))).astype(BF16)
    bg = jnp.pad(b_gate, (0, LANES - ng)).reshape(1, LANES)
    cos, sin = _rope_tables(jnp.arange(S))
    scale = HEAD_DIM ** -0.5
    ones = jnp.ones((S, HEAD_DIM - ROPE_DIMS), F32)
    zeros = jnp.zeros((S, HEAD_DIM - ROPE_DIMS), F32)
    zh = jnp.zeros((S, ROPE_HALF), F32)
    two = lambda a: jnp.concatenate([a, a], axis=1) * scale
    c = two(jnp.concatenate([cos, cos, ones], axis=1))
    s1 = two(jnp.concatenate([zh, sin, zeros], axis=1))
    s2 = two(jnp.concatenate([-sin, zh, zeros], axis=1))
    tab = pl.BlockSpec((ts, LANES), lambda b, i: (i, 0))
    return pl.pallas_call(
        _qproj_kernel,
        out_shape=(jax.ShapeDtypeStruct((B, S, HD), BF16), jax.ShapeDtypeStruct((B, S, LANES), F32)),
        grid=(B, S // ts),
        in_specs=[
            pl.BlockSpec((1, ts, D), lambda b, i: (b, i, 0)),
            pl.BlockSpec((1, D), lambda b, i: (0, 0)),
            pl.BlockSpec((D, HD), lambda b, i: (0, 0)),
            pl.BlockSpec((D, LANES), lambda b, i: (0, 0)),
            pl.BlockSpec((1, LANES), lambda b, i: (0, 0)),
            tab, tab, tab,
        ],
        out_specs=(pl.BlockSpec((1, ts, HD), lambda b, i: (b, i, 0)),
                   pl.BlockSpec((1, ts, LANES), lambda b, i: (b, i, 0))),
        compiler_params=_cparams(("parallel", "parallel"), 40),
        name="nsa_q_proj",
    )(h, g.reshape(1, D), wq, wg, bg, c, s1, s2)


def _stack_heads(qb):
    return jnp.concatenate([qb[:, r * HEAD_DIM:(r + 1) * HEAD_DIM] for r in range(HEADS_PER_GROUP)], axis=0)


def _gated_unstack(o, gates, g, branch, tq):
    lane = lax.broadcasted_iota(I32, gates.shape, 1)
    cols = []
    for r in range(HEADS_PER_GROUP):
        col = (g * HEADS_PER_GROUP + r) * N_BRANCH + branch
        gate = jnp.sum(jnp.where(lane == col, gates, 0.0), axis=-1, keepdims=True)
        cols.append(o[r * tq:(r + 1) * tq] * gate)
    return jnp.concatenate(cols, axis=-1)


def _query_pos(i, tq):
    rows = lax.broadcasted_iota(I32, (HEADS_PER_GROUP * tq, 1), 0)
    return i * tq + (rows & (tq - 1))


def _cmp_kernel(q_ref, gt_ref, k_ref, v_ref, o_ref, bias_ref, *, tq, nc):
    g = pl.program_id(1)
    i = pl.program_id(2)
    qs = _stack_heads(q_ref[0])
    s = _dot(qs, k_ref[0, 0])
    t = _query_pos(i, tq)
    n = lax.broadcasted_iota(I32, (1, nc), 1)
    s = jnp.where(n * CMP_STRIDE + (CMP_BLOCK - 1) <= t, s, -jnp.inf)
    m = jnp.max(s, axis=-1, keepdims=True)
    m = jnp.where(m == -jnp.inf, 0.0, m)
    e = jnp.exp(s - m)
    p = e / jnp.maximum(jnp.sum(e, axis=-1, keepdims=True), 1e-30)
    o = _dot(p.astype(BF16), v_ref[0, 0])
    o_ref[0] = _gated_unstack(o, gt_ref[0], g, 0, tq).astype(BF16)
    ps = p[0:tq]
    for r in range(1, HEADS_PER_GROUP):
        ps = ps + p[r * tq:(r + 1) * tq]
    nn = lax.broadcasted_iota(I32, (nc, MAX_SEL_BLOCKS), 0) * CMP_STRIDE
    jj = lax.broadcasted_iota(I32, (nc, MAX_SEL_BLOCKS), 1) * SEL_BLOCK
    c2s = jnp.where(nn < jj + SEL_BLOCK, jnp.where(nn + CMP_BLOCK > jj, 1.0, 0.0), 0.0)
    imp = jnp.dot(ps, c2s, precision=HIGHEST, preferred_element_type=F32)
    tq_pos = i * tq + lax.broadcasted_iota(I32, (tq, 1), 0)
    cur = tq_pos // SEL_BLOCK
    j = lax.broadcasted_iota(I32, (tq, MAX_SEL_BLOCKS), 1)
    valid = j <= cur
    bonus = jnp.where(j == 0, FORCE_BONUS, jnp.where(j == cur, FORCE_BONUS, jnp.where(j == cur - 1, FORCE_BONUS, 0.0)))
    work = jnp.where(valid, imp + bonus, -jnp.inf)
    sel = jnp.zeros((tq, MAX_SEL_BLOCKS), F32)
    for _ in range(N_SELECT):
        mx = jnp.max(work, axis=-1, keepdims=True)
        idx = jnp.min(jnp.where(work == mx, j, MAX_SEL_BLOCKS), axis=-1, keepdims=True)
        pick = j == idx
        sel = jnp.where(pick, 1.0, sel)
        work = jnp.where(pick, -jnp.inf, work)
    bias_ref[0, 0] = jnp.where(valid, jnp.where(sel > 0.0, 0.0, MASK_BIAS), MASK_BIAS).astype(BF16)


def _cmp_attn(q, gates, kct, vc):
    B, S, HD = q.shape
    G, dh = N_KV_GROUPS, HEAD_DIM
    gw = HD // G
    nc = kct.shape[-1]
    tq = min(Q_TILE, S)
    return pl.pallas_call(
        functools.partial(_cmp_kernel, tq=tq, nc=nc),
        out_shape=(jax.ShapeDtypeStruct((B, S, HD), BF16), jax.ShapeDtypeStruct((B, G, S, MAX_SEL_BLOCKS), BF16)),
        grid=(B, G, S // tq),
        in_specs=[
            pl.BlockSpec((1, tq, gw), lambda b, g, i: (b, i, g)),
            pl.BlockSpec((1, tq, LANES), lambda b, g, i: (b, i, 0)),
            pl.BlockSpec((1, 1, dh, nc), lambda b, g, i: (b, g, 0, 0)),
            pl.BlockSpec((1, 1, nc, dh), lambda b, g, i: (b, g, 0, 0)),
        ],
        out_specs=(pl.BlockSpec((1, tq, gw), lambda b, g, i: (b, i, g)),
                   pl.BlockSpec((1, 1, tq, MAX_SEL_BLOCKS), lambda b, g, i: (b, g, i, 0))),
        compiler_params=_cparams(("parallel", "parallel", "parallel"), 40),
        name="nsa_compressed",
    )(q, gates, kct, vc)


def _sel_kernel(q_ref, bias_ref, gt_ref, k_ref, v_ref, o_ref, qa, m_s, l_s, acc, *, tq, tk):
    g = pl.program_id(1)
    i = pl.program_id(2)
    qb = q_ref[0]
    bias = bias_ref[0, 0]
    for r in range(HEADS_PER_GROUP):
        qa[r * tq:(r + 1) * tq, 0:MAX_SEL_BLOCKS] = bias
        qa[r * tq:(r + 1) * tq, MAX_SEL_BLOCKS:] = qb[:, r * HEAD_DIM:(r + 1) * HEAD_DIM]
    m_s[...] = jnp.full_like(m_s, -jnp.inf)
    l_s[...] = jnp.zeros_like(l_s)
    acc[...] = jnp.zeros_like(acc)
    t = _query_pos(i, tq)
    lane = lax.broadcasted_iota(I32, (1, tk), 1)

    def body(jt, c):
        k0 = pl.multiple_of(jt * tk, tk)
        s = _dot(qa[...], k_ref[0, 0, :, pl.ds(k0, tk)])
        s = jnp.where(k0 + lane <= t, s, -jnp.inf)
        m_old = m_s[...]
        m_new = jnp.maximum(m_old, jnp.max(s, axis=-1, keepdims=True))
        alpha = jnp.exp(m_old - m_new)
        p = jnp.exp(s - m_new)
        l_s[...] = alpha * l_s[...] + jnp.sum(p, axis=-1, keepdims=True)
        acc[...] = alpha * acc[...] + _dot(p.astype(BF16), v_ref[0, 0, pl.ds(k0, tk), :])
        m_s[...] = m_new
        return c

    lax.fori_loop(0, (i * tq + tq + tk - 1) // tk, body, 0)
    o_ref[0] = _gated_unstack(acc[...] / l_s[...], gt_ref[0], g, 1, tq).astype(BF16)


def _sel_attn(q, bias, gates, kst, vs):
    B, S, HD = q.shape
    G, dh = N_KV_GROUPS, HEAD_DIM
    gw = HD // G
    kd = kst.shape[2]
    tq = min(Q_TILE, S)
    tk = min(KEY_TILE, S)
    rows = HEADS_PER_GROUP * tq
    return pl.pallas_call(
        functools.partial(_sel_kernel, tq=tq, tk=tk),
        out_shape=jax.ShapeDtypeStruct((B, S, HD), BF16),
        grid=(B, G, S // tq),
        in_specs=[
            pl.BlockSpec((1, tq, gw), lambda b, g, i: (b, i, g)),
            pl.BlockSpec((1, 1, tq, MAX_SEL_BLOCKS), lambda b, g, i: (b, g, i, 0)),
            pl.BlockSpec((1, tq, LANES), lambda b, g, i: (b, i, 0)),
            pl.BlockSpec((1, 1, kd, S), lambda b, g, i: (b, g, 0, 0)),
            pl.BlockSpec((1, 1, S, dh), lambda b, g, i: (b, g, 0, 0)),
        ],
        out_specs=pl.BlockSpec((1, tq, gw), lambda b, g, i: (b, i, g)),
        scratch_shapes=[pltpu.VMEM((rows, kd), BF16), pltpu.VMEM((rows, 1), F32), pltpu.VMEM((rows, 1), F32),
                        pltpu.VMEM((rows, dh), F32)],
        compiler_params=_cparams(("parallel", "parallel", "arbitrary"), 48),
        name="nsa_selected",
    )(q, bias, gates, kst, vs)


def _win_kernel(q_ref, gt_ref, k_ref, v_ref, o_ref, *, tq, wb):
    g = pl.program_id(1)
    i = pl.program_id(2)
    k0 = pl.multiple_of(jnp.maximum(i * tq - WINDOW, 0), tq)
    qs = _stack_heads(q_ref[0])
    s = _dot(qs, k_ref[0, 0, :, pl.ds(k0, wb)])
    t = _query_pos(i, tq)
    kpos = k0 + lax.broadcasted_iota(I32, (1, wb), 1)
    s = jnp.where(kpos <= t, jnp.where(kpos > t - WINDOW, s, -jnp.inf), -jnp.inf)
    m = jnp.max(s, axis=-1, keepdims=True)
    e = jnp.exp(s - m)
    p = e / jnp.sum(e, axis=-1, keepdims=True)
    o = _dot(p.astype(BF16), v_ref[0, 0, pl.ds(k0, wb), :])
    o_ref[0] = _gated_unstack(o, gt_ref[0], g, 2, tq).astype(BF16)


def _win_attn(q, gates, kwt, vw):
    B, S, HD = q.shape
    G, dh = N_KV_GROUPS, HEAD_DIM
    gw = HD // G
    tq = min(Q_TILE, S)
    wb = WINDOW + tq
    assert S >= wb and tq % LANES == 0
    return pl.pallas_call(
        functools.partial(_win_kernel, tq=tq, wb=wb),
        out_shape=jax.ShapeDtypeStruct((B, S, HD), BF16),
        grid=(B, G, S // tq),
        in_specs=[
            pl.BlockSpec((1, tq, gw), lambda b, g, i: (b, i, g)),
            pl.BlockSpec((1, tq, LANES), lambda b, g, i: (b, i, 0)),
            pl.BlockSpec((1, 1, dh, S), lambda b, g, i: (b, g, 0, 0)),
            pl.BlockSpec((1, 1, S, dh), lambda b, g, i: (b, g, 0, 0)),
        ],
        out_specs=pl.BlockSpec((1, tq, gw), lambda b, g, i: (b, i, g)),
        compiler_params=_cparams(("parallel", "parallel", "parallel"), 40),
        name="nsa_window",
    )(q, gates, kwt, vw)


LOG2E = 1.4426950408889634
Q_SCALE = HEAD_DIM ** -0.5 * LOG2E
V_ROWS = HEAD_DIM + 16
KEY_AUG = 2 * LANES
NEG_BIG = -1e30


def _lane_rope_tables(pos):
    cos, sin = _rope_tables(pos)
    n = pos.shape[0]
    ones = jnp.ones((n, HEAD_DIM - ROPE_DIMS), F32)
    zeros = jnp.zeros((n, HEAD_DIM - ROPE_DIMS), F32)
    zh = jnp.zeros((n, ROPE_HALF), F32)
    two = lambda a: jnp.concatenate([a, a], axis=1)
    return (two(jnp.concatenate([cos, cos, ones], axis=1)), two(jnp.concatenate([zh, sin, zeros], axis=1)),
            two(jnp.concatenate([-sin, zh, zeros], axis=1)))


def _rope_lanes(x, c, s1, s2):
    return x * c + pltpu.roll(x, ROPE_HALF, axis=1) * s1 + pltpu.roll(x, LANES - ROPE_HALF, axis=1) * s2


def _kv2_kernel(h_ref, g_ref, wn_ref, wt_ref, c_ref, s1_ref, s2_ref, cv_ref, ksa_ref, vst_ref, kw_ref, vwt_ref, *, ts):
    i = pl.program_id(1)
    kvw = N_KV_GROUPS * HEAD_DIM
    hn = _rms(h_ref[0], g_ref[...]).astype(BF16)
    nat = _dot(hn, wn_ref[...])
    cv_ref[0] = nat[:, 0:2 * kvw]
    tr = _dot_nt(wt_ref[...], hn)
    c, s1, s2 = c_ref[...], s1_ref[...], s2_ref[...]
    roped = [_rope_lanes(nat[:, 2 * kvw + k * LANES:2 * kvw + (k + 1) * LANES], c, s1, s2)
             for k in range(2 * kvw // LANES)]
    lane = lax.broadcasted_iota(I32, (ts, KEY_AUG), 1)
    pos = i * ts + lax.broadcasted_iota(I32, (ts, KEY_AUG), 0)
    onehot = jnp.where(lane - HEAD_DIM == pos // SEL_BLOCK, 1.0, 0.0).astype(BF16)
    ones_row = jnp.where(lax.broadcasted_iota(I32, (V_ROWS - HEAD_DIM, ts), 0) == 0, 1.0, 0.0).astype(BF16)
    per_tile = LANES // HEAD_DIM
    for g in range(N_KV_GROUPS):
        lo = (g % per_tile) * HEAD_DIM
        ksa_ref[0, g] = onehot
        ksa_ref[0, g, :, 0:HEAD_DIM] = roped[g // per_tile][:, lo:lo + HEAD_DIM].astype(BF16)
        kw_ref[0, g] = roped[N_KV_GROUPS // per_tile + g // per_tile][:, lo:lo + HEAD_DIM].astype(BF16)
        for ref, base in ((vst_ref, 0), (vwt_ref, kvw)):
            ref[0, g, 0:HEAD_DIM, :] = tr[base + g * HEAD_DIM:base + (g + 1) * HEAD_DIM].astype(BF16)
            ref[0, g, HEAD_DIM:, :] = ones_row


def _shared_kv_proj2(h, kv_norm, w_kv):
    B, S, D = h.shape
    G, dh = N_KV_GROUPS, HEAD_DIM
    kvw = G * dh
    ts = min(SEQ_TILE, S)
    br = lambda k: w_kv[:, k * kvw:(k + 1) * kvw]
    w_nat = jnp.concatenate([br(0), br(1), br(2), br(4)], axis=1).astype(BF16)
    w_tr = jnp.concatenate([br(3), br(5)], axis=1).T.astype(BF16)
    tabs = _lane_rope_tables(jnp.arange(S))
    tab = pl.BlockSpec((ts, LANES), lambda b, i: (i, 0))
    return pl.pallas_call(
        functools.partial(_kv2_kernel, ts=ts),
        out_shape=(
            jax.ShapeDtypeStruct((B, S, 2 * kvw), F32),
            jax.ShapeDtypeStruct((B, G, S, KEY_AUG), BF16),
            jax.ShapeDtypeStruct((B, G, V_ROWS, S), BF16),
            jax.ShapeDtypeStruct((B, G, S, dh), BF16),
            jax.ShapeDtypeStruct((B, G, V_ROWS, S), BF16),
        ),
        grid=(B, S // ts),
        in_specs=[
            pl.BlockSpec((1, ts, D), lambda b, i: (b, i, 0)),
            pl.BlockSpec((1, D), lambda b, i: (0, 0)),
            pl.BlockSpec((D, 4 * kvw), lambda b, i: (0, 0)),
            pl.BlockSpec((2 * kvw, D), lambda b, i: (0, 0)),
            tab, tab, tab,
        ],
        out_specs=(
            pl.BlockSpec((1, ts, 2 * kvw), lambda b, i: (b, i, 0)),
            pl.BlockSpec((1, G, ts, KEY_AUG), lambda b, i: (b, 0, i, 0)),
            pl.BlockSpec((1, G, V_ROWS, ts), lambda b, i: (b, 0, 0, i)),
            pl.BlockSpec((1, G, ts, dh), lambda b, i: (b, 0, i, 0)),
            pl.BlockSpec((1, G, V_ROWS, ts), lambda b, i: (b, 0, 0, i)),
        ),
        compiler_params=_cparams(("parallel", "parallel"), 48),
        name="shared_kv_proj",
    )(h, kv_norm.reshape(1, D), w_nat, w_tr, *tabs)


def _compress2_kernel(x_ref, pos_ref, w1_ref, b1_ref, w2_ref, b2_ref, c_ref, s1_ref, s2_ref, o_ref, *, nh, keys):
    x = x_ref[0, 0, 0]
    a = _dot((x + pos_ref[0:1]).astype(BF16), w1_ref[0])
    b = _dot((x + pos_ref[1:2]).astype(BF16), w1_ref[1])
    hid = jax.nn.gelu(a + pltpu.roll(b, nh - 1, axis=0) + b1_ref[...]).astype(BF16)
    if keys:
        out = _rope_lanes(_dot(hid, w2_ref[...]) + b2_ref[...], c_ref[...], s1_ref[...], s2_ref[...])
        o_ref[0, 0] = out[:, 0:HEAD_DIM].astype(BF16)
    else:
        o_ref[0, 0] = (_dot_nt(w2_ref[...], hid) + b2_ref[...]).astype(BF16)


def _compress2(halves, which, pos_emb, w1, b1, w2, b2, keys):
    _, B, G, nh, hw = halves.shape
    dh = HEAD_DIM
    hidden = w1.shape[-1]
    pos2 = pos_emb.reshape(2, hw)
    w1s = w1.reshape(2, hw, hidden).astype(BF16)
    tabs = _lane_rope_tables(jnp.arange(nh) * CMP_STRIDE + CMP_BLOCK - 1)
    if keys:
        w2a = jnp.pad(w2, ((0, 0), (0, LANES - dh))).astype(BF16)
        b2a = jnp.pad(b2, (0, LANES - dh)).reshape(1, LANES)
        out_shape, out_block = (B, G, nh, dh), (1, 1, nh, dh)
    else:
        w2a, b2a = w2.T.astype(BF16), b2.reshape(dh, 1)
        out_shape, out_block = (B, G, dh, nh), (1, 1, dh, nh)
    c2 = lambda b, g: (0, 0)
    tab = pl.BlockSpec((nh, LANES), c2)
    return pl.pallas_call(
        functools.partial(_compress2_kernel, nh=nh, keys=keys),
        out_shape=jax.ShapeDtypeStruct(out_shape, BF16),
        grid=(B, G),
        in_specs=[
            pl.BlockSpec((1, 1, 1, nh, hw), lambda b, g: (which, b, g, 0, 0)),
            pl.BlockSpec((2, hw), c2),
            pl.BlockSpec((2, hw, hidden), lambda b, g: (0, 0, 0)),
            pl.BlockSpec((1, hidden), c2),
            pl.BlockSpec(w2a.shape, c2),
            pl.BlockSpec(b2a.shape, c2),
            tab, tab, tab,
        ],
        out_specs=pl.BlockSpec(out_block, lambda b, g: (b, g, 0, 0)),
        compiler_params=_cparams(("parallel", "parallel"), 40),
        name="compress_k" if keys else "compress_v",
    )(halves, pos2, w1s, b1.reshape(1, hidden), w2a, b2a, *tabs)


def _qproj2_kernel(h_ref, g_ref, wqt_ref, wg_ref, bg_ref, cos_ref, sin_ref, qt_ref, gt_ref):
    xn = _rms(h_ref[0], g_ref[...]).astype(BF16)
    tr = _dot_nt(wqt_ref[...], xn)
    cos, sin = cos_ref[...], sin_ref[...]
    for hd in range(N_HEADS):
        rows = slice(hd * HEAD_DIM, (hd + 1) * HEAD_DIM)
        qt_ref[0, rows, :] = (_rope_rows(tr[rows], cos, sin) * Q_SCALE).astype(BF16)
    gt_ref[0] = jax.nn.sigmoid(_dot(xn, wg_ref[...]) + bg_ref[...])


def _q_proj2(h, g, w_qg, b_gate):
    B, S, D = h.shape
    HD = N_HEADS * HEAD_DIM
    ng = N_HEADS * N_BRANCH
    ts = min(SEQ_TILE, S)
    wqt = w_qg[:, :HD].T.astype(BF16)
    wg = jnp.pad(w_qg[:, HD:], ((0, 0), (0, LANES - ng))).astype(BF16)
    bg = jnp.pad(b_gate, (0, LANES - ng)).reshape(1, LANES)
    cos, sin = _rope_tables(jnp.arange(S))
    tab = pl.BlockSpec((ROPE_HALF, ts), lambda b, i: (0, i))
    return pl.pallas_call(
        _qproj2_kernel,
        out_shape=(jax.ShapeDtypeStruct((B, HD, S), BF16), jax.ShapeDtypeStruct((B, S, LANES), F32)),
        grid=(B, S // ts),
        in_specs=[
            pl.BlockSpec((1, ts, D), lambda b, i: (b, i, 0)),
            pl.BlockSpec((1, D), lambda b, i: (0, 0)),
            pl.BlockSpec((HD, D), lambda b, i: (0, 0)),
            pl.BlockSpec((D, LANES), lambda b, i: (0, 0)),
            pl.BlockSpec((1, LANES), lambda b, i: (0, 0)),
            tab, tab,
        ],
        out_specs=(pl.BlockSpec((1, HD, ts), lambda b, i: (b, 0, i)),
                   pl.BlockSpec((1, ts, LANES), lambda b, i: (b, i, 0))),
        compiler_params=_cparams(("parallel", "parallel"), 40),
        name="nsa_q_proj",
    )(h, g.reshape(1, D), wqt, wg, bg, cos.T, sin.T)


def _heads_on_lanes(qt):
    return jnp.concatenate([qt[r * HEAD_DIM:(r + 1) * HEAD_DIM] for r in range(HEADS_PER_GROUP)], axis=1)


def _lane_query_pos(i, tq):
    lanes = lax.broadcasted_iota(I32, (1, HEADS_PER_GROUP * tq), 1)
    return i * tq + (lanes & (tq - 1))


def _finish_heads(acc_t, gates, g, branch, tq, denom_row):
    lane = lax.broadcasted_iota(I32, gates.shape, 1)
    rows = acc_t.shape[0]
    cols = []
    for r in range(HEADS_PER_GROUP):
        blk = acc_t[:, r * tq:(r + 1) * tq]
        nat = jnp.concatenate([blk, jnp.zeros((-rows % LANES, tq), F32)], axis=0).T
        col = (g * HEADS_PER_GROUP + r) * N_BRANCH + branch
        scale = jnp.sum(jnp.where(lane == col, gates, 0.0), axis=-1, keepdims=True)
        if denom_row is not None:
            scale = scale / nat[:, denom_row:denom_row + 1]
        cols.append(nat[:, 0:HEAD_DIM] * scale)
    return jnp.concatenate(cols, axis=-1)


def _cmp2_kernel(qt_ref, gt_ref, k_ref, vt_ref, c2s_ref, o_ref, bias_ref, *, tq, nc):
    g = pl.program_id(1)
    i = pl.program_id(2)

    def body(nk, nb):
        s = _dot(k_ref[0, 0, 0:nk, :], _heads_on_lanes(qt_ref[0]))
        t = _lane_query_pos(i, tq)
        n = lax.broadcasted_iota(I32, (nk, 1), 0)
        s = jnp.where(n * CMP_STRIDE + (CMP_BLOCK - 1) <= t, s, -jnp.inf)
        m = jnp.max(s, axis=0, keepdims=True)
        m = jnp.where(m == -jnp.inf, 0.0, m)
        e = jnp.exp2(s - m)
        p = e * (1.0 / jnp.maximum(jnp.sum(e, axis=0, keepdims=True), 1e-30))
        ot = _dot(vt_ref[0, 0, :, 0:nk], p.astype(BF16))
        o_ref[0] = _finish_heads(ot, gt_ref[0], g, 0, tq, None).astype(BF16)
        ps = p[:, 0:tq]
        for r in range(1, HEADS_PER_GROUP):
            ps = ps + p[:, r * tq:(r + 1) * tq]
        hi = ps.astype(BF16)
        rem = ps - hi.astype(F32)
        mid = rem.astype(BF16)
        lo = (rem - mid.astype(F32)).astype(BF16)
        c2s = c2s_ref[0:nb, 0:nk]
        imp = _dot(c2s, hi) + _dot(c2s, mid) + _dot(c2s, lo)
        cur = (i * tq + lax.broadcasted_iota(I32, (1, tq), 1)) // SEL_BLOCK
        j = lax.broadcasted_iota(I32, (nb, tq), 0)
        valid = j <= cur
        bonus = jnp.where(j == 0, FORCE_BONUS,
                          jnp.where(j == cur, FORCE_BONUS, jnp.where(j == cur - 1, FORCE_BONUS, 0.0)))
        work = jnp.where(valid, imp + bonus, -jnp.inf)
        sel = jnp.zeros((nb, tq), F32)
        for _ in range(N_SELECT):
            mx = jnp.max(work, axis=0, keepdims=True)
            idx = jnp.min(jnp.where(work == mx, j, MAX_SEL_BLOCKS), axis=0, keepdims=True)
            pick = j == idx
            sel = jnp.where(pick, 1.0, sel)
            work = jnp.where(pick, -jnp.inf, work)
        bias_ref[0, 0, 0:nb, :] = jnp.where(valid, jnp.where(sel > 0.0, 0.0, MASK_BIAS), MASK_BIAS).astype(BF16)
        if nb < MAX_SEL_BLOCKS:
            bias_ref[0, 0, nb:, :] = jnp.full((MAX_SEL_BLOCKS - nb, tq), MASK_BIAS, BF16)

    per_class = LANES * CMP_STRIDE // tq
    n_class = -(-(nc * CMP_STRIDE // tq) // per_class)
    for cls in range(n_class):
        nk = min(nc, LANES * (cls + 1))
        nb = min(MAX_SEL_BLOCKS, nk * CMP_STRIDE // SEL_BLOCK)
        pl.when(i // per_class == cls)(functools.partial(body, nk, nb))


def _cmp_attn2(qt, gates, kc, vct):
    B, HD, S = qt.shape
    G, dh = N_KV_GROUPS, HEAD_DIM
    gw = HD // G
    nc = kc.shape[2]
    tq = min(Q_TILE, S)
    n0 = jnp.arange(nc)[None, :] * CMP_STRIDE
    j0 = jnp.arange(MAX_SEL_BLOCKS)[:, None] * SEL_BLOCK
    c2s = ((n0 < j0 + SEL_BLOCK) & (n0 + CMP_BLOCK > j0)).astype(BF16)
    return pl.pallas_call(
        functools.partial(_cmp2_kernel, tq=tq, nc=nc),
        out_shape=(jax.ShapeDtypeStruct((B, S, HD), BF16), jax.ShapeDtypeStruct((B, G, MAX_SEL_BLOCKS, S), BF16)),
        grid=(B, G, S // tq),
        in_specs=[
            pl.BlockSpec((1, gw, tq), lambda b, g, i: (b, g, i)),
            pl.BlockSpec((1, tq, LANES), lambda b, g, i: (b, i, 0)),
            pl.BlockSpec((1, 1, nc, dh), lambda b, g, i: (b, g, 0, 0)),
            pl.BlockSpec((1, 1, dh, nc), lambda b, g, i: (b, g, 0, 0)),
            pl.BlockSpec((MAX_SEL_BLOCKS, nc), lambda b, g, i: (0, 0)),
        ],
        out_specs=(pl.BlockSpec((1, tq, gw), lambda b, g, i: (b, i, g)),
                   pl.BlockSpec((1, 1, MAX_SEL_BLOCKS, tq), lambda b, g, i: (b, g, 0, i))),
        compiler_params=_cparams(("parallel", "parallel", "parallel"), 40),
        name="nsa_compressed",
    )(qt, gates, kc, vct, c2s)


def _sel2_kernel(qt_ref, bias_ref, gt_ref, k_ref, vt_ref, kw_ref, vwt_ref, o_ref, qa, m_s, acc, acc_w, s_a, s_b,
                 *, tq, tk):
    g = pl.program_id(1)
    i = pl.program_id(2)
    qt = qt_ref[0]
    bias = bias_ref[0, 0]
    for r in range(HEADS_PER_GROUP):
        cols = slice(r * tq, (r + 1) * tq)
        qa[0:HEAD_DIM, cols] = qt[r * HEAD_DIM:(r + 1) * HEAD_DIM]
        qa[HEAD_DIM:HEAD_DIM + MAX_SEL_BLOCKS, cols] = bias
        qa[HEAD_DIM + MAX_SEL_BLOCKS:, cols] = jnp.zeros((KEY_AUG - HEAD_DIM - MAX_SEL_BLOCKS, tq), BF16)
    m_s[...] = jnp.full(m_s.shape, NEG_BIG, F32)
    acc[...] = jnp.zeros(acc.shape, F32)
    width = HEADS_PER_GROUP * tq

    def qk(jt, buf):
        buf[...] = _dot(k_ref[0, 0, pl.ds(pl.multiple_of(jt * tk, tk), tk), :], qa[...])

    def online_softmax(key0, n_keys, s):
        m_old = m_s[...]
        m_new = jnp.maximum(m_old, jnp.max(s, axis=0, keepdims=True))
        p = jnp.exp2(s - m_new).astype(BF16)
        acc[...] = jnp.exp2(m_old - m_new) * acc[...] + _dot(vt_ref[0, 0, :, pl.ds(key0, n_keys)], p)
        m_s[...] = m_new

    def absorb(jt, buf):
        online_softmax(pl.multiple_of(jt * tk, tk), tk, buf[...])

    key_row = lax.broadcasted_iota(I32, (tq, 1), 0)
    query_col = lax.broadcasted_iota(I32, (1, width), 1) & (tq - 1)

    def own_tile(s):
        return jnp.where(key_row <= query_col, s, NEG_BIG)

    def absorb_last(jt, buf):
        k0 = pl.multiple_of(jt * tk, tk)

        @pl.when(i % 2 == 1)
        def _():
            online_softmax(k0, tq, buf[0:tq, :])
            online_softmax(pl.multiple_of(k0 + tq, tq), tq, own_tile(buf[tq:tk, :]))

        @pl.when(i % 2 == 0)
        def _():
            online_softmax(k0, tq, own_tile(buf[0:tq, :]))

    def pair(u, c):
        qk(2 * u + 1, s_b)
        absorb(2 * u, s_a)
        qk(2 * u + 2, s_a)
        absorb(2 * u + 1, s_b)
        return c

    def window(miss):
        n_keys = (nw + 1 - miss) * tq
        k0 = pl.multiple_of((i - nw + miss) * tq, tq)
        s_b[0:n_keys, :] = _dot(kw_ref[0, 0, pl.ds(k0, n_keys), :], qa[0:HEAD_DIM, :])
        qk(0, s_a)
        parts = []
        for j in range(miss, nw + 1):
            part = s_b[(j - miss) * tq:(j - miss + 1) * tq, :]
            if j == 0:
                part = jnp.where(key_row > query_col, part, NEG_BIG)
            parts.append(own_tile(part) if j == nw else part)
        s = jnp.concatenate(parts, axis=0)
        m = jnp.max(s, axis=0, keepdims=True)
        p = jnp.exp2(s - m).astype(BF16)
        acc_w[...] = _dot(vwt_ref[0, 0, :, pl.ds(k0, n_keys)], p)

    nw = WINDOW // tq
    for miss in range(nw + 1):
        pl.when(jnp.maximum(nw - i, 0) == miss)(functools.partial(window, miss))

    last = (i * tq + tq + tk - 1) // tk - 1
    lax.fori_loop(0, last // 2, pair, 0)

    @pl.when(last % 2 == 1)
    def _():
        qk(last, s_b)
        absorb(last - 1, s_a)
        absorb_last(last, s_b)

    @pl.when(last % 2 == 0)
    def _():
        absorb_last(last, s_a)

    gates = gt_ref[0]
    o_ref[0] = (_finish_heads(acc[...], gates, g, 1, tq, HEAD_DIM)
                + _finish_heads(acc_w[...], gates, g, 2, tq, HEAD_DIM)).astype(BF16)


def _sel_win_attn(qt, bias, gates, ksa, vst, kw, vwt):
    B, HD, S = qt.shape
    G, dh = N_KV_GROUPS, HEAD_DIM
    gw = HD // G
    tq = min(SEL_Q_TILE, S // 2)
    tk = 2 * tq
    assert S % tk == 0 and WINDOW % tq == 0 and WINDOW + tq <= tk
    width = HEADS_PER_GROUP * tq
    return pl.pallas_call(
        functools.partial(_sel2_kernel, tq=tq, tk=tk),
        out_shape=jax.ShapeDtypeStruct((B, S, HD), BF16),
        grid=(B, G, S // tq),
        in_specs=[
            pl.BlockSpec((1, gw, tq), lambda b, g, i: (b, g, i)),
            pl.BlockSpec((1, 1, MAX_SEL_BLOCKS, tq), lambda b, g, i: (b, g, 0, i)),
            pl.BlockSpec((1, tq, LANES), lambda b, g, i: (b, i, 0)),
            pl.BlockSpec((1, 1, S, KEY_AUG), lambda b, g, i: (b, g, 0, 0)),
            pl.BlockSpec((1, 1, V_ROWS, S), lambda b, g, i: (b, g, 0, 0)),
            pl.BlockSpec((1, 1, S, dh), lambda b, g, i: (b, g, 0, 0)),
            pl.BlockSpec((1, 1, V_ROWS, S), lambda b, g, i: (b, g, 0, 0)),
        ],
        out_specs=pl.BlockSpec((1, tq, gw), lambda b, g, i: (b, i, g)),
        scratch_shapes=[pltpu.VMEM((KEY_AUG, width), BF16), pltpu.VMEM((1, width), F32),
                        pltpu.VMEM((V_ROWS, width), F32), pltpu.VMEM((V_ROWS, width), F32),
                        pltpu.VMEM((tk, width), F32), pltpu.VMEM((tk, width), F32)],
        compiler_params=_cparams(("parallel", "parallel", "arbitrary"), 60),
        name="nsa_selected_window",
    )(qt, bias, gates, ksa, vst, kw, vwt)


def _win2_kernel(qt_ref, gt_ref, k_ref, vt_ref, o_ref, *, tq, wb):
    g = pl.program_id(1)
    i = pl.program_id(2)
    nw = WINDOW // tq
    width = HEADS_PER_GROUP * tq
    r = lax.broadcasted_iota(I32, (tq, 1), 0)
    c = lax.broadcasted_iota(I32, (1, width), 1) & (tq - 1)
    oldest = jnp.where(r > c, 0.0, -jnp.inf)
    own = jnp.where(r <= c, 0.0, -jnp.inf)

    def body(miss):
        n_keys = (nw + 1 - miss) * tq
        k0 = pl.multiple_of((i - nw + miss) * tq, tq)
        s = _dot(k_ref[0, 0, pl.ds(k0, n_keys), :], _heads_on_lanes(qt_ref[0]))
        parts = []
        for j in range(miss, nw + 1):
            part = s[(j - miss) * tq:(j - miss + 1) * tq]
            parts.append(part + oldest if j == 0 else part + own if j == nw else part)
        s = jnp.concatenate(parts, axis=0)
        m = jnp.max(s, axis=0, keepdims=True)
        p = jnp.exp2(s - m).astype(BF16)
        acc = _dot(vt_ref[0, 0, :, pl.ds(k0, n_keys)], p)
        o_ref[0] = _finish_heads(acc, gt_ref[0], g, 2, tq, HEAD_DIM).astype(BF16)

    for miss in range(nw + 1):
        pl.when(jnp.maximum(nw - i, 0) == miss)(functools.partial(body, miss))


def _win_attn2(qt, gates, kw, vwt):
    B, HD, S = qt.shape
    G, dh = N_KV_GROUPS, HEAD_DIM
    gw = HD // G
    tq = min(Q_TILE, S)
    wb = WINDOW + tq
    assert S >= wb and tq % LANES == 0
    return pl.pallas_call(
        functools.partial(_win2_kernel, tq=tq, wb=wb),
        out_shape=jax.ShapeDtypeStruct((B, S, HD), BF16),
        grid=(B, G, S // tq),
        in_specs=[
            pl.BlockSpec((1, gw, tq), lambda b, g, i: (b, g, i)),
            pl.BlockSpec((1, tq, LANES), lambda b, g, i: (b, i, 0)),
            pl.BlockSpec((1, 1, S, dh), lambda b, g, i: (b, g, 0, 0)),
            pl.BlockSpec((1, 1, V_ROWS, S), lambda b, g, i: (b, g, 0, 0)),
        ],
        out_specs=pl.BlockSpec((1, tq, gw), lambda b, g, i: (b, i, g)),
        compiler_params=_cparams(("parallel", "parallel", "parallel"), 40),
        name="nsa_window",
    )(qt, gates, kw, vwt)


def _oproj_kernel(h_ref, a_ref, b_ref, w_ref, o_ref):
    o = a_ref[...].astype(F32) + b_ref[...].astype(F32)
    o_ref[...] = h_ref[...] + _dot(o.astype(BF16), w_ref[...])


def _out_proj(h2, oc, osw, w_o):
    T, D = h2.shape
    HD = oc.shape[-1]
    tt = min(SEQ_TILE, T)
    blk = lambda w: pl.BlockSpec((tt, w), lambda i: (i, 0))
    return pl.pallas_call(
        _oproj_kernel,
        out_shape=jax.ShapeDtypeStruct((T, D), F32),
        grid=(T // tt,),
        in_specs=[blk(D), blk(HD), blk(HD), pl.BlockSpec((HD, D), lambda i: (0, 0))],
        out_specs=blk(D),
        compiler_params=_cparams(("parallel",), 40),
        name="nsa_out_proj",
    )(h2, oc, osw, w_o.astype(BF16))


def _nsa_layer(h, g, w_qg, b_gate, w_o, shared):
    B, S, D = h.shape
    kc, vct, ksa, vst, kw, vwt = shared
    qt, gates = _q_proj2(h, g, w_qg, b_gate)
    oc, bias = _cmp_attn2(qt, gates, kc, vct)
    osw = _sel_win_attn(qt, bias, gates, ksa, vst, kw, vwt)
    flat = lambda a: a.reshape(B * S, a.shape[-1])
    return _out_proj(flat(h), flat(oc), flat(osw), w_o).reshape(B, S, D)


def _shared_kv(h, kv_norm, w_kv, ck, cv):
    B, S, _ = h.shape
    G, dh = N_KV_GROUPS, HEAD_DIM
    assert S % SEL_BLOCK == 0 and S // SEL_BLOCK <= MAX_SEL_BLOCKS
    cvals, ksa, vst, kw, vwt = _shared_kv_proj2(h, kv_norm, w_kv)
    halves = cvals.reshape(B, S, 2, G, dh).transpose(2, 0, 3, 1, 4).reshape(2, B, G, S // CMP_STRIDE, CMP_STRIDE * dh)
    kc = _compress2(halves, 0, *ck, keys=True)
    vct = _compress2(halves, 1, *cv, keys=False)
    return kc, vct, ksa, vst, kw, vwt


def kernel(x, p, norm_mix, norm_ffn, norm_ple, pool_w, pool_b, pool_scale, kv_norm, w_kv, cmp_k_pos, cmp_k_w1, cmp_k_b1, cmp_k_w2, cmp_k_b2, cmp_v_pos, cmp_v_w1, cmp_v_b1, cmp_v_w2, cmp_v_b2, w_qg, b_gate, w_o, router_g_w, router_g_b, router_e_w, router_e_b, moe_w1, moe_w3, moe_w2, ple_proj, ple_gate_w, ple_gate_b, final_norm):
    B, S, D = x.shape
    depth = p.shape[0]
    n_a = pool_w.shape[0]
    T = B * S
    h = x
    shared = None
    for i in range(depth):
        if i == n_a:
            shared = _shared_kv(h, kv_norm, w_kv,
                                (cmp_k_pos, cmp_k_w1, cmp_k_b1, cmp_k_w2, cmp_k_b2),
                                (cmp_v_pos, cmp_v_w1, cmp_v_b1, cmp_v_w2, cmp_v_b2))
        if i < n_a:
            h = _pool_layer(h, norm_mix[i], pool_w[i], pool_b[i], pool_scale[i])
        else:
            j = i - n_a
            h = _nsa_layer(h, norm_mix[i], w_qg[j], b_gate[j], w_o[j], shared)
        h = _moe_ple_layer(h.reshape(T, D), p.reshape(depth, T, p.shape[-1]), i, norm_ffn[i], router_g_w[i],
                           router_g_b[i], router_e_w[i], router_e_b[i], moe_w1, moe_w3, moe_w2, norm_ple[i],
                           ple_gate_w[i], ple_gate_b[i], ple_proj[i], final_norm, i == depth - 1).reshape(B, S, D)
    return h
```

```python
import functools

import jax
import jax.numpy as jnp
from jax import lax
from jax.experimental import pallas as pl
from jax.experimental.pallas import tpu as pltpu

F32 = jnp.float32
BF16 = jnp.bfloat16
I32 = jnp.int32

POOL_WINDOWS = (2, 4, 8, 16)
N_HEADS = 16
HEAD_DIM = 64
N_KV_GROUPS = 4
HEADS_PER_GROUP = N_HEADS // N_KV_GROUPS
N_BRANCH = 3
ROPE_DIMS = HEAD_DIM // 4
ROPE_HALF = ROPE_DIMS // 2
ROPE_THETA = 500000.0
CMP_BLOCK = 32
CMP_STRIDE = 16
SEL_BLOCK = 64
N_SELECT = 16
WINDOW = 512
FORCE_BONUS = 1e4
N_FORCED = 3
N_EXPERT_GROUPS = 4
EXPERTS_PER_GROUP = 8
N_EXPERTS = N_EXPERT_GROUPS * EXPERTS_PER_GROUP
TOP_K_IN_GROUP = 2
RMS_EPS = 1e-6

LANES = 128
MAX_SEL_BLOCKS = LANES
MASK_BIAS = -30000.0

LOG2E = 1.4426950408889634
Q_SCALE = HEAD_DIM ** -0.5 * LOG2E
V_ROWS = HEAD_DIM + 16
KEY_AUG = 2 * LANES
NEG_BIG = -1e30

SEQ_TILE = 512
TOK_TILE = 512
CMB_TILE = 256
MOE_CHUNK = 512
Q_TILE = 256
SEL_Q_TILE = 512
HALO = 16


def _cparams(sem, vmem_mb):
    return pltpu.CompilerParams(dimension_semantics=sem, vmem_limit_bytes=vmem_mb * 1024 * 1024)


def _rms(x, g):
    return x * lax.rsqrt(jnp.mean(x * x, axis=-1, keepdims=True) + RMS_EPS) * g


def _dot(a, b):
    return jnp.dot(a, b, preferred_element_type=F32)


def _dot_nt(a, b):
    return lax.dot_general(a, b, (((1,), (1,)), ((), ())), preferred_element_type=F32)


def _pool_kernel(h_ref, halo_ref, g_ref, w_ref, b_ref, sc_ref, o_ref, *, ts, cg):
    i = pl.program_id(1)
    x = h_ref[0]
    g = g_ref[...]
    xn = _rms(x, g)
    hn = _rms(halo_ref[0], g)
    hn = jnp.where(i > 0, hn, 0.0)
    ext = jnp.concatenate([hn, xn], axis=0)
    t = i * ts + lax.broadcasted_iota(I32, (ts, 1), 0)
    outs = []
    for gi, w in enumerate(POOL_WINDOWS):
        s = ext[:, gi * cg:(gi + 1) * cg]
        k = 1
        while k < w:
            s = s + pltpu.roll(s, k, axis=0)
            k *= 2
        cnt = jnp.minimum(t + 1, w).astype(F32)
        pooled = s[HALO:] / cnt - xn[:, gi * cg:(gi + 1) * cg]
        outs.append(_dot(pooled.astype(BF16), w_ref[gi]))
    y = jnp.concatenate(outs, axis=-1)
    o_ref[0] = x + (y + b_ref[...]) * sc_ref[...]


def _pool_layer(h, g, w, b, sc):
    B, S, D = h.shape
    ts = min(SEQ_TILE, S)
    cg = D // len(POOL_WINDOWS)
    row = lambda v: v.reshape(1, D)
    return pl.pallas_call(
        functools.partial(_pool_kernel, ts=ts, cg=cg),
        out_shape=jax.ShapeDtypeStruct((B, S, D), F32),
        grid=(B, S // ts),
        in_specs=[
            pl.BlockSpec((1, ts, D), lambda b_, i: (b_, i, 0)),
            pl.BlockSpec((1, HALO, D), lambda b_, i: (b_, jnp.maximum(i * (ts // HALO) - 1, 0), 0)),
            pl.BlockSpec((1, D), lambda b_, i: (0, 0)),
            pl.BlockSpec((len(POOL_WINDOWS), cg, cg), lambda b_, i: (0, 0, 0)),
            pl.BlockSpec((1, D), lambda b_, i: (0, 0)),
            pl.BlockSpec((1, D), lambda b_, i: (0, 0)),
        ],
        out_specs=pl.BlockSpec((1, ts, D), lambda b_, i: (b_, i, 0)),
        compiler_params=_cparams(("parallel", "parallel"), 40),
        name="pool_mixer",
    )(h, h, row(g), w.astype(BF16), row(b), row(sc))


def _router_kernel(h_ref, g_ref, wh_ref, wl_ref, b_ref, info_ref, cnt_ref, *, tt):
    i = pl.program_id(0)

    @pl.when(i == 0)
    def _():
        cnt_ref[...] = jnp.zeros_like(cnt_ref)

    xn = _rms(h_ref[...], g_ref[...])
    xh = xn.astype(BF16)
    xl = (xn - xh.astype(F32)).astype(BF16)
    logits = _dot(xh, wh_ref[...]) + (_dot(xh, wl_ref[...]) + _dot(xl, wh_ref[...])) + b_ref[...]
    lane = lax.broadcasted_iota(I32, (tt, LANES), 1)
    neg = -jnp.inf
    gl = jnp.where(lane < N_EXPERT_GROUPS, logits, neg)
    gmax = jnp.max(gl, axis=-1, keepdims=True)
    grp = jnp.min(jnp.where(gl == gmax, lane, LANES), axis=-1, keepdims=True)
    gprob = 1.0 / jnp.sum(jnp.exp(gl - gmax), axis=-1, keepdims=True)
    lo = N_EXPERT_GROUPS + grp * EXPERTS_PER_GROUP
    el = jnp.where(lane >= lo, jnp.where(lane < lo + EXPERTS_PER_GROUP, logits, neg), neg)
    v1 = jnp.max(el, axis=-1, keepdims=True)
    i1 = jnp.min(jnp.where(el == v1, lane, LANES), axis=-1, keepdims=True)
    el2 = jnp.where(lane == i1, neg, el)
    v2 = jnp.max(el2, axis=-1, keepdims=True)
    i2 = jnp.min(jnp.where(el2 == v2, lane, LANES), axis=-1, keepdims=True)
    e2 = jnp.exp(v2 - v1)
    w1 = gprob / (1.0 + e2)
    w2 = gprob * e2 / (1.0 + e2)
    oh1 = lane == i1
    oh2 = lane == i2
    oh = jnp.where(oh1, 1.0, jnp.where(oh2, 1.0, 0.0))
    r_ = lax.broadcasted_iota(I32, (tt, tt), 0)
    c_ = lax.broadcasted_iota(I32, (tt, tt), 1)
    tri = jnp.where(r_ > c_, 1.0, 0.0).astype(BF16)
    tot = _dot(tri, oh.astype(BF16)) + cnt_ref[...]
    r1 = jnp.sum(jnp.where(oh1, tot, 0.0), axis=-1, keepdims=True)
    r2 = jnp.sum(jnp.where(oh2, tot, 0.0), axis=-1, keepdims=True)
    cnt_ref[...] = cnt_ref[...] + jnp.sum(oh, axis=0, keepdims=True)
    vals = (i1.astype(F32) - N_EXPERT_GROUPS, i2.astype(F32) - N_EXPERT_GROUPS, w1, w2, r1, r2)
    info = jnp.zeros((tt, LANES), F32)
    for k, v in enumerate(vals):
        info = jnp.where(lane == k, v, info)
    info_ref[...] = info


def _router(h2, g, wr, br):
    T, D = h2.shape
    tt = min(TOK_TILE, T)
    return pl.pallas_call(
        functools.partial(_router_kernel, tt=tt),
        out_shape=(jax.ShapeDtypeStruct((T, LANES), F32), jax.ShapeDtypeStruct((1, LANES), F32)),
        grid=(T // tt,),
        in_specs=[
            pl.BlockSpec((tt, D), lambda i: (i, 0)),
            pl.BlockSpec((1, D), lambda i: (0, 0)),
            pl.BlockSpec((D, LANES), lambda i: (0, 0)),
            pl.BlockSpec((D, LANES), lambda i: (0, 0)),
            pl.BlockSpec((1, LANES), lambda i: (0, 0)),
        ],
        out_specs=(pl.BlockSpec((tt, LANES), lambda i: (i, 0)), pl.BlockSpec((1, LANES), lambda i: (0, 0))),
        compiler_params=_cparams(("arbitrary",), 40),
        name="moe_router",
    )(h2, g.reshape(1, D), wr.astype(BF16), (wr - wr.astype(BF16).astype(F32)).astype(BF16), br)


def _row_copy(src, s, dst, d, sem):
    return pltpu.make_async_copy(src.at[pl.ds(s, 1)], dst.at[pl.ds(d, 1)], sem)


def _dispatch_kernel(dest_ref, cend_ref, h_ref, g_ref, xs_ref, xn_s, sem, *, tt, n_tok, n_chunks):
    base = pl.program_id(0) * tt

    @pl.when(pl.program_id(0) == 0)
    def _():
        xn_s[...] = jnp.zeros_like(xn_s)

        def tail(e):
            nonempty = cend_ref[e] > (cend_ref[e - 1] if e > 0 else 0)
            row = pl.multiple_of((cend_ref[e] - 1) * tt, tt)
            return nonempty, pltpu.make_async_copy(xn_s, xs_ref.at[pl.ds(row, tt)], sem)

        def unused(c):
            return pltpu.make_async_copy(xn_s, xs_ref.at[pl.ds(pl.multiple_of(c * tt, tt), tt)], sem)

        n_used = cend_ref[N_EXPERTS - 1]
        for e in range(N_EXPERTS):
            nonempty, cp = tail(e)
            pl.when(nonempty)(cp.start)
        lax.fori_loop(n_used, n_chunks, lambda c, z: (unused(c).start(), z)[1], 0)
        for e in range(N_EXPERTS):
            nonempty, cp = tail(e)
            pl.when(nonempty)(cp.wait)
        lax.fori_loop(n_used, n_chunks, lambda c, z: (unused(c).wait(), z)[1], 0)

    xn_s[...] = _rms(h_ref[...], g_ref[...])

    for r in range(tt):
        for k in range(TOP_K_IN_GROUP):
            _row_copy(xn_s, r, xs_ref, dest_ref[k * n_tok + base + r], sem).start(priority=k % 2)
    for k in range(TOP_K_IN_GROUP):
        pltpu.make_async_copy(xn_s, xs_ref.at[pl.ds(0, tt)], sem).wait()


def _dispatch(dest, chunk_end, h2, g, n_rows):
    T, D = h2.shape
    tt = MOE_CHUNK
    assert T % tt == 0
    return pl.pallas_call(
        functools.partial(_dispatch_kernel, tt=tt, n_tok=T, n_chunks=n_rows // tt),
        out_shape=jax.ShapeDtypeStruct((n_rows, D), F32),
        grid_spec=pltpu.PrefetchScalarGridSpec(
            num_scalar_prefetch=2,
            grid=(T // tt,),
            in_specs=[
                pl.BlockSpec((tt, D), lambda i, d, ce: (i, 0)),
                pl.BlockSpec((1, D), lambda i, d, ce: (0, 0)),
            ],
            out_specs=pl.BlockSpec(memory_space=pl.ANY),
            scratch_shapes=[pltpu.VMEM((tt, D), F32), pltpu.SemaphoreType.DMA],
        ),
        compiler_params=_cparams(("arbitrary",), 40),
        name="moe_dispatch",
    )(dest, chunk_end, h2, g.reshape(1, D))


def _expert_kernel(ce_ref, nv_ref, xs_ref, w1_ref, w3_ref, w2_ref, o_ref, w1b, w3b, w2b):
    c = pl.program_id(0)
    e = ce_ref[c]
    prev = ce_ref[jnp.maximum(c - 1, 0)]

    @pl.when(jnp.logical_or(c == 0, e != prev))
    def _():
        w1b[...] = w1_ref[0, 0].astype(BF16)
        w3b[...] = w3_ref[0, 0].astype(BF16)
        w2b[...] = w2_ref[0, 0].astype(BF16)

    @pl.when(c < nv_ref[0])
    def _():
        x = xs_ref[...].astype(BF16)
        a = _dot(x, w1b[...])
        b = _dot(x, w3b[...])
        hc = a * jax.nn.sigmoid(a) * b
        o_ref[...] = _dot(hc.astype(BF16), w2b[...])

    @pl.when(c >= nv_ref[0])
    def _():
        o_ref[...] = jnp.zeros_like(o_ref)


def _experts(chunk_e, n_valid, xs, w1, w3, w2, layer):
    P, D = xs.shape
    F = w1.shape[-1]
    ch = MOE_CHUNK
    rows = lambda c, ce, nv: (jnp.minimum(c, nv[0] - 1), 0)
    wsel = lambda c, ce, nv: (layer, ce[c], 0, 0)
    return pl.pallas_call(
        _expert_kernel,
        out_shape=jax.ShapeDtypeStruct((P, D), F32),
        grid_spec=pltpu.PrefetchScalarGridSpec(
            num_scalar_prefetch=2,
            grid=(P // ch,),
            in_specs=[
                pl.BlockSpec((ch, D), rows),
                pl.BlockSpec((1, 1, D, F), wsel),
                pl.BlockSpec((1, 1, D, F), wsel),
                pl.BlockSpec((1, 1, F, D), wsel),
            ],
            out_specs=pl.BlockSpec((ch, D), lambda c, ce, nv: (c, 0)),
            scratch_shapes=[pltpu.VMEM((D, F), BF16), pltpu.VMEM((D, F), BF16), pltpu.VMEM((F, D), BF16)],
        ),
        compiler_params=_cparams(("arbitrary",), 56),
        name="moe_experts",
    )(chunk_e, n_valid, xs, w1, w3, w2)


def _combine_kernel(dest_ref, h_ref, info_ref, rows_ref, p_ref, g_ref, gw_ref, gb_ref, pw_ref, fn_ref,
                    o_ref, buf_a, buf_b, sem, *, tt, n_tok, n_steps, final):
    i = pl.program_id(0)
    bufs = (buf_a, buf_b)

    def row(tile, sl, k, r):
        return _row_copy(rows_ref, dest_ref[k * n_tok + tile * tt + r], bufs[sl].at[k], r, sem.at[sl])

    def wait_rows(sl):
        for k in range(TOP_K_IN_GROUP):
            pltpu.make_async_copy(rows_ref.at[pl.ds(0, tt)], bufs[sl].at[k], sem.at[sl]).wait()

    @pl.when(i == 0)
    def _():
        def issue(r, c):
            for k in range(TOP_K_IN_GROUP):
                row(0, 0, k, r).start(priority=k % 2)
            return c

        lax.fori_loop(0, tt, issue, 0, unroll=8)

    def step(sl):
        wait_rows(sl)
        nxt = jnp.minimum(i + 1, n_steps - 1)
        for r in range(tt):
            for k in range(TOP_K_IN_GROUP):
                row(nxt, 1 - sl, k, r).start(priority=k % 2)
        info = info_ref[...]
        y = h_ref[...] + info[:, 2:3] * bufs[sl][0] + info[:, 3:4] * bufs[sl][1]
        hn = _rms(y, g_ref[...])
        gate = jax.nn.sigmoid(_dot(hn.astype(BF16), gw_ref[...]) + gb_ref[...])
        out = y + _dot(p_ref[0].astype(BF16), pw_ref[...]) * gate
        if final:
            out = _rms(out, fn_ref[...])
        o_ref[...] = out
        pl.when(i == n_steps - 1)(functools.partial(wait_rows, 1 - sl))

    for parity in range(2):
        pl.when(i % 2 == parity)(functools.partial(step, parity))


def _combine(dest, h2, info, rows, p3, layer, g, gw, gb, pw, fn, final):
    T, D = h2.shape
    PD = p3.shape[-1]
    tt = min(CMB_TILE, T)
    full = lambda i, d: (0, 0)
    return pl.pallas_call(
        functools.partial(_combine_kernel, tt=tt, n_tok=T, n_steps=T // tt, final=final),
        out_shape=jax.ShapeDtypeStruct((T, D), F32),
        grid_spec=pltpu.PrefetchScalarGridSpec(
            num_scalar_prefetch=1,
            grid=(T // tt,),
            in_specs=[
                pl.BlockSpec((tt, D), lambda i, d: (i, 0)),
                pl.BlockSpec((tt, LANES), lambda i, d: (i, 0)),
                pl.BlockSpec(memory_space=pl.ANY),
                pl.BlockSpec((1, tt, PD), lambda i, d: (layer, i, 0)),
                pl.BlockSpec((1, D), full),
                pl.BlockSpec((D, D), full),
                pl.BlockSpec((1, D), full),
                pl.BlockSpec((PD, D), full),
                pl.BlockSpec((1, D), full),
            ],
            out_specs=pl.BlockSpec((tt, D), lambda i, d: (i, 0)),
            scratch_shapes=[pltpu.VMEM((TOP_K_IN_GROUP, tt, D), F32), pltpu.VMEM((TOP_K_IN_GROUP, tt, D), F32),
                            pltpu.SemaphoreType.DMA((2,))],
        ),
        compiler_params=_cparams(("arbitrary",), 40),
        name="moe_combine_ple",
    )(dest, h2, info, rows, p3, g.reshape(1, D), gw.astype(BF16), gb.reshape(1, D), pw.astype(BF16),
      fn.reshape(1, D))


def _moe_ple_layer(h2, p3, layer, norm_ffn, rg_w, rg_b, re_w, re_b, w1, w3, w2, norm_ple, gate_w, gate_b, ple_proj,
                   final_norm, final):
    T, D = h2.shape
    A = T * TOP_K_IN_GROUP
    pad = LANES - N_EXPERT_GROUPS - N_EXPERTS
    wr = jnp.concatenate([rg_w, re_w, jnp.zeros((D, pad), F32)], axis=1)
    br = jnp.concatenate([rg_b, re_b, jnp.zeros((pad,), F32)]).reshape(1, LANES)
    info, cnt = _router(h2, norm_ffn, wr, br)
    counts = cnt[0, N_EXPERT_GROUPS:N_EXPERT_GROUPS + N_EXPERTS].astype(I32)
    n_chunks_e = (counts + MOE_CHUNK - 1) // MOE_CHUNK
    chunk_end = jnp.cumsum(n_chunks_e)
    pstarts = (chunk_end - n_chunks_e) * MOE_CHUNK
    n_chunks = -(-A // MOE_CHUNK) + N_EXPERTS
    n_valid = chunk_end[-1:].astype(I32)
    cidx = jnp.minimum(jnp.arange(n_chunks, dtype=I32), n_valid[0] - 1)
    chunk_e = jnp.sum((chunk_end[None, :] <= cidx[:, None]).astype(I32), axis=1)
    chunk_e = jnp.minimum(chunk_e, N_EXPERTS - 1)
    e_idx = info[:, 0:TOP_K_IN_GROUP].astype(I32)
    rank = info[:, 4:4 + TOP_K_IN_GROUP].astype(I32)
    start = jnp.sum(jnp.where(e_idx[..., None] == jnp.arange(N_EXPERTS, dtype=I32), pstarts, 0), axis=-1)
    dest = (start + rank).T.reshape(A)
    xs = _dispatch(dest, chunk_end.astype(I32), h2, norm_ffn, n_chunks * MOE_CHUNK)
    rows = _experts(chunk_e, n_valid, xs, w1, w3, w2, layer)
    return _combine(dest, h2, info, rows, p3, layer, norm_ple, gate_w, gate_b, ple_proj, final_norm, final)


def _rope_rows(xt, cos, sin):
    x1 = xt[0:ROPE_HALF]
    x2 = xt[ROPE_HALF:ROPE_DIMS]
    return jnp.concatenate([x1 * cos - x2 * sin, x2 * cos + x1 * sin, xt[ROPE_DIMS:]], axis=0)


def _rope_tables(pos):
    inv = jnp.float32(ROPE_THETA) ** (-jnp.arange(ROPE_HALF, dtype=F32) * 2.0 / ROPE_DIMS)
    ang = pos.astype(F32)[:, None] * inv[None, :]
    return jnp.cos(ang), jnp.sin(ang)


def _lane_rope_tables(pos):
    cos, sin = _rope_tables(pos)
    n = pos.shape[0]
    ones = jnp.ones((n, HEAD_DIM - ROPE_DIMS), F32)
    zeros = jnp.zeros((n, HEAD_DIM - ROPE_DIMS), F32)
    zh = jnp.zeros((n, ROPE_HALF), F32)
    two = lambda a: jnp.concatenate([a, a], axis=1)
    return (two(jnp.concatenate([cos, cos, ones], axis=1)), two(jnp.concatenate([zh, sin, zeros], axis=1)),
            two(jnp.concatenate([-sin, zh, zeros], axis=1)))


def _rope_lanes(x, c, s1, s2):
    return x * c + pltpu.roll(x, ROPE_HALF, axis=1) * s1 + pltpu.roll(x, LANES - ROPE_HALF, axis=1) * s2


def _kv_kernel(h_ref, g_ref, wn_ref, wt_ref, c_ref, s1_ref, s2_ref, cv_ref, ksa_ref, vst_ref, kw_ref, vwt_ref, *, ts):
    i = pl.program_id(1)
    kvw = N_KV_GROUPS * HEAD_DIM
    hn = _rms(h_ref[0], g_ref[...]).astype(BF16)
    nat = _dot(hn, wn_ref[...])
    cv_ref[0] = nat[:, 0:2 * kvw]
    tr = _dot_nt(wt_ref[...], hn)
    c, s1, s2 = c_ref[...], s1_ref[...], s2_ref[...]
    roped = [_rope_lanes(nat[:, 2 * kvw + k * LANES:2 * kvw + (k + 1) * LANES], c, s1, s2)
             for k in range(2 * kvw // LANES)]
    lane = lax.broadcasted_iota(I32, (ts, KEY_AUG), 1)
    pos = i * ts + lax.broadcasted_iota(I32, (ts, KEY_AUG), 0)
    onehot = jnp.where(lane - HEAD_DIM == pos // SEL_BLOCK, 1.0, 0.0).astype(BF16)
    ones_row = jnp.where(lax.broadcasted_iota(I32, (V_ROWS - HEAD_DIM, ts), 0) == 0, 1.0, 0.0).astype(BF16)
    per_tile = LANES // HEAD_DIM
    for g in range(N_KV_GROUPS):
        lo = (g % per_tile) * HEAD_DIM
        ksa_ref[0, g] = onehot
        ksa_ref[0, g, :, 0:HEAD_DIM] = roped[g // per_tile][:, lo:lo + HEAD_DIM].astype(BF16)
        kw_ref[0, g] = roped[N_KV_GROUPS // per_tile + g // per_tile][:, lo:lo + HEAD_DIM].astype(BF16)
        for ref, base in ((vst_ref, 0), (vwt_ref, kvw)):
            ref[0, g, 0:HEAD_DIM, :] = tr[base + g * HEAD_DIM:base + (g + 1) * HEAD_DIM].astype(BF16)
            ref[0, g, HEAD_DIM:, :] = ones_row


def _shared_kv_proj(h, kv_norm, w_kv):
    B, S, D = h.shape
    G, dh = N_KV_GROUPS, HEAD_DIM
    kvw = G * dh
    ts = min(SEQ_TILE, S)
    br = lambda k: w_kv[:, k * kvw:(k + 1) * kvw]
    w_nat = jnp.concatenate([br(0), br(1), br(2), br(4)], axis=1).astype(BF16)
    w_tr = jnp.concatenate([br(3), br(5)], axis=1).T.astype(BF16)
    tabs = _lane_rope_tables(jnp.arange(S))
    tab = pl.BlockSpec((ts, LANES), lambda b, i: (i, 0))
    return pl.pallas_call(
        functools.partial(_kv_kernel, ts=ts),
        out_shape=(
            jax.ShapeDtypeStruct((B, S, 2 * kvw), F32),
            jax.ShapeDtypeStruct((B, G, S, KEY_AUG), BF16),
            jax.ShapeDtypeStruct((B, G, V_ROWS, S), BF16),
            jax.ShapeDtypeStruct((B, G, S, dh), BF16),
            jax.ShapeDtypeStruct((B, G, V_ROWS, S), BF16),
        ),
        grid=(B, S // ts),
        in_specs=[
            pl.BlockSpec((1, ts, D), lambda b, i: (b, i, 0)),
            pl.BlockSpec((1, D), lambda b, i: (0, 0)),
            pl.BlockSpec((D, 4 * kvw), lambda b, i: (0, 0)),
            pl.BlockSpec((2 * kvw, D), lambda b, i: (0, 0)),
            tab, tab, tab,
        ],
        out_specs=(
            pl.BlockSpec((1, ts, 2 * kvw), lambda b, i: (b, i, 0)),
            pl.BlockSpec((1, G, ts, KEY_AUG), lambda b, i: (b, 0, i, 0)),
            pl.BlockSpec((1, G, V_ROWS, ts), lambda b, i: (b, 0, 0, i)),
            pl.BlockSpec((1, G, ts, dh), lambda b, i: (b, 0, i, 0)),
            pl.BlockSpec((1, G, V_ROWS, ts), lambda b, i: (b, 0, 0, i)),
        ),
        compiler_params=_cparams(("parallel", "parallel"), 48),
        name="shared_kv_proj",
    )(h, kv_norm.reshape(1, D), w_nat, w_tr, *tabs)


def _compress_kernel(x_ref, pos_ref, w1_ref, b1_ref, w2_ref, b2_ref, c_ref, s1_ref, s2_ref, o_ref, *, nh, keys):
    x = x_ref[0, 0, 0]
    a = _dot((x + pos_ref[0:1]).astype(BF16), w1_ref[0])
    b = _dot((x + pos_ref[1:2]).astype(BF16), w1_ref[1])
    hid = jax.nn.gelu(a + pltpu.roll(b, nh - 1, axis=0) + b1_ref[...]).astype(BF16)
    if keys:
        out = _rope_lanes(_dot(hid, w2_ref[...]) + b2_ref[...], c_ref[...], s1_ref[...], s2_ref[...])
        o_ref[0, 0] = out[:, 0:HEAD_DIM].astype(BF16)
    else:
        o_ref[0, 0] = (_dot_nt(w2_ref[...], hid) + b2_ref[...]).astype(BF16)


def _compress(halves, which, pos_emb, w1, b1, w2, b2, keys):
    _, B, G, nh, hw = halves.shape
    dh = HEAD_DIM
    hidden = w1.shape[-1]
    pos2 = pos_emb.reshape(2, hw)
    w1s = w1.reshape(2, hw, hidden).astype(BF16)
    tabs = _lane_rope_tables(jnp.arange(nh) * CMP_STRIDE + CMP_BLOCK - 1)
    if keys:
        w2a = jnp.pad(w2, ((0, 0), (0, LANES - dh))).astype(BF16)
        b2a = jnp.pad(b2, (0, LANES - dh)).reshape(1, LANES)
        out_shape, out_block = (B, G, nh, dh), (1, 1, nh, dh)
    else:
        w2a, b2a = w2.T.astype(BF16), b2.reshape(dh, 1)
        out_shape, out_block = (B, G, dh, nh), (1, 1, dh, nh)
    c2 = lambda b, g: (0, 0)
    tab = pl.BlockSpec((nh, LANES), c2)
    return pl.pallas_call(
        functools.partial(_compress_kernel, nh=nh, keys=keys),
        out_shape=jax.ShapeDtypeStruct(out_shape, BF16),
        grid=(B, G),
        in_specs=[
            pl.BlockSpec((1, 1, 1, nh, hw), lambda b, g: (which, b, g, 0, 0)),
            pl.BlockSpec((2, hw), c2),
            pl.BlockSpec((2, hw, hidden), lambda b, g: (0, 0, 0)),
            pl.BlockSpec((1, hidden), c2),
            pl.BlockSpec(w2a.shape, c2),
            pl.BlockSpec(b2a.shape, c2),
            tab, tab, tab,
        ],
        out_specs=pl.BlockSpec(out_block, lambda b, g: (b, g, 0, 0)),
        compiler_params=_cparams(("parallel", "parallel"), 40),
        name="compress_k" if keys else "compress_v",
    )(halves, pos2, w1s, b1.reshape(1, hidden), w2a, b2a, *tabs)


def _qproj_kernel(h_ref, g_ref, wqt_ref, wg_ref, bg_ref, cos_ref, sin_ref, qt_ref, gt_ref):
    xn = _rms(h_ref[0], g_ref[...]).astype(BF16)
    tr = _dot_nt(wqt_ref[...], xn)
    cos, sin = cos_ref[...], sin_ref[...]
    for hd in range(N_HEADS):
        rows = slice(hd * HEAD_DIM, (hd + 1) * HEAD_DIM)
        qt_ref[0, rows, :] = (_rope_rows(tr[rows], cos, sin) * Q_SCALE).astype(BF16)
    gt_ref[0] = jax.nn.sigmoid(_dot(xn, wg_ref[...]) + bg_ref[...])


def _q_proj(h, g, w_qg, b_gate):
    B, S, D = h.shape
    HD = N_HEADS * HEAD_DIM
    ng = N_HEADS * N_BRANCH
    ts = min(SEQ_TILE, S)
    wqt = w_qg[:, :HD].T.astype(BF16)
    wg = jnp.pad(w_qg[:, HD:], ((0, 0), (0, LANES - ng))).astype(BF16)
    bg = jnp.pad(b_gate, (0, LANES - ng)).reshape(1, LANES)
    cos, sin = _rope_tables(jnp.arange(S))
    tab = pl.BlockSpec((ROPE_HALF, ts), lambda b, i: (0, i))
    return pl.pallas_call(
        _qproj_kernel,
        out_shape=(jax.ShapeDtypeStruct((B, HD, S), BF16), jax.ShapeDtypeStruct((B, S, LANES), F32)),
        grid=(B, S // ts),
        in_specs=[
            pl.BlockSpec((1, ts, D), lambda b, i: (b, i, 0)),
            pl.BlockSpec((1, D), lambda b, i: (0, 0)),
            pl.BlockSpec((HD, D), lambda b, i: (0, 0)),
            pl.BlockSpec((D, LANES), lambda b, i: (0, 0)),
            pl.BlockSpec((1, LANES), lambda b, i: (0, 0)),
            tab, tab,
        ],
        out_specs=(pl.BlockSpec((1, HD, ts), lambda b, i: (b, 0, i)),
                   pl.BlockSpec((1, ts, LANES), lambda b, i: (b, i, 0))),
        compiler_params=_cparams(("parallel", "parallel"), 40),
        name="nsa_q_proj",
    )(h, g.reshape(1, D), wqt, wg, bg, cos.T, sin.T)


def _heads_on_lanes(qt):
    return jnp.concatenate([qt[r * HEAD_DIM:(r + 1) * HEAD_DIM] for r in range(HEADS_PER_GROUP)], axis=1)


def _lane_query_pos(i, tq):
    lanes = lax.broadcasted_iota(I32, (1, HEADS_PER_GROUP * tq), 1)
    return i * tq + (lanes & (tq - 1))


def _gated(acc_t, gates_t, g, branch, denom_row):
    row = lax.broadcasted_iota(I32, gates_t.shape, 0)
    scale = jnp.concatenate(
        [jnp.sum(jnp.where(row == (g * HEADS_PER_GROUP + r) * N_BRANCH + branch, gates_t, 0.0), axis=0, keepdims=True)
         for r in range(HEADS_PER_GROUP)], axis=1)
    if denom_row is not None:
        scale = scale / acc_t[denom_row:denom_row + 1]
    return acc_t[0:HEAD_DIM] * scale


def _unstack_heads(o_t, tq):
    per = LANES // HEAD_DIM
    cols = []
    for r in range(0, HEADS_PER_GROUP, per):
        cols.append(jnp.concatenate([o_t[:, (r + k) * tq:(r + k + 1) * tq] for k in range(per)], axis=0).T)
    return jnp.concatenate(cols, axis=-1)


def _cmp_kernel(qt_ref, gt_ref, k_ref, vt_ref, c2s_ref, o_ref, bias_ref, *, tq, nc):
    g = pl.program_id(1)
    i = pl.program_id(2)

    def body(nk, nb):
        s = _dot(k_ref[0, 0, 0:nk, :], _heads_on_lanes(qt_ref[0]))
        t = _lane_query_pos(i, tq)
        n = lax.broadcasted_iota(I32, (nk, 1), 0)
        s = jnp.where(n * CMP_STRIDE + (CMP_BLOCK - 1) <= t, s, -jnp.inf)
        m = jnp.max(s, axis=0, keepdims=True)
        m = jnp.where(m == -jnp.inf, 0.0, m)
        e = jnp.exp2(s - m)
        p = e * (1.0 / jnp.maximum(jnp.sum(e, axis=0, keepdims=True), 1e-30))
        ot = _dot(vt_ref[0, 0, :, 0:nk], p.astype(BF16))
        o_ref[0] = _unstack_heads(_gated(ot, gt_ref[0].T, g, 0, None), tq).astype(BF16)
        ps = p[:, 0:tq]
        for r in range(1, HEADS_PER_GROUP):
            ps = ps + p[:, r * tq:(r + 1) * tq]
        hi = ps.astype(BF16)
        rem = ps - hi.astype(F32)
        mid = rem.astype(BF16)
        lo = (rem - mid.astype(F32)).astype(BF16)
        c2s = c2s_ref[0:nb, 0:nk]
        imp = _dot(c2s, hi) + _dot(c2s, mid) + _dot(c2s, lo)
        cur = (i * tq + lax.broadcasted_iota(I32, (1, tq), 1)) // SEL_BLOCK
        j = lax.broadcasted_iota(I32, (nb, tq), 0)
        valid = j <= cur
        forced = jnp.where(j == 0, 1.0, jnp.where(j == cur, 1.0, jnp.where(j == cur - 1, 1.0, 0.0)))
        sel = jnp.where(valid, forced, 0.0)
        work = jnp.where(valid, jnp.where(forced > 0.0, -jnp.inf, imp), -jnp.inf)
        for _ in range(N_SELECT - N_FORCED):
            mx = jnp.max(work, axis=0, keepdims=True)
            idx = jnp.min(jnp.where(work == mx, j, MAX_SEL_BLOCKS), axis=0, keepdims=True)
            pick = j == idx
            sel = jnp.where(pick, 1.0, sel)
            work = jnp.where(pick, -jnp.inf, work)
        bias_ref[0, 0, 0:nb, :] = jnp.where(valid, jnp.where(sel > 0.0, 0.0, MASK_BIAS), MASK_BIAS).astype(BF16)
        if nb < MAX_SEL_BLOCKS:
            bias_ref[0, 0, nb:, :] = jnp.full((MAX_SEL_BLOCKS - nb, tq), MASK_BIAS, BF16)

    per_class = LANES * CMP_STRIDE // tq
    n_class = -(-(nc * CMP_STRIDE // tq) // per_class)
    for cls in range(n_class):
        nk = min(nc, LANES * (cls + 1))
        nb = min(MAX_SEL_BLOCKS, nk * CMP_STRIDE // SEL_BLOCK)
        pl.when(i // per_class == cls)(functools.partial(body, nk, nb))


def _cmp_attn(qt, gates, kc, vct):
    B, HD, S = qt.shape
    G, dh = N_KV_GROUPS, HEAD_DIM
    gw = HD // G
    nc = kc.shape[2]
    tq = min(Q_TILE, S)
    n0 = jnp.arange(nc)[None, :] * CMP_STRIDE
    j0 = jnp.arange(MAX_SEL_BLOCKS)[:, None] * SEL_BLOCK
    c2s = ((n0 < j0 + SEL_BLOCK) & (n0 + CMP_BLOCK > j0)).astype(BF16)
    return pl.pallas_call(
        functools.partial(_cmp_kernel, tq=tq, nc=nc),
        out_shape=(jax.ShapeDtypeStruct((B, S, HD), BF16), jax.ShapeDtypeStruct((B, G, MAX_SEL_BLOCKS, S), BF16)),
        grid=(B, G, S // tq),
        in_specs=[
            pl.BlockSpec((1, gw, tq), lambda b, g, i: (b, g, i)),
            pl.BlockSpec((1, tq, LANES), lambda b, g, i: (b, i, 0)),
            pl.BlockSpec((1, 1, nc, dh), lambda b, g, i: (b, g, 0, 0)),
            pl.BlockSpec((1, 1, dh, nc), lambda b, g, i: (b, g, 0, 0)),
            pl.BlockSpec((MAX_SEL_BLOCKS, nc), lambda b, g, i: (0, 0)),
        ],
        out_specs=(pl.BlockSpec((1, tq, gw), lambda b, g, i: (b, i, g)),
                   pl.BlockSpec((1, 1, MAX_SEL_BLOCKS, tq), lambda b, g, i: (b, g, 0, i))),
        compiler_params=_cparams(("parallel", "parallel", "parallel"), 40),
        name="nsa_compressed",
    )(qt, gates, kc, vct, c2s)


def _sel_win_kernel(qt_ref, bias_ref, gt_ref, k_ref, vt_ref, kw_ref, vwt_ref, o_ref, qa, m_s, acc, acc_w, s_a, s_b,
                    *, tq, tk):
    g = pl.program_id(1)
    i = pl.program_id(2)
    qt = qt_ref[0]
    bias = bias_ref[0, 0]
    for r in range(HEADS_PER_GROUP):
        cols = slice(r * tq, (r + 1) * tq)
        qa[0:HEAD_DIM, cols] = qt[r * HEAD_DIM:(r + 1) * HEAD_DIM]
        qa[HEAD_DIM:HEAD_DIM + MAX_SEL_BLOCKS, cols] = bias
        qa[HEAD_DIM + MAX_SEL_BLOCKS:, cols] = jnp.zeros((KEY_AUG - HEAD_DIM - MAX_SEL_BLOCKS, tq), BF16)
    m_s[...] = jnp.full(m_s.shape, NEG_BIG, F32)
    acc[...] = jnp.zeros(acc.shape, F32)
    width = HEADS_PER_GROUP * tq

    def qk(jt, buf):
        buf[...] = _dot(k_ref[0, 0, pl.ds(pl.multiple_of(jt * tk, tk), tk), :], qa[...])

    def online_softmax(key0, n_keys, s):
        m_old = m_s[...]
        m_new = jnp.maximum(m_old, jnp.max(s, axis=0, keepdims=True))
        p = jnp.exp2(s - m_new).astype(BF16)
        acc[...] = jnp.exp2(m_old - m_new) * acc[...] + _dot(vt_ref[0, 0, :, pl.ds(key0, n_keys)], p)
        m_s[...] = m_new

    def absorb(jt, buf):
        online_softmax(pl.multiple_of(jt * tk, tk), tk, buf[...])

    key_row = lax.broadcasted_iota(I32, (tq, 1), 0)
    query_col = lax.broadcasted_iota(I32, (1, width), 1) & (tq - 1)

    def own_tile(s):
        return jnp.where(key_row <= query_col, s, NEG_BIG)

    def absorb_last(jt, buf):
        k0 = pl.multiple_of(jt * tk, tk)

        @pl.when(i % 2 == 1)
        def _():
            online_softmax(k0, tq, buf[0:tq, :])
            online_softmax(pl.multiple_of(k0 + tq, tq), tq, own_tile(buf[tq:tk, :]))

        @pl.when(i % 2 == 0)
        def _():
            online_softmax(k0, tq, own_tile(buf[0:tq, :]))

    def pair(u, c):
        qk(2 * u + 1, s_b)
        absorb(2 * u, s_a)
        qk(2 * u + 2, s_a)
        absorb(2 * u + 1, s_b)
        return c

    def window(miss):
        n_keys = (nw + 1 - miss) * tq
        k0 = pl.multiple_of((i - nw + miss) * tq, tq)
        s_b[0:n_keys, :] = _dot(kw_ref[0, 0, pl.ds(k0, n_keys), :], qa[0:HEAD_DIM, :])
        qk(0, s_a)
        parts = []
        for j in range(miss, nw + 1):
            part = s_b[(j - miss) * tq:(j - miss + 1) * tq, :]
            if j == 0:
                part = jnp.where(key_row > query_col, part, NEG_BIG)
            parts.append(own_tile(part) if j == nw else part)
        s = jnp.concatenate(parts, axis=0)
        m = jnp.max(s, axis=0, keepdims=True)
        p = jnp.exp2(s - m).astype(BF16)
        acc_w[...] = _dot(vwt_ref[0, 0, :, pl.ds(k0, n_keys)], p)

    nw = WINDOW // tq
    for miss in range(nw + 1):
        pl.when(jnp.maximum(nw - i, 0) == miss)(functools.partial(window, miss))

    last = (i * tq + tq + tk - 1) // tk - 1
    lax.fori_loop(0, last // 2, pair, 0)

    @pl.when(last % 2 == 1)
    def _():
        qk(last, s_b)
        absorb(last - 1, s_a)
        absorb_last(last, s_b)

    @pl.when(last % 2 == 0)
    def _():
        absorb_last(last, s_a)

    gates_t = gt_ref[0].T
    o_t = _gated(acc[...], gates_t, g, 1, HEAD_DIM) + _gated(acc_w[...], gates_t, g, 2, HEAD_DIM)
    o_ref[0] = _unstack_heads(o_t, tq).astype(BF16)


def _sel_win_attn(qt, bias, gates, ksa, vst, kw, vwt):
    B, HD, S = qt.shape
    G, dh = N_KV_GROUPS, HEAD_DIM
    gw = HD // G
    tq = min(SEL_Q_TILE, S // 2)
    tk = 2 * tq
    assert S % tk == 0 and WINDOW % tq == 0 and WINDOW + tq <= tk
    width = HEADS_PER_GROUP * tq
    return pl.pallas_call(
        functools.partial(_sel_win_kernel, tq=tq, tk=tk),
        out_shape=jax.ShapeDtypeStruct((B, S, HD), BF16),
        grid=(B, G, S // tq),
        in_specs=[
            pl.BlockSpec((1, gw, tq), lambda b, g, i: (b, g, i)),
            pl.BlockSpec((1, 1, MAX_SEL_BLOCKS, tq), lambda b, g, i: (b, g, 0, i)),
            pl.BlockSpec((1, tq, LANES), lambda b, g, i: (b, i, 0)),
            pl.BlockSpec((1, 1, S, KEY_AUG), lambda b, g, i: (b, g, 0, 0)),
            pl.BlockSpec((1, 1, V_ROWS, S), lambda b, g, i: (b, g, 0, 0)),
            pl.BlockSpec((1, 1, S, dh), lambda b, g, i: (b, g, 0, 0)),
            pl.BlockSpec((1, 1, V_ROWS, S), lambda b, g, i: (b, g, 0, 0)),
        ],
        out_specs=pl.BlockSpec((1, tq, gw), lambda b, g, i: (b, i, g)),
        scratch_shapes=[pltpu.VMEM((KEY_AUG, width), BF16), pltpu.VMEM((1, width), F32),
                        pltpu.VMEM((V_ROWS, width), F32), pltpu.VMEM((V_ROWS, width), F32),
                        pltpu.VMEM((tk, width), F32), pltpu.VMEM((tk, width), F32)],
        compiler_params=_cparams(("parallel", "parallel", "arbitrary"), 60),
        name="nsa_selected_window",
    )(qt, bias, gates, ksa, vst, kw, vwt)


def _oproj_kernel(h_ref, a_ref, b_ref, w_ref, o_ref):
    o = a_ref[...].astype(F32) + b_ref[...].astype(F32)
    o_ref[...] = h_ref[...] + _dot(o.astype(BF16), w_ref[...])


def _out_proj(h2, oc, osw, w_o):
    T, D = h2.shape
    HD = oc.shape[-1]
    tt = min(SEQ_TILE, T)
    blk = lambda w: pl.BlockSpec((tt, w), lambda i: (i, 0))
    return pl.pallas_call(
        _oproj_kernel,
        out_shape=jax.ShapeDtypeStruct((T, D), F32),
        grid=(T // tt,),
        in_specs=[blk(D), blk(HD), blk(HD), pl.BlockSpec((HD, D), lambda i: (0, 0))],
        out_specs=blk(D),
        compiler_params=_cparams(("parallel",), 40),
        name="nsa_out_proj",
    )(h2, oc, osw, w_o.astype(BF16))


def _nsa_layer(h, g, w_qg, b_gate, w_o, shared):
    B, S, D = h.shape
    kc, vct, ksa, vst, kw, vwt = shared
    qt, gates = _q_proj(h, g, w_qg, b_gate)
    oc, bias = _cmp_attn(qt, gates, kc, vct)
    osw = _sel_win_attn(qt, bias, gates, ksa, vst, kw, vwt)
    flat = lambda a: a.reshape(B * S, a.shape[-1])
    return _out_proj(flat(h), flat(oc), flat(osw), w_o).reshape(B, S, D)


def _shared_kv(h, kv_norm, w_kv, ck, cv):
    B, S, _ = h.shape
    G, dh = N_KV_GROUPS, HEAD_DIM
    assert S % SEL_BLOCK == 0 and N_SELECT <= S // SEL_BLOCK <= MAX_SEL_BLOCKS
    cvals, ksa, vst, kw, vwt = _shared_kv_proj(h, kv_norm, w_kv)
    halves = cvals.reshape(B, S, 2, G, dh).transpose(2, 0, 3, 1, 4).reshape(2, B, G, S // CMP_STRIDE, CMP_STRIDE * dh)
    kc = _compress(halves, 0, *ck, keys=True)
    vct = _compress(halves, 1, *cv, keys=False)
    return kc, vct, ksa, vst, kw, vwt


def kernel(x, p, norm_mix, norm_ffn, norm_ple, pool_w, pool_b, pool_scale, kv_norm, w_kv, cmp_k_pos, cmp_k_w1, cmp_k_b1, cmp_k_w2, cmp_k_b2, cmp_v_pos, cmp_v_w1, cmp_v_b1, cmp_v_w2, cmp_v_b2, w_qg, b_gate, w_o, router_g_w, router_g_b, router_e_w, router_e_b, moe_w1, moe_w3, moe_w2, ple_proj, ple_gate_w, ple_gate_b, final_norm):
    B, S, D = x.shape
    depth = p.shape[0]
    n_a = pool_w.shape[0]
    T = B * S
    h = x
    shared = None
    for i in range(depth):
        if i == n_a:
            shared = _shared_kv(h, kv_norm, w_kv,
                                (cmp_k_pos, cmp_k_w1, cmp_k_b1, cmp_k_w2, cmp_k_b2),
                                (cmp_v_pos, cmp_v_w1, cmp_v_b1, cmp_v_w2, cmp_v_b2))
        if i < n_a:
            h = _pool_layer(h, norm_mix[i], pool_w[i], pool_b[i], pool_scale[i])
        else:
            j = i - n_a
            h = _nsa_layer(h, norm_mix[i], w_qg[j], b_gate[j], w_o[j], shared)
        h = _moe_ple_layer(h.reshape(T, D), p.reshape(depth, T, p.shape[-1]), i, norm_ffn[i], router_g_w[i],
                           router_g_b[i], router_e_w[i], router_e_b[i], moe_w1, moe_w3, moe_w2, norm_ple[i],
                           ple_gate_w[i], ple_gate_b[i], ple_proj[i], final_norm, i == depth - 1).reshape(B, S, D)
    return h
```

```python
import functools

import jax
import jax.numpy as jnp
from jax import lax
from jax.experimental import pallas as pl
from jax.experimental.pallas import tpu as pltpu

F32 = jnp.float32
BF16 = jnp.bfloat16
I32 = jnp.int32

POOL_WINDOWS = (2, 4, 8, 16)
N_HEADS = 16
HEAD_DIM = 64
N_KV_GROUPS = 4
HEADS_PER_GROUP = N_HEADS // N_KV_GROUPS
N_BRANCH = 3
ROPE_DIMS = HEAD_DIM // 4
ROPE_HALF = ROPE_DIMS // 2
ROPE_THETA = 500000.0
CMP_BLOCK = 32
CMP_STRIDE = 16
SEL_BLOCK = 64
N_SELECT = 16
WINDOW = 512
FORCE_BONUS = 1e4
N_FORCED = 3
N_EXPERT_GROUPS = 4
EXPERTS_PER_GROUP = 8
N_EXPERTS = N_EXPERT_GROUPS * EXPERTS_PER_GROUP
TOP_K_IN_GROUP = 2
RMS_EPS = 1e-6

LANES = 128
MAX_SEL_BLOCKS = LANES
MASK_BIAS = -30000.0

LOG2E = 1.4426950408889634
Q_SCALE = HEAD_DIM ** -0.5 * LOG2E
V_ROWS = HEAD_DIM + 16
KEY_AUG = 2 * LANES
NEG_BIG = -1e30

SEQ_TILE = 512
TOK_TILE = 512
CMB_TILE = 256
MOE_CHUNK = 512
Q_TILE = 512
SEL_Q_TILE = 512
HALO = 16


def _cparams(sem, vmem_mb):
    return pltpu.CompilerParams(dimension_semantics=sem, vmem_limit_bytes=vmem_mb * 1024 * 1024)


def _rms(x, g):
    return x * lax.rsqrt(jnp.mean(x * x, axis=-1, keepdims=True) + RMS_EPS) * g


def _dot(a, b):
    return jnp.dot(a, b, preferred_element_type=F32)


def _dot_nt(a, b):
    return lax.dot_general(a, b, (((1,), (1,)), ((), ())), preferred_element_type=F32)


def _pool_kernel(h_ref, halo_ref, g_ref, w_ref, b_ref, sc_ref, o_ref, *, ts, cg):
    i = pl.program_id(1)
    x = h_ref[0]
    g = g_ref[...]
    xn = _rms(x, g)
    hn = _rms(halo_ref[0], g)
    hn = jnp.where(i > 0, hn, 0.0)
    ext = jnp.concatenate([hn, xn], axis=0)
    t = i * ts + lax.broadcasted_iota(I32, (ts, 1), 0)
    outs = []
    for gi, w in enumerate(POOL_WINDOWS):
        s = ext[:, gi * cg:(gi + 1) * cg]
        k = 1
        while k < w:
            s = s + pltpu.roll(s, k, axis=0)
            k *= 2
        cnt = jnp.minimum(t + 1, w).astype(F32)
        pooled = s[HALO:] / cnt - xn[:, gi * cg:(gi + 1) * cg]
        outs.append(_dot(pooled.astype(BF16), w_ref[gi]))
    y = jnp.concatenate(outs, axis=-1)
    o_ref[0] = x + (y + b_ref[...]) * sc_ref[...]


def _pool_layer(h, g, w, b, sc):
    B, S, D = h.shape
    ts = min(SEQ_TILE, S)
    cg = D // len(POOL_WINDOWS)
    row = lambda v: v.reshape(1, D)
    return pl.pallas_call(
        functools.partial(_pool_kernel, ts=ts, cg=cg),
        out_shape=jax.ShapeDtypeStruct((B, S, D), F32),
        grid=(B, S // ts),
        in_specs=[
            pl.BlockSpec((1, ts, D), lambda b_, i: (b_, i, 0)),
            pl.BlockSpec((1, HALO, D), lambda b_, i: (b_, jnp.maximum(i * (ts // HALO) - 1, 0), 0)),
            pl.BlockSpec((1, D), lambda b_, i: (0, 0)),
            pl.BlockSpec((len(POOL_WINDOWS), cg, cg), lambda b_, i: (0, 0, 0)),
            pl.BlockSpec((1, D), lambda b_, i: (0, 0)),
            pl.BlockSpec((1, D), lambda b_, i: (0, 0)),
        ],
        out_specs=pl.BlockSpec((1, ts, D), lambda b_, i: (b_, i, 0)),
        compiler_params=_cparams(("parallel", "parallel"), 40),
        name="pool_mixer",
    )(h, h, row(g), w.astype(BF16), row(b), row(sc))


def _router_kernel(h_ref, g_ref, wh_ref, wl_ref, b_ref, info_ref, cnt_ref, *, tt):
    i = pl.program_id(0)

    @pl.when(i == 0)
    def _():
        cnt_ref[...] = jnp.zeros_like(cnt_ref)

    xn = _rms(h_ref[...], g_ref[...])
    xh = xn.astype(BF16)
    xl = (xn - xh.astype(F32)).astype(BF16)
    logits = _dot(xh, wh_ref[...]) + (_dot(xh, wl_ref[...]) + _dot(xl, wh_ref[...])) + b_ref[...]
    lane = lax.broadcasted_iota(I32, (tt, LANES), 1)
    neg = -jnp.inf
    gl = jnp.where(lane < N_EXPERT_GROUPS, logits, neg)
    gmax = jnp.max(gl, axis=-1, keepdims=True)
    grp = jnp.min(jnp.where(gl == gmax, lane, LANES), axis=-1, keepdims=True)
    gprob = 1.0 / jnp.sum(jnp.exp(gl - gmax), axis=-1, keepdims=True)
    lo = N_EXPERT_GROUPS + grp * EXPERTS_PER_GROUP
    el = jnp.where(lane >= lo, jnp.where(lane < lo + EXPERTS_PER_GROUP, logits, neg), neg)
    v1 = jnp.max(el, axis=-1, keepdims=True)
    i1 = jnp.min(jnp.where(el == v1, lane, LANES), axis=-1, keepdims=True)
    el2 = jnp.where(lane == i1, neg, el)
    v2 = jnp.max(el2, axis=-1, keepdims=True)
    i2 = jnp.min(jnp.where(el2 == v2, lane, LANES), axis=-1, keepdims=True)
    e2 = jnp.exp(v2 - v1)
    w1 = gprob / (1.0 + e2)
    w2 = gprob * e2 / (1.0 + e2)
    oh1 = lane == i1
    oh2 = lane == i2
    oh = jnp.where(oh1, 1.0, jnp.where(oh2, 1.0, 0.0))
    r_ = lax.broadcasted_iota(I32, (tt, tt), 0)
    c_ = lax.broadcasted_iota(I32, (tt, tt), 1)
    tri = jnp.where(r_ > c_, 1.0, 0.0).astype(BF16)
    tot = _dot(tri, oh.astype(BF16)) + cnt_ref[...]
    r1 = jnp.sum(jnp.where(oh1, tot, 0.0), axis=-1, keepdims=True)
    r2 = jnp.sum(jnp.where(oh2, tot, 0.0), axis=-1, keepdims=True)
    cnt_ref[...] = cnt_ref[...] + jnp.sum(oh, axis=0, keepdims=True)
    vals = (i1.astype(F32) - N_EXPERT_GROUPS, i2.astype(F32) - N_EXPERT_GROUPS, w1, w2, r1, r2)
    info = jnp.zeros((tt, LANES), F32)
    for k, v in enumerate(vals):
        info = jnp.where(lane == k, v, info)
    info_ref[...] = info


def _router(h2, g, wr, br):
    T, D = h2.shape
    tt = min(TOK_TILE, T)
    return pl.pallas_call(
        functools.partial(_router_kernel, tt=tt),
        out_shape=(jax.ShapeDtypeStruct((T, LANES), F32), jax.ShapeDtypeStruct((1, LANES), F32)),
        grid=(T // tt,),
        in_specs=[
            pl.BlockSpec((tt, D), lambda i: (i, 0)),
            pl.BlockSpec((1, D), lambda i: (0, 0)),
            pl.BlockSpec((D, LANES), lambda i: (0, 0)),
            pl.BlockSpec((D, LANES), lambda i: (0, 0)),
            pl.BlockSpec((1, LANES), lambda i: (0, 0)),
        ],
        out_specs=(pl.BlockSpec((tt, LANES), lambda i: (i, 0)), pl.BlockSpec((1, LANES), lambda i: (0, 0))),
        compiler_params=_cparams(("arbitrary",), 40),
        name="moe_router",
    )(h2, g.reshape(1, D), wr.astype(BF16), (wr - wr.astype(BF16).astype(F32)).astype(BF16), br)


def _row_copy(src, s, dst, d, sem):
    return pltpu.make_async_copy(src.at[pl.ds(s, 1)], dst.at[pl.ds(d, 1)], sem)


def _dispatch_kernel(dest_ref, cend_ref, h_ref, g_ref, xs_ref, xn_s, sem, *, tt, n_tok, n_chunks):
    base = pl.program_id(0) * tt

    @pl.when(pl.program_id(0) == 0)
    def _():
        xn_s[...] = jnp.zeros_like(xn_s)

        def tail(e):
            nonempty = cend_ref[e] > (cend_ref[e - 1] if e > 0 else 0)
            row = pl.multiple_of((cend_ref[e] - 1) * tt, tt)
            return nonempty, pltpu.make_async_copy(xn_s, xs_ref.at[pl.ds(row, tt)], sem)

        def unused(c):
            return pltpu.make_async_copy(xn_s, xs_ref.at[pl.ds(pl.multiple_of(c * tt, tt), tt)], sem)

        n_used = cend_ref[N_EXPERTS - 1]
        for e in range(N_EXPERTS):
            nonempty, cp = tail(e)
            pl.when(nonempty)(cp.start)
        lax.fori_loop(n_used, n_chunks, lambda c, z: (unused(c).start(), z)[1], 0)
        for e in range(N_EXPERTS):
            nonempty, cp = tail(e)
            pl.when(nonempty)(cp.wait)
        lax.fori_loop(n_used, n_chunks, lambda c, z: (unused(c).wait(), z)[1], 0)

    xn_s[...] = _rms(h_ref[...], g_ref[...])

    for r in range(tt):
        for k in range(TOP_K_IN_GROUP):
            _row_copy(xn_s, r, xs_ref, dest_ref[k * n_tok + base + r], sem).start(priority=k % 2)
    for k in range(TOP_K_IN_GROUP):
        pltpu.make_async_copy(xn_s, xs_ref.at[pl.ds(0, tt)], sem).wait()


def _dispatch(dest, chunk_end, h2, g, n_rows):
    T, D = h2.shape
    tt = MOE_CHUNK
    assert T % tt == 0
    return pl.pallas_call(
        functools.partial(_dispatch_kernel, tt=tt, n_tok=T, n_chunks=n_rows // tt),
        out_shape=jax.ShapeDtypeStruct((n_rows, D), F32),
        grid_spec=pltpu.PrefetchScalarGridSpec(
            num_scalar_prefetch=2,
            grid=(T // tt,),
            in_specs=[
                pl.BlockSpec((tt, D), lambda i, d, ce: (i, 0)),
                pl.BlockSpec((1, D), lambda i, d, ce: (0, 0)),
            ],
            out_specs=pl.BlockSpec(memory_space=pl.ANY),
            scratch_shapes=[pltpu.VMEM((tt, D), F32), pltpu.SemaphoreType.DMA],
        ),
        compiler_params=_cparams(("arbitrary",), 40),
        name="moe_dispatch",
    )(dest, chunk_end, h2, g.reshape(1, D))


def _expert_kernel(ce_ref, nv_ref, xs_ref, w1_ref, w3_ref, w2_ref, o_ref, w1b, w3b, w2b):
    c = pl.program_id(0)
    e = ce_ref[c]
    prev = ce_ref[jnp.maximum(c - 1, 0)]

    @pl.when(jnp.logical_or(c == 0, e != prev))
    def _():
        w1b[...] = w1_ref[0, 0].astype(BF16)
        w3b[...] = w3_ref[0, 0].astype(BF16)
        w2b[...] = w2_ref[0, 0].astype(BF16)

    @pl.when(c < nv_ref[0])
    def _():
        x = xs_ref[...].astype(BF16)
        a = _dot(x, w1b[...])
        b = _dot(x, w3b[...])
        hc = a * jax.nn.sigmoid(a) * b
        o_ref[...] = _dot(hc.astype(BF16), w2b[...])

    @pl.when(c >= nv_ref[0])
    def _():
        o_ref[...] = jnp.zeros_like(o_ref)


def _experts(chunk_e, n_valid, xs, w1, w3, w2, layer):
    P, D = xs.shape
    F = w1.shape[-1]
    ch = MOE_CHUNK
    rows = lambda c, ce, nv: (jnp.minimum(c, nv[0] - 1), 0)
    wsel = lambda c, ce, nv: (layer, ce[c], 0, 0)
    return pl.pallas_call(
        _expert_kernel,
        out_shape=jax.ShapeDtypeStruct((P, D), F32),
        grid_spec=pltpu.PrefetchScalarGridSpec(
            num_scalar_prefetch=2,
            grid=(P // ch,),
            in_specs=[
                pl.BlockSpec((ch, D), rows),
                pl.BlockSpec((1, 1, D, F), wsel),
                pl.BlockSpec((1, 1, D, F), wsel),
                pl.BlockSpec((1, 1, F, D), wsel),
            ],
            out_specs=pl.BlockSpec((ch, D), lambda c, ce, nv: (c, 0)),
            scratch_shapes=[pltpu.VMEM((D, F), BF16), pltpu.VMEM((D, F), BF16), pltpu.VMEM((F, D), BF16)],
        ),
        compiler_params=_cparams(("arbitrary",), 56),
        name="moe_experts",
    )(chunk_e, n_valid, xs, w1, w3, w2)


def _combine_kernel(dest_ref, h_ref, info_ref, rows_ref, p_ref, g_ref, gw_ref, gb_ref, pw_ref, fn_ref,
                    o_ref, buf_a, buf_b, sem, *, tt, n_tok, n_steps, final):
    i = pl.program_id(0)
    bufs = (buf_a, buf_b)

    def row(tile, sl, k, r):
        return _row_copy(rows_ref, dest_ref[k * n_tok + tile * tt + r], bufs[sl].at[k], r, sem.at[sl])

    def wait_rows(sl):
        for k in range(TOP_K_IN_GROUP):
            pltpu.make_async_copy(rows_ref.at[pl.ds(0, tt)], bufs[sl].at[k], sem.at[sl]).wait()

    @pl.when(i == 0)
    def _():
        def issue(r, c):
            for k in range(TOP_K_IN_GROUP):
                row(0, 0, k, r).start(priority=k % 2)
            return c

        lax.fori_loop(0, tt, issue, 0, unroll=8)

    def step(sl):
        wait_rows(sl)
        nxt = jnp.minimum(i + 1, n_steps - 1)
        for r in range(tt):
            for k in range(TOP_K_IN_GROUP):
                row(nxt, 1 - sl, k, r).start(priority=k % 2)
        info = info_ref[...]
        y = h_ref[...] + info[:, 2:3] * bufs[sl][0] + info[:, 3:4] * bufs[sl][1]
        hn = _rms(y, g_ref[...])
        gate = jax.nn.sigmoid(_dot(hn.astype(BF16), gw_ref[...]) + gb_ref[...])
        out = y + _dot(p_ref[0].astype(BF16), pw_ref[...]) * gate
        if final:
            out = _rms(out, fn_ref[...])
        o_ref[...] = out
        pl.when(i == n_steps - 1)(functools.partial(wait_rows, 1 - sl))

    for parity in range(2):
        pl.when(i % 2 == parity)(functools.partial(step, parity))


def _combine(dest, h2, info, rows, p3, layer, g, gw, gb, pw, fn, final):
    T, D = h2.shape
    PD = p3.shape[-1]
    tt = min(CMB_TILE, T)
    full = lambda i, d: (0, 0)
    return pl.pallas_call(
        functools.partial(_combine_kernel, tt=tt, n_tok=T, n_steps=T // tt, final=final),
        out_shape=jax.ShapeDtypeStruct((T, D), F32),
        grid_spec=pltpu.PrefetchScalarGridSpec(
            num_scalar_prefetch=1,
            grid=(T // tt,),
            in_specs=[
                pl.BlockSpec((tt, D), lambda i, d: (i, 0)),
                pl.BlockSpec((tt, LANES), lambda i, d: (i, 0)),
                pl.BlockSpec(memory_space=pl.ANY),
                pl.BlockSpec((1, tt, PD), lambda i, d: (layer, i, 0)),
                pl.BlockSpec((1, D), full),
                pl.BlockSpec((D, D), full),
                pl.BlockSpec((1, D), full),
                pl.BlockSpec((PD, D), full),
                pl.BlockSpec((1, D), full),
            ],
            out_specs=pl.BlockSpec((tt, D), lambda i, d: (i, 0)),
            scratch_shapes=[pltpu.VMEM((TOP_K_IN_GROUP, tt, D), F32), pltpu.VMEM((TOP_K_IN_GROUP, tt, D), F32),
                            pltpu.SemaphoreType.DMA((2,))],
        ),
        compiler_params=_cparams(("arbitrary",), 40),
        name="moe_combine_ple",
    )(dest, h2, info, rows, p3, g.reshape(1, D), gw.astype(BF16), gb.reshape(1, D), pw.astype(BF16),
      fn.reshape(1, D))


def _moe_ple_layer(h2, p3, layer, norm_ffn, rg_w, rg_b, re_w, re_b, w1, w3, w2, norm_ple, gate_w, gate_b, ple_proj,
                   final_norm, final):
    T, D = h2.shape
    A = T * TOP_K_IN_GROUP
    pad = LANES - N_EXPERT_GROUPS - N_EXPERTS
    wr = jnp.concatenate([rg_w, re_w, jnp.zeros((D, pad), F32)], axis=1)
    br = jnp.concatenate([rg_b, re_b, jnp.zeros((pad,), F32)]).reshape(1, LANES)
    info, cnt = _router(h2, norm_ffn, wr, br)
    counts = cnt[0, N_EXPERT_GROUPS:N_EXPERT_GROUPS + N_EXPERTS].astype(I32)
    n_chunks_e = (counts + MOE_CHUNK - 1) // MOE_CHUNK
    chunk_end = jnp.cumsum(n_chunks_e)
    pstarts = (chunk_end - n_chunks_e) * MOE_CHUNK
    n_chunks = -(-A // MOE_CHUNK) + N_EXPERTS
    n_valid = chunk_end[-1:].astype(I32)
    cidx = jnp.minimum(jnp.arange(n_chunks, dtype=I32), n_valid[0] - 1)
    chunk_e = jnp.sum((chunk_end[None, :] <= cidx[:, None]).astype(I32), axis=1)
    chunk_e = jnp.minimum(chunk_e, N_EXPERTS - 1)
    e_idx = info[:, 0:TOP_K_IN_GROUP].astype(I32)
    rank = info[:, 4:4 + TOP_K_IN_GROUP].astype(I32)
    start = jnp.sum(jnp.where(e_idx[..., None] == jnp.arange(N_EXPERTS, dtype=I32), pstarts, 0), axis=-1)
    dest = (start + rank).T.reshape(A)
    xs = _dispatch(dest, chunk_end.astype(I32), h2, norm_ffn, n_chunks * MOE_CHUNK)
    rows = _experts(chunk_e, n_valid, xs, w1, w3, w2, layer)
    return _combine(dest, h2, info, rows, p3, layer, norm_ple, gate_w, gate_b, ple_proj, final_norm, final)


def _rope_rows(xt, cos, sin):
    x1 = xt[0:ROPE_HALF]
    x2 = xt[ROPE_HALF:ROPE_DIMS]
    return jnp.concatenate([x1 * cos - x2 * sin, x2 * cos + x1 * sin, xt[ROPE_DIMS:]], axis=0)


def _rope_tables(pos):
    inv = jnp.float32(ROPE_THETA) ** (-jnp.arange(ROPE_HALF, dtype=F32) * 2.0 / ROPE_DIMS)
    ang = pos.astype(F32)[:, None] * inv[None, :]
    return jnp.cos(ang), jnp.sin(ang)


def _lane_rope_tables(pos):
    cos, sin = _rope_tables(pos)
    n = pos.shape[0]
    ones = jnp.ones((n, HEAD_DIM - ROPE_DIMS), F32)
    zeros = jnp.zeros((n, HEAD_DIM - ROPE_DIMS), F32)
    zh = jnp.zeros((n, ROPE_HALF), F32)
    two = lambda a: jnp.concatenate([a, a], axis=1)
    return (two(jnp.concatenate([cos, cos, ones], axis=1)), two(jnp.concatenate([zh, sin, zeros], axis=1)),
            two(jnp.concatenate([-sin, zh, zeros], axis=1)))


def _rope_lanes(x, c, s1, s2):
    return x * c + pltpu.roll(x, ROPE_HALF, axis=1) * s1 + pltpu.roll(x, LANES - ROPE_HALF, axis=1) * s2


def _kv_kernel(h_ref, g_ref, wn_ref, wt_ref, c_ref, s1_ref, s2_ref, cv_ref, ksa_ref, vst_ref, kw_ref, vwt_ref, *, ts):
    i = pl.program_id(1)
    kvw = N_KV_GROUPS * HEAD_DIM
    hn = _rms(h_ref[0], g_ref[...]).astype(BF16)
    nat = _dot(hn, wn_ref[...])
    cv_ref[0] = nat[:, 0:2 * kvw]
    tr = _dot_nt(wt_ref[...], hn)
    c, s1, s2 = c_ref[...], s1_ref[...], s2_ref[...]
    roped = [_rope_lanes(nat[:, 2 * kvw + k * LANES:2 * kvw + (k + 1) * LANES], c, s1, s2)
             for k in range(2 * kvw // LANES)]
    lane = lax.broadcasted_iota(I32, (ts, KEY_AUG), 1)
    pos = i * ts + lax.broadcasted_iota(I32, (ts, KEY_AUG), 0)
    onehot = jnp.where(lane - HEAD_DIM == pos // SEL_BLOCK, 1.0, 0.0).astype(BF16)
    ones_row = jnp.where(lax.broadcasted_iota(I32, (V_ROWS - HEAD_DIM, ts), 0) == 0, 1.0, 0.0).astype(BF16)
    per_tile = LANES // HEAD_DIM
    for g in range(N_KV_GROUPS):
        lo = (g % per_tile) * HEAD_DIM
        ksa_ref[0, g] = onehot
        ksa_ref[0, g, :, 0:HEAD_DIM] = roped[g // per_tile][:, lo:lo + HEAD_DIM].astype(BF16)
        kw_ref[0, g] = roped[N_KV_GROUPS // per_tile + g // per_tile][:, lo:lo + HEAD_DIM].astype(BF16)
        for ref, base in ((vst_ref, 0), (vwt_ref, kvw)):
            ref[0, g, 0:HEAD_DIM, :] = tr[base + g * HEAD_DIM:base + (g + 1) * HEAD_DIM].astype(BF16)
            ref[0, g, HEAD_DIM:, :] = ones_row


def _shared_kv_proj(h, kv_norm, w_kv):
    B, S, D = h.shape
    G, dh = N_KV_GROUPS, HEAD_DIM
    kvw = G * dh
    ts = min(SEQ_TILE, S)
    br = lambda k: w_kv[:, k * kvw:(k + 1) * kvw]
    w_nat = jnp.concatenate([br(0), br(1), br(2), br(4)], axis=1).astype(BF16)
    w_tr = jnp.concatenate([br(3), br(5)], axis=1).T.astype(BF16)
    tabs = _lane_rope_tables(jnp.arange(S))
    tab = pl.BlockSpec((ts, LANES), lambda b, i: (i, 0))
    return pl.pallas_call(
        functools.partial(_kv_kernel, ts=ts),
        out_shape=(
            jax.ShapeDtypeStruct((B, S, 2 * kvw), F32),
            jax.ShapeDtypeStruct((B, G, S, KEY_AUG), BF16),
            jax.ShapeDtypeStruct((B, G, V_ROWS, S), BF16),
            jax.ShapeDtypeStruct((B, G, S, dh), BF16),
            jax.ShapeDtypeStruct((B, G, V_ROWS, S), BF16),
        ),
        grid=(B, S // ts),
        in_specs=[
            pl.BlockSpec((1, ts, D), lambda b, i: (b, i, 0)),
            pl.BlockSpec((1, D), lambda b, i: (0, 0)),
            pl.BlockSpec((D, 4 * kvw), lambda b, i: (0, 0)),
            pl.BlockSpec((2 * kvw, D), lambda b, i: (0, 0)),
            tab, tab, tab,
        ],
        out_specs=(
            pl.BlockSpec((1, ts, 2 * kvw), lambda b, i: (b, i, 0)),
            pl.BlockSpec((1, G, ts, KEY_AUG), lambda b, i: (b, 0, i, 0)),
            pl.BlockSpec((1, G, V_ROWS, ts), lambda b, i: (b, 0, 0, i)),
            pl.BlockSpec((1, G, ts, dh), lambda b, i: (b, 0, i, 0)),
            pl.BlockSpec((1, G, V_ROWS, ts), lambda b, i: (b, 0, 0, i)),
        ),
        compiler_params=_cparams(("parallel", "parallel"), 48),
        name="shared_kv_proj",
    )(h, kv_norm.reshape(1, D), w_nat, w_tr, *tabs)


def _compress_kernel(x_ref, pos_ref, w1_ref, b1_ref, w2_ref, b2_ref, c_ref, s1_ref, s2_ref, o_ref, *, nh, keys):
    x = x_ref[0, 0, 0]
    a = _dot((x + pos_ref[0:1]).astype(BF16), w1_ref[0])
    b = _dot((x + pos_ref[1:2]).astype(BF16), w1_ref[1])
    hid = jax.nn.gelu(a + pltpu.roll(b, nh - 1, axis=0) + b1_ref[...]).astype(BF16)
    if keys:
        out = _rope_lanes(_dot(hid, w2_ref[...]) + b2_ref[...], c_ref[...], s1_ref[...], s2_ref[...])
        o_ref[0, 0] = out[:, 0:HEAD_DIM].astype(BF16)
    else:
        o_ref[0, 0] = (_dot_nt(w2_ref[...], hid) + b2_ref[...]).astype(BF16)


def _compress(halves, which, pos_emb, w1, b1, w2, b2, keys):
    _, B, G, nh, hw = halves.shape
    dh = HEAD_DIM
    hidden = w1.shape[-1]
    pos2 = pos_emb.reshape(2, hw)
    w1s = w1.reshape(2, hw, hidden).astype(BF16)
    tabs = _lane_rope_tables(jnp.arange(nh) * CMP_STRIDE + CMP_BLOCK - 1)
    if keys:
        w2a = jnp.pad(w2, ((0, 0), (0, LANES - dh))).astype(BF16)
        b2a = jnp.pad(b2, (0, LANES - dh)).reshape(1, LANES)
        out_shape, out_block = (B, G, nh, dh), (1, 1, nh, dh)
    else:
        w2a, b2a = w2.T.astype(BF16), b2.reshape(dh, 1)
        out_shape, out_block = (B, G, dh, nh), (1, 1, dh, nh)
    c2 = lambda b, g: (0, 0)
    tab = pl.BlockSpec((nh, LANES), c2)
    return pl.pallas_call(
        functools.partial(_compress_kernel, nh=nh, keys=keys),
        out_shape=jax.ShapeDtypeStruct(out_shape, BF16),
        grid=(B, G),
        in_specs=[
            pl.BlockSpec((1, 1, 1, nh, hw), lambda b, g: (which, b, g, 0, 0)),
            pl.BlockSpec((2, hw), c2),
            pl.BlockSpec((2, hw, hidden), lambda b, g: (0, 0, 0)),
            pl.BlockSpec((1, hidden), c2),
            pl.BlockSpec(w2a.shape, c2),
            pl.BlockSpec(b2a.shape, c2),
            tab, tab, tab,
        ],
        out_specs=pl.BlockSpec(out_block, lambda b, g: (b, g, 0, 0)),
        compiler_params=_cparams(("parallel", "parallel"), 40),
        name="compress_k" if keys else "compress_v",
    )(halves, pos2, w1s, b1.reshape(1, hidden), w2a, b2a, *tabs)


def _qproj_kernel(h_ref, g_ref, wqt_ref, wg_ref, bg_ref, cos_ref, sin_ref, qt_ref, gt_ref):
    xn = _rms(h_ref[0], g_ref[...]).astype(BF16)
    tr = _dot_nt(wqt_ref[...], xn)
    cos, sin = cos_ref[...], sin_ref[...]
    for hd in range(N_HEADS):
        rows = slice(hd * HEAD_DIM, (hd + 1) * HEAD_DIM)
        qt_ref[0, rows, :] = (_rope_rows(tr[rows], cos, sin) * Q_SCALE).astype(BF16)
    gt_ref[0] = jax.nn.sigmoid(_dot(xn, wg_ref[...]) + bg_ref[...])


def _q_proj(h, g, w_qg, b_gate):
    B, S, D = h.shape
    HD = N_HEADS * HEAD_DIM
    ng = N_HEADS * N_BRANCH
    ts = min(SEQ_TILE, S)
    wqt = w_qg[:, :HD].T.astype(BF16)
    wg = jnp.pad(w_qg[:, HD:], ((0, 0), (0, LANES - ng))).astype(BF16)
    bg = jnp.pad(b_gate, (0, LANES - ng)).reshape(1, LANES)
    cos, sin = _rope_tables(jnp.arange(S))
    tab = pl.BlockSpec((ROPE_HALF, ts), lambda b, i: (0, i))
    return pl.pallas_call(
        _qproj_kernel,
        out_shape=(jax.ShapeDtypeStruct((B, HD, S), BF16), jax.ShapeDtypeStruct((B, S, LANES), F32)),
        grid=(B, S // ts),
        in_specs=[
            pl.BlockSpec((1, ts, D), lambda b, i: (b, i, 0)),
            pl.BlockSpec((1, D), lambda b, i: (0, 0)),
            pl.BlockSpec((HD, D), lambda b, i: (0, 0)),
            pl.BlockSpec((D, LANES), lambda b, i: (0, 0)),
            pl.BlockSpec((1, LANES), lambda b, i: (0, 0)),
            tab, tab,
        ],
        out_specs=(pl.BlockSpec((1, HD, ts), lambda b, i: (b, 0, i)),
                   pl.BlockSpec((1, ts, LANES), lambda b, i: (b, i, 0))),
        compiler_params=_cparams(("parallel", "parallel"), 40),
        name="nsa_q_proj",
    )(h, g.reshape(1, D), wqt, wg, bg, cos.T, sin.T)


def _heads_on_lanes(qt):
    return jnp.concatenate([qt[r * HEAD_DIM:(r + 1) * HEAD_DIM] for r in range(HEADS_PER_GROUP)], axis=1)


def _lane_query_pos(i, tq):
    lanes = lax.broadcasted_iota(I32, (1, HEADS_PER_GROUP * tq), 1)
    return i * tq + (lanes & (tq - 1))


def _gated(acc_t, gates_t, g, branch, denom_row):
    row = lax.broadcasted_iota(I32, gates_t.shape, 0)
    scale = jnp.concatenate(
        [jnp.sum(jnp.where(row == (g * HEADS_PER_GROUP + r) * N_BRANCH + branch, gates_t, 0.0), axis=0, keepdims=True)
         for r in range(HEADS_PER_GROUP)], axis=1)
    if denom_row is not None:
        scale = scale / acc_t[denom_row:denom_row + 1]
    return acc_t[0:HEAD_DIM] * scale


def _unstack_heads(o_t, tq):
    per = LANES // HEAD_DIM
    cols = []
    for r in range(0, HEADS_PER_GROUP, per):
        cols.append(jnp.concatenate([o_t[:, (r + k) * tq:(r + k + 1) * tq] for k in range(per)], axis=0).T)
    return jnp.concatenate(cols, axis=-1)


def _cmp_kernel(qt_ref, gt_ref, k_ref, vt_ref, c2s_ref, o_ref, bias_ref, *, tq, nc):
    g = pl.program_id(1)
    i = pl.program_id(2)

    def body(nk, nb):
        s = _dot(k_ref[0, 0, 0:nk, :], _heads_on_lanes(qt_ref[0]))
        t = _lane_query_pos(i, tq)
        n = lax.broadcasted_iota(I32, (nk, 1), 0)
        s = jnp.where(n * CMP_STRIDE + (CMP_BLOCK - 1) <= t, s, -jnp.inf)
        m = jnp.max(s, axis=0, keepdims=True)
        m = jnp.where(m == -jnp.inf, 0.0, m)
        e = jnp.exp2(s - m)
        p = e * (1.0 / jnp.maximum(jnp.sum(e, axis=0, keepdims=True), 1e-30))
        ot = _dot(vt_ref[0, 0, :, 0:nk], p.astype(BF16))
        o_ref[0] = _unstack_heads(_gated(ot, gt_ref[0].T, g, 0, None), tq).astype(BF16)
        ps = p[:, 0:tq]
        for r in range(1, HEADS_PER_GROUP):
            ps = ps + p[:, r * tq:(r + 1) * tq]
        hi = ps.astype(BF16)
        rem = ps - hi.astype(F32)
        mid = rem.astype(BF16)
        lo = (rem - mid.astype(F32)).astype(BF16)
        c2s = c2s_ref[0:nb, 0:nk]
        imp = _dot(c2s, hi) + _dot(c2s, mid) + _dot(c2s, lo)
        cur = (i * tq + lax.broadcasted_iota(I32, (1, tq), 1)) // SEL_BLOCK
        j = lax.broadcasted_iota(I32, (nb, tq), 0)
        valid = j <= cur
        forced = jnp.where(j == 0, 1.0, jnp.where(j == cur, 1.0, jnp.where(j == cur - 1, 1.0, 0.0)))
        sel = jnp.where(valid, forced, 0.0)
        work = jnp.where(valid, jnp.where(forced > 0.0, -jnp.inf, imp), -jnp.inf)
        for _ in range(N_SELECT - N_FORCED):
            mx = jnp.max(work, axis=0, keepdims=True)
            idx = jnp.min(jnp.where(work == mx, j, MAX_SEL_BLOCKS), axis=0, keepdims=True)
            pick = j == idx
            sel = jnp.where(pick, 1.0, sel)
            work = jnp.where(pick, -jnp.inf, work)
        bias_ref[0, 0, 0:nb, :] = jnp.where(valid, jnp.where(sel > 0.0, 0.0, MASK_BIAS), MASK_BIAS).astype(BF16)
        if nb < MAX_SEL_BLOCKS:
            bias_ref[0, 0, nb:, :] = jnp.full((MAX_SEL_BLOCKS - nb, tq), MASK_BIAS, BF16)

    per_class = LANES * CMP_STRIDE // tq
    n_class = -(-(nc * CMP_STRIDE // tq) // per_class)
    for cls in range(n_class):
        nk = min(nc, LANES * (cls + 1))
        nb = min(MAX_SEL_BLOCKS, nk * CMP_STRIDE // SEL_BLOCK)
        pl.when(i // per_class == cls)(functools.partial(body, nk, nb))


def _cmp_attn(qt, gates, kc, vct):
    B, HD, S = qt.shape
    G, dh = N_KV_GROUPS, HEAD_DIM
    gw = HD // G
    nc = kc.shape[2]
    tq = min(Q_TILE, S)
    n0 = jnp.arange(nc)[None, :] * CMP_STRIDE
    j0 = jnp.arange(MAX_SEL_BLOCKS)[:, None] * SEL_BLOCK
    c2s = ((n0 < j0 + SEL_BLOCK) & (n0 + CMP_BLOCK > j0)).astype(BF16)
    return pl.pallas_call(
        functools.partial(_cmp_kernel, tq=tq, nc=nc),
        out_shape=(jax.ShapeDtypeStruct((B, S, HD), BF16), jax.ShapeDtypeStruct((B, G, MAX_SEL_BLOCKS, S), BF16)),
        grid=(B, G, S // tq),
        in_specs=[
            pl.BlockSpec((1, gw, tq), lambda b, g, i: (b, g, i)),
            pl.BlockSpec((1, tq, LANES), lambda b, g, i: (b, i, 0)),
            pl.BlockSpec((1, 1, nc, dh), lambda b, g, i: (b, g, 0, 0)),
            pl.BlockSpec((1, 1, dh, nc), lambda b, g, i: (b, g, 0, 0)),
            pl.BlockSpec((MAX_SEL_BLOCKS, nc), lambda b, g, i: (0, 0)),
        ],
        out_specs=(pl.BlockSpec((1, tq, gw), lambda b, g, i: (b, i, g)),
                   pl.BlockSpec((1, 1, MAX_SEL_BLOCKS, tq), lambda b, g, i: (b, g, 0, i))),
        compiler_params=_cparams(("parallel", "parallel", "parallel"), 40),
        name="nsa_compressed",
    )(qt, gates, kc, vct, c2s)


def _sel_win_kernel(qt_ref, bias_ref, gt_ref, k_ref, vt_ref, kw_ref, vwt_ref, o_ref, qa, m_s, acc, acc_w, s_a, s_b,
                    *, tq, tk):
    g = pl.program_id(1)
    i = pl.program_id(2)
    qt = qt_ref[0]
    bias = bias_ref[0, 0]
    for r in range(HEADS_PER_GROUP):
        cols = slice(r * tq, (r + 1) * tq)
        qa[0:HEAD_DIM, cols] = qt[r * HEAD_DIM:(r + 1) * HEAD_DIM]
        qa[HEAD_DIM:HEAD_DIM + MAX_SEL_BLOCKS, cols] = bias
        qa[HEAD_DIM + MAX_SEL_BLOCKS:, cols] = jnp.zeros((KEY_AUG - HEAD_DIM - MAX_SEL_BLOCKS, tq), BF16)
    m_s[...] = jnp.full(m_s.shape, NEG_BIG, F32)
    acc[...] = jnp.zeros(acc.shape, F32)
    width = HEADS_PER_GROUP * tq

    def qk(jt, buf):
        buf[...] = _dot(k_ref[0, 0, pl.ds(pl.multiple_of(jt * tk, tk), tk), :], qa[...])

    def online_softmax(key0, n_keys, s):
        m_old = m_s[...]
        m_new = jnp.maximum(m_old, jnp.max(s, axis=0, keepdims=True))
        p = jnp.exp2(s - m_new).astype(BF16)
        acc[...] = jnp.exp2(m_old - m_new) * acc[...] + _dot(vt_ref[0, 0, :, pl.ds(key0, n_keys)], p)
        m_s[...] = m_new

    def absorb(jt, buf):
        online_softmax(pl.multiple_of(jt * tk, tk), tk, buf[...])

    key_row = lax.broadcasted_iota(I32, (tq, 1), 0)
    query_col = lax.broadcasted_iota(I32, (1, width), 1) & (tq - 1)

    def own_tile(s):
        return jnp.where(key_row <= query_col, s, NEG_BIG)

    def absorb_last(jt, buf):
        k0 = pl.multiple_of(jt * tk, tk)

        @pl.when(i % 2 == 1)
        def _():
            online_softmax(k0, tq, buf[0:tq, :])
            online_softmax(pl.multiple_of(k0 + tq, tq), tq, own_tile(buf[tq:tk, :]))

        @pl.when(i % 2 == 0)
        def _():
            online_softmax(k0, tq, own_tile(buf[0:tq, :]))

    def pair(u, c):
        qk(2 * u + 1, s_b)
        absorb(2 * u, s_a)
        qk(2 * u + 2, s_a)
        absorb(2 * u + 1, s_b)
        return c

    def window(miss):
        n_keys = (nw + 1 - miss) * tq
        k0 = pl.multiple_of((i - nw + miss) * tq, tq)
        s_b[0:n_keys, :] = _dot(kw_ref[0, 0, pl.ds(k0, n_keys), :], qa[0:HEAD_DIM, :])
        qk(0, s_a)
        parts = []
        for j in range(miss, nw + 1):
            part = s_b[(j - miss) * tq:(j - miss + 1) * tq, :]
            if j == 0:
                part = jnp.where(key_row > query_col, part, NEG_BIG)
            parts.append(own_tile(part) if j == nw else part)
        s = jnp.concatenate(parts, axis=0)
        m = jnp.max(s, axis=0, keepdims=True)
        p = jnp.exp2(s - m).astype(BF16)
        acc_w[...] = _dot(vwt_ref[0, 0, :, pl.ds(k0, n_keys)], p)

    nw = WINDOW // tq
    for miss in range(nw + 1):
        pl.when(jnp.maximum(nw - i, 0) == miss)(functools.partial(window, miss))

    last = (i * tq + tq + tk - 1) // tk - 1
    lax.fori_loop(0, last // 2, pair, 0)

    @pl.when(last % 2 == 1)
    def _():
        qk(last, s_b)
        absorb(last - 1, s_a)
        absorb_last(last, s_b)

    @pl.when(last % 2 == 0)
    def _():
        absorb_last(last, s_a)

    gates_t = gt_ref[0].T
    o_t = _gated(acc[...], gates_t, g, 1, HEAD_DIM) + _gated(acc_w[...], gates_t, g, 2, HEAD_DIM)
    o_ref[0] = _unstack_heads(o_t, tq).astype(BF16)


def _sel_win_attn(qt, bias, gates, ksa, vst, kw, vwt):
    B, HD, S = qt.shape
    G, dh = N_KV_GROUPS, HEAD_DIM
    gw = HD // G
    tq = min(SEL_Q_TILE, S // 2)
    tk = 2 * tq
    assert S % tk == 0 and WINDOW % tq == 0 and WINDOW + tq <= tk
    width = HEADS_PER_GROUP * tq
    return pl.pallas_call(
        functools.partial(_sel_win_kernel, tq=tq, tk=tk),
        out_shape=jax.ShapeDtypeStruct((B, S, HD), BF16),
        grid=(B, G, S // tq),
        in_specs=[
            pl.BlockSpec((1, gw, tq), lambda b, g, i: (b, g, i)),
            pl.BlockSpec((1, 1, MAX_SEL_BLOCKS, tq), lambda b, g, i: (b, g, 0, i)),
            pl.BlockSpec((1, tq, LANES), lambda b, g, i: (b, i, 0)),
            pl.BlockSpec((1, 1, S, KEY_AUG), lambda b, g, i: (b, g, 0, 0)),
            pl.BlockSpec((1, 1, V_ROWS, S), lambda b, g, i: (b, g, 0, 0)),
            pl.BlockSpec((1, 1, S, dh), lambda b, g, i: (b, g, 0, 0)),
            pl.BlockSpec((1, 1, V_ROWS, S), lambda b, g, i: (b, g, 0, 0)),
        ],
        out_specs=pl.BlockSpec((1, tq, gw), lambda b, g, i: (b, i, g)),
        scratch_shapes=[pltpu.VMEM((KEY_AUG, width), BF16), pltpu.VMEM((1, width), F32),
                        pltpu.VMEM((V_ROWS, width), F32), pltpu.VMEM((V_ROWS, width), F32),
                        pltpu.VMEM((tk, width), F32), pltpu.VMEM((tk, width), F32)],
        compiler_params=_cparams(("parallel", "parallel", "arbitrary"), 60),
        name="nsa_selected_window",
    )(qt, bias, gates, ksa, vst, kw, vwt)


def _oproj_kernel(h_ref, a_ref, b_ref, w_ref, o_ref):
    o = a_ref[...].astype(F32) + b_ref[...].astype(F32)
    o_ref[...] = h_ref[...] + _dot(o.astype(BF16), w_ref[...])


def _out_proj(h2, oc, osw, w_o):
    T, D = h2.shape
    HD = oc.shape[-1]
    tt = min(SEQ_TILE, T)
    blk = lambda w: pl.BlockSpec((tt, w), lambda i: (i, 0))
    return pl.pallas_call(
        _oproj_kernel,
        out_shape=jax.ShapeDtypeStruct((T, D), F32),
        grid=(T // tt,),
        in_specs=[blk(D), blk(HD), blk(HD), pl.BlockSpec((HD, D), lambda i: (0, 0))],
        out_specs=blk(D),
        compiler_params=_cparams(("parallel",), 40),
        name="nsa_out_proj",
    )(h2, oc, osw, w_o.astype(BF16))


def _nsa_layer(h, g, w_qg, b_gate, w_o, shared):
    B, S, D = h.shape
    kc, vct, ksa, vst, kw, vwt = shared
    qt, gates = _q_proj(h, g, w_qg, b_gate)
    oc, bias = _cmp_attn(qt, gates, kc, vct)
    osw = _sel_win_attn(qt, bias, gates, ksa, vst, kw, vwt)
    flat = lambda a: a.reshape(B * S, a.shape[-1])
    return _out_proj(flat(h), flat(oc), flat(osw), w_o).reshape(B, S, D)


def _shared_kv(h, kv_norm, w_kv, ck, cv):
    B, S, _ = h.shape
    G, dh = N_KV_GROUPS, HEAD_DIM
    assert S % SEL_BLOCK == 0 and N_SELECT <= S // SEL_BLOCK <= MAX_SEL_BLOCKS
    cvals, ksa, vst, kw, vwt = _shared_kv_proj(h, kv_norm, w_kv)
    halves = cvals.reshape(B, S, 2, G, dh).transpose(2, 0, 3, 1, 4).reshape(2, B, G, S // CMP_STRIDE, CMP_STRIDE * dh)
    kc = _compress(halves, 0, *ck, keys=True)
    vct = _compress(halves, 1, *cv, keys=False)
    return kc, vct, ksa, vst, kw, vwt


def kernel(x, p, norm_mix, norm_ffn, norm_ple, pool_w, pool_b, pool_scale, kv_norm, w_kv, cmp_k_pos, cmp_k_w1, cmp_k_b1, cmp_k_w2, cmp_k_b2, cmp_v_pos, cmp_v_w1, cmp_v_b1, cmp_v_w2, cmp_v_b2, w_qg, b_gate, w_o, router_g_w, router_g_b, router_e_w, router_e_b, moe_w1, moe_w3, moe_w2, ple_proj, ple_gate_w, ple_gate_b, final_norm):
    B, S, D = x.shape
    depth = p.shape[0]
    n_a = pool_w.shape[0]
    T = B * S
    h = x
    shared = None
    for i in range(depth):
        if i == n_a:
            shared = _shared_kv(h, kv_norm, w_kv,
                                (cmp_k_pos, cmp_k_w1, cmp_k_b1, cmp_k_w2, cmp_k_b2),
                                (cmp_v_pos, cmp_v_w1, cmp_v_b1, cmp_v_w2, cmp_v_b2))
        if i < n_a:
            h = _pool_layer(h, norm_mix[i], pool_w[i], pool_b[i], pool_scale[i])
        else:
            j = i - n_a
            h = _nsa_layer(h, norm_mix[i], w_qg[j], b_gate[j], w_o[j], shared)
        h = _moe_ple_layer(h.reshape(T, D), p.reshape(depth, T, p.shape[-1]), i, norm_ffn[i], router_g_w[i],
                           router_g_b[i], router_e_w[i], router_e_b[i], moe_w1, moe_w3, moe_w2, norm_ple[i],
                           ple_gate_w[i], ple_gate_b[i], ple_proj[i], final_norm, i == depth - 1).reshape(B, S, D)
    return h
```

```python
import functools

import jax
import jax.numpy as jnp
from jax import lax
from jax.experimental import pallas as pl
from jax.experimental.pallas import tpu as pltpu

F32 = jnp.float32
BF16 = jnp.bfloat16
I32 = jnp.int32

POOL_WINDOWS = (2, 4, 8, 16)
N_HEADS = 16
HEAD_DIM = 64
N_KV_GROUPS = 4
HEADS_PER_GROUP = N_HEADS // N_KV_GROUPS
N_BRANCH = 3
ROPE_DIMS = HEAD_DIM // 4
ROPE_HALF = ROPE_DIMS // 2
ROPE_THETA = 500000.0
CMP_BLOCK = 32
CMP_STRIDE = 16
SEL_BLOCK = 64
N_SELECT = 16
WINDOW = 512
FORCE_BONUS = 1e4
N_FORCED = 3
N_EXPERT_GROUPS = 4
EXPERTS_PER_GROUP = 8
N_EXPERTS = N_EXPERT_GROUPS * EXPERTS_PER_GROUP
TOP_K_IN_GROUP = 2
RMS_EPS = 1e-6

LANES = 128
MAX_SEL_BLOCKS = LANES
MASK_BIAS = -30000.0

LOG2E = 1.4426950408889634
Q_SCALE = HEAD_DIM ** -0.5 * LOG2E
V_ROWS = HEAD_DIM + 16
KEY_AUG = 2 * LANES
NEG_BIG = -1e30

SEQ_TILE = 512
TOK_TILE = 512
CMB_TILE = 256
MOE_CHUNK = 512
Q_TILE = 512
SEL_Q_TILE = 512
HALO = 16


def _cparams(sem, vmem_mb):
    return pltpu.CompilerParams(dimension_semantics=sem, vmem_limit_bytes=vmem_mb * 1024 * 1024)


def _rms(x, g):
    return x * lax.rsqrt(jnp.mean(x * x, axis=-1, keepdims=True) + RMS_EPS) * g


def _dot(a, b):
    return jnp.dot(a, b, preferred_element_type=F32)


def _dot_nt(a, b):
    return lax.dot_general(a, b, (((1,), (1,)), ((), ())), preferred_element_type=F32)


def _pool_kernel(h_ref, halo_ref, g_ref, w_ref, b_ref, sc_ref, o_ref, *, ts, cg):
    i = pl.program_id(1)
    x = h_ref[0]
    g = g_ref[...]
    xn = _rms(x, g)
    hn = _rms(halo_ref[0], g)
    hn = jnp.where(i > 0, hn, 0.0)
    ext = jnp.concatenate([hn, xn], axis=0)
    t = i * ts + lax.broadcasted_iota(I32, (ts, 1), 0)
    outs = []
    for gi, w in enumerate(POOL_WINDOWS):
        s = ext[:, gi * cg:(gi + 1) * cg]
        k = 1
        while k < w:
            s = s + pltpu.roll(s, k, axis=0)
            k *= 2
        cnt = jnp.minimum(t + 1, w).astype(F32)
        pooled = s[HALO:] / cnt - xn[:, gi * cg:(gi + 1) * cg]
        outs.append(_dot(pooled.astype(BF16), w_ref[gi]))
    y = jnp.concatenate(outs, axis=-1)
    o_ref[0] = x + (y + b_ref[...]) * sc_ref[...]


def _pool_layer(h, g, w, b, sc):
    B, S, D = h.shape
    ts = min(SEQ_TILE, S)
    cg = D // len(POOL_WINDOWS)
    row = lambda v: v.reshape(1, D)
    return pl.pallas_call(
        functools.partial(_pool_kernel, ts=ts, cg=cg),
        out_shape=jax.ShapeDtypeStruct((B, S, D), F32),
        grid=(B, S // ts),
        in_specs=[
            pl.BlockSpec((1, ts, D), lambda b_, i: (b_, i, 0)),
            pl.BlockSpec((1, HALO, D), lambda b_, i: (b_, jnp.maximum(i * (ts // HALO) - 1, 0), 0)),
            pl.BlockSpec((1, D), lambda b_, i: (0, 0)),
            pl.BlockSpec((len(POOL_WINDOWS), cg, cg), lambda b_, i: (0, 0, 0)),
            pl.BlockSpec((1, D), lambda b_, i: (0, 0)),
            pl.BlockSpec((1, D), lambda b_, i: (0, 0)),
        ],
        out_specs=pl.BlockSpec((1, ts, D), lambda b_, i: (b_, i, 0)),
        compiler_params=_cparams(("parallel", "parallel"), 40),
        name="pool_mixer",
    )(h, h, row(g), w.astype(BF16), row(b), row(sc))


def _router_kernel(h_ref, g_ref, wh_ref, wl_ref, b_ref, info_ref, cnt_ref, *, tt):
    i = pl.program_id(0)

    @pl.when(i == 0)
    def _():
        cnt_ref[...] = jnp.zeros_like(cnt_ref)

    xn = _rms(h_ref[...], g_ref[...])
    xh = xn.astype(BF16)
    xl = (xn - xh.astype(F32)).astype(BF16)
    logits = _dot(xh, wh_ref[...]) + (_dot(xh, wl_ref[...]) + _dot(xl, wh_ref[...])) + b_ref[...]
    lane = lax.broadcasted_iota(I32, (tt, LANES), 1)
    neg = -jnp.inf
    gl = jnp.where(lane < N_EXPERT_GROUPS, logits, neg)
    gmax = jnp.max(gl, axis=-1, keepdims=True)
    grp = jnp.min(jnp.where(gl == gmax, lane, LANES), axis=-1, keepdims=True)
    gprob = 1.0 / jnp.sum(jnp.exp(gl - gmax), axis=-1, keepdims=True)
    lo = N_EXPERT_GROUPS + grp * EXPERTS_PER_GROUP
    el = jnp.where(lane >= lo, jnp.where(lane < lo + EXPERTS_PER_GROUP, logits, neg), neg)
    v1 = jnp.max(el, axis=-1, keepdims=True)
    i1 = jnp.min(jnp.where(el == v1, lane, LANES), axis=-1, keepdims=True)
    el2 = jnp.where(lane == i1, neg, el)
    v2 = jnp.max(el2, axis=-1, keepdims=True)
    i2 = jnp.min(jnp.where(el2 == v2, lane, LANES), axis=-1, keepdims=True)
    e2 = jnp.exp(v2 - v1)
    w1 = gprob / (1.0 + e2)
    w2 = gprob * e2 / (1.0 + e2)
    oh1 = lane == i1
    oh2 = lane == i2
    oh = jnp.where(oh1, 1.0, jnp.where(oh2, 1.0, 0.0))
    r_ = lax.broadcasted_iota(I32, (tt, tt), 0)
    c_ = lax.broadcasted_iota(I32, (tt, tt), 1)
    tri = jnp.where(r_ > c_, 1.0, 0.0).astype(BF16)
    tot = _dot(tri, oh.astype(BF16)) + cnt_ref[...]
    r1 = jnp.sum(jnp.where(oh1, tot, 0.0), axis=-1, keepdims=True)
    r2 = jnp.sum(jnp.where(oh2, tot, 0.0), axis=-1, keepdims=True)
    cnt_ref[...] = cnt_ref[...] + jnp.sum(oh, axis=0, keepdims=True)
    vals = (i1.astype(F32) - N_EXPERT_GROUPS, i2.astype(F32) - N_EXPERT_GROUPS, w1, w2, r1, r2)
    info = jnp.zeros((tt, LANES), F32)
    for k, v in enumerate(vals):
        info = jnp.where(lane == k, v, info)
    info_ref[...] = info


def _router(h2, g, wr, br):
    T, D = h2.shape
    tt = min(TOK_TILE, T)
    return pl.pallas_call(
        functools.partial(_router_kernel, tt=tt),
        out_shape=(jax.ShapeDtypeStruct((T, LANES), F32), jax.ShapeDtypeStruct((1, LANES), F32)),
        grid=(T // tt,),
        in_specs=[
            pl.BlockSpec((tt, D), lambda i: (i, 0)),
            pl.BlockSpec((1, D), lambda i: (0, 0)),
            pl.BlockSpec((D, LANES), lambda i: (0, 0)),
            pl.BlockSpec((D, LANES), lambda i: (0, 0)),
            pl.BlockSpec((1, LANES), lambda i: (0, 0)),
        ],
        out_specs=(pl.BlockSpec((tt, LANES), lambda i: (i, 0)), pl.BlockSpec((1, LANES), lambda i: (0, 0))),
        compiler_params=_cparams(("arbitrary",), 40),
        name="moe_router",
    )(h2, g.reshape(1, D), wr.astype(BF16), (wr - wr.astype(BF16).astype(F32)).astype(BF16), br)


def _row_copy(src, s, dst, d, sem):
    return pltpu.make_async_copy(src.at[pl.ds(s, 1)], dst.at[pl.ds(d, 1)], sem)


def _to_row_tiles(x):
    return [x[:, j * LANES:(j + 1) * LANES] for j in range(x.shape[-1] // LANES)]


def _store_row_tiles(ref, x):
    for j, col in enumerate(_to_row_tiles(x)):
        ref[:, j, :] = col


def _load_row_tiles(ref):
    return jnp.concatenate([ref[:, j, :] for j in range(ref.shape[1])], axis=-1)


def _dispatch_kernel(dest_ref, cend_ref, h_ref, g_ref, xs_ref, xn_s, sem, *, tt, n_tok, n_chunks):
    base = pl.program_id(0) * tt

    @pl.when(pl.program_id(0) == 0)
    def _():
        xn_s[...] = jnp.zeros_like(xn_s)

        def tail(e):
            nonempty = cend_ref[e] > (cend_ref[e - 1] if e > 0 else 0)
            row = pl.multiple_of((cend_ref[e] - 1) * tt, tt)
            return nonempty, pltpu.make_async_copy(xn_s, xs_ref.at[pl.ds(row, tt)], sem)

        def unused(c):
            return pltpu.make_async_copy(xn_s, xs_ref.at[pl.ds(pl.multiple_of(c * tt, tt), tt)], sem)

        n_used = cend_ref[N_EXPERTS - 1]
        for e in range(N_EXPERTS):
            nonempty, cp = tail(e)
            pl.when(nonempty)(cp.start)
        lax.fori_loop(n_used, n_chunks, lambda c, z: (unused(c).start(), z)[1], 0)
        for e in range(N_EXPERTS):
            nonempty, cp = tail(e)
            pl.when(nonempty)(cp.wait)
        lax.fori_loop(n_used, n_chunks, lambda c, z: (unused(c).wait(), z)[1], 0)

    _store_row_tiles(xn_s, _rms(h_ref[...], g_ref[...]))

    for r in range(tt):
        for k in range(TOP_K_IN_GROUP):
            _row_copy(xn_s, r, xs_ref, dest_ref[k * n_tok + base + r], sem).start(priority=k % 2)
    for k in range(TOP_K_IN_GROUP):
        pltpu.make_async_copy(xn_s, xs_ref.at[pl.ds(0, tt)], sem).wait()


def _dispatch(dest, chunk_end, h2, g, n_rows):
    T, D = h2.shape
    tt = MOE_CHUNK
    assert T % tt == 0
    return pl.pallas_call(
        functools.partial(_dispatch_kernel, tt=tt, n_tok=T, n_chunks=n_rows // tt),
        out_shape=jax.ShapeDtypeStruct((n_rows, D // LANES, LANES), F32),
        grid_spec=pltpu.PrefetchScalarGridSpec(
            num_scalar_prefetch=2,
            grid=(T // tt,),
            in_specs=[
                pl.BlockSpec((tt, D), lambda i, d, ce: (i, 0)),
                pl.BlockSpec((1, D), lambda i, d, ce: (0, 0)),
            ],
            out_specs=pl.BlockSpec(memory_space=pl.ANY),
            scratch_shapes=[pltpu.VMEM((tt, D // LANES, LANES), F32), pltpu.SemaphoreType.DMA],
        ),
        compiler_params=_cparams(("arbitrary",), 40),
        name="moe_dispatch",
    )(dest, chunk_end, h2, g.reshape(1, D))


def _expert_kernel(ce_ref, nv_ref, xs_ref, w1_ref, w3_ref, w2_ref, o_ref, w1b, w3b, w2b):
    c = pl.program_id(0)
    e = ce_ref[c]
    prev = ce_ref[jnp.maximum(c - 1, 0)]

    @pl.when(jnp.logical_or(c == 0, e != prev))
    def _():
        w1b[...] = w1_ref[0, 0].astype(BF16)
        w3b[...] = w3_ref[0, 0].astype(BF16)
        w2b[...] = w2_ref[0, 0].astype(BF16)

    @pl.when(c < nv_ref[0])
    def _():
        x = _load_row_tiles(xs_ref).astype(BF16)
        a = _dot(x, w1b[...])
        b = _dot(x, w3b[...])
        hc = a * jax.nn.sigmoid(a) * b
        _store_row_tiles(o_ref, _dot(hc.astype(BF16), w2b[...]))

    @pl.when(c >= nv_ref[0])
    def _():
        o_ref[...] = jnp.zeros_like(o_ref)


def _experts(chunk_e, n_valid, xs, w1, w3, w2, layer):
    P, nt, _ = xs.shape
    D = nt * LANES
    F = w1.shape[-1]
    ch = MOE_CHUNK
    rows = lambda c, ce, nv: (jnp.minimum(c, nv[0] - 1), 0, 0)
    wsel = lambda c, ce, nv: (layer, ce[c], 0, 0)
    return pl.pallas_call(
        _expert_kernel,
        out_shape=jax.ShapeDtypeStruct((P, nt, LANES), F32),
        grid_spec=pltpu.PrefetchScalarGridSpec(
            num_scalar_prefetch=2,
            grid=(P // ch,),
            in_specs=[
                pl.BlockSpec((ch, nt, LANES), rows),
                pl.BlockSpec((1, 1, D, F), wsel),
                pl.BlockSpec((1, 1, D, F), wsel),
                pl.BlockSpec((1, 1, F, D), wsel),
            ],
            out_specs=pl.BlockSpec((ch, nt, LANES), lambda c, ce, nv: (c, 0, 0)),
            scratch_shapes=[pltpu.VMEM((D, F), BF16), pltpu.VMEM((D, F), BF16), pltpu.VMEM((F, D), BF16)],
        ),
        compiler_params=_cparams(("arbitrary",), 56),
        name="moe_experts",
    )(chunk_e, n_valid, xs, w1, w3, w2)


def _combine_kernel(dest_ref, h_ref, info_ref, rows_ref, p_ref, g_ref, gw_ref, gb_ref, pw_ref, fn_ref,
                    o_ref, buf_a, buf_b, sem, *, tt, n_tok, n_steps, final):
    i = pl.program_id(0)
    bufs = (buf_a, buf_b)

    def row(tile, sl, k, r):
        return _row_copy(rows_ref, dest_ref[k * n_tok + tile * tt + r], bufs[sl].at[k], r, sem.at[sl])

    def wait_rows(sl):
        for k in range(TOP_K_IN_GROUP):
            pltpu.make_async_copy(rows_ref.at[pl.ds(0, tt)], bufs[sl].at[k], sem.at[sl]).wait()

    @pl.when(i == 0)
    def _():
        def issue(r, c):
            for k in range(TOP_K_IN_GROUP):
                row(0, 0, k, r).start(priority=k % 2)
            return c

        lax.fori_loop(0, tt, issue, 0, unroll=8)

    def step(sl):
        wait_rows(sl)
        nxt = jnp.minimum(i + 1, n_steps - 1)
        for r in range(tt):
            for k in range(TOP_K_IN_GROUP):
                row(nxt, 1 - sl, k, r).start(priority=k % 2)
        info = info_ref[...]
        y = (h_ref[...] + info[:, 2:3] * _load_row_tiles(bufs[sl].at[0])
             + info[:, 3:4] * _load_row_tiles(bufs[sl].at[1]))
        hn = _rms(y, g_ref[...])
        gate = jax.nn.sigmoid(_dot(hn.astype(BF16), gw_ref[...]) + gb_ref[...])
        out = y + _dot(p_ref[0].astype(BF16), pw_ref[...]) * gate
        if final:
            out = _rms(out, fn_ref[...])
        o_ref[...] = out
        pl.when(i == n_steps - 1)(functools.partial(wait_rows, 1 - sl))

    for parity in range(2):
        pl.when(i % 2 == parity)(functools.partial(step, parity))


def _combine(dest, h2, info, rows, p3, layer, g, gw, gb, pw, fn, final):
    T, D = h2.shape
    PD = p3.shape[-1]
    tt = min(CMB_TILE, T)
    full = lambda i, d: (0, 0)
    return pl.pallas_call(
        functools.partial(_combine_kernel, tt=tt, n_tok=T, n_steps=T // tt, final=final),
        out_shape=jax.ShapeDtypeStruct((T, D), F32),
        grid_spec=pltpu.PrefetchScalarGridSpec(
            num_scalar_prefetch=1,
            grid=(T // tt,),
            in_specs=[
                pl.BlockSpec((tt, D), lambda i, d: (i, 0)),
                pl.BlockSpec((tt, LANES), lambda i, d: (i, 0)),
                pl.BlockSpec(memory_space=pl.ANY),
                pl.BlockSpec((1, tt, PD), lambda i, d: (layer, i, 0)),
                pl.BlockSpec((1, D), full),
                pl.BlockSpec((D, D), full),
                pl.BlockSpec((1, D), full),
                pl.BlockSpec((PD, D), full),
                pl.BlockSpec((1, D), full),
            ],
            out_specs=pl.BlockSpec((tt, D), lambda i, d: (i, 0)),
            scratch_shapes=[pltpu.VMEM((TOP_K_IN_GROUP, tt, D // LANES, LANES), F32),
                            pltpu.VMEM((TOP_K_IN_GROUP, tt, D // LANES, LANES), F32),
                            pltpu.SemaphoreType.DMA((2,))],
        ),
        compiler_params=_cparams(("arbitrary",), 40),
        name="moe_combine_ple",
    )(dest, h2, info, rows, p3, g.reshape(1, D), gw.astype(BF16), gb.reshape(1, D), pw.astype(BF16),
      fn.reshape(1, D))


def _moe_ple_layer(h2, p3, layer, norm_ffn, rg_w, rg_b, re_w, re_b, w1, w3, w2, norm_ple, gate_w, gate_b, ple_proj,
                   final_norm, final):
    T, D = h2.shape
    A = T * TOP_K_IN_GROUP
    pad = LANES - N_EXPERT_GROUPS - N_EXPERTS
    wr = jnp.concatenate([rg_w, re_w, jnp.zeros((D, pad), F32)], axis=1)
    br = jnp.concatenate([rg_b, re_b, jnp.zeros((pad,), F32)]).reshape(1, LANES)
    info, cnt = _router(h2, norm_ffn, wr, br)
    counts = cnt[0, N_EXPERT_GROUPS:N_EXPERT_GROUPS + N_EXPERTS].astype(I32)
    n_chunks_e = (counts + MOE_CHUNK - 1) // MOE_CHUNK
    chunk_end = jnp.cumsum(n_chunks_e)
    pstarts = (chunk_end - n_chunks_e) * MOE_CHUNK
    n_chunks = -(-A // MOE_CHUNK) + N_EXPERTS
    n_valid = chunk_end[-1:].astype(I32)
    cidx = jnp.minimum(jnp.arange(n_chunks, dtype=I32), n_valid[0] - 1)
    chunk_e = jnp.sum((chunk_end[None, :] <= cidx[:, None]).astype(I32), axis=1)
    chunk_e = jnp.minimum(chunk_e, N_EXPERTS - 1)
    e_idx = info[:, 0:TOP_K_IN_GROUP].astype(I32)
    rank = info[:, 4:4 + TOP_K_IN_GROUP].astype(I32)
    start = jnp.sum(jnp.where(e_idx[..., None] == jnp.arange(N_EXPERTS, dtype=I32), pstarts, 0), axis=-1)
    dest = (start + rank).T.reshape(A)
    xs = _dispatch(dest, chunk_end.astype(I32), h2, norm_ffn, n_chunks * MOE_CHUNK)
    rows = _experts(chunk_e, n_valid, xs, w1, w3, w2, layer)
    return _combine(dest, h2, info, rows, p3, layer, norm_ple, gate_w, gate_b, ple_proj, final_norm, final)


def _rope_rows(xt, cos, sin):
    x1 = xt[0:ROPE_HALF]
    x2 = xt[ROPE_HALF:ROPE_DIMS]
    return jnp.concatenate([x1 * cos - x2 * sin, x2 * cos + x1 * sin, xt[ROPE_DIMS:]], axis=0)


def _rope_tables(pos):
    inv = jnp.float32(ROPE_THETA) ** (-jnp.arange(ROPE_HALF, dtype=F32) * 2.0 / ROPE_DIMS)
    ang = pos.astype(F32)[:, None] * inv[None, :]
    return jnp.cos(ang), jnp.sin(ang)


def _lane_rope_tables(pos):
    cos, sin = _rope_tables(pos)
    n = pos.shape[0]
    ones = jnp.ones((n, HEAD_DIM - ROPE_DIMS), F32)
    zeros = jnp.zeros((n, HEAD_DIM - ROPE_DIMS), F32)
    zh = jnp.zeros((n, ROPE_HALF), F32)
    two = lambda a: jnp.concatenate([a, a], axis=1)
    return (two(jnp.concatenate([cos, cos, ones], axis=1)), two(jnp.concatenate([zh, sin, zeros], axis=1)),
            two(jnp.concatenate([-sin, zh, zeros], axis=1)))


def _rope_lanes(x, c, s1, s2):
    return x * c + pltpu.roll(x, ROPE_HALF, axis=1) * s1 + pltpu.roll(x, LANES - ROPE_HALF, axis=1) * s2


def _kv_kernel(h_ref, g_ref, wn_ref, wt_ref, c_ref, s1_ref, s2_ref, cv_ref, ksa_ref, vst_ref, kw_ref, vwt_ref, *, ts):
    i = pl.program_id(1)
    kvw = N_KV_GROUPS * HEAD_DIM
    hn = _rms(h_ref[0], g_ref[...]).astype(BF16)
    nat = _dot(hn, wn_ref[...])
    cv_ref[0] = nat[:, 0:2 * kvw]
    tr = _dot_nt(wt_ref[...], hn)
    c, s1, s2 = c_ref[...], s1_ref[...], s2_ref[...]
    roped = [_rope_lanes(nat[:, 2 * kvw + k * LANES:2 * kvw + (k + 1) * LANES], c, s1, s2)
             for k in range(2 * kvw // LANES)]
    lane = lax.broadcasted_iota(I32, (ts, KEY_AUG), 1)
    pos = i * ts + lax.broadcasted_iota(I32, (ts, KEY_AUG), 0)
    onehot = jnp.where(lane - HEAD_DIM == pos // SEL_BLOCK, 1.0, 0.0).astype(BF16)
    ones_row = jnp.where(lax.broadcasted_iota(I32, (V_ROWS - HEAD_DIM, ts), 0) == 0, 1.0, 0.0).astype(BF16)
    per_tile = LANES // HEAD_DIM
    for g in range(N_KV_GROUPS):
        lo = (g % per_tile) * HEAD_DIM
        ksa_ref[0, g] = onehot
        ksa_ref[0, g, :, 0:HEAD_DIM] = roped[g // per_tile][:, lo:lo + HEAD_DIM].astype(BF16)
        kw_ref[0, g] = roped[N_KV_GROUPS // per_tile + g // per_tile][:, lo:lo + HEAD_DIM].astype(BF16)
        for ref, base in ((vst_ref, 0), (vwt_ref, kvw)):
            ref[0, g, 0:HEAD_DIM, :] = tr[base + g * HEAD_DIM:base + (g + 1) * HEAD_DIM].astype(BF16)
            ref[0, g, HEAD_DIM:, :] = ones_row


def _shared_kv_proj(h, kv_norm, w_kv):
    B, S, D = h.shape
    G, dh = N_KV_GROUPS, HEAD_DIM
    kvw = G * dh
    ts = min(SEQ_TILE, S)
    br = lambda k: w_kv[:, k * kvw:(k + 1) * kvw]
    w_nat = jnp.concatenate([br(0), br(1), br(2), br(4)], axis=1).astype(BF16)
    w_tr = jnp.concatenate([br(3), br(5)], axis=1).T.astype(BF16)
    tabs = _lane_rope_tables(jnp.arange(S))
    tab = pl.BlockSpec((ts, LANES), lambda b, i: (i, 0))
    return pl.pallas_call(
        functools.partial(_kv_kernel, ts=ts),
        out_shape=(
            jax.ShapeDtypeStruct((B, S, 2 * kvw), F32),
            jax.ShapeDtypeStruct((B, G, S, KEY_AUG), BF16),
            jax.ShapeDtypeStruct((B, G, V_ROWS, S), BF16),
            jax.ShapeDtypeStruct((B, G, S, dh), BF16),
            jax.ShapeDtypeStruct((B, G, V_ROWS, S), BF16),
        ),
        grid=(B, S // ts),
        in_specs=[
            pl.BlockSpec((1, ts, D), lambda b, i: (b, i, 0)),
            pl.BlockSpec((1, D), lambda b, i: (0, 0)),
            pl.BlockSpec((D, 4 * kvw), lambda b, i: (0, 0)),
            pl.BlockSpec((2 * kvw, D), lambda b, i: (0, 0)),
            tab, tab, tab,
        ],
        out_specs=(
            pl.BlockSpec((1, ts, 2 * kvw), lambda b, i: (b, i, 0)),
            pl.BlockSpec((1, G, ts, KEY_AUG), lambda b, i: (b, 0, i, 0)),
            pl.BlockSpec((1, G, V_ROWS, ts), lambda b, i: (b, 0, 0, i)),
            pl.BlockSpec((1, G, ts, dh), lambda b, i: (b, 0, i, 0)),
            pl.BlockSpec((1, G, V_ROWS, ts), lambda b, i: (b, 0, 0, i)),
        ),
        compiler_params=_cparams(("parallel", "parallel"), 48),
        name="shared_kv_proj",
    )(h, kv_norm.reshape(1, D), w_nat, w_tr, *tabs)


def _compress_kernel(x_ref, pos_ref, w1_ref, b1_ref, w2_ref, b2_ref, c_ref, s1_ref, s2_ref, o_ref, *, nh, keys):
    x = x_ref[0, 0, 0]
    a = _dot((x + pos_ref[0:1]).astype(BF16), w1_ref[0])
    b = _dot((x + pos_ref[1:2]).astype(BF16), w1_ref[1])
    hid = jax.nn.gelu(a + pltpu.roll(b, nh - 1, axis=0) + b1_ref[...]).astype(BF16)
    if keys:
        out = _rope_lanes(_dot(hid, w2_ref[...]) + b2_ref[...], c_ref[...], s1_ref[...], s2_ref[...])
        o_ref[0, 0] = out[:, 0:HEAD_DIM].astype(BF16)
    else:
        o_ref[0, 0] = (_dot_nt(w2_ref[...], hid) + b2_ref[...]).astype(BF16)


def _compress(halves, which, pos_emb, w1, b1, w2, b2, keys):
    _, B, G, nh, hw = halves.shape
    dh = HEAD_DIM
    hidden = w1.shape[-1]
    pos2 = pos_emb.reshape(2, hw)
    w1s = w1.reshape(2, hw, hidden).astype(BF16)
    tabs = _lane_rope_tables(jnp.arange(nh) * CMP_STRIDE + CMP_BLOCK - 1)
    if keys:
        w2a = jnp.pad(w2, ((0, 0), (0, LANES - dh))).astype(BF16)
        b2a = jnp.pad(b2, (0, LANES - dh)).reshape(1, LANES)
        out_shape, out_block = (B, G, nh, dh), (1, 1, nh, dh)
    else:
        w2a, b2a = w2.T.astype(BF16), b2.reshape(dh, 1)
        out_shape, out_block = (B, G, dh, nh), (1, 1, dh, nh)
    c2 = lambda b, g: (0, 0)
    tab = pl.BlockSpec((nh, LANES), c2)
    return pl.pallas_call(
        functools.partial(_compress_kernel, nh=nh, keys=keys),
        out_shape=jax.ShapeDtypeStruct(out_shape, BF16),
        grid=(B, G),
        in_specs=[
            pl.BlockSpec((1, 1, 1, nh, hw), lambda b, g: (which, b, g, 0, 0)),
            pl.BlockSpec((2, hw), c2),
            pl.BlockSpec((2, hw, hidden), lambda b, g: (0, 0, 0)),
            pl.BlockSpec((1, hidden), c2),
            pl.BlockSpec(w2a.shape, c2),
            pl.BlockSpec(b2a.shape, c2),
            tab, tab, tab,
        ],
        out_specs=pl.BlockSpec(out_block, lambda b, g: (b, g, 0, 0)),
        compiler_params=_cparams(("parallel", "parallel"), 40),
        name="compress_k" if keys else "compress_v",
    )(halves, pos2, w1s, b1.reshape(1, hidden), w2a, b2a, *tabs)


def _qproj_kernel(h_ref, g_ref, wqt_ref, wg_ref, bg_ref, cos_ref, sin_ref, qt_ref, gt_ref):
    xn = _rms(h_ref[0], g_ref[...]).astype(BF16)
    tr = _dot_nt(wqt_ref[...], xn)
    cos, sin = cos_ref[...], sin_ref[...]
    for hd in range(N_HEADS):
        rows = slice(hd * HEAD_DIM, (hd + 1) * HEAD_DIM)
        qt_ref[0, rows, :] = (_rope_rows(tr[rows], cos, sin) * Q_SCALE).astype(BF16)
    gt_ref[0] = jax.nn.sigmoid(_dot(xn, wg_ref[...]) + bg_ref[...])


def _q_proj(h, g, w_qg, b_gate):
    B, S, D = h.shape
    HD = N_HEADS * HEAD_DIM
    ng = N_HEADS * N_BRANCH
    ts = min(SEQ_TILE, S)
    wqt = w_qg[:, :HD].T.astype(BF16)
    wg = jnp.pad(w_qg[:, HD:], ((0, 0), (0, LANES - ng))).astype(BF16)
    bg = jnp.pad(b_gate, (0, LANES - ng)).reshape(1, LANES)
    cos, sin = _rope_tables(jnp.arange(S))
    tab = pl.BlockSpec((ROPE_HALF, ts), lambda b, i: (0, i))
    return pl.pallas_call(
        _qproj_kernel,
        out_shape=(jax.ShapeDtypeStruct((B, HD, S), BF16), jax.ShapeDtypeStruct((B, S, LANES), F32)),
        grid=(B, S // ts),
        in_specs=[
            pl.BlockSpec((1, ts, D), lambda b, i: (b, i, 0)),
            pl.BlockSpec((1, D), lambda b, i: (0, 0)),
            pl.BlockSpec((HD, D), lambda b, i: (0, 0)),
            pl.BlockSpec((D, LANES), lambda b, i: (0, 0)),
            pl.BlockSpec((1, LANES), lambda b, i: (0, 0)),
            tab, tab,
        ],
        out_specs=(pl.BlockSpec((1, HD, ts), lambda b, i: (b, 0, i)),
                   pl.BlockSpec((1, ts, LANES), lambda b, i: (b, i, 0))),
        compiler_params=_cparams(("parallel", "parallel"), 40),
        name="nsa_q_proj",
    )(h, g.reshape(1, D), wqt, wg, bg, cos.T, sin.T)


def _heads_on_lanes(qt):
    return jnp.concatenate([qt[r * HEAD_DIM:(r + 1) * HEAD_DIM] for r in range(HEADS_PER_GROUP)], axis=1)


def _lane_query_pos(i, tq):
    lanes = lax.broadcasted_iota(I32, (1, HEADS_PER_GROUP * tq), 1)
    return i * tq + (lanes & (tq - 1))


def _gated(acc_t, gates_t, g, branch, denom_row):
    row = lax.broadcasted_iota(I32, gates_t.shape, 0)
    scale = jnp.concatenate(
        [jnp.sum(jnp.where(row == (g * HEADS_PER_GROUP + r) * N_BRANCH + branch, gates_t, 0.0), axis=0, keepdims=True)
         for r in range(HEADS_PER_GROUP)], axis=1)
    if denom_row is not None:
        scale = scale / acc_t[denom_row:denom_row + 1]
    return acc_t[0:HEAD_DIM] * scale


def _unstack_heads(o_t, tq):
    per = LANES // HEAD_DIM
    cols = []
    for r in range(0, HEADS_PER_GROUP, per):
        cols.append(jnp.concatenate([o_t[:, (r + k) * tq:(r + k + 1) * tq] for k in range(per)], axis=0).T)
    return jnp.concatenate(cols, axis=-1)


def _cmp_kernel(qt_ref, gt_ref, k_ref, vt_ref, c2s_ref, o_ref, bias_ref, *, tq, nc):
    g = pl.program_id(1)
    i = pl.program_id(2)

    def body(nk, nb):
        s = _dot(k_ref[0, 0, 0:nk, :], _heads_on_lanes(qt_ref[0]))
        t = _lane_query_pos(i, tq)
        n = lax.broadcasted_iota(I32, (nk, 1), 0)
        s = jnp.where(n * CMP_STRIDE + (CMP_BLOCK - 1) <= t, s, -jnp.inf)
        m = jnp.max(s, axis=0, keepdims=True)
        m = jnp.where(m == -jnp.inf, 0.0, m)
        e = jnp.exp2(s - m)
        p = e * (1.0 / jnp.maximum(jnp.sum(e, axis=0, keepdims=True), 1e-30))
        ot = _dot(vt_ref[0, 0, :, 0:nk], p.astype(BF16))
        o_ref[0] = _unstack_heads(_gated(ot, gt_ref[0].T, g, 0, None), tq).astype(BF16)
        ps = p[:, 0:tq]
        for r in range(1, HEADS_PER_GROUP):
            ps = ps + p[:, r * tq:(r + 1) * tq]
        hi = ps.astype(BF16)
        rem = ps - hi.astype(F32)
        mid = rem.astype(BF16)
        lo = (rem - mid.astype(F32)).astype(BF16)
        c2s = c2s_ref[0:nb, 0:nk]
        imp = _dot(c2s, hi) + _dot(c2s, mid) + _dot(c2s, lo)
        cur = (i * tq + lax.broadcasted_iota(I32, (1, tq), 1)) // SEL_BLOCK
        j = lax.broadcasted_iota(I32, (nb, tq), 0)
        valid = j <= cur
        forced = jnp.where(j == 0, 1.0, jnp.where(j == cur, 1.0, jnp.where(j == cur - 1, 1.0, 0.0)))
        sel = jnp.where(valid, forced, 0.0)
        work = jnp.where(valid, jnp.where(forced > 0.0, -jnp.inf, imp), -jnp.inf)
        for _ in range(N_SELECT - N_FORCED):
            mx = jnp.max(work, axis=0, keepdims=True)
            idx = jnp.min(jnp.where(work == mx, j, MAX_SEL_BLOCKS), axis=0, keepdims=True)
            pick = j == idx
            sel = jnp.where(pick, 1.0, sel)
            work = jnp.where(pick, -jnp.inf, work)
        bias_ref[0, 0, 0:nb, :] = jnp.where(valid, jnp.where(sel > 0.0, 0.0, MASK_BIAS), MASK_BIAS).astype(BF16)
        if nb < MAX_SEL_BLOCKS:
            bias_ref[0, 0, nb:, :] = jnp.full((MAX_SEL_BLOCKS - nb, tq), MASK_BIAS, BF16)

    per_class = LANES * CMP_STRIDE // tq
    n_class = -(-(nc * CMP_STRIDE // tq) // per_class)
    for cls in range(n_class):
        nk = min(nc, LANES * (cls + 1))
        nb = min(MAX_SEL_BLOCKS, nk * CMP_STRIDE // SEL_BLOCK)
        pl.when(i // per_class == cls)(functools.partial(body, nk, nb))


def _cmp_attn(qt, gates, kc, vct):
    B, HD, S = qt.shape
    G, dh = N_KV_GROUPS, HEAD_DIM
    gw = HD // G
    nc = kc.shape[2]
    tq = min(Q_TILE, S)
    n0 = jnp.arange(nc)[None, :] * CMP_STRIDE
    j0 = jnp.arange(MAX_SEL_BLOCKS)[:, None] * SEL_BLOCK
    c2s = ((n0 < j0 + SEL_BLOCK) & (n0 + CMP_BLOCK > j0)).astype(BF16)
    return pl.pallas_call(
        functools.partial(_cmp_kernel, tq=tq, nc=nc),
        out_shape=(jax.ShapeDtypeStruct((B, S, HD), BF16), jax.ShapeDtypeStruct((B, G, MAX_SEL_BLOCKS, S), BF16)),
        grid=(B, G, S // tq),
        in_specs=[
            pl.BlockSpec((1, gw, tq), lambda b, g, i: (b, g, i)),
            pl.BlockSpec((1, tq, LANES), lambda b, g, i: (b, i, 0)),
            pl.BlockSpec((1, 1, nc, dh), lambda b, g, i: (b, g, 0, 0)),
            pl.BlockSpec((1, 1, dh, nc), lambda b, g, i: (b, g, 0, 0)),
            pl.BlockSpec((MAX_SEL_BLOCKS, nc), lambda b, g, i: (0, 0)),
        ],
        out_specs=(pl.BlockSpec((1, tq, gw), lambda b, g, i: (b, i, g)),
                   pl.BlockSpec((1, 1, MAX_SEL_BLOCKS, tq), lambda b, g, i: (b, g, 0, i))),
        compiler_params=_cparams(("parallel", "parallel", "parallel"), 40),
        name="nsa_compressed",
    )(qt, gates, kc, vct, c2s)


def _sel_win_kernel(qt_ref, bias_ref, gt_ref, k_ref, vt_ref, kw_ref, vwt_ref, o_ref, qa, m_s, acc, acc_w, s_a, s_b,
                    *, tq, tk):
    g = pl.program_id(1)
    i = pl.program_id(2)
    qt = qt_ref[0]
    bias = bias_ref[0, 0]
    for r in range(HEADS_PER_GROUP):
        cols = slice(r * tq, (r + 1) * tq)
        qa[0:HEAD_DIM, cols] = qt[r * HEAD_DIM:(r + 1) * HEAD_DIM]
        qa[HEAD_DIM:HEAD_DIM + MAX_SEL_BLOCKS, cols] = bias
        qa[HEAD_DIM + MAX_SEL_BLOCKS:, cols] = jnp.zeros((KEY_AUG - HEAD_DIM - MAX_SEL_BLOCKS, tq), BF16)
    m_s[...] = jnp.full(m_s.shape, NEG_BIG, F32)
    acc[...] = jnp.zeros(acc.shape, F32)
    width = HEADS_PER_GROUP * tq

    def qk(jt, buf):
        buf[...] = _dot(k_ref[0, 0, pl.ds(pl.multiple_of(jt * tk, tk), tk), :], qa[...])

    def online_softmax(key0, n_keys, s):
        m_old = m_s[...]
        m_new = jnp.maximum(m_old, jnp.max(s, axis=0, keepdims=True))
        p = jnp.exp2(s - m_new).astype(BF16)
        acc[...] = jnp.exp2(m_old - m_new) * acc[...] + _dot(vt_ref[0, 0, :, pl.ds(key0, n_keys)], p)
        m_s[...] = m_new

    def absorb(jt, buf):
        online_softmax(pl.multiple_of(jt * tk, tk), tk, buf[...])

    key_row = lax.broadcasted_iota(I32, (tq, 1), 0)
    query_col = lax.broadcasted_iota(I32, (1, width), 1) & (tq - 1)

    def own_tile(s):
        return jnp.where(key_row <= query_col, s, NEG_BIG)

    def absorb_last(jt, buf):
        k0 = pl.multiple_of(jt * tk, tk)

        @pl.when(i % 2 == 1)
        def _():
            online_softmax(k0, tq, buf[0:tq, :])
            online_softmax(pl.multiple_of(k0 + tq, tq), tq, own_tile(buf[tq:tk, :]))

        @pl.when(i % 2 == 0)
        def _():
            online_softmax(k0, tq, own_tile(buf[0:tq, :]))

    def pair(u, c):
        qk(2 * u + 1, s_b)
        absorb(2 * u, s_a)
        qk(2 * u + 2, s_a)
        absorb(2 * u + 1, s_b)
        return c

    def window(miss):
        n_keys = (nw + 1 - miss) * tq
        k0 = pl.multiple_of((i - nw + miss) * tq, tq)
        s_b[0:n_keys, :] = _dot(kw_ref[0, 0, pl.ds(k0, n_keys), :], qa[0:HEAD_DIM, :])
        qk(0, s_a)
        parts = []
        for j in range(miss, nw + 1):
            part = s_b[(j - miss) * tq:(j - miss + 1) * tq, :]
            if j == 0:
                part = jnp.where(key_row > query_col, part, NEG_BIG)
            parts.append(own_tile(part) if j == nw else part)
        s = jnp.concatenate(parts, axis=0)
        m = jnp.max(s, axis=0, keepdims=True)
        p = jnp.exp2(s - m).astype(BF16)
        acc_w[...] = _dot(vwt_ref[0, 0, :, pl.ds(k0, n_keys)], p)

    nw = WINDOW // tq
    for miss in range(nw + 1):
        pl.when(jnp.maximum(nw - i, 0) == miss)(functools.partial(window, miss))

    last = (i * tq + tq + tk - 1) // tk - 1
    lax.fori_loop(0, last // 2, pair, 0)

    @pl.when(last % 2 == 1)
    def _():
        qk(last, s_b)
        absorb(last - 1, s_a)
        absorb_last(last, s_b)

    @pl.when(last % 2 == 0)
    def _():
        absorb_last(last, s_a)

    gates_t = gt_ref[0].T
    o_t = _gated(acc[...], gates_t, g, 1, HEAD_DIM) + _gated(acc_w[...], gates_t, g, 2, HEAD_DIM)
    o_ref[0] = _unstack_heads(o_t, tq).astype(BF16)


def _sel_win_attn(qt, bias, gates, ksa, vst, kw, vwt):
    B, HD, S = qt.shape
    G, dh = N_KV_GROUPS, HEAD_DIM
    gw = HD // G
    tq = min(SEL_Q_TILE, S // 2)
    tk = 2 * tq
    assert S % tk == 0 and WINDOW % tq == 0 and WINDOW + tq <= tk
    width = HEADS_PER_GROUP * tq
    return pl.pallas_call(
        functools.partial(_sel_win_kernel, tq=tq, tk=tk),
        out_shape=jax.ShapeDtypeStruct((B, S, HD), BF16),
        grid=(B, G, S // tq),
        in_specs=[
            pl.BlockSpec((1, gw, tq), lambda b, g, i: (b, g, i)),
            pl.BlockSpec((1, 1, MAX_SEL_BLOCKS, tq), lambda b, g, i: (b, g, 0, i)),
            pl.BlockSpec((1, tq, LANES), lambda b, g, i: (b, i, 0)),
            pl.BlockSpec((1, 1, S, KEY_AUG), lambda b, g, i: (b, g, 0, 0)),
            pl.BlockSpec((1, 1, V_ROWS, S), lambda b, g, i: (b, g, 0, 0)),
            pl.BlockSpec((1, 1, S, dh), lambda b, g, i: (b, g, 0, 0)),
            pl.BlockSpec((1, 1, V_ROWS, S), lambda b, g, i: (b, g, 0, 0)),
        ],
        out_specs=pl.BlockSpec((1, tq, gw), lambda b, g, i: (b, i, g)),
        scratch_shapes=[pltpu.VMEM((KEY_AUG, width), BF16), pltpu.VMEM((1, width), F32),
                        pltpu.VMEM((V_ROWS, width), F32), pltpu.VMEM((V_ROWS, width), F32),
                        pltpu.VMEM((tk, width), F32), pltpu.VMEM((tk, width), F32)],
        compiler_params=_cparams(("parallel", "parallel", "arbitrary"), 60),
        name="nsa_selected_window",
    )(qt, bias, gates, ksa, vst, kw, vwt)


def _oproj_kernel(h_ref, a_ref, b_ref, w_ref, o_ref):
    o = a_ref[...].astype(F32) + b_ref[...].astype(F32)
    o_ref[...] = h_ref[...] + _dot(o.astype(BF16), w_ref[...])


def _out_proj(h2, oc, osw, w_o):
    T, D = h2.shape
    HD = oc.shape[-1]
    tt = min(SEQ_TILE, T)
    blk = lambda w: pl.BlockSpec((tt, w), lambda i: (i, 0))
    return pl.pallas_call(
        _oproj_kernel,
        out_shape=jax.ShapeDtypeStruct((T, D), F32),
        grid=(T // tt,),
        in_specs=[blk(D), blk(HD), blk(HD), pl.BlockSpec((HD, D), lambda i: (0, 0))],
        out_specs=blk(D),
        compiler_params=_cparams(("parallel",), 40),
        name="nsa_out_proj",
    )(h2, oc, osw, w_o.astype(BF16))


def _nsa_layer(h, g, w_qg, b_gate, w_o, shared):
    B, S, D = h.shape
    kc, vct, ksa, vst, kw, vwt = shared
    qt, gates = _q_proj(h, g, w_qg, b_gate)
    oc, bias = _cmp_attn(qt, gates, kc, vct)
    osw = _sel_win_attn(qt, bias, gates, ksa, vst, kw, vwt)
    flat = lambda a: a.reshape(B * S, a.shape[-1])
    return _out_proj(flat(h), flat(oc), flat(osw), w_o).reshape(B, S, D)


def _shared_kv(h, kv_norm, w_kv, ck, cv):
    B, S, _ = h.shape
    G, dh = N_KV_GROUPS, HEAD_DIM
    assert S % SEL_BLOCK == 0 and N_SELECT <= S // SEL_BLOCK <= MAX_SEL_BLOCKS
    cvals, ksa, vst, kw, vwt = _shared_kv_proj(h, kv_norm, w_kv)
    halves = cvals.reshape(B, S, 2, G, dh).transpose(2, 0, 3, 1, 4).reshape(2, B, G, S // CMP_STRIDE, CMP_STRIDE * dh)
    kc = _compress(halves, 0, *ck, keys=True)
    vct = _compress(halves, 1, *cv, keys=False)
    return kc, vct, ksa, vst, kw, vwt


def kernel(x, p, norm_mix, norm_ffn, norm_ple, pool_w, pool_b, pool_scale, kv_norm, w_kv, cmp_k_pos, cmp_k_w1, cmp_k_b1, cmp_k_w2, cmp_k_b2, cmp_v_pos, cmp_v_w1, cmp_v_b1, cmp_v_w2, cmp_v_b2, w_qg, b_gate, w_o, router_g_w, router_g_b, router_e_w, router_e_b, moe_w1, moe_w3, moe_w2, ple_proj, ple_gate_w, ple_gate_b, final_norm):
    B, S, D = x.shape
    depth = p.shape[0]
    n_a = pool_w.shape[0]
    T = B * S
    h = x
    shared = None
    for i in range(depth):
        if i == n_a:
            shared = _shared_kv(h, kv_norm, w_kv,
                                (cmp_k_pos, cmp_k_w1, cmp_k_b1, cmp_k_w2, cmp_k_b2),
                                (cmp_v_pos, cmp_v_w1, cmp_v_b1, cmp_v_w2, cmp_v_b2))
        if i < n_a:
            h = _pool_layer(h, norm_mix[i], pool_w[i], pool_b[i], pool_scale[i])
        else:
            j = i - n_a
            h = _nsa_layer(h, norm_mix[i], w_qg[j], b_gate[j], w_o[j], shared)
        h = _moe_ple_layer(h.reshape(T, D), p.reshape(depth, T, p.shape[-1]), i, norm_ffn[i], router_g_w[i],
                           router_g_b[i], router_e_w[i], router_e_b[i], moe_w1, moe_w3, moe_w2, norm_ple[i],
                           ple_gate_w[i], ple_gate_b[i], ple_proj[i], final_norm, i == depth - 1).reshape(B, S, D)
    return h
```

```python
import functools

import jax
import jax.numpy as jnp
from jax import lax
from jax.experimental import pallas as pl
from jax.experimental.pallas import tpu as pltpu

F32 = jnp.float32
BF16 = jnp.bfloat16
I32 = jnp.int32

POOL_WINDOWS = (2, 4, 8, 16)
N_HEADS = 16
HEAD_DIM = 64
N_KV_GROUPS = 4
HEADS_PER_GROUP = N_HEADS // N_KV_GROUPS
N_BRANCH = 3
ROPE_DIMS = HEAD_DIM // 4
ROPE_HALF = ROPE_DIMS // 2
ROPE_THETA = 500000.0
CMP_BLOCK = 32
CMP_STRIDE = 16
SEL_BLOCK = 64
N_SELECT = 16
WINDOW = 512
FORCE_BONUS = 1e4
N_FORCED = 3
N_EXPERT_GROUPS = 4
EXPERTS_PER_GROUP = 8
N_EXPERTS = N_EXPERT_GROUPS * EXPERTS_PER_GROUP
TOP_K_IN_GROUP = 2
RMS_EPS = 1e-6

LANES = 128
MAX_SEL_BLOCKS = LANES
MASK_BIAS = -30000.0

LOG2E = 1.4426950408889634
Q_SCALE = HEAD_DIM ** -0.5 * LOG2E
V_ROWS = HEAD_DIM + 16
KEY_AUG = 2 * LANES
NEG_BIG = -1e30

SEQ_TILE = 1024
TOK_TILE = 512
CMB_TILE = 256
MOE_CHUNK = 512
Q_TILE = 512
SEL_Q_TILE = 512
HALO = 16


def _cparams(sem, vmem_mb):
    return pltpu.CompilerParams(dimension_semantics=sem, vmem_limit_bytes=vmem_mb * 1024 * 1024)


def _rms(x, g):
    return x * lax.rsqrt(jnp.mean(x * x, axis=-1, keepdims=True) + RMS_EPS) * g


def _dot(a, b):
    return jnp.dot(a, b, preferred_element_type=F32)


def _dot_nt(a, b):
    return lax.dot_general(a, b, (((1,), (1,)), ((), ())), preferred_element_type=F32)


def _pool_kernel(h_ref, halo_ref, g_ref, w_ref, b_ref, sc_ref, o_ref, *, ts, cg):
    i = pl.program_id(1)
    x = h_ref[0]
    g = g_ref[...]
    xn = _rms(x, g)
    hn = _rms(halo_ref[0], g)
    hn = jnp.where(i > 0, hn, 0.0)
    ext = jnp.concatenate([hn, xn], axis=0)
    t = i * ts + lax.broadcasted_iota(I32, (ts, 1), 0)
    outs = []
    for gi, w in enumerate(POOL_WINDOWS):
        s = ext[:, gi * cg:(gi + 1) * cg]
        k = 1
        while k < w:
            s = s + pltpu.roll(s, k, axis=0)
            k *= 2
        cnt = jnp.minimum(t + 1, w).astype(F32)
        pooled = s[HALO:] / cnt - xn[:, gi * cg:(gi + 1) * cg]
        outs.append(_dot(pooled.astype(BF16), w_ref[gi]))
    y = jnp.concatenate(outs, axis=-1)
    o_ref[0] = x + (y + b_ref[...]) * sc_ref[...]


def _pool_layer(h, g, w, b, sc):
    B, S, D = h.shape
    ts = min(SEQ_TILE, S)
    cg = D // len(POOL_WINDOWS)
    row = lambda v: v.reshape(1, D)
    return pl.pallas_call(
        functools.partial(_pool_kernel, ts=ts, cg=cg),
        out_shape=jax.ShapeDtypeStruct((B, S, D), F32),
        grid=(B, S // ts),
        in_specs=[
            pl.BlockSpec((1, ts, D), lambda b_, i: (b_, i, 0)),
            pl.BlockSpec((1, HALO, D), lambda b_, i: (b_, jnp.maximum(i * (ts // HALO) - 1, 0), 0)),
            pl.BlockSpec((1, D), lambda b_, i: (0, 0)),
            pl.BlockSpec((len(POOL_WINDOWS), cg, cg), lambda b_, i: (0, 0, 0)),
            pl.BlockSpec((1, D), lambda b_, i: (0, 0)),
            pl.BlockSpec((1, D), lambda b_, i: (0, 0)),
        ],
        out_specs=pl.BlockSpec((1, ts, D), lambda b_, i: (b_, i, 0)),
        compiler_params=_cparams(("parallel", "parallel"), 40),
        name="pool_mixer",
    )(h, h, row(g), w.astype(BF16), row(b), row(sc))


def _router_kernel(h_ref, g_ref, wh_ref, wl_ref, b_ref, info_ref, cnt_ref, *, tt):
    i = pl.program_id(0)

    @pl.when(i == 0)
    def _():
        cnt_ref[...] = jnp.zeros_like(cnt_ref)

    xn = _rms(h_ref[...], g_ref[...])
    xh = xn.astype(BF16)
    xl = (xn - xh.astype(F32)).astype(BF16)
    logits = _dot(xh, wh_ref[...]) + (_dot(xh, wl_ref[...]) + _dot(xl, wh_ref[...])) + b_ref[...]
    lane = lax.broadcasted_iota(I32, (tt, LANES), 1)
    neg = -jnp.inf
    gl = jnp.where(lane < N_EXPERT_GROUPS, logits, neg)
    gmax = jnp.max(gl, axis=-1, keepdims=True)
    grp = jnp.min(jnp.where(gl == gmax, lane, LANES), axis=-1, keepdims=True)
    gprob = 1.0 / jnp.sum(jnp.exp(gl - gmax), axis=-1, keepdims=True)
    lo = N_EXPERT_GROUPS + grp * EXPERTS_PER_GROUP
    el = jnp.where(lane >= lo, jnp.where(lane < lo + EXPERTS_PER_GROUP, logits, neg), neg)
    v1 = jnp.max(el, axis=-1, keepdims=True)
    i1 = jnp.min(jnp.where(el == v1, lane, LANES), axis=-1, keepdims=True)
    el2 = jnp.where(lane == i1, neg, el)
    v2 = jnp.max(el2, axis=-1, keepdims=True)
    i2 = jnp.min(jnp.where(el2 == v2, lane, LANES), axis=-1, keepdims=True)
    e2 = jnp.exp(v2 - v1)
    w1 = gprob / (1.0 + e2)
    w2 = gprob * e2 / (1.0 + e2)
    oh1 = lane == i1
    oh2 = lane == i2
    oh = jnp.where(oh1, 1.0, jnp.where(oh2, 1.0, 0.0))
    r_ = lax.broadcasted_iota(I32, (tt, tt), 0)
    c_ = lax.broadcasted_iota(I32, (tt, tt), 1)
    tri = jnp.where(r_ > c_, 1.0, 0.0).astype(BF16)
    tot = _dot(tri, oh.astype(BF16)) + cnt_ref[...]
    r1 = jnp.sum(jnp.where(oh1, tot, 0.0), axis=-1, keepdims=True)
    r2 = jnp.sum(jnp.where(oh2, tot, 0.0), axis=-1, keepdims=True)
    cnt_ref[...] = cnt_ref[...] + jnp.sum(oh, axis=0, keepdims=True)
    vals = (i1.astype(F32) - N_EXPERT_GROUPS, i2.astype(F32) - N_EXPERT_GROUPS, w1, w2, r1, r2)
    info = jnp.zeros((tt, LANES), F32)
    for k, v in enumerate(vals):
        info = jnp.where(lane == k, v, info)
    info_ref[...] = info


def _router(h2, g, wr, br):
    T, D = h2.shape
    tt = min(TOK_TILE, T)
    return pl.pallas_call(
        functools.partial(_router_kernel, tt=tt),
        out_shape=(jax.ShapeDtypeStruct((T, LANES), F32), jax.ShapeDtypeStruct((1, LANES), F32)),
        grid=(T // tt,),
        in_specs=[
            pl.BlockSpec((tt, D), lambda i: (i, 0)),
            pl.BlockSpec((1, D), lambda i: (0, 0)),
            pl.BlockSpec((D, LANES), lambda i: (0, 0)),
            pl.BlockSpec((D, LANES), lambda i: (0, 0)),
            pl.BlockSpec((1, LANES), lambda i: (0, 0)),
        ],
        out_specs=(pl.BlockSpec((tt, LANES), lambda i: (i, 0)), pl.BlockSpec((1, LANES), lambda i: (0, 0))),
        compiler_params=_cparams(("arbitrary",), 40),
        name="moe_router",
    )(h2, g.reshape(1, D), wr.astype(BF16), (wr - wr.astype(BF16).astype(F32)).astype(BF16), br)


def _row_copy(src, s, dst, d, sem):
    return pltpu.make_async_copy(src.at[pl.ds(s, 1)], dst.at[pl.ds(d, 1)], sem)


def _dispatch_kernel(dest_ref, cend_ref, h_ref, g_ref, xs_ref, xn_s, sem, *, tt, n_tok, n_chunks):
    base = pl.program_id(0) * tt

    @pl.when(pl.program_id(0) == 0)
    def _():
        xn_s[...] = jnp.zeros_like(xn_s)

        def tail(e):
            nonempty = cend_ref[e] > (cend_ref[e - 1] if e > 0 else 0)
            row = pl.multiple_of((cend_ref[e] - 1) * tt, tt)
            return nonempty, pltpu.make_async_copy(xn_s, xs_ref.at[pl.ds(row, tt)], sem)

        def unused(c):
            return pltpu.make_async_copy(xn_s, xs_ref.at[pl.ds(pl.multiple_of(c * tt, tt), tt)], sem)

        n_used = cend_ref[N_EXPERTS - 1]
        for e in range(N_EXPERTS):
            nonempty, cp = tail(e)
            pl.when(nonempty)(cp.start)
        lax.fori_loop(n_used, n_chunks, lambda c, z: (unused(c).start(), z)[1], 0)
        for e in range(N_EXPERTS):
            nonempty, cp = tail(e)
            pl.when(nonempty)(cp.wait)
        lax.fori_loop(n_used, n_chunks, lambda c, z: (unused(c).wait(), z)[1], 0)

    xn_s[...] = _rms(h_ref[...], g_ref[...])

    for r in range(tt):
        for k in range(TOP_K_IN_GROUP):
            _row_copy(xn_s, r, xs_ref, dest_ref[k * n_tok + base + r], sem).start(priority=k % 2)
    for k in range(TOP_K_IN_GROUP):
        pltpu.make_async_copy(xn_s, xs_ref.at[pl.ds(0, tt)], sem).wait()


def _dispatch(dest, chunk_end, h2, g, n_rows):
    T, D = h2.shape
    tt = MOE_CHUNK
    assert T % tt == 0
    return pl.pallas_call(
        functools.partial(_dispatch_kernel, tt=tt, n_tok=T, n_chunks=n_rows // tt),
        out_shape=jax.ShapeDtypeStruct((n_rows, D), F32),
        grid_spec=pltpu.PrefetchScalarGridSpec(
            num_scalar_prefetch=2,
            grid=(T // tt,),
            in_specs=[
                pl.BlockSpec((tt, D), lambda i, d, ce: (i, 0)),
                pl.BlockSpec((1, D), lambda i, d, ce: (0, 0)),
            ],
            out_specs=pl.BlockSpec(memory_space=pl.ANY),
            scratch_shapes=[pltpu.VMEM((tt, D), F32), pltpu.SemaphoreType.DMA],
        ),
        compiler_params=_cparams(("arbitrary",), 40),
        name="moe_dispatch",
    )(dest, chunk_end, h2, g.reshape(1, D))


def _expert_kernel(ce_ref, nv_ref, xs_ref, w1_ref, w3_ref, w2_ref, o_ref, w1b, w3b, w2b):
    c = pl.program_id(0)
    e = ce_ref[c]
    prev = ce_ref[jnp.maximum(c - 1, 0)]

    @pl.when(jnp.logical_or(c == 0, e != prev))
    def _():
        w1b[...] = w1_ref[0, 0].astype(BF16)
        w3b[...] = w3_ref[0, 0].astype(BF16)
        w2b[...] = w2_ref[0, 0].astype(BF16)

    @pl.when(c < nv_ref[0])
    def _():
        x = xs_ref[...].astype(BF16)
        a = _dot(x, w1b[...])
        b = _dot(x, w3b[...])
        hc = a * jax.nn.sigmoid(a) * b
        o_ref[...] = _dot(hc.astype(BF16), w2b[...])

    @pl.when(c >= nv_ref[0])
    def _():
        o_ref[...] = jnp.zeros_like(o_ref)


def _experts(chunk_e, n_valid, xs, w1, w3, w2, layer):
    P, D = xs.shape
    F = w1.shape[-1]
    ch = MOE_CHUNK
    rows = lambda c, ce, nv: (jnp.minimum(c, nv[0] - 1), 0)
    wsel = lambda c, ce, nv: (layer, ce[c], 0, 0)
    return pl.pallas_call(
        _expert_kernel,
        out_shape=jax.ShapeDtypeStruct((P, D), F32),
        grid_spec=pltpu.PrefetchScalarGridSpec(
            num_scalar_prefetch=2,
            grid=(P // ch,),
            in_specs=[
                pl.BlockSpec((ch, D), rows),
                pl.BlockSpec((1, 1, D, F), wsel),
                pl.BlockSpec((1, 1, D, F), wsel),
                pl.BlockSpec((1, 1, F, D), wsel),
            ],
            out_specs=pl.BlockSpec((ch, D), lambda c, ce, nv: (c, 0)),
            scratch_shapes=[pltpu.VMEM((D, F), BF16), pltpu.VMEM((D, F), BF16), pltpu.VMEM((F, D), BF16)],
        ),
        compiler_params=_cparams(("arbitrary",), 56),
        name="moe_experts",
    )(chunk_e, n_valid, xs, w1, w3, w2)


def _combine_kernel(dest_ref, h_ref, info_ref, rows_ref, p_ref, g_ref, gw_ref, gb_ref, pw_ref, fn_ref,
                    o_ref, buf_a, buf_b, sem, *, tt, n_tok, n_steps, final):
    i = pl.program_id(0)
    bufs = (buf_a, buf_b)

    def row(tile, sl, k, r):
        return _row_copy(rows_ref, dest_ref[k * n_tok + tile * tt + r], bufs[sl].at[k], r, sem.at[sl])

    def wait_rows(sl):
        for k in range(TOP_K_IN_GROUP):
            pltpu.make_async_copy(rows_ref.at[pl.ds(0, tt)], bufs[sl].at[k], sem.at[sl]).wait()

    @pl.when(i == 0)
    def _():
        def issue(r, c):
            for k in range(TOP_K_IN_GROUP):
                row(0, 0, k, r).start(priority=k % 2)
            return c

        lax.fori_loop(0, tt, issue, 0, unroll=8)

    def step(sl):
        wait_rows(sl)
        nxt = jnp.minimum(i + 1, n_steps - 1)
        for r in range(tt):
            for k in range(TOP_K_IN_GROUP):
                row(nxt, 1 - sl, k, r).start(priority=k % 2)
        info = info_ref[...]
        y = h_ref[...] + info[:, 2:3] * bufs[sl][0] + info[:, 3:4] * bufs[sl][1]
        hn = _rms(y, g_ref[...])
        gate = jax.nn.sigmoid(_dot(hn.astype(BF16), gw_ref[...]) + gb_ref[...])
        out = y + _dot(p_ref[0].astype(BF16), pw_ref[...]) * gate
        if final:
            out = _rms(out, fn_ref[...])
        o_ref[...] = out
        pl.when(i == n_steps - 1)(functools.partial(wait_rows, 1 - sl))

    for parity in range(2):
        pl.when(i % 2 == parity)(functools.partial(step, parity))


def _combine(dest, h2, info, rows, p3, layer, g, gw, gb, pw, fn, final):
    T, D = h2.shape
    PD = p3.shape[-1]
    tt = min(CMB_TILE, T)
    full = lambda i, d: (0, 0)
    return pl.pallas_call(
        functools.partial(_combine_kernel, tt=tt, n_tok=T, n_steps=T // tt, final=final),
        out_shape=jax.ShapeDtypeStruct((T, D), F32),
        grid_spec=pltpu.PrefetchScalarGridSpec(
            num_scalar_prefetch=1,
            grid=(T // tt,),
            in_specs=[
                pl.BlockSpec((tt, D), lambda i, d: (i, 0)),
                pl.BlockSpec((tt, LANES), lambda i, d: (i, 0)),
                pl.BlockSpec(memory_space=pl.ANY),
                pl.BlockSpec((1, tt, PD), lambda i, d: (layer, i, 0)),
                pl.BlockSpec((1, D), full),
                pl.BlockSpec((D, D), full),
                pl.BlockSpec((1, D), full),
                pl.BlockSpec((PD, D), full),
                pl.BlockSpec((1, D), full),
            ],
            out_specs=pl.BlockSpec((tt, D), lambda i, d: (i, 0)),
            scratch_shapes=[pltpu.VMEM((TOP_K_IN_GROUP, tt, D), F32), pltpu.VMEM((TOP_K_IN_GROUP, tt, D), F32),
                            pltpu.SemaphoreType.DMA((2,))],
        ),
        compiler_params=_cparams(("arbitrary",), 40),
        name="moe_combine_ple",
    )(dest, h2, info, rows, p3, g.reshape(1, D), gw.astype(BF16), gb.reshape(1, D), pw.astype(BF16),
      fn.reshape(1, D))


def _moe_ple_layer(h2, p3, layer, norm_ffn, rg_w, rg_b, re_w, re_b, w1, w3, w2, norm_ple, gate_w, gate_b, ple_proj,
                   final_norm, final):
    T, D = h2.shape
    A = T * TOP_K_IN_GROUP
    pad = LANES - N_EXPERT_GROUPS - N_EXPERTS
    wr = jnp.concatenate([rg_w, re_w, jnp.zeros((D, pad), F32)], axis=1)
    br = jnp.concatenate([rg_b, re_b, jnp.zeros((pad,), F32)]).reshape(1, LANES)
    info, cnt = _router(h2, norm_ffn, wr, br)
    counts = cnt[0, N_EXPERT_GROUPS:N_EXPERT_GROUPS + N_EXPERTS].astype(I32)
    n_chunks_e = (counts + MOE_CHUNK - 1) // MOE_CHUNK
    chunk_end = jnp.cumsum(n_chunks_e)
    pstarts = (chunk_end - n_chunks_e) * MOE_CHUNK
    n_chunks = -(-A // MOE_CHUNK) + N_EXPERTS
    n_valid = chunk_end[-1:].astype(I32)
    cidx = jnp.minimum(jnp.arange(n_chunks, dtype=I32), n_valid[0] - 1)
    chunk_e = jnp.sum((chunk_end[None, :] <= cidx[:, None]).astype(I32), axis=1)
    chunk_e = jnp.minimum(chunk_e, N_EXPERTS - 1)
    e_idx = info[:, 0:TOP_K_IN_GROUP].astype(I32)
    rank = info[:, 4:4 + TOP_K_IN_GROUP].astype(I32)
    start = jnp.sum(jnp.where(e_idx[..., None] == jnp.arange(N_EXPERTS, dtype=I32), pstarts, 0), axis=-1)
    dest = (start + rank).T.reshape(A)
    xs = _dispatch(dest, chunk_end.astype(I32), h2, norm_ffn, n_chunks * MOE_CHUNK)
    rows = _experts(chunk_e, n_valid, xs, w1, w3, w2, layer)
    return _combine(dest, h2, info, rows, p3, layer, norm_ple, gate_w, gate_b, ple_proj, final_norm, final)


def _rope_rows(xt, cos, sin):
    x1 = xt[0:ROPE_HALF]
    x2 = xt[ROPE_HALF:ROPE_DIMS]
    return jnp.concatenate([x1 * cos - x2 * sin, x2 * cos + x1 * sin, xt[ROPE_DIMS:]], axis=0)


def _rope_tables(pos):
    inv = jnp.float32(ROPE_THETA) ** (-jnp.arange(ROPE_HALF, dtype=F32) * 2.0 / ROPE_DIMS)
    ang = pos.astype(F32)[:, None] * inv[None, :]
    return jnp.cos(ang), jnp.sin(ang)


def _lane_rope_tables(pos):
    cos, sin = _rope_tables(pos)
    n = pos.shape[0]
    ones = jnp.ones((n, HEAD_DIM - ROPE_DIMS), F32)
    zeros = jnp.zeros((n, HEAD_DIM - ROPE_DIMS), F32)
    zh = jnp.zeros((n, ROPE_HALF), F32)
    two = lambda a: jnp.concatenate([a, a], axis=1)
    return (two(jnp.concatenate([cos, cos, ones], axis=1)), two(jnp.concatenate([zh, sin, zeros], axis=1)),
            two(jnp.concatenate([-sin, zh, zeros], axis=1)))


def _rope_lanes(x, c, s1, s2):
    return x * c + pltpu.roll(x, ROPE_HALF, axis=1) * s1 + pltpu.roll(x, LANES - ROPE_HALF, axis=1) * s2


def _kv_kernel(h_ref, g_ref, wn_ref, wt_ref, c_ref, s1_ref, s2_ref, cv_ref, ksa_ref, vst_ref, kw_ref, vwt_ref, *, ts):
    i = pl.program_id(1)
    kvw = N_KV_GROUPS * HEAD_DIM
    hn = _rms(h_ref[0], g_ref[...]).astype(BF16)
    nat = _dot(hn, wn_ref[...])
    cv_ref[0] = nat[:, 0:2 * kvw]
    tr = _dot_nt(wt_ref[...], hn)
    c, s1, s2 = c_ref[...], s1_ref[...], s2_ref[...]
    roped = [_rope_lanes(nat[:, 2 * kvw + k * LANES:2 * kvw + (k + 1) * LANES], c, s1, s2)
             for k in range(2 * kvw // LANES)]
    lane = lax.broadcasted_iota(I32, (ts, KEY_AUG), 1)
    pos = i * ts + lax.broadcasted_iota(I32, (ts, KEY_AUG), 0)
    onehot = jnp.where(lane - HEAD_DIM == pos // SEL_BLOCK, 1.0, 0.0).astype(BF16)
    ones_row = jnp.where(lax.broadcasted_iota(I32, (V_ROWS - HEAD_DIM, ts), 0) == 0, 1.0, 0.0).astype(BF16)
    per_tile = LANES // HEAD_DIM
    for g in range(N_KV_GROUPS):
        lo = (g % per_tile) * HEAD_DIM
        ksa_ref[0, g] = onehot
        ksa_ref[0, g, :, 0:HEAD_DIM] = roped[g // per_tile][:, lo:lo + HEAD_DIM].astype(BF16)
        kw_ref[0, g] = roped[N_KV_GROUPS // per_tile + g // per_tile][:, lo:lo + HEAD_DIM].astype(BF16)
        for ref, base in ((vst_ref, 0), (vwt_ref, kvw)):
            ref[0, g, 0:HEAD_DIM, :] = tr[base + g * HEAD_DIM:base + (g + 1) * HEAD_DIM].astype(BF16)
            ref[0, g, HEAD_DIM:, :] = ones_row


def _shared_kv_proj(h, kv_norm, w_kv):
    B, S, D = h.shape
    G, dh = N_KV_GROUPS, HEAD_DIM
    kvw = G * dh
    ts = min(SEQ_TILE, S)
    br = lambda k: w_kv[:, k * kvw:(k + 1) * kvw]
    w_nat = jnp.concatenate([br(0), br(1), br(2), br(4)], axis=1).astype(BF16)
    w_tr = jnp.concatenate([br(3), br(5)], axis=1).T.astype(BF16)
    tabs = _lane_rope_tables(jnp.arange(S))
    tab = pl.BlockSpec((ts, LANES), lambda b, i: (i, 0))
    return pl.pallas_call(
        functools.partial(_kv_kernel, ts=ts),
        out_shape=(
            jax.ShapeDtypeStruct((B, S, 2 * kvw), F32),
            jax.ShapeDtypeStruct((B, G, S, KEY_AUG), BF16),
            jax.ShapeDtypeStruct((B, G, V_ROWS, S), BF16),
            jax.ShapeDtypeStruct((B, G, S, dh), BF16),
            jax.ShapeDtypeStruct((B, G, V_ROWS, S), BF16),
        ),
        grid=(B, S // ts),
        in_specs=[
            pl.BlockSpec((1, ts, D), lambda b, i: (b, i, 0)),
            pl.BlockSpec((1, D), lambda b, i: (0, 0)),
            pl.BlockSpec((D, 4 * kvw), lambda b, i: (0, 0)),
            pl.BlockSpec((2 * kvw, D), lambda b, i: (0, 0)),
            tab, tab, tab,
        ],
        out_specs=(
            pl.BlockSpec((1, ts, 2 * kvw), lambda b, i: (b, i, 0)),
            pl.BlockSpec((1, G, ts, KEY_AUG), lambda b, i: (b, 0, i, 0)),
            pl.BlockSpec((1, G, V_ROWS, ts), lambda b, i: (b, 0, 0, i)),
            pl.BlockSpec((1, G, ts, dh), lambda b, i: (b, 0, i, 0)),
            pl.BlockSpec((1, G, V_ROWS, ts), lambda b, i: (b, 0, 0, i)),
        ),
        compiler_params=_cparams(("parallel", "parallel"), 48),
        name="shared_kv_proj",
    )(h, kv_norm.reshape(1, D), w_nat, w_tr, *tabs)


def _compress_kernel(x_ref, pos_ref, w1_ref, b1_ref, w2_ref, b2_ref, c_ref, s1_ref, s2_ref, o_ref, *, nh, keys):
    x = x_ref[0, 0, 0]
    a = _dot((x + pos_ref[0:1]).astype(BF16), w1_ref[0])
    b = _dot((x + pos_ref[1:2]).astype(BF16), w1_ref[1])
    hid = jax.nn.gelu(a + pltpu.roll(b, nh - 1, axis=0) + b1_ref[...]).astype(BF16)
    if keys:
        out = _rope_lanes(_dot(hid, w2_ref[...]) + b2_ref[...], c_ref[...], s1_ref[...], s2_ref[...])
        o_ref[0, 0] = out[:, 0:HEAD_DIM].astype(BF16)
    else:
        o_ref[0, 0] = (_dot_nt(w2_ref[...], hid) + b2_ref[...]).astype(BF16)


def _compress(halves, which, pos_emb, w1, b1, w2, b2, keys):
    _, B, G, nh, hw = halves.shape
    dh = HEAD_DIM
    hidden = w1.shape[-1]
    pos2 = pos_emb.reshape(2, hw)
    w1s = w1.reshape(2, hw, hidden).astype(BF16)
    tabs = _lane_rope_tables(jnp.arange(nh) * CMP_STRIDE + CMP_BLOCK - 1)
    if keys:
        w2a = jnp.pad(w2, ((0, 0), (0, LANES - dh))).astype(BF16)
        b2a = jnp.pad(b2, (0, LANES - dh)).reshape(1, LANES)
        out_shape, out_block = (B, G, nh, dh), (1, 1, nh, dh)
    else:
        w2a, b2a = w2.T.astype(BF16), b2.reshape(dh, 1)
        out_shape, out_block = (B, G, dh, nh), (1, 1, dh, nh)
    c2 = lambda b, g: (0, 0)
    tab = pl.BlockSpec((nh, LANES), c2)
    return pl.pallas_call(
        functools.partial(_compress_kernel, nh=nh, keys=keys),
        out_shape=jax.ShapeDtypeStruct(out_shape, BF16),
        grid=(B, G),
        in_specs=[
            pl.BlockSpec((1, 1, 1, nh, hw), lambda b, g: (which, b, g, 0, 0)),
            pl.BlockSpec((2, hw), c2),
            pl.BlockSpec((2, hw, hidden), lambda b, g: (0, 0, 0)),
            pl.BlockSpec((1, hidden), c2),
            pl.BlockSpec(w2a.shape, c2),
            pl.BlockSpec(b2a.shape, c2),
            tab, tab, tab,
        ],
        out_specs=pl.BlockSpec(out_block, lambda b, g: (b, g, 0, 0)),
        compiler_params=_cparams(("parallel", "parallel"), 40),
        name="compress_k" if keys else "compress_v",
    )(halves, pos2, w1s, b1.reshape(1, hidden), w2a, b2a, *tabs)


def _qproj_kernel(h_ref, g_ref, wqt_ref, wg_ref, bg_ref, cos_ref, sin_ref, qt_ref, gt_ref):
    xn = _rms(h_ref[0], g_ref[...]).astype(BF16)
    tr = _dot_nt(wqt_ref[...], xn)
    cos, sin = cos_ref[...], sin_ref[...]
    for hd in range(N_HEADS):
        rows = slice(hd * HEAD_DIM, (hd + 1) * HEAD_DIM)
        qt_ref[0, rows, :] = (_rope_rows(tr[rows], cos, sin) * Q_SCALE).astype(BF16)
    gt_ref[0] = jax.nn.sigmoid(_dot(xn, wg_ref[...]) + bg_ref[...])


def _q_proj(h, g, w_qg, b_gate):
    B, S, D = h.shape
    HD = N_HEADS * HEAD_DIM
    ng = N_HEADS * N_BRANCH
    ts = min(SEQ_TILE, S)
    wqt = w_qg[:, :HD].T.astype(BF16)
    wg = jnp.pad(w_qg[:, HD:], ((0, 0), (0, LANES - ng))).astype(BF16)
    bg = jnp.pad(b_gate, (0, LANES - ng)).reshape(1, LANES)
    cos, sin = _rope_tables(jnp.arange(S))
    tab = pl.BlockSpec((ROPE_HALF, ts), lambda b, i: (0, i))
    return pl.pallas_call(
        _qproj_kernel,
        out_shape=(jax.ShapeDtypeStruct((B, HD, S), BF16), jax.ShapeDtypeStruct((B, S, LANES), F32)),
        grid=(B, S // ts),
        in_specs=[
            pl.BlockSpec((1, ts, D), lambda b, i: (b, i, 0)),
            pl.BlockSpec((1, D), lambda b, i: (0, 0)),
            pl.BlockSpec((HD, D), lambda b, i: (0, 0)),
            pl.BlockSpec((D, LANES), lambda b, i: (0, 0)),
            pl.BlockSpec((1, LANES), lambda b, i: (0, 0)),
            tab, tab,
        ],
        out_specs=(pl.BlockSpec((1, HD, ts), lambda b, i: (b, 0, i)),
                   pl.BlockSpec((1, ts, LANES), lambda b, i: (b, i, 0))),
        compiler_params=_cparams(("parallel", "parallel"), 40),
        name="nsa_q_proj",
    )(h, g.reshape(1, D), wqt, wg, bg, cos.T, sin.T)


def _heads_on_lanes(qt):
    return jnp.concatenate([qt[r * HEAD_DIM:(r + 1) * HEAD_DIM] for r in range(HEADS_PER_GROUP)], axis=1)


def _lane_query_pos(i, tq):
    lanes = lax.broadcasted_iota(I32, (1, HEADS_PER_GROUP * tq), 1)
    return i * tq + (lanes & (tq - 1))


def _gated(acc_t, gates_t, g, branch, denom_row):
    row = lax.broadcasted_iota(I32, gates_t.shape, 0)
    scale = jnp.concatenate(
        [jnp.sum(jnp.where(row == (g * HEADS_PER_GROUP + r) * N_BRANCH + branch, gates_t, 0.0), axis=0, keepdims=True)
         for r in range(HEADS_PER_GROUP)], axis=1)
    if denom_row is not None:
        scale = scale / acc_t[denom_row:denom_row + 1]
    return acc_t[0:HEAD_DIM] * scale


def _unstack_heads(o_t, tq):
    per = LANES // HEAD_DIM
    cols = []
    for r in range(0, HEADS_PER_GROUP, per):
        cols.append(jnp.concatenate([o_t[:, (r + k) * tq:(r + k + 1) * tq] for k in range(per)], axis=0).T)
    return jnp.concatenate(cols, axis=-1)


def _cmp_kernel(qt_ref, gt_ref, k_ref, vt_ref, c2s_ref, o_ref, bias_ref, *, tq, nc):
    g = pl.program_id(1)
    i = pl.program_id(2)

    def body(nk, nb):
        s = _dot(k_ref[0, 0, 0:nk, :], _heads_on_lanes(qt_ref[0]))
        t = _lane_query_pos(i, tq)
        n = lax.broadcasted_iota(I32, (nk, 1), 0)
        s = jnp.where(n * CMP_STRIDE + (CMP_BLOCK - 1) <= t, s, -jnp.inf)
        m = jnp.max(s, axis=0, keepdims=True)
        m = jnp.where(m == -jnp.inf, 0.0, m)
        e = jnp.exp2(s - m)
        p = e * (1.0 / jnp.maximum(jnp.sum(e, axis=0, keepdims=True), 1e-30))
        ot = _dot(vt_ref[0, 0, :, 0:nk], p.astype(BF16))
        o_ref[0] = _unstack_heads(_gated(ot, gt_ref[0].T, g, 0, None), tq).astype(BF16)
        ps = p[:, 0:tq]
        for r in range(1, HEADS_PER_GROUP):
            ps = ps + p[:, r * tq:(r + 1) * tq]
        hi = ps.astype(BF16)
        rem = ps - hi.astype(F32)
        mid = rem.astype(BF16)
        lo = (rem - mid.astype(F32)).astype(BF16)
        c2s = c2s_ref[0:nb, 0:nk]
        imp = _dot(c2s, hi) + _dot(c2s, mid) + _dot(c2s, lo)
        cur = (i * tq + lax.broadcasted_iota(I32, (1, tq), 1)) // SEL_BLOCK
        j = lax.broadcasted_iota(I32, (nb, tq), 0)
        valid = j <= cur
        forced = jnp.where(j == 0, 1.0, jnp.where(j == cur, 1.0, jnp.where(j == cur - 1, 1.0, 0.0)))
        sel = jnp.where(valid, forced, 0.0)
        work = jnp.where(valid, jnp.where(forced > 0.0, -jnp.inf, imp), -jnp.inf)
        for _ in range(N_SELECT - N_FORCED):
            mx = jnp.max(work, axis=0, keepdims=True)
            idx = jnp.min(jnp.where(work == mx, j, MAX_SEL_BLOCKS), axis=0, keepdims=True)
            pick = j == idx
            sel = jnp.where(pick, 1.0, sel)
            work = jnp.where(pick, -jnp.inf, work)
        bias_ref[0, 0, 0:nb, :] = jnp.where(valid, jnp.where(sel > 0.0, 0.0, MASK_BIAS), MASK_BIAS).astype(BF16)
        if nb < MAX_SEL_BLOCKS:
            bias_ref[0, 0, nb:, :] = jnp.full((MAX_SEL_BLOCKS - nb, tq), MASK_BIAS, BF16)

    per_class = LANES * CMP_STRIDE // tq
    n_class = -(-(nc * CMP_STRIDE // tq) // per_class)
    for cls in range(n_class):
        nk = min(nc, LANES * (cls + 1))
        nb = min(MAX_SEL_BLOCKS, nk * CMP_STRIDE // SEL_BLOCK)
        pl.when(i // per_class == cls)(functools.partial(body, nk, nb))


def _cmp_attn(qt, gates, kc, vct):
    B, HD, S = qt.shape
    G, dh = N_KV_GROUPS, HEAD_DIM
    gw = HD // G
    nc = kc.shape[2]
    tq = min(Q_TILE, S)
    n0 = jnp.arange(nc)[None, :] * CMP_STRIDE
    j0 = jnp.arange(MAX_SEL_BLOCKS)[:, None] * SEL_BLOCK
    c2s = ((n0 < j0 + SEL_BLOCK) & (n0 + CMP_BLOCK > j0)).astype(BF16)
    return pl.pallas_call(
        functools.partial(_cmp_kernel, tq=tq, nc=nc),
        out_shape=(jax.ShapeDtypeStruct((B, S, HD), BF16), jax.ShapeDtypeStruct((B, G, MAX_SEL_BLOCKS, S), BF16)),
        grid=(B, G, S // tq),
        in_specs=[
            pl.BlockSpec((1, gw, tq), lambda b, g, i: (b, g, i)),
            pl.BlockSpec((1, tq, LANES), lambda b, g, i: (b, i, 0)),
            pl.BlockSpec((1, 1, nc, dh), lambda b, g, i: (b, g, 0, 0)),
            pl.BlockSpec((1, 1, dh, nc), lambda b, g, i: (b, g, 0, 0)),
            pl.BlockSpec((MAX_SEL_BLOCKS, nc), lambda b, g, i: (0, 0)),
        ],
        out_specs=(pl.BlockSpec((1, tq, gw), lambda b, g, i: (b, i, g)),
                   pl.BlockSpec((1, 1, MAX_SEL_BLOCKS, tq), lambda b, g, i: (b, g, 0, i))),
        compiler_params=_cparams(("parallel", "parallel", "parallel"), 40),
        name="nsa_compressed",
    )(qt, gates, kc, vct, c2s)


def _sel_win_kernel(qt_ref, bias_ref, gt_ref, k_ref, vt_ref, kw_ref, vwt_ref, o_ref, qa, m_s, acc, acc_w, s_a, s_b,
                    *, tq, tk):
    g = pl.program_id(1)
    i = pl.program_id(2)
    qt = qt_ref[0]
    bias = bias_ref[0, 0]
    for r in range(HEADS_PER_GROUP):
        cols = slice(r * tq, (r + 1) * tq)
        qa[0:HEAD_DIM, cols] = qt[r * HEAD_DIM:(r + 1) * HEAD_DIM]
        qa[HEAD_DIM:HEAD_DIM + MAX_SEL_BLOCKS, cols] = bias
        qa[HEAD_DIM + MAX_SEL_BLOCKS:, cols] = jnp.zeros((KEY_AUG - HEAD_DIM - MAX_SEL_BLOCKS, tq), BF16)
    m_s[...] = jnp.full(m_s.shape, NEG_BIG, F32)
    acc[...] = jnp.zeros(acc.shape, F32)
    width = HEADS_PER_GROUP * tq

    def qk(jt, buf):
        buf[...] = _dot(k_ref[0, 0, pl.ds(pl.multiple_of(jt * tk, tk), tk), :], qa[...])

    def online_softmax(key0, n_keys, s):
        m_old = m_s[...]
        m_new = jnp.maximum(m_old, jnp.max(s, axis=0, keepdims=True))
        p = jnp.exp2(s - m_new).astype(BF16)
        acc[...] = jnp.exp2(m_old - m_new) * acc[...] + _dot(vt_ref[0, 0, :, pl.ds(key0, n_keys)], p)
        m_s[...] = m_new

    def absorb(jt, buf):
        online_softmax(pl.multiple_of(jt * tk, tk), tk, buf[...])

    key_row = lax.broadcasted_iota(I32, (tq, 1), 0)
    query_col = lax.broadcasted_iota(I32, (1, width), 1) & (tq - 1)

    def own_tile(s):
        return jnp.where(key_row <= query_col, s, NEG_BIG)

    def absorb_last(jt, buf):
        k0 = pl.multiple_of(jt * tk, tk)

        @pl.when(i % 2 == 1)
        def _():
            online_softmax(k0, tq, buf[0:tq, :])
            online_softmax(pl.multiple_of(k0 + tq, tq), tq, own_tile(buf[tq:tk, :]))

        @pl.when(i % 2 == 0)
        def _():
            online_softmax(k0, tq, own_tile(buf[0:tq, :]))

    def pair(u, c):
        qk(2 * u + 1, s_b)
        absorb(2 * u, s_a)
        qk(2 * u + 2, s_a)
        absorb(2 * u + 1, s_b)
        return c

    def window(miss):
        n_keys = (nw + 1 - miss) * tq
        k0 = pl.multiple_of((i - nw + miss) * tq, tq)
        s_b[0:n_keys, :] = _dot(kw_ref[0, 0, pl.ds(k0, n_keys), :], qa[0:HEAD_DIM, :])
        qk(0, s_a)
        parts = []
        for j in range(miss, nw + 1):
            part = s_b[(j - miss) * tq:(j - miss + 1) * tq, :]
            if j == 0:
                part = jnp.where(key_row > query_col, part, NEG_BIG)
            parts.append(own_tile(part) if j == nw else part)
        s = jnp.concatenate(parts, axis=0)
        m = jnp.max(s, axis=0, keepdims=True)
        p = jnp.exp2(s - m).astype(BF16)
        acc_w[...] = _dot(vwt_ref[0, 0, :, pl.ds(k0, n_keys)], p)

    nw = WINDOW // tq
    for miss in range(nw + 1):
        pl.when(jnp.maximum(nw - i, 0) == miss)(functools.partial(window, miss))

    last = (i * tq + tq + tk - 1) // tk - 1
    lax.fori_loop(0, last // 2, pair, 0)

    @pl.when(last % 2 == 1)
    def _():
        qk(last, s_b)
        absorb(last - 1, s_a)
        absorb_last(last, s_b)

    @pl.when(last % 2 == 0)
    def _():
        absorb_last(last, s_a)

    gates_t = gt_ref[0].T
    o_t = _gated(acc[...], gates_t, g, 1, HEAD_DIM) + _gated(acc_w[...], gates_t, g, 2, HEAD_DIM)
    o_ref[0] = _unstack_heads(o_t, tq).astype(BF16)


def _sel_win_attn(qt, bias, gates, ksa, vst, kw, vwt):
    B, HD, S = qt.shape
    G, dh = N_KV_GROUPS, HEAD_DIM
    gw = HD // G
    tq = min(SEL_Q_TILE, S // 2)
    tk = 2 * tq
    assert S % tk == 0 and WINDOW % tq == 0 and WINDOW + tq <= tk
    width = HEADS_PER_GROUP * tq
    return pl.pallas_call(
        functools.partial(_sel_win_kernel, tq=tq, tk=tk),
        out_shape=jax.ShapeDtypeStruct((B, S, HD), BF16),
        grid=(B, G, S // tq),
        in_specs=[
            pl.BlockSpec((1, gw, tq), lambda b, g, i: (b, g, i)),
            pl.BlockSpec((1, 1, MAX_SEL_BLOCKS, tq), lambda b, g, i: (b, g, 0, i)),
            pl.BlockSpec((1, tq, LANES), lambda b, g, i: (b, i, 0)),
            pl.BlockSpec((1, 1, S, KEY_AUG), lambda b, g, i: (b, g, 0, 0)),
            pl.BlockSpec((1, 1, V_ROWS, S), lambda b, g, i: (b, g, 0, 0)),
            pl.BlockSpec((1, 1, S, dh), lambda b, g, i: (b, g, 0, 0)),
            pl.BlockSpec((1, 1, V_ROWS, S), lambda b, g, i: (b, g, 0, 0)),
        ],
        out_specs=pl.BlockSpec((1, tq, gw), lambda b, g, i: (b, i, g)),
        scratch_shapes=[pltpu.VMEM((KEY_AUG, width), BF16), pltpu.VMEM((1, width), F32),
                        pltpu.VMEM((V_ROWS, width), F32), pltpu.VMEM((V_ROWS, width), F32),
                        pltpu.VMEM((tk, width), F32), pltpu.VMEM((tk, width), F32)],
        compiler_params=_cparams(("parallel", "parallel", "arbitrary"), 60),
        name="nsa_selected_window",
    )(qt, bias, gates, ksa, vst, kw, vwt)


def _oproj_kernel(h_ref, a_ref, b_ref, w_ref, o_ref):
    o = a_ref[...].astype(F32) + b_ref[...].astype(F32)
    o_ref[...] = h_ref[...] + _dot(o.astype(BF16), w_ref[...])


def _out_proj(h2, oc, osw, w_o):
    T, D = h2.shape
    HD = oc.shape[-1]
    tt = min(SEQ_TILE, T)
    blk = lambda w: pl.BlockSpec((tt, w), lambda i: (i, 0))
    return pl.pallas_call(
        _oproj_kernel,
        out_shape=jax.ShapeDtypeStruct((T, D), F32),
        grid=(T // tt,),
        in_specs=[blk(D), blk(HD), blk(HD), pl.BlockSpec((HD, D), lambda i: (0, 0))],
        out_specs=blk(D),
        compiler_params=_cparams(("parallel",), 40),
        name="nsa_out_proj",
    )(h2, oc, osw, w_o.astype(BF16))


def _nsa_layer(h, g, w_qg, b_gate, w_o, shared):
    B, S, D = h.shape
    kc, vct, ksa, vst, kw, vwt = shared
    qt, gates = _q_proj(h, g, w_qg, b_gate)
    oc, bias = _cmp_attn(qt, gates, kc, vct)
    osw = _sel_win_attn(qt, bias, gates, ksa, vst, kw, vwt)
    flat = lambda a: a.reshape(B * S, a.shape[-1])
    return _out_proj(flat(h), flat(oc), flat(osw), w_o).reshape(B, S, D)


def _shared_kv(h, kv_norm, w_kv, ck, cv):
    B, S, _ = h.shape
    G, dh = N_KV_GROUPS, HEAD_DIM
    assert S % SEL_BLOCK == 0 and N_SELECT <= S // SEL_BLOCK <= MAX_SEL_BLOCKS
    cvals, ksa, vst, kw, vwt = _shared_kv_proj(h, kv_norm, w_kv)
    halves = cvals.reshape(B, S, 2, G, dh).transpose(2, 0, 3, 1, 4).reshape(2, B, G, S // CMP_STRIDE, CMP_STRIDE * dh)
    kc = _compress(halves, 0, *ck, keys=True)
    vct = _compress(halves, 1, *cv, keys=False)
    return kc, vct, ksa, vst, kw, vwt


def kernel(x, p, norm_mix, norm_ffn, norm_ple, pool_w, pool_b, pool_scale, kv_norm, w_kv, cmp_k_pos, cmp_k_w1, cmp_k_b1, cmp_k_w2, cmp_k_b2, cmp_v_pos, cmp_v_w1, cmp_v_b1, cmp_v_w2, cmp_v_b2, w_qg, b_gate, w_o, router_g_w, router_g_b, router_e_w, router_e_b, moe_w1, moe_w3, moe_w2, ple_proj, ple_gate_w, ple_gate_b, final_norm):
    B, S, D = x.shape
    depth = p.shape[0]
    n_a = pool_w.shape[0]
    T = B * S
    h = x
    shared = None
    for i in range(depth):
        if i == n_a:
            shared = _shared_kv(h, kv_norm, w_kv,
                                (cmp_k_pos, cmp_k_w1, cmp_k_b1, cmp_k_w2, cmp_k_b2),
                                (cmp_v_pos, cmp_v_w1, cmp_v_b1, cmp_v_w2, cmp_v_b2))
        if i < n_a:
            h = _pool_layer(h, norm_mix[i], pool_w[i], pool_b[i], pool_scale[i])
        else:
            j = i - n_a
            h = _nsa_layer(h, norm_mix[i], w_qg[j], b_gate[j], w_o[j], shared)
        h = _moe_ple_layer(h.reshape(T, D), p.reshape(depth, T, p.shape[-1]), i, norm_ffn[i], router_g_w[i],
                           router_g_b[i], router_e_w[i], router_e_b[i], moe_w1, moe_w3, moe_w2, norm_ple[i],
                           ple_gate_w[i], ple_gate_b[i], ple_proj[i], final_norm, i == depth - 1).reshape(B, S, D)
    return h
```

```python
import functools

import jax
import jax.numpy as jnp
from jax import lax
from jax.experimental import pallas as pl
from jax.experimental.pallas import tpu as pltpu

F32 = jnp.float32
BF16 = jnp.bfloat16
I32 = jnp.int32

POOL_WINDOWS = (2, 4, 8, 16)
N_HEADS = 16
HEAD_DIM = 64
N_KV_GROUPS = 4
HEADS_PER_GROUP = N_HEADS // N_KV_GROUPS
N_BRANCH = 3
ROPE_DIMS = HEAD_DIM // 4
ROPE_HALF = ROPE_DIMS // 2
ROPE_THETA = 500000.0
CMP_BLOCK = 32
CMP_STRIDE = 16
SEL_BLOCK = 64
N_SELECT = 16
WINDOW = 512
FORCE_BONUS = 1e4
N_FORCED = 3
N_EXPERT_GROUPS = 4
EXPERTS_PER_GROUP = 8
N_EXPERTS = N_EXPERT_GROUPS * EXPERTS_PER_GROUP
TOP_K_IN_GROUP = 2
RMS_EPS = 1e-6

LANES = 128
MAX_SEL_BLOCKS = LANES
MASK_BIAS = -30000.0

LOG2E = 1.4426950408889634
Q_SCALE = HEAD_DIM ** -0.5 * LOG2E
V_ROWS = HEAD_DIM + 16
KEY_AUG = 2 * LANES
NEG_BIG = -1e30

SEQ_TILE = 1024
TOK_TILE = 512
CMB_TILE = 512
MOE_CHUNK = 512
Q_TILE = 512
SEL_Q_TILE = 512
HALO = 16


def _cparams(sem, vmem_mb):
    return pltpu.CompilerParams(dimension_semantics=sem, vmem_limit_bytes=vmem_mb * 1024 * 1024)


def _rms(x, g):
    return x * lax.rsqrt(jnp.mean(x * x, axis=-1, keepdims=True) + RMS_EPS) * g


def _dot(a, b):
    return jnp.dot(a, b, preferred_element_type=F32)


def _dot_nt(a, b):
    return lax.dot_general(a, b, (((1,), (1,)), ((), ())), preferred_element_type=F32)


def _pool_kernel(h_ref, halo_ref, g_ref, w_ref, b_ref, sc_ref, o_ref, *, ts, cg):
    i = pl.program_id(1)
    x = h_ref[0]
    g = g_ref[...]
    xn = _rms(x, g)
    hn = _rms(halo_ref[0], g)
    hn = jnp.where(i > 0, hn, 0.0)
    ext = jnp.concatenate([hn, xn], axis=0)
    t = i * ts + lax.broadcasted_iota(I32, (ts, 1), 0)
    outs = []
    for gi, w in enumerate(POOL_WINDOWS):
        s = ext[:, gi * cg:(gi + 1) * cg]
        k = 1
        while k < w:
            s = s + pltpu.roll(s, k, axis=0)
            k *= 2
        cnt = jnp.minimum(t + 1, w).astype(F32)
        pooled = s[HALO:] / cnt - xn[:, gi * cg:(gi + 1) * cg]
        outs.append(_dot(pooled.astype(BF16), w_ref[gi]))
    y = jnp.concatenate(outs, axis=-1)
    o_ref[0] = x + (y + b_ref[...]) * sc_ref[...]


def _pool_layer(h, g, w, b, sc):
    B, S, D = h.shape
    ts = min(SEQ_TILE, S)
    cg = D // len(POOL_WINDOWS)
    row = lambda v: v.reshape(1, D)
    return pl.pallas_call(
        functools.partial(_pool_kernel, ts=ts, cg=cg),
        out_shape=jax.ShapeDtypeStruct((B, S, D), F32),
        grid=(B, S // ts),
        in_specs=[
            pl.BlockSpec((1, ts, D), lambda b_, i: (b_, i, 0)),
            pl.BlockSpec((1, HALO, D), lambda b_, i: (b_, jnp.maximum(i * (ts // HALO) - 1, 0), 0)),
            pl.BlockSpec((1, D), lambda b_, i: (0, 0)),
            pl.BlockSpec((len(POOL_WINDOWS), cg, cg), lambda b_, i: (0, 0, 0)),
            pl.BlockSpec((1, D), lambda b_, i: (0, 0)),
            pl.BlockSpec((1, D), lambda b_, i: (0, 0)),
        ],
        out_specs=pl.BlockSpec((1, ts, D), lambda b_, i: (b_, i, 0)),
        compiler_params=_cparams(("parallel", "parallel"), 40),
        name="pool_mixer",
    )(h, h, row(g), w.astype(BF16), row(b), row(sc))


def _router_kernel(h_ref, g_ref, wh_ref, wl_ref, b_ref, info_ref, cnt_ref, *, tt):
    i = pl.program_id(0)

    @pl.when(i == 0)
    def _():
        cnt_ref[...] = jnp.zeros_like(cnt_ref)

    xn = _rms(h_ref[...], g_ref[...])
    xh = xn.astype(BF16)
    xl = (xn - xh.astype(F32)).astype(BF16)
    logits = _dot(xh, wh_ref[...]) + (_dot(xh, wl_ref[...]) + _dot(xl, wh_ref[...])) + b_ref[...]
    lane = lax.broadcasted_iota(I32, (tt, LANES), 1)
    neg = -jnp.inf
    gl = jnp.where(lane < N_EXPERT_GROUPS, logits, neg)
    gmax = jnp.max(gl, axis=-1, keepdims=True)
    grp = jnp.min(jnp.where(gl == gmax, lane, LANES), axis=-1, keepdims=True)
    gprob = 1.0 / jnp.sum(jnp.exp(gl - gmax), axis=-1, keepdims=True)
    lo = N_EXPERT_GROUPS + grp * EXPERTS_PER_GROUP
    el = jnp.where(lane >= lo, jnp.where(lane < lo + EXPERTS_PER_GROUP, logits, neg), neg)
    v1 = jnp.max(el, axis=-1, keepdims=True)
    i1 = jnp.min(jnp.where(el == v1, lane, LANES), axis=-1, keepdims=True)
    el2 = jnp.where(lane == i1, neg, el)
    v2 = jnp.max(el2, axis=-1, keepdims=True)
    i2 = jnp.min(jnp.where(el2 == v2, lane, LANES), axis=-1, keepdims=True)
    e2 = jnp.exp(v2 - v1)
    w1 = gprob / (1.0 + e2)
    w2 = gprob * e2 / (1.0 + e2)
    oh1 = lane == i1
    oh2 = lane == i2
    oh = jnp.where(oh1, 1.0, jnp.where(oh2, 1.0, 0.0))
    r_ = lax.broadcasted_iota(I32, (tt, tt), 0)
    c_ = lax.broadcasted_iota(I32, (tt, tt), 1)
    tri = jnp.where(r_ > c_, 1.0, 0.0).astype(BF16)
    tot = _dot(tri, oh.astype(BF16)) + cnt_ref[...]
    r1 = jnp.sum(jnp.where(oh1, tot, 0.0), axis=-1, keepdims=True)
    r2 = jnp.sum(jnp.where(oh2, tot, 0.0), axis=-1, keepdims=True)
    cnt_ref[...] = cnt_ref[...] + jnp.sum(oh, axis=0, keepdims=True)
    vals = (i1.astype(F32) - N_EXPERT_GROUPS, i2.astype(F32) - N_EXPERT_GROUPS, w1, w2, r1, r2)
    info = jnp.zeros((tt, LANES), F32)
    for k, v in enumerate(vals):
        info = jnp.where(lane == k, v, info)
    info_ref[...] = info


def _router(h2, g, wr, br):
    T, D = h2.shape
    tt = min(TOK_TILE, T)
    return pl.pallas_call(
        functools.partial(_router_kernel, tt=tt),
        out_shape=(jax.ShapeDtypeStruct((T, LANES), F32), jax.ShapeDtypeStruct((1, LANES), F32)),
        grid=(T // tt,),
        in_specs=[
            pl.BlockSpec((tt, D), lambda i: (i, 0)),
            pl.BlockSpec((1, D), lambda i: (0, 0)),
            pl.BlockSpec((D, LANES), lambda i: (0, 0)),
            pl.BlockSpec((D, LANES), lambda i: (0, 0)),
            pl.BlockSpec((1, LANES), lambda i: (0, 0)),
        ],
        out_specs=(pl.BlockSpec((tt, LANES), lambda i: (i, 0)), pl.BlockSpec((1, LANES), lambda i: (0, 0))),
        compiler_params=_cparams(("arbitrary",), 40),
        name="moe_router",
    )(h2, g.reshape(1, D), wr.astype(BF16), (wr - wr.astype(BF16).astype(F32)).astype(BF16), br)


def _row_copy(src, s, dst, d, sem):
    return pltpu.make_async_copy(src.at[pl.ds(s, 1)], dst.at[pl.ds(d, 1)], sem)


def _dispatch_kernel(dest_ref, cend_ref, h_ref, g_ref, xs_ref, xn_s, sem, *, tt, n_tok, n_chunks):
    base = pl.program_id(0) * tt

    @pl.when(pl.program_id(0) == 0)
    def _():
        xn_s[...] = jnp.zeros_like(xn_s)

        def tail(e):
            nonempty = cend_ref[e] > (cend_ref[e - 1] if e > 0 else 0)
            row = pl.multiple_of((cend_ref[e] - 1) * tt, tt)
            return nonempty, pltpu.make_async_copy(xn_s, xs_ref.at[pl.ds(row, tt)], sem)

        def unused(c):
            return pltpu.make_async_copy(xn_s, xs_ref.at[pl.ds(pl.multiple_of(c * tt, tt), tt)], sem)

        n_used = cend_ref[N_EXPERTS - 1]
        for e in range(N_EXPERTS):
            nonempty, cp = tail(e)
            pl.when(nonempty)(cp.start)
        lax.fori_loop(n_used, n_chunks, lambda c, z: (unused(c).start(), z)[1], 0)
        for e in range(N_EXPERTS):
            nonempty, cp = tail(e)
            pl.when(nonempty)(cp.wait)
        lax.fori_loop(n_used, n_chunks, lambda c, z: (unused(c).wait(), z)[1], 0)

    xn_s[...] = _rms(h_ref[...], g_ref[...])

    for r in range(tt):
        for k in range(TOP_K_IN_GROUP):
            _row_copy(xn_s, r, xs_ref, dest_ref[k * n_tok + base + r], sem).start(priority=k % 2)
    for k in range(TOP_K_IN_GROUP):
        pltpu.make_async_copy(xn_s, xs_ref.at[pl.ds(0, tt)], sem).wait()


def _dispatch(dest, chunk_end, h2, g, n_rows):
    T, D = h2.shape
    tt = MOE_CHUNK
    assert T % tt == 0
    return pl.pallas_call(
        functools.partial(_dispatch_kernel, tt=tt, n_tok=T, n_chunks=n_rows // tt),
        out_shape=jax.ShapeDtypeStruct((n_rows, D), F32),
        grid_spec=pltpu.PrefetchScalarGridSpec(
            num_scalar_prefetch=2,
            grid=(T // tt,),
            in_specs=[
                pl.BlockSpec((tt, D), lambda i, d, ce: (i, 0)),
                pl.BlockSpec((1, D), lambda i, d, ce: (0, 0)),
            ],
            out_specs=pl.BlockSpec(memory_space=pl.ANY),
            scratch_shapes=[pltpu.VMEM((tt, D), F32), pltpu.SemaphoreType.DMA],
        ),
        compiler_params=_cparams(("arbitrary",), 40),
        name="moe_dispatch",
    )(dest, chunk_end, h2, g.reshape(1, D))


def _expert_kernel(ce_ref, nv_ref, xs_ref, w1_ref, w3_ref, w2_ref, o_ref, w1b, w3b, w2b):
    c = pl.program_id(0)
    e = ce_ref[c]
    prev = ce_ref[jnp.maximum(c - 1, 0)]

    @pl.when(jnp.logical_or(c == 0, e != prev))
    def _():
        w1b[...] = w1_ref[0, 0].astype(BF16)
        w3b[...] = w3_ref[0, 0].astype(BF16)
        w2b[...] = w2_ref[0, 0].astype(BF16)

    @pl.when(c < nv_ref[0])
    def _():
        x = xs_ref[...].astype(BF16)
        a = _dot(x, w1b[...])
        b = _dot(x, w3b[...])
        hc = a * jax.nn.sigmoid(a) * b
        o_ref[...] = _dot(hc.astype(BF16), w2b[...])

    @pl.when(c >= nv_ref[0])
    def _():
        o_ref[...] = jnp.zeros_like(o_ref)


def _experts(chunk_e, n_valid, xs, w1, w3, w2, layer):
    P, D = xs.shape
    F = w1.shape[-1]
    ch = MOE_CHUNK
    rows = lambda c, ce, nv: (jnp.minimum(c, nv[0] - 1), 0)
    wsel = lambda c, ce, nv: (layer, ce[c], 0, 0)
    return pl.pallas_call(
        _expert_kernel,
        out_shape=jax.ShapeDtypeStruct((P, D), F32),
        grid_spec=pltpu.PrefetchScalarGridSpec(
            num_scalar_prefetch=2,
            grid=(P // ch,),
            in_specs=[
                pl.BlockSpec((ch, D), rows),
                pl.BlockSpec((1, 1, D, F), wsel),
                pl.BlockSpec((1, 1, D, F), wsel),
                pl.BlockSpec((1, 1, F, D), wsel),
            ],
            out_specs=pl.BlockSpec((ch, D), lambda c, ce, nv: (c, 0)),
            scratch_shapes=[pltpu.VMEM((D, F), BF16), pltpu.VMEM((D, F), BF16), pltpu.VMEM((F, D), BF16)],
        ),
        compiler_params=_cparams(("arbitrary",), 56),
        name="moe_experts",
    )(chunk_e, n_valid, xs, w1, w3, w2)


def _combine_kernel(dest_ref, h_ref, info_ref, rows_ref, p_ref, g_ref, gw_ref, gb_ref, pw_ref, fn_ref,
                    o_ref, buf_a, buf_b, sem, *, tt, n_tok, n_steps, final):
    i = pl.program_id(0)
    bufs = (buf_a, buf_b)

    def row(tile, sl, k, r):
        return _row_copy(rows_ref, dest_ref[k * n_tok + tile * tt + r], bufs[sl].at[k], r, sem.at[sl])

    def wait_rows(sl):
        for k in range(TOP_K_IN_GROUP):
            pltpu.make_async_copy(rows_ref.at[pl.ds(0, tt)], bufs[sl].at[k], sem.at[sl]).wait()

    @pl.when(i == 0)
    def _():
        def issue(r, c):
            for k in range(TOP_K_IN_GROUP):
                row(0, 0, k, r).start(priority=k % 2)
            return c

        lax.fori_loop(0, tt, issue, 0, unroll=8)

    def step(sl):
        wait_rows(sl)
        nxt = jnp.minimum(i + 1, n_steps - 1)
        for r in range(tt):
            for k in range(TOP_K_IN_GROUP):
                row(nxt, 1 - sl, k, r).start(priority=k % 2)
        info = info_ref[...]
        y = h_ref[...] + info[:, 2:3] * bufs[sl][0] + info[:, 3:4] * bufs[sl][1]
        hn = _rms(y, g_ref[...])
        gate = jax.nn.sigmoid(_dot(hn.astype(BF16), gw_ref[...]) + gb_ref[...])
        out = y + _dot(p_ref[0].astype(BF16), pw_ref[...]) * gate
        if final:
            out = _rms(out, fn_ref[...])
        o_ref[...] = out
        pl.when(i == n_steps - 1)(functools.partial(wait_rows, 1 - sl))

    for parity in range(2):
        pl.when(i % 2 == parity)(functools.partial(step, parity))


def _combine(dest, h2, info, rows, p3, layer, g, gw, gb, pw, fn, final):
    T, D = h2.shape
    PD = p3.shape[-1]
    tt = min(CMB_TILE, T)
    full = lambda i, d: (0, 0)
    return pl.pallas_call(
        functools.partial(_combine_kernel, tt=tt, n_tok=T, n_steps=T // tt, final=final),
        out_shape=jax.ShapeDtypeStruct((T, D), F32),
        grid_spec=pltpu.PrefetchScalarGridSpec(
            num_scalar_prefetch=1,
            grid=(T // tt,),
            in_specs=[
                pl.BlockSpec((tt, D), lambda i, d: (i, 0)),
                pl.BlockSpec((tt, LANES), lambda i, d: (i, 0)),
                pl.BlockSpec(memory_space=pl.ANY),
                pl.BlockSpec((1, tt, PD), lambda i, d: (layer, i, 0)),
                pl.BlockSpec((1, D), full),
                pl.BlockSpec((D, D), full),
                pl.BlockSpec((1, D), full),
                pl.BlockSpec((PD, D), full),
                pl.BlockSpec((1, D), full),
            ],
            out_specs=pl.BlockSpec((tt, D), lambda i, d: (i, 0)),
            scratch_shapes=[pltpu.VMEM((TOP_K_IN_GROUP, tt, D), F32), pltpu.VMEM((TOP_K_IN_GROUP, tt, D), F32),
                            pltpu.SemaphoreType.DMA((2,))],
        ),
        compiler_params=_cparams(("arbitrary",), 40),
        name="moe_combine_ple",
    )(dest, h2, info, rows, p3, g.reshape(1, D), gw.astype(BF16), gb.reshape(1, D), pw.astype(BF16),
      fn.reshape(1, D))


def _moe_ple_layer(h2, p3, layer, norm_ffn, rg_w, rg_b, re_w, re_b, w1, w3, w2, norm_ple, gate_w, gate_b, ple_proj,
                   final_norm, final):
    T, D = h2.shape
    A = T * TOP_K_IN_GROUP
    pad = LANES - N_EXPERT_GROUPS - N_EXPERTS
    wr = jnp.concatenate([rg_w, re_w, jnp.zeros((D, pad), F32)], axis=1)
    br = jnp.concatenate([rg_b, re_b, jnp.zeros((pad,), F32)]).reshape(1, LANES)
    info, cnt = _router(h2, norm_ffn, wr, br)
    counts = cnt[0, N_EXPERT_GROUPS:N_EXPERT_GROUPS + N_EXPERTS].astype(I32)
    n_chunks_e = (counts + MOE_CHUNK - 1) // MOE_CHUNK
    chunk_end = jnp.cumsum(n_chunks_e)
    pstarts = (chunk_end - n_chunks_e) * MOE_CHUNK
    n_chunks = -(-A // MOE_CHUNK) + N_EXPERTS
    n_valid = chunk_end[-1:].astype(I32)
    cidx = jnp.minimum(jnp.arange(n_chunks, dtype=I32), n_valid[0] - 1)
    chunk_e = jnp.sum((chunk_end[None, :] <= cidx[:, None]).astype(I32), axis=1)
    chunk_e = jnp.minimum(chunk_e, N_EXPERTS - 1)
    e_idx = info[:, 0:TOP_K_IN_GROUP].astype(I32)
    rank = info[:, 4:4 + TOP_K_IN_GROUP].astype(I32)
    start = jnp.sum(jnp.where(e_idx[..., None] == jnp.arange(N_EXPERTS, dtype=I32), pstarts, 0), axis=-1)
    dest = (start + rank).T.reshape(A)
    xs = _dispatch(dest, chunk_end.astype(I32), h2, norm_ffn, n_chunks * MOE_CHUNK)
    rows = _experts(chunk_e, n_valid, xs, w1, w3, w2, layer)
    return _combine(dest, h2, info, rows, p3, layer, norm_ple, gate_w, gate_b, ple_proj, final_norm, final)


def _rope_rows(xt, cos, sin):
    x1 = xt[0:ROPE_HALF]
    x2 = xt[ROPE_HALF:ROPE_DIMS]
    return jnp.concatenate([x1 * cos - x2 * sin, x2 * cos + x1 * sin, xt[ROPE_DIMS:]], axis=0)


def _rope_tables(pos):
    inv = jnp.float32(ROPE_THETA) ** (-jnp.arange(ROPE_HALF, dtype=F32) * 2.0 / ROPE_DIMS)
    ang = pos.astype(F32)[:, None] * inv[None, :]
    return jnp.cos(ang), jnp.sin(ang)


def _lane_rope_tables(pos):
    cos, sin = _rope_tables(pos)
    n = pos.shape[0]
    ones = jnp.ones((n, HEAD_DIM - ROPE_DIMS), F32)
    zeros = jnp.zeros((n, HEAD_DIM - ROPE_DIMS), F32)
    zh = jnp.zeros((n, ROPE_HALF), F32)
    two = lambda a: jnp.concatenate([a, a], axis=1)
    return (two(jnp.concatenate([cos, cos, ones], axis=1)), two(jnp.concatenate([zh, sin, zeros], axis=1)),
            two(jnp.concatenate([-sin, zh, zeros], axis=1)))


def _rope_lanes(x, c, s1, s2):
    return x * c + pltpu.roll(x, ROPE_HALF, axis=1) * s1 + pltpu.roll(x, LANES - ROPE_HALF, axis=1) * s2


def _kv_kernel(h_ref, g_ref, wn_ref, wt_ref, c_ref, s1_ref, s2_ref, cv_ref, ksa_ref, vst_ref, kw_ref, vwt_ref, *, ts):
    i = pl.program_id(1)
    kvw = N_KV_GROUPS * HEAD_DIM
    hn = _rms(h_ref[0], g_ref[...]).astype(BF16)
    nat = _dot(hn, wn_ref[...])
    cv_ref[0] = nat[:, 0:2 * kvw]
    tr = _dot_nt(wt_ref[...], hn)
    c, s1, s2 = c_ref[...], s1_ref[...], s2_ref[...]
    roped = [_rope_lanes(nat[:, 2 * kvw + k * LANES:2 * kvw + (k + 1) * LANES], c, s1, s2)
             for k in range(2 * kvw // LANES)]
    lane = lax.broadcasted_iota(I32, (ts, KEY_AUG), 1)
    pos = i * ts + lax.broadcasted_iota(I32, (ts, KEY_AUG), 0)
    onehot = jnp.where(lane - HEAD_DIM == pos // SEL_BLOCK, 1.0, 0.0).astype(BF16)
    ones_row = jnp.where(lax.broadcasted_iota(I32, (V_ROWS - HEAD_DIM, ts), 0) == 0, 1.0, 0.0).astype(BF16)
    per_tile = LANES // HEAD_DIM
    for g in range(N_KV_GROUPS):
        lo = (g % per_tile) * HEAD_DIM
        ksa_ref[0, g] = onehot
        ksa_ref[0, g, :, 0:HEAD_DIM] = roped[g // per_tile][:, lo:lo + HEAD_DIM].astype(BF16)
        kw_ref[0, g] = roped[N_KV_GROUPS // per_tile + g // per_tile][:, lo:lo + HEAD_DIM].astype(BF16)
        for ref, base in ((vst_ref, 0), (vwt_ref, kvw)):
            ref[0, g, 0:HEAD_DIM, :] = tr[base + g * HEAD_DIM:base + (g + 1) * HEAD_DIM].astype(BF16)
            ref[0, g, HEAD_DIM:, :] = ones_row


def _shared_kv_proj(h, kv_norm, w_kv):
    B, S, D = h.shape
    G, dh = N_KV_GROUPS, HEAD_DIM
    kvw = G * dh
    ts = min(SEQ_TILE, S)
    br = lambda k: w_kv[:, k * kvw:(k + 1) * kvw]
    w_nat = jnp.concatenate([br(0), br(1), br(2), br(4)], axis=1).astype(BF16)
    w_tr = jnp.concatenate([br(3), br(5)], axis=1).T.astype(BF16)
    tabs = _lane_rope_tables(jnp.arange(S))
    tab = pl.BlockSpec((ts, LANES), lambda b, i: (i, 0))
    return pl.pallas_call(
        functools.partial(_kv_kernel, ts=ts),
        out_shape=(
            jax.ShapeDtypeStruct((B, S, 2 * kvw), F32),
            jax.ShapeDtypeStruct((B, G, S, KEY_AUG), BF16),
            jax.ShapeDtypeStruct((B, G, V_ROWS, S), BF16),
            jax.ShapeDtypeStruct((B, G, S, dh), BF16),
            jax.ShapeDtypeStruct((B, G, V_ROWS, S), BF16),
        ),
        grid=(B, S // ts),
        in_specs=[
            pl.BlockSpec((1, ts, D), lambda b, i: (b, i, 0)),
            pl.BlockSpec((1, D), lambda b, i: (0, 0)),
            pl.BlockSpec((D, 4 * kvw), lambda b, i: (0, 0)),
            pl.BlockSpec((2 * kvw, D), lambda b, i: (0, 0)),
            tab, tab, tab,
        ],
        out_specs=(
            pl.BlockSpec((1, ts, 2 * kvw), lambda b, i: (b, i, 0)),
            pl.BlockSpec((1, G, ts, KEY_AUG), lambda b, i: (b, 0, i, 0)),
            pl.BlockSpec((1, G, V_ROWS, ts), lambda b, i: (b, 0, 0, i)),
            pl.BlockSpec((1, G, ts, dh), lambda b, i: (b, 0, i, 0)),
            pl.BlockSpec((1, G, V_ROWS, ts), lambda b, i: (b, 0, 0, i)),
        ),
        compiler_params=_cparams(("parallel", "parallel"), 48),
        name="shared_kv_proj",
    )(h, kv_norm.reshape(1, D), w_nat, w_tr, *tabs)


def _compress_kernel(x_ref, pos_ref, w1_ref, b1_ref, w2_ref, b2_ref, c_ref, s1_ref, s2_ref, o_ref, *, nh, keys):
    x = x_ref[0, 0, 0]
    a = _dot((x + pos_ref[0:1]).astype(BF16), w1_ref[0])
    b = _dot((x + pos_ref[1:2]).astype(BF16), w1_ref[1])
    hid = jax.nn.gelu(a + pltpu.roll(b, nh - 1, axis=0) + b1_ref[...]).astype(BF16)
    if keys:
        out = _rope_lanes(_dot(hid, w2_ref[...]) + b2_ref[...], c_ref[...], s1_ref[...], s2_ref[...])
        o_ref[0, 0] = out[:, 0:HEAD_DIM].astype(BF16)
    else:
        o_ref[0, 0] = (_dot_nt(w2_ref[...], hid) + b2_ref[...]).astype(BF16)


def _compress(halves, which, pos_emb, w1, b1, w2, b2, keys):
    _, B, G, nh, hw = halves.shape
    dh = HEAD_DIM
    hidden = w1.shape[-1]
    pos2 = pos_emb.reshape(2, hw)
    w1s = w1.reshape(2, hw, hidden).astype(BF16)
    tabs = _lane_rope_tables(jnp.arange(nh) * CMP_STRIDE + CMP_BLOCK - 1)
    if keys:
        w2a = jnp.pad(w2, ((0, 0), (0, LANES - dh))).astype(BF16)
        b2a = jnp.pad(b2, (0, LANES - dh)).reshape(1, LANES)
        out_shape, out_block = (B, G, nh, dh), (1, 1, nh, dh)
    else:
        w2a, b2a = w2.T.astype(BF16), b2.reshape(dh, 1)
        out_shape, out_block = (B, G, dh, nh), (1, 1, dh, nh)
    c2 = lambda b, g: (0, 0)
    tab = pl.BlockSpec((nh, LANES), c2)
    return pl.pallas_call(
        functools.partial(_compress_kernel, nh=nh, keys=keys),
        out_shape=jax.ShapeDtypeStruct(out_shape, BF16),
        grid=(B, G),
        in_specs=[
            pl.BlockSpec((1, 1, 1, nh, hw), lambda b, g: (which, b, g, 0, 0)),
            pl.BlockSpec((2, hw), c2),
            pl.BlockSpec((2, hw, hidden), lambda b, g: (0, 0, 0)),
            pl.BlockSpec((1, hidden), c2),
            pl.BlockSpec(w2a.shape, c2),
            pl.BlockSpec(b2a.shape, c2),
            tab, tab, tab,
        ],
        out_specs=pl.BlockSpec(out_block, lambda b, g: (b, g, 0, 0)),
        compiler_params=_cparams(("parallel", "parallel"), 40),
        name="compress_k" if keys else "compress_v",
    )(halves, pos2, w1s, b1.reshape(1, hidden), w2a, b2a, *tabs)


def _qproj_kernel(h_ref, g_ref, wqt_ref, wg_ref, bg_ref, cos_ref, sin_ref, qt_ref, gt_ref):
    xn = _rms(h_ref[0], g_ref[...]).astype(BF16)
    tr = _dot_nt(wqt_ref[...], xn)
    cos, sin = cos_ref[...], sin_ref[...]
    for hd in range(N_HEADS):
        rows = slice(hd * HEAD_DIM, (hd + 1) * HEAD_DIM)
        qt_ref[0, rows, :] = (_rope_rows(tr[rows], cos, sin) * Q_SCALE).astype(BF16)
    gt_ref[0] = jax.nn.sigmoid(_dot(xn, wg_ref[...]) + bg_ref[...])


def _q_proj(h, g, w_qg, b_gate):
    B, S, D = h.shape
    HD = N_HEADS * HEAD_DIM
    ng = N_HEADS * N_BRANCH
    ts = min(SEQ_TILE, S)
    wqt = w_qg[:, :HD].T.astype(BF16)
    wg = jnp.pad(w_qg[:, HD:], ((0, 0), (0, LANES - ng))).astype(BF16)
    bg = jnp.pad(b_gate, (0, LANES - ng)).reshape(1, LANES)
    cos, sin = _rope_tables(jnp.arange(S))
    tab = pl.BlockSpec((ROPE_HALF, ts), lambda b, i: (0, i))
    return pl.pallas_call(
        _qproj_kernel,
        out_shape=(jax.ShapeDtypeStruct((B, HD, S), BF16), jax.ShapeDtypeStruct((B, S, LANES), F32)),
        grid=(B, S // ts),
        in_specs=[
            pl.BlockSpec((1, ts, D), lambda b, i: (b, i, 0)),
            pl.BlockSpec((1, D), lambda b, i: (0, 0)),
            pl.BlockSpec((HD, D), lambda b, i: (0, 0)),
            pl.BlockSpec((D, LANES), lambda b, i: (0, 0)),
            pl.BlockSpec((1, LANES), lambda b, i: (0, 0)),
            tab, tab,
        ],
        out_specs=(pl.BlockSpec((1, HD, ts), lambda b, i: (b, 0, i)),
                   pl.BlockSpec((1, ts, LANES), lambda b, i: (b, i, 0))),
        compiler_params=_cparams(("parallel", "parallel"), 40),
        name="nsa_q_proj",
    )(h, g.reshape(1, D), wqt, wg, bg, cos.T, sin.T)


def _heads_on_lanes(qt):
    return jnp.concatenate([qt[r * HEAD_DIM:(r + 1) * HEAD_DIM] for r in range(HEADS_PER_GROUP)], axis=1)


def _lane_query_pos(i, tq):
    lanes = lax.broadcasted_iota(I32, (1, HEADS_PER_GROUP * tq), 1)
    return i * tq + (lanes & (tq - 1))


def _gated(acc_t, gates_t, g, branch, denom_row):
    row = lax.broadcasted_iota(I32, gates_t.shape, 0)
    scale = jnp.concatenate(
        [jnp.sum(jnp.where(row == (g * HEADS_PER_GROUP + r) * N_BRANCH + branch, gates_t, 0.0), axis=0, keepdims=True)
         for r in range(HEADS_PER_GROUP)], axis=1)
    if denom_row is not None:
        scale = scale / acc_t[denom_row:denom_row + 1]
    return acc_t[0:HEAD_DIM] * scale


def _unstack_heads(o_t, tq):
    per = LANES // HEAD_DIM
    cols = []
    for r in range(0, HEADS_PER_GROUP, per):
        cols.append(jnp.concatenate([o_t[:, (r + k) * tq:(r + k + 1) * tq] for k in range(per)], axis=0).T)
    return jnp.concatenate(cols, axis=-1)


def _cmp_kernel(qt_ref, gt_ref, k_ref, vt_ref, c2s_ref, o_ref, bias_ref, *, tq, nc):
    g = pl.program_id(1)
    i = pl.program_id(2)

    def body(nk, nb):
        s = _dot(k_ref[0, 0, 0:nk, :], _heads_on_lanes(qt_ref[0]))
        t = _lane_query_pos(i, tq)
        n = lax.broadcasted_iota(I32, (nk, 1), 0)
        s = jnp.where(n * CMP_STRIDE + (CMP_BLOCK - 1) <= t, s, -jnp.inf)
        m = jnp.max(s, axis=0, keepdims=True)
        m = jnp.where(m == -jnp.inf, 0.0, m)
        e = jnp.exp2(s - m)
        p = e * (1.0 / jnp.maximum(jnp.sum(e, axis=0, keepdims=True), 1e-30))
        ot = _dot(vt_ref[0, 0, :, 0:nk], p.astype(BF16))
        o_ref[0] = _unstack_heads(_gated(ot, gt_ref[0].T, g, 0, None), tq).astype(BF16)
        ps = p[:, 0:tq]
        for r in range(1, HEADS_PER_GROUP):
            ps = ps + p[:, r * tq:(r + 1) * tq]
        hi = ps.astype(BF16)
        rem = ps - hi.astype(F32)
        mid = rem.astype(BF16)
        lo = (rem - mid.astype(F32)).astype(BF16)
        c2s = c2s_ref[0:nb, 0:nk]
        imp = _dot(c2s, hi) + _dot(c2s, mid) + _dot(c2s, lo)
        cur = (i * tq + lax.broadcasted_iota(I32, (1, tq), 1)) // SEL_BLOCK
        j = lax.broadcasted_iota(I32, (nb, tq), 0)
        valid = j <= cur
        forced = jnp.where(j == 0, 1.0, jnp.where(j == cur, 1.0, jnp.where(j == cur - 1, 1.0, 0.0)))
        sel = jnp.where(valid, forced, 0.0)
        work = jnp.where(valid, jnp.where(forced > 0.0, -jnp.inf, imp), -jnp.inf)
        for _ in range(N_SELECT - N_FORCED):
            mx = jnp.max(work, axis=0, keepdims=True)
            idx = jnp.min(jnp.where(work == mx, j, MAX_SEL_BLOCKS), axis=0, keepdims=True)
            pick = j == idx
            sel = jnp.where(pick, 1.0, sel)
            work = jnp.where(pick, -jnp.inf, work)
        bias_ref[0, 0, 0:nb, :] = jnp.where(valid, jnp.where(sel > 0.0, 0.0, MASK_BIAS), MASK_BIAS).astype(BF16)
        if nb < MAX_SEL_BLOCKS:
            bias_ref[0, 0, nb:, :] = jnp.full((MAX_SEL_BLOCKS - nb, tq), MASK_BIAS, BF16)

    per_class = LANES * CMP_STRIDE // tq
    n_class = -(-(nc * CMP_STRIDE // tq) // per_class)
    for cls in range(n_class):
        nk = min(nc, LANES * (cls + 1))
        nb = min(MAX_SEL_BLOCKS, nk * CMP_STRIDE // SEL_BLOCK)
        pl.when(i // per_class == cls)(functools.partial(body, nk, nb))


def _cmp_attn(qt, gates, kc, vct):
    B, HD, S = qt.shape
    G, dh = N_KV_GROUPS, HEAD_DIM
    gw = HD // G
    nc = kc.shape[2]
    tq = min(Q_TILE, S)
    n0 = jnp.arange(nc)[None, :] * CMP_STRIDE
    j0 = jnp.arange(MAX_SEL_BLOCKS)[:, None] * SEL_BLOCK
    c2s = ((n0 < j0 + SEL_BLOCK) & (n0 + CMP_BLOCK > j0)).astype(BF16)
    return pl.pallas_call(
        functools.partial(_cmp_kernel, tq=tq, nc=nc),
        out_shape=(jax.ShapeDtypeStruct((B, S, HD), BF16), jax.ShapeDtypeStruct((B, G, MAX_SEL_BLOCKS, S), BF16)),
        grid=(B, G, S // tq),
        in_specs=[
            pl.BlockSpec((1, gw, tq), lambda b, g, i: (b, g, i)),
            pl.BlockSpec((1, tq, LANES), lambda b, g, i: (b, i, 0)),
            pl.BlockSpec((1, 1, nc, dh), lambda b, g, i: (b, g, 0, 0)),
            pl.BlockSpec((1, 1, dh, nc), lambda b, g, i: (b, g, 0, 0)),
            pl.BlockSpec((MAX_SEL_BLOCKS, nc), lambda b, g, i: (0, 0)),
        ],
        out_specs=(pl.BlockSpec((1, tq, gw), lambda b, g, i: (b, i, g)),
                   pl.BlockSpec((1, 1, MAX_SEL_BLOCKS, tq), lambda b, g, i: (b, g, 0, i))),
        compiler_params=_cparams(("parallel", "parallel", "parallel"), 40),
        name="nsa_compressed",
    )(qt, gates, kc, vct, c2s)


def _sel_win_kernel(qt_ref, bias_ref, gt_ref, k_ref, vt_ref, kw_ref, vwt_ref, o_ref, qa, m_s, acc, acc_w, s_a, s_b,
                    *, tq, tk):
    g = pl.program_id(1)
    i = pl.program_id(2)
    qt = qt_ref[0]
    bias = bias_ref[0, 0]
    for r in range(HEADS_PER_GROUP):
        cols = slice(r * tq, (r + 1) * tq)
        qa[0:HEAD_DIM, cols] = qt[r * HEAD_DIM:(r + 1) * HEAD_DIM]
        qa[HEAD_DIM:HEAD_DIM + MAX_SEL_BLOCKS, cols] = bias
        qa[HEAD_DIM + MAX_SEL_BLOCKS:, cols] = jnp.zeros((KEY_AUG - HEAD_DIM - MAX_SEL_BLOCKS, tq), BF16)
    m_s[...] = jnp.full(m_s.shape, NEG_BIG, F32)
    acc[...] = jnp.zeros(acc.shape, F32)
    width = HEADS_PER_GROUP * tq

    def qk(jt, buf):
        buf[...] = _dot(k_ref[0, 0, pl.ds(pl.multiple_of(jt * tk, tk), tk), :], qa[...])

    def online_softmax(key0, n_keys, s):
        m_old = m_s[...]
        m_new = jnp.maximum(m_old, jnp.max(s, axis=0, keepdims=True))
        p = jnp.exp2(s - m_new).astype(BF16)
        acc[...] = jnp.exp2(m_old - m_new) * acc[...] + _dot(vt_ref[0, 0, :, pl.ds(key0, n_keys)], p)
        m_s[...] = m_new

    def absorb(jt, buf):
        online_softmax(pl.multiple_of(jt * tk, tk), tk, buf[...])

    key_row = lax.broadcasted_iota(I32, (tq, 1), 0)
    query_col = lax.broadcasted_iota(I32, (1, width), 1) & (tq - 1)

    def own_tile(s):
        return jnp.where(key_row <= query_col, s, NEG_BIG)

    def absorb_last(jt, buf):
        k0 = pl.multiple_of(jt * tk, tk)

        @pl.when(i % 2 == 1)
        def _():
            online_softmax(k0, tq, buf[0:tq, :])
            online_softmax(pl.multiple_of(k0 + tq, tq), tq, own_tile(buf[tq:tk, :]))

        @pl.when(i % 2 == 0)
        def _():
            online_softmax(k0, tq, own_tile(buf[0:tq, :]))

    def pair(u, c):
        qk(2 * u + 1, s_b)
        absorb(2 * u, s_a)
        qk(2 * u + 2, s_a)
        absorb(2 * u + 1, s_b)
        return c

    def window(miss):
        n_keys = (nw + 1 - miss) * tq
        k0 = pl.multiple_of((i - nw + miss) * tq, tq)
        s_b[0:n_keys, :] = _dot(kw_ref[0, 0, pl.ds(k0, n_keys), :], qa[0:HEAD_DIM, :])
        qk(0, s_a)
        parts = []
        for j in range(miss, nw + 1):
            part = s_b[(j - miss) * tq:(j - miss + 1) * tq, :]
            if j == 0:
                part = jnp.where(key_row > query_col, part, NEG_BIG)
            parts.append(own_tile(part) if j == nw else part)
        s = jnp.concatenate(parts, axis=0)
        m = jnp.max(s, axis=0, keepdims=True)
        p = jnp.exp2(s - m).astype(BF16)
        acc_w[...] = _dot(vwt_ref[0, 0, :, pl.ds(k0, n_keys)], p)

    nw = WINDOW // tq
    for miss in range(nw + 1):
        pl.when(jnp.maximum(nw - i, 0) == miss)(functools.partial(window, miss))

    last = (i * tq + tq + tk - 1) // tk - 1
    lax.fori_loop(0, last // 2, pair, 0)

    @pl.when(last % 2 == 1)
    def _():
        qk(last, s_b)
        absorb(last - 1, s_a)
        absorb_last(last, s_b)

    @pl.when(last % 2 == 0)
    def _():
        absorb_last(last, s_a)

    gates_t = gt_ref[0].T
    o_t = _gated(acc[...], gates_t, g, 1, HEAD_DIM) + _gated(acc_w[...], gates_t, g, 2, HEAD_DIM)
    o_ref[0] = _unstack_heads(o_t, tq).astype(BF16)


def _sel_win_attn(qt, bias, gates, ksa, vst, kw, vwt):
    B, HD, S = qt.shape
    G, dh = N_KV_GROUPS, HEAD_DIM
    gw = HD // G
    tq = min(SEL_Q_TILE, S // 2)
    tk = 2 * tq
    assert S % tk == 0 and WINDOW % tq == 0 and WINDOW + tq <= tk
    width = HEADS_PER_GROUP * tq
    return pl.pallas_call(
        functools.partial(_sel_win_kernel, tq=tq, tk=tk),
        out_shape=jax.ShapeDtypeStruct((B, S, HD), BF16),
        grid=(B, G, S // tq),
        in_specs=[
            pl.BlockSpec((1, gw, tq), lambda b, g, i: (b, g, i)),
            pl.BlockSpec((1, 1, MAX_SEL_BLOCKS, tq), lambda b, g, i: (b, g, 0, i)),
            pl.BlockSpec((1, tq, LANES), lambda b, g, i: (b, i, 0)),
            pl.BlockSpec((1, 1, S, KEY_AUG), lambda b, g, i: (b, g, 0, 0)),
            pl.BlockSpec((1, 1, V_ROWS, S), lambda b, g, i: (b, g, 0, 0)),
            pl.BlockSpec((1, 1, S, dh), lambda b, g, i: (b, g, 0, 0)),
            pl.BlockSpec((1, 1, V_ROWS, S), lambda b, g, i: (b, g, 0, 0)),
        ],
        out_specs=pl.BlockSpec((1, tq, gw), lambda b, g, i: (b, i, g)),
        scratch_shapes=[pltpu.VMEM((KEY_AUG, width), BF16), pltpu.VMEM((1, width), F32),
                        pltpu.VMEM((V_ROWS, width), F32), pltpu.VMEM((V_ROWS, width), F32),
                        pltpu.VMEM((tk, width), F32), pltpu.VMEM((tk, width), F32)],
        compiler_params=_cparams(("parallel", "parallel", "arbitrary"), 60),
        name="nsa_selected_window",
    )(qt, bias, gates, ksa, vst, kw, vwt)


def _oproj_kernel(h_ref, a_ref, b_ref, w_ref, o_ref):
    o = a_ref[...].astype(F32) + b_ref[...].astype(F32)
    o_ref[...] = h_ref[...] + _dot(o.astype(BF16), w_ref[...])


def _out_proj(h2, oc, osw, w_o):
    T, D = h2.shape
    HD = oc.shape[-1]
    tt = min(SEQ_TILE, T)
    blk = lambda w: pl.BlockSpec((tt, w), lambda i: (i, 0))
    return pl.pallas_call(
        _oproj_kernel,
        out_shape=jax.ShapeDtypeStruct((T, D), F32),
        grid=(T // tt,),
        in_specs=[blk(D), blk(HD), blk(HD), pl.BlockSpec((HD, D), lambda i: (0, 0))],
        out_specs=blk(D),
        compiler_params=_cparams(("parallel",), 40),
        name="nsa_out_proj",
    )(h2, oc, osw, w_o.astype(BF16))


def _nsa_layer(h, g, w_qg, b_gate, w_o, shared):
    B, S, D = h.shape
    kc, vct, ksa, vst, kw, vwt = shared
    qt, gates = _q_proj(h, g, w_qg, b_gate)
    oc, bias = _cmp_attn(qt, gates, kc, vct)
    osw = _sel_win_attn(qt, bias, gates, ksa, vst, kw, vwt)
    flat = lambda a: a.reshape(B * S, a.shape[-1])
    return _out_proj(flat(h), flat(oc), flat(osw), w_o).reshape(B, S, D)


def _shared_kv(h, kv_norm, w_kv, ck, cv):
    B, S, _ = h.shape
    G, dh = N_KV_GROUPS, HEAD_DIM
    assert S % SEL_BLOCK == 0 and N_SELECT <= S // SEL_BLOCK <= MAX_SEL_BLOCKS
    cvals, ksa, vst, kw, vwt = _shared_kv_proj(h, kv_norm, w_kv)
    halves = cvals.reshape(B, S, 2, G, dh).transpose(2, 0, 3, 1, 4).reshape(2, B, G, S // CMP_STRIDE, CMP_STRIDE * dh)
    kc = _compress(halves, 0, *ck, keys=True)
    vct = _compress(halves, 1, *cv, keys=False)
    return kc, vct, ksa, vst, kw, vwt


def kernel(x, p, norm_mix, norm_ffn, norm_ple, pool_w, pool_b, pool_scale, kv_norm, w_kv, cmp_k_pos, cmp_k_w1, cmp_k_b1, cmp_k_w2, cmp_k_b2, cmp_v_pos, cmp_v_w1, cmp_v_b1, cmp_v_w2, cmp_v_b2, w_qg, b_gate, w_o, router_g_w, router_g_b, router_e_w, router_e_b, moe_w1, moe_w3, moe_w2, ple_proj, ple_gate_w, ple_gate_b, final_norm):
    B, S, D = x.shape
    depth = p.shape[0]
    n_a = pool_w.shape[0]
    T = B * S
    h = x
    shared = None
    for i in range(depth):
        if i == n_a:
            shared = _shared_kv(h, kv_norm, w_kv,
                                (cmp_k_pos, cmp_k_w1, cmp_k_b1, cmp_k_w2, cmp_k_b2),
                                (cmp_v_pos, cmp_v_w1, cmp_v_b1, cmp_v_w2, cmp_v_b2))
        if i < n_a:
            h = _pool_layer(h, norm_mix[i], pool_w[i], pool_b[i], pool_scale[i])
        else:
            j = i - n_a
            h = _nsa_layer(h, norm_mix[i], w_qg[j], b_gate[j], w_o[j], shared)
        h = _moe_ple_layer(h.reshape(T, D), p.reshape(depth, T, p.shape[-1]), i, norm_ffn[i], router_g_w[i],
                           router_g_b[i], router_e_w[i], router_e_b[i], moe_w1, moe_w3, moe_w2, norm_ple[i],
                           ple_gate_w[i], ple_gate_b[i], ple_proj[i], final_norm, i == depth - 1).reshape(B, S, D)
    return h
```

```python
import functools

import jax
import jax.numpy as jnp
from jax import lax
from jax.experimental import pallas as pl
from jax.experimental.pallas import tpu as pltpu

F32 = jnp.float32
BF16 = jnp.bfloat16
I32 = jnp.int32

POOL_WINDOWS = (2, 4, 8, 16)
N_HEADS = 16
HEAD_DIM = 64
N_KV_GROUPS = 4
HEADS_PER_GROUP = N_HEADS // N_KV_GROUPS
N_BRANCH = 3
ROPE_DIMS = HEAD_DIM // 4
ROPE_HALF = ROPE_DIMS // 2
ROPE_THETA = 500000.0
CMP_BLOCK = 32
CMP_STRIDE = 16
SEL_BLOCK = 64
N_SELECT = 16
WINDOW = 512
FORCE_BONUS = 1e4
N_FORCED = 3
N_EXPERT_GROUPS = 4
EXPERTS_PER_GROUP = 8
N_EXPERTS = N_EXPERT_GROUPS * EXPERTS_PER_GROUP
TOP_K_IN_GROUP = 2
RMS_EPS = 1e-6

LANES = 128
MAX_SEL_BLOCKS = LANES
MASK_BIAS = -30000.0

LOG2E = 1.4426950408889634
Q_SCALE = HEAD_DIM ** -0.5 * LOG2E
V_ROWS = HEAD_DIM + 16
KEY_AUG = 2 * LANES
NEG_BIG = -1e30

SEQ_TILE = 1024
TOK_TILE = 512
CMB_TILE = 512
MOE_CHUNK = 512
Q_TILE = 512
SEL_Q_TILE = 512
HALO = 16


def _cparams(sem, vmem_mb):
    return pltpu.CompilerParams(dimension_semantics=sem, vmem_limit_bytes=vmem_mb * 1024 * 1024)


def _rms(x, g):
    return x * lax.rsqrt(jnp.mean(x * x, axis=-1, keepdims=True) + RMS_EPS) * g


def _dot(a, b):
    return jnp.dot(a, b, preferred_element_type=F32)


def _dot_nt(a, b):
    return lax.dot_general(a, b, (((1,), (1,)), ((), ())), preferred_element_type=F32)


def _pool_kernel(h_ref, halo_ref, g_ref, w_ref, b_ref, sc_ref, o_ref, *, ts, cg):
    i = pl.program_id(1)
    x = h_ref[0]
    g = g_ref[...]
    xn = _rms(x, g)
    hn = _rms(halo_ref[0], g)
    hn = jnp.where(i > 0, hn, 0.0)
    ext = jnp.concatenate([hn, xn], axis=0)
    t = i * ts + lax.broadcasted_iota(I32, (ts, 1), 0)
    outs = []
    for gi, w in enumerate(POOL_WINDOWS):
        s = ext[:, gi * cg:(gi + 1) * cg]
        k = 1
        while k < w:
            s = s + pltpu.roll(s, k, axis=0)
            k *= 2
        cnt = jnp.minimum(t + 1, w).astype(F32)
        pooled = s[HALO:] / cnt - xn[:, gi * cg:(gi + 1) * cg]
        outs.append(_dot(pooled.astype(BF16), w_ref[gi]))
    y = jnp.concatenate(outs, axis=-1)
    o_ref[0] = x + (y + b_ref[...]) * sc_ref[...]


def _pool_layer(h, g, w, b, sc):
    B, S, D = h.shape
    ts = min(SEQ_TILE, S)
    cg = D // len(POOL_WINDOWS)
    row = lambda v: v.reshape(1, D)
    return pl.pallas_call(
        functools.partial(_pool_kernel, ts=ts, cg=cg),
        out_shape=jax.ShapeDtypeStruct((B, S, D), F32),
        grid=(B, S // ts),
        in_specs=[
            pl.BlockSpec((1, ts, D), lambda b_, i: (b_, i, 0)),
            pl.BlockSpec((1, HALO, D), lambda b_, i: (b_, jnp.maximum(i * (ts // HALO) - 1, 0), 0)),
            pl.BlockSpec((1, D), lambda b_, i: (0, 0)),
            pl.BlockSpec((len(POOL_WINDOWS), cg, cg), lambda b_, i: (0, 0, 0)),
            pl.BlockSpec((1, D), lambda b_, i: (0, 0)),
            pl.BlockSpec((1, D), lambda b_, i: (0, 0)),
        ],
        out_specs=pl.BlockSpec((1, ts, D), lambda b_, i: (b_, i, 0)),
        compiler_params=_cparams(("parallel", "parallel"), 40),
        name="pool_mixer",
    )(h, h, row(g), w.astype(BF16), row(b), row(sc))


def _route(h, g_ref, wh_ref, wl_ref, b_ref, cnt):
    tt = h.shape[0]
    xn = _rms(h, g_ref[...])
    xh = xn.astype(BF16)
    xl = (xn - xh.astype(F32)).astype(BF16)
    logits = _dot(xh, wh_ref[...]) + (_dot(xh, wl_ref[...]) + _dot(xl, wh_ref[...])) + b_ref[...]
    lane = lax.broadcasted_iota(I32, (tt, LANES), 1)
    neg = -jnp.inf
    gl = jnp.where(lane < N_EXPERT_GROUPS, logits, neg)
    gmax = jnp.max(gl, axis=-1, keepdims=True)
    grp = jnp.min(jnp.where(gl == gmax, lane, LANES), axis=-1, keepdims=True)
    gprob = 1.0 / jnp.sum(jnp.exp(gl - gmax), axis=-1, keepdims=True)
    lo = N_EXPERT_GROUPS + grp * EXPERTS_PER_GROUP
    el = jnp.where(lane >= lo, jnp.where(lane < lo + EXPERTS_PER_GROUP, logits, neg), neg)
    v1 = jnp.max(el, axis=-1, keepdims=True)
    i1 = jnp.min(jnp.where(el == v1, lane, LANES), axis=-1, keepdims=True)
    el2 = jnp.where(lane == i1, neg, el)
    v2 = jnp.max(el2, axis=-1, keepdims=True)
    i2 = jnp.min(jnp.where(el2 == v2, lane, LANES), axis=-1, keepdims=True)
    e2 = jnp.exp(v2 - v1)
    w1 = gprob / (1.0 + e2)
    w2 = gprob * e2 / (1.0 + e2)
    oh1 = lane == i1
    oh2 = lane == i2
    oh = jnp.where(oh1, 1.0, jnp.where(oh2, 1.0, 0.0))
    r_ = lax.broadcasted_iota(I32, (tt, tt), 0)
    c_ = lax.broadcasted_iota(I32, (tt, tt), 1)
    tri = jnp.where(r_ > c_, 1.0, 0.0).astype(BF16)
    tot = _dot(tri, oh.astype(BF16)) + cnt
    r1 = jnp.sum(jnp.where(oh1, tot, 0.0), axis=-1, keepdims=True)
    r2 = jnp.sum(jnp.where(oh2, tot, 0.0), axis=-1, keepdims=True)
    vals = (i1.astype(F32) - N_EXPERT_GROUPS, i2.astype(F32) - N_EXPERT_GROUPS, w1, w2, r1, r2)
    info = jnp.zeros((tt, LANES), F32)
    for k, v in enumerate(vals):
        info = jnp.where(lane == k, v, info)
    return info, cnt + jnp.sum(oh, axis=0, keepdims=True)


def _router_kernel(h_ref, g_ref, wh_ref, wl_ref, b_ref, info_ref, cnt_ref):
    @pl.when(pl.program_id(0) == 0)
    def _():
        cnt_ref[...] = jnp.zeros_like(cnt_ref)

    info_ref[...], cnt_ref[...] = _route(h_ref[...], g_ref, wh_ref, wl_ref, b_ref, cnt_ref[...])


def _router_operands(norm_ffn, rg_w, rg_b, re_w, re_b):
    D = rg_w.shape[0]
    pad = LANES - N_EXPERT_GROUPS - N_EXPERTS
    wr = jnp.concatenate([rg_w, re_w, jnp.zeros((D, pad), F32)], axis=1)
    br = jnp.concatenate([rg_b, re_b, jnp.zeros((pad,), F32)]).reshape(1, LANES)
    wh = wr.astype(BF16)
    return norm_ffn.reshape(1, D), wh, (wr - wh.astype(F32)).astype(BF16), br


def _router_specs(D, const):
    return [pl.BlockSpec((1, D), const), pl.BlockSpec((D, LANES), const), pl.BlockSpec((D, LANES), const),
            pl.BlockSpec((1, LANES), const)]


def _router(h2, router_ops):
    T, D = h2.shape
    tt = min(TOK_TILE, T)
    return pl.pallas_call(
        _router_kernel,
        out_shape=(jax.ShapeDtypeStruct((T, LANES), F32), jax.ShapeDtypeStruct((1, LANES), F32)),
        grid=(T // tt,),
        in_specs=[pl.BlockSpec((tt, D), lambda i: (i, 0))] + _router_specs(D, lambda i: (0, 0)),
        out_specs=(pl.BlockSpec((tt, LANES), lambda i: (i, 0)), pl.BlockSpec((1, LANES), lambda i: (0, 0))),
        compiler_params=_cparams(("arbitrary",), 40),
        name="moe_router",
    )(h2, *router_ops)


def _row_copy(src, s, dst, d, sem):
    return pltpu.make_async_copy(src.at[pl.ds(s, 1)], dst.at[pl.ds(d, 1)], sem)


def _dispatch_kernel(dest_ref, cend_ref, h_ref, g_ref, xs_ref, xn_s, sem, *, tt, n_tok, n_chunks):
    base = pl.program_id(0) * tt

    @pl.when(pl.program_id(0) == 0)
    def _():
        xn_s[...] = jnp.zeros_like(xn_s)

        def tail(e):
            nonempty = cend_ref[e] > (cend_ref[e - 1] if e > 0 else 0)
            row = pl.multiple_of((cend_ref[e] - 1) * tt, tt)
            return nonempty, pltpu.make_async_copy(xn_s, xs_ref.at[pl.ds(row, tt)], sem)

        def unused(c):
            return pltpu.make_async_copy(xn_s, xs_ref.at[pl.ds(pl.multiple_of(c * tt, tt), tt)], sem)

        n_used = cend_ref[N_EXPERTS - 1]
        for e in range(N_EXPERTS):
            nonempty, cp = tail(e)
            pl.when(nonempty)(cp.start)
        lax.fori_loop(n_used, n_chunks, lambda c, z: (unused(c).start(), z)[1], 0)
        for e in range(N_EXPERTS):
            nonempty, cp = tail(e)
            pl.when(nonempty)(cp.wait)
        lax.fori_loop(n_used, n_chunks, lambda c, z: (unused(c).wait(), z)[1], 0)

    xn_s[...] = _rms(h_ref[...], g_ref[...])

    for r in range(tt):
        for k in range(TOP_K_IN_GROUP):
            _row_copy(xn_s, r, xs_ref, dest_ref[k * n_tok + base + r], sem).start(priority=k % 2)
    for k in range(TOP_K_IN_GROUP):
        pltpu.make_async_copy(xn_s, xs_ref.at[pl.ds(0, tt)], sem).wait()


def _dispatch(dest, chunk_end, h2, g, n_rows):
    T, D = h2.shape
    tt = MOE_CHUNK
    assert T % tt == 0
    return pl.pallas_call(
        functools.partial(_dispatch_kernel, tt=tt, n_tok=T, n_chunks=n_rows // tt),
        out_shape=jax.ShapeDtypeStruct((n_rows, D), F32),
        grid_spec=pltpu.PrefetchScalarGridSpec(
            num_scalar_prefetch=2,
            grid=(T // tt,),
            in_specs=[
                pl.BlockSpec((tt, D), lambda i, d, ce: (i, 0)),
                pl.BlockSpec((1, D), lambda i, d, ce: (0, 0)),
            ],
            out_specs=pl.BlockSpec(memory_space=pl.ANY),
            scratch_shapes=[pltpu.VMEM((tt, D), F32), pltpu.SemaphoreType.DMA],
        ),
        compiler_params=_cparams(("arbitrary",), 40),
        name="moe_dispatch",
    )(dest, chunk_end, h2, g.reshape(1, D))


def _expert_kernel(ce_ref, nv_ref, xs_ref, w1_ref, w3_ref, w2_ref, o_ref, w1b, w3b, w2b):
    c = pl.program_id(0)
    e = ce_ref[c]
    prev = ce_ref[jnp.maximum(c - 1, 0)]

    @pl.when(jnp.logical_or(c == 0, e != prev))
    def _():
        w1b[...] = w1_ref[0, 0].astype(BF16)
        w3b[...] = w3_ref[0, 0].astype(BF16)
        w2b[...] = w2_ref[0, 0].astype(BF16)

    @pl.when(c < nv_ref[0])
    def _():
        x = xs_ref[...].astype(BF16)
        a = _dot(x, w1b[...])
        b = _dot(x, w3b[...])
        hc = a * jax.nn.sigmoid(a) * b
        o_ref[...] = _dot(hc.astype(BF16), w2b[...])

    @pl.when(c >= nv_ref[0])
    def _():
        o_ref[...] = jnp.zeros_like(o_ref)


def _experts(chunk_e, n_valid, xs, w1, w3, w2, layer):
    P, D = xs.shape
    F = w1.shape[-1]
    ch = MOE_CHUNK
    rows = lambda c, ce, nv: (jnp.minimum(c, nv[0] - 1), 0)
    wsel = lambda c, ce, nv: (layer, ce[c], 0, 0)
    return pl.pallas_call(
        _expert_kernel,
        out_shape=jax.ShapeDtypeStruct((P, D), F32),
        grid_spec=pltpu.PrefetchScalarGridSpec(
            num_scalar_prefetch=2,
            grid=(P // ch,),
            in_specs=[
                pl.BlockSpec((ch, D), rows),
                pl.BlockSpec((1, 1, D, F), wsel),
                pl.BlockSpec((1, 1, D, F), wsel),
                pl.BlockSpec((1, 1, F, D), wsel),
            ],
            out_specs=pl.BlockSpec((ch, D), lambda c, ce, nv: (c, 0)),
            scratch_shapes=[pltpu.VMEM((D, F), BF16), pltpu.VMEM((D, F), BF16), pltpu.VMEM((F, D), BF16)],
        ),
        compiler_params=_cparams(("arbitrary",), 56),
        name="moe_experts",
    )(chunk_e, n_valid, xs, w1, w3, w2)


def _combine_kernel(dest_ref, h_ref, info_ref, rows_ref, p_ref, g_ref, gw_ref, gb_ref, pw_ref, fn_ref,
                    o_ref, buf_a, buf_b, sem, *, tt, n_tok, n_steps, final):
    i = pl.program_id(0)
    bufs = (buf_a, buf_b)

    def row(tile, sl, k, r):
        return _row_copy(rows_ref, dest_ref[k * n_tok + tile * tt + r], bufs[sl].at[k], r, sem.at[sl])

    def wait_rows(sl):
        for k in range(TOP_K_IN_GROUP):
            pltpu.make_async_copy(rows_ref.at[pl.ds(0, tt)], bufs[sl].at[k], sem.at[sl]).wait()

    @pl.when(i == 0)
    def _():
        def issue(r, c):
            for k in range(TOP_K_IN_GROUP):
                row(0, 0, k, r).start(priority=k % 2)
            return c

        lax.fori_loop(0, tt, issue, 0, unroll=8)

    def step(sl):
        wait_rows(sl)
        nxt = jnp.minimum(i + 1, n_steps - 1)
        for r in range(tt):
            for k in range(TOP_K_IN_GROUP):
                row(nxt, 1 - sl, k, r).start(priority=k % 2)
        info = info_ref[...]
        y = h_ref[...] + info[:, 2:3] * bufs[sl][0] + info[:, 3:4] * bufs[sl][1]
        hn = _rms(y, g_ref[...])
        gate = jax.nn.sigmoid(_dot(hn.astype(BF16), gw_ref[...]) + gb_ref[...])
        out = y + _dot(p_ref[0].astype(BF16), pw_ref[...]) * gate
        if final:
            out = _rms(out, fn_ref[...])
        o_ref[...] = out
        pl.when(i == n_steps - 1)(functools.partial(wait_rows, 1 - sl))

    for parity in range(2):
        pl.when(i % 2 == parity)(functools.partial(step, parity))


def _combine(dest, h2, info, rows, p3, layer, g, gw, gb, pw, fn, final):
    T, D = h2.shape
    PD = p3.shape[-1]
    tt = min(CMB_TILE, T)
    full = lambda i, d: (0, 0)
    return pl.pallas_call(
        functools.partial(_combine_kernel, tt=tt, n_tok=T, n_steps=T // tt, final=final),
        out_shape=jax.ShapeDtypeStruct((T, D), F32),
        grid_spec=pltpu.PrefetchScalarGridSpec(
            num_scalar_prefetch=1,
            grid=(T // tt,),
            in_specs=[
                pl.BlockSpec((tt, D), lambda i, d: (i, 0)),
                pl.BlockSpec((tt, LANES), lambda i, d: (i, 0)),
                pl.BlockSpec(memory_space=pl.ANY),
                pl.BlockSpec((1, tt, PD), lambda i, d: (layer, i, 0)),
                pl.BlockSpec((1, D), full),
                pl.BlockSpec((D, D), full),
                pl.BlockSpec((1, D), full),
                pl.BlockSpec((PD, D), full),
                pl.BlockSpec((1, D), full),
            ],
            out_specs=pl.BlockSpec((tt, D), lambda i, d: (i, 0)),
            scratch_shapes=[pltpu.VMEM((TOP_K_IN_GROUP, tt, D), F32), pltpu.VMEM((TOP_K_IN_GROUP, tt, D), F32),
                            pltpu.SemaphoreType.DMA((2,))],
        ),
        compiler_params=_cparams(("arbitrary",), 40),
        name="moe_combine_ple",
    )(dest, h2, info, rows, p3, g.reshape(1, D), gw.astype(BF16), gb.reshape(1, D), pw.astype(BF16),
      fn.reshape(1, D))


def _moe_ple_layer(h2, routing, p3, layer, norm_ffn, w1, w3, w2, norm_ple, gate_w, gate_b, ple_proj, final_norm, final):
    T, D = h2.shape
    A = T * TOP_K_IN_GROUP
    info, cnt = routing
    counts = cnt[0, N_EXPERT_GROUPS:N_EXPERT_GROUPS + N_EXPERTS].astype(I32)
    n_chunks_e = (counts + MOE_CHUNK - 1) // MOE_CHUNK
    chunk_end = jnp.cumsum(n_chunks_e)
    pstarts = (chunk_end - n_chunks_e) * MOE_CHUNK
    n_chunks = -(-A // MOE_CHUNK) + N_EXPERTS
    n_valid = chunk_end[-1:].astype(I32)
    cidx = jnp.minimum(jnp.arange(n_chunks, dtype=I32), n_valid[0] - 1)
    chunk_e = jnp.sum((chunk_end[None, :] <= cidx[:, None]).astype(I32), axis=1)
    chunk_e = jnp.minimum(chunk_e, N_EXPERTS - 1)
    e_idx = info[:, 0:TOP_K_IN_GROUP].astype(I32)
    rank = info[:, 4:4 + TOP_K_IN_GROUP].astype(I32)
    start = jnp.sum(jnp.where(e_idx[..., None] == jnp.arange(N_EXPERTS, dtype=I32), pstarts, 0), axis=-1)
    dest = (start + rank).T.reshape(A)
    xs = _dispatch(dest, chunk_end.astype(I32), h2, norm_ffn, n_chunks * MOE_CHUNK)
    rows = _experts(chunk_e, n_valid, xs, w1, w3, w2, layer)
    return _combine(dest, h2, info, rows, p3, layer, norm_ple, gate_w, gate_b, ple_proj, final_norm, final)


def _rope_rows(xt, cos, sin):
    x1 = xt[0:ROPE_HALF]
    x2 = xt[ROPE_HALF:ROPE_DIMS]
    return jnp.concatenate([x1 * cos - x2 * sin, x2 * cos + x1 * sin, xt[ROPE_DIMS:]], axis=0)


def _rope_tables(pos):
    inv = jnp.float32(ROPE_THETA) ** (-jnp.arange(ROPE_HALF, dtype=F32) * 2.0 / ROPE_DIMS)
    ang = pos.astype(F32)[:, None] * inv[None, :]
    return jnp.cos(ang), jnp.sin(ang)


def _lane_rope_tables(pos):
    cos, sin = _rope_tables(pos)
    n = pos.shape[0]
    ones = jnp.ones((n, HEAD_DIM - ROPE_DIMS), F32)
    zeros = jnp.zeros((n, HEAD_DIM - ROPE_DIMS), F32)
    zh = jnp.zeros((n, ROPE_HALF), F32)
    two = lambda a: jnp.concatenate([a, a], axis=1)
    return (two(jnp.concatenate([cos, cos, ones], axis=1)), two(jnp.concatenate([zh, sin, zeros], axis=1)),
            two(jnp.concatenate([-sin, zh, zeros], axis=1)))


def _rope_lanes(x, c, s1, s2):
    return x * c + pltpu.roll(x, ROPE_HALF, axis=1) * s1 + pltpu.roll(x, LANES - ROPE_HALF, axis=1) * s2


def _kv_kernel(h_ref, g_ref, wn_ref, wt_ref, c_ref, s1_ref, s2_ref, cv_ref, ksa_ref, vst_ref, kw_ref, vwt_ref, *, ts):
    i = pl.program_id(1)
    kvw = N_KV_GROUPS * HEAD_DIM
    hn = _rms(h_ref[0], g_ref[...]).astype(BF16)
    nat = _dot(hn, wn_ref[...])
    cv_ref[0] = nat[:, 0:2 * kvw]
    tr = _dot_nt(wt_ref[...], hn)
    c, s1, s2 = c_ref[...], s1_ref[...], s2_ref[...]
    roped = [_rope_lanes(nat[:, 2 * kvw + k * LANES:2 * kvw + (k + 1) * LANES], c, s1, s2)
             for k in range(2 * kvw // LANES)]
    lane = lax.broadcasted_iota(I32, (ts, KEY_AUG), 1)
    pos = i * ts + lax.broadcasted_iota(I32, (ts, KEY_AUG), 0)
    onehot = jnp.where(lane - HEAD_DIM == pos // SEL_BLOCK, 1.0, 0.0).astype(BF16)
    ones_row = jnp.where(lax.broadcasted_iota(I32, (V_ROWS - HEAD_DIM, ts), 0) == 0, 1.0, 0.0).astype(BF16)
    per_tile = LANES // HEAD_DIM
    for g in range(N_KV_GROUPS):
        lo = (g % per_tile) * HEAD_DIM
        ksa_ref[0, g] = onehot
        ksa_ref[0, g, :, 0:HEAD_DIM] = roped[g // per_tile][:, lo:lo + HEAD_DIM].astype(BF16)
        kw_ref[0, g] = roped[N_KV_GROUPS // per_tile + g // per_tile][:, lo:lo + HEAD_DIM].astype(BF16)
        for ref, base in ((vst_ref, 0), (vwt_ref, kvw)):
            ref[0, g, 0:HEAD_DIM, :] = tr[base + g * HEAD_DIM:base + (g + 1) * HEAD_DIM].astype(BF16)
            ref[0, g, HEAD_DIM:, :] = ones_row


def _shared_kv_proj(h, kv_norm, w_kv):
    B, S, D = h.shape
    G, dh = N_KV_GROUPS, HEAD_DIM
    kvw = G * dh
    ts = min(SEQ_TILE, S)
    br = lambda k: w_kv[:, k * kvw:(k + 1) * kvw]
    w_nat = jnp.concatenate([br(0), br(1), br(2), br(4)], axis=1).astype(BF16)
    w_tr = jnp.concatenate([br(3), br(5)], axis=1).T.astype(BF16)
    tabs = _lane_rope_tables(jnp.arange(S))
    tab = pl.BlockSpec((ts, LANES), lambda b, i: (i, 0))
    return pl.pallas_call(
        functools.partial(_kv_kernel, ts=ts),
        out_shape=(
            jax.ShapeDtypeStruct((B, S, 2 * kvw), F32),
            jax.ShapeDtypeStruct((B, G, S, KEY_AUG), BF16),
            jax.ShapeDtypeStruct((B, G, V_ROWS, S), BF16),
            jax.ShapeDtypeStruct((B, G, S, dh), BF16),
            jax.ShapeDtypeStruct((B, G, V_ROWS, S), BF16),
        ),
        grid=(B, S // ts),
        in_specs=[
            pl.BlockSpec((1, ts, D), lambda b, i: (b, i, 0)),
            pl.BlockSpec((1, D), lambda b, i: (0, 0)),
            pl.BlockSpec((D, 4 * kvw), lambda b, i: (0, 0)),
            pl.BlockSpec((2 * kvw, D), lambda b, i: (0, 0)),
            tab, tab, tab,
        ],
        out_specs=(
            pl.BlockSpec((1, ts, 2 * kvw), lambda b, i: (b, i, 0)),
            pl.BlockSpec((1, G, ts, KEY_AUG), lambda b, i: (b, 0, i, 0)),
            pl.BlockSpec((1, G, V_ROWS, ts), lambda b, i: (b, 0, 0, i)),
            pl.BlockSpec((1, G, ts, dh), lambda b, i: (b, 0, i, 0)),
            pl.BlockSpec((1, G, V_ROWS, ts), lambda b, i: (b, 0, 0, i)),
        ),
        compiler_params=_cparams(("parallel", "parallel"), 48),
        name="shared_kv_proj",
    )(h, kv_norm.reshape(1, D), w_nat, w_tr, *tabs)


def _compress_kernel(x_ref, pos_ref, w1_ref, b1_ref, w2_ref, b2_ref, c_ref, s1_ref, s2_ref, o_ref, *, nh, keys):
    x = x_ref[0, 0, 0]
    a = _dot((x + pos_ref[0:1]).astype(BF16), w1_ref[0])
    b = _dot((x + pos_ref[1:2]).astype(BF16), w1_ref[1])
    hid = jax.nn.gelu(a + pltpu.roll(b, nh - 1, axis=0) + b1_ref[...]).astype(BF16)
    if keys:
        out = _rope_lanes(_dot(hid, w2_ref[...]) + b2_ref[...], c_ref[...], s1_ref[...], s2_ref[...])
        o_ref[0, 0] = out[:, 0:HEAD_DIM].astype(BF16)
    else:
        o_ref[0, 0] = (_dot_nt(w2_ref[...], hid) + b2_ref[...]).astype(BF16)


def _compress(halves, which, pos_emb, w1, b1, w2, b2, keys):
    _, B, G, nh, hw = halves.shape
    dh = HEAD_DIM
    hidden = w1.shape[-1]
    pos2 = pos_emb.reshape(2, hw)
    w1s = w1.reshape(2, hw, hidden).astype(BF16)
    tabs = _lane_rope_tables(jnp.arange(nh) * CMP_STRIDE + CMP_BLOCK - 1)
    if keys:
        w2a = jnp.pad(w2, ((0, 0), (0, LANES - dh))).astype(BF16)
        b2a = jnp.pad(b2, (0, LANES - dh)).reshape(1, LANES)
        out_shape, out_block = (B, G, nh, dh), (1, 1, nh, dh)
    else:
        w2a, b2a = w2.T.astype(BF16), b2.reshape(dh, 1)
        out_shape, out_block = (B, G, dh, nh), (1, 1, dh, nh)
    c2 = lambda b, g: (0, 0)
    tab = pl.BlockSpec((nh, LANES), c2)
    return pl.pallas_call(
        functools.partial(_compress_kernel, nh=nh, keys=keys),
        out_shape=jax.ShapeDtypeStruct(out_shape, BF16),
        grid=(B, G),
        in_specs=[
            pl.BlockSpec((1, 1, 1, nh, hw), lambda b, g: (which, b, g, 0, 0)),
            pl.BlockSpec((2, hw), c2),
            pl.BlockSpec((2, hw, hidden), lambda b, g: (0, 0, 0)),
            pl.BlockSpec((1, hidden), c2),
            pl.BlockSpec(w2a.shape, c2),
            pl.BlockSpec(b2a.shape, c2),
            tab, tab, tab,
        ],
        out_specs=pl.BlockSpec(out_block, lambda b, g: (b, g, 0, 0)),
        compiler_params=_cparams(("parallel", "parallel"), 40),
        name="compress_k" if keys else "compress_v",
    )(halves, pos2, w1s, b1.reshape(1, hidden), w2a, b2a, *tabs)


def _qproj_kernel(h_ref, g_ref, wqt_ref, wg_ref, bg_ref, cos_ref, sin_ref, qt_ref, gt_ref):
    xn = _rms(h_ref[0], g_ref[...]).astype(BF16)
    tr = _dot_nt(wqt_ref[...], xn)
    cos, sin = cos_ref[...], sin_ref[...]
    for hd in range(N_HEADS):
        rows = slice(hd * HEAD_DIM, (hd + 1) * HEAD_DIM)
        qt_ref[0, rows, :] = (_rope_rows(tr[rows], cos, sin) * Q_SCALE).astype(BF16)
    gt_ref[0] = jax.nn.sigmoid(_dot(xn, wg_ref[...]) + bg_ref[...])


def _q_proj(h, g, w_qg, b_gate):
    B, S, D = h.shape
    HD = N_HEADS * HEAD_DIM
    ng = N_HEADS * N_BRANCH
    ts = min(SEQ_TILE, S)
    wqt = w_qg[:, :HD].T.astype(BF16)
    wg = jnp.pad(w_qg[:, HD:], ((0, 0), (0, LANES - ng))).astype(BF16)
    bg = jnp.pad(b_gate, (0, LANES - ng)).reshape(1, LANES)
    cos, sin = _rope_tables(jnp.arange(S))
    tab = pl.BlockSpec((ROPE_HALF, ts), lambda b, i: (0, i))
    return pl.pallas_call(
        _qproj_kernel,
        out_shape=(jax.ShapeDtypeStruct((B, HD, S), BF16), jax.ShapeDtypeStruct((B, S, LANES), F32)),
        grid=(B, S // ts),
        in_specs=[
            pl.BlockSpec((1, ts, D), lambda b, i: (b, i, 0)),
            pl.BlockSpec((1, D), lambda b, i: (0, 0)),
            pl.BlockSpec((HD, D), lambda b, i: (0, 0)),
            pl.BlockSpec((D, LANES), lambda b, i: (0, 0)),
            pl.BlockSpec((1, LANES), lambda b, i: (0, 0)),
            tab, tab,
        ],
        out_specs=(pl.BlockSpec((1, HD, ts), lambda b, i: (b, 0, i)),
                   pl.BlockSpec((1, ts, LANES), lambda b, i: (b, i, 0))),
        compiler_params=_cparams(("parallel", "parallel"), 40),
        name="nsa_q_proj",
    )(h, g.reshape(1, D), wqt, wg, bg, cos.T, sin.T)


def _heads_on_lanes(qt):
    return jnp.concatenate([qt[r * HEAD_DIM:(r + 1) * HEAD_DIM] for r in range(HEADS_PER_GROUP)], axis=1)


def _lane_query_pos(i, tq):
    lanes = lax.broadcasted_iota(I32, (1, HEADS_PER_GROUP * tq), 1)
    return i * tq + (lanes & (tq - 1))


def _gated(acc_t, gates_t, g, branch, denom_row):
    row = lax.broadcasted_iota(I32, gates_t.shape, 0)
    scale = jnp.concatenate(
        [jnp.sum(jnp.where(row == (g * HEADS_PER_GROUP + r) * N_BRANCH + branch, gates_t, 0.0), axis=0, keepdims=True)
         for r in range(HEADS_PER_GROUP)], axis=1)
    if denom_row is not None:
        scale = scale / acc_t[denom_row:denom_row + 1]
    return acc_t[0:HEAD_DIM] * scale


def _unstack_heads(o_t, tq):
    per = LANES // HEAD_DIM
    cols = []
    for r in range(0, HEADS_PER_GROUP, per):
        cols.append(jnp.concatenate([o_t[:, (r + k) * tq:(r + k + 1) * tq] for k in range(per)], axis=0).T)
    return jnp.concatenate(cols, axis=-1)


def _cmp_kernel(qt_ref, gt_ref, k_ref, vt_ref, c2s_ref, o_ref, bias_ref, *, tq, nc):
    g = pl.program_id(1)
    i = pl.program_id(2)

    def body(nk, nb):
        s = _dot(k_ref[0, 0, 0:nk, :], _heads_on_lanes(qt_ref[0]))
        t = _lane_query_pos(i, tq)
        n = lax.broadcasted_iota(I32, (nk, 1), 0)
        s = jnp.where(n * CMP_STRIDE + (CMP_BLOCK - 1) <= t, s, -jnp.inf)
        m = jnp.max(s, axis=0, keepdims=True)
        m = jnp.where(m == -jnp.inf, 0.0, m)
        e = jnp.exp2(s - m)
        p = e * (1.0 / jnp.maximum(jnp.sum(e, axis=0, keepdims=True), 1e-30))
        ot = _dot(vt_ref[0, 0, :, 0:nk], p.astype(BF16))
        o_ref[0] = _unstack_heads(_gated(ot, gt_ref[0].T, g, 0, None), tq).astype(BF16)
        ps = p[:, 0:tq]
        for r in range(1, HEADS_PER_GROUP):
            ps = ps + p[:, r * tq:(r + 1) * tq]
        hi = ps.astype(BF16)
        rem = ps - hi.astype(F32)
        mid = rem.astype(BF16)
        lo = (rem - mid.astype(F32)).astype(BF16)
        c2s = c2s_ref[0:nb, 0:nk]
        imp = _dot(c2s, hi) + _dot(c2s, mid) + _dot(c2s, lo)
        cur = (i * tq + lax.broadcasted_iota(I32, (1, tq), 1)) // SEL_BLOCK
        j = lax.broadcasted_iota(I32, (nb, tq), 0)
        valid = j <= cur
        forced = jnp.where(j == 0, 1.0, jnp.where(j == cur, 1.0, jnp.where(j == cur - 1, 1.0, 0.0)))
        sel = jnp.where(valid, forced, 0.0)
        work = jnp.where(valid, jnp.where(forced > 0.0, -jnp.inf, imp), -jnp.inf)
        for _ in range(N_SELECT - N_FORCED):
            mx = jnp.max(work, axis=0, keepdims=True)
            idx = jnp.min(jnp.where(work == mx, j, MAX_SEL_BLOCKS), axis=0, keepdims=True)
            pick = j == idx
            sel = jnp.where(pick, 1.0, sel)
            work = jnp.where(pick, -jnp.inf, work)
        bias_ref[0, 0, 0:nb, :] = jnp.where(valid, jnp.where(sel > 0.0, 0.0, MASK_BIAS), MASK_BIAS).astype(BF16)
        if nb < MAX_SEL_BLOCKS:
            bias_ref[0, 0, nb:, :] = jnp.full((MAX_SEL_BLOCKS - nb, tq), MASK_BIAS, BF16)

    per_class = LANES * CMP_STRIDE // tq
    n_class = -(-(nc * CMP_STRIDE // tq) // per_class)
    for cls in range(n_class):
        nk = min(nc, LANES * (cls + 1))
        nb = min(MAX_SEL_BLOCKS, nk * CMP_STRIDE // SEL_BLOCK)
        pl.when(i // per_class == cls)(functools.partial(body, nk, nb))


def _cmp_attn(qt, gates, kc, vct):
    B, HD, S = qt.shape
    G, dh = N_KV_GROUPS, HEAD_DIM
    gw = HD // G
    nc = kc.shape[2]
    tq = min(Q_TILE, S)
    n0 = jnp.arange(nc)[None, :] * CMP_STRIDE
    j0 = jnp.arange(MAX_SEL_BLOCKS)[:, None] * SEL_BLOCK
    c2s = ((n0 < j0 + SEL_BLOCK) & (n0 + CMP_BLOCK > j0)).astype(BF16)
    return pl.pallas_call(
        functools.partial(_cmp_kernel, tq=tq, nc=nc),
        out_shape=(jax.ShapeDtypeStruct((B, S, HD), BF16), jax.ShapeDtypeStruct((B, G, MAX_SEL_BLOCKS, S), BF16)),
        grid=(B, G, S // tq),
        in_specs=[
            pl.BlockSpec((1, gw, tq), lambda b, g, i: (b, g, i)),
            pl.BlockSpec((1, tq, LANES), lambda b, g, i: (b, i, 0)),
            pl.BlockSpec((1, 1, nc, dh), lambda b, g, i: (b, g, 0, 0)),
            pl.BlockSpec((1, 1, dh, nc), lambda b, g, i: (b, g, 0, 0)),
            pl.BlockSpec((MAX_SEL_BLOCKS, nc), lambda b, g, i: (0, 0)),
        ],
        out_specs=(pl.BlockSpec((1, tq, gw), lambda b, g, i: (b, i, g)),
                   pl.BlockSpec((1, 1, MAX_SEL_BLOCKS, tq), lambda b, g, i: (b, g, 0, i))),
        compiler_params=_cparams(("parallel", "parallel", "parallel"), 40),
        name="nsa_compressed",
    )(qt, gates, kc, vct, c2s)


def _sel_win_kernel(qt_ref, bias_ref, gt_ref, k_ref, vt_ref, kw_ref, vwt_ref, o_ref, qa, m_s, acc, acc_w, s_a, s_b,
                    *, tq, tk):
    g = pl.program_id(1)
    i = pl.program_id(2)
    qt = qt_ref[0]
    bias = bias_ref[0, 0]
    for r in range(HEADS_PER_GROUP):
        cols = slice(r * tq, (r + 1) * tq)
        qa[0:HEAD_DIM, cols] = qt[r * HEAD_DIM:(r + 1) * HEAD_DIM]
        qa[HEAD_DIM:HEAD_DIM + MAX_SEL_BLOCKS, cols] = bias
        qa[HEAD_DIM + MAX_SEL_BLOCKS:, cols] = jnp.zeros((KEY_AUG - HEAD_DIM - MAX_SEL_BLOCKS, tq), BF16)
    m_s[...] = jnp.full(m_s.shape, NEG_BIG, F32)
    acc[...] = jnp.zeros(acc.shape, F32)
    width = HEADS_PER_GROUP * tq

    def qk(jt, buf):
        buf[...] = _dot(k_ref[0, 0, pl.ds(pl.multiple_of(jt * tk, tk), tk), :], qa[...])

    def online_softmax(key0, n_keys, s):
        m_old = m_s[...]
        m_new = jnp.maximum(m_old, jnp.max(s, axis=0, keepdims=True))
        p = jnp.exp2(s - m_new).astype(BF16)
        acc[...] = jnp.exp2(m_old - m_new) * acc[...] + _dot(vt_ref[0, 0, :, pl.ds(key0, n_keys)], p)
        m_s[...] = m_new

    def absorb(jt, buf):
        online_softmax(pl.multiple_of(jt * tk, tk), tk, buf[...])

    key_row = lax.broadcasted_iota(I32, (tq, 1), 0)
    query_col = lax.broadcasted_iota(I32, (1, width), 1) & (tq - 1)

    def own_tile(s):
        return jnp.where(key_row <= query_col, s, NEG_BIG)

    def absorb_last(jt, buf):
        k0 = pl.multiple_of(jt * tk, tk)

        @pl.when(i % 2 == 1)
        def _():
            online_softmax(k0, tq, buf[0:tq, :])
            online_softmax(pl.multiple_of(k0 + tq, tq), tq, own_tile(buf[tq:tk, :]))

        @pl.when(i % 2 == 0)
        def _():
            online_softmax(k0, tq, own_tile(buf[0:tq, :]))

    def pair(u, c):
        qk(2 * u + 1, s_b)
        absorb(2 * u, s_a)
        qk(2 * u + 2, s_a)
        absorb(2 * u + 1, s_b)
        return c

    def window(miss):
        n_keys = (nw + 1 - miss) * tq
        k0 = pl.multiple_of((i - nw + miss) * tq, tq)
        s_b[0:n_keys, :] = _dot(kw_ref[0, 0, pl.ds(k0, n_keys), :], qa[0:HEAD_DIM, :])
        qk(0, s_a)
        parts = []
        for j in range(miss, nw + 1):
            part = s_b[(j - miss) * tq:(j - miss + 1) * tq, :]
            if j == 0:
                part = jnp.where(key_row > query_col, part, NEG_BIG)
            parts.append(own_tile(part) if j == nw else part)
        s = jnp.concatenate(parts, axis=0)
        m = jnp.max(s, axis=0, keepdims=True)
        p = jnp.exp2(s - m).astype(BF16)
        acc_w[...] = _dot(vwt_ref[0, 0, :, pl.ds(k0, n_keys)], p)

    nw = WINDOW // tq
    for miss in range(nw + 1):
        pl.when(jnp.maximum(nw - i, 0) == miss)(functools.partial(window, miss))

    last = (i * tq + tq + tk - 1) // tk - 1
    lax.fori_loop(0, last // 2, pair, 0)

    @pl.when(last % 2 == 1)
    def _():
        qk(last, s_b)
        absorb(last - 1, s_a)
        absorb_last(last, s_b)

    @pl.when(last % 2 == 0)
    def _():
        absorb_last(last, s_a)

    gates_t = gt_ref[0].T
    o_t = _gated(acc[...], gates_t, g, 1, HEAD_DIM) + _gated(acc_w[...], gates_t, g, 2, HEAD_DIM)
    o_ref[0] = _unstack_heads(o_t, tq).astype(BF16)


def _sel_win_attn(qt, bias, gates, ksa, vst, kw, vwt):
    B, HD, S = qt.shape
    G, dh = N_KV_GROUPS, HEAD_DIM
    gw = HD // G
    tq = min(SEL_Q_TILE, S // 2)
    tk = 2 * tq
    assert S % tk == 0 and WINDOW % tq == 0 and WINDOW + tq <= tk
    width = HEADS_PER_GROUP * tq
    return pl.pallas_call(
        functools.partial(_sel_win_kernel, tq=tq, tk=tk),
        out_shape=jax.ShapeDtypeStruct((B, S, HD), BF16),
        grid=(B, G, S // tq),
        in_specs=[
            pl.BlockSpec((1, gw, tq), lambda b, g, i: (b, g, i)),
            pl.BlockSpec((1, 1, MAX_SEL_BLOCKS, tq), lambda b, g, i: (b, g, 0, i)),
            pl.BlockSpec((1, tq, LANES), lambda b, g, i: (b, i, 0)),
            pl.BlockSpec((1, 1, S, KEY_AUG), lambda b, g, i: (b, g, 0, 0)),
            pl.BlockSpec((1, 1, V_ROWS, S), lambda b, g, i: (b, g, 0, 0)),
            pl.BlockSpec((1, 1, S, dh), lambda b, g, i: (b, g, 0, 0)),
            pl.BlockSpec((1, 1, V_ROWS, S), lambda b, g, i: (b, g, 0, 0)),
        ],
        out_specs=pl.BlockSpec((1, tq, gw), lambda b, g, i: (b, i, g)),
        scratch_shapes=[pltpu.VMEM((KEY_AUG, width), BF16), pltpu.VMEM((1, width), F32),
                        pltpu.VMEM((V_ROWS, width), F32), pltpu.VMEM((V_ROWS, width), F32),
                        pltpu.VMEM((tk, width), F32), pltpu.VMEM((tk, width), F32)],
        compiler_params=_cparams(("parallel", "parallel", "arbitrary"), 60),
        name="nsa_selected_window",
    )(qt, bias, gates, ksa, vst, kw, vwt)


def _oproj_kernel(h_ref, a_ref, b_ref, w_ref, g_ref, wh_ref, wl_ref, br_ref, o_ref, info_ref, cnt_ref, *, tt, rt):
    @pl.when(pl.program_id(0) == 0)
    def _():
        cnt_ref[...] = jnp.zeros_like(cnt_ref)

    o = a_ref[...].astype(F32) + b_ref[...].astype(F32)
    h_new = h_ref[...] + _dot(o.astype(BF16), w_ref[...])
    o_ref[...] = h_new
    cnt = cnt_ref[...]
    for k in range(tt // rt):
        rows = slice(k * rt, (k + 1) * rt)
        info_ref[rows, :], cnt = _route(h_new[rows], g_ref, wh_ref, wl_ref, br_ref, cnt)
    cnt_ref[...] = cnt


def _out_proj(h2, oc, osw, w_o, router_ops):
    T, D = h2.shape
    HD = oc.shape[-1]
    tt = min(SEQ_TILE, T)
    rt = min(TOK_TILE, tt)
    blk = lambda w: pl.BlockSpec((tt, w), lambda i: (i, 0))
    const = lambda i: (0, 0)
    return pl.pallas_call(
        functools.partial(_oproj_kernel, tt=tt, rt=rt),
        out_shape=(jax.ShapeDtypeStruct((T, D), F32), jax.ShapeDtypeStruct((T, LANES), F32),
                   jax.ShapeDtypeStruct((1, LANES), F32)),
        grid=(T // tt,),
        in_specs=[blk(D), blk(HD), blk(HD), pl.BlockSpec((HD, D), const)] + _router_specs(D, const),
        out_specs=(blk(D), blk(LANES), pl.BlockSpec((1, LANES), const)),
        compiler_params=_cparams(("arbitrary",), 48),
        name="nsa_out_proj_router",
    )(h2, oc, osw, w_o.astype(BF16), *router_ops)


def _nsa_layer(h, g, w_qg, b_gate, w_o, shared, router_ops):
    B, S, D = h.shape
    kc, vct, ksa, vst, kw, vwt = shared
    qt, gates = _q_proj(h, g, w_qg, b_gate)
    oc, bias = _cmp_attn(qt, gates, kc, vct)
    osw = _sel_win_attn(qt, bias, gates, ksa, vst, kw, vwt)
    flat = lambda a: a.reshape(B * S, a.shape[-1])
    h_new, info, cnt = _out_proj(flat(h), flat(oc), flat(osw), w_o, router_ops)
    return h_new, (info, cnt)


def _shared_kv(h, kv_norm, w_kv, ck, cv):
    B, S, _ = h.shape
    G, dh = N_KV_GROUPS, HEAD_DIM
    assert S % SEL_BLOCK == 0 and N_SELECT <= S // SEL_BLOCK <= MAX_SEL_BLOCKS
    cvals, ksa, vst, kw, vwt = _shared_kv_proj(h, kv_norm, w_kv)
    halves = cvals.reshape(B, S, 2, G, dh).transpose(2, 0, 3, 1, 4).reshape(2, B, G, S // CMP_STRIDE, CMP_STRIDE * dh)
    kc = _compress(halves, 0, *ck, keys=True)
    vct = _compress(halves, 1, *cv, keys=False)
    return kc, vct, ksa, vst, kw, vwt


def kernel(x, p, norm_mix, norm_ffn, norm_ple, pool_w, pool_b, pool_scale, kv_norm, w_kv, cmp_k_pos, cmp_k_w1, cmp_k_b1, cmp_k_w2, cmp_k_b2, cmp_v_pos, cmp_v_w1, cmp_v_b1, cmp_v_w2, cmp_v_b2, w_qg, b_gate, w_o, router_g_w, router_g_b, router_e_w, router_e_b, moe_w1, moe_w3, moe_w2, ple_proj, ple_gate_w, ple_gate_b, final_norm):
    B, S, D = x.shape
    depth = p.shape[0]
    n_a = pool_w.shape[0]
    T = B * S
    h = x
    shared = None
    for i in range(depth):
        if i == n_a:
            shared = _shared_kv(h, kv_norm, w_kv,
                                (cmp_k_pos, cmp_k_w1, cmp_k_b1, cmp_k_w2, cmp_k_b2),
                                (cmp_v_pos, cmp_v_w1, cmp_v_b1, cmp_v_w2, cmp_v_b2))
        router_ops = _router_operands(norm_ffn[i], router_g_w[i], router_g_b[i], router_e_w[i], router_e_b[i])
        if i < n_a:
            h2 = _pool_layer(h, norm_mix[i], pool_w[i], pool_b[i], pool_scale[i]).reshape(T, D)
            routing = _router(h2, router_ops)
        else:
            j = i - n_a
            h2, routing = _nsa_layer(h, norm_mix[i], w_qg[j], b_gate[j], w_o[j], shared, router_ops)
        h = _moe_ple_layer(h2, routing, p.reshape(depth, T, p.shape[-1]), i, norm_ffn[i], moe_w1, moe_w3, moe_w2,
                           norm_ple[i], ple_gate_w[i], ple_gate_b[i], ple_proj[i], final_norm,
                           i == depth - 1).reshape(B, S, D)
    return h
```

```python
import functools

import jax
import jax.numpy as jnp
from jax import lax
from jax.experimental import pallas as pl
from jax.experimental.pallas import tpu as pltpu

F32 = jnp.float32
BF16 = jnp.bfloat16
I32 = jnp.int32

POOL_WINDOWS = (2, 4, 8, 16)
N_HEADS = 16
HEAD_DIM = 64
N_KV_GROUPS = 4
HEADS_PER_GROUP = N_HEADS // N_KV_GROUPS
N_BRANCH = 3
ROPE_DIMS = HEAD_DIM // 4
ROPE_HALF = ROPE_DIMS // 2
ROPE_THETA = 500000.0
CMP_BLOCK = 32
CMP_STRIDE = 16
SEL_BLOCK = 64
N_SELECT = 16
WINDOW = 512
FORCE_BONUS = 1e4
N_FORCED = 3
N_EXPERT_GROUPS = 4
EXPERTS_PER_GROUP = 8
N_EXPERTS = N_EXPERT_GROUPS * EXPERTS_PER_GROUP
TOP_K_IN_GROUP = 2
RMS_EPS = 1e-6

LANES = 128
MAX_SEL_BLOCKS = LANES
MASK_BIAS = -30000.0

LOG2E = 1.4426950408889634
Q_SCALE = HEAD_DIM ** -0.5 * LOG2E
V_ROWS = HEAD_DIM + 16
KEY_AUG = 2 * LANES
NEG_BIG = -1e30

SEQ_TILE = 1024
TOK_TILE = 512
CMB_TILE = 512
MOE_CHUNK = 512
Q_TILE = 1024
SEL_Q_TILE = 512
HALO = 16


def _cparams(sem, vmem_mb):
    return pltpu.CompilerParams(dimension_semantics=sem, vmem_limit_bytes=vmem_mb * 1024 * 1024)


def _rms(x, g):
    return x * lax.rsqrt(jnp.mean(x * x, axis=-1, keepdims=True) + RMS_EPS) * g


def _dot(a, b):
    return jnp.dot(a, b, preferred_element_type=F32)


def _dot_nt(a, b):
    return lax.dot_general(a, b, (((1,), (1,)), ((), ())), preferred_element_type=F32)


def _pool_kernel(h_ref, halo_ref, g_ref, w_ref, b_ref, sc_ref, o_ref, *, ts, cg):
    i = pl.program_id(1)
    x = h_ref[0]
    g = g_ref[...]
    xn = _rms(x, g)
    hn = _rms(halo_ref[0], g)
    hn = jnp.where(i > 0, hn, 0.0)
    ext = jnp.concatenate([hn, xn], axis=0)
    t = i * ts + lax.broadcasted_iota(I32, (ts, 1), 0)
    outs = []
    for gi, w in enumerate(POOL_WINDOWS):
        s = ext[:, gi * cg:(gi + 1) * cg]
        k = 1
        while k < w:
            s = s + pltpu.roll(s, k, axis=0)
            k *= 2
        cnt = jnp.minimum(t + 1, w).astype(F32)
        pooled = s[HALO:] / cnt - xn[:, gi * cg:(gi + 1) * cg]
        outs.append(_dot(pooled.astype(BF16), w_ref[gi]))
    y = jnp.concatenate(outs, axis=-1)
    o_ref[0] = x + (y + b_ref[...]) * sc_ref[...]


def _pool_layer(h, g, w, b, sc):
    B, S, D = h.shape
    ts = min(SEQ_TILE, S)
    cg = D // len(POOL_WINDOWS)
    row = lambda v: v.reshape(1, D)
    return pl.pallas_call(
        functools.partial(_pool_kernel, ts=ts, cg=cg),
        out_shape=jax.ShapeDtypeStruct((B, S, D), F32),
        grid=(B, S // ts),
        in_specs=[
            pl.BlockSpec((1, ts, D), lambda b_, i: (b_, i, 0)),
            pl.BlockSpec((1, HALO, D), lambda b_, i: (b_, jnp.maximum(i * (ts // HALO) - 1, 0), 0)),
            pl.BlockSpec((1, D), lambda b_, i: (0, 0)),
            pl.BlockSpec((len(POOL_WINDOWS), cg, cg), lambda b_, i: (0, 0, 0)),
            pl.BlockSpec((1, D), lambda b_, i: (0, 0)),
            pl.BlockSpec((1, D), lambda b_, i: (0, 0)),
        ],
        out_specs=pl.BlockSpec((1, ts, D), lambda b_, i: (b_, i, 0)),
        compiler_params=_cparams(("parallel", "parallel"), 40),
        name="pool_mixer",
    )(h, h, row(g), w.astype(BF16), row(b), row(sc))


def _route(h, g_ref, wh_ref, wl_ref, b_ref, cnt):
    tt = h.shape[0]
    xn = _rms(h, g_ref[...])
    xh = xn.astype(BF16)
    xl = (xn - xh.astype(F32)).astype(BF16)
    logits = _dot(xh, wh_ref[...]) + (_dot(xh, wl_ref[...]) + _dot(xl, wh_ref[...])) + b_ref[...]
    lane = lax.broadcasted_iota(I32, (tt, LANES), 1)
    neg = -jnp.inf
    gl = jnp.where(lane < N_EXPERT_GROUPS, logits, neg)
    gmax = jnp.max(gl, axis=-1, keepdims=True)
    grp = jnp.min(jnp.where(gl == gmax, lane, LANES), axis=-1, keepdims=True)
    gprob = 1.0 / jnp.sum(jnp.exp(gl - gmax), axis=-1, keepdims=True)
    lo = N_EXPERT_GROUPS + grp * EXPERTS_PER_GROUP
    el = jnp.where(lane >= lo, jnp.where(lane < lo + EXPERTS_PER_GROUP, logits, neg), neg)
    v1 = jnp.max(el, axis=-1, keepdims=True)
    i1 = jnp.min(jnp.where(el == v1, lane, LANES), axis=-1, keepdims=True)
    el2 = jnp.where(lane == i1, neg, el)
    v2 = jnp.max(el2, axis=-1, keepdims=True)
    i2 = jnp.min(jnp.where(el2 == v2, lane, LANES), axis=-1, keepdims=True)
    e2 = jnp.exp(v2 - v1)
    w1 = gprob / (1.0 + e2)
    w2 = gprob * e2 / (1.0 + e2)
    oh1 = lane == i1
    oh2 = lane == i2
    oh = jnp.where(oh1, 1.0, jnp.where(oh2, 1.0, 0.0))
    r_ = lax.broadcasted_iota(I32, (tt, tt), 0)
    c_ = lax.broadcasted_iota(I32, (tt, tt), 1)
    tri = jnp.where(r_ > c_, 1.0, 0.0).astype(BF16)
    tot = _dot(tri, oh.astype(BF16)) + cnt
    r1 = jnp.sum(jnp.where(oh1, tot, 0.0), axis=-1, keepdims=True)
    r2 = jnp.sum(jnp.where(oh2, tot, 0.0), axis=-1, keepdims=True)
    vals = (i1.astype(F32) - N_EXPERT_GROUPS, i2.astype(F32) - N_EXPERT_GROUPS, w1, w2, r1, r2)
    info = jnp.zeros((tt, LANES), F32)
    for k, v in enumerate(vals):
        info = jnp.where(lane == k, v, info)
    return info, cnt + jnp.sum(oh, axis=0, keepdims=True)


def _router_kernel(h_ref, g_ref, wh_ref, wl_ref, b_ref, info_ref, cnt_ref):
    @pl.when(pl.program_id(0) == 0)
    def _():
        cnt_ref[...] = jnp.zeros_like(cnt_ref)

    info_ref[...], cnt_ref[...] = _route(h_ref[...], g_ref, wh_ref, wl_ref, b_ref, cnt_ref[...])


def _router_operands(norm_ffn, rg_w, rg_b, re_w, re_b):
    D = rg_w.shape[0]
    pad = LANES - N_EXPERT_GROUPS - N_EXPERTS
    wr = jnp.concatenate([rg_w, re_w, jnp.zeros((D, pad), F32)], axis=1)
    br = jnp.concatenate([rg_b, re_b, jnp.zeros((pad,), F32)]).reshape(1, LANES)
    wh = wr.astype(BF16)
    return norm_ffn.reshape(1, D), wh, (wr - wh.astype(F32)).astype(BF16), br


def _router_specs(D, const):
    return [pl.BlockSpec((1, D), const), pl.BlockSpec((D, LANES), const), pl.BlockSpec((D, LANES), const),
            pl.BlockSpec((1, LANES), const)]


def _router(h2, router_ops):
    T, D = h2.shape
    tt = min(TOK_TILE, T)
    return pl.pallas_call(
        _router_kernel,
        out_shape=(jax.ShapeDtypeStruct((T, LANES), F32), jax.ShapeDtypeStruct((1, LANES), F32)),
        grid=(T // tt,),
        in_specs=[pl.BlockSpec((tt, D), lambda i: (i, 0))] + _router_specs(D, lambda i: (0, 0)),
        out_specs=(pl.BlockSpec((tt, LANES), lambda i: (i, 0)), pl.BlockSpec((1, LANES), lambda i: (0, 0))),
        compiler_params=_cparams(("arbitrary",), 40),
        name="moe_router",
    )(h2, *router_ops)


def _row_copy(src, s, dst, d, sem):
    return pltpu.make_async_copy(src.at[pl.ds(s, 1)], dst.at[pl.ds(d, 1)], sem)


def _dispatch_kernel(dest_ref, cend_ref, h_ref, g_ref, xs_ref, xn_s, sem, *, tt, n_tok, n_chunks):
    base = pl.program_id(0) * tt

    @pl.when(pl.program_id(0) == 0)
    def _():
        xn_s[...] = jnp.zeros_like(xn_s)

        def tail(e):
            nonempty = cend_ref[e] > (cend_ref[e - 1] if e > 0 else 0)
            row = pl.multiple_of((cend_ref[e] - 1) * tt, tt)
            return nonempty, pltpu.make_async_copy(xn_s, xs_ref.at[pl.ds(row, tt)], sem)

        def unused(c):
            return pltpu.make_async_copy(xn_s, xs_ref.at[pl.ds(pl.multiple_of(c * tt, tt), tt)], sem)

        n_used = cend_ref[N_EXPERTS - 1]
        for e in range(N_EXPERTS):
            nonempty, cp = tail(e)
            pl.when(nonempty)(cp.start)
        lax.fori_loop(n_used, n_chunks, lambda c, z: (unused(c).start(), z)[1], 0)
        for e in range(N_EXPERTS):
            nonempty, cp = tail(e)
            pl.when(nonempty)(cp.wait)
        lax.fori_loop(n_used, n_chunks, lambda c, z: (unused(c).wait(), z)[1], 0)

    xn_s[...] = _rms(h_ref[...], g_ref[...])

    for r in range(tt):
        for k in range(TOP_K_IN_GROUP):
            _row_copy(xn_s, r, xs_ref, dest_ref[k * n_tok + base + r], sem).start(priority=k % 2)
    for k in range(TOP_K_IN_GROUP):
        pltpu.make_async_copy(xn_s, xs_ref.at[pl.ds(0, tt)], sem).wait()


def _dispatch(dest, chunk_end, h2, g, n_rows):
    T, D = h2.shape
    tt = MOE_CHUNK
    assert T % tt == 0
    return pl.pallas_call(
        functools.partial(_dispatch_kernel, tt=tt, n_tok=T, n_chunks=n_rows // tt),
        out_shape=jax.ShapeDtypeStruct((n_rows, D), F32),
        grid_spec=pltpu.PrefetchScalarGridSpec(
            num_scalar_prefetch=2,
            grid=(T // tt,),
            in_specs=[
                pl.BlockSpec((tt, D), lambda i, d, ce: (i, 0)),
                pl.BlockSpec((1, D), lambda i, d, ce: (0, 0)),
            ],
            out_specs=pl.BlockSpec(memory_space=pl.ANY),
            scratch_shapes=[pltpu.VMEM((tt, D), F32), pltpu.SemaphoreType.DMA],
        ),
        compiler_params=_cparams(("arbitrary",), 40),
        name="moe_dispatch",
    )(dest, chunk_end, h2, g.reshape(1, D))


def _expert_kernel(ce_ref, nv_ref, xs_ref, w1_ref, w3_ref, w2_ref, o_ref, w1b, w3b, w2b):
    c = pl.program_id(0)
    e = ce_ref[c]
    prev = ce_ref[jnp.maximum(c - 1, 0)]

    @pl.when(jnp.logical_or(c == 0, e != prev))
    def _():
        w1b[...] = w1_ref[0, 0].astype(BF16)
        w3b[...] = w3_ref[0, 0].astype(BF16)
        w2b[...] = w2_ref[0, 0].astype(BF16)

    @pl.when(c < nv_ref[0])
    def _():
        x = xs_ref[...].astype(BF16)
        a = _dot(x, w1b[...])
        b = _dot(x, w3b[...])
        hc = a * jax.nn.sigmoid(a) * b
        o_ref[...] = _dot(hc.astype(BF16), w2b[...])

    @pl.when(c >= nv_ref[0])
    def _():
        o_ref[...] = jnp.zeros_like(o_ref)


def _experts(chunk_e, n_valid, xs, w1, w3, w2, layer):
    P, D = xs.shape
    F = w1.shape[-1]
    ch = MOE_CHUNK
    rows = lambda c, ce, nv: (jnp.minimum(c, nv[0] - 1), 0)
    wsel = lambda c, ce, nv: (layer, ce[c], 0, 0)
    return pl.pallas_call(
        _expert_kernel,
        out_shape=jax.ShapeDtypeStruct((P, D), F32),
        grid_spec=pltpu.PrefetchScalarGridSpec(
            num_scalar_prefetch=2,
            grid=(P // ch,),
            in_specs=[
                pl.BlockSpec((ch, D), rows),
                pl.BlockSpec((1, 1, D, F), wsel),
                pl.BlockSpec((1, 1, D, F), wsel),
                pl.BlockSpec((1, 1, F, D), wsel),
            ],
            out_specs=pl.BlockSpec((ch, D), lambda c, ce, nv: (c, 0)),
            scratch_shapes=[pltpu.VMEM((D, F), BF16), pltpu.VMEM((D, F), BF16), pltpu.VMEM((F, D), BF16)],
        ),
        compiler_params=_cparams(("arbitrary",), 56),
        name="moe_experts",
    )(chunk_e, n_valid, xs, w1, w3, w2)


def _combine_kernel(dest_ref, h_ref, info_ref, rows_ref, p_ref, g_ref, gw_ref, gb_ref, pw_ref, fn_ref,
                    o_ref, buf_a, buf_b, sem, *, tt, n_tok, n_steps, final):
    i = pl.program_id(0)
    bufs = (buf_a, buf_b)

    def row(tile, sl, k, r):
        return _row_copy(rows_ref, dest_ref[k * n_tok + tile * tt + r], bufs[sl].at[k], r, sem.at[sl])

    def wait_rows(sl):
        for k in range(TOP_K_IN_GROUP):
            pltpu.make_async_copy(rows_ref.at[pl.ds(0, tt)], bufs[sl].at[k], sem.at[sl]).wait()

    @pl.when(i == 0)
    def _():
        def issue(r, c):
            for k in range(TOP_K_IN_GROUP):
                row(0, 0, k, r).start(priority=k % 2)
            return c

        lax.fori_loop(0, tt, issue, 0, unroll=8)

    def step(sl):
        wait_rows(sl)
        nxt = jnp.minimum(i + 1, n_steps - 1)
        for r in range(tt):
            for k in range(TOP_K_IN_GROUP):
                row(nxt, 1 - sl, k, r).start(priority=k % 2)
        info = info_ref[...]
        y = h_ref[...] + info[:, 2:3] * bufs[sl][0] + info[:, 3:4] * bufs[sl][1]
        hn = _rms(y, g_ref[...])
        gate = jax.nn.sigmoid(_dot(hn.astype(BF16), gw_ref[...]) + gb_ref[...])
        out = y + _dot(p_ref[0].astype(BF16), pw_ref[...]) * gate
        if final:
            out = _rms(out, fn_ref[...])
        o_ref[...] = out
        pl.when(i == n_steps - 1)(functools.partial(wait_rows, 1 - sl))

    for parity in range(2):
        pl.when(i % 2 == parity)(functools.partial(step, parity))


def _combine(dest, h2, info, rows, p3, layer, g, gw, gb, pw, fn, final):
    T, D = h2.shape
    PD = p3.shape[-1]
    tt = min(CMB_TILE, T)
    full = lambda i, d: (0, 0)
    return pl.pallas_call(
        functools.partial(_combine_kernel, tt=tt, n_tok=T, n_steps=T // tt, final=final),
        out_shape=jax.ShapeDtypeStruct((T, D), F32),
        grid_spec=pltpu.PrefetchScalarGridSpec(
            num_scalar_prefetch=1,
            grid=(T // tt,),
            in_specs=[
                pl.BlockSpec((tt, D), lambda i, d: (i, 0)),
                pl.BlockSpec((tt, LANES), lambda i, d: (i, 0)),
                pl.BlockSpec(memory_space=pl.ANY),
                pl.BlockSpec((1, tt, PD), lambda i, d: (layer, i, 0)),
                pl.BlockSpec((1, D), full),
                pl.BlockSpec((D, D), full),
                pl.BlockSpec((1, D), full),
                pl.BlockSpec((PD, D), full),
                pl.BlockSpec((1, D), full),
            ],
            out_specs=pl.BlockSpec((tt, D), lambda i, d: (i, 0)),
            scratch_shapes=[pltpu.VMEM((TOP_K_IN_GROUP, tt, D), F32), pltpu.VMEM((TOP_K_IN_GROUP, tt, D), F32),
                            pltpu.SemaphoreType.DMA((2,))],
        ),
        compiler_params=_cparams(("arbitrary",), 40),
        name="moe_combine_ple",
    )(dest, h2, info, rows, p3, g.reshape(1, D), gw.astype(BF16), gb.reshape(1, D), pw.astype(BF16),
      fn.reshape(1, D))


def _moe_ple_layer(h2, routing, p3, layer, norm_ffn, w1, w3, w2, norm_ple, gate_w, gate_b, ple_proj, final_norm, final):
    T, D = h2.shape
    A = T * TOP_K_IN_GROUP
    info, cnt = routing
    counts = cnt[0, N_EXPERT_GROUPS:N_EXPERT_GROUPS + N_EXPERTS].astype(I32)
    n_chunks_e = (counts + MOE_CHUNK - 1) // MOE_CHUNK
    chunk_end = jnp.cumsum(n_chunks_e)
    pstarts = (chunk_end - n_chunks_e) * MOE_CHUNK
    n_chunks = -(-A // MOE_CHUNK) + N_EXPERTS
    n_valid = chunk_end[-1:].astype(I32)
    cidx = jnp.minimum(jnp.arange(n_chunks, dtype=I32), n_valid[0] - 1)
    chunk_e = jnp.sum((chunk_end[None, :] <= cidx[:, None]).astype(I32), axis=1)
    chunk_e = jnp.minimum(chunk_e, N_EXPERTS - 1)
    e_idx = info[:, 0:TOP_K_IN_GROUP].astype(I32)
    rank = info[:, 4:4 + TOP_K_IN_GROUP].astype(I32)
    start = jnp.sum(jnp.where(e_idx[..., None] == jnp.arange(N_EXPERTS, dtype=I32), pstarts, 0), axis=-1)
    dest = (start + rank).T.reshape(A)
    xs = _dispatch(dest, chunk_end.astype(I32), h2, norm_ffn, n_chunks * MOE_CHUNK)
    rows = _experts(chunk_e, n_valid, xs, w1, w3, w2, layer)
    return _combine(dest, h2, info, rows, p3, layer, norm_ple, gate_w, gate_b, ple_proj, final_norm, final)


def _rope_rows(xt, cos, sin):
    x1 = xt[0:ROPE_HALF]
    x2 = xt[ROPE_HALF:ROPE_DIMS]
    return jnp.concatenate([x1 * cos - x2 * sin, x2 * cos + x1 * sin, xt[ROPE_DIMS:]], axis=0)


def _rope_tables(pos):
    inv = jnp.float32(ROPE_THETA) ** (-jnp.arange(ROPE_HALF, dtype=F32) * 2.0 / ROPE_DIMS)
    ang = pos.astype(F32)[:, None] * inv[None, :]
    return jnp.cos(ang), jnp.sin(ang)


def _lane_rope_tables(pos):
    cos, sin = _rope_tables(pos)
    n = pos.shape[0]
    ones = jnp.ones((n, HEAD_DIM - ROPE_DIMS), F32)
    zeros = jnp.zeros((n, HEAD_DIM - ROPE_DIMS), F32)
    zh = jnp.zeros((n, ROPE_HALF), F32)
    two = lambda a: jnp.concatenate([a, a], axis=1)
    return (two(jnp.concatenate([cos, cos, ones], axis=1)), two(jnp.concatenate([zh, sin, zeros], axis=1)),
            two(jnp.concatenate([-sin, zh, zeros], axis=1)))


def _rope_lanes(x, c, s1, s2):
    return x * c + pltpu.roll(x, ROPE_HALF, axis=1) * s1 + pltpu.roll(x, LANES - ROPE_HALF, axis=1) * s2


def _kv_kernel(h_ref, g_ref, wn_ref, wt_ref, c_ref, s1_ref, s2_ref, cv_ref, ksa_ref, vst_ref, kw_ref, vwt_ref, *, ts):
    i = pl.program_id(1)
    kvw = N_KV_GROUPS * HEAD_DIM
    hn = _rms(h_ref[0], g_ref[...]).astype(BF16)
    nat = _dot(hn, wn_ref[...])
    cv_ref[0] = nat[:, 0:2 * kvw]
    tr = _dot_nt(wt_ref[...], hn)
    c, s1, s2 = c_ref[...], s1_ref[...], s2_ref[...]
    roped = [_rope_lanes(nat[:, 2 * kvw + k * LANES:2 * kvw + (k + 1) * LANES], c, s1, s2)
             for k in range(2 * kvw // LANES)]
    lane = lax.broadcasted_iota(I32, (ts, KEY_AUG), 1)
    pos = i * ts + lax.broadcasted_iota(I32, (ts, KEY_AUG), 0)
    onehot = jnp.where(lane - HEAD_DIM == pos // SEL_BLOCK, 1.0, 0.0).astype(BF16)
    ones_row = jnp.where(lax.broadcasted_iota(I32, (V_ROWS - HEAD_DIM, ts), 0) == 0, 1.0, 0.0).astype(BF16)
    per_tile = LANES // HEAD_DIM
    for g in range(N_KV_GROUPS):
        lo = (g % per_tile) * HEAD_DIM
        ksa_ref[0, g] = onehot
        ksa_ref[0, g, :, 0:HEAD_DIM] = roped[g // per_tile][:, lo:lo + HEAD_DIM].astype(BF16)
        kw_ref[0, g] = roped[N_KV_GROUPS // per_tile + g // per_tile][:, lo:lo + HEAD_DIM].astype(BF16)
        for ref, base in ((vst_ref, 0), (vwt_ref, kvw)):
            ref[0, g, 0:HEAD_DIM, :] = tr[base + g * HEAD_DIM:base + (g + 1) * HEAD_DIM].astype(BF16)
            ref[0, g, HEAD_DIM:, :] = ones_row


def _shared_kv_proj(h, kv_norm, w_kv):
    B, S, D = h.shape
    G, dh = N_KV_GROUPS, HEAD_DIM
    kvw = G * dh
    ts = min(SEQ_TILE, S)
    br = lambda k: w_kv[:, k * kvw:(k + 1) * kvw]
    w_nat = jnp.concatenate([br(0), br(1), br(2), br(4)], axis=1).astype(BF16)
    w_tr = jnp.concatenate([br(3), br(5)], axis=1).T.astype(BF16)
    tabs = _lane_rope_tables(jnp.arange(S))
    tab = pl.BlockSpec((ts, LANES), lambda b, i: (i, 0))
    return pl.pallas_call(
        functools.partial(_kv_kernel, ts=ts),
        out_shape=(
            jax.ShapeDtypeStruct((B, S, 2 * kvw), F32),
            jax.ShapeDtypeStruct((B, G, S, KEY_AUG), BF16),
            jax.ShapeDtypeStruct((B, G, V_ROWS, S), BF16),
            jax.ShapeDtypeStruct((B, G, S, dh), BF16),
            jax.ShapeDtypeStruct((B, G, V_ROWS, S), BF16),
        ),
        grid=(B, S // ts),
        in_specs=[
            pl.BlockSpec((1, ts, D), lambda b, i: (b, i, 0)),
            pl.BlockSpec((1, D), lambda b, i: (0, 0)),
            pl.BlockSpec((D, 4 * kvw), lambda b, i: (0, 0)),
            pl.BlockSpec((2 * kvw, D), lambda b, i: (0, 0)),
            tab, tab, tab,
        ],
        out_specs=(
            pl.BlockSpec((1, ts, 2 * kvw), lambda b, i: (b, i, 0)),
            pl.BlockSpec((1, G, ts, KEY_AUG), lambda b, i: (b, 0, i, 0)),
            pl.BlockSpec((1, G, V_ROWS, ts), lambda b, i: (b, 0, 0, i)),
            pl.BlockSpec((1, G, ts, dh), lambda b, i: (b, 0, i, 0)),
            pl.BlockSpec((1, G, V_ROWS, ts), lambda b, i: (b, 0, 0, i)),
        ),
        compiler_params=_cparams(("parallel", "parallel"), 48),
        name="shared_kv_proj",
    )(h, kv_norm.reshape(1, D), w_nat, w_tr, *tabs)


def _compress_kernel(x_ref, pos_ref, w1_ref, b1_ref, w2_ref, b2_ref, c_ref, s1_ref, s2_ref, o_ref, *, nh, keys):
    x = x_ref[0, 0, 0]
    a = _dot((x + pos_ref[0:1]).astype(BF16), w1_ref[0])
    b = _dot((x + pos_ref[1:2]).astype(BF16), w1_ref[1])
    hid = jax.nn.gelu(a + pltpu.roll(b, nh - 1, axis=0) + b1_ref[...]).astype(BF16)
    if keys:
        out = _rope_lanes(_dot(hid, w2_ref[...]) + b2_ref[...], c_ref[...], s1_ref[...], s2_ref[...])
        o_ref[0, 0] = out[:, 0:HEAD_DIM].astype(BF16)
    else:
        o_ref[0, 0] = (_dot_nt(w2_ref[...], hid) + b2_ref[...]).astype(BF16)


def _compress(halves, which, pos_emb, w1, b1, w2, b2, keys):
    _, B, G, nh, hw = halves.shape
    dh = HEAD_DIM
    hidden = w1.shape[-1]
    pos2 = pos_emb.reshape(2, hw)
    w1s = w1.reshape(2, hw, hidden).astype(BF16)
    tabs = _lane_rope_tables(jnp.arange(nh) * CMP_STRIDE + CMP_BLOCK - 1)
    if keys:
        w2a = jnp.pad(w2, ((0, 0), (0, LANES - dh))).astype(BF16)
        b2a = jnp.pad(b2, (0, LANES - dh)).reshape(1, LANES)
        out_shape, out_block = (B, G, nh, dh), (1, 1, nh, dh)
    else:
        w2a, b2a = w2.T.astype(BF16), b2.reshape(dh, 1)
        out_shape, out_block = (B, G, dh, nh), (1, 1, dh, nh)
    c2 = lambda b, g: (0, 0)
    tab = pl.BlockSpec((nh, LANES), c2)
    return pl.pallas_call(
        functools.partial(_compress_kernel, nh=nh, keys=keys),
        out_shape=jax.ShapeDtypeStruct(out_shape, BF16),
        grid=(B, G),
        in_specs=[
            pl.BlockSpec((1, 1, 1, nh, hw), lambda b, g: (which, b, g, 0, 0)),
            pl.BlockSpec((2, hw), c2),
            pl.BlockSpec((2, hw, hidden), lambda b, g: (0, 0, 0)),
            pl.BlockSpec((1, hidden), c2),
            pl.BlockSpec(w2a.shape, c2),
            pl.BlockSpec(b2a.shape, c2),
            tab, tab, tab,
        ],
        out_specs=pl.BlockSpec(out_block, lambda b, g: (b, g, 0, 0)),
        compiler_params=_cparams(("parallel", "parallel"), 40),
        name="compress_k" if keys else "compress_v",
    )(halves, pos2, w1s, b1.reshape(1, hidden), w2a, b2a, *tabs)


def _qproj_kernel(h_ref, g_ref, wqt_ref, wg_ref, bg_ref, cos_ref, sin_ref, qt_ref, gt_ref):
    xn = _rms(h_ref[0], g_ref[...]).astype(BF16)
    tr = _dot_nt(wqt_ref[...], xn)
    cos, sin = cos_ref[...], sin_ref[...]
    for hd in range(N_HEADS):
        rows = slice(hd * HEAD_DIM, (hd + 1) * HEAD_DIM)
        qt_ref[0, rows, :] = (_rope_rows(tr[rows], cos, sin) * Q_SCALE).astype(BF16)
    gt_ref[0] = jax.nn.sigmoid(_dot(xn, wg_ref[...]) + bg_ref[...])


def _q_proj(h, g, w_qg, b_gate):
    B, S, D = h.shape
    HD = N_HEADS * HEAD_DIM
    ng = N_HEADS * N_BRANCH
    ts = min(SEQ_TILE, S)
    wqt = w_qg[:, :HD].T.astype(BF16)
    wg = jnp.pad(w_qg[:, HD:], ((0, 0), (0, LANES - ng))).astype(BF16)
    bg = jnp.pad(b_gate, (0, LANES - ng)).reshape(1, LANES)
    cos, sin = _rope_tables(jnp.arange(S))
    tab = pl.BlockSpec((ROPE_HALF, ts), lambda b, i: (0, i))
    return pl.pallas_call(
        _qproj_kernel,
        out_shape=(jax.ShapeDtypeStruct((B, HD, S), BF16), jax.ShapeDtypeStruct((B, S, LANES), F32)),
        grid=(B, S // ts),
        in_specs=[
            pl.BlockSpec((1, ts, D), lambda b, i: (b, i, 0)),
            pl.BlockSpec((1, D), lambda b, i: (0, 0)),
            pl.BlockSpec((HD, D), lambda b, i: (0, 0)),
            pl.BlockSpec((D, LANES), lambda b, i: (0, 0)),
            pl.BlockSpec((1, LANES), lambda b, i: (0, 0)),
            tab, tab,
        ],
        out_specs=(pl.BlockSpec((1, HD, ts), lambda b, i: (b, 0, i)),
                   pl.BlockSpec((1, ts, LANES), lambda b, i: (b, i, 0))),
        compiler_params=_cparams(("parallel", "parallel"), 40),
        name="nsa_q_proj",
    )(h, g.reshape(1, D), wqt, wg, bg, cos.T, sin.T)


def _heads_on_lanes(qt):
    return jnp.concatenate([qt[r * HEAD_DIM:(r + 1) * HEAD_DIM] for r in range(HEADS_PER_GROUP)], axis=1)


def _lane_query_pos(i, tq):
    lanes = lax.broadcasted_iota(I32, (1, HEADS_PER_GROUP * tq), 1)
    return i * tq + (lanes & (tq - 1))


def _gated(acc_t, gates_t, g, branch, denom_row):
    row = lax.broadcasted_iota(I32, gates_t.shape, 0)
    scale = jnp.concatenate(
        [jnp.sum(jnp.where(row == (g * HEADS_PER_GROUP + r) * N_BRANCH + branch, gates_t, 0.0), axis=0, keepdims=True)
         for r in range(HEADS_PER_GROUP)], axis=1)
    if denom_row is not None:
        scale = scale / acc_t[denom_row:denom_row + 1]
    return acc_t[0:HEAD_DIM] * scale


def _unstack_heads(o_t, tq):
    per = LANES // HEAD_DIM
    cols = []
    for r in range(0, HEADS_PER_GROUP, per):
        cols.append(jnp.concatenate([o_t[:, (r + k) * tq:(r + k + 1) * tq] for k in range(per)], axis=0).T)
    return jnp.concatenate(cols, axis=-1)


def _cmp_kernel(qt_ref, gt_ref, k_ref, vt_ref, c2s_ref, o_ref, bias_ref, *, tq, nc):
    g = pl.program_id(1)
    i = pl.program_id(2)

    def body(nk, nb):
        s = _dot(k_ref[0, 0, 0:nk, :], _heads_on_lanes(qt_ref[0]))
        t = _lane_query_pos(i, tq)
        n = lax.broadcasted_iota(I32, (nk, 1), 0)
        s = jnp.where(n * CMP_STRIDE + (CMP_BLOCK - 1) <= t, s, -jnp.inf)
        m = jnp.max(s, axis=0, keepdims=True)
        m = jnp.where(m == -jnp.inf, 0.0, m)
        e = jnp.exp2(s - m)
        p = e * (1.0 / jnp.maximum(jnp.sum(e, axis=0, keepdims=True), 1e-30))
        ot = _dot(vt_ref[0, 0, :, 0:nk], p.astype(BF16))
        o_ref[0] = _unstack_heads(_gated(ot, gt_ref[0].T, g, 0, None), tq).astype(BF16)
        ps = p[:, 0:tq]
        for r in range(1, HEADS_PER_GROUP):
            ps = ps + p[:, r * tq:(r + 1) * tq]
        hi = ps.astype(BF16)
        rem = ps - hi.astype(F32)
        mid = rem.astype(BF16)
        lo = (rem - mid.astype(F32)).astype(BF16)
        c2s = c2s_ref[0:nb, 0:nk]
        imp = _dot(c2s, hi) + _dot(c2s, mid) + _dot(c2s, lo)
        cur = (i * tq + lax.broadcasted_iota(I32, (1, tq), 1)) // SEL_BLOCK
        j = lax.broadcasted_iota(I32, (nb, tq), 0)
        valid = j <= cur
        forced = jnp.where(j == 0, 1.0, jnp.where(j == cur, 1.0, jnp.where(j == cur - 1, 1.0, 0.0)))
        sel = jnp.where(valid, forced, 0.0)
        work = jnp.where(valid, jnp.where(forced > 0.0, -jnp.inf, imp), -jnp.inf)
        for _ in range(N_SELECT - N_FORCED):
            mx = jnp.max(work, axis=0, keepdims=True)
            idx = jnp.min(jnp.where(work == mx, j, MAX_SEL_BLOCKS), axis=0, keepdims=True)
            pick = j == idx
            sel = jnp.where(pick, 1.0, sel)
            work = jnp.where(pick, -jnp.inf, work)
        bias_ref[0, 0, 0:nb, :] = jnp.where(valid, jnp.where(sel > 0.0, 0.0, MASK_BIAS), MASK_BIAS).astype(BF16)
        if nb < MAX_SEL_BLOCKS:
            bias_ref[0, 0, nb:, :] = jnp.full((MAX_SEL_BLOCKS - nb, tq), MASK_BIAS, BF16)

    per_class = LANES * CMP_STRIDE // tq
    n_class = -(-(nc * CMP_STRIDE // tq) // per_class)
    for cls in range(n_class):
        nk = min(nc, LANES * (cls + 1))
        nb = min(MAX_SEL_BLOCKS, nk * CMP_STRIDE // SEL_BLOCK)
        pl.when(i // per_class == cls)(functools.partial(body, nk, nb))


def _cmp_attn(qt, gates, kc, vct):
    B, HD, S = qt.shape
    G, dh = N_KV_GROUPS, HEAD_DIM
    gw = HD // G
    nc = kc.shape[2]
    tq = min(Q_TILE, S)
    n0 = jnp.arange(nc)[None, :] * CMP_STRIDE
    j0 = jnp.arange(MAX_SEL_BLOCKS)[:, None] * SEL_BLOCK
    c2s = ((n0 < j0 + SEL_BLOCK) & (n0 + CMP_BLOCK > j0)).astype(BF16)
    return pl.pallas_call(
        functools.partial(_cmp_kernel, tq=tq, nc=nc),
        out_shape=(jax.ShapeDtypeStruct((B, S, HD), BF16), jax.ShapeDtypeStruct((B, G, MAX_SEL_BLOCKS, S), BF16)),
        grid=(B, G, S // tq),
        in_specs=[
            pl.BlockSpec((1, gw, tq), lambda b, g, i: (b, g, i)),
            pl.BlockSpec((1, tq, LANES), lambda b, g, i: (b, i, 0)),
            pl.BlockSpec((1, 1, nc, dh), lambda b, g, i: (b, g, 0, 0)),
            pl.BlockSpec((1, 1, dh, nc), lambda b, g, i: (b, g, 0, 0)),
            pl.BlockSpec((MAX_SEL_BLOCKS, nc), lambda b, g, i: (0, 0)),
        ],
        out_specs=(pl.BlockSpec((1, tq, gw), lambda b, g, i: (b, i, g)),
                   pl.BlockSpec((1, 1, MAX_SEL_BLOCKS, tq), lambda b, g, i: (b, g, 0, i))),
        compiler_params=_cparams(("parallel", "parallel", "parallel"), 40),
        name="nsa_compressed",
    )(qt, gates, kc, vct, c2s)


def _sel_win_kernel(qt_ref, bias_ref, gt_ref, k_ref, vt_ref, kw_ref, vwt_ref, o_ref, qa, m_s, acc, acc_w, s_a, s_b,
                    *, tq, tk):
    g = pl.program_id(1)
    i = pl.program_id(2)
    qt = qt_ref[0]
    bias = bias_ref[0, 0]
    for r in range(HEADS_PER_GROUP):
        cols = slice(r * tq, (r + 1) * tq)
        qa[0:HEAD_DIM, cols] = qt[r * HEAD_DIM:(r + 1) * HEAD_DIM]
        qa[HEAD_DIM:HEAD_DIM + MAX_SEL_BLOCKS, cols] = bias
        qa[HEAD_DIM + MAX_SEL_BLOCKS:, cols] = jnp.zeros((KEY_AUG - HEAD_DIM - MAX_SEL_BLOCKS, tq), BF16)
    m_s[...] = jnp.full(m_s.shape, NEG_BIG, F32)
    acc[...] = jnp.zeros(acc.shape, F32)
    width = HEADS_PER_GROUP * tq

    def qk(jt, buf):
        buf[...] = _dot(k_ref[0, 0, pl.ds(pl.multiple_of(jt * tk, tk), tk), :], qa[...])

    def online_softmax(key0, n_keys, s):
        m_old = m_s[...]
        m_new = jnp.maximum(m_old, jnp.max(s, axis=0, keepdims=True))
        p = jnp.exp2(s - m_new).astype(BF16)
        acc[...] = jnp.exp2(m_old - m_new) * acc[...] + _dot(vt_ref[0, 0, :, pl.ds(key0, n_keys)], p)
        m_s[...] = m_new

    def absorb(jt, buf):
        online_softmax(pl.multiple_of(jt * tk, tk), tk, buf[...])

    key_row = lax.broadcasted_iota(I32, (tq, 1), 0)
    query_col = lax.broadcasted_iota(I32, (1, width), 1) & (tq - 1)

    def own_tile(s):
        return jnp.where(key_row <= query_col, s, NEG_BIG)

    def absorb_last(jt, buf):
        k0 = pl.multiple_of(jt * tk, tk)

        @pl.when(i % 2 == 1)
        def _():
            online_softmax(k0, tq, buf[0:tq, :])
            online_softmax(pl.multiple_of(k0 + tq, tq), tq, own_tile(buf[tq:tk, :]))

        @pl.when(i % 2 == 0)
        def _():
            online_softmax(k0, tq, own_tile(buf[0:tq, :]))

    def pair(u, c):
        qk(2 * u + 1, s_b)
        absorb(2 * u, s_a)
        qk(2 * u + 2, s_a)
        absorb(2 * u + 1, s_b)
        return c

    def window(miss):
        n_keys = (nw + 1 - miss) * tq
        k0 = pl.multiple_of((i - nw + miss) * tq, tq)
        s_b[0:n_keys, :] = _dot(kw_ref[0, 0, pl.ds(k0, n_keys), :], qa[0:HEAD_DIM, :])
        qk(0, s_a)
        parts = []
        for j in range(miss, nw + 1):
            part = s_b[(j - miss) * tq:(j - miss + 1) * tq, :]
            if j == 0:
                part = jnp.where(key_row > query_col, part, NEG_BIG)
            parts.append(own_tile(part) if j == nw else part)
        s = jnp.concatenate(parts, axis=0)
        m = jnp.max(s, axis=0, keepdims=True)
        p = jnp.exp2(s - m).astype(BF16)
        acc_w[...] = _dot(vwt_ref[0, 0, :, pl.ds(k0, n_keys)], p)

    nw = WINDOW // tq
    for miss in range(nw + 1):
        pl.when(jnp.maximum(nw - i, 0) == miss)(functools.partial(window, miss))

    last = (i * tq + tq + tk - 1) // tk - 1
    lax.fori_loop(0, last // 2, pair, 0)

    @pl.when(last % 2 == 1)
    def _():
        qk(last, s_b)
        absorb(last - 1, s_a)
        absorb_last(last, s_b)

    @pl.when(last % 2 == 0)
    def _():
        absorb_last(last, s_a)

    gates_t = gt_ref[0].T
    o_t = _gated(acc[...], gates_t, g, 1, HEAD_DIM) + _gated(acc_w[...], gates_t, g, 2, HEAD_DIM)
    o_ref[0] = _unstack_heads(o_t, tq).astype(BF16)


def _sel_win_attn(qt, bias, gates, ksa, vst, kw, vwt):
    B, HD, S = qt.shape
    G, dh = N_KV_GROUPS, HEAD_DIM
    gw = HD // G
    tq = min(SEL_Q_TILE, S // 2)
    tk = 2 * tq
    assert S % tk == 0 and WINDOW % tq == 0 and WINDOW + tq <= tk
    width = HEADS_PER_GROUP * tq
    return pl.pallas_call(
        functools.partial(_sel_win_kernel, tq=tq, tk=tk),
        out_shape=jax.ShapeDtypeStruct((B, S, HD), BF16),
        grid=(B, G, S // tq),
        in_specs=[
            pl.BlockSpec((1, gw, tq), lambda b, g, i: (b, g, i)),
            pl.BlockSpec((1, 1, MAX_SEL_BLOCKS, tq), lambda b, g, i: (b, g, 0, i)),
            pl.BlockSpec((1, tq, LANES), lambda b, g, i: (b, i, 0)),
            pl.BlockSpec((1, 1, S, KEY_AUG), lambda b, g, i: (b, g, 0, 0)),
            pl.BlockSpec((1, 1, V_ROWS, S), lambda b, g, i: (b, g, 0, 0)),
            pl.BlockSpec((1, 1, S, dh), lambda b, g, i: (b, g, 0, 0)),
            pl.BlockSpec((1, 1, V_ROWS, S), lambda b, g, i: (b, g, 0, 0)),
        ],
        out_specs=pl.BlockSpec((1, tq, gw), lambda b, g, i: (b, i, g)),
        scratch_shapes=[pltpu.VMEM((KEY_AUG, width), BF16), pltpu.VMEM((1, width), F32),
                        pltpu.VMEM((V_ROWS, width), F32), pltpu.VMEM((V_ROWS, width), F32),
                        pltpu.VMEM((tk, width), F32), pltpu.VMEM((tk, width), F32)],
        compiler_params=_cparams(("parallel", "parallel", "arbitrary"), 60),
        name="nsa_selected_window",
    )(qt, bias, gates, ksa, vst, kw, vwt)


def _oproj_kernel(h_ref, a_ref, b_ref, w_ref, g_ref, wh_ref, wl_ref, br_ref, o_ref, info_ref, cnt_ref, *, tt, rt):
    @pl.when(pl.program_id(0) == 0)
    def _():
        cnt_ref[...] = jnp.zeros_like(cnt_ref)

    o = a_ref[...].astype(F32) + b_ref[...].astype(F32)
    h_new = h_ref[...] + _dot(o.astype(BF16), w_ref[...])
    o_ref[...] = h_new
    cnt = cnt_ref[...]
    for k in range(tt // rt):
        rows = slice(k * rt, (k + 1) * rt)
        info_ref[rows, :], cnt = _route(h_new[rows], g_ref, wh_ref, wl_ref, br_ref, cnt)
    cnt_ref[...] = cnt


def _out_proj(h2, oc, osw, w_o, router_ops):
    T, D = h2.shape
    HD = oc.shape[-1]
    tt = min(SEQ_TILE, T)
    rt = min(TOK_TILE, tt)
    blk = lambda w: pl.BlockSpec((tt, w), lambda i: (i, 0))
    const = lambda i: (0, 0)
    return pl.pallas_call(
        functools.partial(_oproj_kernel, tt=tt, rt=rt),
        out_shape=(jax.ShapeDtypeStruct((T, D), F32), jax.ShapeDtypeStruct((T, LANES), F32),
                   jax.ShapeDtypeStruct((1, LANES), F32)),
        grid=(T // tt,),
        in_specs=[blk(D), blk(HD), blk(HD), pl.BlockSpec((HD, D), const)] + _router_specs(D, const),
        out_specs=(blk(D), blk(LANES), pl.BlockSpec((1, LANES), const)),
        compiler_params=_cparams(("arbitrary",), 48),
        name="nsa_out_proj_router",
    )(h2, oc, osw, w_o.astype(BF16), *router_ops)


def _nsa_layer(h, g, w_qg, b_gate, w_o, shared, router_ops):
    B, S, D = h.shape
    kc, vct, ksa, vst, kw, vwt = shared
    qt, gates = _q_proj(h, g, w_qg, b_gate)
    oc, bias = _cmp_attn(qt, gates, kc, vct)
    osw = _sel_win_attn(qt, bias, gates, ksa, vst, kw, vwt)
    flat = lambda a: a.reshape(B * S, a.shape[-1])
    h_new, info, cnt = _out_proj(flat(h), flat(oc), flat(osw), w_o, router_ops)
    return h_new, (info, cnt)


def _shared_kv(h, kv_norm, w_kv, ck, cv):
    B, S, _ = h.shape
    G, dh = N_KV_GROUPS, HEAD_DIM
    assert S % SEL_BLOCK == 0 and N_SELECT <= S // SEL_BLOCK <= MAX_SEL_BLOCKS
    cvals, ksa, vst, kw, vwt = _shared_kv_proj(h, kv_norm, w_kv)
    halves = cvals.reshape(B, S, 2, G, dh).transpose(2, 0, 3, 1, 4).reshape(2, B, G, S // CMP_STRIDE, CMP_STRIDE * dh)
    kc = _compress(halves, 0, *ck, keys=True)
    vct = _compress(halves, 1, *cv, keys=False)
    return kc, vct, ksa, vst, kw, vwt


def kernel(x, p, norm_mix, norm_ffn, norm_ple, pool_w, pool_b, pool_scale, kv_norm, w_kv, cmp_k_pos, cmp_k_w1, cmp_k_b1, cmp_k_w2, cmp_k_b2, cmp_v_pos, cmp_v_w1, cmp_v_b1, cmp_v_w2, cmp_v_b2, w_qg, b_gate, w_o, router_g_w, router_g_b, router_e_w, router_e_b, moe_w1, moe_w3, moe_w2, ple_proj, ple_gate_w, ple_gate_b, final_norm):
    B, S, D = x.shape
    depth = p.shape[0]
    n_a = pool_w.shape[0]
    T = B * S
    h = x
    shared = None
    for i in range(depth):
        if i == n_a:
            shared = _shared_kv(h, kv_norm, w_kv,
                                (cmp_k_pos, cmp_k_w1, cmp_k_b1, cmp_k_w2, cmp_k_b2),
                                (cmp_v_pos, cmp_v_w1, cmp_v_b1, cmp_v_w2, cmp_v_b2))
        router_ops = _router_operands(norm_ffn[i], router_g_w[i], router_g_b[i], router_e_w[i], router_e_b[i])
        if i < n_a:
            h2 = _pool_layer(h, norm_mix[i], pool_w[i], pool_b[i], pool_scale[i]).reshape(T, D)
            routing = _router(h2, router_ops)
        else:
            j = i - n_a
            h2, routing = _nsa_layer(h, norm_mix[i], w_qg[j], b_gate[j], w_o[j], shared, router_ops)
        h = _moe_ple_layer(h2, routing, p.reshape(depth, T, p.shape[-1]), i, norm_ffn[i], moe_w1, moe_w3, moe_w2,
                           norm_ple[i], ple_gate_w[i], ple_gate_b[i], ple_proj[i], final_norm,
                           i == depth - 1).reshape(B, S, D)
    return h
```

```python
import functools

import jax
import jax.numpy as jnp
from jax import lax
from jax.experimental import pallas as pl
from jax.experimental.pallas import tpu as pltpu

F32 = jnp.float32
BF16 = jnp.bfloat16
I32 = jnp.int32

POOL_WINDOWS = (2, 4, 8, 16)
N_HEADS = 16
HEAD_DIM = 64
N_KV_GROUPS = 4
HEADS_PER_GROUP = N_HEADS // N_KV_GROUPS
N_BRANCH = 3
ROPE_DIMS = HEAD_DIM // 4
ROPE_HALF = ROPE_DIMS // 2
ROPE_THETA = 500000.0
CMP_BLOCK = 32
CMP_STRIDE = 16
SEL_BLOCK = 64
N_SELECT = 16
WINDOW = 512
FORCE_BONUS = 1e4
N_FORCED = 3
N_EXPERT_GROUPS = 4
EXPERTS_PER_GROUP = 8
N_EXPERTS = N_EXPERT_GROUPS * EXPERTS_PER_GROUP
TOP_K_IN_GROUP = 2
RMS_EPS = 1e-6

LANES = 128
MAX_SEL_BLOCKS = LANES
MASK_BIAS = -30000.0

LOG2E = 1.4426950408889634
Q_SCALE = HEAD_DIM ** -0.5 * LOG2E
V_ROWS = HEAD_DIM + 16
KEY_AUG = 2 * LANES
NEG_BIG = -1e30

SEQ_TILE = 1024
TOK_TILE = 512
CMB_TILE = 512
MOE_CHUNK = 512
Q_TILE = 1024
SEL_Q_TILE = 512
HALO = 16


def _cparams(sem, vmem_mb):
    return pltpu.CompilerParams(dimension_semantics=sem, vmem_limit_bytes=vmem_mb * 1024 * 1024)


def _rms(x, g):
    return x * lax.rsqrt(jnp.mean(x * x, axis=-1, keepdims=True) + RMS_EPS) * g


def _dot(a, b):
    return jnp.dot(a, b, preferred_element_type=F32)


def _dot_nt(a, b):
    return lax.dot_general(a, b, (((1,), (1,)), ((), ())), preferred_element_type=F32)


def _pool_kernel(h_ref, halo_ref, g_ref, w_ref, b_ref, sc_ref, gf_ref, wh_ref, wl_ref, br_ref, o_ref, info_ref, cnt_ref,
                 *, ts, cg, rt):
    i = pl.program_id(1)

    @pl.when(jnp.logical_and(pl.program_id(0) == 0, i == 0))
    def _():
        cnt_ref[...] = jnp.zeros_like(cnt_ref)

    x = h_ref[0]
    g = g_ref[...]
    xn = _rms(x, g)
    hn = _rms(halo_ref[0], g)
    hn = jnp.where(i > 0, hn, 0.0)
    ext = jnp.concatenate([hn, xn], axis=0)
    t = i * ts + lax.broadcasted_iota(I32, (ts, 1), 0)
    outs = []
    for gi, w in enumerate(POOL_WINDOWS):
        s = ext[:, gi * cg:(gi + 1) * cg]
        k = 1
        while k < w:
            s = s + pltpu.roll(s, k, axis=0)
            k *= 2
        cnt = jnp.minimum(t + 1, w).astype(F32)
        pooled = s[HALO:] / cnt - xn[:, gi * cg:(gi + 1) * cg]
        outs.append(_dot(pooled.astype(BF16), w_ref[gi]))
    y = jnp.concatenate(outs, axis=-1)
    h_new = x + (y + b_ref[...]) * sc_ref[...]
    o_ref[0] = h_new
    cnt = cnt_ref[...]
    for k in range(ts // rt):
        rows = slice(k * rt, (k + 1) * rt)
        info_ref[rows, :], cnt = _route(h_new[rows], gf_ref, wh_ref, wl_ref, br_ref, cnt)
    cnt_ref[...] = cnt


def _pool_layer(h, g, w, b, sc, router_ops):
    B, S, D = h.shape
    ts = min(SEQ_TILE, S)
    rt = min(TOK_TILE, ts)
    cg = D // len(POOL_WINDOWS)
    row = lambda v: v.reshape(1, D)
    const = lambda b_, i: (0, 0)
    h_new, info, cnt = pl.pallas_call(
        functools.partial(_pool_kernel, ts=ts, cg=cg, rt=rt),
        out_shape=(jax.ShapeDtypeStruct((B, S, D), F32), jax.ShapeDtypeStruct((B * S, LANES), F32),
                   jax.ShapeDtypeStruct((1, LANES), F32)),
        grid=(B, S // ts),
        in_specs=[
            pl.BlockSpec((1, ts, D), lambda b_, i: (b_, i, 0)),
            pl.BlockSpec((1, HALO, D), lambda b_, i: (b_, jnp.maximum(i * (ts // HALO) - 1, 0), 0)),
            pl.BlockSpec((1, D), const),
            pl.BlockSpec((len(POOL_WINDOWS), cg, cg), lambda b_, i: (0, 0, 0)),
            pl.BlockSpec((1, D), const),
            pl.BlockSpec((1, D), const),
        ] + _router_specs(D, const),
        out_specs=(pl.BlockSpec((1, ts, D), lambda b_, i: (b_, i, 0)),
                   pl.BlockSpec((ts, LANES), lambda b_, i: (b_ * (S // ts) + i, 0)),
                   pl.BlockSpec((1, LANES), const)),
        compiler_params=_cparams(("arbitrary", "arbitrary"), 48),
        name="pool_mixer_router",
    )(h, h, row(g), w.astype(BF16), row(b), row(sc), *router_ops)
    return h_new, (info, cnt)


def _route(h, g_ref, wh_ref, wl_ref, b_ref, cnt):
    tt = h.shape[0]
    xn = _rms(h, g_ref[...])
    xh = xn.astype(BF16)
    xl = (xn - xh.astype(F32)).astype(BF16)
    logits = _dot(xh, wh_ref[...]) + (_dot(xh, wl_ref[...]) + _dot(xl, wh_ref[...])) + b_ref[...]
    lane = lax.broadcasted_iota(I32, (tt, LANES), 1)
    neg = -jnp.inf
    gl = jnp.where(lane < N_EXPERT_GROUPS, logits, neg)
    gmax = jnp.max(gl, axis=-1, keepdims=True)
    grp = jnp.min(jnp.where(gl == gmax, lane, LANES), axis=-1, keepdims=True)
    gprob = 1.0 / jnp.sum(jnp.exp(gl - gmax), axis=-1, keepdims=True)
    lo = N_EXPERT_GROUPS + grp * EXPERTS_PER_GROUP
    el = jnp.where(lane >= lo, jnp.where(lane < lo + EXPERTS_PER_GROUP, logits, neg), neg)
    v1 = jnp.max(el, axis=-1, keepdims=True)
    i1 = jnp.min(jnp.where(el == v1, lane, LANES), axis=-1, keepdims=True)
    el2 = jnp.where(lane == i1, neg, el)
    v2 = jnp.max(el2, axis=-1, keepdims=True)
    i2 = jnp.min(jnp.where(el2 == v2, lane, LANES), axis=-1, keepdims=True)
    e2 = jnp.exp(v2 - v1)
    w1 = gprob / (1.0 + e2)
    w2 = gprob * e2 / (1.0 + e2)
    oh1 = lane == i1
    oh2 = lane == i2
    oh = jnp.where(oh1, 1.0, jnp.where(oh2, 1.0, 0.0))
    r_ = lax.broadcasted_iota(I32, (tt, tt), 0)
    c_ = lax.broadcasted_iota(I32, (tt, tt), 1)
    tri = jnp.where(r_ > c_, 1.0, 0.0).astype(BF16)
    tot = _dot(tri, oh.astype(BF16)) + cnt
    r1 = jnp.sum(jnp.where(oh1, tot, 0.0), axis=-1, keepdims=True)
    r2 = jnp.sum(jnp.where(oh2, tot, 0.0), axis=-1, keepdims=True)
    vals = (i1.astype(F32) - N_EXPERT_GROUPS, i2.astype(F32) - N_EXPERT_GROUPS, w1, w2, r1, r2)
    info = jnp.zeros((tt, LANES), F32)
    for k, v in enumerate(vals):
        info = jnp.where(lane == k, v, info)
    return info, cnt + jnp.sum(oh, axis=0, keepdims=True)


def _router_operands(norm_ffn, rg_w, rg_b, re_w, re_b):
    D = rg_w.shape[0]
    pad = LANES - N_EXPERT_GROUPS - N_EXPERTS
    wr = jnp.concatenate([rg_w, re_w, jnp.zeros((D, pad), F32)], axis=1)
    br = jnp.concatenate([rg_b, re_b, jnp.zeros((pad,), F32)]).reshape(1, LANES)
    wh = wr.astype(BF16)
    return norm_ffn.reshape(1, D), wh, (wr - wh.astype(F32)).astype(BF16), br


def _router_specs(D, const):
    return [pl.BlockSpec((1, D), const), pl.BlockSpec((D, LANES), const), pl.BlockSpec((D, LANES), const),
            pl.BlockSpec((1, LANES), const)]


def _row_copy(src, s, dst, d, sem):
    return pltpu.make_async_copy(src.at[pl.ds(s, 1)], dst.at[pl.ds(d, 1)], sem)


def _dispatch_kernel(dest_ref, cend_ref, h_ref, g_ref, xs_ref, xn_s, sem, *, tt, n_tok, n_chunks):
    base = pl.program_id(0) * tt

    @pl.when(pl.program_id(0) == 0)
    def _():
        xn_s[...] = jnp.zeros_like(xn_s)

        def tail(e):
            nonempty = cend_ref[e] > (cend_ref[e - 1] if e > 0 else 0)
            row = pl.multiple_of((cend_ref[e] - 1) * tt, tt)
            return nonempty, pltpu.make_async_copy(xn_s, xs_ref.at[pl.ds(row, tt)], sem)

        def unused(c):
            return pltpu.make_async_copy(xn_s, xs_ref.at[pl.ds(pl.multiple_of(c * tt, tt), tt)], sem)

        n_used = cend_ref[N_EXPERTS - 1]
        for e in range(N_EXPERTS):
            nonempty, cp = tail(e)
            pl.when(nonempty)(cp.start)
        lax.fori_loop(n_used, n_chunks, lambda c, z: (unused(c).start(), z)[1], 0)
        for e in range(N_EXPERTS):
            nonempty, cp = tail(e)
            pl.when(nonempty)(cp.wait)
        lax.fori_loop(n_used, n_chunks, lambda c, z: (unused(c).wait(), z)[1], 0)

    xn_s[...] = _rms(h_ref[...], g_ref[...])

    for r in range(tt):
        for k in range(TOP_K_IN_GROUP):
            _row_copy(xn_s, r, xs_ref, dest_ref[k * n_tok + base + r], sem).start(priority=k % 2)
    for k in range(TOP_K_IN_GROUP):
        pltpu.make_async_copy(xn_s, xs_ref.at[pl.ds(0, tt)], sem).wait()


def _dispatch(dest, chunk_end, h2, g, n_rows):
    T, D = h2.shape
    tt = MOE_CHUNK
    assert T % tt == 0
    return pl.pallas_call(
        functools.partial(_dispatch_kernel, tt=tt, n_tok=T, n_chunks=n_rows // tt),
        out_shape=jax.ShapeDtypeStruct((n_rows, D), F32),
        grid_spec=pltpu.PrefetchScalarGridSpec(
            num_scalar_prefetch=2,
            grid=(T // tt,),
            in_specs=[
                pl.BlockSpec((tt, D), lambda i, d, ce: (i, 0)),
                pl.BlockSpec((1, D), lambda i, d, ce: (0, 0)),
            ],
            out_specs=pl.BlockSpec(memory_space=pl.ANY),
            scratch_shapes=[pltpu.VMEM((tt, D), F32), pltpu.SemaphoreType.DMA],
        ),
        compiler_params=_cparams(("arbitrary",), 40),
        name="moe_dispatch",
    )(dest, chunk_end, h2, g.reshape(1, D))


def _expert_kernel(ce_ref, nv_ref, xs_ref, w1_ref, w3_ref, w2_ref, o_ref, w1b, w3b, w2b):
    c = pl.program_id(0)
    e = ce_ref[c]
    prev = ce_ref[jnp.maximum(c - 1, 0)]

    @pl.when(jnp.logical_or(c == 0, e != prev))
    def _():
        w1b[...] = w1_ref[0, 0].astype(BF16)
        w3b[...] = w3_ref[0, 0].astype(BF16)
        w2b[...] = w2_ref[0, 0].astype(BF16)

    @pl.when(c < nv_ref[0])
    def _():
        x = xs_ref[...].astype(BF16)
        a = _dot(x, w1b[...])
        b = _dot(x, w3b[...])
        hc = a * jax.nn.sigmoid(a) * b
        o_ref[...] = _dot(hc.astype(BF16), w2b[...])

    @pl.when(c >= nv_ref[0])
    def _():
        o_ref[...] = jnp.zeros_like(o_ref)


def _experts(chunk_e, n_valid, xs, w1, w3, w2, layer):
    P, D = xs.shape
    F = w1.shape[-1]
    ch = MOE_CHUNK
    rows = lambda c, ce, nv: (jnp.minimum(c, nv[0] - 1), 0)
    wsel = lambda c, ce, nv: (layer, ce[c], 0, 0)
    return pl.pallas_call(
        _expert_kernel,
        out_shape=jax.ShapeDtypeStruct((P, D), F32),
        grid_spec=pltpu.PrefetchScalarGridSpec(
            num_scalar_prefetch=2,
            grid=(P // ch,),
            in_specs=[
                pl.BlockSpec((ch, D), rows),
                pl.BlockSpec((1, 1, D, F), wsel),
                pl.BlockSpec((1, 1, D, F), wsel),
                pl.BlockSpec((1, 1, F, D), wsel),
            ],
            out_specs=pl.BlockSpec((ch, D), lambda c, ce, nv: (c, 0)),
            scratch_shapes=[pltpu.VMEM((D, F), BF16), pltpu.VMEM((D, F), BF16), pltpu.VMEM((F, D), BF16)],
        ),
        compiler_params=_cparams(("arbitrary",), 56),
        name="moe_experts",
    )(chunk_e, n_valid, xs, w1, w3, w2)


def _combine_kernel(dest_ref, h_ref, info_ref, rows_ref, p_ref, g_ref, gw_ref, gb_ref, pw_ref, fn_ref,
                    o_ref, buf_a, buf_b, sem, *, tt, n_tok, n_steps, final):
    i = pl.program_id(0)
    bufs = (buf_a, buf_b)

    def row(tile, sl, k, r):
        return _row_copy(rows_ref, dest_ref[k * n_tok + tile * tt + r], bufs[sl].at[k], r, sem.at[sl])

    def wait_rows(sl):
        for k in range(TOP_K_IN_GROUP):
            pltpu.make_async_copy(rows_ref.at[pl.ds(0, tt)], bufs[sl].at[k], sem.at[sl]).wait()

    @pl.when(i == 0)
    def _():
        def issue(r, c):
            for k in range(TOP_K_IN_GROUP):
                row(0, 0, k, r).start(priority=k % 2)
            return c

        lax.fori_loop(0, tt, issue, 0, unroll=8)

    def step(sl):
        wait_rows(sl)
        nxt = jnp.minimum(i + 1, n_steps - 1)
        for r in range(tt):
            for k in range(TOP_K_IN_GROUP):
                row(nxt, 1 - sl, k, r).start(priority=k % 2)
        info = info_ref[...]
        y = h_ref[...] + info[:, 2:3] * bufs[sl][0] + info[:, 3:4] * bufs[sl][1]
        hn = _rms(y, g_ref[...])
        gate = jax.nn.sigmoid(_dot(hn.astype(BF16), gw_ref[...]) + gb_ref[...])
        out = y + _dot(p_ref[0].astype(BF16), pw_ref[...]) * gate
        if final:
            out = _rms(out, fn_ref[...])
        o_ref[...] = out
        pl.when(i == n_steps - 1)(functools.partial(wait_rows, 1 - sl))

    for parity in range(2):
        pl.when(i % 2 == parity)(functools.partial(step, parity))


def _combine(dest, h2, info, rows, p3, layer, g, gw, gb, pw, fn, final):
    T, D = h2.shape
    PD = p3.shape[-1]
    tt = min(CMB_TILE, T)
    full = lambda i, d: (0, 0)
    return pl.pallas_call(
        functools.partial(_combine_kernel, tt=tt, n_tok=T, n_steps=T // tt, final=final),
        out_shape=jax.ShapeDtypeStruct((T, D), F32),
        grid_spec=pltpu.PrefetchScalarGridSpec(
            num_scalar_prefetch=1,
            grid=(T // tt,),
            in_specs=[
                pl.BlockSpec((tt, D), lambda i, d: (i, 0)),
                pl.BlockSpec((tt, LANES), lambda i, d: (i, 0)),
                pl.BlockSpec(memory_space=pl.ANY),
                pl.BlockSpec((1, tt, PD), lambda i, d: (layer, i, 0)),
                pl.BlockSpec((1, D), full),
                pl.BlockSpec((D, D), full),
                pl.BlockSpec((1, D), full),
                pl.BlockSpec((PD, D), full),
                pl.BlockSpec((1, D), full),
            ],
            out_specs=pl.BlockSpec((tt, D), lambda i, d: (i, 0)),
            scratch_shapes=[pltpu.VMEM((TOP_K_IN_GROUP, tt, D), F32), pltpu.VMEM((TOP_K_IN_GROUP, tt, D), F32),
                            pltpu.SemaphoreType.DMA((2,))],
        ),
        compiler_params=_cparams(("arbitrary",), 40),
        name="moe_combine_ple",
    )(dest, h2, info, rows, p3, g.reshape(1, D), gw.astype(BF16), gb.reshape(1, D), pw.astype(BF16),
      fn.reshape(1, D))


def _moe_ple_layer(h2, routing, p3, layer, norm_ffn, w1, w3, w2, norm_ple, gate_w, gate_b, ple_proj, final_norm, final):
    T, D = h2.shape
    A = T * TOP_K_IN_GROUP
    info, cnt = routing
    counts = cnt[0, N_EXPERT_GROUPS:N_EXPERT_GROUPS + N_EXPERTS].astype(I32)
    n_chunks_e = (counts + MOE_CHUNK - 1) // MOE_CHUNK
    chunk_end = jnp.cumsum(n_chunks_e)
    pstarts = (chunk_end - n_chunks_e) * MOE_CHUNK
    n_chunks = -(-A // MOE_CHUNK) + N_EXPERTS
    n_valid = chunk_end[-1:].astype(I32)
    cidx = jnp.minimum(jnp.arange(n_chunks, dtype=I32), n_valid[0] - 1)
    chunk_e = jnp.sum((chunk_end[None, :] <= cidx[:, None]).astype(I32), axis=1)
    chunk_e = jnp.minimum(chunk_e, N_EXPERTS - 1)
    e_idx = info[:, 0:TOP_K_IN_GROUP].astype(I32)
    rank = info[:, 4:4 + TOP_K_IN_GROUP].astype(I32)
    start = jnp.sum(jnp.where(e_idx[..., None] == jnp.arange(N_EXPERTS, dtype=I32), pstarts, 0), axis=-1)
    dest = (start + rank).T.reshape(A)
    xs = _dispatch(dest, chunk_end.astype(I32), h2, norm_ffn, n_chunks * MOE_CHUNK)
    rows = _experts(chunk_e, n_valid, xs, w1, w3, w2, layer)
    return _combine(dest, h2, info, rows, p3, layer, norm_ple, gate_w, gate_b, ple_proj, final_norm, final)


def _rope_rows(xt, cos, sin):
    x1 = xt[0:ROPE_HALF]
    x2 = xt[ROPE_HALF:ROPE_DIMS]
    return jnp.concatenate([x1 * cos - x2 * sin, x2 * cos + x1 * sin, xt[ROPE_DIMS:]], axis=0)


def _rope_tables(pos):
    inv = jnp.float32(ROPE_THETA) ** (-jnp.arange(ROPE_HALF, dtype=F32) * 2.0 / ROPE_DIMS)
    ang = pos.astype(F32)[:, None] * inv[None, :]
    return jnp.cos(ang), jnp.sin(ang)


def _lane_rope_tables(pos):
    cos, sin = _rope_tables(pos)
    n = pos.shape[0]
    ones = jnp.ones((n, HEAD_DIM - ROPE_DIMS), F32)
    zeros = jnp.zeros((n, HEAD_DIM - ROPE_DIMS), F32)
    zh = jnp.zeros((n, ROPE_HALF), F32)
    two = lambda a: jnp.concatenate([a, a], axis=1)
    return (two(jnp.concatenate([cos, cos, ones], axis=1)), two(jnp.concatenate([zh, sin, zeros], axis=1)),
            two(jnp.concatenate([-sin, zh, zeros], axis=1)))


def _rope_lanes(x, c, s1, s2):
    return x * c + pltpu.roll(x, ROPE_HALF, axis=1) * s1 + pltpu.roll(x, LANES - ROPE_HALF, axis=1) * s2


def _kv_kernel(h_ref, g_ref, wn_ref, wt_ref, c_ref, s1_ref, s2_ref, cv_ref, ksa_ref, vst_ref, kw_ref, vwt_ref, *, ts):
    i = pl.program_id(1)
    kvw = N_KV_GROUPS * HEAD_DIM
    hn = _rms(h_ref[0], g_ref[...]).astype(BF16)
    nat = _dot(hn, wn_ref[...])
    cv_ref[0] = nat[:, 0:2 * kvw]
    tr = _dot_nt(wt_ref[...], hn)
    c, s1, s2 = c_ref[...], s1_ref[...], s2_ref[...]
    roped = [_rope_lanes(nat[:, 2 * kvw + k * LANES:2 * kvw + (k + 1) * LANES], c, s1, s2)
             for k in range(2 * kvw // LANES)]
    lane = lax.broadcasted_iota(I32, (ts, KEY_AUG), 1)
    pos = i * ts + lax.broadcasted_iota(I32, (ts, KEY_AUG), 0)
    onehot = jnp.where(lane - HEAD_DIM == pos // SEL_BLOCK, 1.0, 0.0).astype(BF16)
    ones_row = jnp.where(lax.broadcasted_iota(I32, (V_ROWS - HEAD_DIM, ts), 0) == 0, 1.0, 0.0).astype(BF16)
    per_tile = LANES // HEAD_DIM
    for g in range(N_KV_GROUPS):
        lo = (g % per_tile) * HEAD_DIM
        ksa_ref[0, g] = onehot
        ksa_ref[0, g, :, 0:HEAD_DIM] = roped[g // per_tile][:, lo:lo + HEAD_DIM].astype(BF16)
        kw_ref[0, g] = roped[N_KV_GROUPS // per_tile + g // per_tile][:, lo:lo + HEAD_DIM].astype(BF16)
        for ref, base in ((vst_ref, 0), (vwt_ref, kvw)):
            ref[0, g, 0:HEAD_DIM, :] = tr[base + g * HEAD_DIM:base + (g + 1) * HEAD_DIM].astype(BF16)
            ref[0, g, HEAD_DIM:, :] = ones_row


def _shared_kv_proj(h, kv_norm, w_kv):
    B, S, D = h.shape
    G, dh = N_KV_GROUPS, HEAD_DIM
    kvw = G * dh
    ts = min(SEQ_TILE, S)
    br = lambda k: w_kv[:, k * kvw:(k + 1) * kvw]
    w_nat = jnp.concatenate([br(0), br(1), br(2), br(4)], axis=1).astype(BF16)
    w_tr = jnp.concatenate([br(3), br(5)], axis=1).T.astype(BF16)
    tabs = _lane_rope_tables(jnp.arange(S))
    tab = pl.BlockSpec((ts, LANES), lambda b, i: (i, 0))
    return pl.pallas_call(
        functools.partial(_kv_kernel, ts=ts),
        out_shape=(
            jax.ShapeDtypeStruct((B, S, 2 * kvw), F32),
            jax.ShapeDtypeStruct((B, G, S, KEY_AUG), BF16),
            jax.ShapeDtypeStruct((B, G, V_ROWS, S), BF16),
            jax.ShapeDtypeStruct((B, G, S, dh), BF16),
            jax.ShapeDtypeStruct((B, G, V_ROWS, S), BF16),
        ),
        grid=(B, S // ts),
        in_specs=[
            pl.BlockSpec((1, ts, D), lambda b, i: (b, i, 0)),
            pl.BlockSpec((1, D), lambda b, i: (0, 0)),
            pl.BlockSpec((D, 4 * kvw), lambda b, i: (0, 0)),
            pl.BlockSpec((2 * kvw, D), lambda b, i: (0, 0)),
            tab, tab, tab,
        ],
        out_specs=(
            pl.BlockSpec((1, ts, 2 * kvw), lambda b, i: (b, i, 0)),
            pl.BlockSpec((1, G, ts, KEY_AUG), lambda b, i: (b, 0, i, 0)),
            pl.BlockSpec((1, G, V_ROWS, ts), lambda b, i: (b, 0, 0, i)),
            pl.BlockSpec((1, G, ts, dh), lambda b, i: (b, 0, i, 0)),
            pl.BlockSpec((1, G, V_ROWS, ts), lambda b, i: (b, 0, 0, i)),
        ),
        compiler_params=_cparams(("parallel", "parallel"), 48),
        name="shared_kv_proj",
    )(h, kv_norm.reshape(1, D), w_nat, w_tr, *tabs)


def _compress_kernel(x_ref, pos_ref, w1_ref, b1_ref, w2_ref, b2_ref, c_ref, s1_ref, s2_ref, o_ref, *, nh, keys):
    x = x_ref[0, 0, 0]
    a = _dot((x + pos_ref[0:1]).astype(BF16), w1_ref[0])
    b = _dot((x + pos_ref[1:2]).astype(BF16), w1_ref[1])
    hid = jax.nn.gelu(a + pltpu.roll(b, nh - 1, axis=0) + b1_ref[...]).astype(BF16)
    if keys:
        out = _rope_lanes(_dot(hid, w2_ref[...]) + b2_ref[...], c_ref[...], s1_ref[...], s2_ref[...])
        o_ref[0, 0] = out[:, 0:HEAD_DIM].astype(BF16)
    else:
        o_ref[0, 0] = (_dot_nt(w2_ref[...], hid) + b2_ref[...]).astype(BF16)


def _compress(halves, which, pos_emb, w1, b1, w2, b2, keys):
    _, B, G, nh, hw = halves.shape
    dh = HEAD_DIM
    hidden = w1.shape[-1]
    pos2 = pos_emb.reshape(2, hw)
    w1s = w1.reshape(2, hw, hidden).astype(BF16)
    tabs = _lane_rope_tables(jnp.arange(nh) * CMP_STRIDE + CMP_BLOCK - 1)
    if keys:
        w2a = jnp.pad(w2, ((0, 0), (0, LANES - dh))).astype(BF16)
        b2a = jnp.pad(b2, (0, LANES - dh)).reshape(1, LANES)
        out_shape, out_block = (B, G, nh, dh), (1, 1, nh, dh)
    else:
        w2a, b2a = w2.T.astype(BF16), b2.reshape(dh, 1)
        out_shape, out_block = (B, G, dh, nh), (1, 1, dh, nh)
    c2 = lambda b, g: (0, 0)
    tab = pl.BlockSpec((nh, LANES), c2)
    return pl.pallas_call(
        functools.partial(_compress_kernel, nh=nh, keys=keys),
        out_shape=jax.ShapeDtypeStruct(out_shape, BF16),
        grid=(B, G),
        in_specs=[
            pl.BlockSpec((1, 1, 1, nh, hw), lambda b, g: (which, b, g, 0, 0)),
            pl.BlockSpec((2, hw), c2),
            pl.BlockSpec((2, hw, hidden), lambda b, g: (0, 0, 0)),
            pl.BlockSpec((1, hidden), c2),
            pl.BlockSpec(w2a.shape, c2),
            pl.BlockSpec(b2a.shape, c2),
            tab, tab, tab,
        ],
        out_specs=pl.BlockSpec(out_block, lambda b, g: (b, g, 0, 0)),
        compiler_params=_cparams(("parallel", "parallel"), 40),
        name="compress_k" if keys else "compress_v",
    )(halves, pos2, w1s, b1.reshape(1, hidden), w2a, b2a, *tabs)


def _qproj_kernel(h_ref, g_ref, wqt_ref, wg_ref, bg_ref, cos_ref, sin_ref, qt_ref, gt_ref):
    xn = _rms(h_ref[0], g_ref[...]).astype(BF16)
    tr = _dot_nt(wqt_ref[...], xn)
    cos, sin = cos_ref[...], sin_ref[...]
    for hd in range(N_HEADS):
        rows = slice(hd * HEAD_DIM, (hd + 1) * HEAD_DIM)
        qt_ref[0, rows, :] = (_rope_rows(tr[rows], cos, sin) * Q_SCALE).astype(BF16)
    gt_ref[0] = jax.nn.sigmoid(_dot(xn, wg_ref[...]) + bg_ref[...])


def _q_proj(h, g, w_qg, b_gate):
    B, S, D = h.shape
    HD = N_HEADS * HEAD_DIM
    ng = N_HEADS * N_BRANCH
    ts = min(SEQ_TILE, S)
    wqt = w_qg[:, :HD].T.astype(BF16)
    wg = jnp.pad(w_qg[:, HD:], ((0, 0), (0, LANES - ng))).astype(BF16)
    bg = jnp.pad(b_gate, (0, LANES - ng)).reshape(1, LANES)
    cos, sin = _rope_tables(jnp.arange(S))
    tab = pl.BlockSpec((ROPE_HALF, ts), lambda b, i: (0, i))
    return pl.pallas_call(
        _qproj_kernel,
        out_shape=(jax.ShapeDtypeStruct((B, HD, S), BF16), jax.ShapeDtypeStruct((B, S, LANES), F32)),
        grid=(B, S // ts),
        in_specs=[
            pl.BlockSpec((1, ts, D), lambda b, i: (b, i, 0)),
            pl.BlockSpec((1, D), lambda b, i: (0, 0)),
            pl.BlockSpec((HD, D), lambda b, i: (0, 0)),
            pl.BlockSpec((D, LANES), lambda b, i: (0, 0)),
            pl.BlockSpec((1, LANES), lambda b, i: (0, 0)),
            tab, tab,
        ],
        out_specs=(pl.BlockSpec((1, HD, ts), lambda b, i: (b, 0, i)),
                   pl.BlockSpec((1, ts, LANES), lambda b, i: (b, i, 0))),
        compiler_params=_cparams(("parallel", "parallel"), 40),
        name="nsa_q_proj",
    )(h, g.reshape(1, D), wqt, wg, bg, cos.T, sin.T)


def _heads_on_lanes(qt):
    return jnp.concatenate([qt[r * HEAD_DIM:(r + 1) * HEAD_DIM] for r in range(HEADS_PER_GROUP)], axis=1)


def _lane_query_pos(i, tq):
    lanes = lax.broadcasted_iota(I32, (1, HEADS_PER_GROUP * tq), 1)
    return i * tq + (lanes & (tq - 1))


def _gated(acc_t, gates_t, g, branch, denom_row):
    row = lax.broadcasted_iota(I32, gates_t.shape, 0)
    scale = jnp.concatenate(
        [jnp.sum(jnp.where(row == (g * HEADS_PER_GROUP + r) * N_BRANCH + branch, gates_t, 0.0), axis=0, keepdims=True)
         for r in range(HEADS_PER_GROUP)], axis=1)
    if denom_row is not None:
        scale = scale / acc_t[denom_row:denom_row + 1]
    return acc_t[0:HEAD_DIM] * scale


def _unstack_heads(o_t, tq):
    per = LANES // HEAD_DIM
    cols = []
    for r in range(0, HEADS_PER_GROUP, per):
        cols.append(jnp.concatenate([o_t[:, (r + k) * tq:(r + k + 1) * tq] for k in range(per)], axis=0).T)
    return jnp.concatenate(cols, axis=-1)


def _cmp_kernel(qt_ref, gt_ref, k_ref, vt_ref, c2s_ref, o_ref, bias_ref, *, tq, nc):
    g = pl.program_id(1)
    i = pl.program_id(2)

    def body(nk, nb):
        s = _dot(k_ref[0, 0, 0:nk, :], _heads_on_lanes(qt_ref[0]))
        t = _lane_query_pos(i, tq)
        n = lax.broadcasted_iota(I32, (nk, 1), 0)
        s = jnp.where(n * CMP_STRIDE + (CMP_BLOCK - 1) <= t, s, -jnp.inf)
        m = jnp.max(s, axis=0, keepdims=True)
        m = jnp.where(m == -jnp.inf, 0.0, m)
        e = jnp.exp2(s - m)
        p = e * (1.0 / jnp.maximum(jnp.sum(e, axis=0, keepdims=True), 1e-30))
        ot = _dot(vt_ref[0, 0, :, 0:nk], p.astype(BF16))
        o_ref[0] = _unstack_heads(_gated(ot, gt_ref[0].T, g, 0, None), tq).astype(BF16)
        ps = p[:, 0:tq]
        for r in range(1, HEADS_PER_GROUP):
            ps = ps + p[:, r * tq:(r + 1) * tq]
        hi = ps.astype(BF16)
        rem = ps - hi.astype(F32)
        mid = rem.astype(BF16)
        lo = (rem - mid.astype(F32)).astype(BF16)
        c2s = c2s_ref[0:nb, 0:nk]
        imp = _dot(c2s, hi) + _dot(c2s, mid) + _dot(c2s, lo)
        cur = (i * tq + lax.broadcasted_iota(I32, (1, tq), 1)) // SEL_BLOCK
        j = lax.broadcasted_iota(I32, (nb, tq), 0)
        valid = j <= cur
        forced = jnp.where(j == 0, 1.0, jnp.where(j == cur, 1.0, jnp.where(j == cur - 1, 1.0, 0.0)))
        sel = jnp.where(valid, forced, 0.0)
        work = jnp.where(valid, jnp.where(forced > 0.0, -jnp.inf, imp), -jnp.inf)
        for _ in range(N_SELECT - N_FORCED):
            mx = jnp.max(work, axis=0, keepdims=True)
            idx = jnp.min(jnp.where(work == mx, j, MAX_SEL_BLOCKS), axis=0, keepdims=True)
            pick = j == idx
            sel = jnp.where(pick, 1.0, sel)
            work = jnp.where(pick, -jnp.inf, work)
        bias_ref[0, 0, 0:nb, :] = jnp.where(valid, jnp.where(sel > 0.0, 0.0, MASK_BIAS), MASK_BIAS).astype(BF16)
        if nb < MAX_SEL_BLOCKS:
            bias_ref[0, 0, nb:, :] = jnp.full((MAX_SEL_BLOCKS - nb, tq), MASK_BIAS, BF16)

    per_class = LANES * CMP_STRIDE // tq
    n_class = -(-(nc * CMP_STRIDE // tq) // per_class)
    for cls in range(n_class):
        nk = min(nc, LANES * (cls + 1))
        nb = min(MAX_SEL_BLOCKS, nk * CMP_STRIDE // SEL_BLOCK)
        pl.when(i // per_class == cls)(functools.partial(body, nk, nb))


def _cmp_attn(qt, gates, kc, vct):
    B, HD, S = qt.shape
    G, dh = N_KV_GROUPS, HEAD_DIM
    gw = HD // G
    nc = kc.shape[2]
    tq = min(Q_TILE, S)
    n0 = jnp.arange(nc)[None, :] * CMP_STRIDE
    j0 = jnp.arange(MAX_SEL_BLOCKS)[:, None] * SEL_BLOCK
    c2s = ((n0 < j0 + SEL_BLOCK) & (n0 + CMP_BLOCK > j0)).astype(BF16)
    return pl.pallas_call(
        functools.partial(_cmp_kernel, tq=tq, nc=nc),
        out_shape=(jax.ShapeDtypeStruct((B, S, HD), BF16), jax.ShapeDtypeStruct((B, G, MAX_SEL_BLOCKS, S), BF16)),
        grid=(B, G, S // tq),
        in_specs=[
            pl.BlockSpec((1, gw, tq), lambda b, g, i: (b, g, i)),
            pl.BlockSpec((1, tq, LANES), lambda b, g, i: (b, i, 0)),
            pl.BlockSpec((1, 1, nc, dh), lambda b, g, i: (b, g, 0, 0)),
            pl.BlockSpec((1, 1, dh, nc), lambda b, g, i: (b, g, 0, 0)),
            pl.BlockSpec((MAX_SEL_BLOCKS, nc), lambda b, g, i: (0, 0)),
        ],
        out_specs=(pl.BlockSpec((1, tq, gw), lambda b, g, i: (b, i, g)),
                   pl.BlockSpec((1, 1, MAX_SEL_BLOCKS, tq), lambda b, g, i: (b, g, 0, i))),
        compiler_params=_cparams(("parallel", "parallel", "parallel"), 40),
        name="nsa_compressed",
    )(qt, gates, kc, vct, c2s)


def _sel_win_kernel(qt_ref, bias_ref, gt_ref, k_ref, vt_ref, kw_ref, vwt_ref, o_ref, qa, m_s, acc, acc_w, s_a, s_b,
                    *, tq, tk):
    g = pl.program_id(1)
    i = pl.program_id(2)
    qt = qt_ref[0]
    bias = bias_ref[0, 0]
    for r in range(HEADS_PER_GROUP):
        cols = slice(r * tq, (r + 1) * tq)
        qa[0:HEAD_DIM, cols] = qt[r * HEAD_DIM:(r + 1) * HEAD_DIM]
        qa[HEAD_DIM:HEAD_DIM + MAX_SEL_BLOCKS, cols] = bias
        qa[HEAD_DIM + MAX_SEL_BLOCKS:, cols] = jnp.zeros((KEY_AUG - HEAD_DIM - MAX_SEL_BLOCKS, tq), BF16)
    m_s[...] = jnp.full(m_s.shape, NEG_BIG, F32)
    acc[...] = jnp.zeros(acc.shape, F32)
    width = HEADS_PER_GROUP * tq

    def qk(jt, buf):
        buf[...] = _dot(k_ref[0, 0, pl.ds(pl.multiple_of(jt * tk, tk), tk), :], qa[...])

    def online_softmax(key0, n_keys, s):
        m_old = m_s[...]
        m_new = jnp.maximum(m_old, jnp.max(s, axis=0, keepdims=True))
        p = jnp.exp2(s - m_new).astype(BF16)
        acc[...] = jnp.exp2(m_old - m_new) * acc[...] + _dot(vt_ref[0, 0, :, pl.ds(key0, n_keys)], p)
        m_s[...] = m_new

    def absorb(jt, buf):
        online_softmax(pl.multiple_of(jt * tk, tk), tk, buf[...])

    key_row = lax.broadcasted_iota(I32, (tq, 1), 0)
    query_col = lax.broadcasted_iota(I32, (1, width), 1) & (tq - 1)

    def own_tile(s):
        return jnp.where(key_row <= query_col, s, NEG_BIG)

    def absorb_last(jt, buf):
        k0 = pl.multiple_of(jt * tk, tk)

        @pl.when(i % 2 == 1)
        def _():
            online_softmax(k0, tq, buf[0:tq, :])
            online_softmax(pl.multiple_of(k0 + tq, tq), tq, own_tile(buf[tq:tk, :]))

        @pl.when(i % 2 == 0)
        def _():
            online_softmax(k0, tq, own_tile(buf[0:tq, :]))

    def pair(u, c):
        qk(2 * u + 1, s_b)
        absorb(2 * u, s_a)
        qk(2 * u + 2, s_a)
        absorb(2 * u + 1, s_b)
        return c

    def window(miss):
        n_keys = (nw + 1 - miss) * tq
        k0 = pl.multiple_of((i - nw + miss) * tq, tq)
        s_b[0:n_keys, :] = _dot(kw_ref[0, 0, pl.ds(k0, n_keys), :], qa[0:HEAD_DIM, :])
        qk(0, s_a)
        parts = []
        for j in range(miss, nw + 1):
            part = s_b[(j - miss) * tq:(j - miss + 1) * tq, :]
            if j == 0:
                part = jnp.where(key_row > query_col, part, NEG_BIG)
            parts.append(own_tile(part) if j == nw else part)
        s = jnp.concatenate(parts, axis=0)
        m = jnp.max(s, axis=0, keepdims=True)
        p = jnp.exp2(s - m).astype(BF16)
        acc_w[...] = _dot(vwt_ref[0, 0, :, pl.ds(k0, n_keys)], p)

    nw = WINDOW // tq
    for miss in range(nw + 1):
        pl.when(jnp.maximum(nw - i, 0) == miss)(functools.partial(window, miss))

    last = (i * tq + tq + tk - 1) // tk - 1
    lax.fori_loop(0, last // 2, pair, 0)

    @pl.when(last % 2 == 1)
    def _():
        qk(last, s_b)
        absorb(last - 1, s_a)
        absorb_last(last, s_b)

    @pl.when(last % 2 == 0)
    def _():
        absorb_last(last, s_a)

    gates_t = gt_ref[0].T
    o_t = _gated(acc[...], gates_t, g, 1, HEAD_DIM) + _gated(acc_w[...], gates_t, g, 2, HEAD_DIM)
    o_ref[0] = _unstack_heads(o_t, tq).astype(BF16)


def _sel_win_attn(qt, bias, gates, ksa, vst, kw, vwt):
    B, HD, S = qt.shape
    G, dh = N_KV_GROUPS, HEAD_DIM
    gw = HD // G
    tq = min(SEL_Q_TILE, S // 2)
    tk = 2 * tq
    assert S % tk == 0 and WINDOW % tq == 0 and WINDOW + tq <= tk
    width = HEADS_PER_GROUP * tq
    return pl.pallas_call(
        functools.partial(_sel_win_kernel, tq=tq, tk=tk),
        out_shape=jax.ShapeDtypeStruct((B, S, HD), BF16),
        grid=(B, G, S // tq),
        in_specs=[
            pl.BlockSpec((1, gw, tq), lambda b, g, i: (b, g, i)),
            pl.BlockSpec((1, 1, MAX_SEL_BLOCKS, tq), lambda b, g, i: (b, g, 0, i)),
            pl.BlockSpec((1, tq, LANES), lambda b, g, i: (b, i, 0)),
            pl.BlockSpec((1, 1, S, KEY_AUG), lambda b, g, i: (b, g, 0, 0)),
            pl.BlockSpec((1, 1, V_ROWS, S), lambda b, g, i: (b, g, 0, 0)),
            pl.BlockSpec((1, 1, S, dh), lambda b, g, i: (b, g, 0, 0)),
            pl.BlockSpec((1, 1, V_ROWS, S), lambda b, g, i: (b, g, 0, 0)),
        ],
        out_specs=pl.BlockSpec((1, tq, gw), lambda b, g, i: (b, i, g)),
        scratch_shapes=[pltpu.VMEM((KEY_AUG, width), BF16), pltpu.VMEM((1, width), F32),
                        pltpu.VMEM((V_ROWS, width), F32), pltpu.VMEM((V_ROWS, width), F32),
                        pltpu.VMEM((tk, width), F32), pltpu.VMEM((tk, width), F32)],
        compiler_params=_cparams(("parallel", "parallel", "arbitrary"), 60),
        name="nsa_selected_window",
    )(qt, bias, gates, ksa, vst, kw, vwt)


def _oproj_kernel(h_ref, a_ref, b_ref, w_ref, g_ref, wh_ref, wl_ref, br_ref, o_ref, info_ref, cnt_ref, *, tt, rt):
    @pl.when(pl.program_id(0) == 0)
    def _():
        cnt_ref[...] = jnp.zeros_like(cnt_ref)

    o = a_ref[...].astype(F32) + b_ref[...].astype(F32)
    h_new = h_ref[...] + _dot(o.astype(BF16), w_ref[...])
    o_ref[...] = h_new
    cnt = cnt_ref[...]
    for k in range(tt // rt):
        rows = slice(k * rt, (k + 1) * rt)
        info_ref[rows, :], cnt = _route(h_new[rows], g_ref, wh_ref, wl_ref, br_ref, cnt)
    cnt_ref[...] = cnt


def _out_proj(h2, oc, osw, w_o, router_ops):
    T, D = h2.shape
    HD = oc.shape[-1]
    tt = min(SEQ_TILE, T)
    rt = min(TOK_TILE, tt)
    blk = lambda w: pl.BlockSpec((tt, w), lambda i: (i, 0))
    const = lambda i: (0, 0)
    return pl.pallas_call(
        functools.partial(_oproj_kernel, tt=tt, rt=rt),
        out_shape=(jax.ShapeDtypeStruct((T, D), F32), jax.ShapeDtypeStruct((T, LANES), F32),
                   jax.ShapeDtypeStruct((1, LANES), F32)),
        grid=(T // tt,),
        in_specs=[blk(D), blk(HD), blk(HD), pl.BlockSpec((HD, D), const)] + _router_specs(D, const),
        out_specs=(blk(D), blk(LANES), pl.BlockSpec((1, LANES), const)),
        compiler_params=_cparams(("arbitrary",), 48),
        name="nsa_out_proj_router",
    )(h2, oc, osw, w_o.astype(BF16), *router_ops)


def _nsa_layer(h, g, w_qg, b_gate, w_o, shared, router_ops):
    B, S, D = h.shape
    kc, vct, ksa, vst, kw, vwt = shared
    qt, gates = _q_proj(h, g, w_qg, b_gate)
    oc, bias = _cmp_attn(qt, gates, kc, vct)
    osw = _sel_win_attn(qt, bias, gates, ksa, vst, kw, vwt)
    flat = lambda a: a.reshape(B * S, a.shape[-1])
    h_new, info, cnt = _out_proj(flat(h), flat(oc), flat(osw), w_o, router_ops)
    return h_new, (info, cnt)


def _shared_kv(h, kv_norm, w_kv, ck, cv):
    B, S, _ = h.shape
    G, dh = N_KV_GROUPS, HEAD_DIM
    assert S % SEL_BLOCK == 0 and N_SELECT <= S // SEL_BLOCK <= MAX_SEL_BLOCKS
    cvals, ksa, vst, kw, vwt = _shared_kv_proj(h, kv_norm, w_kv)
    halves = cvals.reshape(B, S, 2, G, dh).transpose(2, 0, 3, 1, 4).reshape(2, B, G, S // CMP_STRIDE, CMP_STRIDE * dh)
    kc = _compress(halves, 0, *ck, keys=True)
    vct = _compress(halves, 1, *cv, keys=False)
    return kc, vct, ksa, vst, kw, vwt


def kernel(x, p, norm_mix, norm_ffn, norm_ple, pool_w, pool_b, pool_scale, kv_norm, w_kv, cmp_k_pos, cmp_k_w1, cmp_k_b1, cmp_k_w2, cmp_k_b2, cmp_v_pos, cmp_v_w1, cmp_v_b1, cmp_v_w2, cmp_v_b2, w_qg, b_gate, w_o, router_g_w, router_g_b, router_e_w, router_e_b, moe_w1, moe_w3, moe_w2, ple_proj, ple_gate_w, ple_gate_b, final_norm):
    B, S, D = x.shape
    depth = p.shape[0]
    n_a = pool_w.shape[0]
    T = B * S
    h = x
    shared = None
    for i in range(depth):
        if i == n_a:
            shared = _shared_kv(h, kv_norm, w_kv,
                                (cmp_k_pos, cmp_k_w1, cmp_k_b1, cmp_k_w2, cmp_k_b2),
                                (cmp_v_pos, cmp_v_w1, cmp_v_b1, cmp_v_w2, cmp_v_b2))
        router_ops = _router_operands(norm_ffn[i], router_g_w[i], router_g_b[i], router_e_w[i], router_e_b[i])
        if i < n_a:
            h2, routing = _pool_layer(h, norm_mix[i], pool_w[i], pool_b[i], pool_scale[i], router_ops)
            h2 = h2.reshape(T, D)
        else:
            j = i - n_a
            h2, routing = _nsa_layer(h, norm_mix[i], w_qg[j], b_gate[j], w_o[j], shared, router_ops)
        h = _moe_ple_layer(h2, routing, p.reshape(depth, T, p.shape[-1]), i, norm_ffn[i], moe_w1, moe_w3, moe_w2,
                           norm_ple[i], ple_gate_w[i], ple_gate_b[i], ple_proj[i], final_norm,
                           i == depth - 1).reshape(B, S, D)
    return h
```

```python
import functools

import jax
import jax.numpy as jnp
from jax import lax
from jax.experimental import pallas as pl
from jax.experimental.pallas import tpu as pltpu

F32 = jnp.float32
BF16 = jnp.bfloat16
I32 = jnp.int32

POOL_WINDOWS = (2, 4, 8, 16)
N_HEADS = 16
HEAD_DIM = 64
N_KV_GROUPS = 4
HEADS_PER_GROUP = N_HEADS // N_KV_GROUPS
N_BRANCH = 3
ROPE_DIMS = HEAD_DIM // 4
ROPE_HALF = ROPE_DIMS // 2
ROPE_THETA = 500000.0
CMP_BLOCK = 32
CMP_STRIDE = 16
SEL_BLOCK = 64
N_SELECT = 16
WINDOW = 512
FORCE_BONUS = 1e4
N_FORCED = 3
N_EXPERT_GROUPS = 4
EXPERTS_PER_GROUP = 8
N_EXPERTS = N_EXPERT_GROUPS * EXPERTS_PER_GROUP
TOP_K_IN_GROUP = 2
RMS_EPS = 1e-6

LANES = 128
MAX_SEL_BLOCKS = LANES
MASK_BIAS = -30000.0

LOG2E = 1.4426950408889634
Q_SCALE = HEAD_DIM ** -0.5 * LOG2E
V_ROWS = HEAD_DIM + 16
KEY_AUG = 2 * LANES
NEG_BIG = -1e30

SEQ_TILE = 1024
TOK_TILE = 512
CMB_TILE = 512
MOE_CHUNK = 512
Q_TILE = 1024
SEL_Q_TILE = 512
HALO = 16


def _cparams(sem, vmem_mb):
    return pltpu.CompilerParams(dimension_semantics=sem, vmem_limit_bytes=vmem_mb * 1024 * 1024)


def _rms(x, g):
    return x * lax.rsqrt(jnp.mean(x * x, axis=-1, keepdims=True) + RMS_EPS) * g


def _dot(a, b):
    return jnp.dot(a, b, preferred_element_type=F32)


def _dot_nt(a, b):
    return lax.dot_general(a, b, (((1,), (1,)), ((), ())), preferred_element_type=F32)


def _pool_kernel(h_ref, halo_ref, g_ref, w_ref, b_ref, sc_ref, gf_ref, wh_ref, wl_ref, br_ref, o_ref, info_ref, cnt_ref,
                 *, ts, cg, rt):
    i = pl.program_id(1)

    @pl.when(jnp.logical_and(pl.program_id(0) == 0, i == 0))
    def _():
        cnt_ref[...] = jnp.zeros_like(cnt_ref)

    x = h_ref[0]
    g = g_ref[...]
    xn = _rms(x, g)
    hn = _rms(halo_ref[0], g)
    hn = jnp.where(i > 0, hn, 0.0)
    ext = jnp.concatenate([hn, xn], axis=0)
    t = i * ts + lax.broadcasted_iota(I32, (ts, 1), 0)
    outs = []
    for gi, w in enumerate(POOL_WINDOWS):
        s = ext[:, gi * cg:(gi + 1) * cg]
        k = 1
        while k < w:
            s = s + pltpu.roll(s, k, axis=0)
            k *= 2
        cnt = jnp.minimum(t + 1, w).astype(F32)
        pooled = s[HALO:] / cnt - xn[:, gi * cg:(gi + 1) * cg]
        outs.append(_dot(pooled.astype(BF16), w_ref[gi]))
    y = jnp.concatenate(outs, axis=-1)
    h_new = x + (y + b_ref[...]) * sc_ref[...]
    o_ref[0] = h_new
    cnt = cnt_ref[...]
    for k in range(ts // rt):
        rows = slice(k * rt, (k + 1) * rt)
        info_ref[rows, :], cnt = _route(h_new[rows], gf_ref, wh_ref, wl_ref, br_ref, cnt)
    cnt_ref[...] = cnt


def _pool_layer(h, g, w, b, sc, router_ops):
    B, S, D = h.shape
    ts = min(SEQ_TILE, S)
    rt = min(TOK_TILE, ts)
    cg = D // len(POOL_WINDOWS)
    row = lambda v: v.reshape(1, D)
    const = lambda b_, i: (0, 0)
    h_new, info, cnt = pl.pallas_call(
        functools.partial(_pool_kernel, ts=ts, cg=cg, rt=rt),
        out_shape=(jax.ShapeDtypeStruct((B, S, D), F32), jax.ShapeDtypeStruct((B * S, LANES), F32),
                   jax.ShapeDtypeStruct((1, LANES), F32)),
        grid=(B, S // ts),
        in_specs=[
            pl.BlockSpec((1, ts, D), lambda b_, i: (b_, i, 0)),
            pl.BlockSpec((1, HALO, D), lambda b_, i: (b_, jnp.maximum(i * (ts // HALO) - 1, 0), 0)),
            pl.BlockSpec((1, D), const),
            pl.BlockSpec((len(POOL_WINDOWS), cg, cg), lambda b_, i: (0, 0, 0)),
            pl.BlockSpec((1, D), const),
            pl.BlockSpec((1, D), const),
        ] + _router_specs(D, const),
        out_specs=(pl.BlockSpec((1, ts, D), lambda b_, i: (b_, i, 0)),
                   pl.BlockSpec((ts, LANES), lambda b_, i: (b_ * (S // ts) + i, 0)),
                   pl.BlockSpec((1, LANES), const)),
        compiler_params=_cparams(("arbitrary", "arbitrary"), 48),
        name="pool_mixer_router",
    )(h, h, row(g), w.astype(BF16), row(b), row(sc), *router_ops)
    return h_new, (info, cnt)


def _route(h, g_ref, wh_ref, wl_ref, b_ref, cnt):
    tt = h.shape[0]
    xn = _rms(h, g_ref[...])
    xh = xn.astype(BF16)
    xl = (xn - xh.astype(F32)).astype(BF16)
    logits = _dot(xh, wh_ref[...]) + (_dot(xh, wl_ref[...]) + _dot(xl, wh_ref[...])) + b_ref[...]
    lane = lax.broadcasted_iota(I32, (tt, LANES), 1)
    neg = -jnp.inf
    gl = jnp.where(lane < N_EXPERT_GROUPS, logits, neg)
    gmax = jnp.max(gl, axis=-1, keepdims=True)
    grp = jnp.min(jnp.where(gl == gmax, lane, LANES), axis=-1, keepdims=True)
    gprob = 1.0 / jnp.sum(jnp.exp(gl - gmax), axis=-1, keepdims=True)
    lo = N_EXPERT_GROUPS + grp * EXPERTS_PER_GROUP
    el = jnp.where(lane >= lo, jnp.where(lane < lo + EXPERTS_PER_GROUP, logits, neg), neg)
    v1 = jnp.max(el, axis=-1, keepdims=True)
    i1 = jnp.min(jnp.where(el == v1, lane, LANES), axis=-1, keepdims=True)
    el2 = jnp.where(lane == i1, neg, el)
    v2 = jnp.max(el2, axis=-1, keepdims=True)
    i2 = jnp.min(jnp.where(el2 == v2, lane, LANES), axis=-1, keepdims=True)
    e2 = jnp.exp(v2 - v1)
    w1 = gprob / (1.0 + e2)
    w2 = gprob * e2 / (1.0 + e2)
    oh1 = lane == i1
    oh2 = lane == i2
    oh = jnp.where(oh1, 1.0, jnp.where(oh2, 1.0, 0.0))
    r_ = lax.broadcasted_iota(I32, (tt, tt), 0)
    c_ = lax.broadcasted_iota(I32, (tt, tt), 1)
    tri = jnp.where(r_ > c_, 1.0, 0.0).astype(BF16)
    tot = _dot(tri, oh.astype(BF16)) + cnt
    r1 = jnp.sum(jnp.where(oh1, tot, 0.0), axis=-1, keepdims=True)
    r2 = jnp.sum(jnp.where(oh2, tot, 0.0), axis=-1, keepdims=True)
    vals = (i1.astype(F32) - N_EXPERT_GROUPS, i2.astype(F32) - N_EXPERT_GROUPS, w1, w2, r1, r2)
    info = jnp.zeros((tt, LANES), F32)
    for k, v in enumerate(vals):
        info = jnp.where(lane == k, v, info)
    return info, cnt + jnp.sum(oh, axis=0, keepdims=True)


def _router_operands(norm_ffn, rg_w, rg_b, re_w, re_b):
    D = rg_w.shape[0]
    pad = LANES - N_EXPERT_GROUPS - N_EXPERTS
    wr = jnp.concatenate([rg_w, re_w, jnp.zeros((D, pad), F32)], axis=1)
    br = jnp.concatenate([rg_b, re_b, jnp.zeros((pad,), F32)]).reshape(1, LANES)
    wh = wr.astype(BF16)
    return norm_ffn.reshape(1, D), wh, (wr - wh.astype(F32)).astype(BF16), br


def _router_specs(D, const):
    return [pl.BlockSpec((1, D), const), pl.BlockSpec((D, LANES), const), pl.BlockSpec((D, LANES), const),
            pl.BlockSpec((1, LANES), const)]


def _row_copy(src, s, dst, d, sem):
    return pltpu.make_async_copy(src.at[pl.ds(s, 1)], dst.at[pl.ds(d, 1)], sem)


def _dispatch_kernel(dest_ref, cend_ref, h_ref, g_ref, xs_ref, xn_s, sem, *, tt, n_tok, n_chunks):
    base = pl.program_id(0) * tt

    @pl.when(pl.program_id(0) == 0)
    def _():
        xn_s[...] = jnp.zeros_like(xn_s)

        def tail(e):
            nonempty = cend_ref[e] > (cend_ref[e - 1] if e > 0 else 0)
            row = pl.multiple_of((cend_ref[e] - 1) * tt, tt)
            return nonempty, pltpu.make_async_copy(xn_s, xs_ref.at[pl.ds(row, tt)], sem)

        def unused(c):
            return pltpu.make_async_copy(xn_s, xs_ref.at[pl.ds(pl.multiple_of(c * tt, tt), tt)], sem)

        n_used = cend_ref[N_EXPERTS - 1]
        for e in range(N_EXPERTS):
            nonempty, cp = tail(e)
            pl.when(nonempty)(cp.start)
        lax.fori_loop(n_used, n_chunks, lambda c, z: (unused(c).start(), z)[1], 0)
        for e in range(N_EXPERTS):
            nonempty, cp = tail(e)
            pl.when(nonempty)(cp.wait)
        lax.fori_loop(n_used, n_chunks, lambda c, z: (unused(c).wait(), z)[1], 0)

    xn_s[...] = _rms(h_ref[...], g_ref[...])

    for r in range(tt):
        for k in range(TOP_K_IN_GROUP):
            _row_copy(xn_s, r, xs_ref, dest_ref[k * n_tok + base + r], sem).start(priority=k % 2)
    for k in range(TOP_K_IN_GROUP):
        pltpu.make_async_copy(xn_s, xs_ref.at[pl.ds(0, tt)], sem).wait()


def _dispatch(dest, chunk_end, h2, g, n_rows):
    T, D = h2.shape
    tt = MOE_CHUNK
    assert T % tt == 0
    return pl.pallas_call(
        functools.partial(_dispatch_kernel, tt=tt, n_tok=T, n_chunks=n_rows // tt),
        out_shape=jax.ShapeDtypeStruct((n_rows, D), F32),
        grid_spec=pltpu.PrefetchScalarGridSpec(
            num_scalar_prefetch=2,
            grid=(T // tt,),
            in_specs=[
                pl.BlockSpec((tt, D), lambda i, d, ce: (i, 0)),
                pl.BlockSpec((1, D), lambda i, d, ce: (0, 0)),
            ],
            out_specs=pl.BlockSpec(memory_space=pl.ANY),
            scratch_shapes=[pltpu.VMEM((tt, D), F32), pltpu.SemaphoreType.DMA],
        ),
        compiler_params=_cparams(("arbitrary",), 40),
        name="moe_dispatch",
    )(dest, chunk_end, h2, g.reshape(1, D))


def _expert_kernel(ce_ref, nv_ref, xs_ref, w1_ref, w3_ref, w2_ref, o_ref, w1b, w3b, w2b):
    c = pl.program_id(0)
    e = ce_ref[c]
    prev = ce_ref[jnp.maximum(c - 1, 0)]

    @pl.when(jnp.logical_or(c == 0, e != prev))
    def _():
        w1b[...] = w1_ref[0, 0].astype(BF16)
        w3b[...] = w3_ref[0, 0].astype(BF16)
        w2b[...] = w2_ref[0, 0].astype(BF16)

    @pl.when(c < nv_ref[0])
    def _():
        x = xs_ref[...].astype(BF16)
        half = w1b.shape[1] // 2
        out = None
        for k in range(2):
            cols = slice(k * half, (k + 1) * half)
            a = _dot(x, w1b[:, cols])
            b = _dot(x, w3b[:, cols])
            hc = (a * jax.nn.sigmoid(a) * b).astype(BF16)
            part = _dot(hc, w2b[cols, :])
            out = part if out is None else out + part
        o_ref[...] = out

    @pl.when(c >= nv_ref[0])
    def _():
        o_ref[...] = jnp.zeros_like(o_ref)


def _experts(chunk_e, n_valid, xs, w1, w3, w2, layer):
    P, D = xs.shape
    F = w1.shape[-1]
    ch = MOE_CHUNK
    rows = lambda c, ce, nv: (jnp.minimum(c, nv[0] - 1), 0)
    wsel = lambda c, ce, nv: (layer, ce[c], 0, 0)
    return pl.pallas_call(
        _expert_kernel,
        out_shape=jax.ShapeDtypeStruct((P, D), F32),
        grid_spec=pltpu.PrefetchScalarGridSpec(
            num_scalar_prefetch=2,
            grid=(P // ch,),
            in_specs=[
                pl.BlockSpec((ch, D), rows),
                pl.BlockSpec((1, 1, D, F), wsel),
                pl.BlockSpec((1, 1, D, F), wsel),
                pl.BlockSpec((1, 1, F, D), wsel),
            ],
            out_specs=pl.BlockSpec((ch, D), lambda c, ce, nv: (c, 0)),
            scratch_shapes=[pltpu.VMEM((D, F), BF16), pltpu.VMEM((D, F), BF16), pltpu.VMEM((F, D), BF16)],
        ),
        compiler_params=_cparams(("arbitrary",), 56),
        name="moe_experts",
    )(chunk_e, n_valid, xs, w1, w3, w2)


def _combine_kernel(dest_ref, h_ref, info_ref, rows_ref, p_ref, g_ref, gw_ref, gb_ref, pw_ref, fn_ref,
                    o_ref, buf_a, buf_b, sem, *, tt, n_tok, n_steps, final):
    i = pl.program_id(0)
    bufs = (buf_a, buf_b)

    def row(tile, sl, k, r):
        return _row_copy(rows_ref, dest_ref[k * n_tok + tile * tt + r], bufs[sl].at[k], r, sem.at[sl])

    def wait_rows(sl):
        for k in range(TOP_K_IN_GROUP):
            pltpu.make_async_copy(rows_ref.at[pl.ds(0, tt)], bufs[sl].at[k], sem.at[sl]).wait()

    @pl.when(i == 0)
    def _():
        def issue(r, c):
            for k in range(TOP_K_IN_GROUP):
                row(0, 0, k, r).start(priority=k % 2)
            return c

        lax.fori_loop(0, tt, issue, 0, unroll=8)

    def step(sl):
        wait_rows(sl)
        nxt = jnp.minimum(i + 1, n_steps - 1)
        for r in range(tt):
            for k in range(TOP_K_IN_GROUP):
                row(nxt, 1 - sl, k, r).start(priority=k % 2)
        info = info_ref[...]
        y = h_ref[...] + info[:, 2:3] * bufs[sl][0] + info[:, 3:4] * bufs[sl][1]
        hn = _rms(y, g_ref[...])
        gate = jax.nn.sigmoid(_dot(hn.astype(BF16), gw_ref[...]) + gb_ref[...])
        out = y + _dot(p_ref[0].astype(BF16), pw_ref[...]) * gate
        if final:
            out = _rms(out, fn_ref[...])
        o_ref[...] = out
        pl.when(i == n_steps - 1)(functools.partial(wait_rows, 1 - sl))

    for parity in range(2):
        pl.when(i % 2 == parity)(functools.partial(step, parity))


def _combine(dest, h2, info, rows, p3, layer, g, gw, gb, pw, fn, final):
    T, D = h2.shape
    PD = p3.shape[-1]
    tt = min(CMB_TILE, T)
    full = lambda i, d: (0, 0)
    return pl.pallas_call(
        functools.partial(_combine_kernel, tt=tt, n_tok=T, n_steps=T // tt, final=final),
        out_shape=jax.ShapeDtypeStruct((T, D), F32),
        grid_spec=pltpu.PrefetchScalarGridSpec(
            num_scalar_prefetch=1,
            grid=(T // tt,),
            in_specs=[
                pl.BlockSpec((tt, D), lambda i, d: (i, 0)),
                pl.BlockSpec((tt, LANES), lambda i, d: (i, 0)),
                pl.BlockSpec(memory_space=pl.ANY),
                pl.BlockSpec((1, tt, PD), lambda i, d: (layer, i, 0)),
                pl.BlockSpec((1, D), full),
                pl.BlockSpec((D, D), full),
                pl.BlockSpec((1, D), full),
                pl.BlockSpec((PD, D), full),
                pl.BlockSpec((1, D), full),
            ],
            out_specs=pl.BlockSpec((tt, D), lambda i, d: (i, 0)),
            scratch_shapes=[pltpu.VMEM((TOP_K_IN_GROUP, tt, D), F32), pltpu.VMEM((TOP_K_IN_GROUP, tt, D), F32),
                            pltpu.SemaphoreType.DMA((2,))],
        ),
        compiler_params=_cparams(("arbitrary",), 40),
        name="moe_combine_ple",
    )(dest, h2, info, rows, p3, g.reshape(1, D), gw.astype(BF16), gb.reshape(1, D), pw.astype(BF16),
      fn.reshape(1, D))


def _moe_ple_layer(h2, routing, p3, layer, norm_ffn, w1, w3, w2, norm_ple, gate_w, gate_b, ple_proj, final_norm, final):
    T, D = h2.shape
    A = T * TOP_K_IN_GROUP
    info, cnt = routing
    counts = cnt[0, N_EXPERT_GROUPS:N_EXPERT_GROUPS + N_EXPERTS].astype(I32)
    n_chunks_e = (counts + MOE_CHUNK - 1) // MOE_CHUNK
    chunk_end = jnp.cumsum(n_chunks_e)
    pstarts = (chunk_end - n_chunks_e) * MOE_CHUNK
    n_chunks = -(-A // MOE_CHUNK) + N_EXPERTS
    n_valid = chunk_end[-1:].astype(I32)
    cidx = jnp.minimum(jnp.arange(n_chunks, dtype=I32), n_valid[0] - 1)
    chunk_e = jnp.sum((chunk_end[None, :] <= cidx[:, None]).astype(I32), axis=1)
    chunk_e = jnp.minimum(chunk_e, N_EXPERTS - 1)
    e_idx = info[:, 0:TOP_K_IN_GROUP].astype(I32)
    rank = info[:, 4:4 + TOP_K_IN_GROUP].astype(I32)
    start = jnp.sum(jnp.where(e_idx[..., None] == jnp.arange(N_EXPERTS, dtype=I32), pstarts, 0), axis=-1)
    dest = (start + rank).T.reshape(A)
    xs = _dispatch(dest, chunk_end.astype(I32), h2, norm_ffn, n_chunks * MOE_CHUNK)
    rows = _experts(chunk_e, n_valid, xs, w1, w3, w2, layer)
    return _combine(dest, h2, info, rows, p3, layer, norm_ple, gate_w, gate_b, ple_proj, final_norm, final)


def _rope_rows(xt, cos, sin):
    x1 = xt[0:ROPE_HALF]
    x2 = xt[ROPE_HALF:ROPE_DIMS]
    return jnp.concatenate([x1 * cos - x2 * sin, x2 * cos + x1 * sin, xt[ROPE_DIMS:]], axis=0)


def _rope_tables(pos):
    inv = jnp.float32(ROPE_THETA) ** (-jnp.arange(ROPE_HALF, dtype=F32) * 2.0 / ROPE_DIMS)
    ang = pos.astype(F32)[:, None] * inv[None, :]
    return jnp.cos(ang), jnp.sin(ang)


def _lane_rope_tables(pos):
    cos, sin = _rope_tables(pos)
    n = pos.shape[0]
    ones = jnp.ones((n, HEAD_DIM - ROPE_DIMS), F32)
    zeros = jnp.zeros((n, HEAD_DIM - ROPE_DIMS), F32)
    zh = jnp.zeros((n, ROPE_HALF), F32)
    two = lambda a: jnp.concatenate([a, a], axis=1)
    return (two(jnp.concatenate([cos, cos, ones], axis=1)), two(jnp.concatenate([zh, sin, zeros], axis=1)),
            two(jnp.concatenate([-sin, zh, zeros], axis=1)))


def _rope_lanes(x, c, s1, s2):
    return x * c + pltpu.roll(x, ROPE_HALF, axis=1) * s1 + pltpu.roll(x, LANES - ROPE_HALF, axis=1) * s2


def _kv_kernel(h_ref, g_ref, wn_ref, wt_ref, c_ref, s1_ref, s2_ref, cv_ref, ksa_ref, vst_ref, kw_ref, vwt_ref, *, ts):
    i = pl.program_id(1)
    kvw = N_KV_GROUPS * HEAD_DIM
    hn = _rms(h_ref[0], g_ref[...]).astype(BF16)
    nat = _dot(hn, wn_ref[...])
    cv_ref[0] = nat[:, 0:2 * kvw]
    tr = _dot_nt(wt_ref[...], hn)
    c, s1, s2 = c_ref[...], s1_ref[...], s2_ref[...]
    roped = [_rope_lanes(nat[:, 2 * kvw + k * LANES:2 * kvw + (k + 1) * LANES], c, s1, s2)
             for k in range(2 * kvw // LANES)]
    lane = lax.broadcasted_iota(I32, (ts, KEY_AUG), 1)
    pos = i * ts + lax.broadcasted_iota(I32, (ts, KEY_AUG), 0)
    onehot = jnp.where(lane - HEAD_DIM == pos // SEL_BLOCK, 1.0, 0.0).astype(BF16)
    ones_row = jnp.where(lax.broadcasted_iota(I32, (V_ROWS - HEAD_DIM, ts), 0) == 0, 1.0, 0.0).astype(BF16)
    per_tile = LANES // HEAD_DIM
    for g in range(N_KV_GROUPS):
        lo = (g % per_tile) * HEAD_DIM
        ksa_ref[0, g] = onehot
        ksa_ref[0, g, :, 0:HEAD_DIM] = roped[g // per_tile][:, lo:lo + HEAD_DIM].astype(BF16)
        kw_ref[0, g] = roped[N_KV_GROUPS // per_tile + g // per_tile][:, lo:lo + HEAD_DIM].astype(BF16)
        for ref, base in ((vst_ref, 0), (vwt_ref, kvw)):
            ref[0, g, 0:HEAD_DIM, :] = tr[base + g * HEAD_DIM:base + (g + 1) * HEAD_DIM].astype(BF16)
            ref[0, g, HEAD_DIM:, :] = ones_row


def _shared_kv_proj(h, kv_norm, w_kv):
    B, S, D = h.shape
    G, dh = N_KV_GROUPS, HEAD_DIM
    kvw = G * dh
    ts = min(SEQ_TILE, S)
    br = lambda k: w_kv[:, k * kvw:(k + 1) * kvw]
    w_nat = jnp.concatenate([br(0), br(1), br(2), br(4)], axis=1).astype(BF16)
    w_tr = jnp.concatenate([br(3), br(5)], axis=1).T.astype(BF16)
    tabs = _lane_rope_tables(jnp.arange(S))
    tab = pl.BlockSpec((ts, LANES), lambda b, i: (i, 0))
    return pl.pallas_call(
        functools.partial(_kv_kernel, ts=ts),
        out_shape=(
            jax.ShapeDtypeStruct((B, S, 2 * kvw), F32),
            jax.ShapeDtypeStruct((B, G, S, KEY_AUG), BF16),
            jax.ShapeDtypeStruct((B, G, V_ROWS, S), BF16),
            jax.ShapeDtypeStruct((B, G, S, dh), BF16),
            jax.ShapeDtypeStruct((B, G, V_ROWS, S), BF16),
        ),
        grid=(B, S // ts),
        in_specs=[
            pl.BlockSpec((1, ts, D), lambda b, i: (b, i, 0)),
            pl.BlockSpec((1, D), lambda b, i: (0, 0)),
            pl.BlockSpec((D, 4 * kvw), lambda b, i: (0, 0)),
            pl.BlockSpec((2 * kvw, D), lambda b, i: (0, 0)),
            tab, tab, tab,
        ],
        out_specs=(
            pl.BlockSpec((1, ts, 2 * kvw), lambda b, i: (b, i, 0)),
            pl.BlockSpec((1, G, ts, KEY_AUG), lambda b, i: (b, 0, i, 0)),
            pl.BlockSpec((1, G, V_ROWS, ts), lambda b, i: (b, 0, 0, i)),
            pl.BlockSpec((1, G, ts, dh), lambda b, i: (b, 0, i, 0)),
            pl.BlockSpec((1, G, V_ROWS, ts), lambda b, i: (b, 0, 0, i)),
        ),
        compiler_params=_cparams(("parallel", "parallel"), 48),
        name="shared_kv_proj",
    )(h, kv_norm.reshape(1, D), w_nat, w_tr, *tabs)


def _compress_kernel(x_ref, pos_ref, w1_ref, b1_ref, w2_ref, b2_ref, c_ref, s1_ref, s2_ref, o_ref, *, nh, keys):
    x = x_ref[0, 0, 0]
    a = _dot((x + pos_ref[0:1]).astype(BF16), w1_ref[0])
    b = _dot((x + pos_ref[1:2]).astype(BF16), w1_ref[1])
    hid = jax.nn.gelu(a + pltpu.roll(b, nh - 1, axis=0) + b1_ref[...]).astype(BF16)
    if keys:
        out = _rope_lanes(_dot(hid, w2_ref[...]) + b2_ref[...], c_ref[...], s1_ref[...], s2_ref[...])
        o_ref[0, 0] = out[:, 0:HEAD_DIM].astype(BF16)
    else:
        o_ref[0, 0] = (_dot_nt(w2_ref[...], hid) + b2_ref[...]).astype(BF16)


def _compress(halves, which, pos_emb, w1, b1, w2, b2, keys):
    _, B, G, nh, hw = halves.shape
    dh = HEAD_DIM
    hidden = w1.shape[-1]
    pos2 = pos_emb.reshape(2, hw)
    w1s = w1.reshape(2, hw, hidden).astype(BF16)
    tabs = _lane_rope_tables(jnp.arange(nh) * CMP_STRIDE + CMP_BLOCK - 1)
    if keys:
        w2a = jnp.pad(w2, ((0, 0), (0, LANES - dh))).astype(BF16)
        b2a = jnp.pad(b2, (0, LANES - dh)).reshape(1, LANES)
        out_shape, out_block = (B, G, nh, dh), (1, 1, nh, dh)
    else:
        w2a, b2a = w2.T.astype(BF16), b2.reshape(dh, 1)
        out_shape, out_block = (B, G, dh, nh), (1, 1, dh, nh)
    c2 = lambda b, g: (0, 0)
    tab = pl.BlockSpec((nh, LANES), c2)
    return pl.pallas_call(
        functools.partial(_compress_kernel, nh=nh, keys=keys),
        out_shape=jax.ShapeDtypeStruct(out_shape, BF16),
        grid=(B, G),
        in_specs=[
            pl.BlockSpec((1, 1, 1, nh, hw), lambda b, g: (which, b, g, 0, 0)),
            pl.BlockSpec((2, hw), c2),
            pl.BlockSpec((2, hw, hidden), lambda b, g: (0, 0, 0)),
            pl.BlockSpec((1, hidden), c2),
            pl.BlockSpec(w2a.shape, c2),
            pl.BlockSpec(b2a.shape, c2),
            tab, tab, tab,
        ],
        out_specs=pl.BlockSpec(out_block, lambda b, g: (b, g, 0, 0)),
        compiler_params=_cparams(("parallel", "parallel"), 40),
        name="compress_k" if keys else "compress_v",
    )(halves, pos2, w1s, b1.reshape(1, hidden), w2a, b2a, *tabs)


def _qproj_kernel(h_ref, g_ref, wqt_ref, wg_ref, bg_ref, cos_ref, sin_ref, qt_ref, gt_ref):
    xn = _rms(h_ref[0], g_ref[...]).astype(BF16)
    tr = _dot_nt(wqt_ref[...], xn)
    cos, sin = cos_ref[...], sin_ref[...]
    for hd in range(N_HEADS):
        rows = slice(hd * HEAD_DIM, (hd + 1) * HEAD_DIM)
        qt_ref[0, rows, :] = (_rope_rows(tr[rows], cos, sin) * Q_SCALE).astype(BF16)
    gt_ref[0] = jax.nn.sigmoid(_dot(xn, wg_ref[...]) + bg_ref[...])


def _q_proj(h, g, w_qg, b_gate):
    B, S, D = h.shape
    HD = N_HEADS * HEAD_DIM
    ng = N_HEADS * N_BRANCH
    ts = min(SEQ_TILE, S)
    wqt = w_qg[:, :HD].T.astype(BF16)
    wg = jnp.pad(w_qg[:, HD:], ((0, 0), (0, LANES - ng))).astype(BF16)
    bg = jnp.pad(b_gate, (0, LANES - ng)).reshape(1, LANES)
    cos, sin = _rope_tables(jnp.arange(S))
    tab = pl.BlockSpec((ROPE_HALF, ts), lambda b, i: (0, i))
    return pl.pallas_call(
        _qproj_kernel,
        out_shape=(jax.ShapeDtypeStruct((B, HD, S), BF16), jax.ShapeDtypeStruct((B, S, LANES), F32)),
        grid=(B, S // ts),
        in_specs=[
            pl.BlockSpec((1, ts, D), lambda b, i: (b, i, 0)),
            pl.BlockSpec((1, D), lambda b, i: (0, 0)),
            pl.BlockSpec((HD, D), lambda b, i: (0, 0)),
            pl.BlockSpec((D, LANES), lambda b, i: (0, 0)),
            pl.BlockSpec((1, LANES), lambda b, i: (0, 0)),
            tab, tab,
        ],
        out_specs=(pl.BlockSpec((1, HD, ts), lambda b, i: (b, 0, i)),
                   pl.BlockSpec((1, ts, LANES), lambda b, i: (b, i, 0))),
        compiler_params=_cparams(("parallel", "parallel"), 40),
        name="nsa_q_proj",
    )(h, g.reshape(1, D), wqt, wg, bg, cos.T, sin.T)


def _heads_on_lanes(qt):
    return jnp.concatenate([qt[r * HEAD_DIM:(r + 1) * HEAD_DIM] for r in range(HEADS_PER_GROUP)], axis=1)


def _lane_query_pos(i, tq):
    lanes = lax.broadcasted_iota(I32, (1, HEADS_PER_GROUP * tq), 1)
    return i * tq + (lanes & (tq - 1))


def _gated(acc_t, gates_t, g, branch, denom_row):
    row = lax.broadcasted_iota(I32, gates_t.shape, 0)
    scale = jnp.concatenate(
        [jnp.sum(jnp.where(row == (g * HEADS_PER_GROUP + r) * N_BRANCH + branch, gates_t, 0.0), axis=0, keepdims=True)
         for r in range(HEADS_PER_GROUP)], axis=1)
    if denom_row is not None:
        scale = scale / acc_t[denom_row:denom_row + 1]
    return acc_t[0:HEAD_DIM] * scale


def _unstack_heads(o_t, tq):
    per = LANES // HEAD_DIM
    cols = []
    for r in range(0, HEADS_PER_GROUP, per):
        cols.append(jnp.concatenate([o_t[:, (r + k) * tq:(r + k + 1) * tq] for k in range(per)], axis=0).T)
    return jnp.concatenate(cols, axis=-1)


def _cmp_kernel(qt_ref, gt_ref, k_ref, vt_ref, c2s_ref, o_ref, bias_ref, *, tq, nc):
    g = pl.program_id(1)
    i = pl.program_id(2)

    def body(nk, nb):
        s = _dot(k_ref[0, 0, 0:nk, :], _heads_on_lanes(qt_ref[0]))
        t = _lane_query_pos(i, tq)
        n = lax.broadcasted_iota(I32, (nk, 1), 0)
        s = jnp.where(n * CMP_STRIDE + (CMP_BLOCK - 1) <= t, s, -jnp.inf)
        m = jnp.max(s, axis=0, keepdims=True)
        m = jnp.where(m == -jnp.inf, 0.0, m)
        e = jnp.exp2(s - m)
        p = e * (1.0 / jnp.maximum(jnp.sum(e, axis=0, keepdims=True), 1e-30))
        ot = _dot(vt_ref[0, 0, :, 0:nk], p.astype(BF16))
        o_ref[0] = _unstack_heads(_gated(ot, gt_ref[0].T, g, 0, None), tq).astype(BF16)
        ps = p[:, 0:tq]
        for r in range(1, HEADS_PER_GROUP):
            ps = ps + p[:, r * tq:(r + 1) * tq]
        hi = ps.astype(BF16)
        rem = ps - hi.astype(F32)
        mid = rem.astype(BF16)
        lo = (rem - mid.astype(F32)).astype(BF16)
        c2s = c2s_ref[0:nb, 0:nk]
        imp = _dot(c2s, hi) + _dot(c2s, mid) + _dot(c2s, lo)
        cur = (i * tq + lax.broadcasted_iota(I32, (1, tq), 1)) // SEL_BLOCK
        j = lax.broadcasted_iota(I32, (nb, tq), 0)
        valid = j <= cur
        forced = jnp.where(j == 0, 1.0, jnp.where(j == cur, 1.0, jnp.where(j == cur - 1, 1.0, 0.0)))
        sel = jnp.where(valid, forced, 0.0)
        work = jnp.where(valid, jnp.where(forced > 0.0, -jnp.inf, imp), -jnp.inf)
        for _ in range(N_SELECT - N_FORCED):
            mx = jnp.max(work, axis=0, keepdims=True)
            idx = jnp.min(jnp.where(work == mx, j, MAX_SEL_BLOCKS), axis=0, keepdims=True)
            pick = j == idx
            sel = jnp.where(pick, 1.0, sel)
            work = jnp.where(pick, -jnp.inf, work)
        bias_ref[0, 0, 0:nb, :] = jnp.where(valid, jnp.where(sel > 0.0, 0.0, MASK_BIAS), MASK_BIAS).astype(BF16)
        if nb < MAX_SEL_BLOCKS:
            bias_ref[0, 0, nb:, :] = jnp.full((MAX_SEL_BLOCKS - nb, tq), MASK_BIAS, BF16)

    per_class = LANES * CMP_STRIDE // tq
    n_class = -(-(nc * CMP_STRIDE // tq) // per_class)
    for cls in range(n_class):
        nk = min(nc, LANES * (cls + 1))
        nb = min(MAX_SEL_BLOCKS, nk * CMP_STRIDE // SEL_BLOCK)
        pl.when(i // per_class == cls)(functools.partial(body, nk, nb))


def _cmp_attn(qt, gates, kc, vct):
    B, HD, S = qt.shape
    G, dh = N_KV_GROUPS, HEAD_DIM
    gw = HD // G
    nc = kc.shape[2]
    tq = min(Q_TILE, S)
    n0 = jnp.arange(nc)[None, :] * CMP_STRIDE
    j0 = jnp.arange(MAX_SEL_BLOCKS)[:, None] * SEL_BLOCK
    c2s = ((n0 < j0 + SEL_BLOCK) & (n0 + CMP_BLOCK > j0)).astype(BF16)
    return pl.pallas_call(
        functools.partial(_cmp_kernel, tq=tq, nc=nc),
        out_shape=(jax.ShapeDtypeStruct((B, S, HD), BF16), jax.ShapeDtypeStruct((B, G, MAX_SEL_BLOCKS, S), BF16)),
        grid=(B, G, S // tq),
        in_specs=[
            pl.BlockSpec((1, gw, tq), lambda b, g, i: (b, g, i)),
            pl.BlockSpec((1, tq, LANES), lambda b, g, i: (b, i, 0)),
            pl.BlockSpec((1, 1, nc, dh), lambda b, g, i: (b, g, 0, 0)),
            pl.BlockSpec((1, 1, dh, nc), lambda b, g, i: (b, g, 0, 0)),
            pl.BlockSpec((MAX_SEL_BLOCKS, nc), lambda b, g, i: (0, 0)),
        ],
        out_specs=(pl.BlockSpec((1, tq, gw), lambda b, g, i: (b, i, g)),
                   pl.BlockSpec((1, 1, MAX_SEL_BLOCKS, tq), lambda b, g, i: (b, g, 0, i))),
        compiler_params=_cparams(("parallel", "parallel", "parallel"), 40),
        name="nsa_compressed",
    )(qt, gates, kc, vct, c2s)


def _sel_win_kernel(qt_ref, bias_ref, gt_ref, k_ref, vt_ref, kw_ref, vwt_ref, o_ref, qa, m_s, acc, acc_w, s_a, s_b,
                    *, tq, tk):
    g = pl.program_id(1)
    i = pl.program_id(2)
    qt = qt_ref[0]
    bias = bias_ref[0, 0]
    for r in range(HEADS_PER_GROUP):
        cols = slice(r * tq, (r + 1) * tq)
        qa[0:HEAD_DIM, cols] = qt[r * HEAD_DIM:(r + 1) * HEAD_DIM]
        qa[HEAD_DIM:HEAD_DIM + MAX_SEL_BLOCKS, cols] = bias
        qa[HEAD_DIM + MAX_SEL_BLOCKS:, cols] = jnp.zeros((KEY_AUG - HEAD_DIM - MAX_SEL_BLOCKS, tq), BF16)
    m_s[...] = jnp.full(m_s.shape, NEG_BIG, F32)
    acc[...] = jnp.zeros(acc.shape, F32)
    width = HEADS_PER_GROUP * tq

    def qk(jt, buf):
        buf[...] = _dot(k_ref[0, 0, pl.ds(pl.multiple_of(jt * tk, tk), tk), :], qa[...])

    def online_softmax(key0, n_keys, s):
        m_old = m_s[...]
        m_new = jnp.maximum(m_old, jnp.max(s, axis=0, keepdims=True))
        p = jnp.exp2(s - m_new).astype(BF16)
        acc[...] = jnp.exp2(m_old - m_new) * acc[...] + _dot(vt_ref[0, 0, :, pl.ds(key0, n_keys)], p)
        m_s[...] = m_new

    def absorb(jt, buf):
        online_softmax(pl.multiple_of(jt * tk, tk), tk, buf[...])

    key_row = lax.broadcasted_iota(I32, (tq, 1), 0)
    query_col = lax.broadcasted_iota(I32, (1, width), 1) & (tq - 1)

    def own_tile(s):
        return jnp.where(key_row <= query_col, s, NEG_BIG)

    def absorb_last(jt, buf):
        k0 = pl.multiple_of(jt * tk, tk)

        @pl.when(i % 2 == 1)
        def _():
            online_softmax(k0, tq, buf[0:tq, :])
            online_softmax(pl.multiple_of(k0 + tq, tq), tq, own_tile(buf[tq:tk, :]))

        @pl.when(i % 2 == 0)
        def _():
            online_softmax(k0, tq, own_tile(buf[0:tq, :]))

    def pair(u, c):
        qk(2 * u + 1, s_b)
        absorb(2 * u, s_a)
        qk(2 * u + 2, s_a)
        absorb(2 * u + 1, s_b)
        return c

    def window(miss):
        n_keys = (nw + 1 - miss) * tq
        k0 = pl.multiple_of((i - nw + miss) * tq, tq)
        s_b[0:n_keys, :] = _dot(kw_ref[0, 0, pl.ds(k0, n_keys), :], qa[0:HEAD_DIM, :])
        qk(0, s_a)
        parts = []
        for j in range(miss, nw + 1):
            part = s_b[(j - miss) * tq:(j - miss + 1) * tq, :]
            if j == 0:
                part = jnp.where(key_row > query_col, part, NEG_BIG)
            parts.append(own_tile(part) if j == nw else part)
        s = jnp.concatenate(parts, axis=0)
        m = jnp.max(s, axis=0, keepdims=True)
        p = jnp.exp2(s - m).astype(BF16)
        acc_w[...] = _dot(vwt_ref[0, 0, :, pl.ds(k0, n_keys)], p)

    nw = WINDOW // tq
    for miss in range(nw + 1):
        pl.when(jnp.maximum(nw - i, 0) == miss)(functools.partial(window, miss))

    last = (i * tq + tq + tk - 1) // tk - 1
    lax.fori_loop(0, last // 2, pair, 0)

    @pl.when(last % 2 == 1)
    def _():
        qk(last, s_b)
        absorb(last - 1, s_a)
        absorb_last(last, s_b)

    @pl.when(last % 2 == 0)
    def _():
        absorb_last(last, s_a)

    gates_t = gt_ref[0].T
    o_t = _gated(acc[...], gates_t, g, 1, HEAD_DIM) + _gated(acc_w[...], gates_t, g, 2, HEAD_DIM)
    o_ref[0] = _unstack_heads(o_t, tq).astype(BF16)


def _sel_win_attn(qt, bias, gates, ksa, vst, kw, vwt):
    B, HD, S = qt.shape
    G, dh = N_KV_GROUPS, HEAD_DIM
    gw = HD // G
    tq = min(SEL_Q_TILE, S // 2)
    tk = 2 * tq
    assert S % tk == 0 and WINDOW % tq == 0 and WINDOW + tq <= tk
    width = HEADS_PER_GROUP * tq
    return pl.pallas_call(
        functools.partial(_sel_win_kernel, tq=tq, tk=tk),
        out_shape=jax.ShapeDtypeStruct((B, S, HD), BF16),
        grid=(B, G, S // tq),
        in_specs=[
            pl.BlockSpec((1, gw, tq), lambda b, g, i: (b, g, i)),
            pl.BlockSpec((1, 1, MAX_SEL_BLOCKS, tq), lambda b, g, i: (b, g, 0, i)),
            pl.BlockSpec((1, tq, LANES), lambda b, g, i: (b, i, 0)),
            pl.BlockSpec((1, 1, S, KEY_AUG), lambda b, g, i: (b, g, 0, 0)),
            pl.BlockSpec((1, 1, V_ROWS, S), lambda b, g, i: (b, g, 0, 0)),
            pl.BlockSpec((1, 1, S, dh), lambda b, g, i: (b, g, 0, 0)),
            pl.BlockSpec((1, 1, V_ROWS, S), lambda b, g, i: (b, g, 0, 0)),
        ],
        out_specs=pl.BlockSpec((1, tq, gw), lambda b, g, i: (b, i, g)),
        scratch_shapes=[pltpu.VMEM((KEY_AUG, width), BF16), pltpu.VMEM((1, width), F32),
                        pltpu.VMEM((V_ROWS, width), F32), pltpu.VMEM((V_ROWS, width), F32),
                        pltpu.VMEM((tk, width), F32), pltpu.VMEM((tk, width), F32)],
        compiler_params=_cparams(("parallel", "parallel", "arbitrary"), 60),
        name="nsa_selected_window",
    )(qt, bias, gates, ksa, vst, kw, vwt)


def _oproj_kernel(h_ref, a_ref, b_ref, w_ref, g_ref, wh_ref, wl_ref, br_ref, o_ref, info_ref, cnt_ref, *, tt, rt):
    @pl.when(pl.program_id(0) == 0)
    def _():
        cnt_ref[...] = jnp.zeros_like(cnt_ref)

    o = a_ref[...].astype(F32) + b_ref[...].astype(F32)
    h_new = h_ref[...] + _dot(o.astype(BF16), w_ref[...])
    o_ref[...] = h_new
    cnt = cnt_ref[...]
    for k in range(tt // rt):
        rows = slice(k * rt, (k + 1) * rt)
        info_ref[rows, :], cnt = _route(h_new[rows], g_ref, wh_ref, wl_ref, br_ref, cnt)
    cnt_ref[...] = cnt


def _out_proj(h2, oc, osw, w_o, router_ops):
    T, D = h2.shape
    HD = oc.shape[-1]
    tt = min(SEQ_TILE, T)
    rt = min(TOK_TILE, tt)
    blk = lambda w: pl.BlockSpec((tt, w), lambda i: (i, 0))
    const = lambda i: (0, 0)
    return pl.pallas_call(
        functools.partial(_oproj_kernel, tt=tt, rt=rt),
        out_shape=(jax.ShapeDtypeStruct((T, D), F32), jax.ShapeDtypeStruct((T, LANES), F32),
                   jax.ShapeDtypeStruct((1, LANES), F32)),
        grid=(T // tt,),
        in_specs=[blk(D), blk(HD), blk(HD), pl.BlockSpec((HD, D), const)] + _router_specs(D, const),
        out_specs=(blk(D), blk(LANES), pl.BlockSpec((1, LANES), const)),
        compiler_params=_cparams(("arbitrary",), 48),
        name="nsa_out_proj_router",
    )(h2, oc, osw, w_o.astype(BF16), *router_ops)


def _nsa_layer(h, g, w_qg, b_gate, w_o, shared, router_ops):
    B, S, D = h.shape
    kc, vct, ksa, vst, kw, vwt = shared
    qt, gates = _q_proj(h, g, w_qg, b_gate)
    oc, bias = _cmp_attn(qt, gates, kc, vct)
    osw = _sel_win_attn(qt, bias, gates, ksa, vst, kw, vwt)
    flat = lambda a: a.reshape(B * S, a.shape[-1])
    h_new, info, cnt = _out_proj(flat(h), flat(oc), flat(osw), w_o, router_ops)
    return h_new, (info, cnt)


def _shared_kv(h, kv_norm, w_kv, ck, cv):
    B, S, _ = h.shape
    G, dh = N_KV_GROUPS, HEAD_DIM
    assert S % SEL_BLOCK == 0 and N_SELECT <= S // SEL_BLOCK <= MAX_SEL_BLOCKS
    cvals, ksa, vst, kw, vwt = _shared_kv_proj(h, kv_norm, w_kv)
    halves = cvals.reshape(B, S, 2, G, dh).transpose(2, 0, 3, 1, 4).reshape(2, B, G, S // CMP_STRIDE, CMP_STRIDE * dh)
    kc = _compress(halves, 0, *ck, keys=True)
    vct = _compress(halves, 1, *cv, keys=False)
    return kc, vct, ksa, vst, kw, vwt


def kernel(x, p, norm_mix, norm_ffn, norm_ple, pool_w, pool_b, pool_scale, kv_norm, w_kv, cmp_k_pos, cmp_k_w1, cmp_k_b1, cmp_k_w2, cmp_k_b2, cmp_v_pos, cmp_v_w1, cmp_v_b1, cmp_v_w2, cmp_v_b2, w_qg, b_gate, w_o, router_g_w, router_g_b, router_e_w, router_e_b, moe_w1, moe_w3, moe_w2, ple_proj, ple_gate_w, ple_gate_b, final_norm):
    B, S, D = x.shape
    depth = p.shape[0]
    n_a = pool_w.shape[0]
    T = B * S
    h = x
    shared = None
    for i in range(depth):
        if i == n_a:
            shared = _shared_kv(h, kv_norm, w_kv,
                                (cmp_k_pos, cmp_k_w1, cmp_k_b1, cmp_k_w2, cmp_k_b2),
                                (cmp_v_pos, cmp_v_w1, cmp_v_b1, cmp_v_w2, cmp_v_b2))
        router_ops = _router_operands(norm_ffn[i], router_g_w[i], router_g_b[i], router_e_w[i], router_e_b[i])
        if i < n_a:
            h2, routing = _pool_layer(h, norm_mix[i], pool_w[i], pool_b[i], pool_scale[i], router_ops)
            h2 = h2.reshape(T, D)
        else:
            j = i - n_a
            h2, routing = _nsa_layer(h, norm_mix[i], w_qg[j], b_gate[j], w_o[j], shared, router_ops)
        h = _moe_ple_layer(h2, routing, p.reshape(depth, T, p.shape[-1]), i, norm_ffn[i], moe_w1, moe_w3, moe_w2,
                           norm_ple[i], ple_gate_w[i], ple_gate_b[i], ple_proj[i], final_norm,
                           i == depth - 1).reshape(B, S, D)
    return h
```
